```python
import math
import jax, jax.numpy as jnp
from jax import lax
import numpy as np

D_MODEL = 1024
BATCH = 32
SEQ = 256
DEPTH = 4
DEC_BATCH = 2
DEC_SEQ = 4096
PAST_LEN = 512

GRID_W = 64
N_MIXERS = 2
N_GMLP_LAYERS = DEPTH - DEPTH // N_MIXERS
N_MLA_LAYERS = DEPTH // N_MIXERS
RMS_EPS = 1e-6

GMLP_WIDTH = 2 * D_MODEL
GMLP_GROUPS = 8
CHUNK = 128

N_HEADS = 16
QK_NOPE_DIM = 64
QK_ROPE_DIM = 32
QK_HEAD_DIM = QK_NOPE_DIM + QK_ROPE_DIM
V_HEAD_DIM = 64
Q_RANK = 256
KV_RANK = 128
ROPE_THETA = 10000.0
Q_BLOCK = 128

N_EXPERTS = 32
TOP_K = 4
D_FF = D_MODEL
SWIGLU_LIMIT = 7.0
SWIGLU_ALPHA = 1.702
EXPERT_BLOCK = 128

kernel_name = 'hybrid_gmlp_mla_moe_prefix_dit_step'


def rms_norm(x, g):
    xf = x.astype(jnp.float32)
    y = xf * lax.rsqrt(jnp.mean(xf * xf, axis=-1, keepdims=True) + RMS_EPS)
    return (y * g.astype(jnp.float32)).astype(x.dtype)


def adaln_params(cond, w, b):
    m = jax.nn.silu(cond) @ w + b
    return tuple(t[:, None, :] for t in jnp.split(m, 6, axis=-1))


def modulate(h, shift, scale):
    return h * (1 + scale) + shift


def grid_rope_tables(n_tokens):
    rows = n_tokens // GRID_W
    row_id = jnp.repeat(jnp.arange(rows), GRID_W).astype(jnp.float32)
    col_id = jnp.tile(jnp.arange(GRID_W), rows).astype(jnp.float32)
    axis_dim = QK_ROPE_DIM // 2
    inv_freq = ROPE_THETA ** (-jnp.arange(0, axis_dim, 2, dtype=jnp.float32) / axis_dim)
    ang = jnp.stack([row_id[:, None] * inv_freq, col_id[:, None] * inv_freq], axis=1)
    return jnp.cos(ang), jnp.sin(ang)


def rope_2d(x, cos, sin):
    b, t, h, r = x.shape
    xa = x.reshape(b, t, h, 2, r // 2)
    x1, x2 = xa[..., : r // 4], xa[..., r // 4:]
    c = cos[None, :, None].astype(x.dtype)
    s = sin[None, :, None].astype(x.dtype)
    out = jnp.concatenate([x1 * c - x2 * s, x2 * c + x1 * s], axis=-1)
    return out.reshape(b, t, h, r)


def rope_part(x, cos, sin):
    return jnp.concatenate([x[..., :QK_NOPE_DIM], rope_2d(x[..., QK_NOPE_DIM:], cos, sin)], axis=-1)


def gmlp_mix(h, w_in, b_in, g_v, w_s, b_s, w_out):
    b, t, _ = h.shape
    z = jax.nn.gelu(h @ w_in + b_in)
    u, v = z[..., :GMLP_WIDTH], z[..., GMLP_WIDTH:]
    v = rms_norm(v, g_v)
    vb = v.reshape(b, t // CHUNK, CHUNK, GMLP_GROUPS, GMLP_WIDTH // GMLP_GROUPS)
    vm = jnp.einsum('gpq,bnqgc->bnpgc', w_s, vb) + b_s.T[:, :, None]
    return (u * vm.reshape(b, t, GMLP_WIDTH)) @ w_out


def mla_project(h, w_dq, norm_q, w_uq, w_dkv, norm_kv):
    b, t, _ = h.shape
    cq = rms_norm(h @ w_dq, norm_q)
    q = (cq @ w_uq).reshape(b, t, N_HEADS, QK_HEAD_DIM)
    kv_a = h @ w_dkv
    c_kv = rms_norm(kv_a[..., :KV_RANK], norm_kv)
    k_rope = kv_a[..., KV_RANK:]
    return q, c_kv, k_rope


def mla_expand(c_kv, k_rope, w_ukv, qk_norm_k):
    b, t, _ = c_kv.shape
    kv = (c_kv @ w_ukv).reshape(b, t, N_HEADS, QK_NOPE_DIM + V_HEAD_DIM)
    k_nope, v = kv[..., :QK_NOPE_DIM], kv[..., QK_NOPE_DIM:]
    kr = jnp.broadcast_to(k_rope[:, :, None, :], (b, t, N_HEADS, QK_ROPE_DIM))
    k = rms_norm(jnp.concatenate([k_nope, kr], axis=-1), qk_norm_k)
    return k, v


def block_attention(q, k, v):
    b, tq, h, dk = q.shape
    nb = tq // Q_BLOCK
    scale = 1.0 / math.sqrt(dk)
    qb = q.reshape(b, nb, Q_BLOCK, h, dk).transpose(1, 0, 2, 3, 4)

    def one_block(qblk):
        s = jnp.einsum('bqhd,bkhd->bhqk', qblk, k, preferred_element_type=jnp.float32) * scale
        p = jax.nn.softmax(s, axis=-1).astype(v.dtype)
        return jnp.einsum('bhqk,bkhd->bqhd', p, v)

    o = lax.map(one_block, qb)
    return o.transpose(1, 0, 2, 3, 4).reshape(b, tq, h * v.shape[-1])


def mla_context(h, w_dq, norm_q, w_uq, w_dkv, norm_kv, w_ukv, qk_norm_q, qk_norm_k, w_o):
    q, c_kv, k_rope = mla_project(h, w_dq, norm_q, w_uq, w_dkv, norm_kv)
    q = rms_norm(q, qk_norm_q)
    k, v = mla_expand(c_kv, k_rope, w_ukv, qk_norm_k)
    return block_attention(q, k, v) @ w_o, c_kv, k_rope


def mla_latent(h, ckv_ctx, krope_ctx, cos, sin, w_dq, norm_q, w_uq, w_dkv, norm_kv, w_ukv,
               qk_norm_q, qk_norm_k, w_o):
    q, c_kv, k_rope = mla_project(h, w_dq, norm_q, w_uq, w_dkv, norm_kv)
    q = rope_part(rms_norm(q, qk_norm_q), cos, sin)
    k_lat, v_lat = mla_expand(c_kv, k_rope, w_ukv, qk_norm_k)
    k_lat = rope_part(k_lat, cos, sin)
    k_ctx, v_ctx = mla_expand(ckv_ctx, krope_ctx, w_ukv, qk_norm_k)
    k = jnp.concatenate([k_ctx, k_lat], axis=1)
    v = jnp.concatenate([v_ctx, v_lat], axis=1)
    return block_attention(q, k, v) @ w_o


def moe_ffn(h, w_router, b_router, w_gu, b_gu, w_down, b_down):
    b, t, d = h.shape
    xt = h.reshape(-1, d)
    n_assign = xt.shape[0] * TOP_K
    logits = jnp.dot(xt, w_router, preferred_element_type=jnp.float32) + b_router.astype(jnp.float32)
    top_val, top_idx = lax.top_k(logits, TOP_K)
    gates = jax.nn.softmax(top_val, axis=-1)
    flat_e = top_idx.reshape(-1)
    order = jnp.argsort(flat_e)
    e_sorted = flat_e[order]
    tok_sorted = order // TOP_K
    gate_sorted = gates.reshape(-1)[order]
    counts = jnp.bincount(flat_e, length=N_EXPERTS)
    padded = (counts + EXPERT_BLOCK - 1) // EXPERT_BLOCK * EXPERT_BLOCK
    pad_end = jnp.cumsum(padded)
    pad_start = pad_end - padded
    start = jnp.cumsum(counts) - counts
    dest = pad_start[e_sorted] + jnp.arange(n_assign) - start[e_sorted]
    n_blocks = -(-n_assign // EXPERT_BLOCK) + N_EXPERTS
    buf = jnp.zeros((n_blocks * EXPERT_BLOCK, d), h.dtype).at[dest].set(xt[tok_sorted])
    block_e = jnp.minimum(
        jnp.searchsorted(pad_end, jnp.arange(n_blocks) * EXPERT_BLOCK, side='right'), N_EXPERTS - 1)

    def expert_block(args):
        xb, e = args
        gu = xb @ w_gu[e] + b_gu[e]
        glu, lin = gu[:, 0::2], gu[:, 1::2]
        glu = jnp.minimum(glu, SWIGLU_LIMIT)
        lin = jnp.clip(lin, -SWIGLU_LIMIT, SWIGLU_LIMIT)
        act = glu * jax.nn.sigmoid(SWIGLU_ALPHA * glu) * (lin + 1)
        return act @ w_down[e] + b_down[e]

    yb = lax.map(expert_block, (buf.reshape(n_blocks, EXPERT_BLOCK, d), block_e))
    y_sorted = yb.reshape(-1, d)[dest] * gate_sorted[:, None].astype(h.dtype)
    y = jnp.zeros_like(xt).at[tok_sorted].add(y_sorted)
    return y.reshape(b, t, d)


def setup_inputs(seed: int = 0) -> dict:
    key = jax.random.key(seed)
    keys = iter(jax.random.split(key, 40))

    def nrm(shape, scale):
        return jax.random.normal(next(keys), shape, jnp.float32) * scale

    def gain(shape):
        return 1.0 + 0.05 * jax.random.normal(next(keys), shape, jnp.float32)

    d = D_MODEL
    na, nb = N_GMLP_LAYERS, N_MLA_LAYERS
    return {
        'x_prompt': nrm((BATCH, SEQ, d), 1.0),
        'x_sample': nrm((DEC_BATCH, DEC_SEQ, d), 1.0),
        'c': nrm((DEC_BATCH, d), 1.0),
        'cache_ckv': nrm((DEC_BATCH, nb, PAST_LEN, KV_RANK), 1.0),
        'cache_krope': nrm((DEC_BATCH, nb, PAST_LEN, QK_ROPE_DIM), 1.0),
        'c_ctx': nrm((d,), 1.0),
        'norm_mix_g': gain((DEPTH, d)),
        'norm_ffn_g': gain((DEPTH, d)),
        'w_mod': nrm((DEPTH, d, 6 * d), 0.5 * d ** -0.5),
        'b_mod': nrm((DEPTH, 6 * d), 0.02),
        'g_w_in': nrm((na, d, 2 * GMLP_WIDTH), d ** -0.5),
        'g_b_in': nrm((na, 2 * GMLP_WIDTH), 0.02),
        'g_norm_v': gain((na, GMLP_WIDTH)),
        'g_w_s': nrm((na, GMLP_GROUPS, CHUNK, CHUNK), CHUNK ** -0.5),
        'g_b_s': 1.0 + nrm((na, GMLP_GROUPS, CHUNK), 0.02),
        'g_w_out': nrm((na, GMLP_WIDTH, d), GMLP_WIDTH ** -0.5),
        'm_w_dq': nrm((nb, d, Q_RANK), d ** -0.5),
        'm_norm_q': gain((nb, Q_RANK)),
        'm_w_uq': nrm((nb, Q_RANK, N_HEADS * QK_HEAD_DIM), Q_RANK ** -0.5),
        'm_w_dkv': nrm((nb, d, KV_RANK + QK_ROPE_DIM), d ** -0.5),
        'm_norm_kv': gain((nb, KV_RANK)),
        'm_w_ukv': nrm((nb, KV_RANK, N_HEADS * (QK_NOPE_DIM + V_HEAD_DIM)), KV_RANK ** -0.5),
        'm_qk_norm_q': gain((nb, QK_HEAD_DIM)),
        'm_qk_norm_k': gain((nb, QK_HEAD_DIM)),
        'm_w_o': nrm((nb, N_HEADS * V_HEAD_DIM, d), (N_HEADS * V_HEAD_DIM) ** -0.5),
        'e_w_router': nrm((DEPTH, d, N_EXPERTS), d ** -0.5),
        'e_b_router': nrm((DEPTH, N_EXPERTS), 0.01),
        'e_w_gu': nrm((DEPTH, N_EXPERTS, d, 2 * D_FF), d ** -0.5),
        'e_b_gu': nrm((DEPTH, N_EXPERTS, 2 * D_FF), 0.02),
        'e_w_down': nrm((DEPTH, N_EXPERTS, D_FF, d), D_FF ** -0.5),
        'e_b_down': nrm((DEPTH, N_EXPERTS, d), 0.02),
    }


def reference(x_prompt, x_sample, c, cache_ckv, cache_krope, c_ctx, norm_mix_g, norm_ffn_g,
              w_mod, b_mod, g_w_in, g_b_in, g_norm_v, g_w_s, g_b_s, g_w_out,
              m_w_dq, m_norm_q, m_w_uq, m_w_dkv, m_norm_kv, m_w_ukv, m_qk_norm_q, m_qk_norm_k, m_w_o,
              e_w_router, e_b_router, e_w_gu, e_b_gu, e_w_down, e_b_down):
    cos, sin = grid_rope_tables(x_sample.shape[1])
    xp, xs = x_prompt, x_sample
    ckv_list, krope_list = [], []
    for layer in range(DEPTH):
        j = layer // N_MIXERS
        mp = adaln_params(c_ctx[None, :], w_mod[layer], b_mod[layer])
        ms = adaln_params(c, w_mod[layer], b_mod[layer])
        hp = modulate(rms_norm(xp, norm_mix_g[layer]), mp[0], mp[1])
        hs = modulate(rms_norm(xs, norm_mix_g[layer]), ms[0], ms[1])
        if layer % N_MIXERS == 0:
            ga = (g_w_in[j], g_b_in[j], g_norm_v[j], g_w_s[j], g_b_s[j], g_w_out[j])
            dp = gmlp_mix(hp, *ga)
            ds = gmlp_mix(hs, *ga)
        else:
            ma = (m_w_dq[j], m_norm_q[j], m_w_uq[j], m_w_dkv[j], m_norm_kv[j], m_w_ukv[j],
                  m_qk_norm_q[j], m_qk_norm_k[j], m_w_o[j])
            dp, ckv, krope = mla_context(hp, *ma)
            ds = mla_latent(hs, cache_ckv[:, j], cache_krope[:, j], cos, sin, *ma)
            ckv_list.append(ckv)
            krope_list.append(krope)
        xp = xp + mp[2] * dp
        xs = xs + ms[2] * ds
        ea = (e_w_router[layer], e_b_router[layer], e_w_gu[layer], e_b_gu[layer],
              e_w_down[layer], e_b_down[layer])
        xp = xp + mp[5] * moe_ffn(modulate(rms_norm(xp, norm_ffn_g[layer]), mp[3], mp[4]), *ea)
        xs = xs + ms[5] * moe_ffn(modulate(rms_norm(xs, norm_ffn_g[layer]), ms[3], ms[4]), *ea)
    state_ckv = jnp.stack(ckv_list, axis=1)
    state_krope = jnp.stack(krope_list, axis=1)
    return (xp, xs, state_ckv, state_krope)
```

```python
import functools
import math

import jax
import jax.numpy as jnp
from jax import lax
from jax.experimental import pallas as pl
from jax.experimental.pallas import tpu as pltpu

F32 = jnp.float32
BF16 = jnp.bfloat16

D_MODEL = 1024
BATCH = 32
SEQ = 256
DEPTH = 4
DEC_BATCH = 2
DEC_SEQ = 4096
PAST_LEN = 512
GRID_W = 64
RMS_EPS = 1e-6
GMLP_WIDTH = 2 * D_MODEL
GMLP_GROUPS = 8
GROUP_W = GMLP_WIDTH // GMLP_GROUPS
CHUNK = 128
N_HEADS = 16
QK_NOPE_DIM = 64
QK_ROPE_DIM = 32
QK_HEAD_DIM = QK_NOPE_DIM + QK_ROPE_DIM
V_HEAD_DIM = 64
Q_RANK = 256
KV_RANK = 128
ROPE_THETA = 10000.0
N_EXPERTS = 32
TOP_K = 4
D_FF = D_MODEL
SWIGLU_LIMIT = 7.0
SWIGLU_ALPHA = 1.702

N_PROMPT = BATCH * SEQ
N_SAMPLE = DEC_BATCH * DEC_SEQ
N_TOK = N_PROMPT + N_SAMPLE
N_COND = 1 + DEC_BATCH
KV_LEN = PAST_LEN + DEC_SEQ
N_KV_ROWS = DEC_BATCH * KV_LEN + N_PROMPT

LANE = 128
SUBLANE = 8
HEAD_PAD = LANE
QKV_W = N_HEADS * HEAD_PAD
VMEM_LIMIT = 56 * 1024 * 1024

TB = 256
TQ = 256
TKC = 512
TM = 256
N_TILES = N_TOK * TOP_K // TM + N_EXPERTS
ROPE_BLOCKS = DEC_SEQ // TB
NEG_BIG = -1e30


def _cparams(sem):
    return pltpu.CompilerParams(dimension_semantics=sem, vmem_limit_bytes=VMEM_LIMIT)


def _cond_of_block(i, tb):
    n_p = N_PROMPT // tb
    per = DEC_SEQ // tb
    return jnp.where(i < n_p, 0, 1 + (i - n_p) // per)


def _rms(x, g, n=None):
    n = x.shape[-1] if n is None else n
    ss = jnp.sum(x * x, axis=-1, keepdims=True) * (1.0 / n)
    return x * lax.rsqrt(ss + RMS_EPS) * g


def _dot(a, b):
    return jnp.dot(a, b, preferred_element_type=F32)


def _mod_kernel(c_ref, w_ref, b_ref, o_ref):
    c = c_ref[...]
    s = c * jax.nn.sigmoid(c)
    o_ref[0] = _dot(s.astype(BF16), w_ref[0].astype(BF16)) + b_ref[0]


def _modulation(cond, w_mod, b_mod):
    tn = 1536
    out = pl.pallas_call(
        _mod_kernel,
        grid=(DEPTH, 6 * D_MODEL // tn),
        in_specs=[
            pl.BlockSpec((SUBLANE, D_MODEL), lambda l, j: (0, 0)),
            pl.BlockSpec((1, D_MODEL, tn), lambda l, j: (l, 0, j)),
            pl.BlockSpec((1, 1, tn), lambda l, j: (l, 0, j)),
        ],
        out_specs=pl.BlockSpec((1, SUBLANE, tn), lambda l, j: (l, 0, j)),
        out_shape=jax.ShapeDtypeStruct((DEPTH, SUBLANE, 6 * D_MODEL), F32),
        compiler_params=_cparams(("parallel", "parallel")),
        name="adaln_mod",
    )(cond, w_mod, b_mod.reshape(DEPTH, 1, 6 * D_MODEL))
    m = out[:, :N_COND].reshape(DEPTH, N_COND, 6, D_MODEL)
    return jnp.pad(m, ((0, 0), (0, 0), (0, SUBLANE - 6), (0, 0)))


def _mod_spec(tb):
    return pl.BlockSpec((1, SUBLANE, D_MODEL), lambda i: (_cond_of_block(i, tb), 0, 0))


def _const_spec(shape):
    nd = len(shape)
    return pl.BlockSpec(shape, lambda i: (0,) * nd)


def _gmlp_kernel(x_ref, mod_ref, g_ref, win_ref, bin_ref, gv_ref, ws_ref, bst_ref, wout_ref, o_ref):
    x = x_ref[...]
    m = mod_ref[0]
    h = _rms(x, g_ref[...]) * (1.0 + m[1:2]) + m[0:1]
    hb = h.astype(BF16)
    zv = jax.nn.gelu(_dot(hb, win_ref[:, GMLP_WIDTH:]) + bin_ref[:, GMLP_WIDTH:], approximate=True)
    vn = _rms(zv, gv_ref[...]).astype(BF16)
    rows = []
    for c in range(TB // CHUNK):
        cols = []
        for g in range(GMLP_GROUPS):
            blk = vn[c * CHUNK:(c + 1) * CHUNK, g * GROUP_W:(g + 1) * GROUP_W]
            cols.append(_dot(ws_ref[g], blk) + bst_ref[:, g:g + 1])
        rows.append(jnp.concatenate(cols, axis=1))
    vm = jnp.concatenate(rows, axis=0)
    u = jax.nn.gelu(_dot(hb, win_ref[:, :GMLP_WIDTH]) + bin_ref[:, :GMLP_WIDTH], approximate=True)
    d = _dot((u * vm).astype(BF16), wout_ref[...])
    o_ref[...] = x + m[2:3] * d


def _gmlp_layer(x, mod, g, w_in, b_in, g_v, w_s, b_s, w_out):
    return pl.pallas_call(
        _gmlp_kernel,
        grid=(N_TOK // TB,),
        in_specs=[
            pl.BlockSpec((TB, D_MODEL), lambda i: (i, 0)),
            _mod_spec(TB),
            _const_spec((1, D_MODEL)),
            _const_spec((D_MODEL, 2 * GMLP_WIDTH)),
            _const_spec((1, 2 * GMLP_WIDTH)),
            _const_spec((1, GMLP_WIDTH)),
            _const_spec((GMLP_GROUPS, CHUNK, CHUNK)),
            _const_spec((CHUNK, GMLP_GROUPS)),
            _const_spec((GMLP_WIDTH, D_MODEL)),
        ],
        out_specs=pl.BlockSpec((TB, D_MODEL), lambda i: (i, 0)),
        out_shape=jax.ShapeDtypeStruct((N_TOK, D_MODEL), F32),
        compiler_params=_cparams(("parallel",)),
        name="gmlp_mixer",
    )(x, mod, g.reshape(1, -1), w_in.astype(BF16), b_in.reshape(1, -1), g_v.reshape(1, -1),
      w_s.astype(BF16), b_s.T, w_out.astype(BF16))


def _rope_tables():
    t = jnp.arange(DEC_SEQ)
    row_id = (t // GRID_W).astype(F32)
    col_id = (t % GRID_W).astype(F32)
    axis_dim = QK_ROPE_DIM // 2
    inv_freq = ROPE_THETA ** (-jnp.arange(0, axis_dim, 2, dtype=F32) / axis_dim)
    ang = jnp.stack([row_id[:, None] * inv_freq, col_id[:, None] * inv_freq], axis=1)
    cos, sin = jnp.cos(ang), jnp.sin(ang)
    zeros = jnp.zeros_like(sin)
    cos_l = jnp.concatenate([cos, cos], axis=-1).reshape(DEC_SEQ, QK_ROPE_DIM)
    s1_l = jnp.concatenate([-sin, zeros], axis=-1).reshape(DEC_SEQ, QK_ROPE_DIM)
    s2_l = jnp.concatenate([zeros, sin], axis=-1).reshape(DEC_SEQ, QK_ROPE_DIM)

    def widen(rope_part, nope_fill):
        left = jnp.full((DEC_SEQ, QK_NOPE_DIM), nope_fill, F32)
        right = jnp.zeros((DEC_SEQ, HEAD_PAD - QK_HEAD_DIM), F32)
        return jnp.concatenate([left, rope_part, right], axis=-1)

    pos = jnp.stack([widen(cos_l, 1.0), widen(s1_l, 0.0), widen(s2_l, 0.0)])
    ident_c = jnp.concatenate([jnp.ones((TB, QK_HEAD_DIM), F32),
                               jnp.zeros((TB, HEAD_PAD - QK_HEAD_DIM), F32)], axis=-1)
    ident = jnp.stack([ident_c, jnp.zeros_like(ident_c), jnp.zeros_like(ident_c)])
    return jnp.concatenate([pos, ident], axis=1)


def _rope(xn, tab_ref):
    half = QK_ROPE_DIM // 4
    return (xn * tab_ref[0] + pltpu.roll(xn, HEAD_PAD - half, 1) * tab_ref[1]
            + pltpu.roll(xn, half, 1) * tab_ref[2])


def _mla_proj_kernel(x_ref, mod_ref, g_ref, wdq_ref, nq_ref, wuq_ref, gq_ref, wdkv_ref, nkv_ref,
                     tab_ref, q_ref, ckv_ref, krp_ref):
    x = x_ref[...]
    m = mod_ref[0]
    h = _rms(x, g_ref[...]) * (1.0 + m[1:2]) + m[0:1]
    hb = h.astype(BF16)
    cq = _rms(_dot(hb, wdq_ref[...]), nq_ref[...])
    q = _dot(cq.astype(BF16), wuq_ref[...])
    gq = gq_ref[...]
    for hd in range(N_HEADS):
        qh = _rms(q[:, hd * HEAD_PAD:(hd + 1) * HEAD_PAD], gq, QK_HEAD_DIM)
        q_ref[:, hd * HEAD_PAD:(hd + 1) * HEAD_PAD] = _rope(qh, tab_ref).astype(BF16)
    kva = _dot(hb, wdkv_ref[...])
    ckv_ref[...] = _rms(kva[:, :KV_RANK], nkv_ref[...])
    krp_ref[...] = kva[:, KV_RANK:]


def _pad_heads(w, per_head):
    k = w.shape[0]
    w = w.reshape(k, N_HEADS, per_head)
    return jnp.pad(w, ((0, 0), (0, 0), (0, HEAD_PAD - per_head))).reshape(k, QKV_W)


def _pad_gain(g):
    return jnp.pad(g, (0, HEAD_PAD - QK_HEAD_DIM)).reshape(1, HEAD_PAD)


def _mla_proj(x, mod, g, w_dq, norm_q, w_uq, gq, w_dkv, norm_kv, tab):
    wdkv = jnp.concatenate([w_dkv[:, :KV_RANK], jnp.zeros((D_MODEL, QK_NOPE_DIM), F32),
                            w_dkv[:, KV_RANK:], jnp.zeros((D_MODEL, HEAD_PAD - QK_HEAD_DIM), F32)], axis=1)
    n_pb = N_PROMPT // TB

    def tab_idx(i):
        return (0, jnp.where(i < n_pb, ROPE_BLOCKS, (i - n_pb) % ROPE_BLOCKS), 0)

    return pl.pallas_call(
        _mla_proj_kernel,
        grid=(N_TOK // TB,),
        in_specs=[
            pl.BlockSpec((TB, D_MODEL), lambda i: (i, 0)),
            _mod_spec(TB),
            _const_spec((1, D_MODEL)),
            _const_spec((D_MODEL, Q_RANK)),
            _const_spec((1, Q_RANK)),
            _const_spec((Q_RANK, QKV_W)),
            _const_spec((1, HEAD_PAD)),
            _const_spec((D_MODEL, 2 * LANE)),
            _const_spec((1, KV_RANK)),
            pl.BlockSpec((3, TB, HEAD_PAD), tab_idx),
        ],
        out_specs=[
            pl.BlockSpec((TB, QKV_W), lambda i: (i, 0)),
            pl.BlockSpec((TB, KV_RANK), lambda i: (i, 0)),
            pl.BlockSpec((TB, HEAD_PAD), lambda i: (i, 0)),
        ],
        out_shape=[
            jax.ShapeDtypeStruct((N_TOK, QKV_W), BF16),
            jax.ShapeDtypeStruct((N_TOK, KV_RANK), F32),
            jax.ShapeDtypeStruct((N_TOK, HEAD_PAD), F32),
        ],
        compiler_params=_cparams(("parallel",)),
        name="mla_proj",
    )(x, mod, g.reshape(1, -1), w_dq.astype(BF16), norm_q.reshape(1, -1),
      _pad_heads(w_uq, QK_HEAD_DIM).astype(BF16), _pad_gain(gq), wdkv.astype(BF16),
      norm_kv.reshape(1, -1), tab)


def _kv_expand_kernel(ckv_ref, krp_ref, wukv_ref, gk_ref, tab_ref, k_ref, v_ref):
    kv = _dot(ckv_ref[...].astype(BF16), wukv_ref[...])
    krp = krp_ref[...]
    gk = gk_ref[...]
    lane = lax.broadcasted_iota(jnp.int32, (TB, HEAD_PAD), 1)
    is_nope = lane < QK_NOPE_DIM
    ones_col = jnp.where(lane == 0, 1.0, 0.0)
    for hd in range(N_HEADS):
        kvh = kv[:, hd * HEAD_PAD:(hd + 1) * HEAD_PAD]
        kn = _rms(jnp.where(is_nope, kvh, krp), gk, QK_HEAD_DIM)
        k_ref[:, hd * HEAD_PAD:(hd + 1) * HEAD_PAD] = _rope(kn, tab_ref).astype(BF16)
        v_ref[:, hd * HEAD_PAD:(hd + 1) * HEAD_PAD] = jnp.where(is_nope, ones_col, kvh).astype(BF16)


def _kv_expand(ckv_all, krp_all, w_ukv, gk, tab):
    n_sb = DEC_BATCH * KV_LEN // TB
    per = KV_LEN // TB
    n_cache = PAST_LEN // TB

    def tab_idx(i):
        j = i % per
        blk = jnp.where((i >= n_sb) | (j < n_cache), ROPE_BLOCKS, j - n_cache)
        return (0, blk, 0)

    return pl.pallas_call(
        _kv_expand_kernel,
        grid=(N_KV_ROWS // TB,),
        in_specs=[
            pl.BlockSpec((TB, KV_RANK), lambda i: (i, 0)),
            pl.BlockSpec((TB, HEAD_PAD), lambda i: (i, 0)),
            _const_spec((KV_RANK, QKV_W)),
            _const_spec((1, HEAD_PAD)),
            pl.BlockSpec((3, TB, HEAD_PAD), tab_idx),
        ],
        out_specs=[
            pl.BlockSpec((TB, QKV_W), lambda i: (i, 0)),
            pl.BlockSpec((TB, QKV_W), lambda i: (i, 0)),
        ],
        out_shape=[
            jax.ShapeDtypeStruct((N_KV_ROWS, QKV_W), BF16),
            jax.ShapeDtypeStruct((N_KV_ROWS, QKV_W), BF16),
        ],
        compiler_params=_cparams(("parallel",)),
        name="mla_kv_expand",
    )(ckv_all, krp_all, w_ukv.astype(BF16), _pad_gain(gk), tab)


def _attn_kernel(q_ref, k_ref, v_ref, o_ref, *, t_k):
    c = 1.0 / math.sqrt(QK_HEAD_DIM)
    heads = []
    for hh in range(2):
        sl = slice(hh * HEAD_PAD, (hh + 1) * HEAD_PAD)
        q = q_ref[:, sl]
        m = jnp.full((TQ, 1), NEG_BIG, F32)
        acc = jnp.zeros((TQ, HEAD_PAD), F32)
        kc = min(TKC, t_k)
        for j in range(t_k // kc):
            k = k_ref[j * kc:(j + 1) * kc, sl]
            v = v_ref[j * kc:(j + 1) * kc, sl]
            s = lax.dot_general(q, k, (((1,), (1,)), ((), ())), preferred_element_type=F32)
            m_new = jnp.maximum(m, jnp.max(s, axis=-1, keepdims=True))
            alpha = jnp.exp((m - m_new) * c)
            p = jnp.exp((s - m_new) * c).astype(BF16)
            acc = acc * alpha + _dot(p, v)
            m = m_new
        heads.append(acc / acc[:, 0:1])
    lane = lax.broadcasted_iota(jnp.int32, (TQ, HEAD_PAD), 1)
    pair = jnp.where(lane < V_HEAD_DIM, pltpu.roll(heads[0], V_HEAD_DIM, 1), heads[1])
    o_ref[...] = pair.astype(BF16)


def _attention(q, k, v, *, n_batch, t_q, t_k, q_row0, kv_row0):
    nq = t_q // TQ
    q0 = q_row0 // TQ
    k0 = kv_row0 // t_k
    return pl.pallas_call(
        functools.partial(_attn_kernel, t_k=t_k),
        grid=(n_batch, N_HEADS // 2, nq),
        in_specs=[
            pl.BlockSpec((TQ, 2 * HEAD_PAD), lambda b, h, i: (q0 + b * nq + i, h)),
            pl.BlockSpec((t_k, 2 * HEAD_PAD), lambda b, h, i: (k0 + b, h)),
            pl.BlockSpec((t_k, 2 * HEAD_PAD), lambda b, h, i: (k0 + b, h)),
        ],
        out_specs=pl.BlockSpec((TQ, 2 * V_HEAD_DIM), lambda b, h, i: (b * nq + i, h)),
        out_shape=jax.ShapeDtypeStruct((n_batch * t_q, N_HEADS * V_HEAD_DIM), BF16),
        compiler_params=_cparams(("parallel", "parallel", "parallel")),
        name=f"mla_attention_tk{t_k}",
    )(q, k, v)


def _out_proj_kernel(a_ref, x_ref, mod_ref, wo_ref, o_ref):
    o_ref[...] = x_ref[...] + mod_ref[0][2:3] * _dot(a_ref[...], wo_ref[...])


def _out_proj(attn, x, mod, w_o):
    return pl.pallas_call(
        _out_proj_kernel,
        grid=(N_TOK // TB,),
        in_specs=[
            pl.BlockSpec((TB, D_MODEL), lambda i: (i, 0)),
            pl.BlockSpec((TB, D_MODEL), lambda i: (i, 0)),
            _mod_spec(TB),
            _const_spec((D_MODEL, D_MODEL)),
        ],
        out_specs=pl.BlockSpec((TB, D_MODEL), lambda i: (i, 0)),
        out_shape=jax.ShapeDtypeStruct((N_TOK, D_MODEL), F32),
        compiler_params=_cparams(("parallel",)),
        name="mla_out_proj",
    )(attn, x, mod, w_o.astype(BF16))


def _mla_layer(x, mod, g, cache_ckv, cache_krope, tab, w_dq, norm_q, w_uq, w_dkv, norm_kv, w_ukv,
               gq, gk, w_o):
    q, ckv, krp = _mla_proj(x, mod, g, w_dq, norm_q, w_uq, gq, w_dkv, norm_kv, tab)
    cache_krp = jnp.pad(cache_krope, ((0, 0), (0, 0), (QK_NOPE_DIM, HEAD_PAD - QK_HEAD_DIM)))
    ckv_s = ckv[N_PROMPT:].reshape(DEC_BATCH, DEC_SEQ, KV_RANK)
    krp_s = krp[N_PROMPT:].reshape(DEC_BATCH, DEC_SEQ, HEAD_PAD)
    ckv_all = jnp.concatenate([jnp.concatenate([cache_ckv, ckv_s], axis=1).reshape(-1, KV_RANK),
                               ckv[:N_PROMPT]], axis=0)
    krp_all = jnp.concatenate([jnp.concatenate([cache_krp, krp_s], axis=1).reshape(-1, HEAD_PAD),
                               krp[:N_PROMPT]], axis=0)
    k, v = _kv_expand(ckv_all, krp_all, w_ukv, gk, tab)
    a_p = _attention(q, k, v, n_batch=BATCH, t_q=SEQ, t_k=SEQ, q_row0=0, kv_row0=DEC_BATCH * KV_LEN)
    a_s = _attention(q, k, v, n_batch=DEC_BATCH, t_q=DEC_SEQ, t_k=KV_LEN, q_row0=N_PROMPT, kv_row0=0)
    x = _out_proj(jnp.concatenate([a_p, a_s], axis=0), x, mod, w_o)
    state_ckv = ckv[:N_PROMPT].reshape(BATCH, SEQ, KV_RANK)
    state_krope = krp[:N_PROMPT, QK_NOPE_DIM:QK_HEAD_DIM].reshape(BATCH, SEQ, QK_ROPE_DIM)
    return x, state_ckv, state_krope


def _route_kernel(x_ref, mod_ref, g_ref, wr_ref, br_ref, h_ref, sel_ref, cw_ref):
    x = x_ref[...]
    m = mod_ref[0]
    h = _rms(x, g_ref[...]) * (1.0 + m[4:5]) + m[3:4]
    h_ref[...] = h
    logits = _dot(h.astype(BF16), wr_ref[...]) + br_ref[...]
    lane = lax.broadcasted_iota(jnp.int32, logits.shape, 1).astype(F32)
    work = logits
    sel = jnp.zeros(logits.shape, F32)
    top = None
    for k in range(TOP_K):
        mk = jnp.max(work, axis=-1, keepdims=True)
        first = jnp.min(jnp.where(work == mk, lane, float(LANE)), axis=-1, keepdims=True)
        hit = lane == first
        sel = jnp.where(hit, 1.0, sel)
        work = jnp.where(hit, -jnp.inf, work)
        if k == 0:
            top = mk
    e = jnp.where(sel > 0.0, jnp.exp(logits - top), 0.0)
    cw_ref[...] = e / jnp.sum(e, axis=-1, keepdims=True)
    sel_ref[...] = sel.astype(jnp.int32)


def _route(x, mod, g, w_router, b_router):
    wr = jnp.pad(w_router, ((0, 0), (0, LANE - N_EXPERTS))).astype(BF16)
    br = jnp.pad(b_router, (0, LANE - N_EXPERTS), constant_values=NEG_BIG).reshape(1, LANE)
    return pl.pallas_call(
        _route_kernel,
        grid=(N_TOK // TB,),
        in_specs=[
            pl.BlockSpec((TB, D_MODEL), lambda i: (i, 0)),
            _mod_spec(TB),
            _const_spec((1, D_MODEL)),
            _const_spec((D_MODEL, LANE)),
            _const_spec((1, LANE)),
        ],
        out_specs=[
            pl.BlockSpec((TB, D_MODEL), lambda i: (i, 0)),
            pl.BlockSpec((TB, LANE), lambda i: (i, 0)),
            pl.BlockSpec((TB, LANE), lambda i: (i, 0)),
        ],
        out_shape=[
            jax.ShapeDtypeStruct((N_TOK, D_MODEL), F32),
            jax.ShapeDtypeStruct((N_TOK, LANE), jnp.int32),
            jax.ShapeDtypeStruct((N_TOK, LANE), F32),
        ],
        compiler_params=_cparams(("parallel",)),
        name="moe_route",
    )(x, mod, g.reshape(1, -1), wr, br)


def _deinterleave_matrix():
    src = jnp.arange(2 * LANE)[:, None]
    dst = jnp.arange(2 * LANE)[None, :]
    want = jnp.where(dst < LANE, 2 * dst, 2 * (dst - LANE) + 1)
    return (src == want).astype(BF16)


def _expert_kernel(te_ref, nu_ref, x_ref, wgu_ref, bgu_ref, wd_ref, bd_ref, perm_ref, o_ref,
                   wgu_bf, wd_bf):
    i = pl.program_id(0)
    prev = te_ref[jnp.maximum(i - 1, 0)]
    fresh = jnp.logical_or(i == 0, te_ref[i] != prev)

    @pl.when(jnp.logical_and(fresh, i < nu_ref[0]))
    def _():
        for b in range(2 * D_FF // (2 * LANE)):
            sl = slice(b * 2 * LANE, (b + 1) * 2 * LANE)
            wgu_bf[:, sl] = _dot(wgu_ref[:, sl].astype(BF16), perm_ref[...]).astype(BF16)
        wd_bf[...] = wd_ref[...].astype(BF16)

    @pl.when(i < nu_ref[0])
    def _():
        x = x_ref[...].astype(BF16)
        gu = _dot(x, wgu_bf[...]) + bgu_ref[...]
        acts = []
        for b in range(D_FF // LANE):
            glu = jnp.minimum(gu[:, b * 2 * LANE:b * 2 * LANE + LANE], SWIGLU_LIMIT)
            lin = jnp.clip(gu[:, b * 2 * LANE + LANE:(b + 1) * 2 * LANE], -SWIGLU_LIMIT, SWIGLU_LIMIT)
            acts.append((glu * jax.nn.sigmoid(SWIGLU_ALPHA * glu) * (lin + 1.0)).astype(BF16))
        act = jnp.concatenate(acts, axis=1)
        o_ref[...] = _dot(act, wd_bf[...]) + bd_ref[...]


def _experts(buf, tile_expert, n_used, layer, w_gu, b_gu, w_down, b_down):
    bgu = b_gu.reshape(N_EXPERTS, D_FF // LANE, LANE, 2).transpose(0, 1, 3, 2).reshape(N_EXPERTS, 1, 2 * D_FF)

    def row_idx(i, te, nu):
        return (jnp.minimum(i, nu[0] - 1), 0)

    def w_idx(i, te, nu):
        return (layer, te[i], 0, 0)

    def b_idx(i, te, nu):
        return (te[i], 0, 0)

    grid_spec = pltpu.PrefetchScalarGridSpec(
        num_scalar_prefetch=2,
        grid=(N_TILES,),
        in_specs=[
            pl.BlockSpec((TM, D_MODEL), row_idx),
            pl.BlockSpec((None, None, D_MODEL, 2 * D_FF), w_idx),
            pl.BlockSpec((None, 1, 2 * D_FF), b_idx),
            pl.BlockSpec((None, None, D_FF, D_MODEL), w_idx),
            pl.BlockSpec((None, 1, D_MODEL), b_idx),
            pl.BlockSpec((2 * LANE, 2 * LANE), lambda i, te, nu: (0, 0)),
        ],
        out_specs=pl.BlockSpec((TM, D_MODEL), row_idx),
        scratch_shapes=[pltpu.VMEM((D_MODEL, 2 * D_FF), BF16), pltpu.VMEM((D_FF, D_MODEL), BF16)],
    )
    return pl.pallas_call(
        _expert_kernel,
        grid_spec=grid_spec,
        out_shape=jax.ShapeDtypeStruct((N_TILES * TM, D_MODEL), F32),
        compiler_params=_cparams(("arbitrary",)),
        name="moe_experts",
    )(tile_expert, n_used, buf, w_gu, bgu, w_down, b_down.reshape(N_EXPERTS, 1, D_MODEL),
      _deinterleave_matrix())


def _combine_kernel(x_ref, mod_ref, y_ref, w_ref, o_ref):
    w = w_ref[...]
    y = y_ref[0] * w[:, 0:1]
    for k in range(1, TOP_K):
        y = y + y_ref[k] * w[:, k:k + 1]
    o_ref[...] = x_ref[...] + mod_ref[0][5:6] * y


def _combine(x, mod, y4, w4):
    return pl.pallas_call(
        _combine_kernel,
        grid=(N_TOK // TB,),
        in_specs=[
            pl.BlockSpec((TB, D_MODEL), lambda i: (i, 0)),
            _mod_spec(TB),
            pl.BlockSpec((TOP_K, TB, D_MODEL), lambda i: (0, i, 0)),
            pl.BlockSpec((TB, TOP_K), lambda i: (i, 0)),
        ],
        out_specs=pl.BlockSpec((TB, D_MODEL), lambda i: (i, 0)),
        out_shape=jax.ShapeDtypeStruct((N_TOK, D_MODEL), F32),
        compiler_params=_cparams(("parallel",)),
        name="moe_combine",
    )(x, mod, y4, w4)


def _moe_layer(x, mod, g, layer, w_router, b_router, w_gu, b_gu, w_down, b_down):
    h, sel, cw = _route(x, mod, g, w_router, b_router)
    sel = sel[:, :N_EXPERTS]
    cw = cw[:, :N_EXPERTS]
    pos = jnp.cumsum(sel, axis=0) - sel
    counts = jnp.sum(sel, axis=0)
    tiles = (counts + TM - 1) // TM
    tile_end = jnp.cumsum(tiles)
    start = (tile_end - tiles) * TM
    n_used = tile_end[-1:].astype(jnp.int32)
    tile_ids = jnp.minimum(jnp.arange(N_TILES), n_used[0] - 1)
    tile_expert = jnp.searchsorted(tile_end, tile_ids, side="right").astype(jnp.int32)
    _, e4 = lax.top_k(sel, TOP_K)
    dest4 = jnp.take_along_axis(start[None, :] + pos, e4, axis=1)
    w4 = jnp.take_along_axis(cw, e4, axis=1)
    buf = jnp.zeros((N_TILES * TM, D_MODEL), F32).at[dest4.T.reshape(-1)].set(
        jnp.tile(h, (TOP_K, 1)), unique_indices=True)
    yb = _experts(buf, tile_expert, n_used, layer, w_gu, b_gu, w_down, b_down)
    y4 = yb[dest4.T.reshape(-1)].reshape(TOP_K, N_TOK, D_MODEL)
    return _combine(x, mod, y4, w4)


def kernel(x_prompt, x_sample, c, cache_ckv, cache_krope, c_ctx, norm_mix_g, norm_ffn_g, w_mod, b_mod,
           g_w_in, g_b_in, g_norm_v, g_w_s, g_b_s, g_w_out, m_w_dq, m_norm_q, m_w_uq, m_w_dkv,
           m_norm_kv, m_w_ukv, m_qk_norm_q, m_qk_norm_k, m_w_o, e_w_router, e_b_router, e_w_gu,
           e_b_gu, e_w_down, e_b_down):
    x = jnp.concatenate([x_prompt.reshape(N_PROMPT, D_MODEL), x_sample.reshape(N_SAMPLE, D_MODEL)], axis=0)
    cond = jnp.concatenate([c_ctx[None, :], c, jnp.zeros((SUBLANE - N_COND, D_MODEL), F32)], axis=0)
    mod = _modulation(cond, w_mod, b_mod)
    tab = _rope_tables()
    ckv_states, krope_states = [], []
    for layer in range(DEPTH):
        j = layer // 2
        if layer % 2 == 0:
            x = _gmlp_layer(x, mod[layer], norm_mix_g[layer], g_w_in[j], g_b_in[j], g_norm_v[j],
                            g_w_s[j], g_b_s[j], g_w_out[j])
        else:
            x, s_ckv, s_krope = _mla_layer(
                x, mod[layer], norm_mix_g[layer], cache_ckv[:, j], cache_krope[:, j], tab,
                m_w_dq[j], m_norm_q[j], m_w_uq[j], m_w_dkv[j], m_norm_kv[j], m_w_ukv[j],
                m_qk_norm_q[j], m_qk_norm_k[j], m_w_o[j])
            ckv_states.append(s_ckv)
            krope_states.append(s_krope)
        x = _moe_layer(x, mod[layer], norm_ffn_g[layer], layer, e_w_router[layer], e_b_router[layer],
                       e_w_gu, e_b_gu[layer], e_w_down, e_b_down[layer])
    y_prompt = x[:N_PROMPT].reshape(BATCH, SEQ, D_MODEL)
    y_sample = x[N_PROMPT:].reshape(DEC_BATCH, DEC_SEQ, D_MODEL)
    return (y_prompt, y_sample, jnp.stack(ckv_states, axis=1), jnp.stack(krope_states, axis=1))
```

```python
import functools
import math

import jax
import jax.numpy as jnp
from jax import lax
from jax.experimental import pallas as pl
from jax.experimental.pallas import tpu as pltpu
from jax.experimental.pallas import tpu_sc as plsc

F32 = jnp.float32
BF16 = jnp.bfloat16

D_MODEL = 1024
BATCH = 32
SEQ = 256
DEPTH = 4
DEC_BATCH = 2
DEC_SEQ = 4096
PAST_LEN = 512
GRID_W = 64
RMS_EPS = 1e-6
GMLP_WIDTH = 2 * D_MODEL
GMLP_GROUPS = 8
GROUP_W = GMLP_WIDTH // GMLP_GROUPS
CHUNK = 128
N_HEADS = 16
QK_NOPE_DIM = 64
QK_ROPE_DIM = 32
QK_HEAD_DIM = QK_NOPE_DIM + QK_ROPE_DIM
V_HEAD_DIM = 64
Q_RANK = 256
KV_RANK = 128
ROPE_THETA = 10000.0
N_EXPERTS = 32
TOP_K = 4
D_FF = D_MODEL
SWIGLU_LIMIT = 7.0
SWIGLU_ALPHA = 1.702

N_PROMPT = BATCH * SEQ
N_SAMPLE = DEC_BATCH * DEC_SEQ
N_TOK = N_PROMPT + N_SAMPLE
N_COND = 1 + DEC_BATCH
KV_LEN = PAST_LEN + DEC_SEQ
N_KV_ROWS = DEC_BATCH * KV_LEN + N_PROMPT

LANE = 128
SUBLANE = 8
HEAD_PAD = LANE
QKV_W = N_HEADS * HEAD_PAD
VMEM_LIMIT = 56 * 1024 * 1024

TB = 256
TQ = 256
TKC = 512
TM = 256
N_TILES = N_TOK * TOP_K // TM + N_EXPERTS
N_SLOTS = N_TILES * TM
SC_CORES = 2
SC_WORKERS = SC_CORES * 16
ROPE_BLOCKS = DEC_SEQ // TB
NEG_BIG = -1e30


def _cparams(sem):
    return pltpu.CompilerParams(dimension_semantics=sem, vmem_limit_bytes=VMEM_LIMIT)


def _cond_of_block(i, tb):
    n_p = N_PROMPT // tb
    per = DEC_SEQ // tb
    return jnp.where(i < n_p, 0, 1 + (i - n_p) // per)


def _rms(x, g, n=None):
    n = x.shape[-1] if n is None else n
    ss = jnp.sum(x * x, axis=-1, keepdims=True) * (1.0 / n)
    return x * lax.rsqrt(ss + RMS_EPS) * g


def _dot(a, b):
    return jnp.dot(a, b, preferred_element_type=F32)


def _mod_kernel(c_ref, w_ref, b_ref, o_ref):
    c = c_ref[...]
    s = c * jax.nn.sigmoid(c)
    o_ref[0] = _dot(s.astype(BF16), w_ref[0].astype(BF16)) + b_ref[0]


def _modulation(cond, w_mod, b_mod):
    tn = 1536
    out = pl.pallas_call(
        _mod_kernel,
        grid=(DEPTH, 6 * D_MODEL // tn),
        in_specs=[
            pl.BlockSpec((SUBLANE, D_MODEL), lambda l, j: (0, 0)),
            pl.BlockSpec((1, D_MODEL, tn), lambda l, j: (l, 0, j)),
            pl.BlockSpec((1, 1, tn), lambda l, j: (l, 0, j)),
        ],
        out_specs=pl.BlockSpec((1, SUBLANE, tn), lambda l, j: (l, 0, j)),
        out_shape=jax.ShapeDtypeStruct((DEPTH, SUBLANE, 6 * D_MODEL), F32),
        compiler_params=_cparams(("parallel", "parallel")),
        name="adaln_mod",
    )(cond, w_mod, b_mod.reshape(DEPTH, 1, 6 * D_MODEL))
    m = out[:, :N_COND].reshape(DEPTH, N_COND, 6, D_MODEL)
    return jnp.pad(m, ((0, 0), (0, 0), (0, SUBLANE - 6), (0, 0)))


def _mod_spec(tb):
    return pl.BlockSpec((1, SUBLANE, D_MODEL), lambda i: (_cond_of_block(i, tb), 0, 0))


def _const_spec(shape):
    nd = len(shape)
    return pl.BlockSpec(shape, lambda i: (0,) * nd)


def _gmlp_kernel(x_ref, mod_ref, g_ref, win_ref, bin_ref, gv_ref, ws_ref, bst_ref, wout_ref, o_ref):
    x = x_ref[...]
    m = mod_ref[0]
    h = _rms(x, g_ref[...]) * (1.0 + m[1:2]) + m[0:1]
    hb = h.astype(BF16)
    zv = jax.nn.gelu(_dot(hb, win_ref[:, GMLP_WIDTH:]) + bin_ref[:, GMLP_WIDTH:], approximate=True)
    vn = _rms(zv, gv_ref[...]).astype(BF16)
    rows = []
    for c in range(TB // CHUNK):
        cols = []
        for g in range(GMLP_GROUPS):
            blk = vn[c * CHUNK:(c + 1) * CHUNK, g * GROUP_W:(g + 1) * GROUP_W]
            cols.append(_dot(ws_ref[g], blk) + bst_ref[:, g:g + 1])
        rows.append(jnp.concatenate(cols, axis=1))
    vm = jnp.concatenate(rows, axis=0)
    u = jax.nn.gelu(_dot(hb, win_ref[:, :GMLP_WIDTH]) + bin_ref[:, :GMLP_WIDTH], approximate=True)
    d = _dot((u * vm).astype(BF16), wout_ref[...])
    o_ref[...] = x + m[2:3] * d


def _gmlp_layer(x, mod, g, w_in, b_in, g_v, w_s, b_s, w_out):
    return pl.pallas_call(
        _gmlp_kernel,
        grid=(N_TOK // TB,),
        in_specs=[
            pl.BlockSpec((TB, D_MODEL), lambda i: (i, 0)),
            _mod_spec(TB),
            _const_spec((1, D_MODEL)),
            _const_spec((D_MODEL, 2 * GMLP_WIDTH)),
            _const_spec((1, 2 * GMLP_WIDTH)),
            _const_spec((1, GMLP_WIDTH)),
            _const_spec((GMLP_GROUPS, CHUNK, CHUNK)),
            _const_spec((CHUNK, GMLP_GROUPS)),
            _const_spec((GMLP_WIDTH, D_MODEL)),
        ],
        out_specs=pl.BlockSpec((TB, D_MODEL), lambda i: (i, 0)),
        out_shape=jax.ShapeDtypeStruct((N_TOK, D_MODEL), F32),
        compiler_params=_cparams(("parallel",)),
        name="gmlp_mixer",
    )(x, mod, g.reshape(1, -1), w_in.astype(BF16), b_in.reshape(1, -1), g_v.reshape(1, -1),
      w_s.astype(BF16), b_s.T, w_out.astype(BF16))


def _rope_tables():
    t = jnp.arange(DEC_SEQ)
    row_id = (t // GRID_W).astype(F32)
    col_id = (t % GRID_W).astype(F32)
    axis_dim = QK_ROPE_DIM // 2
    inv_freq = ROPE_THETA ** (-jnp.arange(0, axis_dim, 2, dtype=F32) / axis_dim)
    ang = jnp.stack([row_id[:, None] * inv_freq, col_id[:, None] * inv_freq], axis=1)
    cos, sin = jnp.cos(ang), jnp.sin(ang)
    zeros = jnp.zeros_like(sin)
    cos_l = jnp.concatenate([cos, cos], axis=-1).reshape(DEC_SEQ, QK_ROPE_DIM)
    s1_l = jnp.concatenate([-sin, zeros], axis=-1).reshape(DEC_SEQ, QK_ROPE_DIM)
    s2_l = jnp.concatenate([zeros, sin], axis=-1).reshape(DEC_SEQ, QK_ROPE_DIM)

    def widen(rope_part, nope_fill):
        left = jnp.full((DEC_SEQ, QK_NOPE_DIM), nope_fill, F32)
        right = jnp.zeros((DEC_SEQ, HEAD_PAD - QK_HEAD_DIM), F32)
        return jnp.concatenate([left, rope_part, right], axis=-1)

    pos = jnp.stack([widen(cos_l, 1.0), widen(s1_l, 0.0), widen(s2_l, 0.0)])
    ident_c = jnp.concatenate([jnp.ones((TB, QK_HEAD_DIM), F32),
                               jnp.zeros((TB, HEAD_PAD - QK_HEAD_DIM), F32)], axis=-1)
    ident = jnp.stack([ident_c, jnp.zeros_like(ident_c), jnp.zeros_like(ident_c)])
    return jnp.concatenate([pos, ident], axis=1)


def _rope(xn, tab_ref):
    half = QK_ROPE_DIM // 4
    return (xn * tab_ref[0] + pltpu.roll(xn, HEAD_PAD - half, 1) * tab_ref[1]
            + pltpu.roll(xn, half, 1) * tab_ref[2])


def _mla_proj_kernel(x_ref, mod_ref, g_ref, wdq_ref, nq_ref, wuq_ref, gq_ref, wdkv_ref, nkv_ref,
                     tab_ref, q_ref, ckv_ref, krp_ref):
    x = x_ref[...]
    m = mod_ref[0]
    h = _rms(x, g_ref[...]) * (1.0 + m[1:2]) + m[0:1]
    hb = h.astype(BF16)
    cq = _rms(_dot(hb, wdq_ref[...]), nq_ref[...])
    q = _dot(cq.astype(BF16), wuq_ref[...])
    gq = gq_ref[...]
    for hd in range(N_HEADS):
        qh = _rms(q[:, hd * HEAD_PAD:(hd + 1) * HEAD_PAD], gq, QK_HEAD_DIM)
        q_ref[:, hd * HEAD_PAD:(hd + 1) * HEAD_PAD] = _rope(qh, tab_ref).astype(BF16)
    kva = _dot(hb, wdkv_ref[...])
    ckv_ref[...] = _rms(kva[:, :KV_RANK], nkv_ref[...])
    krp_ref[...] = kva[:, KV_RANK:]


def _pad_heads(w, per_head):
    k = w.shape[0]
    w = w.reshape(k, N_HEADS, per_head)
    return jnp.pad(w, ((0, 0), (0, 0), (0, HEAD_PAD - per_head))).reshape(k, QKV_W)


def _pad_gain(g):
    return jnp.pad(g, (0, HEAD_PAD - QK_HEAD_DIM)).reshape(1, HEAD_PAD)


def _mla_proj(x, mod, g, w_dq, norm_q, w_uq, gq, w_dkv, norm_kv, tab):
    wdkv = jnp.concatenate([w_dkv[:, :KV_RANK], jnp.zeros((D_MODEL, QK_NOPE_DIM), F32),
                            w_dkv[:, KV_RANK:], jnp.zeros((D_MODEL, HEAD_PAD - QK_HEAD_DIM), F32)], axis=1)
    n_pb = N_PROMPT // TB

    def tab_idx(i):
        return (0, jnp.where(i < n_pb, ROPE_BLOCKS, (i - n_pb) % ROPE_BLOCKS), 0)

    return pl.pallas_call(
        _mla_proj_kernel,
        grid=(N_TOK // TB,),
        in_specs=[
            pl.BlockSpec((TB, D_MODEL), lambda i: (i, 0)),
            _mod_spec(TB),
            _const_spec((1, D_MODEL)),
            _const_spec((D_MODEL, Q_RANK)),
            _const_spec((1, Q_RANK)),
            _const_spec((Q_RANK, QKV_W)),
            _const_spec((1, HEAD_PAD)),
            _const_spec((D_MODEL, 2 * LANE)),
            _const_spec((1, KV_RANK)),
            pl.BlockSpec((3, TB, HEAD_PAD), tab_idx),
        ],
        out_specs=[
            pl.BlockSpec((TB, QKV_W), lambda i: (i, 0)),
            pl.BlockSpec((TB, KV_RANK), lambda i: (i, 0)),
            pl.BlockSpec((TB, HEAD_PAD), lambda i: (i, 0)),
        ],
        out_shape=[
            jax.ShapeDtypeStruct((N_TOK, QKV_W), BF16),
            jax.ShapeDtypeStruct((N_TOK, KV_RANK), F32),
            jax.ShapeDtypeStruct((N_TOK, HEAD_PAD), F32),
        ],
        compiler_params=_cparams(("parallel",)),
        name="mla_proj",
    )(x, mod, g.reshape(1, -1), w_dq.astype(BF16), norm_q.reshape(1, -1),
      _pad_heads(w_uq, QK_HEAD_DIM).astype(BF16), _pad_gain(gq), wdkv.astype(BF16),
      norm_kv.reshape(1, -1), tab)


def _kv_expand_kernel(ckv_ref, krp_ref, wukv_ref, gk_ref, tab_ref, k_ref, v_ref):
    kv = _dot(ckv_ref[...].astype(BF16), wukv_ref[...])
    krp = krp_ref[...]
    gk = gk_ref[...]
    lane = lax.broadcasted_iota(jnp.int32, (TB, HEAD_PAD), 1)
    is_nope = lane < QK_NOPE_DIM
    ones_col = jnp.where(lane == 0, 1.0, 0.0)
    for hd in range(N_HEADS):
        kvh = kv[:, hd * HEAD_PAD:(hd + 1) * HEAD_PAD]
        kn = _rms(jnp.where(is_nope, kvh, krp), gk, QK_HEAD_DIM)
        k_ref[:, hd * HEAD_PAD:(hd + 1) * HEAD_PAD] = _rope(kn, tab_ref).astype(BF16)
        v_ref[:, hd * HEAD_PAD:(hd + 1) * HEAD_PAD] = jnp.where(is_nope, ones_col, kvh).astype(BF16)


def _kv_expand(ckv_all, krp_all, w_ukv, gk, tab):
    n_sb = DEC_BATCH * KV_LEN // TB
    per = KV_LEN // TB
    n_cache = PAST_LEN // TB

    def tab_idx(i):
        j = i % per
        blk = jnp.where((i >= n_sb) | (j < n_cache), ROPE_BLOCKS, j - n_cache)
        return (0, blk, 0)

    return pl.pallas_call(
        _kv_expand_kernel,
        grid=(N_KV_ROWS // TB,),
        in_specs=[
            pl.BlockSpec((TB, KV_RANK), lambda i: (i, 0)),
            pl.BlockSpec((TB, HEAD_PAD), lambda i: (i, 0)),
            _const_spec((KV_RANK, QKV_W)),
            _const_spec((1, HEAD_PAD)),
            pl.BlockSpec((3, TB, HEAD_PAD), tab_idx),
        ],
        out_specs=[
            pl.BlockSpec((TB, QKV_W), lambda i: (i, 0)),
            pl.BlockSpec((TB, QKV_W), lambda i: (i, 0)),
        ],
        out_shape=[
            jax.ShapeDtypeStruct((N_KV_ROWS, QKV_W), BF16),
            jax.ShapeDtypeStruct((N_KV_ROWS, QKV_W), BF16),
        ],
        compiler_params=_cparams(("parallel",)),
        name="mla_kv_expand",
    )(ckv_all, krp_all, w_ukv.astype(BF16), _pad_gain(gk), tab)


def _attn_kernel(q_ref, k_ref, v_ref, o_ref, *, t_k):
    c = 1.0 / math.sqrt(QK_HEAD_DIM)
    heads = []
    for hh in range(2):
        sl = slice(hh * HEAD_PAD, (hh + 1) * HEAD_PAD)
        q = q_ref[:, sl]
        m = jnp.full((TQ, 1), NEG_BIG, F32)
        acc = jnp.zeros((TQ, HEAD_PAD), F32)
        kc = min(TKC, t_k)
        for j in range(t_k // kc):
            k = k_ref[j * kc:(j + 1) * kc, sl]
            v = v_ref[j * kc:(j + 1) * kc, sl]
            s = lax.dot_general(q, k, (((1,), (1,)), ((), ())), preferred_element_type=F32)
            m_new = jnp.maximum(m, jnp.max(s, axis=-1, keepdims=True))
            alpha = jnp.exp((m - m_new) * c)
            p = jnp.exp((s - m_new) * c).astype(BF16)
            acc = acc * alpha + _dot(p, v)
            m = m_new
        heads.append(acc / acc[:, 0:1])
    lane = lax.broadcasted_iota(jnp.int32, (TQ, HEAD_PAD), 1)
    pair = jnp.where(lane < V_HEAD_DIM, pltpu.roll(heads[0], V_HEAD_DIM, 1), heads[1])
    o_ref[...] = pair.astype(BF16)


def _attention(q, k, v, *, n_batch, t_q, t_k, q_row0, kv_row0):
    nq = t_q // TQ
    q0 = q_row0 // TQ
    k0 = kv_row0 // t_k
    return pl.pallas_call(
        functools.partial(_attn_kernel, t_k=t_k),
        grid=(n_batch, N_HEADS // 2, nq),
        in_specs=[
            pl.BlockSpec((TQ, 2 * HEAD_PAD), lambda b, h, i: (q0 + b * nq + i, h)),
            pl.BlockSpec((t_k, 2 * HEAD_PAD), lambda b, h, i: (k0 + b, h)),
            pl.BlockSpec((t_k, 2 * HEAD_PAD), lambda b, h, i: (k0 + b, h)),
        ],
        out_specs=pl.BlockSpec((TQ, 2 * V_HEAD_DIM), lambda b, h, i: (b * nq + i, h)),
        out_shape=jax.ShapeDtypeStruct((n_batch * t_q, N_HEADS * V_HEAD_DIM), BF16),
        compiler_params=_cparams(("parallel", "parallel", "parallel")),
        name=f"mla_attention_tk{t_k}",
    )(q, k, v)


def _out_proj_kernel(a_ref, x_ref, mod_ref, wo_ref, o_ref):
    o_ref[...] = x_ref[...] + mod_ref[0][2:3] * _dot(a_ref[...], wo_ref[...])


def _out_proj(attn, x, mod, w_o):
    return pl.pallas_call(
        _out_proj_kernel,
        grid=(N_TOK // TB,),
        in_specs=[
            pl.BlockSpec((TB, D_MODEL), lambda i: (i, 0)),
            pl.BlockSpec((TB, D_MODEL), lambda i: (i, 0)),
            _mod_spec(TB),
            _const_spec((D_MODEL, D_MODEL)),
        ],
        out_specs=pl.BlockSpec((TB, D_MODEL), lambda i: (i, 0)),
        out_shape=jax.ShapeDtypeStruct((N_TOK, D_MODEL), F32),
        compiler_params=_cparams(("parallel",)),
        name="mla_out_proj",
    )(attn, x, mod, w_o.astype(BF16))


def _mla_layer(x, mod, g, cache_ckv, cache_krope, tab, w_dq, norm_q, w_uq, w_dkv, norm_kv, w_ukv,
               gq, gk, w_o):
    q, ckv, krp = _mla_proj(x, mod, g, w_dq, norm_q, w_uq, gq, w_dkv, norm_kv, tab)
    cache_krp = jnp.pad(cache_krope, ((0, 0), (0, 0), (QK_NOPE_DIM, HEAD_PAD - QK_HEAD_DIM)))
    ckv_s = ckv[N_PROMPT:].reshape(DEC_BATCH, DEC_SEQ, KV_RANK)
    krp_s = krp[N_PROMPT:].reshape(DEC_BATCH, DEC_SEQ, HEAD_PAD)
    ckv_all = jnp.concatenate([jnp.concatenate([cache_ckv, ckv_s], axis=1).reshape(-1, KV_RANK),
                               ckv[:N_PROMPT]], axis=0)
    krp_all = jnp.concatenate([jnp.concatenate([cache_krp, krp_s], axis=1).reshape(-1, HEAD_PAD),
                               krp[:N_PROMPT]], axis=0)
    k, v = _kv_expand(ckv_all, krp_all, w_ukv, gk, tab)
    a_p = _attention(q, k, v, n_batch=BATCH, t_q=SEQ, t_k=SEQ, q_row0=0, kv_row0=DEC_BATCH * KV_LEN)
    a_s = _attention(q, k, v, n_batch=DEC_BATCH, t_q=DEC_SEQ, t_k=KV_LEN, q_row0=N_PROMPT, kv_row0=0)
    x = _out_proj(jnp.concatenate([a_p, a_s], axis=0), x, mod, w_o)
    state_ckv = ckv[:N_PROMPT].reshape(BATCH, SEQ, KV_RANK)
    state_krope = krp[:N_PROMPT, QK_NOPE_DIM:QK_HEAD_DIM].reshape(BATCH, SEQ, QK_ROPE_DIM)
    return x, state_ckv, state_krope


def _route_kernel(x_ref, mod_ref, g_ref, wr_ref, br_ref, tri_ref, h_ref, meta_ref, cnt_ref, carry):
    i = pl.program_id(0)

    @pl.when(i == 0)
    def _():
        carry[...] = jnp.zeros_like(carry)

    x = x_ref[...]
    m = mod_ref[0]
    h = _rms(x, g_ref[...]) * (1.0 + m[4:5]) + m[3:4]
    bits = pltpu.bitcast(h.astype(BF16).astype(F32), jnp.uint32)
    h_ref[...] = (bits[:, D_MODEL // 2:] & jnp.uint32(0xFFFF0000)) | (bits[:, :D_MODEL // 2] >> 16)
    logits = _dot(h.astype(BF16), wr_ref[...]) + br_ref[...]
    lane = lax.broadcasted_iota(jnp.int32, logits.shape, 1).astype(F32)
    work = logits
    sel = jnp.zeros(logits.shape, F32)
    hits, tops = [], []
    for k in range(TOP_K):
        mk = jnp.max(work, axis=-1, keepdims=True)
        first = jnp.min(jnp.where(work == mk, lane, float(LANE)), axis=-1, keepdims=True)
        hit = lane == first
        sel = jnp.where(hit, 1.0, sel)
        work = jnp.where(hit, -jnp.inf, work)
        hits.append((hit, first))
        tops.append(mk)
    es = [jnp.exp(t - tops[0]) for t in tops]
    denom = es[0] + es[1] + es[2] + es[3]
    pos = _dot(tri_ref[...], sel.astype(BF16)) + carry[0:1, :]
    carry[...] = carry[...] + jnp.sum(sel, axis=0, keepdims=True)
    cnt_ref[...] = carry[...]
    meta = jnp.zeros(logits.shape, F32)
    for k in range(TOP_K):
        hit, first = hits[k]
        pk = jnp.sum(jnp.where(hit, pos, 0.0), axis=-1, keepdims=True)
        meta = jnp.where(lane == float(k), first, meta)
        meta = jnp.where(lane == float(TOP_K + k), es[k] / denom, meta)
        meta = jnp.where(lane == float(2 * TOP_K + k), pk, meta)
    meta_ref[...] = meta


def _route(x, mod, g, w_router, b_router):
    wr = jnp.pad(w_router, ((0, 0), (0, LANE - N_EXPERTS))).astype(BF16)
    br = jnp.pad(b_router, (0, LANE - N_EXPERTS), constant_values=NEG_BIG).reshape(1, LANE)
    tri = jnp.tri(TB, TB, -1, dtype=BF16)
    return pl.pallas_call(
        _route_kernel,
        grid=(N_TOK // TB,),
        in_specs=[
            pl.BlockSpec((TB, D_MODEL), lambda i: (i, 0)),
            _mod_spec(TB),
            _const_spec((1, D_MODEL)),
            _const_spec((D_MODEL, LANE)),
            _const_spec((1, LANE)),
            _const_spec((TB, TB)),
        ],
        out_specs=[
            pl.BlockSpec((TB, D_MODEL // 2), lambda i: (i, 0)),
            pl.BlockSpec((TB, LANE), lambda i: (i, 0)),
            _const_spec((SUBLANE, LANE)),
        ],
        out_shape=[
            jax.ShapeDtypeStruct((N_TOK, D_MODEL // 2), jnp.uint32),
            jax.ShapeDtypeStruct((N_TOK, LANE), F32),
            jax.ShapeDtypeStruct((SUBLANE, LANE), F32),
        ],
        scratch_shapes=[pltpu.VMEM((SUBLANE, LANE), F32)],
        compiler_params=_cparams(("arbitrary",)),
        name="moe_route",
    )(x, mod, g.reshape(1, -1), wr, br, tri)


def _sc_gather(table, idx, ch):
    b, w = idx.shape[0], table.shape[1]
    per_w = b // SC_WORKERS
    n_ch = per_w // ch
    assert per_w * SC_WORKERS == b and n_ch * ch == per_w and n_ch % 2 == 0
    mesh = plsc.VectorSubcoreMesh(core_axis_name="c", subcore_axis_name="s")

    @functools.partial(
        pl.kernel, mesh=mesh,
        out_type=jax.ShapeDtypeStruct((b, w), table.dtype),
        scratch_types=[
            pltpu.VMEM((n_ch, ch), jnp.int32),
            pltpu.VMEM((ch, w), table.dtype),
            pltpu.VMEM((ch, w), table.dtype),
            pltpu.SemaphoreType.DMA, pltpu.SemaphoreType.DMA,
            pltpu.SemaphoreType.DMA, pltpu.SemaphoreType.DMA,
        ],
        name="sc_row_gather",
    )
    def gather_rows(table_hbm, idx_hbm, out_hbm, idx_v, buf0, buf1, g0, g1, s0, s1):
        wid = lax.axis_index("s") * SC_CORES + lax.axis_index("c")
        base = wid * per_w
        pltpu.sync_copy(idx_hbm.at[wid], idx_v)

        def gather(j, buf, sem):
            return pltpu.make_async_copy(table_hbm.at[idx_v.at[j]], buf, sem)

        def store(j, buf, sem):
            return pltpu.make_async_copy(buf, out_hbm.at[pl.ds(base + j * ch, ch)], sem)

        gather(0, buf0, g0).start()

        @pl.loop(0, n_ch, step=2)
        def _(j):
            @pl.when(j > 0)
            def _():
                store(j - 1, buf1, s1).wait()

            gather(j + 1, buf1, g1).start()
            gather(j, buf0, g0).wait()
            store(j, buf0, s0).start()
            gather(j + 1, buf1, g1).wait()
            store(j + 1, buf1, s1).start()
            store(j, buf0, s0).wait()

            @pl.when(j + 2 < n_ch)
            def _():
                gather(j + 2, buf0, g0).start()

        store(n_ch - 1, buf1, s1).wait()

    return gather_rows(table, idx.reshape(SC_WORKERS, n_ch, ch))


def _deinterleave_matrix():
    src = jnp.arange(2 * LANE)[:, None]
    dst = jnp.arange(2 * LANE)[None, :]
    want = jnp.where(dst < LANE, 2 * dst, 2 * (dst - LANE) + 1)
    return (src == want).astype(BF16)


def _expert_kernel(te_ref, nu_ref, x_ref, wgu_ref, bgu_ref, wd_ref, bd_ref, perm_ref, o_ref,
                   wgu_bf, wd_bf):
    i = pl.program_id(0)
    prev = te_ref[jnp.maximum(i - 1, 0)]
    fresh = jnp.logical_or(i == 0, te_ref[i] != prev)

    @pl.when(jnp.logical_and(fresh, i < nu_ref[0]))
    def _():
        for b in range(2 * D_FF // (2 * LANE)):
            sl = slice(b * 2 * LANE, (b + 1) * 2 * LANE)
            wgu_bf[:, sl] = _dot(wgu_ref[:, sl].astype(BF16), perm_ref[...]).astype(BF16)
        wd_bf[...] = wd_ref[...].astype(BF16)

    @pl.when(i < nu_ref[0])
    def _():
        w = x_ref[...]
        x = jnp.concatenate([pltpu.bitcast(w << 16, F32), pltpu.bitcast(w & jnp.uint32(0xFFFF0000), F32)],
                            axis=1).astype(BF16)
        gu = _dot(x, wgu_bf[...]) + bgu_ref[...]
        acts = []
        for b in range(D_FF // LANE):
            glu = jnp.minimum(gu[:, b * 2 * LANE:b * 2 * LANE + LANE], SWIGLU_LIMIT)
            lin = jnp.clip(gu[:, b * 2 * LANE + LANE:(b + 1) * 2 * LANE], -SWIGLU_LIMIT, SWIGLU_LIMIT)
            acts.append((glu * jax.nn.sigmoid(SWIGLU_ALPHA * glu) * (lin + 1.0)).astype(BF16))
        act = jnp.concatenate(acts, axis=1)
        o_ref[...] = _dot(act, wd_bf[...]) + bd_ref[...]


def _experts(buf, tile_expert, n_used, layer, w_gu, b_gu, w_down, b_down):
    bgu = b_gu.reshape(N_EXPERTS, D_FF // LANE, LANE, 2).transpose(0, 1, 3, 2).reshape(N_EXPERTS, 1, 2 * D_FF)

    def row_idx(i, te, nu):
        return (jnp.minimum(i, nu[0] - 1), 0)

    def w_idx(i, te, nu):
        return (layer, te[i], 0, 0)

    def b_idx(i, te, nu):
        return (te[i], 0, 0)

    grid_spec = pltpu.PrefetchScalarGridSpec(
        num_scalar_prefetch=2,
        grid=(N_TILES,),
        in_specs=[
            pl.BlockSpec((TM, D_MODEL // 2), row_idx),
            pl.BlockSpec((None, None, D_MODEL, 2 * D_FF), w_idx),
            pl.BlockSpec((None, 1, 2 * D_FF), b_idx),
            pl.BlockSpec((None, None, D_FF, D_MODEL), w_idx),
            pl.BlockSpec((None, 1, D_MODEL), b_idx),
            pl.BlockSpec((2 * LANE, 2 * LANE), lambda i, te, nu: (0, 0)),
        ],
        out_specs=pl.BlockSpec((TM, D_MODEL), row_idx),
        scratch_shapes=[pltpu.VMEM((D_MODEL, 2 * D_FF), BF16), pltpu.VMEM((D_FF, D_MODEL), BF16)],
    )
    return pl.pallas_call(
        _expert_kernel,
        grid_spec=grid_spec,
        out_shape=jax.ShapeDtypeStruct((N_SLOTS, D_MODEL), F32),
        compiler_params=_cparams(("arbitrary",)),
        name="moe_experts",
    )(tile_expert, n_used, buf, w_gu, bgu, w_down, b_down.reshape(N_EXPERTS, 1, D_MODEL),
      _deinterleave_matrix())


def _combine_kernel(x_ref, mod_ref, y_ref, w_ref, o_ref):
    w = w_ref[...]
    y = y_ref[0] * w[:, 0:1]
    for k in range(1, TOP_K):
        y = y + y_ref[k] * w[:, k:k + 1]
    o_ref[...] = x_ref[...] + mod_ref[0][5:6] * y


def _combine(x, mod, y4, w4):
    return pl.pallas_call(
        _combine_kernel,
        grid=(N_TOK // TB,),
        in_specs=[
            pl.BlockSpec((TB, D_MODEL), lambda i: (i, 0)),
            _mod_spec(TB),
            pl.BlockSpec((TOP_K, TB, D_MODEL), lambda i: (0, i, 0)),
            pl.BlockSpec((TB, TOP_K), lambda i: (i, 0)),
        ],
        out_specs=pl.BlockSpec((TB, D_MODEL), lambda i: (i, 0)),
        out_shape=jax.ShapeDtypeStruct((N_TOK, D_MODEL), F32),
        compiler_params=_cparams(("parallel",)),
        name="moe_combine",
    )(x, mod, y4, w4)


def _moe_layer(x, mod, g, layer, w_router, b_router, w_gu, b_gu, w_down, b_down):
    hp, meta, cnt = _route(x, mod, g, w_router, b_router)
    counts = cnt[0, :N_EXPERTS].astype(jnp.int32)
    tiles = (counts + TM - 1) // TM
    tile_end = jnp.cumsum(tiles)
    start = (tile_end - tiles) * TM
    n_used = tile_end[-1:].astype(jnp.int32)
    tile_ids = jnp.minimum(jnp.arange(N_TILES), n_used[0] - 1)
    tile_expert = jnp.searchsorted(tile_end, tile_ids, side="right").astype(jnp.int32)
    e4 = meta[:, 0:TOP_K].astype(jnp.int32)
    w4 = meta[:, TOP_K:2 * TOP_K]
    dest4 = start[e4] + meta[:, 2 * TOP_K:3 * TOP_K].astype(jnp.int32)
    dest = dest4.T.reshape(-1)
    src = jnp.zeros((N_SLOTS,), jnp.int32).at[dest].set(
        jnp.tile(jnp.arange(N_TOK, dtype=jnp.int32), TOP_K), unique_indices=True)
    buf = _sc_gather(hp, src, 64)
    yb = _experts(buf, tile_expert, n_used, layer, w_gu, b_gu, w_down, b_down)
    y4 = _sc_gather(yb, dest, 32).reshape(TOP_K, N_TOK, D_MODEL)
    return _combine(x, mod, y4, w4)


def kernel(x_prompt, x_sample, c, cache_ckv, cache_krope, c_ctx, norm_mix_g, norm_ffn_g, w_mod, b_mod,
           g_w_in, g_b_in, g_norm_v, g_w_s, g_b_s, g_w_out, m_w_dq, m_norm_q, m_w_uq, m_w_dkv,
           m_norm_kv, m_w_ukv, m_qk_norm_q, m_qk_norm_k, m_w_o, e_w_router, e_b_router, e_w_gu,
           e_b_gu, e_w_down, e_b_down):
    x = jnp.concatenate([x_prompt.reshape(N_PROMPT, D_MODEL), x_sample.reshape(N_SAMPLE, D_MODEL)], axis=0)
    cond = jnp.concatenate([c_ctx[None, :], c, jnp.zeros((SUBLANE - N_COND, D_MODEL), F32)], axis=0)
    mod = _modulation(cond, w_mod, b_mod)
    tab = _rope_tables()
    ckv_states, krope_states = [], []
    for layer in range(DEPTH):
        j = layer // 2
        if layer % 2 == 0:
            x = _gmlp_layer(x, mod[layer], norm_mix_g[layer], g_w_in[j], g_b_in[j], g_norm_v[j],
                            g_w_s[j], g_b_s[j], g_w_out[j])
        else:
            x, s_ckv, s_krope = _mla_layer(
                x, mod[layer], norm_mix_g[layer], cache_ckv[:, j], cache_krope[:, j], tab,
                m_w_dq[j], m_norm_q[j], m_w_uq[j], m_w_dkv[j], m_norm_kv[j], m_w_ukv[j],
                m_qk_norm_q[j], m_qk_norm_k[j], m_w_o[j])
            ckv_states.append(s_ckv)
            krope_states.append(s_krope)
        x = _moe_layer(x, mod[layer], norm_ffn_g[layer], layer, e_w_router[layer], e_b_router[layer],
                       e_w_gu, e_b_gu[layer], e_w_down, e_b_down[layer])
    y_prompt = x[:N_PROMPT].reshape(BATCH, SEQ, D_MODEL)
    y_sample = x[N_PROMPT:].reshape(DEC_BATCH, DEC_SEQ, D_MODEL)
    return (y_prompt, y_sample, jnp.stack(ckv_states, axis=1), jnp.stack(krope_states, axis=1))
```

```python
import functools
import math

import jax
import jax.numpy as jnp
from jax import lax
from jax.experimental import pallas as pl
from jax.experimental.pallas import tpu as pltpu
from jax.experimental.pallas import tpu_sc as plsc

F32 = jnp.float32
BF16 = jnp.bfloat16

D_MODEL = 1024
BATCH = 32
SEQ = 256
DEPTH = 4
DEC_BATCH = 2
DEC_SEQ = 4096
PAST_LEN = 512
GRID_W = 64
RMS_EPS = 1e-6
GMLP_WIDTH = 2 * D_MODEL
GMLP_GROUPS = 8
GROUP_W = GMLP_WIDTH // GMLP_GROUPS
CHUNK = 128
N_HEADS = 16
QK_NOPE_DIM = 64
QK_ROPE_DIM = 32
QK_HEAD_DIM = QK_NOPE_DIM + QK_ROPE_DIM
V_HEAD_DIM = 64
Q_RANK = 256
KV_RANK = 128
ROPE_THETA = 10000.0
N_EXPERTS = 32
TOP_K = 4
D_FF = D_MODEL
SWIGLU_LIMIT = 7.0
SWIGLU_ALPHA = 1.702

N_PROMPT = BATCH * SEQ
N_SAMPLE = DEC_BATCH * DEC_SEQ
N_TOK = N_PROMPT + N_SAMPLE
N_COND = 1 + DEC_BATCH
KV_LEN = PAST_LEN + DEC_SEQ
N_KV_ROWS = DEC_BATCH * KV_LEN + N_PROMPT

LANE = 128
SUBLANE = 8
HEAD_PAD = LANE
QKV_W = N_HEADS * HEAD_PAD
VMEM_LIMIT = 56 * 1024 * 1024

TB = 256
TQ = 256
TKC = 512
TM = 512
N_TILES = N_TOK * TOP_K // TM + N_EXPERTS
N_SLOTS = N_TILES * TM
SC_CORES = 2
SC_WORKERS = SC_CORES * 16
ROPE_BLOCKS = DEC_SEQ // TB
NEG_BIG = -1e30


def _cparams(sem):
    return pltpu.CompilerParams(dimension_semantics=sem, vmem_limit_bytes=VMEM_LIMIT)


def _cond_of_block(i, tb):
    n_p = N_PROMPT // tb
    per = DEC_SEQ // tb
    return jnp.where(i < n_p, 0, 1 + (i - n_p) // per)


def _rms(x, g, n=None):
    n = x.shape[-1] if n is None else n
    ss = jnp.sum(x * x, axis=-1, keepdims=True) * (1.0 / n)
    return x * lax.rsqrt(ss + RMS_EPS) * g


def _dot(a, b):
    return jnp.dot(a, b, preferred_element_type=F32)


def _mod_kernel(c_ref, w_ref, b_ref, o_ref):
    c = c_ref[...]
    s = c * jax.nn.sigmoid(c)
    o_ref[0] = _dot(s.astype(BF16), w_ref[0].astype(BF16)) + b_ref[0]


def _modulation(cond, w_mod, b_mod):
    tn = 1536
    out = pl.pallas_call(
        _mod_kernel,
        grid=(DEPTH, 6 * D_MODEL // tn),
        in_specs=[
            pl.BlockSpec((SUBLANE, D_MODEL), lambda l, j: (0, 0)),
            pl.BlockSpec((1, D_MODEL, tn), lambda l, j: (l, 0, j)),
            pl.BlockSpec((1, 1, tn), lambda l, j: (l, 0, j)),
        ],
        out_specs=pl.BlockSpec((1, SUBLANE, tn), lambda l, j: (l, 0, j)),
        out_shape=jax.ShapeDtypeStruct((DEPTH, SUBLANE, 6 * D_MODEL), F32),
        compiler_params=_cparams(("parallel", "parallel")),
        name="adaln_mod",
    )(cond, w_mod, b_mod.reshape(DEPTH, 1, 6 * D_MODEL))
    m = out[:, :N_COND].reshape(DEPTH, N_COND, 6, D_MODEL)
    return jnp.pad(m, ((0, 0), (0, 0), (0, SUBLANE - 6), (0, 0)))


def _mod_spec(tb):
    return pl.BlockSpec((1, SUBLANE, D_MODEL), lambda i: (_cond_of_block(i, tb), 0, 0))


def _const_spec(shape):
    nd = len(shape)
    return pl.BlockSpec(shape, lambda i: (0,) * nd)


def _gmlp_kernel(x_ref, mod_ref, g_ref, win_ref, bin_ref, gv_ref, ws_ref, bst_ref, wout_ref, o_ref):
    x = x_ref[...]
    m = mod_ref[0]
    h = _rms(x, g_ref[...]) * (1.0 + m[1:2]) + m[0:1]
    hb = h.astype(BF16)
    zv = jax.nn.gelu(_dot(hb, win_ref[:, GMLP_WIDTH:]) + bin_ref[:, GMLP_WIDTH:], approximate=True)
    vn = _rms(zv, gv_ref[...]).astype(BF16)
    rows = []
    for c in range(TB // CHUNK):
        cols = []
        for g in range(GMLP_GROUPS):
            blk = vn[c * CHUNK:(c + 1) * CHUNK, g * GROUP_W:(g + 1) * GROUP_W]
            cols.append(_dot(ws_ref[g], blk) + bst_ref[:, g:g + 1])
        rows.append(jnp.concatenate(cols, axis=1))
    vm = jnp.concatenate(rows, axis=0)
    u = jax.nn.gelu(_dot(hb, win_ref[:, :GMLP_WIDTH]) + bin_ref[:, :GMLP_WIDTH], approximate=True)
    d = _dot((u * vm).astype(BF16), wout_ref[...])
    o_ref[...] = x + m[2:3] * d


def _gmlp_layer(x, mod, g, w_in, b_in, g_v, w_s, b_s, w_out):
    return pl.pallas_call(
        _gmlp_kernel,
        grid=(N_TOK // TB,),
        in_specs=[
            pl.BlockSpec((TB, D_MODEL), lambda i: (i, 0)),
            _mod_spec(TB),
            _const_spec((1, D_MODEL)),
            _const_spec((D_MODEL, 2 * GMLP_WIDTH)),
            _const_spec((1, 2 * GMLP_WIDTH)),
            _const_spec((1, GMLP_WIDTH)),
            _const_spec((GMLP_GROUPS, CHUNK, CHUNK)),
            _const_spec((CHUNK, GMLP_GROUPS)),
            _const_spec((GMLP_WIDTH, D_MODEL)),
        ],
        out_specs=pl.BlockSpec((TB, D_MODEL), lambda i: (i, 0)),
        out_shape=jax.ShapeDtypeStruct((N_TOK, D_MODEL), F32),
        compiler_params=_cparams(("parallel",)),
        name="gmlp_mixer",
    )(x, mod, g.reshape(1, -1), w_in.astype(BF16), b_in.reshape(1, -1), g_v.reshape(1, -1),
      w_s.astype(BF16), b_s.T, w_out.astype(BF16))


def _rope_tables():
    t = jnp.arange(DEC_SEQ)
    row_id = (t // GRID_W).astype(F32)
    col_id = (t % GRID_W).astype(F32)
    axis_dim = QK_ROPE_DIM // 2
    inv_freq = ROPE_THETA ** (-jnp.arange(0, axis_dim, 2, dtype=F32) / axis_dim)
    ang = jnp.stack([row_id[:, None] * inv_freq, col_id[:, None] * inv_freq], axis=1)
    cos, sin = jnp.cos(ang), jnp.sin(ang)
    zeros = jnp.zeros_like(sin)
    cos_l = jnp.concatenate([cos, cos], axis=-1).reshape(DEC_SEQ, QK_ROPE_DIM)
    s1_l = jnp.concatenate([-sin, zeros], axis=-1).reshape(DEC_SEQ, QK_ROPE_DIM)
    s2_l = jnp.concatenate([zeros, sin], axis=-1).reshape(DEC_SEQ, QK_ROPE_DIM)

    def widen(rope_part, nope_fill):
        left = jnp.full((DEC_SEQ, QK_NOPE_DIM), nope_fill, F32)
        right = jnp.zeros((DEC_SEQ, HEAD_PAD - QK_HEAD_DIM), F32)
        return jnp.concatenate([left, rope_part, right], axis=-1)

    pos = jnp.stack([widen(cos_l, 1.0), widen(s1_l, 0.0), widen(s2_l, 0.0)])
    ident_c = jnp.concatenate([jnp.ones((TB, QK_HEAD_DIM), F32),
                               jnp.zeros((TB, HEAD_PAD - QK_HEAD_DIM), F32)], axis=-1)
    ident = jnp.stack([ident_c, jnp.zeros_like(ident_c), jnp.zeros_like(ident_c)])
    return jnp.concatenate([pos, ident], axis=1)


def _rope(xn, tab_ref):
    half = QK_ROPE_DIM // 4
    return (xn * tab_ref[0] + pltpu.roll(xn, HEAD_PAD - half, 1) * tab_ref[1]
            + pltpu.roll(xn, half, 1) * tab_ref[2])


def _mla_proj_kernel(x_ref, mod_ref, g_ref, wdq_ref, nq_ref, wuq_ref, gq_ref, wdkv_ref, nkv_ref,
                     tab_ref, q_ref, ckv_ref, krp_ref):
    x = x_ref[...]
    m = mod_ref[0]
    h = _rms(x, g_ref[...]) * (1.0 + m[1:2]) + m[0:1]
    hb = h.astype(BF16)
    cq = _rms(_dot(hb, wdq_ref[...]), nq_ref[...])
    q = _dot(cq.astype(BF16), wuq_ref[...])
    gq = gq_ref[...]
    for hd in range(N_HEADS):
        qh = _rms(q[:, hd * HEAD_PAD:(hd + 1) * HEAD_PAD], gq, QK_HEAD_DIM)
        q_ref[:, hd * HEAD_PAD:(hd + 1) * HEAD_PAD] = _rope(qh, tab_ref).astype(BF16)
    kva = _dot(hb, wdkv_ref[...])
    ckv_ref[...] = _rms(kva[:, :KV_RANK], nkv_ref[...])
    krp_ref[...] = kva[:, KV_RANK:]


def _pad_heads(w, per_head):
    k = w.shape[0]
    w = w.reshape(k, N_HEADS, per_head)
    return jnp.pad(w, ((0, 0), (0, 0), (0, HEAD_PAD - per_head))).reshape(k, QKV_W)


def _pad_gain(g):
    return jnp.pad(g, (0, HEAD_PAD - QK_HEAD_DIM)).reshape(1, HEAD_PAD)


def _mla_proj(x, mod, g, w_dq, norm_q, w_uq, gq, w_dkv, norm_kv, tab):
    wdkv = jnp.concatenate([w_dkv[:, :KV_RANK], jnp.zeros((D_MODEL, QK_NOPE_DIM), F32),
                            w_dkv[:, KV_RANK:], jnp.zeros((D_MODEL, HEAD_PAD - QK_HEAD_DIM), F32)], axis=1)
    n_pb = N_PROMPT // TB

    def tab_idx(i):
        return (0, jnp.where(i < n_pb, ROPE_BLOCKS, (i - n_pb) % ROPE_BLOCKS), 0)

    return pl.pallas_call(
        _mla_proj_kernel,
        grid=(N_TOK // TB,),
        in_specs=[
            pl.BlockSpec((TB, D_MODEL), lambda i: (i, 0)),
            _mod_spec(TB),
            _const_spec((1, D_MODEL)),
            _const_spec((D_MODEL, Q_RANK)),
            _const_spec((1, Q_RANK)),
            _const_spec((Q_RANK, QKV_W)),
            _const_spec((1, HEAD_PAD)),
            _const_spec((D_MODEL, 2 * LANE)),
            _const_spec((1, KV_RANK)),
            pl.BlockSpec((3, TB, HEAD_PAD), tab_idx),
        ],
        out_specs=[
            pl.BlockSpec((TB, QKV_W), lambda i: (i, 0)),
            pl.BlockSpec((TB, KV_RANK), lambda i: (i, 0)),
            pl.BlockSpec((TB, HEAD_PAD), lambda i: (i, 0)),
        ],
        out_shape=[
            jax.ShapeDtypeStruct((N_TOK, QKV_W), BF16),
            jax.ShapeDtypeStruct((N_TOK, KV_RANK), F32),
            jax.ShapeDtypeStruct((N_TOK, HEAD_PAD), F32),
        ],
        compiler_params=_cparams(("parallel",)),
        name="mla_proj",
    )(x, mod, g.reshape(1, -1), w_dq.astype(BF16), norm_q.reshape(1, -1),
      _pad_heads(w_uq, QK_HEAD_DIM).astype(BF16), _pad_gain(gq), wdkv.astype(BF16),
      norm_kv.reshape(1, -1), tab)


def _kv_expand_kernel(ckv_ref, krp_ref, wukv_ref, gk_ref, tab_ref, k_ref, v_ref):
    kv = _dot(ckv_ref[...].astype(BF16), wukv_ref[...])
    krp = krp_ref[...]
    gk = gk_ref[...]
    lane = lax.broadcasted_iota(jnp.int32, (TB, HEAD_PAD), 1)
    is_nope = lane < QK_NOPE_DIM
    ones_col = jnp.where(lane == 0, 1.0, 0.0)
    for hd in range(N_HEADS):
        kvh = kv[:, hd * HEAD_PAD:(hd + 1) * HEAD_PAD]
        kn = _rms(jnp.where(is_nope, kvh, krp), gk, QK_HEAD_DIM)
        k_ref[:, hd * HEAD_PAD:(hd + 1) * HEAD_PAD] = _rope(kn, tab_ref).astype(BF16)
        v_ref[:, hd * HEAD_PAD:(hd + 1) * HEAD_PAD] = jnp.where(is_nope, ones_col, kvh).astype(BF16)


def _kv_expand(ckv_all, krp_all, w_ukv, gk, tab):
    n_sb = DEC_BATCH * KV_LEN // TB
    per = KV_LEN // TB
    n_cache = PAST_LEN // TB

    def tab_idx(i):
        j = i % per
        blk = jnp.where((i >= n_sb) | (j < n_cache), ROPE_BLOCKS, j - n_cache)
        return (0, blk, 0)

    return pl.pallas_call(
        _kv_expand_kernel,
        grid=(N_KV_ROWS // TB,),
        in_specs=[
            pl.BlockSpec((TB, KV_RANK), lambda i: (i, 0)),
            pl.BlockSpec((TB, HEAD_PAD), lambda i: (i, 0)),
            _const_spec((KV_RANK, QKV_W)),
            _const_spec((1, HEAD_PAD)),
            pl.BlockSpec((3, TB, HEAD_PAD), tab_idx),
        ],
        out_specs=[
            pl.BlockSpec((TB, QKV_W), lambda i: (i, 0)),
            pl.BlockSpec((TB, QKV_W), lambda i: (i, 0)),
        ],
        out_shape=[
            jax.ShapeDtypeStruct((N_KV_ROWS, QKV_W), BF16),
            jax.ShapeDtypeStruct((N_KV_ROWS, QKV_W), BF16),
        ],
        compiler_params=_cparams(("parallel",)),
        name="mla_kv_expand",
    )(ckv_all, krp_all, w_ukv.astype(BF16), _pad_gain(gk), tab)


def _attn_kernel(q_ref, k_ref, v_ref, o_ref, *, t_k):
    c = 1.0 / math.sqrt(QK_HEAD_DIM)
    heads = []
    for hh in range(2):
        sl = slice(hh * HEAD_PAD, (hh + 1) * HEAD_PAD)
        q = q_ref[:, sl]
        m = jnp.full((TQ, 1), NEG_BIG, F32)
        acc = jnp.zeros((TQ, HEAD_PAD), F32)
        kc = min(TKC, t_k)
        for j in range(t_k // kc):
            k = k_ref[j * kc:(j + 1) * kc, sl]
            v = v_ref[j * kc:(j + 1) * kc, sl]
            s = lax.dot_general(q, k, (((1,), (1,)), ((), ())), preferred_element_type=F32)
            m_new = jnp.maximum(m, jnp.max(s, axis=-1, keepdims=True))
            alpha = jnp.exp((m - m_new) * c)
            p = jnp.exp((s - m_new) * c).astype(BF16)
            acc = acc * alpha + _dot(p, v)
            m = m_new
        heads.append(acc / acc[:, 0:1])
    lane = lax.broadcasted_iota(jnp.int32, (TQ, HEAD_PAD), 1)
    pair = jnp.where(lane < V_HEAD_DIM, pltpu.roll(heads[0], V_HEAD_DIM, 1), heads[1])
    o_ref[...] = pair.astype(BF16)


def _attention(q, k, v, *, n_batch, t_q, t_k, q_row0, kv_row0):
    nq = t_q // TQ
    q0 = q_row0 // TQ
    k0 = kv_row0 // t_k
    return pl.pallas_call(
        functools.partial(_attn_kernel, t_k=t_k),
        grid=(n_batch, N_HEADS // 2, nq),
        in_specs=[
            pl.BlockSpec((TQ, 2 * HEAD_PAD), lambda b, h, i: (q0 + b * nq + i, h)),
            pl.BlockSpec((t_k, 2 * HEAD_PAD), lambda b, h, i: (k0 + b, h)),
            pl.BlockSpec((t_k, 2 * HEAD_PAD), lambda b, h, i: (k0 + b, h)),
        ],
        out_specs=pl.BlockSpec((TQ, 2 * V_HEAD_DIM), lambda b, h, i: (b * nq + i, h)),
        out_shape=jax.ShapeDtypeStruct((n_batch * t_q, N_HEADS * V_HEAD_DIM), BF16),
        compiler_params=_cparams(("parallel", "parallel", "parallel")),
        name=f"mla_attention_tk{t_k}",
    )(q, k, v)


def _out_proj_kernel(a_ref, x_ref, mod_ref, wo_ref, o_ref):
    o_ref[...] = x_ref[...] + mod_ref[0][2:3] * _dot(a_ref[...], wo_ref[...])


def _out_proj(attn, x, mod, w_o):
    return pl.pallas_call(
        _out_proj_kernel,
        grid=(N_TOK // TB,),
        in_specs=[
            pl.BlockSpec((TB, D_MODEL), lambda i: (i, 0)),
            pl.BlockSpec((TB, D_MODEL), lambda i: (i, 0)),
            _mod_spec(TB),
            _const_spec((D_MODEL, D_MODEL)),
        ],
        out_specs=pl.BlockSpec((TB, D_MODEL), lambda i: (i, 0)),
        out_shape=jax.ShapeDtypeStruct((N_TOK, D_MODEL), F32),
        compiler_params=_cparams(("parallel",)),
        name="mla_out_proj",
    )(attn, x, mod, w_o.astype(BF16))


def _mla_layer(x, mod, g, cache_ckv, cache_krope, tab, w_dq, norm_q, w_uq, w_dkv, norm_kv, w_ukv,
               gq, gk, w_o):
    q, ckv, krp = _mla_proj(x, mod, g, w_dq, norm_q, w_uq, gq, w_dkv, norm_kv, tab)
    cache_krp = jnp.pad(cache_krope, ((0, 0), (0, 0), (QK_NOPE_DIM, HEAD_PAD - QK_HEAD_DIM)))
    ckv_s = ckv[N_PROMPT:].reshape(DEC_BATCH, DEC_SEQ, KV_RANK)
    krp_s = krp[N_PROMPT:].reshape(DEC_BATCH, DEC_SEQ, HEAD_PAD)
    ckv_all = jnp.concatenate([jnp.concatenate([cache_ckv, ckv_s], axis=1).reshape(-1, KV_RANK),
                               ckv[:N_PROMPT]], axis=0)
    krp_all = jnp.concatenate([jnp.concatenate([cache_krp, krp_s], axis=1).reshape(-1, HEAD_PAD),
                               krp[:N_PROMPT]], axis=0)
    k, v = _kv_expand(ckv_all, krp_all, w_ukv, gk, tab)
    a_p = _attention(q, k, v, n_batch=BATCH, t_q=SEQ, t_k=SEQ, q_row0=0, kv_row0=DEC_BATCH * KV_LEN)
    a_s = _attention(q, k, v, n_batch=DEC_BATCH, t_q=DEC_SEQ, t_k=KV_LEN, q_row0=N_PROMPT, kv_row0=0)
    x = _out_proj(jnp.concatenate([a_p, a_s], axis=0), x, mod, w_o)
    state_ckv = ckv[:N_PROMPT].reshape(BATCH, SEQ, KV_RANK)
    state_krope = krp[:N_PROMPT, QK_NOPE_DIM:QK_HEAD_DIM].reshape(BATCH, SEQ, QK_ROPE_DIM)
    return x, state_ckv, state_krope


def _route_kernel(x_ref, mod_ref, g_ref, wr_ref, br_ref, tri_ref, h_ref, meta_ref, cnt_ref, carry):
    i = pl.program_id(0)

    @pl.when(i == 0)
    def _():
        carry[...] = jnp.zeros_like(carry)

    x = x_ref[...]
    m = mod_ref[0]
    h = _rms(x, g_ref[...]) * (1.0 + m[4:5]) + m[3:4]
    h_ref[...] = _pack_bf16_pairs(h)
    logits = _dot(h.astype(BF16), wr_ref[...]) + br_ref[...]
    lane = lax.broadcasted_iota(jnp.int32, logits.shape, 1).astype(F32)
    work = logits
    sel = jnp.zeros(logits.shape, F32)
    hits, tops = [], []
    for k in range(TOP_K):
        mk = jnp.max(work, axis=-1, keepdims=True)
        first = jnp.min(jnp.where(work == mk, lane, float(LANE)), axis=-1, keepdims=True)
        hit = lane == first
        sel = jnp.where(hit, 1.0, sel)
        work = jnp.where(hit, -jnp.inf, work)
        hits.append((hit, first))
        tops.append(mk)
    es = [jnp.exp(t - tops[0]) for t in tops]
    denom = es[0] + es[1] + es[2] + es[3]
    pos = _dot(tri_ref[...], sel.astype(BF16)) + carry[0:1, :]
    carry[...] = carry[...] + jnp.sum(sel, axis=0, keepdims=True)
    cnt_ref[...] = carry[...]
    meta = jnp.zeros(logits.shape, F32)
    for k in range(TOP_K):
        hit, first = hits[k]
        pk = jnp.sum(jnp.where(hit, pos, 0.0), axis=-1, keepdims=True)
        meta = jnp.where(lane == float(k), first, meta)
        meta = jnp.where(lane == float(TOP_K + k), es[k] / denom, meta)
        meta = jnp.where(lane == float(2 * TOP_K + k), pk, meta)
    meta_ref[...] = meta


def _route(x, mod, g, w_router, b_router):
    wr = jnp.pad(w_router, ((0, 0), (0, LANE - N_EXPERTS))).astype(BF16)
    br = jnp.pad(b_router, (0, LANE - N_EXPERTS), constant_values=NEG_BIG).reshape(1, LANE)
    tri = jnp.tri(TB, TB, -1, dtype=BF16)
    return pl.pallas_call(
        _route_kernel,
        grid=(N_TOK // TB,),
        in_specs=[
            pl.BlockSpec((TB, D_MODEL), lambda i: (i, 0)),
            _mod_spec(TB),
            _const_spec((1, D_MODEL)),
            _const_spec((D_MODEL, LANE)),
            _const_spec((1, LANE)),
            _const_spec((TB, TB)),
        ],
        out_specs=[
            pl.BlockSpec((TB, D_MODEL // 2), lambda i: (i, 0)),
            pl.BlockSpec((TB, LANE), lambda i: (i, 0)),
            _const_spec((SUBLANE, LANE)),
        ],
        out_shape=[
            jax.ShapeDtypeStruct((N_TOK, D_MODEL // 2), jnp.uint32),
            jax.ShapeDtypeStruct((N_TOK, LANE), F32),
            jax.ShapeDtypeStruct((SUBLANE, LANE), F32),
        ],
        scratch_shapes=[pltpu.VMEM((SUBLANE, LANE), F32)],
        compiler_params=_cparams(("arbitrary",)),
        name="moe_route",
    )(x, mod, g.reshape(1, -1), wr, br, tri)


def _sc_gather(table, idx, ch):
    b, w = idx.shape[0], table.shape[1]
    per_w = b // SC_WORKERS
    n_ch = per_w // ch
    assert per_w * SC_WORKERS == b and n_ch * ch == per_w and n_ch % 2 == 0
    mesh = plsc.VectorSubcoreMesh(core_axis_name="c", subcore_axis_name="s")

    @functools.partial(
        pl.kernel, mesh=mesh,
        out_type=jax.ShapeDtypeStruct((b, w), table.dtype),
        scratch_types=[
            pltpu.VMEM((n_ch, ch), jnp.int32),
            pltpu.VMEM((ch, w), table.dtype),
            pltpu.VMEM((ch, w), table.dtype),
            pltpu.SemaphoreType.DMA, pltpu.SemaphoreType.DMA,
            pltpu.SemaphoreType.DMA, pltpu.SemaphoreType.DMA,
        ],
        name="sc_row_gather",
    )
    def gather_rows(table_hbm, idx_hbm, out_hbm, idx_v, buf0, buf1, g0, g1, s0, s1):
        wid = lax.axis_index("s") * SC_CORES + lax.axis_index("c")
        base = wid * per_w
        pltpu.sync_copy(idx_hbm.at[wid], idx_v)

        def gather(j, buf, sem):
            return pltpu.make_async_copy(table_hbm.at[idx_v.at[j]], buf, sem)

        def store(j, buf, sem):
            return pltpu.make_async_copy(buf, out_hbm.at[pl.ds(base + j * ch, ch)], sem)

        gather(0, buf0, g0).start()

        @pl.loop(0, n_ch, step=2)
        def _(j):
            @pl.when(j > 0)
            def _():
                store(j - 1, buf1, s1).wait()

            gather(j + 1, buf1, g1).start()
            gather(j, buf0, g0).wait()
            store(j, buf0, s0).start()
            gather(j + 1, buf1, g1).wait()
            store(j + 1, buf1, s1).start()
            store(j, buf0, s0).wait()

            @pl.when(j + 2 < n_ch)
            def _():
                gather(j + 2, buf0, g0).start()

        store(n_ch - 1, buf1, s1).wait()

    return gather_rows(table, idx.reshape(SC_WORKERS, n_ch, ch))


def _sc_dispatch(rows, dest4, ch):
    n, w = rows.shape
    per_w = n // SC_WORKERS
    n_ch = per_w // ch
    assert per_w * SC_WORKERS == n and n_ch * ch == per_w and n_ch % 2 == 0
    mesh = plsc.VectorSubcoreMesh(core_axis_name="c", subcore_axis_name="s")
    idx = dest4.reshape(SC_WORKERS, n_ch, ch, TOP_K).transpose(0, 3, 1, 2).reshape(SC_WORKERS, TOP_K * n_ch, ch)

    @functools.partial(
        pl.kernel, mesh=mesh,
        out_type=jax.ShapeDtypeStruct((N_SLOTS, w), rows.dtype),
        scratch_types=[
            pltpu.VMEM((TOP_K * n_ch, ch), jnp.int32),
            pltpu.VMEM((ch, w), rows.dtype),
            pltpu.VMEM((ch, w), rows.dtype),
            pltpu.SemaphoreType.DMA, pltpu.SemaphoreType.DMA,
            pltpu.SemaphoreType.DMA, pltpu.SemaphoreType.DMA,
        ],
        name="sc_row_dispatch",
    )
    def dispatch_rows(rows_hbm, idx_hbm, out_hbm, idx_v, buf0, buf1, l0, l1, s0, s1):
        wid = lax.axis_index("s") * SC_CORES + lax.axis_index("c")
        base = wid * per_w
        pltpu.sync_copy(idx_hbm.at[wid], idx_v)

        def load(j, buf, sem):
            return pltpu.make_async_copy(rows_hbm.at[pl.ds(base + j * ch, ch)], buf, sem)

        def scatter(j, k, buf, sem):
            return pltpu.make_async_copy(buf, out_hbm.at[idx_v.at[k * n_ch + j]], sem)

        load(0, buf0, l0).start()

        @pl.loop(0, n_ch, step=2)
        def _(j):
            load(j + 1, buf1, l1).start()
            load(j, buf0, l0).wait()
            for k in range(TOP_K):
                scatter(j, k, buf0, s0).start()
            load(j + 1, buf1, l1).wait()
            for k in range(TOP_K):
                scatter(j + 1, k, buf1, s1).start()
            for k in range(TOP_K):
                scatter(j, k, buf0, s0).wait()

            @pl.when(j + 2 < n_ch)
            def _():
                load(j + 2, buf0, l0).start()

            for k in range(TOP_K):
                scatter(j + 1, k, buf1, s1).wait()

    return dispatch_rows(rows, idx)


def _deinterleave_matrix():
    src = jnp.arange(2 * LANE)[:, None]
    dst = jnp.arange(2 * LANE)[None, :]
    want = jnp.where(dst < LANE, 2 * dst, 2 * (dst - LANE) + 1)
    return (src == want).astype(BF16)


def _expert_kernel(te_ref, nu_ref, tv_ref, x_ref, wgu_ref, bgu_ref, wd_ref, bd_ref, perm_ref, o_ref,
                   wgu_bf, wd_bf):
    i = pl.program_id(0)
    prev = te_ref[jnp.maximum(i - 1, 0)]
    fresh = jnp.logical_or(i == 0, te_ref[i] != prev)

    @pl.when(jnp.logical_and(fresh, i < nu_ref[0]))
    def _():
        for b in range(2 * D_FF // (2 * LANE)):
            sl = slice(b * 2 * LANE, (b + 1) * 2 * LANE)
            wgu_bf[:, sl] = _dot(wgu_ref[:, sl].astype(BF16), perm_ref[...]).astype(BF16)
        wd_bf[...] = wd_ref[...].astype(BF16)

    @pl.when(i < nu_ref[0])
    def _():
        row = lax.broadcasted_iota(jnp.int32, (TM, D_MODEL // 2), 0)
        w = jnp.where(row < tv_ref[i], x_ref[...], jnp.uint32(0))
        x = _unpack_bf16_pairs(w).astype(BF16)
        gu = _dot(x, wgu_bf[...]) + bgu_ref[...]
        acts = []
        for b in range(D_FF // LANE):
            glu = jnp.minimum(gu[:, b * 2 * LANE:b * 2 * LANE + LANE], SWIGLU_LIMIT)
            lin = jnp.clip(gu[:, b * 2 * LANE + LANE:(b + 1) * 2 * LANE], -SWIGLU_LIMIT, SWIGLU_LIMIT)
            acts.append((glu * jax.nn.sigmoid(SWIGLU_ALPHA * glu) * (lin + 1.0)).astype(BF16))
        act = jnp.concatenate(acts, axis=1)
        o_ref[...] = _pack_bf16_pairs(_dot(act, wd_bf[...]) + bd_ref[...])


def _experts(buf, tile_expert, n_used, tile_valid, layer, w_gu, b_gu, w_down, b_down):
    bgu = b_gu.reshape(N_EXPERTS, D_FF // LANE, LANE, 2).transpose(0, 1, 3, 2).reshape(N_EXPERTS, 1, 2 * D_FF)

    def row_idx(i, te, nu, tv):
        return (jnp.minimum(i, nu[0] - 1), 0)

    def w_idx(i, te, nu, tv):
        return (layer, te[i], 0, 0)

    def b_idx(i, te, nu, tv):
        return (te[i], 0, 0)

    grid_spec = pltpu.PrefetchScalarGridSpec(
        num_scalar_prefetch=3,
        grid=(N_TILES,),
        in_specs=[
            pl.BlockSpec((TM, D_MODEL // 2), row_idx),
            pl.BlockSpec((None, None, D_MODEL, 2 * D_FF), w_idx),
            pl.BlockSpec((None, 1, 2 * D_FF), b_idx),
            pl.BlockSpec((None, None, D_FF, D_MODEL), w_idx),
            pl.BlockSpec((None, 1, D_MODEL), b_idx),
            pl.BlockSpec((2 * LANE, 2 * LANE), lambda i, te, nu, tv: (0, 0)),
        ],
        out_specs=pl.BlockSpec((TM, D_MODEL // 2), row_idx),
        scratch_shapes=[pltpu.VMEM((D_MODEL, 2 * D_FF), BF16), pltpu.VMEM((D_FF, D_MODEL), BF16)],
    )
    return pl.pallas_call(
        _expert_kernel,
        grid_spec=grid_spec,
        out_shape=jax.ShapeDtypeStruct((N_SLOTS, D_MODEL // 2), jnp.uint32),
        compiler_params=_cparams(("arbitrary",)),
        name="moe_experts",
    )(tile_expert, n_used, tile_valid, buf, w_gu, bgu, w_down, b_down.reshape(N_EXPERTS, 1, D_MODEL),
      _deinterleave_matrix())


def _pack_bf16_pairs(v):
    half = v.shape[1] // 2
    bits = pltpu.bitcast(v.astype(BF16).astype(F32), jnp.uint32)
    return (bits[:, half:] & jnp.uint32(0xFFFF0000)) | (bits[:, :half] >> 16)


def _unpack_bf16_pairs(w):
    return jnp.concatenate([pltpu.bitcast(w << 16, F32), pltpu.bitcast(w & jnp.uint32(0xFFFF0000), F32)],
                           axis=1)


def _combine_kernel(x_ref, mod_ref, y_ref, w_ref, o_ref):
    w = w_ref[...]
    y = _unpack_bf16_pairs(y_ref[0]) * w[:, 0:1]
    for k in range(1, TOP_K):
        y = y + _unpack_bf16_pairs(y_ref[k]) * w[:, k:k + 1]
    o_ref[...] = x_ref[...] + mod_ref[0][5:6] * y


def _combine(x, mod, y4, w4):
    return pl.pallas_call(
        _combine_kernel,
        grid=(N_TOK // TB,),
        in_specs=[
            pl.BlockSpec((TB, D_MODEL), lambda i: (i, 0)),
            _mod_spec(TB),
            pl.BlockSpec((TOP_K, TB, D_MODEL // 2), lambda i: (0, i, 0)),
            pl.BlockSpec((TB, TOP_K), lambda i: (i, 0)),
        ],
        out_specs=pl.BlockSpec((TB, D_MODEL), lambda i: (i, 0)),
        out_shape=jax.ShapeDtypeStruct((N_TOK, D_MODEL), F32),
        compiler_params=_cparams(("parallel",)),
        name="moe_combine",
    )(x, mod, y4, w4)


def _moe_layer(x, mod, g, layer, w_router, b_router, w_gu, b_gu, w_down, b_down):
    hp, meta, cnt = _route(x, mod, g, w_router, b_router)
    counts = cnt[0, :N_EXPERTS].astype(jnp.int32)
    tiles = (counts + TM - 1) // TM
    tile_end = jnp.cumsum(tiles)
    start = (tile_end - tiles) * TM
    n_used = tile_end[-1:].astype(jnp.int32)
    tile_ids = jnp.minimum(jnp.arange(N_TILES), n_used[0] - 1)
    tile_expert = jnp.sum(tile_end[None, :] <= tile_ids[:, None], axis=1).astype(jnp.int32)
    first_tile = tile_end - tiles
    tile_valid = jnp.clip(counts[tile_expert] - (tile_ids - first_tile[tile_expert]) * TM, 0, TM).astype(jnp.int32)
    e4 = meta[:, 0:TOP_K].astype(jnp.int32)
    w4 = meta[:, TOP_K:2 * TOP_K]
    dest4 = start[e4] + meta[:, 2 * TOP_K:3 * TOP_K].astype(jnp.int32)
    buf = _sc_dispatch(hp, dest4, 64)
    yb = _experts(buf, tile_expert, n_used, tile_valid, layer, w_gu, b_gu, w_down, b_down)
    y4 = _sc_gather(yb, dest4.T.reshape(-1), 64).reshape(TOP_K, N_TOK, D_MODEL // 2)
    return _combine(x, mod, y4, w4)


def kernel(x_prompt, x_sample, c, cache_ckv, cache_krope, c_ctx, norm_mix_g, norm_ffn_g, w_mod, b_mod,
           g_w_in, g_b_in, g_norm_v, g_w_s, g_b_s, g_w_out, m_w_dq, m_norm_q, m_w_uq, m_w_dkv,
           m_norm_kv, m_w_ukv, m_qk_norm_q, m_qk_norm_k, m_w_o, e_w_router, e_b_router, e_w_gu,
           e_b_gu, e_w_down, e_b_down):
    x = jnp.concatenate([x_prompt.reshape(N_PROMPT, D_MODEL), x_sample.reshape(N_SAMPLE, D_MODEL)], axis=0)
    cond = jnp.concatenate([c_ctx[None, :], c, jnp.zeros((SUBLANE - N_COND, D_MODEL), F32)], axis=0)
    mod = _modulation(cond, w_mod, b_mod)
    tab = _rope_tables()
    ckv_states, krope_states = [], []
    for layer in range(DEPTH):
        j = layer // 2
        if layer % 2 == 0:
            x = _gmlp_layer(x, mod[layer], norm_mix_g[layer], g_w_in[j], g_b_in[j], g_norm_v[j],
                            g_w_s[j], g_b_s[j], g_w_out[j])
        else:
            x, s_ckv, s_krope = _mla_layer(
                x, mod[layer], norm_mix_g[layer], cache_ckv[:, j], cache_krope[:, j], tab,
                m_w_dq[j], m_norm_q[j], m_w_uq[j], m_w_dkv[j], m_norm_kv[j], m_w_ukv[j],
                m_qk_norm_q[j], m_qk_norm_k[j], m_w_o[j])
            ckv_states.append(s_ckv)
            krope_states.append(s_krope)
        x = _moe_layer(x, mod[layer], norm_ffn_g[layer], layer, e_w_router[layer], e_b_router[layer],
                       e_w_gu, e_b_gu[layer], e_w_down, e_b_down[layer])
    y_prompt = x[:N_PROMPT].reshape(BATCH, SEQ, D_MODEL)
    y_sample = x[N_PROMPT:].reshape(DEC_BATCH, DEC_SEQ, D_MODEL)
    return (y_prompt, y_sample, jnp.stack(ckv_states, axis=1), jnp.stack(krope_states, axis=1))
```

```python
import functools
import math

import jax
import jax.numpy as jnp
from jax import lax
from jax.experimental import pallas as pl
from jax.experimental.pallas import tpu as pltpu
from jax.experimental.pallas import tpu_sc as plsc

F32 = jnp.float32
BF16 = jnp.bfloat16

D_MODEL = 1024
BATCH = 32
SEQ = 256
DEPTH = 4
DEC_BATCH = 2
DEC_SEQ = 4096
PAST_LEN = 512
GRID_W = 64
RMS_EPS = 1e-6
GMLP_WIDTH = 2 * D_MODEL
GMLP_GROUPS = 8
GROUP_W = GMLP_WIDTH // GMLP_GROUPS
CHUNK = 128
N_HEADS = 16
QK_NOPE_DIM = 64
QK_ROPE_DIM = 32
QK_HEAD_DIM = QK_NOPE_DIM + QK_ROPE_DIM
V_HEAD_DIM = 64
Q_RANK = 256
KV_RANK = 128
ROPE_THETA = 10000.0
N_EXPERTS = 32
TOP_K = 4
D_FF = D_MODEL
SWIGLU_LIMIT = 7.0
SWIGLU_ALPHA = 1.702

N_PROMPT = BATCH * SEQ
N_SAMPLE = DEC_BATCH * DEC_SEQ
N_TOK = N_PROMPT + N_SAMPLE
N_COND = 1 + DEC_BATCH
KV_LEN = PAST_LEN + DEC_SEQ
N_KV_ROWS = DEC_BATCH * KV_LEN + N_PROMPT

LANE = 128
SUBLANE = 8
HEAD_PAD = LANE
QKV_W = N_HEADS * HEAD_PAD
VMEM_LIMIT = 56 * 1024 * 1024

TB = 256
TQ = 256
TKC = 256
TM = 512
N_TILES = N_TOK * TOP_K // TM + N_EXPERTS
N_SLOTS = N_TILES * TM
SC_CORES = 2
SC_WORKERS = SC_CORES * 16
ROPE_BLOCKS = DEC_SEQ // TB
NEG_BIG = -1e30


def _cparams(sem):
    return pltpu.CompilerParams(dimension_semantics=sem, vmem_limit_bytes=VMEM_LIMIT)


def _cond_of_block(i, tb):
    n_p = N_PROMPT // tb
    per = DEC_SEQ // tb
    return jnp.where(i < n_p, 0, 1 + (i - n_p) // per)


def _rms(x, g, n=None):
    n = x.shape[-1] if n is None else n
    ss = jnp.sum(x * x, axis=-1, keepdims=True) * (1.0 / n)
    return x * lax.rsqrt(ss + RMS_EPS) * g


def _dot(a, b):
    return jnp.dot(a, b, preferred_element_type=F32)


def _mod_kernel(c_ref, w_ref, b_ref, o_ref):
    c = c_ref[...]
    s = c * jax.nn.sigmoid(c)
    o_ref[0] = _dot(s.astype(BF16), w_ref[0].astype(BF16)) + b_ref[0]


def _modulation(cond, w_mod, b_mod):
    tn = 1536
    out = pl.pallas_call(
        _mod_kernel,
        grid=(DEPTH, 6 * D_MODEL // tn),
        in_specs=[
            pl.BlockSpec((SUBLANE, D_MODEL), lambda l, j: (0, 0)),
            pl.BlockSpec((1, D_MODEL, tn), lambda l, j: (l, 0, j)),
            pl.BlockSpec((1, 1, tn), lambda l, j: (l, 0, j)),
        ],
        out_specs=pl.BlockSpec((1, SUBLANE, tn), lambda l, j: (l, 0, j)),
        out_shape=jax.ShapeDtypeStruct((DEPTH, SUBLANE, 6 * D_MODEL), F32),
        compiler_params=_cparams(("parallel", "parallel")),
        name="adaln_mod",
    )(cond, w_mod, b_mod.reshape(DEPTH, 1, 6 * D_MODEL))
    m = out[:, :N_COND].reshape(DEPTH, N_COND, 6, D_MODEL)
    return jnp.pad(m, ((0, 0), (0, 0), (0, SUBLANE - 6), (0, 0)))


def _mod_spec(tb):
    return pl.BlockSpec((1, SUBLANE, D_MODEL), lambda i: (_cond_of_block(i, tb), 0, 0))


def _const_spec(shape):
    nd = len(shape)
    return pl.BlockSpec(shape, lambda i: (0,) * nd)


def _gmlp_kernel(x_ref, mod_ref, g_ref, win_ref, bin_ref, gv_ref, ws_ref, bst_ref, wout_ref, o_ref):
    x = x_ref[...]
    m = mod_ref[0]
    h = _rms(x, g_ref[...]) * (1.0 + m[1:2]) + m[0:1]
    hb = h.astype(BF16)
    zv = jax.nn.gelu(_dot(hb, win_ref[:, GMLP_WIDTH:]) + bin_ref[:, GMLP_WIDTH:], approximate=True)
    vn = _rms(zv, gv_ref[...]).astype(BF16)
    rows = []
    for c in range(TB // CHUNK):
        cols = []
        for g in range(GMLP_GROUPS):
            blk = vn[c * CHUNK:(c + 1) * CHUNK, g * GROUP_W:(g + 1) * GROUP_W]
            cols.append(_dot(ws_ref[g], blk) + bst_ref[:, g:g + 1])
        rows.append(jnp.concatenate(cols, axis=1))
    vm = jnp.concatenate(rows, axis=0)
    u = jax.nn.gelu(_dot(hb, win_ref[:, :GMLP_WIDTH]) + bin_ref[:, :GMLP_WIDTH], approximate=True)
    d = _dot((u * vm).astype(BF16), wout_ref[...])
    o_ref[...] = x + m[2:3] * d


def _gmlp_layer(x, mod, g, w_in, b_in, g_v, w_s, b_s, w_out):
    return pl.pallas_call(
        _gmlp_kernel,
        grid=(N_TOK // TB,),
        in_specs=[
            pl.BlockSpec((TB, D_MODEL), lambda i: (i, 0)),
            _mod_spec(TB),
            _const_spec((1, D_MODEL)),
            _const_spec((D_MODEL, 2 * GMLP_WIDTH)),
            _const_spec((1, 2 * GMLP_WIDTH)),
            _const_spec((1, GMLP_WIDTH)),
            _const_spec((GMLP_GROUPS, CHUNK, CHUNK)),
            _const_spec((CHUNK, GMLP_GROUPS)),
            _const_spec((GMLP_WIDTH, D_MODEL)),
        ],
        out_specs=pl.BlockSpec((TB, D_MODEL), lambda i: (i, 0)),
        out_shape=jax.ShapeDtypeStruct((N_TOK, D_MODEL), F32),
        compiler_params=_cparams(("parallel",)),
        name="gmlp_mixer",
    )(x, mod, g.reshape(1, -1), w_in.astype(BF16), b_in.reshape(1, -1), g_v.reshape(1, -1),
      w_s.astype(BF16), b_s.T, w_out.astype(BF16))


def _rope_tables():
    t = jnp.arange(DEC_SEQ)
    row_id = (t // GRID_W).astype(F32)
    col_id = (t % GRID_W).astype(F32)
    axis_dim = QK_ROPE_DIM // 2
    inv_freq = ROPE_THETA ** (-jnp.arange(0, axis_dim, 2, dtype=F32) / axis_dim)
    ang = jnp.stack([row_id[:, None] * inv_freq, col_id[:, None] * inv_freq], axis=1)
    cos, sin = jnp.cos(ang), jnp.sin(ang)
    zeros = jnp.zeros_like(sin)
    cos_l = jnp.concatenate([cos, cos], axis=-1).reshape(DEC_SEQ, QK_ROPE_DIM)
    s1_l = jnp.concatenate([-sin, zeros], axis=-1).reshape(DEC_SEQ, QK_ROPE_DIM)
    s2_l = jnp.concatenate([zeros, sin], axis=-1).reshape(DEC_SEQ, QK_ROPE_DIM)

    def widen(rope_part, nope_fill):
        left = jnp.full((DEC_SEQ, QK_NOPE_DIM), nope_fill, F32)
        right = jnp.zeros((DEC_SEQ, HEAD_PAD - QK_HEAD_DIM), F32)
        return jnp.concatenate([left, rope_part, right], axis=-1)

    pos = jnp.stack([widen(cos_l, 1.0), widen(s1_l, 0.0), widen(s2_l, 0.0)])
    ident_c = jnp.concatenate([jnp.ones((TB, QK_HEAD_DIM), F32),
                               jnp.zeros((TB, HEAD_PAD - QK_HEAD_DIM), F32)], axis=-1)
    ident = jnp.stack([ident_c, jnp.zeros_like(ident_c), jnp.zeros_like(ident_c)])
    return jnp.concatenate([pos, ident], axis=1)


def _rope(xn, tab_ref):
    half = QK_ROPE_DIM // 4
    return (xn * tab_ref[0] + pltpu.roll(xn, HEAD_PAD - half, 1) * tab_ref[1]
            + pltpu.roll(xn, half, 1) * tab_ref[2])


def _dot_nt(a, b):
    return lax.dot_general(a, b, (((1,), (1,)), ((), ())), preferred_element_type=F32)


def _shift_rows(x, n):
    n = n % x.shape[0]
    return jnp.concatenate([x[n:], x[:n]], axis=0)


def _mla_proj_kernel(x_ref, mod_ref, g_ref, wdq_ref, nq_ref, wuqt_ref, gq_ref, wdkv_ref, nkv_ref,
                     tabt_ref, qt_ref, ckv_ref, krp_ref):
    x = x_ref[...]
    m = mod_ref[0]
    h = _rms(x, g_ref[...]) * (1.0 + m[1:2]) + m[0:1]
    hb = h.astype(BF16)
    cq = _rms(_dot(hb, wdq_ref[...]), nq_ref[...])
    qt = _dot_nt(wuqt_ref[...], cq.astype(BF16))
    gq = gq_ref[...]
    half = QK_ROPE_DIM // 4
    for hd in range(N_HEADS):
        qh = qt[hd * HEAD_PAD:(hd + 1) * HEAD_PAD, :]
        ss = jnp.sum(qh * qh, axis=0, keepdims=True) * (1.0 / QK_HEAD_DIM)
        qn = qh * lax.rsqrt(ss + RMS_EPS) * gq
        qr = qn * tabt_ref[0] + _shift_rows(qn, half) * tabt_ref[1] + _shift_rows(qn, -half) * tabt_ref[2]
        qt_ref[hd * HEAD_PAD:(hd + 1) * HEAD_PAD, :] = qr.astype(BF16)
    kva = _dot(hb, wdkv_ref[...])
    ckv_ref[...] = _rms(kva[:, :KV_RANK], nkv_ref[...])
    krp_ref[...] = kva[:, KV_RANK:]


def _pad_heads(w, per_head):
    k = w.shape[0]
    w = w.reshape(k, N_HEADS, per_head)
    return jnp.pad(w, ((0, 0), (0, 0), (0, HEAD_PAD - per_head))).reshape(k, QKV_W)


def _pad_gain(g):
    return jnp.pad(g, (0, HEAD_PAD - QK_HEAD_DIM)).reshape(1, HEAD_PAD)


def _mla_proj(x, mod, g, w_dq, norm_q, w_uq, gq, w_dkv, norm_kv, tab_t):
    wdkv = jnp.concatenate([w_dkv[:, :KV_RANK], jnp.zeros((D_MODEL, QK_NOPE_DIM), F32),
                            w_dkv[:, KV_RANK:], jnp.zeros((D_MODEL, HEAD_PAD - QK_HEAD_DIM), F32)], axis=1)
    n_pb = N_PROMPT // TB

    def tab_idx(i):
        return (0, 0, jnp.where(i < n_pb, ROPE_BLOCKS, (i - n_pb) % ROPE_BLOCKS))

    return pl.pallas_call(
        _mla_proj_kernel,
        grid=(N_TOK // TB,),
        in_specs=[
            pl.BlockSpec((TB, D_MODEL), lambda i: (i, 0)),
            _mod_spec(TB),
            _const_spec((1, D_MODEL)),
            _const_spec((D_MODEL, Q_RANK)),
            _const_spec((1, Q_RANK)),
            _const_spec((QKV_W, Q_RANK)),
            _const_spec((HEAD_PAD, 1)),
            _const_spec((D_MODEL, 2 * LANE)),
            _const_spec((1, KV_RANK)),
            pl.BlockSpec((3, HEAD_PAD, TB), tab_idx),
        ],
        out_specs=[
            pl.BlockSpec((QKV_W, TB), lambda i: (0, i)),
            pl.BlockSpec((TB, KV_RANK), lambda i: (i, 0)),
            pl.BlockSpec((TB, HEAD_PAD), lambda i: (i, 0)),
        ],
        out_shape=[
            jax.ShapeDtypeStruct((QKV_W, N_TOK), BF16),
            jax.ShapeDtypeStruct((N_TOK, KV_RANK), F32),
            jax.ShapeDtypeStruct((N_TOK, HEAD_PAD), F32),
        ],
        compiler_params=_cparams(("parallel",)),
        name="mla_proj",
    )(x, mod, g.reshape(1, -1), w_dq.astype(BF16), norm_q.reshape(1, -1),
      _pad_heads(w_uq, QK_HEAD_DIM).T.astype(BF16), _pad_gain(gq).reshape(HEAD_PAD, 1), wdkv.astype(BF16),
      norm_kv.reshape(1, -1), tab_t)


def _kv_expand_kernel(ckv_ref, krp_ref, wukv_ref, wukvt_ref, gk_ref, tab_ref, k_ref, vt_ref):
    ckv = ckv_ref[...].astype(BF16)
    kv = _dot(ckv, wukv_ref[...])
    kvt = _dot_nt(wukvt_ref[...], ckv)
    krp = krp_ref[...]
    gk = gk_ref[...]
    lane = lax.broadcasted_iota(jnp.int32, (TB, HEAD_PAD), 1)
    is_nope = lane < QK_NOPE_DIM
    row = lax.broadcasted_iota(jnp.int32, (HEAD_PAD, TB), 0)
    ones_row = jnp.where(row == 0, 1.0, 0.0)
    for hd in range(N_HEADS):
        kvh = kv[:, hd * HEAD_PAD:(hd + 1) * HEAD_PAD]
        kn = _rms(jnp.where(is_nope, kvh, krp), gk, QK_HEAD_DIM)
        k_ref[:, hd * HEAD_PAD:(hd + 1) * HEAD_PAD] = _rope(kn, tab_ref).astype(BF16)
        vth = kvt[hd * HEAD_PAD:(hd + 1) * HEAD_PAD, :]
        vt_ref[hd * HEAD_PAD:(hd + 1) * HEAD_PAD, :] = jnp.where(row < QK_NOPE_DIM, ones_row, vth).astype(BF16)


def _kv_expand(ckv_all, krp_all, w_ukv, gk, tab):
    n_sb = DEC_BATCH * KV_LEN // TB
    per = KV_LEN // TB
    n_cache = PAST_LEN // TB

    def tab_idx(i):
        j = i % per
        blk = jnp.where((i >= n_sb) | (j < n_cache), ROPE_BLOCKS, j - n_cache)
        return (0, blk, 0)

    return pl.pallas_call(
        _kv_expand_kernel,
        grid=(N_KV_ROWS // TB,),
        in_specs=[
            pl.BlockSpec((TB, KV_RANK), lambda i: (i, 0)),
            pl.BlockSpec((TB, HEAD_PAD), lambda i: (i, 0)),
            _const_spec((KV_RANK, QKV_W)),
            _const_spec((QKV_W, KV_RANK)),
            _const_spec((1, HEAD_PAD)),
            pl.BlockSpec((3, TB, HEAD_PAD), tab_idx),
        ],
        out_specs=[
            pl.BlockSpec((TB, QKV_W), lambda i: (i, 0)),
            pl.BlockSpec((QKV_W, TB), lambda i: (0, i)),
        ],
        out_shape=[
            jax.ShapeDtypeStruct((N_KV_ROWS, QKV_W), BF16),
            jax.ShapeDtypeStruct((QKV_W, N_KV_ROWS), BF16),
        ],
        compiler_params=_cparams(("parallel",)),
        name="mla_kv_expand",
    )(ckv_all, krp_all, w_ukv.astype(BF16), w_ukv.T.astype(BF16), _pad_gain(gk), tab)


def _attn_kernel(qt_ref, k_ref, vt_ref, o_ref, s_a, s_b, *, t_k, hps):
    c = (1.0 / math.sqrt(QK_HEAD_DIM)) * math.log2(math.e)
    kc = min(TKC, t_k)
    n_chunks = t_k // kc
    bufs = (s_a, s_b)

    def rows(h):
        return slice(h * HEAD_PAD, (h + 1) * HEAD_PAD)

    def scores(h, j, m8):
        st = _dot(k_ref[j * kc:(j + 1) * kc, rows(h)], qt_ref[rows(h), :])
        bufs[h % 2][j * kc:(j + 1) * kc, :] = st
        return jnp.maximum(m8, jnp.max(st.reshape(kc // SUBLANE, SUBLANE, TQ), axis=0))

    def weigh(h, j, m, acc):
        pt = jnp.exp2((bufs[h % 2][j * kc:(j + 1) * kc, :] - m) * c).astype(BF16)
        return acc + _dot(vt_ref[rows(h), j * kc:(j + 1) * kc], pt)

    m8_init = jnp.full((SUBLANE, TQ), NEG_BIG, F32)
    outs = []
    m8 = m8_init
    for j in range(n_chunks):
        m8 = scores(0, j, m8)
    for h in range(1, hps + 1):
        m = jnp.max(m8, axis=0, keepdims=True)
        acc = jnp.zeros((HEAD_PAD, TQ), F32)
        m8 = m8_init
        for j in range(n_chunks):
            if h < hps:
                m8 = scores(h, j, m8)
            acc = weigh(h - 1, j, m, acc)
        outs.append(acc[QK_NOPE_DIM:, :] / acc[0:1, :])
    o_ref[...] = jnp.concatenate(outs, axis=0).T.astype(BF16)


def _attention(qt, k, vt, *, n_batch, t_q, t_k, q_row0, kv_row0, hps):
    nq = t_q // TQ
    q0 = q_row0 // TQ
    k0 = kv_row0 // t_k
    return pl.pallas_call(
        functools.partial(_attn_kernel, t_k=t_k, hps=hps),
        grid=(n_batch, N_HEADS // hps, nq),
        in_specs=[
            pl.BlockSpec((hps * HEAD_PAD, TQ), lambda b, h, i: (h, q0 + b * nq + i)),
            pl.BlockSpec((t_k, hps * HEAD_PAD), lambda b, h, i: (k0 + b, h)),
            pl.BlockSpec((hps * HEAD_PAD, t_k), lambda b, h, i: (h, k0 + b)),
        ],
        out_specs=pl.BlockSpec((TQ, hps * V_HEAD_DIM), lambda b, h, i: (b * nq + i, h)),
        out_shape=jax.ShapeDtypeStruct((n_batch * t_q, N_HEADS * V_HEAD_DIM), BF16),
        scratch_shapes=[pltpu.VMEM((t_k, TQ), F32), pltpu.VMEM((t_k, TQ), F32)],
        compiler_params=_cparams(("parallel", "parallel", "parallel")),
        name=f"mla_attention_tk{t_k}",
    )(qt, k, vt)


def _out_proj_kernel(a_ref, x_ref, mod_ref, wo_ref, o_ref):
    o_ref[...] = x_ref[...] + mod_ref[0][2:3] * _dot(a_ref[...], wo_ref[...])


def _out_proj(attn, x, mod, w_o):
    return pl.pallas_call(
        _out_proj_kernel,
        grid=(N_TOK // TB,),
        in_specs=[
            pl.BlockSpec((TB, D_MODEL), lambda i: (i, 0)),
            pl.BlockSpec((TB, D_MODEL), lambda i: (i, 0)),
            _mod_spec(TB),
            _const_spec((D_MODEL, D_MODEL)),
        ],
        out_specs=pl.BlockSpec((TB, D_MODEL), lambda i: (i, 0)),
        out_shape=jax.ShapeDtypeStruct((N_TOK, D_MODEL), F32),
        compiler_params=_cparams(("parallel",)),
        name="mla_out_proj",
    )(attn, x, mod, w_o.astype(BF16))


def _mla_layer(x, mod, g, cache_ckv, cache_krope, tab, w_dq, norm_q, w_uq, w_dkv, norm_kv, w_ukv,
               gq, gk, w_o):
    tab_t = jnp.swapaxes(tab, 1, 2)
    qt, ckv, krp = _mla_proj(x, mod, g, w_dq, norm_q, w_uq, gq, w_dkv, norm_kv, tab_t)
    cache_krp = jnp.pad(cache_krope, ((0, 0), (0, 0), (QK_NOPE_DIM, HEAD_PAD - QK_HEAD_DIM)))
    ckv_s = ckv[N_PROMPT:].reshape(DEC_BATCH, DEC_SEQ, KV_RANK)
    krp_s = krp[N_PROMPT:].reshape(DEC_BATCH, DEC_SEQ, HEAD_PAD)
    ckv_all = jnp.concatenate([jnp.concatenate([cache_ckv, ckv_s], axis=1).reshape(-1, KV_RANK),
                               ckv[:N_PROMPT]], axis=0)
    krp_all = jnp.concatenate([jnp.concatenate([cache_krp, krp_s], axis=1).reshape(-1, HEAD_PAD),
                               krp[:N_PROMPT]], axis=0)
    k, vt = _kv_expand(ckv_all, krp_all, w_ukv, gk, tab)
    a_p = _attention(qt, k, vt, n_batch=BATCH, t_q=SEQ, t_k=SEQ, q_row0=0, kv_row0=DEC_BATCH * KV_LEN,
                     hps=N_HEADS)
    a_s = _attention(qt, k, vt, n_batch=DEC_BATCH, t_q=DEC_SEQ, t_k=KV_LEN, q_row0=N_PROMPT, kv_row0=0,
                     hps=4)
    x = _out_proj(jnp.concatenate([a_p, a_s], axis=0), x, mod, w_o)
    state_ckv = ckv[:N_PROMPT].reshape(BATCH, SEQ, KV_RANK)
    state_krope = krp[:N_PROMPT, QK_NOPE_DIM:QK_HEAD_DIM].reshape(BATCH, SEQ, QK_ROPE_DIM)
    return x, state_ckv, state_krope


def _route_kernel(x_ref, mod_ref, g_ref, wr_ref, br_ref, tri_ref, h_ref, meta_ref, cnt_ref, carry):
    i = pl.program_id(0)

    @pl.when(i == 0)
    def _():
        carry[...] = jnp.zeros_like(carry)

    x = x_ref[...]
    m = mod_ref[0]
    h = _rms(x, g_ref[...]) * (1.0 + m[4:5]) + m[3:4]
    h_ref[...] = _pack_bf16_pairs(h)
    logits = _dot(h.astype(BF16), wr_ref[...]) + br_ref[...]
    lane = lax.broadcasted_iota(jnp.int32, logits.shape, 1).astype(F32)
    work = logits
    sel = jnp.zeros(logits.shape, F32)
    hits, tops = [], []
    for k in range(TOP_K):
        mk = jnp.max(work, axis=-1, keepdims=True)
        first = jnp.min(jnp.where(work == mk, lane, float(LANE)), axis=-1, keepdims=True)
        hit = lane == first
        sel = jnp.where(hit, 1.0, sel)
        work = jnp.where(hit, -jnp.inf, work)
        hits.append((hit, first))
        tops.append(mk)
    es = [jnp.exp(t - tops[0]) for t in tops]
    denom = es[0] + es[1] + es[2] + es[3]
    pos = _dot(tri_ref[...], sel.astype(BF16)) + carry[0:1, :]
    carry[...] = carry[...] + jnp.sum(sel, axis=0, keepdims=True)
    cnt_ref[...] = carry[...]
    meta = jnp.zeros(logits.shape, F32)
    for k in range(TOP_K):
        hit, first = hits[k]
        pk = jnp.sum(jnp.where(hit, pos, 0.0), axis=-1, keepdims=True)
        meta = jnp.where(lane == float(k), first, meta)
        meta = jnp.where(lane == float(TOP_K + k), es[k] / denom, meta)
        meta = jnp.where(lane == float(2 * TOP_K + k), pk, meta)
    meta_ref[...] = meta


def _route(x, mod, g, w_router, b_router):
    wr = jnp.pad(w_router, ((0, 0), (0, LANE - N_EXPERTS))).astype(BF16)
    br = jnp.pad(b_router, (0, LANE - N_EXPERTS), constant_values=NEG_BIG).reshape(1, LANE)
    tri = jnp.tri(TB, TB, -1, dtype=BF16)
    return pl.pallas_call(
        _route_kernel,
        grid=(N_TOK // TB,),
        in_specs=[
            pl.BlockSpec((TB, D_MODEL), lambda i: (i, 0)),
            _mod_spec(TB),
            _const_spec((1, D_MODEL)),
            _const_spec((D_MODEL, LANE)),
            _const_spec((1, LANE)),
            _const_spec((TB, TB)),
        ],
        out_specs=[
            pl.BlockSpec((TB, D_MODEL // 2), lambda i: (i, 0)),
            pl.BlockSpec((TB, LANE), lambda i: (i, 0)),
            _const_spec((SUBLANE, LANE)),
        ],
        out_shape=[
            jax.ShapeDtypeStruct((N_TOK, D_MODEL // 2), jnp.uint32),
            jax.ShapeDtypeStruct((N_TOK, LANE), F32),
            jax.ShapeDtypeStruct((SUBLANE, LANE), F32),
        ],
        scratch_shapes=[pltpu.VMEM((SUBLANE, LANE), F32)],
        compiler_params=_cparams(("arbitrary",)),
        name="moe_route",
    )(x, mod, g.reshape(1, -1), wr, br, tri)


def _sc_gather(table, idx, ch):
    b, w = idx.shape[0], table.shape[1]
    per_w = b // SC_WORKERS
    n_ch = per_w // ch
    assert per_w * SC_WORKERS == b and n_ch * ch == per_w and n_ch % 2 == 0
    mesh = plsc.VectorSubcoreMesh(core_axis_name="c", subcore_axis_name="s")

    @functools.partial(
        pl.kernel, mesh=mesh,
        out_type=jax.ShapeDtypeStruct((b, w), table.dtype),
        scratch_types=[
            pltpu.VMEM((n_ch, ch), jnp.int32),
            pltpu.VMEM((ch, w), table.dtype),
            pltpu.VMEM((ch, w), table.dtype),
            pltpu.SemaphoreType.DMA, pltpu.SemaphoreType.DMA,
            pltpu.SemaphoreType.DMA, pltpu.SemaphoreType.DMA,
        ],
        name="sc_row_gather",
    )
    def gather_rows(table_hbm, idx_hbm, out_hbm, idx_v, buf0, buf1, g0, g1, s0, s1):
        wid = lax.axis_index("s") * SC_CORES + lax.axis_index("c")
        base = wid * per_w
        pltpu.sync_copy(idx_hbm.at[wid], idx_v)

        def gather(j, buf, sem):
            return pltpu.make_async_copy(table_hbm.at[idx_v.at[j]], buf, sem)

        def store(j, buf, sem):
            return pltpu.make_async_copy(buf, out_hbm.at[pl.ds(base + j * ch, ch)], sem)

        gather(0, buf0, g0).start()

        @pl.loop(0, n_ch, step=2)
        def _(j):
            @pl.when(j > 0)
            def _():
                store(j - 1, buf1, s1).wait()

            gather(j + 1, buf1, g1).start()
            gather(j, buf0, g0).wait()
            store(j, buf0, s0).start()
            gather(j + 1, buf1, g1).wait()
            store(j + 1, buf1, s1).start()
            store(j, buf0, s0).wait()

            @pl.when(j + 2 < n_ch)
            def _():
                gather(j + 2, buf0, g0).start()

        store(n_ch - 1, buf1, s1).wait()

    return gather_rows(table, idx.reshape(SC_WORKERS, n_ch, ch))


def _sc_dispatch(rows, dest4, ch):
    n, w = rows.shape
    per_w = n // SC_WORKERS
    n_ch = per_w // ch
    assert per_w * SC_WORKERS == n and n_ch * ch == per_w and n_ch % 2 == 0
    mesh = plsc.VectorSubcoreMesh(core_axis_name="c", subcore_axis_name="s")
    idx = dest4.reshape(SC_WORKERS, n_ch, ch, TOP_K).transpose(0, 3, 1, 2).reshape(SC_WORKERS, TOP_K * n_ch, ch)

    @functools.partial(
        pl.kernel, mesh=mesh,
        out_type=jax.ShapeDtypeStruct((N_SLOTS, w), rows.dtype),
        scratch_types=[
            pltpu.VMEM((TOP_K * n_ch, ch), jnp.int32),
            pltpu.VMEM((ch, w), rows.dtype),
            pltpu.VMEM((ch, w), rows.dtype),
            pltpu.SemaphoreType.DMA, pltpu.SemaphoreType.DMA,
            pltpu.SemaphoreType.DMA, pltpu.SemaphoreType.DMA,
        ],
        name="sc_row_dispatch",
    )
    def dispatch_rows(rows_hbm, idx_hbm, out_hbm, idx_v, buf0, buf1, l0, l1, s0, s1):
        wid = lax.axis_index("s") * SC_CORES + lax.axis_index("c")
        base = wid * per_w
        pltpu.sync_copy(idx_hbm.at[wid], idx_v)

        def load(j, buf, sem):
            return pltpu.make_async_copy(rows_hbm.at[pl.ds(base + j * ch, ch)], buf, sem)

        def scatter(j, k, buf, sem):
            return pltpu.make_async_copy(buf, out_hbm.at[idx_v.at[k * n_ch + j]], sem)

        load(0, buf0, l0).start()

        @pl.loop(0, n_ch, step=2)
        def _(j):
            load(j + 1, buf1, l1).start()
            load(j, buf0, l0).wait()
            for k in range(TOP_K):
                scatter(j, k, buf0, s0).start()
            load(j + 1, buf1, l1).wait()
            for k in range(TOP_K):
                scatter(j + 1, k, buf1, s1).start()
            for k in range(TOP_K):
                scatter(j, k, buf0, s0).wait()

            @pl.when(j + 2 < n_ch)
            def _():
                load(j + 2, buf0, l0).start()

            for k in range(TOP_K):
                scatter(j + 1, k, buf1, s1).wait()

    return dispatch_rows(rows, idx)


def _deinterleave_matrix():
    src = jnp.arange(2 * LANE)[:, None]
    dst = jnp.arange(2 * LANE)[None, :]
    want = jnp.where(dst < LANE, 2 * dst, 2 * (dst - LANE) + 1)
    return (src == want).astype(BF16)


def _expert_kernel(te_ref, nu_ref, tv_ref, x_ref, wgu_ref, bgu_ref, wd_ref, bd_ref, perm_ref, o_ref,
                   wgu_bf, wd_bf):
    i = pl.program_id(0)
    prev = te_ref[jnp.maximum(i - 1, 0)]
    fresh = jnp.logical_or(i == 0, te_ref[i] != prev)

    @pl.when(jnp.logical_and(fresh, i < nu_ref[0]))
    def _():
        for b in range(2 * D_FF // (2 * LANE)):
            sl = slice(b * 2 * LANE, (b + 1) * 2 * LANE)
            wgu_bf[:, sl] = _dot(wgu_ref[:, sl].astype(BF16), perm_ref[...]).astype(BF16)
        wd_bf[...] = wd_ref[...].astype(BF16)

    @pl.when(i < nu_ref[0])
    def _():
        row = lax.broadcasted_iota(jnp.int32, (TM, D_MODEL // 2), 0)
        w = jnp.where(row < tv_ref[i], x_ref[...], jnp.uint32(0))
        x = _unpack_bf16_pairs(w).astype(BF16)
        gu = _dot(x, wgu_bf[...]) + bgu_ref[...]
        acts = []
        for b in range(D_FF // LANE):
            glu = jnp.minimum(gu[:, b * 2 * LANE:b * 2 * LANE + LANE], SWIGLU_LIMIT)
            lin = jnp.clip(gu[:, b * 2 * LANE + LANE:(b + 1) * 2 * LANE], -SWIGLU_LIMIT, SWIGLU_LIMIT)
            acts.append((glu * jax.nn.sigmoid(SWIGLU_ALPHA * glu) * (lin + 1.0)).astype(BF16))
        act = jnp.concatenate(acts, axis=1)
        o_ref[...] = _pack_bf16_pairs(_dot(act, wd_bf[...]) + bd_ref[...])


def _experts(buf, tile_expert, n_used, tile_valid, layer, w_gu, b_gu, w_down, b_down):
    bgu = b_gu.reshape(N_EXPERTS, D_FF // LANE, LANE, 2).transpose(0, 1, 3, 2).reshape(N_EXPERTS, 1, 2 * D_FF)

    def row_idx(i, te, nu, tv):
        return (jnp.minimum(i, nu[0] - 1), 0)

    def w_idx(i, te, nu, tv):
        return (layer, te[i], 0, 0)

    def b_idx(i, te, nu, tv):
        return (te[i], 0, 0)

    grid_spec = pltpu.PrefetchScalarGridSpec(
        num_scalar_prefetch=3,
        grid=(N_TILES,),
        in_specs=[
            pl.BlockSpec((TM, D_MODEL // 2), row_idx),
            pl.BlockSpec((None, None, D_MODEL, 2 * D_FF), w_idx),
            pl.BlockSpec((None, 1, 2 * D_FF), b_idx),
            pl.BlockSpec((None, None, D_FF, D_MODEL), w_idx),
            pl.BlockSpec((None, 1, D_MODEL), b_idx),
            pl.BlockSpec((2 * LANE, 2 * LANE), lambda i, te, nu, tv: (0, 0)),
        ],
        out_specs=pl.BlockSpec((TM, D_MODEL // 2), row_idx),
        scratch_shapes=[pltpu.VMEM((D_MODEL, 2 * D_FF), BF16), pltpu.VMEM((D_FF, D_MODEL), BF16)],
    )
    return pl.pallas_call(
        _expert_kernel,
        grid_spec=grid_spec,
        out_shape=jax.ShapeDtypeStruct((N_SLOTS, D_MODEL // 2), jnp.uint32),
        compiler_params=_cparams(("arbitrary",)),
        name="moe_experts",
    )(tile_expert, n_used, tile_valid, buf, w_gu, bgu, w_down, b_down.reshape(N_EXPERTS, 1, D_MODEL),
      _deinterleave_matrix())


def _pack_bf16_pairs(v):
    half = v.shape[1] // 2
    bits = pltpu.bitcast(v.astype(BF16).astype(F32), jnp.uint32)
    return (bits[:, half:] & jnp.uint32(0xFFFF0000)) | (bits[:, :half] >> 16)


def _unpack_bf16_pairs(w):
    return jnp.concatenate([pltpu.bitcast(w << 16, F32), pltpu.bitcast(w & jnp.uint32(0xFFFF0000), F32)],
                           axis=1)


def _combine_kernel(x_ref, mod_ref, y_ref, w_ref, o_ref):
    w = w_ref[...]
    y = _unpack_bf16_pairs(y_ref[0]) * w[:, 0:1]
    for k in range(1, TOP_K):
        y = y + _unpack_bf16_pairs(y_ref[k]) * w[:, k:k + 1]
    o_ref[...] = x_ref[...] + mod_ref[0][5:6] * y


def _combine(x, mod, y4, w4):
    return pl.pallas_call(
        _combine_kernel,
        grid=(N_TOK // TB,),
        in_specs=[
            pl.BlockSpec((TB, D_MODEL), lambda i: (i, 0)),
            _mod_spec(TB),
            pl.BlockSpec((TOP_K, TB, D_MODEL // 2), lambda i: (0, i, 0)),
            pl.BlockSpec((TB, TOP_K), lambda i: (i, 0)),
        ],
        out_specs=pl.BlockSpec((TB, D_MODEL), lambda i: (i, 0)),
        out_shape=jax.ShapeDtypeStruct((N_TOK, D_MODEL), F32),
        compiler_params=_cparams(("parallel",)),
        name="moe_combine",
    )(x, mod, y4, w4)


def _moe_layer(x, mod, g, layer, w_router, b_router, w_gu, b_gu, w_down, b_down):
    hp, meta, cnt = _route(x, mod, g, w_router, b_router)
    counts = cnt[0, :N_EXPERTS].astype(jnp.int32)
    tiles = (counts + TM - 1) // TM
    tile_end = jnp.cumsum(tiles)
    start = (tile_end - tiles) * TM
    n_used = tile_end[-1:].astype(jnp.int32)
    tile_ids = jnp.minimum(jnp.arange(N_TILES), n_used[0] - 1)
    tile_expert = jnp.sum(tile_end[None, :] <= tile_ids[:, None], axis=1).astype(jnp.int32)
    first_tile = tile_end - tiles
    tile_valid = jnp.clip(counts[tile_expert] - (tile_ids - first_tile[tile_expert]) * TM, 0, TM).astype(jnp.int32)
    e4 = meta[:, 0:TOP_K].astype(jnp.int32)
    w4 = meta[:, TOP_K:2 * TOP_K]
    dest4 = start[e4] + meta[:, 2 * TOP_K:3 * TOP_K].astype(jnp.int32)
    buf = _sc_dispatch(hp, dest4, 64)
    yb = _experts(buf, tile_expert, n_used, tile_valid, layer, w_gu, b_gu, w_down, b_down)
    y4 = _sc_gather(yb, dest4.T.reshape(-1), 64).reshape(TOP_K, N_TOK, D_MODEL // 2)
    return _combine(x, mod, y4, w4)


def kernel(x_prompt, x_sample, c, cache_ckv, cache_krope, c_ctx, norm_mix_g, norm_ffn_g, w_mod, b_mod,
           g_w_in, g_b_in, g_norm_v, g_w_s, g_b_s, g_w_out, m_w_dq, m_norm_q, m_w_uq, m_w_dkv,
           m_norm_kv, m_w_ukv, m_qk_norm_q, m_qk_norm_k, m_w_o, e_w_router, e_b_router, e_w_gu,
           e_b_gu, e_w_down, e_b_down):
    x = jnp.concatenate([x_prompt.reshape(N_PROMPT, D_MODEL), x_sample.reshape(N_SAMPLE, D_MODEL)], axis=0)
    cond = jnp.concatenate([c_ctx[None, :], c, jnp.zeros((SUBLANE - N_COND, D_MODEL), F32)], axis=0)
    mod = _modulation(cond, w_mod, b_mod)
    tab = _rope_tables()
    ckv_states, krope_states = [], []
    for layer in range(DEPTH):
        j = layer // 2
        if layer % 2 == 0:
            x = _gmlp_layer(x, mod[layer], norm_mix_g[layer], g_w_in[j], g_b_in[j], g_norm_v[j],
                            g_w_s[j], g_b_s[j], g_w_out[j])
        else:
            x, s_ckv, s_krope = _mla_layer(
                x, mod[layer], norm_mix_g[layer], cache_ckv[:, j], cache_krope[:, j], tab,
                m_w_dq[j], m_norm_q[j], m_w_uq[j], m_w_dkv[j], m_norm_kv[j], m_w_ukv[j],
                m_qk_norm_q[j], m_qk_norm_k[j], m_w_o[j])
            ckv_states.append(s_ckv)
            krope_states.append(s_krope)
        x = _moe_layer(x, mod[layer], norm_ffn_g[layer], layer, e_w_router[layer], e_b_router[layer],
                       e_w_gu, e_b_gu[layer], e_w_down, e_b_down[layer])
    y_prompt = x[:N_PROMPT].reshape(BATCH, SEQ, D_MODEL)
    y_sample = x[N_PROMPT:].reshape(DEC_BATCH, DEC_SEQ, D_MODEL)
    return (y_prompt, y_sample, jnp.stack(ckv_states, axis=1), jnp.stack(krope_states, axis=1))
```

```python
import functools
import math

import jax
import jax.numpy as jnp
from jax import lax
from jax.experimental import pallas as pl
from jax.experimental.pallas import tpu as pltpu
from jax.experimental.pallas import tpu_sc as plsc

F32 = jnp.float32
BF16 = jnp.bfloat16

D_MODEL = 1024
BATCH = 32
SEQ = 256
DEPTH = 4
DEC_BATCH = 2
DEC_SEQ = 4096
PAST_LEN = 512
GRID_W = 64
RMS_EPS = 1e-6
GMLP_WIDTH = 2 * D_MODEL
GMLP_GROUPS = 8
GROUP_W = GMLP_WIDTH // GMLP_GROUPS
CHUNK = 128
N_HEADS = 16
QK_NOPE_DIM = 64
QK_ROPE_DIM = 32
QK_HEAD_DIM = QK_NOPE_DIM + QK_ROPE_DIM
V_HEAD_DIM = 64
Q_RANK = 256
KV_RANK = 128
ROPE_THETA = 10000.0
N_EXPERTS = 32
TOP_K = 4
D_FF = D_MODEL
SWIGLU_LIMIT = 7.0
SWIGLU_ALPHA = 1.702

N_PROMPT = BATCH * SEQ
N_SAMPLE = DEC_BATCH * DEC_SEQ
N_TOK = N_PROMPT + N_SAMPLE
N_COND = 1 + DEC_BATCH
KV_LEN = PAST_LEN + DEC_SEQ
N_KV_ROWS = DEC_BATCH * KV_LEN + N_PROMPT

LANE = 128
SUBLANE = 8
HEAD_PAD = LANE
QKV_W = N_HEADS * HEAD_PAD
VMEM_LIMIT = 56 * 1024 * 1024

TB = 256
TQ = 256
TKC = 256
TM = 512
N_TILES = N_TOK * TOP_K // TM + N_EXPERTS
N_SLOTS = N_TILES * TM
SC_CORES = 2
SC_WORKERS = SC_CORES * 16
ROPE_BLOCKS = DEC_SEQ // TB
NEG_BIG = -1e30


def _cparams(sem):
    return pltpu.CompilerParams(dimension_semantics=sem, vmem_limit_bytes=VMEM_LIMIT)


def _cond_of_block(i, tb):
    n_p = N_PROMPT // tb
    per = DEC_SEQ // tb
    return jnp.where(i < n_p, 0, 1 + (i - n_p) // per)


def _rms(x, g, n=None):
    n = x.shape[-1] if n is None else n
    ss = jnp.sum(x * x, axis=-1, keepdims=True) * (1.0 / n)
    return x * lax.rsqrt(ss + RMS_EPS) * g


def _dot(a, b):
    return jnp.dot(a, b, preferred_element_type=F32)


def _mod_kernel(c_ref, w_ref, b_ref, o_ref):
    c = c_ref[...]
    s = c * jax.nn.sigmoid(c)
    o_ref[0] = _dot(s.astype(BF16), w_ref[0].astype(BF16)) + b_ref[0]


def _modulation(cond, w_mod, b_mod):
    tn = 1536
    out = pl.pallas_call(
        _mod_kernel,
        grid=(DEPTH, 6 * D_MODEL // tn),
        in_specs=[
            pl.BlockSpec((SUBLANE, D_MODEL), lambda l, j: (0, 0)),
            pl.BlockSpec((1, D_MODEL, tn), lambda l, j: (l, 0, j)),
            pl.BlockSpec((1, 1, tn), lambda l, j: (l, 0, j)),
        ],
        out_specs=pl.BlockSpec((1, SUBLANE, tn), lambda l, j: (l, 0, j)),
        out_shape=jax.ShapeDtypeStruct((DEPTH, SUBLANE, 6 * D_MODEL), F32),
        compiler_params=_cparams(("parallel", "parallel")),
        name="adaln_mod",
    )(cond, w_mod, b_mod.reshape(DEPTH, 1, 6 * D_MODEL))
    m = out[:, :N_COND].reshape(DEPTH, N_COND, 6, D_MODEL)
    return jnp.pad(m, ((0, 0), (0, 0), (0, SUBLANE - 6), (0, 0)))


def _mod_spec(tb):
    return pl.BlockSpec((1, SUBLANE, D_MODEL), lambda i: (_cond_of_block(i, tb), 0, 0))


def _const_spec(shape):
    nd = len(shape)
    return pl.BlockSpec(shape, lambda i: (0,) * nd)


def _gmlp_kernel(x_ref, mod_ref, g_ref, win_ref, bin_ref, gv_ref, ws_ref, bst_ref, wout_ref, o_ref):
    x = x_ref[...]
    m = mod_ref[0]
    h = _rms(x, g_ref[...]) * (1.0 + m[1:2]) + m[0:1]
    hb = h.astype(BF16)
    zv = jax.nn.gelu(_dot(hb, win_ref[:, GMLP_WIDTH:]) + bin_ref[:, GMLP_WIDTH:], approximate=True)
    vn = _rms(zv, gv_ref[...]).astype(BF16)
    rows = []
    for c in range(TB // CHUNK):
        cols = []
        for g in range(GMLP_GROUPS):
            blk = vn[c * CHUNK:(c + 1) * CHUNK, g * GROUP_W:(g + 1) * GROUP_W]
            cols.append(_dot(ws_ref[g], blk) + bst_ref[:, g:g + 1])
        rows.append(jnp.concatenate(cols, axis=1))
    vm = jnp.concatenate(rows, axis=0)
    u = jax.nn.gelu(_dot(hb, win_ref[:, :GMLP_WIDTH]) + bin_ref[:, :GMLP_WIDTH], approximate=True)
    d = _dot((u * vm).astype(BF16), wout_ref[...])
    o_ref[...] = x + m[2:3] * d


def _gmlp_layer(x, mod, g, w_in, b_in, g_v, w_s, b_s, w_out):
    return pl.pallas_call(
        _gmlp_kernel,
        grid=(N_TOK // TB,),
        in_specs=[
            pl.BlockSpec((TB, D_MODEL), lambda i: (i, 0)),
            _mod_spec(TB),
            _const_spec((1, D_MODEL)),
            _const_spec((D_MODEL, 2 * GMLP_WIDTH)),
            _const_spec((1, 2 * GMLP_WIDTH)),
            _const_spec((1, GMLP_WIDTH)),
            _const_spec((GMLP_GROUPS, CHUNK, CHUNK)),
            _const_spec((CHUNK, GMLP_GROUPS)),
            _const_spec((GMLP_WIDTH, D_MODEL)),
        ],
        out_specs=pl.BlockSpec((TB, D_MODEL), lambda i: (i, 0)),
        out_shape=jax.ShapeDtypeStruct((N_TOK, D_MODEL), F32),
        compiler_params=_cparams(("parallel",)),
        name="gmlp_mixer",
    )(x, mod, g.reshape(1, -1), w_in.astype(BF16), b_in.reshape(1, -1), g_v.reshape(1, -1),
      w_s.astype(BF16), b_s.T, w_out.astype(BF16))


def _rope_tables():
    t = jnp.arange(DEC_SEQ)
    row_id = (t // GRID_W).astype(F32)
    col_id = (t % GRID_W).astype(F32)
    axis_dim = QK_ROPE_DIM // 2
    inv_freq = ROPE_THETA ** (-jnp.arange(0, axis_dim, 2, dtype=F32) / axis_dim)
    ang = jnp.stack([row_id[:, None] * inv_freq, col_id[:, None] * inv_freq], axis=1)
    cos, sin = jnp.cos(ang), jnp.sin(ang)
    zeros = jnp.zeros_like(sin)
    cos_l = jnp.concatenate([cos, cos], axis=-1).reshape(DEC_SEQ, QK_ROPE_DIM)
    s1_l = jnp.concatenate([-sin, zeros], axis=-1).reshape(DEC_SEQ, QK_ROPE_DIM)
    s2_l = jnp.concatenate([zeros, sin], axis=-1).reshape(DEC_SEQ, QK_ROPE_DIM)

    def widen(rope_part, nope_fill):
        left = jnp.full((DEC_SEQ, QK_NOPE_DIM), nope_fill, F32)
        right = jnp.zeros((DEC_SEQ, HEAD_PAD - QK_HEAD_DIM), F32)
        return jnp.concatenate([left, rope_part, right], axis=-1)

    pos = jnp.stack([widen(cos_l, 1.0), widen(s1_l, 0.0), widen(s2_l, 0.0)])
    ident_c = jnp.concatenate([jnp.ones((TB, QK_HEAD_DIM), F32),
                               jnp.zeros((TB, HEAD_PAD - QK_HEAD_DIM), F32)], axis=-1)
    ident = jnp.stack([ident_c, jnp.zeros_like(ident_c), jnp.zeros_like(ident_c)])
    return jnp.concatenate([pos, ident], axis=1)


def _rope(xn, tab_ref):
    half = QK_ROPE_DIM // 4
    return (xn * tab_ref[0] + pltpu.roll(xn, HEAD_PAD - half, 1) * tab_ref[1]
            + pltpu.roll(xn, half, 1) * tab_ref[2])


def _dot_nt(a, b):
    return lax.dot_general(a, b, (((1,), (1,)), ((), ())), preferred_element_type=F32)


def _shift_rows(x, n):
    n = n % x.shape[0]
    return jnp.concatenate([x[n:], x[:n]], axis=0)


def _mla_proj_kernel(x_ref, mod_ref, g_ref, wdq_ref, nq_ref, wuqt_ref, gq_ref, wdkv_ref, nkv_ref,
                     tabt_ref, qt_ref, ckv_ref, krp_ref):
    x = x_ref[...]
    m = mod_ref[0]
    h = _rms(x, g_ref[...]) * (1.0 + m[1:2]) + m[0:1]
    hb = h.astype(BF16)
    cq = _rms(_dot(hb, wdq_ref[...]), nq_ref[...])
    qt = _dot_nt(wuqt_ref[...], cq.astype(BF16))
    gq = gq_ref[...]
    half = QK_ROPE_DIM // 4
    for hd in range(N_HEADS):
        qh = qt[hd * HEAD_PAD:(hd + 1) * HEAD_PAD, :]
        ss = jnp.sum(qh * qh, axis=0, keepdims=True) * (1.0 / QK_HEAD_DIM)
        qn = qh * lax.rsqrt(ss + RMS_EPS) * gq
        qr = qn * tabt_ref[0] + _shift_rows(qn, half) * tabt_ref[1] + _shift_rows(qn, -half) * tabt_ref[2]
        qt_ref[hd * HEAD_PAD:(hd + 1) * HEAD_PAD, :] = qr.astype(BF16)
    kva = _dot(hb, wdkv_ref[...])
    ckv_ref[...] = _rms(kva[:, :KV_RANK], nkv_ref[...])
    krp_ref[...] = kva[:, KV_RANK:]


def _pad_heads(w, per_head):
    k = w.shape[0]
    w = w.reshape(k, N_HEADS, per_head)
    return jnp.pad(w, ((0, 0), (0, 0), (0, HEAD_PAD - per_head))).reshape(k, QKV_W)


def _pad_gain(g):
    return jnp.pad(g, (0, HEAD_PAD - QK_HEAD_DIM)).reshape(1, HEAD_PAD)


def _mla_proj(x, mod, g, w_dq, norm_q, w_uq, gq, w_dkv, norm_kv, tab_t):
    wdkv = jnp.concatenate([w_dkv[:, :KV_RANK], jnp.zeros((D_MODEL, QK_NOPE_DIM), F32),
                            w_dkv[:, KV_RANK:], jnp.zeros((D_MODEL, HEAD_PAD - QK_HEAD_DIM), F32)], axis=1)
    n_pb = N_PROMPT // TB

    def tab_idx(i):
        return (0, 0, jnp.where(i < n_pb, ROPE_BLOCKS, (i - n_pb) % ROPE_BLOCKS))

    return pl.pallas_call(
        _mla_proj_kernel,
        grid=(N_TOK // TB,),
        in_specs=[
            pl.BlockSpec((TB, D_MODEL), lambda i: (i, 0)),
            _mod_spec(TB),
            _const_spec((1, D_MODEL)),
            _const_spec((D_MODEL, Q_RANK)),
            _const_spec((1, Q_RANK)),
            _const_spec((QKV_W, Q_RANK)),
            _const_spec((HEAD_PAD, 1)),
            _const_spec((D_MODEL, 2 * LANE)),
            _const_spec((1, KV_RANK)),
            pl.BlockSpec((3, HEAD_PAD, TB), tab_idx),
        ],
        out_specs=[
            pl.BlockSpec((QKV_W, TB), lambda i: (0, i)),
            pl.BlockSpec((TB, KV_RANK), lambda i: (i, 0)),
            pl.BlockSpec((TB, HEAD_PAD), lambda i: (i, 0)),
        ],
        out_shape=[
            jax.ShapeDtypeStruct((QKV_W, N_TOK), BF16),
            jax.ShapeDtypeStruct((N_TOK, KV_RANK), F32),
            jax.ShapeDtypeStruct((N_TOK, HEAD_PAD), F32),
        ],
        compiler_params=_cparams(("parallel",)),
        name="mla_proj",
    )(x, mod, g.reshape(1, -1), w_dq.astype(BF16), norm_q.reshape(1, -1),
      _pad_heads(w_uq, QK_HEAD_DIM).T.astype(BF16), _pad_gain(gq).reshape(HEAD_PAD, 1), wdkv.astype(BF16),
      norm_kv.reshape(1, -1), tab_t)


def _kv_expand_kernel(ckv_ref, krp_ref, wukv_ref, wukvt_ref, gk_ref, tab_ref, k_ref, vt_ref):
    ckv = ckv_ref[...].astype(BF16)
    kv = _dot(ckv, wukv_ref[...])
    kvt = _dot_nt(wukvt_ref[...], ckv)
    krp = krp_ref[...]
    gk = gk_ref[...]
    lane = lax.broadcasted_iota(jnp.int32, (TB, HEAD_PAD), 1)
    is_nope = lane < QK_NOPE_DIM
    row = lax.broadcasted_iota(jnp.int32, (HEAD_PAD, TB), 0)
    ones_row = jnp.where(row == 0, 1.0, 0.0)
    for hd in range(N_HEADS):
        kvh = kv[:, hd * HEAD_PAD:(hd + 1) * HEAD_PAD]
        kn = _rms(jnp.where(is_nope, kvh, krp), gk, QK_HEAD_DIM)
        k_ref[:, hd * HEAD_PAD:(hd + 1) * HEAD_PAD] = _rope(kn, tab_ref).astype(BF16)
        vth = kvt[hd * HEAD_PAD:(hd + 1) * HEAD_PAD, :]
        vt_ref[hd * HEAD_PAD:(hd + 1) * HEAD_PAD, :] = jnp.where(row < QK_NOPE_DIM, ones_row, vth).astype(BF16)


def _kv_expand(ckv_all, krp_all, w_ukv, gk, tab):
    n_sb = DEC_BATCH * KV_LEN // TB
    per = KV_LEN // TB
    n_cache = PAST_LEN // TB

    def tab_idx(i):
        j = i % per
        blk = jnp.where((i >= n_sb) | (j < n_cache), ROPE_BLOCKS, j - n_cache)
        return (0, blk, 0)

    return pl.pallas_call(
        _kv_expand_kernel,
        grid=(N_KV_ROWS // TB,),
        in_specs=[
            pl.BlockSpec((TB, KV_RANK), lambda i: (i, 0)),
            pl.BlockSpec((TB, HEAD_PAD), lambda i: (i, 0)),
            _const_spec((KV_RANK, QKV_W)),
            _const_spec((QKV_W, KV_RANK)),
            _const_spec((1, HEAD_PAD)),
            pl.BlockSpec((3, TB, HEAD_PAD), tab_idx),
        ],
        out_specs=[
            pl.BlockSpec((TB, QKV_W), lambda i: (i, 0)),
            pl.BlockSpec((QKV_W, TB), lambda i: (0, i)),
        ],
        out_shape=[
            jax.ShapeDtypeStruct((N_KV_ROWS, QKV_W), BF16),
            jax.ShapeDtypeStruct((QKV_W, N_KV_ROWS), BF16),
        ],
        compiler_params=_cparams(("parallel",)),
        name="mla_kv_expand",
    )(ckv_all, krp_all, w_ukv.astype(BF16), w_ukv.T.astype(BF16), _pad_gain(gk), tab)


def _attn_kernel(qt_ref, k_ref, vt_ref, o_ref, s_a, s_b, *, t_k, hps):
    c = (1.0 / math.sqrt(QK_HEAD_DIM)) * math.log2(math.e)
    kc = min(TKC, t_k)
    n_chunks = t_k // kc
    bufs = (s_a, s_b)

    def rows(h):
        return slice(h * HEAD_PAD, (h + 1) * HEAD_PAD)

    def scores(h, j, m8):
        st = _dot(k_ref[j * kc:(j + 1) * kc, rows(h)], qt_ref[rows(h), :])
        bufs[h % 2][j * kc:(j + 1) * kc, :] = st
        return jnp.maximum(m8, jnp.max(st.reshape(kc // SUBLANE, SUBLANE, TQ), axis=0))

    def weigh(h, j, m, acc):
        pt = jnp.exp2((bufs[h % 2][j * kc:(j + 1) * kc, :] - m) * c).astype(BF16)
        return acc + _dot(vt_ref[rows(h), j * kc:(j + 1) * kc], pt)

    m8_init = jnp.full((SUBLANE, TQ), NEG_BIG, F32)
    outs = []
    m8 = m8_init
    for j in range(n_chunks):
        m8 = scores(0, j, m8)
    for h in range(1, hps + 1):
        m = jnp.max(m8, axis=0, keepdims=True)
        acc = jnp.zeros((HEAD_PAD, TQ), F32)
        m8 = m8_init
        for j in range(n_chunks):
            if h < hps:
                m8 = scores(h, j, m8)
            acc = weigh(h - 1, j, m, acc)
        outs.append(acc[QK_NOPE_DIM:, :] / acc[0:1, :])
    o_ref[...] = jnp.concatenate(outs, axis=0).T.astype(BF16)


def _attention(qt, k, vt, *, n_batch, t_q, t_k, q_row0, kv_row0, hps):
    nq = t_q // TQ
    q0 = q_row0 // TQ
    k0 = kv_row0 // t_k
    return pl.pallas_call(
        functools.partial(_attn_kernel, t_k=t_k, hps=hps),
        grid=(n_batch, N_HEADS // hps, nq),
        in_specs=[
            pl.BlockSpec((hps * HEAD_PAD, TQ), lambda b, h, i: (h, q0 + b * nq + i)),
            pl.BlockSpec((t_k, hps * HEAD_PAD), lambda b, h, i: (k0 + b, h)),
            pl.BlockSpec((hps * HEAD_PAD, t_k), lambda b, h, i: (h, k0 + b)),
        ],
        out_specs=pl.BlockSpec((TQ, hps * V_HEAD_DIM), lambda b, h, i: (b * nq + i, h)),
        out_shape=jax.ShapeDtypeStruct((n_batch * t_q, N_HEADS * V_HEAD_DIM), BF16),
        scratch_shapes=[pltpu.VMEM((t_k, TQ), F32), pltpu.VMEM((t_k, TQ), F32)],
        compiler_params=_cparams(("parallel", "parallel", "parallel")),
        name=f"mla_attention_tk{t_k}",
    )(qt, k, vt)


def _out_proj_kernel(ap_ref, as_ref, x_ref, mod_ref, wo_ref, o_ref):
    a = jnp.where(pl.program_id(0) < N_PROMPT // TB, ap_ref[...], as_ref[...])
    o_ref[...] = x_ref[...] + mod_ref[0][2:3] * _dot(a, wo_ref[...])


def _out_proj(attn_p, attn_s, x, mod, w_o):
    n_pb = N_PROMPT // TB
    return pl.pallas_call(
        _out_proj_kernel,
        grid=(N_TOK // TB,),
        in_specs=[
            pl.BlockSpec((TB, D_MODEL), lambda i: (jnp.minimum(i, n_pb - 1), 0)),
            pl.BlockSpec((TB, D_MODEL), lambda i: (jnp.maximum(i - n_pb, 0), 0)),
            pl.BlockSpec((TB, D_MODEL), lambda i: (i, 0)),
            _mod_spec(TB),
            _const_spec((D_MODEL, D_MODEL)),
        ],
        out_specs=pl.BlockSpec((TB, D_MODEL), lambda i: (i, 0)),
        out_shape=jax.ShapeDtypeStruct((N_TOK, D_MODEL), F32),
        compiler_params=_cparams(("parallel",)),
        name="mla_out_proj",
    )(attn_p, attn_s, x, mod, w_o.astype(BF16))


def _mla_layer(x, mod, g, cache_ckv, cache_krope, tab, w_dq, norm_q, w_uq, w_dkv, norm_kv, w_ukv,
               gq, gk, w_o):
    tab_t = jnp.swapaxes(tab, 1, 2)
    qt, ckv, krp = _mla_proj(x, mod, g, w_dq, norm_q, w_uq, gq, w_dkv, norm_kv, tab_t)
    cache_krp = jnp.pad(cache_krope, ((0, 0), (0, 0), (QK_NOPE_DIM, HEAD_PAD - QK_HEAD_DIM)))
    ckv_s = ckv[N_PROMPT:].reshape(DEC_BATCH, DEC_SEQ, KV_RANK)
    krp_s = krp[N_PROMPT:].reshape(DEC_BATCH, DEC_SEQ, HEAD_PAD)
    ckv_all = jnp.concatenate([jnp.concatenate([cache_ckv, ckv_s], axis=1).reshape(-1, KV_RANK),
                               ckv[:N_PROMPT]], axis=0)
    krp_all = jnp.concatenate([jnp.concatenate([cache_krp, krp_s], axis=1).reshape(-1, HEAD_PAD),
                               krp[:N_PROMPT]], axis=0)
    k, vt = _kv_expand(ckv_all, krp_all, w_ukv, gk, tab)
    a_p = _attention(qt, k, vt, n_batch=BATCH, t_q=SEQ, t_k=SEQ, q_row0=0, kv_row0=DEC_BATCH * KV_LEN,
                     hps=N_HEADS)
    a_s = _attention(qt, k, vt, n_batch=DEC_BATCH, t_q=DEC_SEQ, t_k=KV_LEN, q_row0=N_PROMPT, kv_row0=0,
                     hps=4)
    x = _out_proj(a_p, a_s, x, mod, w_o)
    state_ckv = ckv[:N_PROMPT].reshape(BATCH, SEQ, KV_RANK)
    state_krope = krp[:N_PROMPT, QK_NOPE_DIM:QK_HEAD_DIM].reshape(BATCH, SEQ, QK_ROPE_DIM)
    return x, state_ckv, state_krope


def _route_kernel(x_ref, mod_ref, g_ref, wr_ref, br_ref, tri_ref, h_ref, meta_ref, metat_ref, cnt_ref, carry):
    i = pl.program_id(0)

    @pl.when(i == 0)
    def _():
        carry[...] = jnp.zeros_like(carry)

    x = x_ref[...]
    m = mod_ref[0]
    h = _rms(x, g_ref[...]) * (1.0 + m[4:5]) + m[3:4]
    h_ref[...] = _pack_bf16_pairs(h)
    logits = _dot(h.astype(BF16), wr_ref[...]) + br_ref[...]
    lane = lax.broadcasted_iota(jnp.int32, logits.shape, 1).astype(F32)
    work = logits
    sel = jnp.zeros(logits.shape, F32)
    hits, tops = [], []
    for k in range(TOP_K):
        mk = jnp.max(work, axis=-1, keepdims=True)
        first = jnp.min(jnp.where(work == mk, lane, float(LANE)), axis=-1, keepdims=True)
        hit = lane == first
        sel = jnp.where(hit, 1.0, sel)
        work = jnp.where(hit, -jnp.inf, work)
        hits.append((hit, first))
        tops.append(mk)
    es = [jnp.exp(t - tops[0]) for t in tops]
    denom = es[0] + es[1] + es[2] + es[3]
    pos = _dot(tri_ref[...], sel.astype(BF16)) + carry[0:1, :]
    carry[...] = carry[...] + jnp.sum(sel, axis=0, keepdims=True)
    cnt_ref[...] = carry[...]
    meta = jnp.zeros(logits.shape, F32)
    for k in range(TOP_K):
        hit, first = hits[k]
        pk = jnp.sum(jnp.where(hit, pos, 0.0), axis=-1, keepdims=True)
        meta = jnp.where(lane == float(k), first, meta)
        meta = jnp.where(lane == float(TOP_K + k), es[k] / denom, meta)
        meta = jnp.where(lane == float(2 * TOP_K + k), pk, meta)
    meta_ref[...] = meta
    metat_ref[...] = meta.T[:2 * SUBLANE, :]


def _route(x, mod, g, w_router, b_router):
    wr = jnp.pad(w_router, ((0, 0), (0, LANE - N_EXPERTS))).astype(BF16)
    br = jnp.pad(b_router, (0, LANE - N_EXPERTS), constant_values=NEG_BIG).reshape(1, LANE)
    tri = jnp.tri(TB, TB, -1, dtype=BF16)
    return pl.pallas_call(
        _route_kernel,
        grid=(N_TOK // TB,),
        in_specs=[
            pl.BlockSpec((TB, D_MODEL), lambda i: (i, 0)),
            _mod_spec(TB),
            _const_spec((1, D_MODEL)),
            _const_spec((D_MODEL, LANE)),
            _const_spec((1, LANE)),
            _const_spec((TB, TB)),
        ],
        out_specs=[
            pl.BlockSpec((TB, D_MODEL // 2), lambda i: (i, 0)),
            pl.BlockSpec((TB, LANE), lambda i: (i, 0)),
            pl.BlockSpec((2 * SUBLANE, TB), lambda i: (0, i)),
            _const_spec((SUBLANE, LANE)),
        ],
        out_shape=[
            jax.ShapeDtypeStruct((N_TOK, D_MODEL // 2), jnp.uint32),
            jax.ShapeDtypeStruct((N_TOK, LANE), F32),
            jax.ShapeDtypeStruct((2 * SUBLANE, N_TOK), F32),
            jax.ShapeDtypeStruct((SUBLANE, LANE), F32),
        ],
        scratch_shapes=[pltpu.VMEM((SUBLANE, LANE), F32)],
        compiler_params=_cparams(("arbitrary",)),
        name="moe_route",
    )(x, mod, g.reshape(1, -1), wr, br, tri)


def _sc_gather(table, idx, ch):
    b, w = idx.shape[0], table.shape[1]
    per_w = b // SC_WORKERS
    n_ch = per_w // ch
    assert per_w * SC_WORKERS == b and n_ch * ch == per_w and n_ch % 2 == 0
    mesh = plsc.VectorSubcoreMesh(core_axis_name="c", subcore_axis_name="s")

    @functools.partial(
        pl.kernel, mesh=mesh,
        out_type=jax.ShapeDtypeStruct((b, w), table.dtype),
        scratch_types=[
            pltpu.VMEM((n_ch, ch), jnp.int32),
            pltpu.VMEM((ch, w), table.dtype),
            pltpu.VMEM((ch, w), table.dtype),
            pltpu.SemaphoreType.DMA, pltpu.SemaphoreType.DMA,
            pltpu.SemaphoreType.DMA, pltpu.SemaphoreType.DMA,
        ],
        name="sc_row_gather",
    )
    def gather_rows(table_hbm, idx_hbm, out_hbm, idx_v, buf0, buf1, g0, g1, s0, s1):
        wid = lax.axis_index("s") * SC_CORES + lax.axis_index("c")
        base = wid * per_w
        pltpu.sync_copy(idx_hbm.at[wid], idx_v)

        def gather(j, buf, sem):
            return pltpu.make_async_copy(table_hbm.at[idx_v.at[j]], buf, sem)

        def store(j, buf, sem):
            return pltpu.make_async_copy(buf, out_hbm.at[pl.ds(base + j * ch, ch)], sem)

        gather(0, buf0, g0).start()

        @pl.loop(0, n_ch, step=2)
        def _(j):
            @pl.when(j > 0)
            def _():
                store(j - 1, buf1, s1).wait()

            gather(j + 1, buf1, g1).start()
            gather(j, buf0, g0).wait()
            store(j, buf0, s0).start()
            gather(j + 1, buf1, g1).wait()
            store(j + 1, buf1, s1).start()
            store(j, buf0, s0).wait()

            @pl.when(j + 2 < n_ch)
            def _():
                gather(j + 2, buf0, g0).start()

        store(n_ch - 1, buf1, s1).wait()

    return gather_rows(table, idx.reshape(SC_WORKERS, n_ch, ch))


def _sc_dispatch(rows, dest_t, ch):
    n, w = rows.shape
    per_w = n // SC_WORKERS
    n_ch = per_w // ch
    assert per_w * SC_WORKERS == n and n_ch * ch == per_w and n_ch % 2 == 0
    mesh = plsc.VectorSubcoreMesh(core_axis_name="c", subcore_axis_name="s")
    idx = dest_t.reshape(TOP_K, SC_WORKERS, n_ch, ch).transpose(1, 0, 2, 3).reshape(SC_WORKERS, TOP_K * n_ch, ch)

    @functools.partial(
        pl.kernel, mesh=mesh,
        out_type=jax.ShapeDtypeStruct((N_SLOTS, w), rows.dtype),
        scratch_types=[
            pltpu.VMEM((TOP_K * n_ch, ch), jnp.int32),
            pltpu.VMEM((ch, w), rows.dtype),
            pltpu.VMEM((ch, w), rows.dtype),
            pltpu.SemaphoreType.DMA, pltpu.SemaphoreType.DMA,
            pltpu.SemaphoreType.DMA, pltpu.SemaphoreType.DMA,
        ],
        name="sc_row_dispatch",
    )
    def dispatch_rows(rows_hbm, idx_hbm, out_hbm, idx_v, buf0, buf1, l0, l1, s0, s1):
        wid = lax.axis_index("s") * SC_CORES + lax.axis_index("c")
        base = wid * per_w
        pltpu.sync_copy(idx_hbm.at[wid], idx_v)

        def load(j, buf, sem):
            return pltpu.make_async_copy(rows_hbm.at[pl.ds(base + j * ch, ch)], buf, sem)

        def scatter(j, k, buf, sem):
            return pltpu.make_async_copy(buf, out_hbm.at[idx_v.at[k * n_ch + j]], sem)

        load(0, buf0, l0).start()

        @pl.loop(0, n_ch, step=2)
        def _(j):
            load(j + 1, buf1, l1).start()
            load(j, buf0, l0).wait()
            for k in range(TOP_K):
                scatter(j, k, buf0, s0).start()
            load(j + 1, buf1, l1).wait()
            for k in range(TOP_K):
                scatter(j + 1, k, buf1, s1).start()
            for k in range(TOP_K):
                scatter(j, k, buf0, s0).wait()

            @pl.when(j + 2 < n_ch)
            def _():
                load(j + 2, buf0, l0).start()

            for k in range(TOP_K):
                scatter(j + 1, k, buf1, s1).wait()

    return dispatch_rows(rows, idx)


def _deinterleave_matrix():
    src = jnp.arange(2 * LANE)[:, None]
    dst = jnp.arange(2 * LANE)[None, :]
    want = jnp.where(dst < LANE, 2 * dst, 2 * (dst - LANE) + 1)
    return (src == want).astype(BF16)


def _expert_kernel(te_ref, nu_ref, tv_ref, x_ref, wgu_ref, bgu_ref, wd_ref, bd_ref, perm_ref, o_ref,
                   wgu_bf, wd_bf):
    i = pl.program_id(0)
    prev = te_ref[jnp.maximum(i - 1, 0)]
    fresh = jnp.logical_or(i == 0, te_ref[i] != prev)

    @pl.when(jnp.logical_and(fresh, i < nu_ref[0]))
    def _():
        for b in range(2 * D_FF // (2 * LANE)):
            sl = slice(b * 2 * LANE, (b + 1) * 2 * LANE)
            wgu_bf[:, sl] = _dot(wgu_ref[:, sl].astype(BF16), perm_ref[...]).astype(BF16)
        wd_bf[...] = wd_ref[...].astype(BF16)

    @pl.when(i < nu_ref[0])
    def _():
        row = lax.broadcasted_iota(jnp.int32, (TM, D_MODEL // 2), 0)
        w = jnp.where(row < tv_ref[i], x_ref[...], jnp.uint32(0))
        x = _unpack_bf16_pairs(w).astype(BF16)
        gu = _dot(x, wgu_bf[...]) + bgu_ref[...]
        acts = []
        for b in range(D_FF // LANE):
            glu = jnp.minimum(gu[:, b * 2 * LANE:b * 2 * LANE + LANE], SWIGLU_LIMIT)
            lin = jnp.clip(gu[:, b * 2 * LANE + LANE:(b + 1) * 2 * LANE], -SWIGLU_LIMIT, SWIGLU_LIMIT)
            acts.append((glu * jax.nn.sigmoid(SWIGLU_ALPHA * glu) * (lin + 1.0)).astype(BF16))
        act = jnp.concatenate(acts, axis=1)
        o_ref[...] = _pack_bf16_pairs(_dot(act, wd_bf[...]) + bd_ref[...])


def _experts(buf, tile_expert, n_used, tile_valid, layer, w_gu, b_gu, w_down, b_down):
    bgu = b_gu.reshape(N_EXPERTS, D_FF // LANE, LANE, 2).transpose(0, 1, 3, 2).reshape(N_EXPERTS, 1, 2 * D_FF)

    def row_idx(i, te, nu, tv):
        return (jnp.minimum(i, nu[0] - 1), 0)

    def w_idx(i, te, nu, tv):
        return (layer, te[i], 0, 0)

    def b_idx(i, te, nu, tv):
        return (te[i], 0, 0)

    grid_spec = pltpu.PrefetchScalarGridSpec(
        num_scalar_prefetch=3,
        grid=(N_TILES,),
        in_specs=[
            pl.BlockSpec((TM, D_MODEL // 2), row_idx),
            pl.BlockSpec((None, None, D_MODEL, 2 * D_FF), w_idx),
            pl.BlockSpec((None, 1, 2 * D_FF), b_idx),
            pl.BlockSpec((None, None, D_FF, D_MODEL), w_idx),
            pl.BlockSpec((None, 1, D_MODEL), b_idx),
            pl.BlockSpec((2 * LANE, 2 * LANE), lambda i, te, nu, tv: (0, 0)),
        ],
        out_specs=pl.BlockSpec((TM, D_MODEL // 2), row_idx),
        scratch_shapes=[pltpu.VMEM((D_MODEL, 2 * D_FF), BF16), pltpu.VMEM((D_FF, D_MODEL), BF16)],
    )
    return pl.pallas_call(
        _expert_kernel,
        grid_spec=grid_spec,
        out_shape=jax.ShapeDtypeStruct((N_SLOTS, D_MODEL // 2), jnp.uint32),
        compiler_params=_cparams(("arbitrary",)),
        name="moe_experts",
    )(tile_expert, n_used, tile_valid, buf, w_gu, bgu, w_down, b_down.reshape(N_EXPERTS, 1, D_MODEL),
      _deinterleave_matrix())


def _pack_bf16_pairs(v):
    half = v.shape[1] // 2
    bits = pltpu.bitcast(v.astype(BF16).astype(F32), jnp.uint32)
    return (bits[:, half:] & jnp.uint32(0xFFFF0000)) | (bits[:, :half] >> 16)


def _unpack_bf16_pairs(w):
    return jnp.concatenate([pltpu.bitcast(w << 16, F32), pltpu.bitcast(w & jnp.uint32(0xFFFF0000), F32)],
                           axis=1)


def _combine_kernel(x_ref, mod_ref, y_ref, w_ref, o_ref):
    w = w_ref[:, TOP_K:2 * TOP_K]
    y = _unpack_bf16_pairs(y_ref[0]) * w[:, 0:1]
    for k in range(1, TOP_K):
        y = y + _unpack_bf16_pairs(y_ref[k]) * w[:, k:k + 1]
    o_ref[...] = x_ref[...] + mod_ref[0][5:6] * y


def _combine(x, mod, y4, meta):
    return pl.pallas_call(
        _combine_kernel,
        grid=(N_TOK // TB,),
        in_specs=[
            pl.BlockSpec((TB, D_MODEL), lambda i: (i, 0)),
            _mod_spec(TB),
            pl.BlockSpec((TOP_K, TB, D_MODEL // 2), lambda i: (0, i, 0)),
            pl.BlockSpec((TB, LANE), lambda i: (i, 0)),
        ],
        out_specs=pl.BlockSpec((TB, D_MODEL), lambda i: (i, 0)),
        out_shape=jax.ShapeDtypeStruct((N_TOK, D_MODEL), F32),
        compiler_params=_cparams(("parallel",)),
        name="moe_combine",
    )(x, mod, y4, meta)


def _moe_layer(x, mod, g, layer, w_router, b_router, w_gu, b_gu, w_down, b_down):
    hp, meta, meta_t, cnt = _route(x, mod, g, w_router, b_router)
    counts = cnt[0, :N_EXPERTS].astype(jnp.int32)
    tiles = (counts + TM - 1) // TM
    tile_end = jnp.cumsum(tiles)
    start = (tile_end - tiles) * TM
    n_used = tile_end[-1:].astype(jnp.int32)
    tile_ids = jnp.minimum(jnp.arange(N_TILES), n_used[0] - 1)
    tile_expert = jnp.sum(tile_end[None, :] <= tile_ids[:, None], axis=1).astype(jnp.int32)
    first_tile = tile_end - tiles
    tile_valid = jnp.clip(counts[tile_expert] - (tile_ids - first_tile[tile_expert]) * TM, 0, TM).astype(jnp.int32)
    e_t = meta_t[0:TOP_K].astype(jnp.int32)
    start_t = jnp.sum(jnp.where(e_t[None] == jnp.arange(N_EXPERTS)[:, None, None], start[:, None, None], 0), axis=0)
    dest_t = start_t + meta_t[2 * TOP_K:3 * TOP_K].astype(jnp.int32)
    buf = _sc_dispatch(hp, dest_t, 64)
    yb = _experts(buf, tile_expert, n_used, tile_valid, layer, w_gu, b_gu, w_down, b_down)
    y4 = _sc_gather(yb, dest_t.reshape(-1), 64).reshape(TOP_K, N_TOK, D_MODEL // 2)
    return _combine(x, mod, y4, meta)


def kernel(x_prompt, x_sample, c, cache_ckv, cache_krope, c_ctx, norm_mix_g, norm_ffn_g, w_mod, b_mod,
           g_w_in, g_b_in, g_norm_v, g_w_s, g_b_s, g_w_out, m_w_dq, m_norm_q, m_w_uq, m_w_dkv,
           m_norm_kv, m_w_ukv, m_qk_norm_q, m_qk_norm_k, m_w_o, e_w_router, e_b_router, e_w_gu,
           e_b_gu, e_w_down, e_b_down):
    x = jnp.concatenate([x_prompt.reshape(N_PROMPT, D_MODEL), x_sample.reshape(N_SAMPLE, D_MODEL)], axis=0)
    cond = jnp.concatenate([c_ctx[None, :], c, jnp.zeros((SUBLANE - N_COND, D_MODEL), F32)], axis=0)
    mod = _modulation(cond, w_mod, b_mod)
    tab = _rope_tables()
    ckv_states, krope_states = [], []
    for layer in range(DEPTH):
        j = layer // 2
        if layer % 2 == 0:
            x = _gmlp_layer(x, mod[layer], norm_mix_g[layer], g_w_in[j], g_b_in[j], g_norm_v[j],
                            g_w_s[j], g_b_s[j], g_w_out[j])
        else:
            x, s_ckv, s_krope = _mla_layer(
                x, mod[layer], norm_mix_g[layer], cache_ckv[:, j], cache_krope[:, j], tab,
                m_w_dq[j], m_norm_q[j], m_w_uq[j], m_w_dkv[j], m_norm_kv[j], m_w_ukv[j],
                m_qk_norm_q[j], m_qk_norm_k[j], m_w_o[j])
            ckv_states.append(s_ckv)
            krope_states.append(s_krope)
        x = _moe_layer(x, mod[layer], norm_ffn_g[layer], layer, e_w_router[layer], e_b_router[layer],
                       e_w_gu, e_b_gu[layer], e_w_down, e_b_down[layer])
    y_prompt = x[:N_PROMPT].reshape(BATCH, SEQ, D_MODEL)
    y_sample = x[N_PROMPT:].reshape(DEC_BATCH, DEC_SEQ, D_MODEL)
    return (y_prompt, y_sample, jnp.stack(ckv_states, axis=1), jnp.stack(krope_states, axis=1))
```

```python
import functools
import math

import jax
import jax.numpy as jnp
from jax import lax
from jax.experimental import pallas as pl
from jax.experimental.pallas import tpu as pltpu
from jax.experimental.pallas import tpu_sc as plsc

F32 = jnp.float32
BF16 = jnp.bfloat16

D_MODEL = 1024
BATCH = 32
SEQ = 256
DEPTH = 4
DEC_BATCH = 2
DEC_SEQ = 4096
PAST_LEN = 512
GRID_W = 64
RMS_EPS = 1e-6
GMLP_WIDTH = 2 * D_MODEL
GMLP_GROUPS = 8
GROUP_W = GMLP_WIDTH // GMLP_GROUPS
CHUNK = 128
N_HEADS = 16
QK_NOPE_DIM = 64
QK_ROPE_DIM = 32
QK_HEAD_DIM = QK_NOPE_DIM + QK_ROPE_DIM
V_HEAD_DIM = 64
Q_RANK = 256
KV_RANK = 128
ROPE_THETA = 10000.0
N_EXPERTS = 32
TOP_K = 4
D_FF = D_MODEL
SWIGLU_LIMIT = 7.0
SWIGLU_ALPHA = 1.702

N_PROMPT = BATCH * SEQ
N_SAMPLE = DEC_BATCH * DEC_SEQ
N_TOK = N_PROMPT + N_SAMPLE
N_COND = 1 + DEC_BATCH
KV_LEN = PAST_LEN + DEC_SEQ
N_KV_ROWS = DEC_BATCH * KV_LEN + N_PROMPT

LANE = 128
SUBLANE = 8
HEAD_PAD = LANE
QKV_W = N_HEADS * HEAD_PAD
VMEM_LIMIT = 56 * 1024 * 1024

TB = 256
TQ = 256
TKC = 256
TM = 512
N_TILES = N_TOK * TOP_K // TM + N_EXPERTS
N_SLOTS = N_TILES * TM
SC_CORES = 2
SC_WORKERS = SC_CORES * 16
ROPE_BLOCKS = DEC_SEQ // TB
NEG_BIG = -1e30


def _cparams(sem):
    return pltpu.CompilerParams(dimension_semantics=sem, vmem_limit_bytes=VMEM_LIMIT)


def _cond_of_block(i, tb):
    n_p = N_PROMPT // tb
    per = DEC_SEQ // tb
    return jnp.where(i < n_p, 0, 1 + (i - n_p) // per)


def _rms(x, g, n=None):
    n = x.shape[-1] if n is None else n
    ss = jnp.sum(x * x, axis=-1, keepdims=True) * (1.0 / n)
    return x * lax.rsqrt(ss + RMS_EPS) * g


def _dot(a, b):
    return jnp.dot(a, b, preferred_element_type=F32)


def _mod_kernel(c_ref, w_ref, b_ref, o_ref):
    c = c_ref[...]
    s = c * jax.nn.sigmoid(c)
    o_ref[0] = _dot(s.astype(BF16), w_ref[0].astype(BF16)) + b_ref[0]


def _modulation(cond, w_mod, b_mod):
    tn = 1536
    out = pl.pallas_call(
        _mod_kernel,
        grid=(DEPTH, 6 * D_MODEL // tn),
        in_specs=[
            pl.BlockSpec((SUBLANE, D_MODEL), lambda l, j: (0, 0)),
            pl.BlockSpec((1, D_MODEL, tn), lambda l, j: (l, 0, j)),
            pl.BlockSpec((1, 1, tn), lambda l, j: (l, 0, j)),
        ],
        out_specs=pl.BlockSpec((1, SUBLANE, tn), lambda l, j: (l, 0, j)),
        out_shape=jax.ShapeDtypeStruct((DEPTH, SUBLANE, 6 * D_MODEL), F32),
        compiler_params=_cparams(("parallel", "parallel")),
        name="adaln_mod",
    )(cond, w_mod, b_mod.reshape(DEPTH, 1, 6 * D_MODEL))
    m = out[:, :N_COND].reshape(DEPTH, N_COND, 6, D_MODEL)
    return jnp.pad(m, ((0, 0), (0, 0), (0, SUBLANE - 6), (0, 0)))


def _mod_spec(tb):
    return pl.BlockSpec((1, SUBLANE, D_MODEL), lambda i: (_cond_of_block(i, tb), 0, 0))


def _const_spec(shape):
    nd = len(shape)
    return pl.BlockSpec(shape, lambda i: (0,) * nd)


def _gmlp_kernel(x_ref, mod_ref, g_ref, win_ref, bin_ref, gv_ref, ws_ref, bst_ref, wout_ref, o_ref):
    x = x_ref[...]
    m = mod_ref[0]
    h = _rms(x, g_ref[...]) * (1.0 + m[1:2]) + m[0:1]
    hb = h.astype(BF16)
    zv = jax.nn.gelu(_dot(hb, win_ref[:, GMLP_WIDTH:]) + bin_ref[:, GMLP_WIDTH:], approximate=True)
    vn = _rms(zv, gv_ref[...]).astype(BF16)
    rows = []
    for c in range(TB // CHUNK):
        cols = []
        for g in range(GMLP_GROUPS):
            blk = vn[c * CHUNK:(c + 1) * CHUNK, g * GROUP_W:(g + 1) * GROUP_W]
            cols.append(_dot(ws_ref[g], blk) + bst_ref[:, g:g + 1])
        rows.append(jnp.concatenate(cols, axis=1))
    vm = jnp.concatenate(rows, axis=0)
    u = jax.nn.gelu(_dot(hb, win_ref[:, :GMLP_WIDTH]) + bin_ref[:, :GMLP_WIDTH], approximate=True)
    d = _dot((u * vm).astype(BF16), wout_ref[...])
    o_ref[...] = x + m[2:3] * d


def _gmlp_layer(x, mod, g, w_in, b_in, g_v, w_s, b_s, w_out):
    return pl.pallas_call(
        _gmlp_kernel,
        grid=(N_TOK // TB,),
        in_specs=[
            pl.BlockSpec((TB, D_MODEL), lambda i: (i, 0)),
            _mod_spec(TB),
            _const_spec((1, D_MODEL)),
            _const_spec((D_MODEL, 2 * GMLP_WIDTH)),
            _const_spec((1, 2 * GMLP_WIDTH)),
            _const_spec((1, GMLP_WIDTH)),
            _const_spec((GMLP_GROUPS, CHUNK, CHUNK)),
            _const_spec((CHUNK, GMLP_GROUPS)),
            _const_spec((GMLP_WIDTH, D_MODEL)),
        ],
        out_specs=pl.BlockSpec((TB, D_MODEL), lambda i: (i, 0)),
        out_shape=jax.ShapeDtypeStruct((N_TOK, D_MODEL), F32),
        compiler_params=_cparams(("parallel",)),
        name="gmlp_mixer",
    )(x, mod, g.reshape(1, -1), w_in.astype(BF16), b_in.reshape(1, -1), g_v.reshape(1, -1),
      w_s.astype(BF16), b_s.T, w_out.astype(BF16))


def _rope_tables():
    t = jnp.arange(DEC_SEQ)
    row_id = (t // GRID_W).astype(F32)
    col_id = (t % GRID_W).astype(F32)
    axis_dim = QK_ROPE_DIM // 2
    inv_freq = ROPE_THETA ** (-jnp.arange(0, axis_dim, 2, dtype=F32) / axis_dim)
    ang = jnp.stack([row_id[:, None] * inv_freq, col_id[:, None] * inv_freq], axis=1)
    cos, sin = jnp.cos(ang), jnp.sin(ang)
    zeros = jnp.zeros_like(sin)
    cos_l = jnp.concatenate([cos, cos], axis=-1).reshape(DEC_SEQ, QK_ROPE_DIM)
    s1_l = jnp.concatenate([-sin, zeros], axis=-1).reshape(DEC_SEQ, QK_ROPE_DIM)
    s2_l = jnp.concatenate([zeros, sin], axis=-1).reshape(DEC_SEQ, QK_ROPE_DIM)

    def widen(rope_part, nope_fill):
        left = jnp.full((DEC_SEQ, QK_NOPE_DIM), nope_fill, F32)
        right = jnp.zeros((DEC_SEQ, HEAD_PAD - QK_HEAD_DIM), F32)
        return jnp.concatenate([left, rope_part, right], axis=-1)

    pos = jnp.stack([widen(cos_l, 1.0), widen(s1_l, 0.0), widen(s2_l, 0.0)])
    ident_c = jnp.concatenate([jnp.ones((TB, QK_HEAD_DIM), F32),
                               jnp.zeros((TB, HEAD_PAD - QK_HEAD_DIM), F32)], axis=-1)
    ident = jnp.stack([ident_c, jnp.zeros_like(ident_c), jnp.zeros_like(ident_c)])
    return jnp.concatenate([pos, ident], axis=1)


def _rope(xn, tab_ref):
    half = QK_ROPE_DIM // 4
    return (xn * tab_ref[0] + pltpu.roll(xn, HEAD_PAD - half, 1) * tab_ref[1]
            + pltpu.roll(xn, half, 1) * tab_ref[2])


def _dot_nt(a, b):
    return lax.dot_general(a, b, (((1,), (1,)), ((), ())), preferred_element_type=F32)


def _shift_rows(x, n):
    n = n % x.shape[0]
    return jnp.concatenate([x[n:], x[:n]], axis=0)


def _mla_proj_kernel(x_ref, mod_ref, g_ref, wdq_ref, nq_ref, wuqt_ref, gq_ref, wdkv_ref, nkv_ref,
                     tabt_ref, qt_ref, ckv_ref, krp_ref):
    x = x_ref[...]
    m = mod_ref[0]
    h = _rms(x, g_ref[...]) * (1.0 + m[1:2]) + m[0:1]
    hb = h.astype(BF16)
    cq = _rms(_dot(hb, wdq_ref[...]), nq_ref[...])
    qt = _dot_nt(wuqt_ref[...], cq.astype(BF16))
    gq = gq_ref[...]
    half = QK_ROPE_DIM // 4
    for hd in range(N_HEADS):
        qh = qt[hd * HEAD_PAD:(hd + 1) * HEAD_PAD, :]
        ss = jnp.sum(qh * qh, axis=0, keepdims=True) * (1.0 / QK_HEAD_DIM)
        qn = qh * lax.rsqrt(ss + RMS_EPS) * gq
        qr = qn * tabt_ref[0] + _shift_rows(qn, half) * tabt_ref[1] + _shift_rows(qn, -half) * tabt_ref[2]
        qt_ref[hd * HEAD_PAD:(hd + 1) * HEAD_PAD, :] = qr.astype(BF16)
    kva = _dot(hb, wdkv_ref[...])
    ckv_ref[...] = _rms(kva[:, :KV_RANK], nkv_ref[...])
    krp_ref[...] = kva[:, KV_RANK:]


def _pad_heads(w, per_head):
    k = w.shape[0]
    w = w.reshape(k, N_HEADS, per_head)
    return jnp.pad(w, ((0, 0), (0, 0), (0, HEAD_PAD - per_head))).reshape(k, QKV_W)


def _pad_gain(g):
    return jnp.pad(g, (0, HEAD_PAD - QK_HEAD_DIM)).reshape(1, HEAD_PAD)


def _mla_proj(x, mod, g, w_dq, norm_q, w_uq, gq, w_dkv, norm_kv, tab_t):
    wdkv = jnp.concatenate([w_dkv[:, :KV_RANK], jnp.zeros((D_MODEL, QK_NOPE_DIM), F32),
                            w_dkv[:, KV_RANK:], jnp.zeros((D_MODEL, HEAD_PAD - QK_HEAD_DIM), F32)], axis=1)
    n_pb = N_PROMPT // TB

    def tab_idx(i):
        return (0, 0, jnp.where(i < n_pb, ROPE_BLOCKS, (i - n_pb) % ROPE_BLOCKS))

    return pl.pallas_call(
        _mla_proj_kernel,
        grid=(N_TOK // TB,),
        in_specs=[
            pl.BlockSpec((TB, D_MODEL), lambda i: (i, 0)),
            _mod_spec(TB),
            _const_spec((1, D_MODEL)),
            _const_spec((D_MODEL, Q_RANK)),
            _const_spec((1, Q_RANK)),
            _const_spec((QKV_W, Q_RANK)),
            _const_spec((HEAD_PAD, 1)),
            _const_spec((D_MODEL, 2 * LANE)),
            _const_spec((1, KV_RANK)),
            pl.BlockSpec((3, HEAD_PAD, TB), tab_idx),
        ],
        out_specs=[
            pl.BlockSpec((QKV_W, TB), lambda i: (0, i)),
            pl.BlockSpec((TB, KV_RANK), lambda i: (i, 0)),
            pl.BlockSpec((TB, HEAD_PAD), lambda i: (i, 0)),
        ],
        out_shape=[
            jax.ShapeDtypeStruct((QKV_W, N_TOK), BF16),
            jax.ShapeDtypeStruct((N_TOK, KV_RANK), F32),
            jax.ShapeDtypeStruct((N_TOK, HEAD_PAD), F32),
        ],
        compiler_params=_cparams(("parallel",)),
        name="mla_proj",
    )(x, mod, g.reshape(1, -1), w_dq.astype(BF16), norm_q.reshape(1, -1),
      _pad_heads(w_uq, QK_HEAD_DIM).T.astype(BF16), _pad_gain(gq).reshape(HEAD_PAD, 1), wdkv.astype(BF16),
      norm_kv.reshape(1, -1), tab_t)


def _kv_expand_kernel(ckv_ref, krp_ref, wukv_ref, wukvt_ref, gk_ref, tab_ref, k_ref, vt_ref):
    ckv = ckv_ref[...].astype(BF16)
    kv = _dot(ckv, wukv_ref[...])
    kvt = _dot_nt(wukvt_ref[...], ckv)
    krp = krp_ref[...]
    gk = gk_ref[...]
    lane = lax.broadcasted_iota(jnp.int32, (TB, HEAD_PAD), 1)
    is_nope = lane < QK_NOPE_DIM
    row = lax.broadcasted_iota(jnp.int32, (HEAD_PAD, TB), 0)
    ones_row = jnp.where(row == 0, 1.0, 0.0)
    for hd in range(N_HEADS):
        kvh = kv[:, hd * HEAD_PAD:(hd + 1) * HEAD_PAD]
        kn = _rms(jnp.where(is_nope, kvh, krp), gk, QK_HEAD_DIM)
        k_ref[:, hd * HEAD_PAD:(hd + 1) * HEAD_PAD] = _rope(kn, tab_ref).astype(BF16)
        vth = kvt[hd * HEAD_PAD:(hd + 1) * HEAD_PAD, :]
        vt_ref[hd * HEAD_PAD:(hd + 1) * HEAD_PAD, :] = jnp.where(row < QK_NOPE_DIM, ones_row, vth).astype(BF16)


def _kv_expand(ckv_all, krp_all, w_ukv, gk, tab):
    n_sb = DEC_BATCH * KV_LEN // TB
    per = KV_LEN // TB
    n_cache = PAST_LEN // TB

    def tab_idx(i):
        j = i % per
        blk = jnp.where((i >= n_sb) | (j < n_cache), ROPE_BLOCKS, j - n_cache)
        return (0, blk, 0)

    return pl.pallas_call(
        _kv_expand_kernel,
        grid=(N_KV_ROWS // TB,),
        in_specs=[
            pl.BlockSpec((TB, KV_RANK), lambda i: (i, 0)),
            pl.BlockSpec((TB, HEAD_PAD), lambda i: (i, 0)),
            _const_spec((KV_RANK, QKV_W)),
            _const_spec((QKV_W, KV_RANK)),
            _const_spec((1, HEAD_PAD)),
            pl.BlockSpec((3, TB, HEAD_PAD), tab_idx),
        ],
        out_specs=[
            pl.BlockSpec((TB, QKV_W), lambda i: (i, 0)),
            pl.BlockSpec((QKV_W, TB), lambda i: (0, i)),
        ],
        out_shape=[
            jax.ShapeDtypeStruct((N_KV_ROWS, QKV_W), BF16),
            jax.ShapeDtypeStruct((QKV_W, N_KV_ROWS), BF16),
        ],
        compiler_params=_cparams(("parallel",)),
        name="mla_kv_expand",
    )(ckv_all, krp_all, w_ukv.astype(BF16), w_ukv.T.astype(BF16), _pad_gain(gk), tab)


def _attn_kernel(qt_ref, k_ref, vt_ref, o_ref, s_a, s_b, *, t_k, hps):
    c = (1.0 / math.sqrt(QK_HEAD_DIM)) * math.log2(math.e)
    kc = min(TKC, t_k)
    n_chunks = t_k // kc
    bufs = (s_a, s_b)

    def rows(h):
        return slice(h * HEAD_PAD, (h + 1) * HEAD_PAD)

    def scores(h, j, m8):
        st = _dot(k_ref[j * kc:(j + 1) * kc, rows(h)], qt_ref[rows(h), :])
        bufs[h % 2][j * kc:(j + 1) * kc, :] = st
        return jnp.maximum(m8, jnp.max(st.reshape(kc // SUBLANE, SUBLANE, TQ), axis=0))

    def weigh(h, j, m, acc):
        pt = jnp.exp2((bufs[h % 2][j * kc:(j + 1) * kc, :] - m) * c).astype(BF16)
        return acc + _dot(vt_ref[rows(h), j * kc:(j + 1) * kc], pt)

    m8_init = jnp.full((SUBLANE, TQ), NEG_BIG, F32)
    outs = []
    m8 = m8_init
    for j in range(n_chunks):
        m8 = scores(0, j, m8)
    for h in range(1, hps + 1):
        m = jnp.max(m8, axis=0, keepdims=True)
        acc = jnp.zeros((HEAD_PAD, TQ), F32)
        m8 = m8_init
        for j in range(n_chunks):
            if h < hps:
                m8 = scores(h, j, m8)
            acc = weigh(h - 1, j, m, acc)
        outs.append(acc[QK_NOPE_DIM:, :] / acc[0:1, :])
    o_ref[...] = jnp.concatenate(outs, axis=0).T.astype(BF16)


def _attention(qt, k, vt, *, n_batch, t_q, t_k, q_row0, kv_row0, hps):
    nq = t_q // TQ
    q0 = q_row0 // TQ
    k0 = kv_row0 // t_k
    return pl.pallas_call(
        functools.partial(_attn_kernel, t_k=t_k, hps=hps),
        grid=(n_batch, N_HEADS // hps, nq),
        in_specs=[
            pl.BlockSpec((hps * HEAD_PAD, TQ), lambda b, h, i: (h, q0 + b * nq + i)),
            pl.BlockSpec((t_k, hps * HEAD_PAD), lambda b, h, i: (k0 + b, h)),
            pl.BlockSpec((hps * HEAD_PAD, t_k), lambda b, h, i: (h, k0 + b)),
        ],
        out_specs=pl.BlockSpec((TQ, hps * V_HEAD_DIM), lambda b, h, i: (b * nq + i, h)),
        out_shape=jax.ShapeDtypeStruct((n_batch * t_q, N_HEADS * V_HEAD_DIM), BF16),
        scratch_shapes=[pltpu.VMEM((t_k, TQ), F32), pltpu.VMEM((t_k, TQ), F32)],
        compiler_params=_cparams(("parallel", "parallel", "parallel")),
        name=f"mla_attention_tk{t_k}",
    )(qt, k, vt)


def _out_proj_kernel(ap_ref, as_ref, x_ref, mod_ref, wo_ref, o_ref):
    a = jnp.where(pl.program_id(0) < N_PROMPT // TB, ap_ref[...], as_ref[...])
    o_ref[...] = x_ref[...] + mod_ref[0][2:3] * _dot(a, wo_ref[...])


def _out_proj(attn_p, attn_s, x, mod, w_o):
    n_pb = N_PROMPT // TB
    return pl.pallas_call(
        _out_proj_kernel,
        grid=(N_TOK // TB,),
        in_specs=[
            pl.BlockSpec((TB, D_MODEL), lambda i: (jnp.minimum(i, n_pb - 1), 0)),
            pl.BlockSpec((TB, D_MODEL), lambda i: (jnp.maximum(i - n_pb, 0), 0)),
            pl.BlockSpec((TB, D_MODEL), lambda i: (i, 0)),
            _mod_spec(TB),
            _const_spec((D_MODEL, D_MODEL)),
        ],
        out_specs=pl.BlockSpec((TB, D_MODEL), lambda i: (i, 0)),
        out_shape=jax.ShapeDtypeStruct((N_TOK, D_MODEL), F32),
        compiler_params=_cparams(("parallel",)),
        name="mla_out_proj",
    )(attn_p, attn_s, x, mod, w_o.astype(BF16))


def _mla_layer(x, mod, g, cache_ckv, cache_krope, tab, w_dq, norm_q, w_uq, w_dkv, norm_kv, w_ukv,
               gq, gk, w_o):
    tab_t = jnp.swapaxes(tab, 1, 2)
    qt, ckv, krp = _mla_proj(x, mod, g, w_dq, norm_q, w_uq, gq, w_dkv, norm_kv, tab_t)
    cache_krp = jnp.pad(cache_krope, ((0, 0), (0, 0), (QK_NOPE_DIM, HEAD_PAD - QK_HEAD_DIM)))
    ckv_s = ckv[N_PROMPT:].reshape(DEC_BATCH, DEC_SEQ, KV_RANK)
    krp_s = krp[N_PROMPT:].reshape(DEC_BATCH, DEC_SEQ, HEAD_PAD)
    ckv_all = jnp.concatenate([jnp.concatenate([cache_ckv, ckv_s], axis=1).reshape(-1, KV_RANK),
                               ckv[:N_PROMPT]], axis=0)
    krp_all = jnp.concatenate([jnp.concatenate([cache_krp, krp_s], axis=1).reshape(-1, HEAD_PAD),
                               krp[:N_PROMPT]], axis=0)
    k, vt = _kv_expand(ckv_all, krp_all, w_ukv, gk, tab)
    a_p = _attention(qt, k, vt, n_batch=BATCH, t_q=SEQ, t_k=SEQ, q_row0=0, kv_row0=DEC_BATCH * KV_LEN,
                     hps=N_HEADS)
    a_s = _attention(qt, k, vt, n_batch=DEC_BATCH, t_q=DEC_SEQ, t_k=KV_LEN, q_row0=N_PROMPT, kv_row0=0,
                     hps=4)
    x = _out_proj(a_p, a_s, x, mod, w_o)
    state_ckv = ckv[:N_PROMPT].reshape(BATCH, SEQ, KV_RANK)
    state_krope = krp[:N_PROMPT, QK_NOPE_DIM:QK_HEAD_DIM].reshape(BATCH, SEQ, QK_ROPE_DIM)
    return x, state_ckv, state_krope


def _route_kernel(x_ref, mod_ref, g_ref, wr_ref, br_ref, tri_ref, h_ref, meta_ref, metat_ref, cnt_ref, carry):
    i = pl.program_id(0)

    @pl.when(i == 0)
    def _():
        carry[...] = jnp.zeros_like(carry)

    x = x_ref[...]
    m = mod_ref[0]
    h = _rms(x, g_ref[...]) * (1.0 + m[4:5]) + m[3:4]
    h_ref[...] = _pack_bf16_pairs(h)
    logits = _dot(h.astype(BF16), wr_ref[...]) + br_ref[...]
    lane = lax.broadcasted_iota(jnp.int32, logits.shape, 1).astype(F32)
    work = logits
    sel = jnp.zeros(logits.shape, F32)
    hits, tops = [], []
    for k in range(TOP_K):
        mk = jnp.max(work, axis=-1, keepdims=True)
        first = jnp.min(jnp.where(work == mk, lane, float(LANE)), axis=-1, keepdims=True)
        hit = lane == first
        sel = jnp.where(hit, 1.0, sel)
        work = jnp.where(hit, -jnp.inf, work)
        hits.append((hit, first))
        tops.append(mk)
    es = [jnp.exp(t - tops[0]) for t in tops]
    denom = es[0] + es[1] + es[2] + es[3]
    pos = _dot(tri_ref[...], sel.astype(BF16)) + carry[0:1, :]
    carry[...] = carry[...] + jnp.sum(sel, axis=0, keepdims=True)
    cnt_ref[...] = carry[...]
    meta = jnp.zeros(logits.shape, F32)
    for k in range(TOP_K):
        hit, first = hits[k]
        pk = jnp.sum(jnp.where(hit, pos, 0.0), axis=-1, keepdims=True)
        meta = jnp.where(lane == float(k), first, meta)
        meta = jnp.where(lane == float(TOP_K + k), es[k] / denom, meta)
        meta = jnp.where(lane == float(2 * TOP_K + k), pk, meta)
    meta_ref[...] = meta
    metat_ref[...] = meta.T[:2 * SUBLANE, :]


def _route(x, mod, g, w_router, b_router):
    wr = jnp.pad(w_router, ((0, 0), (0, LANE - N_EXPERTS))).astype(BF16)
    br = jnp.pad(b_router, (0, LANE - N_EXPERTS), constant_values=NEG_BIG).reshape(1, LANE)
    tri = jnp.tri(TB, TB, -1, dtype=BF16)
    return pl.pallas_call(
        _route_kernel,
        grid=(N_TOK // TB,),
        in_specs=[
            pl.BlockSpec((TB, D_MODEL), lambda i: (i, 0)),
            _mod_spec(TB),
            _const_spec((1, D_MODEL)),
            _const_spec((D_MODEL, LANE)),
            _const_spec((1, LANE)),
            _const_spec((TB, TB)),
        ],
        out_specs=[
            pl.BlockSpec((TB, D_MODEL // 2), lambda i: (i, 0)),
            pl.BlockSpec((TB, LANE), lambda i: (i, 0)),
            pl.BlockSpec((2 * SUBLANE, TB), lambda i: (0, i)),
            _const_spec((SUBLANE, LANE)),
        ],
        out_shape=[
            jax.ShapeDtypeStruct((N_TOK, D_MODEL // 2), jnp.uint32),
            jax.ShapeDtypeStruct((N_TOK, LANE), F32),
            jax.ShapeDtypeStruct((2 * SUBLANE, N_TOK), F32),
            jax.ShapeDtypeStruct((SUBLANE, LANE), F32),
        ],
        scratch_shapes=[pltpu.VMEM((SUBLANE, LANE), F32)],
        compiler_params=_cparams(("arbitrary",)),
        name="moe_route",
    )(x, mod, g.reshape(1, -1), wr, br, tri)


def _sc_gather(table, idx, ch):
    b, w = idx.shape[0], table.shape[1]
    per_w = b // SC_WORKERS
    n_ch = per_w // ch
    assert per_w * SC_WORKERS == b and n_ch * ch == per_w and n_ch % 2 == 0
    mesh = plsc.VectorSubcoreMesh(core_axis_name="c", subcore_axis_name="s")

    @functools.partial(
        pl.kernel, mesh=mesh,
        out_type=jax.ShapeDtypeStruct((b, w), table.dtype),
        scratch_types=[
            pltpu.VMEM((n_ch, ch), jnp.int32),
            pltpu.VMEM((ch, w), table.dtype),
            pltpu.VMEM((ch, w), table.dtype),
            pltpu.SemaphoreType.DMA, pltpu.SemaphoreType.DMA,
            pltpu.SemaphoreType.DMA, pltpu.SemaphoreType.DMA,
        ],
        name="sc_row_gather",
    )
    def gather_rows(table_hbm, idx_hbm, out_hbm, idx_v, buf0, buf1, g0, g1, s0, s1):
        wid = lax.axis_index("s") * SC_CORES + lax.axis_index("c")
        base = wid * per_w
        pltpu.sync_copy(idx_hbm.at[wid], idx_v)

        def gather(j, buf, sem):
            return pltpu.make_async_copy(table_hbm.at[idx_v.at[j]], buf, sem)

        def store(j, buf, sem):
            return pltpu.make_async_copy(buf, out_hbm.at[pl.ds(base + j * ch, ch)], sem)

        gather(0, buf0, g0).start()

        @pl.loop(0, n_ch, step=2)
        def _(j):
            @pl.when(j > 0)
            def _():
                store(j - 1, buf1, s1).wait()

            gather(j + 1, buf1, g1).start()
            gather(j, buf0, g0).wait()
            store(j, buf0, s0).start()
            gather(j + 1, buf1, g1).wait()
            store(j + 1, buf1, s1).start()
            store(j, buf0, s0).wait()

            @pl.when(j + 2 < n_ch)
            def _():
                gather(j + 2, buf0, g0).start()

        store(n_ch - 1, buf1, s1).wait()

    return gather_rows(table, idx.reshape(SC_WORKERS, n_ch, ch))


def _sc_dispatch(rows, dest_t, ch):
    n, w = rows.shape
    per_w = n // SC_WORKERS
    n_ch = per_w // ch
    assert per_w * SC_WORKERS == n and n_ch * ch == per_w and n_ch % 2 == 0
    mesh = plsc.VectorSubcoreMesh(core_axis_name="c", subcore_axis_name="s")
    idx = dest_t.reshape(TOP_K, SC_WORKERS, n_ch, ch).transpose(1, 0, 2, 3).reshape(SC_WORKERS, TOP_K * n_ch, ch)

    @functools.partial(
        pl.kernel, mesh=mesh,
        out_type=jax.ShapeDtypeStruct((N_SLOTS, w), rows.dtype),
        scratch_types=[
            pltpu.VMEM((TOP_K * n_ch, ch), jnp.int32),
            pltpu.VMEM((ch, w), rows.dtype),
            pltpu.VMEM((ch, w), rows.dtype),
            pltpu.SemaphoreType.DMA, pltpu.SemaphoreType.DMA,
            pltpu.SemaphoreType.DMA, pltpu.SemaphoreType.DMA,
        ],
        name="sc_row_dispatch",
    )
    def dispatch_rows(rows_hbm, idx_hbm, out_hbm, idx_v, buf0, buf1, l0, l1, s0, s1):
        wid = lax.axis_index("s") * SC_CORES + lax.axis_index("c")
        base = wid * per_w
        pltpu.sync_copy(idx_hbm.at[wid], idx_v)

        def load(j, buf, sem):
            return pltpu.make_async_copy(rows_hbm.at[pl.ds(base + j * ch, ch)], buf, sem)

        def scatter(j, k, buf, sem):
            return pltpu.make_async_copy(buf, out_hbm.at[idx_v.at[k * n_ch + j]], sem)

        load(0, buf0, l0).start()

        @pl.loop(0, n_ch, step=2)
        def _(j):
            load(j + 1, buf1, l1).start()
            load(j, buf0, l0).wait()
            for k in range(TOP_K):
                scatter(j, k, buf0, s0).start()
            load(j + 1, buf1, l1).wait()
            for k in range(TOP_K):
                scatter(j + 1, k, buf1, s1).start()
            for k in range(TOP_K):
                scatter(j, k, buf0, s0).wait()

            @pl.when(j + 2 < n_ch)
            def _():
                load(j + 2, buf0, l0).start()

            for k in range(TOP_K):
                scatter(j + 1, k, buf1, s1).wait()

    return dispatch_rows(rows, idx)


def _deinterleave_matrix():
    src = jnp.arange(2 * LANE)[:, None]
    dst = jnp.arange(2 * LANE)[None, :]
    want = jnp.where(dst < LANE, 2 * dst, 2 * (dst - LANE) + 1)
    return (src == want).astype(BF16)


def _expert_kernel(te_ref, nu_ref, tv_ref, nx_ref, x_ref, wgu_hbm, bgu_ref, wd_hbm, bd_ref, perm_ref, o_ref,
                   wgu_st, wd_st, wgu_bf, wd_bf, sems, *, layer):
    i = pl.program_id(0)
    prev = te_ref[jnp.maximum(i - 1, 0)]
    fresh = jnp.logical_or(i == 0, te_ref[i] != prev)

    def fetch(e):
        return (pltpu.make_async_copy(wgu_hbm.at[layer, e], wgu_st, sems.at[0]),
                pltpu.make_async_copy(wd_hbm.at[layer, e], wd_st, sems.at[1]))

    @pl.when(i == 0)
    def _():
        for cp in fetch(te_ref[0]):
            cp.start()

    @pl.when(jnp.logical_and(fresh, i < nu_ref[0]))
    def _():
        for cp in fetch(te_ref[i]):
            cp.wait()
        for b in range(2 * D_FF // (2 * LANE)):
            sl = slice(b * 2 * LANE, (b + 1) * 2 * LANE)
            wgu_bf[:, sl] = _dot(wgu_st[:, sl].astype(BF16), perm_ref[...]).astype(BF16)
        wd_bf[...] = wd_st[...].astype(BF16)

        @pl.when(nx_ref[i] >= 0)
        def _():
            for cp in fetch(nx_ref[i]):
                cp.start()

    @pl.when(i < nu_ref[0])
    def _():
        row = lax.broadcasted_iota(jnp.int32, (TM, D_MODEL // 2), 0)
        w = jnp.where(row < tv_ref[i], x_ref[...], jnp.uint32(0))
        x = _unpack_bf16_pairs(w).astype(BF16)
        gu = _dot(x, wgu_bf[...]) + bgu_ref[...]
        acts = []
        for b in range(D_FF // LANE):
            glu = jnp.minimum(gu[:, b * 2 * LANE:b * 2 * LANE + LANE], SWIGLU_LIMIT)
            lin = jnp.clip(gu[:, b * 2 * LANE + LANE:(b + 1) * 2 * LANE], -SWIGLU_LIMIT, SWIGLU_LIMIT)
            acts.append((glu * jax.nn.sigmoid(SWIGLU_ALPHA * glu) * (lin + 1.0)).astype(BF16))
        act = jnp.concatenate(acts, axis=1)
        o_ref[...] = _pack_bf16_pairs(_dot(act, wd_bf[...]) + bd_ref[...])


def _experts(buf, tile_expert, n_used, tile_valid, tile_next, layer, w_gu, b_gu, w_down, b_down):
    bgu = b_gu.reshape(N_EXPERTS, D_FF // LANE, LANE, 2).transpose(0, 1, 3, 2).reshape(N_EXPERTS, 1, 2 * D_FF)

    def row_idx(i, te, nu, tv, nx):
        return (jnp.minimum(i, nu[0] - 1), 0)

    def b_idx(i, te, nu, tv, nx):
        return (te[i], 0, 0)

    grid_spec = pltpu.PrefetchScalarGridSpec(
        num_scalar_prefetch=4,
        grid=(N_TILES,),
        in_specs=[
            pl.BlockSpec((TM, D_MODEL // 2), row_idx),
            pl.BlockSpec(memory_space=pl.ANY),
            pl.BlockSpec((None, 1, 2 * D_FF), b_idx),
            pl.BlockSpec(memory_space=pl.ANY),
            pl.BlockSpec((None, 1, D_MODEL), b_idx),
            pl.BlockSpec((2 * LANE, 2 * LANE), lambda i, te, nu, tv, nx: (0, 0)),
        ],
        out_specs=pl.BlockSpec((TM, D_MODEL // 2), row_idx),
        scratch_shapes=[
            pltpu.VMEM((D_MODEL, 2 * D_FF), F32),
            pltpu.VMEM((D_FF, D_MODEL), F32),
            pltpu.VMEM((D_MODEL, 2 * D_FF), BF16),
            pltpu.VMEM((D_FF, D_MODEL), BF16),
            pltpu.SemaphoreType.DMA((2,)),
        ],
    )
    return pl.pallas_call(
        functools.partial(_expert_kernel, layer=layer),
        grid_spec=grid_spec,
        out_shape=jax.ShapeDtypeStruct((N_SLOTS, D_MODEL // 2), jnp.uint32),
        compiler_params=_cparams(("arbitrary",)),
        name="moe_experts",
    )(tile_expert, n_used, tile_valid, tile_next, buf, w_gu, bgu, w_down,
      b_down.reshape(N_EXPERTS, 1, D_MODEL), _deinterleave_matrix())


def _pack_bf16_pairs(v):
    half = v.shape[1] // 2
    bits = pltpu.bitcast(v.astype(BF16).astype(F32), jnp.uint32)
    return (bits[:, half:] & jnp.uint32(0xFFFF0000)) | (bits[:, :half] >> 16)


def _unpack_bf16_pairs(w):
    return jnp.concatenate([pltpu.bitcast(w << 16, F32), pltpu.bitcast(w & jnp.uint32(0xFFFF0000), F32)],
                           axis=1)


def _combine_kernel(x_ref, mod_ref, y_ref, w_ref, o_ref):
    w = w_ref[:, TOP_K:2 * TOP_K]
    y = _unpack_bf16_pairs(y_ref[0]) * w[:, 0:1]
    for k in range(1, TOP_K):
        y = y + _unpack_bf16_pairs(y_ref[k]) * w[:, k:k + 1]
    o_ref[...] = x_ref[...] + mod_ref[0][5:6] * y


def _combine(x, mod, y4, meta):
    return pl.pallas_call(
        _combine_kernel,
        grid=(N_TOK // TB,),
        in_specs=[
            pl.BlockSpec((TB, D_MODEL), lambda i: (i, 0)),
            _mod_spec(TB),
            pl.BlockSpec((TOP_K, TB, D_MODEL // 2), lambda i: (0, i, 0)),
            pl.BlockSpec((TB, LANE), lambda i: (i, 0)),
        ],
        out_specs=pl.BlockSpec((TB, D_MODEL), lambda i: (i, 0)),
        out_shape=jax.ShapeDtypeStruct((N_TOK, D_MODEL), F32),
        compiler_params=_cparams(("parallel",)),
        name="moe_combine",
    )(x, mod, y4, meta)


def _moe_layer(x, mod, g, layer, w_router, b_router, w_gu, b_gu, w_down, b_down):
    hp, meta, meta_t, cnt = _route(x, mod, g, w_router, b_router)
    counts = cnt[0, :N_EXPERTS].astype(jnp.int32)
    tiles = (counts + TM - 1) // TM
    tile_end = jnp.cumsum(tiles)
    start = (tile_end - tiles) * TM
    n_used = tile_end[-1:].astype(jnp.int32)
    tile_ids = jnp.minimum(jnp.arange(N_TILES), n_used[0] - 1)
    tile_expert = jnp.sum(tile_end[None, :] <= tile_ids[:, None], axis=1).astype(jnp.int32)
    first_tile = tile_end - tiles
    tile_valid = jnp.clip(counts[tile_expert] - (tile_ids - first_tile[tile_expert]) * TM, 0, TM).astype(jnp.int32)
    e_t = meta_t[0:TOP_K].astype(jnp.int32)
    start_t = jnp.zeros_like(e_t)
    for e in range(N_EXPERTS):
        start_t = jnp.where(e_t == e, start[e], start_t)
    dest_t = start_t + meta_t[2 * TOP_K:3 * TOP_K].astype(jnp.int32)
    ids = jnp.arange(N_EXPERTS)
    later = jnp.where((ids[None, :] > ids[:, None]) & (tiles[None, :] > 0), ids[None, :], N_EXPERTS)
    next_e = jnp.min(later, axis=1)
    tile_next = jnp.where(next_e < N_EXPERTS, next_e, -1)[tile_expert].astype(jnp.int32)
    buf = _sc_dispatch(hp, dest_t, 64)
    yb = _experts(buf, tile_expert, n_used, tile_valid, tile_next, layer, w_gu, b_gu, w_down, b_down)
    y4 = _sc_gather(yb, dest_t.reshape(-1), 64).reshape(TOP_K, N_TOK, D_MODEL // 2)
    return _combine(x, mod, y4, meta)


def kernel(x_prompt, x_sample, c, cache_ckv, cache_krope, c_ctx, norm_mix_g, norm_ffn_g, w_mod, b_mod,
           g_w_in, g_b_in, g_norm_v, g_w_s, g_b_s, g_w_out, m_w_dq, m_norm_q, m_w_uq, m_w_dkv,
           m_norm_kv, m_w_ukv, m_qk_norm_q, m_qk_norm_k, m_w_o, e_w_router, e_b_router, e_w_gu,
           e_b_gu, e_w_down, e_b_down):
    x = jnp.concatenate([x_prompt.reshape(N_PROMPT, D_MODEL), x_sample.reshape(N_SAMPLE, D_MODEL)], axis=0)
    cond = jnp.concatenate([c_ctx[None, :], c, jnp.zeros((SUBLANE - N_COND, D_MODEL), F32)], axis=0)
    mod = _modulation(cond, w_mod, b_mod)
    tab = _rope_tables()
    ckv_states, krope_states = [], []
    for layer in range(DEPTH):
        j = layer // 2
        if layer % 2 == 0:
            x = _gmlp_layer(x, mod[layer], norm_mix_g[layer], g_w_in[j], g_b_in[j], g_norm_v[j],
                            g_w_s[j], g_b_s[j], g_w_out[j])
        else:
            x, s_ckv, s_krope = _mla_layer(
                x, mod[layer], norm_mix_g[layer], cache_ckv[:, j], cache_krope[:, j], tab,
                m_w_dq[j], m_norm_q[j], m_w_uq[j], m_w_dkv[j], m_norm_kv[j], m_w_ukv[j],
                m_qk_norm_q[j], m_qk_norm_k[j], m_w_o[j])
            ckv_states.append(s_ckv)
            krope_states.append(s_krope)
        x = _moe_layer(x, mod[layer], norm_ffn_g[layer], layer, e_w_router[layer], e_b_router[layer],
                       e_w_gu, e_b_gu[layer], e_w_down, e_b_down[layer])
    y_prompt = x[:N_PROMPT].reshape(BATCH, SEQ, D_MODEL)
    y_sample = x[N_PROMPT:].reshape(DEC_BATCH, DEC_SEQ, D_MODEL)
    return (y_prompt, y_sample, jnp.stack(ckv_states, axis=1), jnp.stack(krope_states, axis=1))
```

```python
import functools
import math

import jax
import jax.numpy as jnp
from jax import lax
from jax.experimental import pallas as pl
from jax.experimental.pallas import tpu as pltpu
from jax.experimental.pallas import tpu_sc as plsc

F32 = jnp.float32
BF16 = jnp.bfloat16

D_MODEL = 1024
BATCH = 32
SEQ = 256
DEPTH = 4
DEC_BATCH = 2
DEC_SEQ = 4096
PAST_LEN = 512
GRID_W = 64
RMS_EPS = 1e-6
GMLP_WIDTH = 2 * D_MODEL
GMLP_GROUPS = 8
GROUP_W = GMLP_WIDTH // GMLP_GROUPS
CHUNK = 128
N_HEADS = 16
QK_NOPE_DIM = 64
QK_ROPE_DIM = 32
QK_HEAD_DIM = QK_NOPE_DIM + QK_ROPE_DIM
V_HEAD_DIM = 64
Q_RANK = 256
KV_RANK = 128
ROPE_THETA = 10000.0
N_EXPERTS = 32
TOP_K = 4
D_FF = D_MODEL
SWIGLU_LIMIT = 7.0
SWIGLU_ALPHA = 1.702

N_PROMPT = BATCH * SEQ
N_SAMPLE = DEC_BATCH * DEC_SEQ
N_TOK = N_PROMPT + N_SAMPLE
N_COND = 1 + DEC_BATCH
KV_LEN = PAST_LEN + DEC_SEQ
N_KV_ROWS = DEC_BATCH * KV_LEN + N_PROMPT

LANE = 128
SUBLANE = 8
HEAD_PAD = LANE
QKV_W = N_HEADS * HEAD_PAD
VMEM_LIMIT = 56 * 1024 * 1024

TB = 256
TQ = 256
TKC = 256
TM = 512
N_TILES = N_TOK * TOP_K // TM + N_EXPERTS
N_SLOTS = N_TILES * TM
PLAN_LANES = -(-N_TILES // LANE) * LANE
SLOT_TB = 2048
SC_CORES = 2
SC_WORKERS = SC_CORES * 16
ROPE_BLOCKS = DEC_SEQ // TB
NEG_BIG = -1e30


def _cparams(sem):
    return pltpu.CompilerParams(dimension_semantics=sem, vmem_limit_bytes=VMEM_LIMIT)


def _cond_of_block(i, tb):
    n_p = N_PROMPT // tb
    per = DEC_SEQ // tb
    return jnp.where(i < n_p, 0, 1 + (i - n_p) // per)


def _rms(x, g, n=None):
    n = x.shape[-1] if n is None else n
    ss = jnp.sum(x * x, axis=-1, keepdims=True) * (1.0 / n)
    return x * lax.rsqrt(ss + RMS_EPS) * g


def _dot(a, b):
    return jnp.dot(a, b, preferred_element_type=F32)


def _mod_kernel(c_ref, w_ref, b_ref, o_ref):
    c = c_ref[...]
    s = c * jax.nn.sigmoid(c)
    o_ref[0] = _dot(s.astype(BF16), w_ref[0].astype(BF16)) + b_ref[0]


def _modulation(cond, w_mod, b_mod):
    tn = 1536
    out = pl.pallas_call(
        _mod_kernel,
        grid=(DEPTH, 6 * D_MODEL // tn),
        in_specs=[
            pl.BlockSpec((SUBLANE, D_MODEL), lambda l, j: (0, 0)),
            pl.BlockSpec((1, D_MODEL, tn), lambda l, j: (l, 0, j)),
            pl.BlockSpec((1, 1, tn), lambda l, j: (l, 0, j)),
        ],
        out_specs=pl.BlockSpec((1, SUBLANE, tn), lambda l, j: (l, 0, j)),
        out_shape=jax.ShapeDtypeStruct((DEPTH, SUBLANE, 6 * D_MODEL), F32),
        compiler_params=_cparams(("parallel", "parallel")),
        name="adaln_mod",
    )(cond, w_mod, b_mod.reshape(DEPTH, 1, 6 * D_MODEL))
    m = out[:, :N_COND].reshape(DEPTH, N_COND, 6, D_MODEL)
    return jnp.pad(m, ((0, 0), (0, 0), (0, SUBLANE - 6), (0, 0)))


def _mod_spec(tb):
    return pl.BlockSpec((1, SUBLANE, D_MODEL), lambda i: (_cond_of_block(i, tb), 0, 0))


def _const_spec(shape):
    nd = len(shape)
    return pl.BlockSpec(shape, lambda i: (0,) * nd)


def _gmlp_kernel(x_ref, mod_ref, g_ref, win_ref, bin_ref, gv_ref, ws_ref, bst_ref, wout_ref, o_ref):
    x = x_ref[...]
    m = mod_ref[0]
    h = _rms(x, g_ref[...]) * (1.0 + m[1:2]) + m[0:1]
    hb = h.astype(BF16)
    zv = jax.nn.gelu(_dot(hb, win_ref[:, GMLP_WIDTH:]) + bin_ref[:, GMLP_WIDTH:], approximate=True)
    vn = _rms(zv, gv_ref[...]).astype(BF16)
    rows = []
    for c in range(TB // CHUNK):
        cols = []
        for g in range(GMLP_GROUPS):
            blk = vn[c * CHUNK:(c + 1) * CHUNK, g * GROUP_W:(g + 1) * GROUP_W]
            cols.append(_dot(ws_ref[g], blk) + bst_ref[:, g:g + 1])
        rows.append(jnp.concatenate(cols, axis=1))
    vm = jnp.concatenate(rows, axis=0)
    u = jax.nn.gelu(_dot(hb, win_ref[:, :GMLP_WIDTH]) + bin_ref[:, :GMLP_WIDTH], approximate=True)
    d = _dot((u * vm).astype(BF16), wout_ref[...])
    o_ref[...] = x + m[2:3] * d


def _gmlp_layer(x, mod, g, w_in, b_in, g_v, w_s, b_s, w_out):
    return pl.pallas_call(
        _gmlp_kernel,
        grid=(N_TOK // TB,),
        in_specs=[
            pl.BlockSpec((TB, D_MODEL), lambda i: (i, 0)),
            _mod_spec(TB),
            _const_spec((1, D_MODEL)),
            _const_spec((D_MODEL, 2 * GMLP_WIDTH)),
            _const_spec((1, 2 * GMLP_WIDTH)),
            _const_spec((1, GMLP_WIDTH)),
            _const_spec((GMLP_GROUPS, CHUNK, CHUNK)),
            _const_spec((CHUNK, GMLP_GROUPS)),
            _const_spec((GMLP_WIDTH, D_MODEL)),
        ],
        out_specs=pl.BlockSpec((TB, D_MODEL), lambda i: (i, 0)),
        out_shape=jax.ShapeDtypeStruct((N_TOK, D_MODEL), F32),
        compiler_params=_cparams(("parallel",)),
        name="gmlp_mixer",
    )(x, mod, g.reshape(1, -1), w_in.astype(BF16), b_in.reshape(1, -1), g_v.reshape(1, -1),
      w_s.astype(BF16), b_s.T, w_out.astype(BF16))


def _rope_tables():
    t = jnp.arange(DEC_SEQ)
    row_id = (t // GRID_W).astype(F32)
    col_id = (t % GRID_W).astype(F32)
    axis_dim = QK_ROPE_DIM // 2
    inv_freq = ROPE_THETA ** (-jnp.arange(0, axis_dim, 2, dtype=F32) / axis_dim)
    ang = jnp.stack([row_id[:, None] * inv_freq, col_id[:, None] * inv_freq], axis=1)
    cos, sin = jnp.cos(ang), jnp.sin(ang)
    zeros = jnp.zeros_like(sin)
    cos_l = jnp.concatenate([cos, cos], axis=-1).reshape(DEC_SEQ, QK_ROPE_DIM)
    s1_l = jnp.concatenate([-sin, zeros], axis=-1).reshape(DEC_SEQ, QK_ROPE_DIM)
    s2_l = jnp.concatenate([zeros, sin], axis=-1).reshape(DEC_SEQ, QK_ROPE_DIM)

    def widen(rope_part, nope_fill):
        left = jnp.full((DEC_SEQ, QK_NOPE_DIM), nope_fill, F32)
        right = jnp.zeros((DEC_SEQ, HEAD_PAD - QK_HEAD_DIM), F32)
        return jnp.concatenate([left, rope_part, right], axis=-1)

    pos = jnp.stack([widen(cos_l, 1.0), widen(s1_l, 0.0), widen(s2_l, 0.0)])
    ident_c = jnp.concatenate([jnp.ones((TB, QK_HEAD_DIM), F32),
                               jnp.zeros((TB, HEAD_PAD - QK_HEAD_DIM), F32)], axis=-1)
    ident = jnp.stack([ident_c, jnp.zeros_like(ident_c), jnp.zeros_like(ident_c)])
    return jnp.concatenate([pos, ident], axis=1)


def _rope(xn, tab_ref):
    half = QK_ROPE_DIM // 4
    return (xn * tab_ref[0] + pltpu.roll(xn, HEAD_PAD - half, 1) * tab_ref[1]
            + pltpu.roll(xn, half, 1) * tab_ref[2])


def _dot_nt(a, b):
    return lax.dot_general(a, b, (((1,), (1,)), ((), ())), preferred_element_type=F32)


def _shift_rows(x, n):
    n = n % x.shape[0]
    return jnp.concatenate([x[n:], x[:n]], axis=0)


def _mla_proj_kernel(x_ref, mod_ref, g_ref, wdq_ref, nq_ref, wuqt_ref, gq_ref, wdkv_ref, nkv_ref,
                     tabt_ref, qt_ref, ckv_ref, krp_ref):
    x = x_ref[...]
    m = mod_ref[0]
    h = _rms(x, g_ref[...]) * (1.0 + m[1:2]) + m[0:1]
    hb = h.astype(BF16)
    cq = _rms(_dot(hb, wdq_ref[...]), nq_ref[...])
    qt = _dot_nt(wuqt_ref[...], cq.astype(BF16))
    gq = gq_ref[...]
    half = QK_ROPE_DIM // 4
    for hd in range(N_HEADS):
        qh = qt[hd * HEAD_PAD:(hd + 1) * HEAD_PAD, :]
        ss = jnp.sum(qh * qh, axis=0, keepdims=True) * (1.0 / QK_HEAD_DIM)
        qn = qh * lax.rsqrt(ss + RMS_EPS) * gq
        qr = qn * tabt_ref[0] + _shift_rows(qn, half) * tabt_ref[1] + _shift_rows(qn, -half) * tabt_ref[2]
        qt_ref[hd * HEAD_PAD:(hd + 1) * HEAD_PAD, :] = qr.astype(BF16)
    kva = _dot(hb, wdkv_ref[...])
    ckv_ref[...] = _rms(kva[:, :KV_RANK], nkv_ref[...])
    krp_ref[...] = kva[:, KV_RANK:]


def _pad_heads(w, per_head):
    k = w.shape[0]
    w = w.reshape(k, N_HEADS, per_head)
    return jnp.pad(w, ((0, 0), (0, 0), (0, HEAD_PAD - per_head))).reshape(k, QKV_W)


def _pad_gain(g):
    return jnp.pad(g, (0, HEAD_PAD - QK_HEAD_DIM)).reshape(1, HEAD_PAD)


def _mla_proj(x, mod, g, w_dq, norm_q, w_uq, gq, w_dkv, norm_kv, tab_t):
    wdkv = jnp.concatenate([w_dkv[:, :KV_RANK], jnp.zeros((D_MODEL, QK_NOPE_DIM), F32),
                            w_dkv[:, KV_RANK:], jnp.zeros((D_MODEL, HEAD_PAD - QK_HEAD_DIM), F32)], axis=1)
    n_pb = N_PROMPT // TB

    def tab_idx(i):
        return (0, 0, jnp.where(i < n_pb, ROPE_BLOCKS, (i - n_pb) % ROPE_BLOCKS))

    return pl.pallas_call(
        _mla_proj_kernel,
        grid=(N_TOK // TB,),
        in_specs=[
            pl.BlockSpec((TB, D_MODEL), lambda i: (i, 0)),
            _mod_spec(TB),
            _const_spec((1, D_MODEL)),
            _const_spec((D_MODEL, Q_RANK)),
            _const_spec((1, Q_RANK)),
            _const_spec((QKV_W, Q_RANK)),
            _const_spec((HEAD_PAD, 1)),
            _const_spec((D_MODEL, 2 * LANE)),
            _const_spec((1, KV_RANK)),
            pl.BlockSpec((3, HEAD_PAD, TB), tab_idx),
        ],
        out_specs=[
            pl.BlockSpec((QKV_W, TB), lambda i: (0, i)),
            pl.BlockSpec((TB, KV_RANK), lambda i: (i, 0)),
            pl.BlockSpec((TB, HEAD_PAD), lambda i: (i, 0)),
        ],
        out_shape=[
            jax.ShapeDtypeStruct((QKV_W, N_TOK), BF16),
            jax.ShapeDtypeStruct((N_TOK, KV_RANK), F32),
            jax.ShapeDtypeStruct((N_TOK, HEAD_PAD), F32),
        ],
        compiler_params=_cparams(("parallel",)),
        name="mla_proj",
    )(x, mod, g.reshape(1, -1), w_dq.astype(BF16), norm_q.reshape(1, -1),
      _pad_heads(w_uq, QK_HEAD_DIM).T.astype(BF16), _pad_gain(gq).reshape(HEAD_PAD, 1), wdkv.astype(BF16),
      norm_kv.reshape(1, -1), tab_t)


def _kv_expand_kernel(ckv_ref, krp_ref, wukv_ref, wukvt_ref, gk_ref, tab_ref, k_ref, vt_ref):
    ckv = ckv_ref[...].astype(BF16)
    kv = _dot(ckv, wukv_ref[...])
    kvt = _dot_nt(wukvt_ref[...], ckv)
    krp = krp_ref[...]
    gk = gk_ref[...]
    lane = lax.broadcasted_iota(jnp.int32, (TB, HEAD_PAD), 1)
    is_nope = lane < QK_NOPE_DIM
    row = lax.broadcasted_iota(jnp.int32, (HEAD_PAD, TB), 0)
    ones_row = jnp.where(row == 0, 1.0, 0.0)
    for hd in range(N_HEADS):
        kvh = kv[:, hd * HEAD_PAD:(hd + 1) * HEAD_PAD]
        kn = _rms(jnp.where(is_nope, kvh, krp), gk, QK_HEAD_DIM)
        k_ref[:, hd * HEAD_PAD:(hd + 1) * HEAD_PAD] = _rope(kn, tab_ref).astype(BF16)
        vth = kvt[hd * HEAD_PAD:(hd + 1) * HEAD_PAD, :]
        vt_ref[hd * HEAD_PAD:(hd + 1) * HEAD_PAD, :] = jnp.where(row < QK_NOPE_DIM, ones_row, vth).astype(BF16)


def _kv_expand(ckv_all, krp_all, w_ukv, gk, tab):
    n_sb = DEC_BATCH * KV_LEN // TB
    per = KV_LEN // TB
    n_cache = PAST_LEN // TB

    def tab_idx(i):
        j = i % per
        blk = jnp.where((i >= n_sb) | (j < n_cache), ROPE_BLOCKS, j - n_cache)
        return (0, blk, 0)

    return pl.pallas_call(
        _kv_expand_kernel,
        grid=(N_KV_ROWS // TB,),
        in_specs=[
            pl.BlockSpec((TB, KV_RANK), lambda i: (i, 0)),
            pl.BlockSpec((TB, HEAD_PAD), lambda i: (i, 0)),
            _const_spec((KV_RANK, QKV_W)),
            _const_spec((QKV_W, KV_RANK)),
            _const_spec((1, HEAD_PAD)),
            pl.BlockSpec((3, TB, HEAD_PAD), tab_idx),
        ],
        out_specs=[
            pl.BlockSpec((TB, QKV_W), lambda i: (i, 0)),
            pl.BlockSpec((QKV_W, TB), lambda i: (0, i)),
        ],
        out_shape=[
            jax.ShapeDtypeStruct((N_KV_ROWS, QKV_W), BF16),
            jax.ShapeDtypeStruct((QKV_W, N_KV_ROWS), BF16),
        ],
        compiler_params=_cparams(("parallel",)),
        name="mla_kv_expand",
    )(ckv_all, krp_all, w_ukv.astype(BF16), w_ukv.T.astype(BF16), _pad_gain(gk), tab)


def _attn_kernel(qt_ref, k_ref, vt_ref, o_ref, s_a, s_b, *, t_k, hps):
    c = (1.0 / math.sqrt(QK_HEAD_DIM)) * math.log2(math.e)
    kc = min(TKC, t_k)
    n_chunks = t_k // kc
    bufs = (s_a, s_b)

    def rows(h):
        return slice(h * HEAD_PAD, (h + 1) * HEAD_PAD)

    def scores(h, j, m8):
        st = _dot(k_ref[j * kc:(j + 1) * kc, rows(h)], qt_ref[rows(h), :])
        bufs[h % 2][j * kc:(j + 1) * kc, :] = st
        return jnp.maximum(m8, jnp.max(st.reshape(kc // SUBLANE, SUBLANE, TQ), axis=0))

    def weigh(h, j, m, acc):
        pt = jnp.exp2((bufs[h % 2][j * kc:(j + 1) * kc, :] - m) * c).astype(BF16)
        return acc + _dot(vt_ref[rows(h), j * kc:(j + 1) * kc], pt)

    m8_init = jnp.full((SUBLANE, TQ), NEG_BIG, F32)
    outs = []
    m8 = m8_init
    for j in range(n_chunks):
        m8 = scores(0, j, m8)
    for h in range(1, hps + 1):
        m = jnp.max(m8, axis=0, keepdims=True)
        acc = jnp.zeros((HEAD_PAD, TQ), F32)
        m8 = m8_init
        for j in range(n_chunks):
            if h < hps:
                m8 = scores(h, j, m8)
            acc = weigh(h - 1, j, m, acc)
        outs.append(acc[QK_NOPE_DIM:, :] / acc[0:1, :])
    o_ref[...] = jnp.concatenate(outs, axis=0).T.astype(BF16)


def _attention(qt, k, vt, *, n_batch, t_q, t_k, q_row0, kv_row0, hps):
    nq = t_q // TQ
    q0 = q_row0 // TQ
    k0 = kv_row0 // t_k
    return pl.pallas_call(
        functools.partial(_attn_kernel, t_k=t_k, hps=hps),
        grid=(n_batch, N_HEADS // hps, nq),
        in_specs=[
            pl.BlockSpec((hps * HEAD_PAD, TQ), lambda b, h, i: (h, q0 + b * nq + i)),
            pl.BlockSpec((t_k, hps * HEAD_PAD), lambda b, h, i: (k0 + b, h)),
            pl.BlockSpec((hps * HEAD_PAD, t_k), lambda b, h, i: (h, k0 + b)),
        ],
        out_specs=pl.BlockSpec((TQ, hps * V_HEAD_DIM), lambda b, h, i: (b * nq + i, h)),
        out_shape=jax.ShapeDtypeStruct((n_batch * t_q, N_HEADS * V_HEAD_DIM), BF16),
        scratch_shapes=[pltpu.VMEM((t_k, TQ), F32), pltpu.VMEM((t_k, TQ), F32)],
        compiler_params=_cparams(("parallel", "parallel", "parallel")),
        name=f"mla_attention_tk{t_k}",
    )(qt, k, vt)


def _out_proj_kernel(ap_ref, as_ref, x_ref, mod_ref, wo_ref, o_ref):
    a = jnp.where(pl.program_id(0) < N_PROMPT // TB, ap_ref[...], as_ref[...])
    o_ref[...] = x_ref[...] + mod_ref[0][2:3] * _dot(a, wo_ref[...])


def _out_proj(attn_p, attn_s, x, mod, w_o):
    n_pb = N_PROMPT // TB
    return pl.pallas_call(
        _out_proj_kernel,
        grid=(N_TOK // TB,),
        in_specs=[
            pl.BlockSpec((TB, D_MODEL), lambda i: (jnp.minimum(i, n_pb - 1), 0)),
            pl.BlockSpec((TB, D_MODEL), lambda i: (jnp.maximum(i - n_pb, 0), 0)),
            pl.BlockSpec((TB, D_MODEL), lambda i: (i, 0)),
            _mod_spec(TB),
            _const_spec((D_MODEL, D_MODEL)),
        ],
        out_specs=pl.BlockSpec((TB, D_MODEL), lambda i: (i, 0)),
        out_shape=jax.ShapeDtypeStruct((N_TOK, D_MODEL), F32),
        compiler_params=_cparams(("parallel",)),
        name="mla_out_proj",
    )(attn_p, attn_s, x, mod, w_o.astype(BF16))


def _mla_layer(x, mod, g, cache_ckv, cache_krope, tab, w_dq, norm_q, w_uq, w_dkv, norm_kv, w_ukv,
               gq, gk, w_o):
    tab_t = jnp.swapaxes(tab, 1, 2)
    qt, ckv, krp = _mla_proj(x, mod, g, w_dq, norm_q, w_uq, gq, w_dkv, norm_kv, tab_t)
    cache_krp = jnp.pad(cache_krope, ((0, 0), (0, 0), (QK_NOPE_DIM, HEAD_PAD - QK_HEAD_DIM)))
    ckv_s = ckv[N_PROMPT:].reshape(DEC_BATCH, DEC_SEQ, KV_RANK)
    krp_s = krp[N_PROMPT:].reshape(DEC_BATCH, DEC_SEQ, HEAD_PAD)
    ckv_all = jnp.concatenate([jnp.concatenate([cache_ckv, ckv_s], axis=1).reshape(-1, KV_RANK),
                               ckv[:N_PROMPT]], axis=0)
    krp_all = jnp.concatenate([jnp.concatenate([cache_krp, krp_s], axis=1).reshape(-1, HEAD_PAD),
                               krp[:N_PROMPT]], axis=0)
    k, vt = _kv_expand(ckv_all, krp_all, w_ukv, gk, tab)
    a_p = _attention(qt, k, vt, n_batch=BATCH, t_q=SEQ, t_k=SEQ, q_row0=0, kv_row0=DEC_BATCH * KV_LEN,
                     hps=N_HEADS)
    a_s = _attention(qt, k, vt, n_batch=DEC_BATCH, t_q=DEC_SEQ, t_k=KV_LEN, q_row0=N_PROMPT, kv_row0=0,
                     hps=4)
    x = _out_proj(a_p, a_s, x, mod, w_o)
    state_ckv = ckv[:N_PROMPT].reshape(BATCH, SEQ, KV_RANK)
    state_krope = krp[:N_PROMPT, QK_NOPE_DIM:QK_HEAD_DIM].reshape(BATCH, SEQ, QK_ROPE_DIM)
    return x, state_ckv, state_krope


def _tile_plan(cnt8, plan_ref, start_ref):
    tm = float(TM)
    cnt_row = cnt8[0:1, :]
    cnt_col = cnt8.T[:, 0:1]
    tiles_row = jnp.floor((cnt_row + (tm - 1.0)) * (1.0 / tm))
    tiles_col = jnp.floor((cnt_col + (tm - 1.0)) * (1.0 / tm))
    sub = lax.broadcasted_iota(jnp.int32, (LANE, LANE), 0).astype(F32)
    lan = lax.broadcasted_iota(jnp.int32, (LANE, LANE), 1).astype(F32)
    tile_end_row = jnp.sum(jnp.where(sub <= lan, tiles_col, 0.0), axis=0, keepdims=True)
    tile_end_col = jnp.sum(jnp.where(lan <= sub, tiles_row, 0.0), axis=1, keepdims=True)
    n_used = jnp.max(tile_end_row, axis=1, keepdims=True)
    start_col = (tile_end_col - tiles_col) * tm
    end_col = start_col + cnt_col
    cand = jnp.where(jnp.logical_and(lan > sub, tiles_row > 0.0), lan, float(LANE))
    next_col = jnp.min(cand, axis=1, keepdims=True)
    next_col = jnp.where(next_col < float(LANE), next_col, -1.0)
    n_lanes = plan_ref.shape[1]
    tidx = jnp.minimum(lax.broadcasted_iota(jnp.int32, (LANE, n_lanes), 1).astype(F32), n_used - 1.0)
    esub = lax.broadcasted_iota(jnp.int32, (LANE, n_lanes), 0).astype(F32)
    te_row = jnp.sum(jnp.where(tile_end_col <= tidx, 1.0, 0.0), axis=0, keepdims=True)
    mine = esub == te_row
    end_at = jnp.sum(jnp.where(mine, end_col, 0.0), axis=0, keepdims=True)
    tv_row = jnp.clip(end_at - tidx[0:1, :] * tm, 0.0, tm)
    nx_row = jnp.sum(jnp.where(mine, next_col, 0.0), axis=0, keepdims=True)
    nu_row = jnp.broadcast_to(n_used, (1, n_lanes))
    plan_ref[...] = jnp.concatenate([te_row, tv_row, nx_row, nu_row, jnp.zeros((SUBLANE - 4, n_lanes), F32)], axis=0)
    start_ref[...] = jnp.broadcast_to(start_col, (LANE, LANE))


def _route_kernel(x_ref, mod_ref, g_ref, wr_ref, br_ref, tri_ref, h_ref, meta_ref, metat_ref, plan_ref, start_ref,
                  carry):
    i = pl.program_id(0)

    @pl.when(i == 0)
    def _():
        carry[...] = jnp.zeros_like(carry)

    x = x_ref[...]
    m = mod_ref[0]
    h = _rms(x, g_ref[...]) * (1.0 + m[4:5]) + m[3:4]
    h_ref[...] = _pack_bf16_pairs(h)
    logits = _dot(h.astype(BF16), wr_ref[...]) + br_ref[...]
    lane = lax.broadcasted_iota(jnp.int32, logits.shape, 1).astype(F32)
    work = logits
    sel = jnp.zeros(logits.shape, F32)
    hits, tops = [], []
    for k in range(TOP_K):
        mk = jnp.max(work, axis=-1, keepdims=True)
        first = jnp.min(jnp.where(work == mk, lane, float(LANE)), axis=-1, keepdims=True)
        hit = lane == first
        sel = jnp.where(hit, 1.0, sel)
        work = jnp.where(hit, -jnp.inf, work)
        hits.append((hit, first))
        tops.append(mk)
    es = [jnp.exp(t - tops[0]) for t in tops]
    denom = es[0] + es[1] + es[2] + es[3]
    pos = _dot(tri_ref[...], sel.astype(BF16)) + carry[0:1, :]
    carry[...] = carry[...] + jnp.sum(sel, axis=0, keepdims=True)

    @pl.when(i == pl.num_programs(0) - 1)
    def _():
        _tile_plan(carry[...], plan_ref, start_ref)

    meta = jnp.zeros(logits.shape, F32)
    for k in range(TOP_K):
        hit, first = hits[k]
        pk = jnp.sum(jnp.where(hit, pos, 0.0), axis=-1, keepdims=True)
        meta = jnp.where(lane == float(k), first, meta)
        meta = jnp.where(lane == float(TOP_K + k), es[k] / denom, meta)
        meta = jnp.where(lane == float(2 * TOP_K + k), pk, meta)
    meta_ref[...] = meta
    metat_ref[...] = meta.T[:2 * SUBLANE, :]


def _route(x, mod, g, w_router, b_router):
    wr = jnp.pad(w_router, ((0, 0), (0, LANE - N_EXPERTS))).astype(BF16)
    br = jnp.pad(b_router, (0, LANE - N_EXPERTS), constant_values=NEG_BIG).reshape(1, LANE)
    tri = jnp.tri(TB, TB, -1, dtype=BF16)
    return pl.pallas_call(
        _route_kernel,
        grid=(N_TOK // TB,),
        in_specs=[
            pl.BlockSpec((TB, D_MODEL), lambda i: (i, 0)),
            _mod_spec(TB),
            _const_spec((1, D_MODEL)),
            _const_spec((D_MODEL, LANE)),
            _const_spec((1, LANE)),
            _const_spec((TB, TB)),
        ],
        out_specs=[
            pl.BlockSpec((TB, D_MODEL // 2), lambda i: (i, 0)),
            pl.BlockSpec((TB, LANE), lambda i: (i, 0)),
            pl.BlockSpec((2 * SUBLANE, TB), lambda i: (0, i)),
            _const_spec((SUBLANE, PLAN_LANES)),
            _const_spec((LANE, LANE)),
        ],
        out_shape=[
            jax.ShapeDtypeStruct((N_TOK, D_MODEL // 2), jnp.uint32),
            jax.ShapeDtypeStruct((N_TOK, LANE), F32),
            jax.ShapeDtypeStruct((2 * SUBLANE, N_TOK), F32),
            jax.ShapeDtypeStruct((SUBLANE, PLAN_LANES), F32),
            jax.ShapeDtypeStruct((LANE, LANE), F32),
        ],
        scratch_shapes=[pltpu.VMEM((SUBLANE, LANE), F32)],
        compiler_params=_cparams(("arbitrary",)),
        name="moe_route",
    )(x, mod, g.reshape(1, -1), wr, br, tri)


def _slots_kernel(start_ref, metat_ref, dest_ref):
    start_col = start_ref[:, 0:1]
    esub = lax.broadcasted_iota(jnp.int32, (LANE, SLOT_TB), 0).astype(F32)
    rows = []
    for k in range(TOP_K):
        e = metat_ref[k:k + 1, :]
        first = jnp.sum(jnp.where(esub == e, start_col, 0.0), axis=0, keepdims=True)
        rows.append(first + metat_ref[2 * TOP_K + k:2 * TOP_K + k + 1, :])
    dest_ref[...] = jnp.concatenate(rows, axis=0).astype(jnp.int32)


def _slots(start, meta_t):
    return pl.pallas_call(
        _slots_kernel,
        grid=(N_TOK // SLOT_TB,),
        in_specs=[
            _const_spec((LANE, LANE)),
            pl.BlockSpec((2 * SUBLANE, SLOT_TB), lambda i: (0, i)),
        ],
        out_specs=pl.BlockSpec((TOP_K, SLOT_TB), lambda i: (0, i)),
        out_shape=jax.ShapeDtypeStruct((TOP_K, N_TOK), jnp.int32),
        compiler_params=_cparams(("parallel",)),
        name="moe_slots",
    )(start, meta_t)


def _sc_gather(table, idx, ch):
    b, w = idx.shape[0], table.shape[1]
    per_w = b // SC_WORKERS
    n_ch = per_w // ch
    assert per_w * SC_WORKERS == b and n_ch * ch == per_w and n_ch % 2 == 0
    mesh = plsc.VectorSubcoreMesh(core_axis_name="c", subcore_axis_name="s")

    @functools.partial(
        pl.kernel, mesh=mesh,
        out_type=jax.ShapeDtypeStruct((b, w), table.dtype),
        scratch_types=[
            pltpu.VMEM((n_ch, ch), jnp.int32),
            pltpu.VMEM((ch, w), table.dtype),
            pltpu.VMEM((ch, w), table.dtype),
            pltpu.SemaphoreType.DMA, pltpu.SemaphoreType.DMA,
            pltpu.SemaphoreType.DMA, pltpu.SemaphoreType.DMA,
        ],
        name="sc_row_gather",
    )
    def gather_rows(table_hbm, idx_hbm, out_hbm, idx_v, buf0, buf1, g0, g1, s0, s1):
        wid = lax.axis_index("s") * SC_CORES + lax.axis_index("c")
        base = wid * per_w
        pltpu.sync_copy(idx_hbm.at[wid], idx_v)

        def gather(j, buf, sem):
            return pltpu.make_async_copy(table_hbm.at[idx_v.at[j]], buf, sem)

        def store(j, buf, sem):
            return pltpu.make_async_copy(buf, out_hbm.at[pl.ds(base + j * ch, ch)], sem)

        gather(0, buf0, g0).start()

        @pl.loop(0, n_ch, step=2)
        def _(j):
            @pl.when(j > 0)
            def _():
                store(j - 1, buf1, s1).wait()

            gather(j + 1, buf1, g1).start()
            gather(j, buf0, g0).wait()
            store(j, buf0, s0).start()
            gather(j + 1, buf1, g1).wait()
            store(j + 1, buf1, s1).start()
            store(j, buf0, s0).wait()

            @pl.when(j + 2 < n_ch)
            def _():
                gather(j + 2, buf0, g0).start()

        store(n_ch - 1, buf1, s1).wait()

    return gather_rows(table, idx.reshape(SC_WORKERS, n_ch, ch))


def _sc_dispatch(rows, dest_t, ch):
    n, w = rows.shape
    per_w = n // SC_WORKERS
    n_ch = per_w // ch
    assert per_w * SC_WORKERS == n and n_ch * ch == per_w and n_ch % 2 == 0
    mesh = plsc.VectorSubcoreMesh(core_axis_name="c", subcore_axis_name="s")
    idx = dest_t.reshape(TOP_K, SC_WORKERS, n_ch, ch)

    @functools.partial(
        pl.kernel, mesh=mesh,
        out_type=jax.ShapeDtypeStruct((N_SLOTS, w), rows.dtype),
        scratch_types=[
            pltpu.VMEM((TOP_K * n_ch, ch), jnp.int32),
            pltpu.VMEM((ch, w), rows.dtype),
            pltpu.VMEM((ch, w), rows.dtype),
            pltpu.SemaphoreType.DMA, pltpu.SemaphoreType.DMA,
            pltpu.SemaphoreType.DMA, pltpu.SemaphoreType.DMA,
        ],
        name="sc_row_dispatch",
    )
    def dispatch_rows(rows_hbm, idx_hbm, out_hbm, idx_v, buf0, buf1, l0, l1, s0, s1):
        wid = lax.axis_index("s") * SC_CORES + lax.axis_index("c")
        base = wid * per_w
        for k in range(TOP_K):
            pltpu.sync_copy(idx_hbm.at[k, wid], idx_v.at[pl.ds(k * n_ch, n_ch)])

        def load(j, buf, sem):
            return pltpu.make_async_copy(rows_hbm.at[pl.ds(base + j * ch, ch)], buf, sem)

        def scatter(j, k, buf, sem):
            return pltpu.make_async_copy(buf, out_hbm.at[idx_v.at[k * n_ch + j]], sem)

        load(0, buf0, l0).start()

        @pl.loop(0, n_ch, step=2)
        def _(j):
            load(j + 1, buf1, l1).start()
            load(j, buf0, l0).wait()
            for k in range(TOP_K):
                scatter(j, k, buf0, s0).start()
            load(j + 1, buf1, l1).wait()
            for k in range(TOP_K):
                scatter(j + 1, k, buf1, s1).start()
            for k in range(TOP_K):
                scatter(j, k, buf0, s0).wait()

            @pl.when(j + 2 < n_ch)
            def _():
                load(j + 2, buf0, l0).start()

            for k in range(TOP_K):
                scatter(j + 1, k, buf1, s1).wait()

    return dispatch_rows(rows, idx)


def _deinterleave_matrix():
    src = jnp.arange(2 * LANE)[:, None]
    dst = jnp.arange(2 * LANE)[None, :]
    want = jnp.where(dst < LANE, 2 * dst, 2 * (dst - LANE) + 1)
    return (src == want).astype(BF16)


def _expert_kernel(te_ref, nu_ref, tv_ref, nx_ref, x_ref, wgu_hbm, bgu_ref, wd_hbm, bd_ref, perm_ref, o_ref,
                   wgu_st, wd_st, wgu_bf, wd_bf, sems, *, layer):
    i = pl.program_id(0)
    prev = te_ref[jnp.maximum(i - 1, 0)]
    fresh = jnp.logical_or(i == 0, te_ref[i] != prev)

    def fetch(e):
        return (pltpu.make_async_copy(wgu_hbm.at[layer, e], wgu_st, sems.at[0]),
                pltpu.make_async_copy(wd_hbm.at[layer, e], wd_st, sems.at[1]))

    @pl.when(i == 0)
    def _():
        for cp in fetch(te_ref[0]):
            cp.start()

    @pl.when(jnp.logical_and(fresh, i < nu_ref[0]))
    def _():
        for cp in fetch(te_ref[i]):
            cp.wait()
        for b in range(2 * D_FF // (2 * LANE)):
            sl = slice(b * 2 * LANE, (b + 1) * 2 * LANE)
            wgu_bf[:, sl] = _dot(wgu_st[:, sl].astype(BF16), perm_ref[...]).astype(BF16)
        wd_bf[...] = wd_st[...].astype(BF16)

        @pl.when(nx_ref[i] >= 0)
        def _():
            for cp in fetch(nx_ref[i]):
                cp.start()

    @pl.when(i < nu_ref[0])
    def _():
        row = lax.broadcasted_iota(jnp.int32, (TM, D_MODEL // 2), 0)
        w = jnp.where(row < tv_ref[i], x_ref[...], jnp.uint32(0))
        x = _unpack_bf16_pairs(w).astype(BF16)
        gu = _dot(x, wgu_bf[...]) + bgu_ref[...]
        acts = []
        for b in range(D_FF // LANE):
            glu = jnp.minimum(gu[:, b * 2 * LANE:b * 2 * LANE + LANE], SWIGLU_LIMIT)
            lin = jnp.clip(gu[:, b * 2 * LANE + LANE:(b + 1) * 2 * LANE], -SWIGLU_LIMIT, SWIGLU_LIMIT)
            acts.append((glu * jax.nn.sigmoid(SWIGLU_ALPHA * glu) * (lin + 1.0)).astype(BF16))
        act = jnp.concatenate(acts, axis=1)
        o_ref[...] = _pack_bf16_pairs(_dot(act, wd_bf[...]) + bd_ref[...])


def _experts(buf, tile_expert, n_used, tile_valid, tile_next, layer, w_gu, b_gu, w_down, b_down):
    bgu = b_gu.reshape(N_EXPERTS, D_FF // LANE, LANE, 2).transpose(0, 1, 3, 2).reshape(N_EXPERTS, 1, 2 * D_FF)

    def row_idx(i, te, nu, tv, nx):
        return (jnp.minimum(i, nu[0] - 1), 0)

    def b_idx(i, te, nu, tv, nx):
        return (te[i], 0, 0)

    grid_spec = pltpu.PrefetchScalarGridSpec(
        num_scalar_prefetch=4,
        grid=(N_TILES,),
        in_specs=[
            pl.BlockSpec((TM, D_MODEL // 2), row_idx),
            pl.BlockSpec(memory_space=pl.ANY),
            pl.BlockSpec((None, 1, 2 * D_FF), b_idx),
            pl.BlockSpec(memory_space=pl.ANY),
            pl.BlockSpec((None, 1, D_MODEL), b_idx),
            pl.BlockSpec((2 * LANE, 2 * LANE), lambda i, te, nu, tv, nx: (0, 0)),
        ],
        out_specs=pl.BlockSpec((TM, D_MODEL // 2), row_idx),
        scratch_shapes=[
            pltpu.VMEM((D_MODEL, 2 * D_FF), F32),
            pltpu.VMEM((D_FF, D_MODEL), F32),
            pltpu.VMEM((D_MODEL, 2 * D_FF), BF16),
            pltpu.VMEM((D_FF, D_MODEL), BF16),
            pltpu.SemaphoreType.DMA((2,)),
        ],
    )
    return pl.pallas_call(
        functools.partial(_expert_kernel, layer=layer),
        grid_spec=grid_spec,
        out_shape=jax.ShapeDtypeStruct((N_SLOTS, D_MODEL // 2), jnp.uint32),
        compiler_params=_cparams(("arbitrary",)),
        name="moe_experts",
    )(tile_expert, n_used, tile_valid, tile_next, buf, w_gu, bgu, w_down,
      b_down.reshape(N_EXPERTS, 1, D_MODEL), _deinterleave_matrix())


def _pack_bf16_pairs(v):
    half = v.shape[1] // 2
    bits = pltpu.bitcast(v.astype(BF16).astype(F32), jnp.uint32)
    return (bits[:, half:] & jnp.uint32(0xFFFF0000)) | (bits[:, :half] >> 16)


def _unpack_bf16_pairs(w):
    return jnp.concatenate([pltpu.bitcast(w << 16, F32), pltpu.bitcast(w & jnp.uint32(0xFFFF0000), F32)],
                           axis=1)


def _combine_kernel(x_ref, mod_ref, y_ref, w_ref, o_ref):
    w = w_ref[:, TOP_K:2 * TOP_K]
    y = _unpack_bf16_pairs(y_ref[0]) * w[:, 0:1]
    for k in range(1, TOP_K):
        y = y + _unpack_bf16_pairs(y_ref[k]) * w[:, k:k + 1]
    o_ref[...] = x_ref[...] + mod_ref[0][5:6] * y


def _combine(x, mod, y4, meta):
    return pl.pallas_call(
        _combine_kernel,
        grid=(N_TOK // TB,),
        in_specs=[
            pl.BlockSpec((TB, D_MODEL), lambda i: (i, 0)),
            _mod_spec(TB),
            pl.BlockSpec((TOP_K, TB, D_MODEL // 2), lambda i: (0, i, 0)),
            pl.BlockSpec((TB, LANE), lambda i: (i, 0)),
        ],
        out_specs=pl.BlockSpec((TB, D_MODEL), lambda i: (i, 0)),
        out_shape=jax.ShapeDtypeStruct((N_TOK, D_MODEL), F32),
        compiler_params=_cparams(("parallel",)),
        name="moe_combine",
    )(x, mod, y4, meta)


def _moe_layer(x, mod, g, layer, w_router, b_router, w_gu, b_gu, w_down, b_down):
    hp, meta, meta_t, plan, start = _route(x, mod, g, w_router, b_router)
    plan = plan[:4, :N_TILES].astype(jnp.int32)
    tile_expert, tile_valid, tile_next, n_used = plan[0], plan[1], plan[2], plan[3, :1]
    dest_t = _slots(start, meta_t)
    buf = _sc_dispatch(hp, dest_t, 64)
    yb = _experts(buf, tile_expert, n_used, tile_valid, tile_next, layer, w_gu, b_gu, w_down, b_down)
    y4 = _sc_gather(yb, dest_t.reshape(-1), 64).reshape(TOP_K, N_TOK, D_MODEL // 2)
    return _combine(x, mod, y4, meta)


def kernel(x_prompt, x_sample, c, cache_ckv, cache_krope, c_ctx, norm_mix_g, norm_ffn_g, w_mod, b_mod,
           g_w_in, g_b_in, g_norm_v, g_w_s, g_b_s, g_w_out, m_w_dq, m_norm_q, m_w_uq, m_w_dkv,
           m_norm_kv, m_w_ukv, m_qk_norm_q, m_qk_norm_k, m_w_o, e_w_router, e_b_router, e_w_gu,
           e_b_gu, e_w_down, e_b_down):
    x = jnp.concatenate([x_prompt.reshape(N_PROMPT, D_MODEL), x_sample.reshape(N_SAMPLE, D_MODEL)], axis=0)
    cond = jnp.concatenate([c_ctx[None, :], c, jnp.zeros((SUBLANE - N_COND, D_MODEL), F32)], axis=0)
    mod = _modulation(cond, w_mod, b_mod)
    tab = _rope_tables()
    ckv_states, krope_states = [], []
    for layer in range(DEPTH):
        j = layer // 2
        if layer % 2 == 0:
            x = _gmlp_layer(x, mod[layer], norm_mix_g[layer], g_w_in[j], g_b_in[j], g_norm_v[j],
                            g_w_s[j], g_b_s[j], g_w_out[j])
        else:
            x, s_ckv, s_krope = _mla_layer(
                x, mod[layer], norm_mix_g[layer], cache_ckv[:, j], cache_krope[:, j], tab,
                m_w_dq[j], m_norm_q[j], m_w_uq[j], m_w_dkv[j], m_norm_kv[j], m_w_ukv[j],
                m_qk_norm_q[j], m_qk_norm_k[j], m_w_o[j])
            ckv_states.append(s_ckv)
            krope_states.append(s_krope)
        x = _moe_layer(x, mod[layer], norm_ffn_g[layer], layer, e_w_router[layer], e_b_router[layer],
                       e_w_gu, e_b_gu[layer], e_w_down, e_b_down[layer])
    y_prompt = x[:N_PROMPT].reshape(BATCH, SEQ, D_MODEL)
    y_sample = x[N_PROMPT:].reshape(DEC_BATCH, DEC_SEQ, D_MODEL)
    return (y_prompt, y_sample, jnp.stack(ckv_states, axis=1), jnp.stack(krope_states, axis=1))
```

```python
import functools
import math

import jax
import jax.numpy as jnp
from jax import lax
from jax.experimental import pallas as pl
from jax.experimental.pallas import tpu as pltpu
from jax.experimental.pallas import tpu_sc as plsc

F32 = jnp.float32
BF16 = jnp.bfloat16

D_MODEL = 1024
BATCH = 32
SEQ = 256
DEPTH = 4
DEC_BATCH = 2
DEC_SEQ = 4096
PAST_LEN = 512
GRID_W = 64
RMS_EPS = 1e-6
GMLP_WIDTH = 2 * D_MODEL
GMLP_GROUPS = 8
GROUP_W = GMLP_WIDTH // GMLP_GROUPS
CHUNK = 128
N_HEADS = 16
QK_NOPE_DIM = 64
QK_ROPE_DIM = 32
QK_HEAD_DIM = QK_NOPE_DIM + QK_ROPE_DIM
V_HEAD_DIM = 64
Q_RANK = 256
KV_RANK = 128
ROPE_THETA = 10000.0
N_EXPERTS = 32
TOP_K = 4
D_FF = D_MODEL
SWIGLU_LIMIT = 7.0
SWIGLU_ALPHA = 1.702

N_PROMPT = BATCH * SEQ
N_SAMPLE = DEC_BATCH * DEC_SEQ
N_TOK = N_PROMPT + N_SAMPLE
N_COND = 1 + DEC_BATCH
KV_LEN = PAST_LEN + DEC_SEQ
N_KV_ROWS = DEC_BATCH * KV_LEN + N_PROMPT

LANE = 128
SUBLANE = 8
HEAD_PAD = LANE
QKV_W = N_HEADS * HEAD_PAD
VMEM_LIMIT = 56 * 1024 * 1024

TB = 256
TQ = 256
TKC = 256
ATTN_CHUNKS_PER_ITER = 9
TM = 512
N_TILES = N_TOK * TOP_K // TM + N_EXPERTS
N_SLOTS = N_TILES * TM
PLAN_LANES = -(-N_TILES // LANE) * LANE
SLOT_TB = 2048
ROUTE_TB = 512
SC_CORES = 2
SC_WORKERS = SC_CORES * 16
ROPE_BLOCKS = DEC_SEQ // TB
NEG_BIG = -1e30


def _cparams(sem):
    return pltpu.CompilerParams(dimension_semantics=sem, vmem_limit_bytes=VMEM_LIMIT)


def _cond_of_block(i, tb):
    n_p = N_PROMPT // tb
    per = DEC_SEQ // tb
    return jnp.where(i < n_p, 0, 1 + (i - n_p) // per)


def _rms(x, g, n=None):
    n = x.shape[-1] if n is None else n
    ss = jnp.sum(x * x, axis=-1, keepdims=True) * (1.0 / n)
    return x * lax.rsqrt(ss + RMS_EPS) * g


def _dot(a, b):
    return jnp.dot(a, b, preferred_element_type=F32)


def _mod_kernel(c_ref, w_ref, b_ref, o_ref):
    c = c_ref[...]
    s = c * jax.nn.sigmoid(c)
    o_ref[0] = _dot(s.astype(BF16), w_ref[0].astype(BF16)) + b_ref[0]


def _modulation(cond, w_mod, b_mod):
    tn = 1536
    out = pl.pallas_call(
        _mod_kernel,
        grid=(DEPTH, 6 * D_MODEL // tn),
        in_specs=[
            pl.BlockSpec((SUBLANE, D_MODEL), lambda l, j: (0, 0)),
            pl.BlockSpec((1, D_MODEL, tn), lambda l, j: (l, 0, j)),
            pl.BlockSpec((1, 1, tn), lambda l, j: (l, 0, j)),
        ],
        out_specs=pl.BlockSpec((1, SUBLANE, tn), lambda l, j: (l, 0, j)),
        out_shape=jax.ShapeDtypeStruct((DEPTH, SUBLANE, 6 * D_MODEL), F32),
        compiler_params=_cparams(("parallel", "parallel")),
        name="adaln_mod",
    )(cond, w_mod, b_mod.reshape(DEPTH, 1, 6 * D_MODEL))
    m = out[:, :N_COND].reshape(DEPTH, N_COND, 6, D_MODEL)
    return jnp.pad(m, ((0, 0), (0, 0), (0, SUBLANE - 6), (0, 0)))


def _mod_spec(tb):
    return pl.BlockSpec((1, SUBLANE, D_MODEL), lambda i: (_cond_of_block(i, tb), 0, 0))


def _const_spec(shape):
    nd = len(shape)
    return pl.BlockSpec(shape, lambda i: (0,) * nd)


def _gmlp_kernel(x_ref, mod_ref, g_ref, win_ref, bin_ref, gv_ref, ws_ref, bst_ref, wout_ref, o_ref):
    x = x_ref[...]
    m = mod_ref[0]
    h = _rms(x, g_ref[...]) * (1.0 + m[1:2]) + m[0:1]
    hb = h.astype(BF16)
    zv = jax.nn.gelu(_dot(hb, win_ref[:, GMLP_WIDTH:]) + bin_ref[:, GMLP_WIDTH:], approximate=True)
    vn = _rms(zv, gv_ref[...]).astype(BF16)
    rows = []
    for c in range(TB // CHUNK):
        cols = []
        for g in range(GMLP_GROUPS):
            blk = vn[c * CHUNK:(c + 1) * CHUNK, g * GROUP_W:(g + 1) * GROUP_W]
            cols.append(_dot(ws_ref[g], blk) + bst_ref[:, g:g + 1])
        rows.append(jnp.concatenate(cols, axis=1))
    vm = jnp.concatenate(rows, axis=0)
    u = jax.nn.gelu(_dot(hb, win_ref[:, :GMLP_WIDTH]) + bin_ref[:, :GMLP_WIDTH], approximate=True)
    d = _dot((u * vm).astype(BF16), wout_ref[...])
    o_ref[...] = x + m[2:3] * d


def _gmlp_layer(x, mod, g, w_in, b_in, g_v, w_s, b_s, w_out):
    return pl.pallas_call(
        _gmlp_kernel,
        grid=(N_TOK // TB,),
        in_specs=[
            pl.BlockSpec((TB, D_MODEL), lambda i: (i, 0)),
            _mod_spec(TB),
            _const_spec((1, D_MODEL)),
            _const_spec((D_MODEL, 2 * GMLP_WIDTH)),
            _const_spec((1, 2 * GMLP_WIDTH)),
            _const_spec((1, GMLP_WIDTH)),
            _const_spec((GMLP_GROUPS, CHUNK, CHUNK)),
            _const_spec((CHUNK, GMLP_GROUPS)),
            _const_spec((GMLP_WIDTH, D_MODEL)),
        ],
        out_specs=pl.BlockSpec((TB, D_MODEL), lambda i: (i, 0)),
        out_shape=jax.ShapeDtypeStruct((N_TOK, D_MODEL), F32),
        compiler_params=_cparams(("parallel",)),
        name="gmlp_mixer",
    )(x, mod, g.reshape(1, -1), w_in.astype(BF16), b_in.reshape(1, -1), g_v.reshape(1, -1),
      w_s.astype(BF16), b_s.T, w_out.astype(BF16))


def _rope_tables():
    t = jnp.arange(DEC_SEQ)
    row_id = (t // GRID_W).astype(F32)
    col_id = (t % GRID_W).astype(F32)
    axis_dim = QK_ROPE_DIM // 2
    inv_freq = ROPE_THETA ** (-jnp.arange(0, axis_dim, 2, dtype=F32) / axis_dim)
    ang = jnp.stack([row_id[:, None] * inv_freq, col_id[:, None] * inv_freq], axis=1)
    cos, sin = jnp.cos(ang), jnp.sin(ang)
    zeros = jnp.zeros_like(sin)
    cos_l = jnp.concatenate([cos, cos], axis=-1).reshape(DEC_SEQ, QK_ROPE_DIM)
    s1_l = jnp.concatenate([-sin, zeros], axis=-1).reshape(DEC_SEQ, QK_ROPE_DIM)
    s2_l = jnp.concatenate([zeros, sin], axis=-1).reshape(DEC_SEQ, QK_ROPE_DIM)

    def widen(rope_part, nope_fill):
        left = jnp.full((DEC_SEQ, QK_NOPE_DIM), nope_fill, F32)
        right = jnp.zeros((DEC_SEQ, HEAD_PAD - QK_HEAD_DIM), F32)
        return jnp.concatenate([left, rope_part, right], axis=-1)

    pos = jnp.stack([widen(cos_l, 1.0), widen(s1_l, 0.0), widen(s2_l, 0.0)])
    ident_c = jnp.concatenate([jnp.ones((TB, QK_HEAD_DIM), F32),
                               jnp.zeros((TB, HEAD_PAD - QK_HEAD_DIM), F32)], axis=-1)
    ident = jnp.stack([ident_c, jnp.zeros_like(ident_c), jnp.zeros_like(ident_c)])
    return jnp.swapaxes(jnp.concatenate([pos, ident], axis=1), 1, 2)


def _dot_nt(a, b):
    return lax.dot_general(a, b, (((1,), (1,)), ((), ())), preferred_element_type=F32)


def _shift_rows(x, n):
    n = n % x.shape[0]
    return jnp.concatenate([x[n:], x[:n]], axis=0)


def _rope_rows(xn, tabt_ref):
    half = QK_ROPE_DIM // 4
    return xn * tabt_ref[0] + _shift_rows(xn, half) * tabt_ref[1] + _shift_rows(xn, -half) * tabt_ref[2]


def _mla_proj_kernel(x_ref, mod_ref, g_ref, wdq_ref, nq_ref, wuqt_ref, gq_ref, wdkv_ref, nkv_ref,
                     tabt_ref, qt_ref, ckv_ref, krp_ref):
    x = x_ref[...]
    m = mod_ref[0]
    h = _rms(x, g_ref[...]) * (1.0 + m[1:2]) + m[0:1]
    hb = h.astype(BF16)
    cq = _rms(_dot(hb, wdq_ref[...]), nq_ref[...])
    qt = _dot_nt(wuqt_ref[...], cq.astype(BF16))
    gq = gq_ref[...]
    for hd in range(N_HEADS):
        qh = qt[hd * HEAD_PAD:(hd + 1) * HEAD_PAD, :]
        ss = jnp.sum(qh * qh, axis=0, keepdims=True) * (1.0 / QK_HEAD_DIM)
        qn = qh * lax.rsqrt(ss + RMS_EPS) * gq
        qt_ref[hd * HEAD_PAD:(hd + 1) * HEAD_PAD, :] = _rope_rows(qn, tabt_ref).astype(BF16)
    kva = _dot(hb, wdkv_ref[...])
    ckv_ref[...] = _rms(kva[:, :KV_RANK], nkv_ref[...])
    krp_ref[...] = kva[:, KV_RANK:]


def _pad_heads(w, per_head):
    k = w.shape[0]
    w = w.reshape(k, N_HEADS, per_head)
    return jnp.pad(w, ((0, 0), (0, 0), (0, HEAD_PAD - per_head))).reshape(k, QKV_W)


def _pad_gain(g):
    return jnp.pad(g, (0, HEAD_PAD - QK_HEAD_DIM)).reshape(1, HEAD_PAD)


def _mla_proj(x, mod, g, w_dq, norm_q, w_uq, gq, w_dkv, norm_kv, tab_t):
    wdkv = jnp.concatenate([w_dkv[:, :KV_RANK], jnp.zeros((D_MODEL, QK_NOPE_DIM), F32),
                            w_dkv[:, KV_RANK:], jnp.zeros((D_MODEL, HEAD_PAD - QK_HEAD_DIM), F32)], axis=1)
    n_pb = N_PROMPT // TB

    def tab_idx(i):
        return (0, 0, jnp.where(i < n_pb, ROPE_BLOCKS, (i - n_pb) % ROPE_BLOCKS))

    return pl.pallas_call(
        _mla_proj_kernel,
        grid=(N_TOK // TB,),
        in_specs=[
            pl.BlockSpec((TB, D_MODEL), lambda i: (i, 0)),
            _mod_spec(TB),
            _const_spec((1, D_MODEL)),
            _const_spec((D_MODEL, Q_RANK)),
            _const_spec((1, Q_RANK)),
            _const_spec((QKV_W, Q_RANK)),
            _const_spec((HEAD_PAD, 1)),
            _const_spec((D_MODEL, 2 * LANE)),
            _const_spec((1, KV_RANK)),
            pl.BlockSpec((3, HEAD_PAD, TB), tab_idx),
        ],
        out_specs=[
            pl.BlockSpec((QKV_W, TB), lambda i: (0, i)),
            pl.BlockSpec((TB, KV_RANK), lambda i: (i, 0)),
            pl.BlockSpec((TB, HEAD_PAD), lambda i: (i, 0)),
        ],
        out_shape=[
            jax.ShapeDtypeStruct((QKV_W, N_TOK), BF16),
            jax.ShapeDtypeStruct((N_TOK, KV_RANK), F32),
            jax.ShapeDtypeStruct((N_TOK, HEAD_PAD), F32),
        ],
        compiler_params=_cparams(("parallel",)),
        name="mla_proj",
    )(x, mod, g.reshape(1, -1), w_dq.astype(BF16), norm_q.reshape(1, -1),
      _pad_heads(w_uq, QK_HEAD_DIM).T.astype(BF16), _pad_gain(gq).reshape(HEAD_PAD, 1), wdkv.astype(BF16),
      norm_kv.reshape(1, -1), tab_t)


def _kv_expand_kernel(ckv_ref, krp_ref, wukvt_ref, gk_ref, tabt_ref, k_ref, vt_ref):
    kvt = _dot_nt(wukvt_ref[...], ckv_ref[...].astype(BF16))
    krpt = krp_ref[...].T
    gk = gk_ref[...]
    row = lax.broadcasted_iota(jnp.int32, (HEAD_PAD, TB), 0)
    is_nope = row < QK_NOPE_DIM
    ones_row = jnp.where(row == 0, 1.0, 0.0)
    for hd in range(N_HEADS):
        blk = kvt[hd * HEAD_PAD:(hd + 1) * HEAD_PAD, :]
        kraw = jnp.where(is_nope, blk, krpt)
        ss = jnp.sum(kraw * kraw, axis=0, keepdims=True) * (1.0 / QK_HEAD_DIM)
        kn = kraw * lax.rsqrt(ss + RMS_EPS) * gk
        k_ref[:, hd * HEAD_PAD:(hd + 1) * HEAD_PAD] = _rope_rows(kn, tabt_ref).T.astype(BF16)
        vt_ref[hd * HEAD_PAD:(hd + 1) * HEAD_PAD, :] = jnp.where(is_nope, ones_row, blk).astype(BF16)


def _kv_expand(ckv_all, krp_all, w_ukv, gk, tab_t):
    n_sb = DEC_BATCH * KV_LEN // TB
    per = KV_LEN // TB
    n_cache = PAST_LEN // TB

    def tab_idx(i):
        j = i % per
        blk = jnp.where((i >= n_sb) | (j < n_cache), ROPE_BLOCKS, j - n_cache)
        return (0, 0, blk)

    return pl.pallas_call(
        _kv_expand_kernel,
        grid=(N_KV_ROWS // TB,),
        in_specs=[
            pl.BlockSpec((TB, KV_RANK), lambda i: (i, 0)),
            pl.BlockSpec((TB, HEAD_PAD), lambda i: (i, 0)),
            _const_spec((QKV_W, KV_RANK)),
            _const_spec((HEAD_PAD, 1)),
            pl.BlockSpec((3, HEAD_PAD, TB), tab_idx),
        ],
        out_specs=[
            pl.BlockSpec((TB, QKV_W), lambda i: (i, 0)),
            pl.BlockSpec((QKV_W, TB), lambda i: (0, i)),
        ],
        out_shape=[
            jax.ShapeDtypeStruct((N_KV_ROWS, QKV_W), BF16),
            jax.ShapeDtypeStruct((QKV_W, N_KV_ROWS), BF16),
        ],
        compiler_params=_cparams(("parallel",)),
        name="mla_kv_expand",
    )(ckv_all, krp_all, w_ukv.T.astype(BF16), _pad_gain(gk).reshape(HEAD_PAD, 1), tab_t)


def _attn_kernel(qt_ref, k_ref, vt_ref, o_ref, s_a, s_b, *, t_k, hps):
    c = (1.0 / math.sqrt(QK_HEAD_DIM)) * math.log2(math.e)
    kc = min(TKC, t_k)
    n_chunks = t_k // kc
    per_it = min(ATTN_CHUNKS_PER_ITER, n_chunks)
    n_it = n_chunks // per_it
    bufs = (s_a, s_b)

    def rows(h):
        return slice(h * HEAD_PAD, (h + 1) * HEAD_PAD)

    def scores(h, off, m8):
        st = _dot(k_ref[pl.ds(off, kc), rows(h)], qt_ref[rows(h), :])
        bufs[h % 2][pl.ds(off, kc), :] = st
        return jnp.maximum(m8, jnp.max(st.reshape(kc // SUBLANE, SUBLANE, TQ), axis=0))

    def weigh(h, off, m, acc):
        pt = jnp.exp2((bufs[h % 2][pl.ds(off, kc), :] - m) * c).astype(BF16)
        return acc + _dot(vt_ref[rows(h), pl.ds(off, kc)], pt)

    def phase(h_scores, h_weigh, m):
        def body(it, carry):
            m8, acc = carry
            for u in range(per_it):
                off = (it * per_it + u) * kc
                off = off if isinstance(off, int) else pl.multiple_of(off, kc)
                if h_scores is not None:
                    m8 = scores(h_scores, off, m8)
                if h_weigh is not None:
                    acc = weigh(h_weigh, off, m, acc)
            return m8, acc

        init = (jnp.full((SUBLANE, TQ), NEG_BIG, F32), jnp.zeros((HEAD_PAD, TQ), F32))
        return body(0, init) if n_it == 1 else lax.fori_loop(0, n_it, body, init)

    outs = []
    m8, _ = phase(0, None, None)
    for h in range(1, hps + 1):
        m = jnp.max(m8, axis=0, keepdims=True)
        m8, acc = phase(h if h < hps else None, h - 1, m)
        outs.append(acc[QK_NOPE_DIM:, :] / acc[0:1, :])
    o_ref[...] = jnp.concatenate(outs, axis=0).T.astype(BF16)


def _attention(qt, k, vt, *, n_batch, t_q, t_k, q_row0, kv_row0, hps):
    nq = t_q // TQ
    q0 = q_row0 // TQ
    k0 = kv_row0 // t_k
    return pl.pallas_call(
        functools.partial(_attn_kernel, t_k=t_k, hps=hps),
        grid=(n_batch, N_HEADS // hps, nq),
        in_specs=[
            pl.BlockSpec((hps * HEAD_PAD, TQ), lambda b, h, i: (h, q0 + b * nq + i)),
            pl.BlockSpec((t_k, hps * HEAD_PAD), lambda b, h, i: (k0 + b, h)),
            pl.BlockSpec((hps * HEAD_PAD, t_k), lambda b, h, i: (h, k0 + b)),
        ],
        out_specs=pl.BlockSpec((TQ, hps * V_HEAD_DIM), lambda b, h, i: (b * nq + i, h)),
        out_shape=jax.ShapeDtypeStruct((n_batch * t_q, N_HEADS * V_HEAD_DIM), BF16),
        scratch_shapes=[pltpu.VMEM((t_k, TQ), F32), pltpu.VMEM((t_k, TQ), F32)],
        compiler_params=_cparams(("parallel", "parallel", "parallel")),
        name=f"mla_attention_tk{t_k}",
    )(qt, k, vt)


def _out_proj_kernel(ap_ref, as_ref, x_ref, mod_ref, wo_ref, o_ref):
    a = jnp.where(pl.program_id(0) < N_PROMPT // TB, ap_ref[...], as_ref[...])
    o_ref[...] = x_ref[...] + mod_ref[0][2:3] * _dot(a, wo_ref[...])


def _out_proj(attn_p, attn_s, x, mod, w_o):
    n_pb = N_PROMPT // TB
    return pl.pallas_call(
        _out_proj_kernel,
        grid=(N_TOK // TB,),
        in_specs=[
            pl.BlockSpec((TB, D_MODEL), lambda i: (jnp.minimum(i, n_pb - 1), 0)),
            pl.BlockSpec((TB, D_MODEL), lambda i: (jnp.maximum(i - n_pb, 0), 0)),
            pl.BlockSpec((TB, D_MODEL), lambda i: (i, 0)),
            _mod_spec(TB),
            _const_spec((D_MODEL, D_MODEL)),
        ],
        out_specs=pl.BlockSpec((TB, D_MODEL), lambda i: (i, 0)),
        out_shape=jax.ShapeDtypeStruct((N_TOK, D_MODEL), F32),
        compiler_params=_cparams(("parallel",)),
        name="mla_out_proj",
    )(attn_p, attn_s, x, mod, w_o.astype(BF16))


def _mla_layer(x, mod, g, cache_ckv, cache_krope, tab, w_dq, norm_q, w_uq, w_dkv, norm_kv, w_ukv,
               gq, gk, w_o):
    qt, ckv, krp = _mla_proj(x, mod, g, w_dq, norm_q, w_uq, gq, w_dkv, norm_kv, tab)
    cache_krp = jnp.pad(cache_krope, ((0, 0), (0, 0), (QK_NOPE_DIM, HEAD_PAD - QK_HEAD_DIM)))
    ckv_s = ckv[N_PROMPT:].reshape(DEC_BATCH, DEC_SEQ, KV_RANK)
    krp_s = krp[N_PROMPT:].reshape(DEC_BATCH, DEC_SEQ, HEAD_PAD)
    ckv_all = jnp.concatenate([jnp.concatenate([cache_ckv, ckv_s], axis=1).reshape(-1, KV_RANK),
                               ckv[:N_PROMPT]], axis=0)
    krp_all = jnp.concatenate([jnp.concatenate([cache_krp, krp_s], axis=1).reshape(-1, HEAD_PAD),
                               krp[:N_PROMPT]], axis=0)
    k, vt = _kv_expand(ckv_all, krp_all, w_ukv, gk, tab)
    a_p = _attention(qt, k, vt, n_batch=BATCH, t_q=SEQ, t_k=SEQ, q_row0=0, kv_row0=DEC_BATCH * KV_LEN,
                     hps=N_HEADS)
    a_s = _attention(qt, k, vt, n_batch=DEC_BATCH, t_q=DEC_SEQ, t_k=KV_LEN, q_row0=N_PROMPT, kv_row0=0,
                     hps=4)
    x = _out_proj(a_p, a_s, x, mod, w_o)
    state_ckv = ckv[:N_PROMPT].reshape(BATCH, SEQ, KV_RANK)
    state_krope = krp[:N_PROMPT, QK_NOPE_DIM:QK_HEAD_DIM].reshape(BATCH, SEQ, QK_ROPE_DIM)
    return x, state_ckv, state_krope


def _tile_plan(cnt8, plan_ref, start_ref):
    tm = float(TM)
    cnt_row = cnt8[0:1, :]
    cnt_col = cnt8.T[:, 0:1]
    tiles_row = jnp.floor((cnt_row + (tm - 1.0)) * (1.0 / tm))
    tiles_col = jnp.floor((cnt_col + (tm - 1.0)) * (1.0 / tm))
    sub = lax.broadcasted_iota(jnp.int32, (LANE, LANE), 0).astype(F32)
    lan = lax.broadcasted_iota(jnp.int32, (LANE, LANE), 1).astype(F32)
    tile_end_row = jnp.sum(jnp.where(sub <= lan, tiles_col, 0.0), axis=0, keepdims=True)
    tile_end_col = jnp.sum(jnp.where(lan <= sub, tiles_row, 0.0), axis=1, keepdims=True)
    n_used = jnp.max(tile_end_row, axis=1, keepdims=True)
    start_col = (tile_end_col - tiles_col) * tm
    end_col = start_col + cnt_col
    cand = jnp.where(jnp.logical_and(lan > sub, tiles_row > 0.0), lan, float(LANE))
    next_col = jnp.min(cand, axis=1, keepdims=True)
    next_col = jnp.where(next_col < float(LANE), next_col, -1.0)
    n_lanes = plan_ref.shape[1]
    tidx = jnp.minimum(lax.broadcasted_iota(jnp.int32, (LANE, n_lanes), 1).astype(F32), n_used - 1.0)
    esub = lax.broadcasted_iota(jnp.int32, (LANE, n_lanes), 0).astype(F32)
    te_row = jnp.sum(jnp.where(tile_end_col <= tidx, 1.0, 0.0), axis=0, keepdims=True)
    mine = esub == te_row
    end_at = jnp.sum(jnp.where(mine, end_col, 0.0), axis=0, keepdims=True)
    tv_row = jnp.clip(end_at - tidx[0:1, :] * tm, 0.0, tm)
    nx_row = jnp.sum(jnp.where(mine, next_col, 0.0), axis=0, keepdims=True)
    nu_row = jnp.broadcast_to(n_used, (1, n_lanes))
    plan_ref[...] = jnp.concatenate([te_row, tv_row, nx_row, nu_row, jnp.zeros((SUBLANE - 4, n_lanes), F32)], axis=0)
    start_ref[...] = jnp.broadcast_to(start_col, (LANE, LANE))


def _route_kernel(x_ref, mod_ref, g_ref, wr_ref, br_ref, tri_ref, h_ref, meta_ref, metat_ref, plan_ref, start_ref,
                  carry):
    i = pl.program_id(0)

    @pl.when(i == 0)
    def _():
        carry[...] = jnp.zeros_like(carry)

    x = x_ref[...]
    m = mod_ref[0]
    h = _rms(x, g_ref[...]) * (1.0 + m[4:5]) + m[3:4]
    h_ref[...] = _pack_bf16_pairs(h)
    logits = _dot(h.astype(BF16), wr_ref[...]) + br_ref[...]
    lane = lax.broadcasted_iota(jnp.int32, logits.shape, 1).astype(F32)
    work = logits
    sel = jnp.zeros(logits.shape, F32)
    hits, tops = [], []
    for k in range(TOP_K):
        mk = jnp.max(work, axis=-1, keepdims=True)
        first = jnp.min(jnp.where(work == mk, lane, float(LANE)), axis=-1, keepdims=True)
        hit = lane == first
        sel = jnp.where(hit, 1.0, sel)
        work = jnp.where(hit, -jnp.inf, work)
        hits.append((hit, first))
        tops.append(mk)
    es = [jnp.exp(t - tops[0]) for t in tops]
    denom = es[0] + es[1] + es[2] + es[3]
    pos = _dot(tri_ref[...], sel.astype(BF16)) + carry[0:1, :]
    carry[...] = carry[...] + jnp.sum(sel, axis=0, keepdims=True)

    @pl.when(i == pl.num_programs(0) - 1)
    def _():
        _tile_plan(carry[...], plan_ref, start_ref)

    meta = jnp.zeros(logits.shape, F32)
    for k in range(TOP_K):
        hit, first = hits[k]
        pk = jnp.sum(jnp.where(hit, pos, 0.0), axis=-1, keepdims=True)
        meta = jnp.where(lane == float(k), first, meta)
        meta = jnp.where(lane == float(TOP_K + k), es[k] / denom, meta)
        meta = jnp.where(lane == float(2 * TOP_K + k), pk, meta)
    meta_ref[...] = meta
    metat_ref[...] = meta.T[:2 * SUBLANE, :]


def _route(x, mod, g, w_router, b_router):
    wr = jnp.pad(w_router, ((0, 0), (0, LANE - N_EXPERTS))).astype(BF16)
    br = jnp.pad(b_router, (0, LANE - N_EXPERTS), constant_values=NEG_BIG).reshape(1, LANE)
    tb = ROUTE_TB
    tri = jnp.tri(tb, tb, -1, dtype=BF16)
    return pl.pallas_call(
        _route_kernel,
        grid=(N_TOK // tb,),
        in_specs=[
            pl.BlockSpec((tb, D_MODEL), lambda i: (i, 0)),
            _mod_spec(tb),
            _const_spec((1, D_MODEL)),
            _const_spec((D_MODEL, LANE)),
            _const_spec((1, LANE)),
            _const_spec((tb, tb)),
        ],
        out_specs=[
            pl.BlockSpec((tb, D_MODEL // 2), lambda i: (i, 0)),
            pl.BlockSpec((tb, LANE), lambda i: (i, 0)),
            pl.BlockSpec((2 * SUBLANE, tb), lambda i: (0, i)),
            _const_spec((SUBLANE, PLAN_LANES)),
            _const_spec((LANE, LANE)),
        ],
        out_shape=[
            jax.ShapeDtypeStruct((N_TOK, D_MODEL // 2), jnp.uint32),
            jax.ShapeDtypeStruct((N_TOK, LANE), F32),
            jax.ShapeDtypeStruct((2 * SUBLANE, N_TOK), F32),
            jax.ShapeDtypeStruct((SUBLANE, PLAN_LANES), F32),
            jax.ShapeDtypeStruct((LANE, LANE), F32),
        ],
        scratch_shapes=[pltpu.VMEM((SUBLANE, LANE), F32)],
        compiler_params=_cparams(("arbitrary",)),
        name="moe_route",
    )(x, mod, g.reshape(1, -1), wr, br, tri)


def _slots_kernel(start_ref, metat_ref, dest_ref):
    start_col = start_ref[:, 0:1]
    esub = lax.broadcasted_iota(jnp.int32, (LANE, SLOT_TB), 0).astype(F32)
    rows = []
    for k in range(TOP_K):
        e = metat_ref[k:k + 1, :]
        first = jnp.sum(jnp.where(esub == e, start_col, 0.0), axis=0, keepdims=True)
        rows.append(first + metat_ref[2 * TOP_K + k:2 * TOP_K + k + 1, :])
    dest_ref[...] = jnp.concatenate(rows, axis=0).astype(jnp.int32)


def _slots(start, meta_t):
    return pl.pallas_call(
        _slots_kernel,
        grid=(N_TOK // SLOT_TB,),
        in_specs=[
            _const_spec((LANE, LANE)),
            pl.BlockSpec((2 * SUBLANE, SLOT_TB), lambda i: (0, i)),
        ],
        out_specs=pl.BlockSpec((TOP_K, SLOT_TB), lambda i: (0, i)),
        out_shape=jax.ShapeDtypeStruct((TOP_K, N_TOK), jnp.int32),
        compiler_params=_cparams(("parallel",)),
        name="moe_slots",
    )(start, meta_t)


def _sc_gather(table, idx, ch):
    b, w = idx.shape[0], table.shape[1]
    per_w = b // SC_WORKERS
    n_ch = per_w // ch
    assert per_w * SC_WORKERS == b and n_ch * ch == per_w and n_ch % 2 == 0
    mesh = plsc.VectorSubcoreMesh(core_axis_name="c", subcore_axis_name="s")

    @functools.partial(
        pl.kernel, mesh=mesh,
        out_type=jax.ShapeDtypeStruct((b, w), table.dtype),
        scratch_types=[
            pltpu.VMEM((n_ch, ch), jnp.int32),
            pltpu.VMEM((ch, w), table.dtype),
            pltpu.VMEM((ch, w), table.dtype),
            pltpu.SemaphoreType.DMA, pltpu.SemaphoreType.DMA,
            pltpu.SemaphoreType.DMA, pltpu.SemaphoreType.DMA,
        ],
        name="sc_row_gather",
    )
    def gather_rows(table_hbm, idx_hbm, out_hbm, idx_v, buf0, buf1, g0, g1, s0, s1):
        wid = lax.axis_index("s") * SC_CORES + lax.axis_index("c")
        base = wid * per_w
        pltpu.sync_copy(idx_hbm.at[wid], idx_v)

        def gather(j, buf, sem):
            return pltpu.make_async_copy(table_hbm.at[idx_v.at[j]], buf, sem)

        def store(j, buf, sem):
            return pltpu.make_async_copy(buf, out_hbm.at[pl.ds(base + j * ch, ch)], sem)

        gather(0, buf0, g0).start()

        @pl.loop(0, n_ch, step=2)
        def _(j):
            @pl.when(j > 0)
            def _():
                store(j - 1, buf1, s1).wait()

            gather(j + 1, buf1, g1).start()
            gather(j, buf0, g0).wait()
            store(j, buf0, s0).start()
            gather(j + 1, buf1, g1).wait()
            store(j + 1, buf1, s1).start()
            store(j, buf0, s0).wait()

            @pl.when(j + 2 < n_ch)
            def _():
                gather(j + 2, buf0, g0).start()

        store(n_ch - 1, buf1, s1).wait()

    return gather_rows(table, idx.reshape(SC_WORKERS, n_ch, ch))


def _sc_dispatch(rows, dest_t, ch):
    n, w = rows.shape
    per_w = n // SC_WORKERS
    n_ch = per_w // ch
    assert per_w * SC_WORKERS == n and n_ch * ch == per_w and n_ch % 2 == 0
    mesh = plsc.VectorSubcoreMesh(core_axis_name="c", subcore_axis_name="s")
    idx = dest_t.reshape(TOP_K, SC_WORKERS, n_ch, ch)

    @functools.partial(
        pl.kernel, mesh=mesh,
        out_type=jax.ShapeDtypeStruct((N_SLOTS, w), rows.dtype),
        scratch_types=[
            pltpu.VMEM((TOP_K * n_ch, ch), jnp.int32),
            pltpu.VMEM((ch, w), rows.dtype),
            pltpu.VMEM((ch, w), rows.dtype),
            pltpu.SemaphoreType.DMA, pltpu.SemaphoreType.DMA,
            pltpu.SemaphoreType.DMA, pltpu.SemaphoreType.DMA,
        ],
        name="sc_row_dispatch",
    )
    def dispatch_rows(rows_hbm, idx_hbm, out_hbm, idx_v, buf0, buf1, l0, l1, s0, s1):
        wid = lax.axis_index("s") * SC_CORES + lax.axis_index("c")
        base = wid * per_w
        for k in range(TOP_K):
            pltpu.sync_copy(idx_hbm.at[k, wid], idx_v.at[pl.ds(k * n_ch, n_ch)])

        def load(j, buf, sem):
            return pltpu.make_async_copy(rows_hbm.at[pl.ds(base + j * ch, ch)], buf, sem)

        def scatter(j, k, buf, sem):
            return pltpu.make_async_copy(buf, out_hbm.at[idx_v.at[k * n_ch + j]], sem)

        load(0, buf0, l0).start()

        @pl.loop(0, n_ch, step=2)
        def _(j):
            load(j + 1, buf1, l1).start()
            load(j, buf0, l0).wait()
            for k in range(TOP_K):
                scatter(j, k, buf0, s0).start()
            load(j + 1, buf1, l1).wait()
            for k in range(TOP_K):
                scatter(j + 1, k, buf1, s1).start()
            for k in range(TOP_K):
                scatter(j, k, buf0, s0).wait()

            @pl.when(j + 2 < n_ch)
            def _():
                load(j + 2, buf0, l0).start()

            for k in range(TOP_K):
                scatter(j + 1, k, buf1, s1).wait()

    return dispatch_rows(rows, idx)


def _deinterleave_matrix():
    src = jnp.arange(2 * LANE)[:, None]
    dst = jnp.arange(2 * LANE)[None, :]
    want = jnp.where(dst < LANE, 2 * dst, 2 * (dst - LANE) + 1)
    return (src == want).astype(BF16)


def _expert_kernel(te_ref, nu_ref, tv_ref, nx_ref, x_ref, wgu_hbm, bgu_ref, wd_hbm, bd_ref, perm_ref, o_ref,
                   wgu_st, wd_st, wgu_bf, wd_bf, sems, *, layer):
    i = pl.program_id(0)
    prev = te_ref[jnp.maximum(i - 1, 0)]
    fresh = jnp.logical_or(i == 0, te_ref[i] != prev)

    def fetch(e):
        return (pltpu.make_async_copy(wgu_hbm.at[layer, e], wgu_st, sems.at[0]),
                pltpu.make_async_copy(wd_hbm.at[layer, e], wd_st, sems.at[1]))

    @pl.when(i == 0)
    def _():
        for cp in fetch(te_ref[0]):
            cp.start()

    @pl.when(jnp.logical_and(fresh, i < nu_ref[0]))
    def _():
        for cp in fetch(te_ref[i]):
            cp.wait()
        for b in range(2 * D_FF // (2 * LANE)):
            sl = slice(b * 2 * LANE, (b + 1) * 2 * LANE)
            wgu_bf[:, sl] = _dot(wgu_st[:, sl].astype(BF16), perm_ref[...]).astype(BF16)
        wd_bf[...] = wd_st[...].astype(BF16)

        @pl.when(nx_ref[i] >= 0)
        def _():
            for cp in fetch(nx_ref[i]):
                cp.start()

    @pl.when(i < nu_ref[0])
    def _():
        row = lax.broadcasted_iota(jnp.int32, (TM, D_MODEL // 2), 0)
        w = jnp.where(row < tv_ref[i], x_ref[...], jnp.uint32(0))
        x = _unpack_bf16_pairs(w).astype(BF16)
        gu = _dot(x, wgu_bf[...]) + bgu_ref[...]
        acts = []
        for b in range(D_FF // LANE):
            glu = jnp.minimum(gu[:, b * 2 * LANE:b * 2 * LANE + LANE], SWIGLU_LIMIT)
            lin = jnp.clip(gu[:, b * 2 * LANE + LANE:(b + 1) * 2 * LANE], -SWIGLU_LIMIT, SWIGLU_LIMIT)
            acts.append((glu * jax.nn.sigmoid(SWIGLU_ALPHA * glu) * (lin + 1.0)).astype(BF16))
        act = jnp.concatenate(acts, axis=1)
        o_ref[...] = _pack_bf16_pairs(_dot(act, wd_bf[...]) + bd_ref[...])


def _experts(buf, tile_expert, n_used, tile_valid, tile_next, layer, w_gu, b_gu, w_down, b_down):
    bgu = b_gu.reshape(N_EXPERTS, D_FF // LANE, LANE, 2).transpose(0, 1, 3, 2).reshape(N_EXPERTS, 1, 2 * D_FF)

    def row_idx(i, te, nu, tv, nx):
        return (jnp.minimum(i, nu[0] - 1), 0)

    def b_idx(i, te, nu, tv, nx):
        return (te[i], 0, 0)

    grid_spec = pltpu.PrefetchScalarGridSpec(
        num_scalar_prefetch=4,
        grid=(N_TILES,),
        in_specs=[
            pl.BlockSpec((TM, D_MODEL // 2), row_idx),
            pl.BlockSpec(memory_space=pl.ANY),
            pl.BlockSpec((None, 1, 2 * D_FF), b_idx),
            pl.BlockSpec(memory_space=pl.ANY),
            pl.BlockSpec((None, 1, D_MODEL), b_idx),
            pl.BlockSpec((2 * LANE, 2 * LANE), lambda i, te, nu, tv, nx: (0, 0)),
        ],
        out_specs=pl.BlockSpec((TM, D_MODEL // 2), row_idx),
        scratch_shapes=[
            pltpu.VMEM((D_MODEL, 2 * D_FF), F32),
            pltpu.VMEM((D_FF, D_MODEL), F32),
            pltpu.VMEM((D_MODEL, 2 * D_FF), BF16),
            pltpu.VMEM((D_FF, D_MODEL), BF16),
            pltpu.SemaphoreType.DMA((2,)),
        ],
    )
    return pl.pallas_call(
        functools.partial(_expert_kernel, layer=layer),
        grid_spec=grid_spec,
        out_shape=jax.ShapeDtypeStruct((N_SLOTS, D_MODEL // 2), jnp.uint32),
        compiler_params=_cparams(("arbitrary",)),
        name="moe_experts",
    )(tile_expert, n_used, tile_valid, tile_next, buf, w_gu, bgu, w_down,
      b_down.reshape(N_EXPERTS, 1, D_MODEL), _deinterleave_matrix())


def _pack_bf16_pairs(v):
    half = v.shape[1] // 2
    bits = pltpu.bitcast(v.astype(BF16).astype(F32), jnp.uint32)
    return (bits[:, half:] & jnp.uint32(0xFFFF0000)) | (bits[:, :half] >> 16)


def _unpack_bf16_pairs(w):
    return jnp.concatenate([pltpu.bitcast(w << 16, F32), pltpu.bitcast(w & jnp.uint32(0xFFFF0000), F32)],
                           axis=1)


def _combine_kernel(x_ref, mod_ref, y_ref, w_ref, o_ref):
    w = w_ref[:, TOP_K:2 * TOP_K]
    y = _unpack_bf16_pairs(y_ref[0]) * w[:, 0:1]
    for k in range(1, TOP_K):
        y = y + _unpack_bf16_pairs(y_ref[k]) * w[:, k:k + 1]
    o_ref[...] = x_ref[...] + mod_ref[0][5:6] * y


def _combine(x, mod, y4, meta):
    return pl.pallas_call(
        _combine_kernel,
        grid=(N_TOK // TB,),
        in_specs=[
            pl.BlockSpec((TB, D_MODEL), lambda i: (i, 0)),
            _mod_spec(TB),
            pl.BlockSpec((TOP_K, TB, D_MODEL // 2), lambda i: (0, i, 0)),
            pl.BlockSpec((TB, LANE), lambda i: (i, 0)),
        ],
        out_specs=pl.BlockSpec((TB, D_MODEL), lambda i: (i, 0)),
        out_shape=jax.ShapeDtypeStruct((N_TOK, D_MODEL), F32),
        compiler_params=_cparams(("parallel",)),
        name="moe_combine",
    )(x, mod, y4, meta)


def _moe_layer(x, mod, g, layer, w_router, b_router, w_gu, b_gu, w_down, b_down):
    hp, meta, meta_t, plan, start = _route(x, mod, g, w_router, b_router)
    plan = plan[:4, :N_TILES].astype(jnp.int32)
    tile_expert, tile_valid, tile_next, n_used = plan[0], plan[1], plan[2], plan[3, :1]
    dest_t = _slots(start, meta_t)
    buf = _sc_dispatch(hp, dest_t, 64)
    yb = _experts(buf, tile_expert, n_used, tile_valid, tile_next, layer, w_gu, b_gu, w_down, b_down)
    y4 = _sc_gather(yb, dest_t.reshape(-1), 64).reshape(TOP_K, N_TOK, D_MODEL // 2)
    return _combine(x, mod, y4, meta)


def kernel(x_prompt, x_sample, c, cache_ckv, cache_krope, c_ctx, norm_mix_g, norm_ffn_g, w_mod, b_mod,
           g_w_in, g_b_in, g_norm_v, g_w_s, g_b_s, g_w_out, m_w_dq, m_norm_q, m_w_uq, m_w_dkv,
           m_norm_kv, m_w_ukv, m_qk_norm_q, m_qk_norm_k, m_w_o, e_w_router, e_b_router, e_w_gu,
           e_b_gu, e_w_down, e_b_down):
    x = jnp.concatenate([x_prompt.reshape(N_PROMPT, D_MODEL), x_sample.reshape(N_SAMPLE, D_MODEL)], axis=0)
    cond = jnp.concatenate([c_ctx[None, :], c, jnp.zeros((SUBLANE - N_COND, D_MODEL), F32)], axis=0)
    mod = _modulation(cond, w_mod, b_mod)
    tab = _rope_tables()
    ckv_states, krope_states = [], []
    for layer in range(DEPTH):
        j = layer // 2
        if layer % 2 == 0:
            x = _gmlp_layer(x, mod[layer], norm_mix_g[layer], g_w_in[j], g_b_in[j], g_norm_v[j],
                            g_w_s[j], g_b_s[j], g_w_out[j])
        else:
            x, s_ckv, s_krope = _mla_layer(
                x, mod[layer], norm_mix_g[layer], cache_ckv[:, j], cache_krope[:, j], tab,
                m_w_dq[j], m_norm_q[j], m_w_uq[j], m_w_dkv[j], m_norm_kv[j], m_w_ukv[j],
                m_qk_norm_q[j], m_qk_norm_k[j], m_w_o[j])
            ckv_states.append(s_ckv)
            krope_states.append(s_krope)
        x = _moe_layer(x, mod[layer], norm_ffn_g[layer], layer, e_w_router[layer], e_b_router[layer],
                       e_w_gu, e_b_gu[layer], e_w_down, e_b_down[layer])
    y_prompt = x[:N_PROMPT].reshape(BATCH, SEQ, D_MODEL)
    y_sample = x[N_PROMPT:].reshape(DEC_BATCH, DEC_SEQ, D_MODEL)
    return (y_prompt, y_sample, jnp.stack(ckv_states, axis=1), jnp.stack(krope_states, axis=1))
```

```python
import functools
import math

import jax
import jax.numpy as jnp
from jax import lax
from jax.experimental import pallas as pl
from jax.experimental.pallas import tpu as pltpu
from jax.experimental.pallas import tpu_sc as plsc

F32 = jnp.float32
BF16 = jnp.bfloat16

D_MODEL = 1024
BATCH = 32
SEQ = 256
DEPTH = 4
DEC_BATCH = 2
DEC_SEQ = 4096
PAST_LEN = 512
GRID_W = 64
RMS_EPS = 1e-6
GMLP_WIDTH = 2 * D_MODEL
GMLP_GROUPS = 8
GROUP_W = GMLP_WIDTH // GMLP_GROUPS
CHUNK = 128
N_HEADS = 16
QK_NOPE_DIM = 64
QK_ROPE_DIM = 32
QK_HEAD_DIM = QK_NOPE_DIM + QK_ROPE_DIM
V_HEAD_DIM = 64
Q_RANK = 256
KV_RANK = 128
ROPE_THETA = 10000.0
N_EXPERTS = 32
TOP_K = 4
D_FF = D_MODEL
SWIGLU_LIMIT = 7.0
SWIGLU_ALPHA = 1.702

N_PROMPT = BATCH * SEQ
N_SAMPLE = DEC_BATCH * DEC_SEQ
N_TOK = N_PROMPT + N_SAMPLE
N_COND = 1 + DEC_BATCH
KV_LEN = PAST_LEN + DEC_SEQ
N_KV_ROWS = DEC_BATCH * KV_LEN + N_PROMPT

LANE = 128
SUBLANE = 8
HEAD_PAD = LANE
QKV_W = N_HEADS * HEAD_PAD
VMEM_LIMIT = 56 * 1024 * 1024

TB = 256
TQ = 256
TKC = 256
ATTN_CHUNKS_PER_ITER = 9
TM = 512
N_TILES = N_TOK * TOP_K // TM + N_EXPERTS
N_SLOTS = N_TILES * TM
PLAN_LANES = -(-N_TILES // LANE) * LANE
SLOT_TB = 2048
ROUTE_TB = 512
SC_CORES = 2
SC_WORKERS = SC_CORES * 16
ROPE_BLOCKS = DEC_SEQ // TB
NEG_BIG = -1e30


def _cparams(sem):
    return pltpu.CompilerParams(dimension_semantics=sem, vmem_limit_bytes=VMEM_LIMIT)


def _cond_of_block(i, tb):
    n_p = N_PROMPT // tb
    per = DEC_SEQ // tb
    return jnp.where(i < n_p, 0, 1 + (i - n_p) // per)


def _rms(x, g, n=None):
    n = x.shape[-1] if n is None else n
    ss = jnp.sum(x * x, axis=-1, keepdims=True) * (1.0 / n)
    return x * lax.rsqrt(ss + RMS_EPS) * g


def _dot(a, b):
    return jnp.dot(a, b, preferred_element_type=F32)


def _mod_kernel(c_ref, w_ref, b_ref, o_ref):
    c = c_ref[...]
    s = c * jax.nn.sigmoid(c)
    o_ref[0] = _dot(s.astype(BF16), w_ref[0].astype(BF16)) + b_ref[0]


def _modulation(cond, w_mod, b_mod):
    tn = 1536
    out = pl.pallas_call(
        _mod_kernel,
        grid=(DEPTH, 6 * D_MODEL // tn),
        in_specs=[
            pl.BlockSpec((SUBLANE, D_MODEL), lambda l, j: (0, 0)),
            pl.BlockSpec((1, D_MODEL, tn), lambda l, j: (l, 0, j)),
            pl.BlockSpec((1, 1, tn), lambda l, j: (l, 0, j)),
        ],
        out_specs=pl.BlockSpec((1, SUBLANE, tn), lambda l, j: (l, 0, j)),
        out_shape=jax.ShapeDtypeStruct((DEPTH, SUBLANE, 6 * D_MODEL), F32),
        compiler_params=_cparams(("parallel", "parallel")),
        name="adaln_mod",
    )(cond, w_mod, b_mod.reshape(DEPTH, 1, 6 * D_MODEL))
    m = out[:, :N_COND].reshape(DEPTH, N_COND, 6, D_MODEL)
    return jnp.pad(m, ((0, 0), (0, 0), (0, SUBLANE - 6), (0, 0)))


def _mod_spec(tb, layer):
    return pl.BlockSpec((None, 1, SUBLANE, D_MODEL), lambda i: (layer, _cond_of_block(i, tb), 0, 0))


def _const_spec(shape):
    nd = len(shape)
    return pl.BlockSpec(shape, lambda *_: (0,) * nd)


def _layer_spec(shape, j):
    nd = len(shape)
    return pl.BlockSpec((None,) + tuple(shape), lambda *_: (j,) + (0,) * nd)


def _gmlp_kernel(xp_ref, xs_ref, mod_ref, g_ref, win_ref, bin_ref, gv_ref, ws_ref, bst_ref, wout_ref, o_ref):
    x = jnp.where(pl.program_id(0) < N_PROMPT // TB, xp_ref[...], xs_ref[...])
    m = mod_ref[0]
    h = _rms(x, g_ref[...]) * (1.0 + m[1:2]) + m[0:1]
    hb = h.astype(BF16)
    zv = jax.nn.gelu(_dot(hb, win_ref[:, GMLP_WIDTH:]) + bin_ref[:, GMLP_WIDTH:], approximate=True)
    vn = _rms(zv, gv_ref[...]).astype(BF16)
    rows = []
    for c in range(TB // CHUNK):
        cols = []
        for g in range(GMLP_GROUPS):
            blk = vn[c * CHUNK:(c + 1) * CHUNK, g * GROUP_W:(g + 1) * GROUP_W]
            cols.append(_dot(ws_ref[g], blk) + bst_ref[:, g:g + 1])
        rows.append(jnp.concatenate(cols, axis=1))
    vm = jnp.concatenate(rows, axis=0)
    u = jax.nn.gelu(_dot(hb, win_ref[:, :GMLP_WIDTH]) + bin_ref[:, :GMLP_WIDTH], approximate=True)
    d = _dot((u * vm).astype(BF16), wout_ref[...])
    o_ref[...] = x + m[2:3] * d


def _split_rows_specs(whole):
    n_pb = N_PROMPT // TB
    first = 0 if whole else n_pb
    return [pl.BlockSpec((TB, D_MODEL), lambda i: (jnp.minimum(i, n_pb - 1), 0)),
            pl.BlockSpec((TB, D_MODEL), lambda i: (jnp.maximum(i, n_pb) - first, 0))]


def _gmlp_layer(x_p, x_s, p, layer):
    j = layer // 2
    return pl.pallas_call(
        _gmlp_kernel,
        grid=(N_TOK // TB,),
        in_specs=_split_rows_specs(x_p is x_s) + [
            _mod_spec(TB, layer),
            _layer_spec((1, D_MODEL), layer),
            _layer_spec((D_MODEL, 2 * GMLP_WIDTH), j),
            _layer_spec((1, 2 * GMLP_WIDTH), j),
            _layer_spec((1, GMLP_WIDTH), j),
            _layer_spec((GMLP_GROUPS, CHUNK, CHUNK), j),
            _layer_spec((CHUNK, GMLP_GROUPS), j),
            _layer_spec((GMLP_WIDTH, D_MODEL), j),
        ],
        out_specs=pl.BlockSpec((TB, D_MODEL), lambda i: (i, 0)),
        out_shape=jax.ShapeDtypeStruct((N_TOK, D_MODEL), F32),
        compiler_params=_cparams(("parallel",)),
        name="gmlp_mixer",
    )(x_p, x_s, p["mod"], p["norm_mix_g"], p["g_w_in"], p["g_b_in"], p["g_norm_v"], p["g_w_s"], p["g_b_st"],
      p["g_w_out"])


def _rope_tables():
    t = jnp.arange(DEC_SEQ)
    row_id = (t // GRID_W).astype(F32)
    col_id = (t % GRID_W).astype(F32)
    axis_dim = QK_ROPE_DIM // 2
    inv_freq = ROPE_THETA ** (-jnp.arange(0, axis_dim, 2, dtype=F32) / axis_dim)
    ang = jnp.stack([row_id[:, None] * inv_freq, col_id[:, None] * inv_freq], axis=1)
    cos, sin = jnp.cos(ang), jnp.sin(ang)
    zeros = jnp.zeros_like(sin)
    cos_l = jnp.concatenate([cos, cos], axis=-1).reshape(DEC_SEQ, QK_ROPE_DIM)
    s1_l = jnp.concatenate([-sin, zeros], axis=-1).reshape(DEC_SEQ, QK_ROPE_DIM)
    s2_l = jnp.concatenate([zeros, sin], axis=-1).reshape(DEC_SEQ, QK_ROPE_DIM)

    def widen(rope_part, nope_fill):
        left = jnp.full((DEC_SEQ, QK_NOPE_DIM), nope_fill, F32)
        right = jnp.zeros((DEC_SEQ, HEAD_PAD - QK_HEAD_DIM), F32)
        return jnp.concatenate([left, rope_part, right], axis=-1)

    pos = jnp.stack([widen(cos_l, 1.0), widen(s1_l, 0.0), widen(s2_l, 0.0)])
    ident_c = jnp.concatenate([jnp.ones((TB, QK_HEAD_DIM), F32),
                               jnp.zeros((TB, HEAD_PAD - QK_HEAD_DIM), F32)], axis=-1)
    ident = jnp.stack([ident_c, jnp.zeros_like(ident_c), jnp.zeros_like(ident_c)])
    return jnp.swapaxes(jnp.concatenate([pos, ident], axis=1), 1, 2)


def _dot_nt(a, b):
    return lax.dot_general(a, b, (((1,), (1,)), ((), ())), preferred_element_type=F32)


def _shift_rows(x, n):
    n = n % x.shape[0]
    return jnp.concatenate([x[n:], x[:n]], axis=0)


def _rope_rows(xn, tabt_ref):
    half = QK_ROPE_DIM // 4
    return xn * tabt_ref[0] + _shift_rows(xn, half) * tabt_ref[1] + _shift_rows(xn, -half) * tabt_ref[2]


def _mla_proj_kernel(x_ref, mod_ref, g_ref, wdq_ref, nq_ref, wuqt_ref, gq_ref, wdkv_ref, nkv_ref,
                     tabt_ref, qt_ref, ckv_ref, krp_ref):
    x = x_ref[...]
    m = mod_ref[0]
    h = _rms(x, g_ref[...]) * (1.0 + m[1:2]) + m[0:1]
    hb = h.astype(BF16)
    cq = _rms(_dot(hb, wdq_ref[...]), nq_ref[...])
    qt = _dot_nt(wuqt_ref[...], cq.astype(BF16))
    gq = gq_ref[...]
    for hd in range(N_HEADS):
        qh = qt[hd * HEAD_PAD:(hd + 1) * HEAD_PAD, :]
        ss = jnp.sum(qh * qh, axis=0, keepdims=True) * (1.0 / QK_HEAD_DIM)
        qn = qh * lax.rsqrt(ss + RMS_EPS) * gq
        qt_ref[hd * HEAD_PAD:(hd + 1) * HEAD_PAD, :] = _rope_rows(qn, tabt_ref).astype(BF16)
    kva = _dot(hb, wdkv_ref[...])
    ckv_ref[...] = _rms(kva[:, :KV_RANK], nkv_ref[...])
    krp_ref[...] = kva[:, KV_RANK:]


def _mla_proj(x, p, layer):
    j = layer // 2
    n_pb = N_PROMPT // TB

    def tab_idx(i):
        return (0, 0, jnp.where(i < n_pb, ROPE_BLOCKS, (i - n_pb) % ROPE_BLOCKS))

    return pl.pallas_call(
        _mla_proj_kernel,
        grid=(N_TOK // TB,),
        in_specs=[
            pl.BlockSpec((TB, D_MODEL), lambda i: (i, 0)),
            _mod_spec(TB, layer),
            _layer_spec((1, D_MODEL), layer),
            _layer_spec((D_MODEL, Q_RANK), j),
            _layer_spec((1, Q_RANK), j),
            _layer_spec((QKV_W, Q_RANK), j),
            _layer_spec((HEAD_PAD, 1), j),
            _layer_spec((D_MODEL, 2 * LANE), j),
            _layer_spec((1, KV_RANK), j),
            pl.BlockSpec((3, HEAD_PAD, TB), tab_idx),
        ],
        out_specs=[
            pl.BlockSpec((QKV_W, TB), lambda i: (0, i)),
            pl.BlockSpec((TB, KV_RANK), lambda i: (i, 0)),
            pl.BlockSpec((TB, HEAD_PAD), lambda i: (i, 0)),
        ],
        out_shape=[
            jax.ShapeDtypeStruct((QKV_W, N_TOK), BF16),
            jax.ShapeDtypeStruct((N_TOK, KV_RANK), F32),
            jax.ShapeDtypeStruct((N_TOK, HEAD_PAD), F32),
        ],
        compiler_params=_cparams(("parallel",)),
        name="mla_proj",
    )(x, p["mod"], p["norm_mix_g"], p["m_w_dq"], p["m_norm_q"], p["m_w_uq_t"], p["m_gq"], p["m_w_dkv"],
      p["m_norm_kv"], p["rope_t"])


def _kv_block_source(i):
    n_sb = DEC_BATCH * KV_LEN // TB
    per = KV_LEN // TB
    b = jnp.minimum(i // per, DEC_BATCH - 1)
    jj = i % per
    return jnp.logical_and(i < n_sb, jj < PAST_LEN // TB), b, jj


def _kv_expand_kernel(cckv_ref, ckrp_ref, ckv_ref, krp_ref, wukvt_ref, gk_ref, tabt_ref, k_ref, vt_ref):
    from_cache, _, _ = _kv_block_source(pl.program_id(0))
    ckv = jnp.where(from_cache, cckv_ref[...], ckv_ref[...])
    krp = jnp.where(from_cache, ckrp_ref[...], krp_ref[...])
    kvt = _dot_nt(wukvt_ref[...], ckv.astype(BF16))
    krpt = krp.T
    gk = gk_ref[...]
    row = lax.broadcasted_iota(jnp.int32, (HEAD_PAD, TB), 0)
    is_nope = row < QK_NOPE_DIM
    ones_row = jnp.where(row == 0, 1.0, 0.0)
    for hd in range(N_HEADS):
        blk = kvt[hd * HEAD_PAD:(hd + 1) * HEAD_PAD, :]
        kraw = jnp.where(is_nope, blk, krpt)
        ss = jnp.sum(kraw * kraw, axis=0, keepdims=True) * (1.0 / QK_HEAD_DIM)
        kn = kraw * lax.rsqrt(ss + RMS_EPS) * gk
        k_ref[:, hd * HEAD_PAD:(hd + 1) * HEAD_PAD] = _rope_rows(kn, tabt_ref).T.astype(BF16)
        vt_ref[hd * HEAD_PAD:(hd + 1) * HEAD_PAD, :] = jnp.where(is_nope, ones_row, blk).astype(BF16)


def _kv_expand(ckv, krp, p, layer):
    j = layer // 2
    n_sb = DEC_BATCH * KV_LEN // TB
    n_cache = PAST_LEN // TB
    n_pb = N_PROMPT // TB
    lat_blocks = DEC_SEQ // TB

    def tab_idx(i):
        from_cache, _, jj = _kv_block_source(i)
        return (0, 0, jnp.where((i >= n_sb) | from_cache, ROPE_BLOCKS, jj - n_cache))

    def cache_idx(i):
        _, b, jj = _kv_block_source(i)
        return (b, j, jnp.minimum(jj, n_cache - 1), 0)

    def tok_idx(i):
        _, b, jj = _kv_block_source(i)
        latent = n_pb + b * lat_blocks + jnp.maximum(jj - n_cache, 0)
        return (jnp.where(i < n_sb, latent, i - n_sb), 0)

    return pl.pallas_call(
        _kv_expand_kernel,
        grid=(N_KV_ROWS // TB,),
        in_specs=[
            pl.BlockSpec((None, None, TB, KV_RANK), cache_idx),
            pl.BlockSpec((None, None, TB, HEAD_PAD), cache_idx),
            pl.BlockSpec((TB, KV_RANK), tok_idx),
            pl.BlockSpec((TB, HEAD_PAD), tok_idx),
            _layer_spec((QKV_W, KV_RANK), j),
            _layer_spec((HEAD_PAD, 1), j),
            pl.BlockSpec((3, HEAD_PAD, TB), tab_idx),
        ],
        out_specs=[
            pl.BlockSpec((TB, QKV_W), lambda i: (i, 0)),
            pl.BlockSpec((QKV_W, TB), lambda i: (0, i)),
        ],
        out_shape=[
            jax.ShapeDtypeStruct((N_KV_ROWS, QKV_W), BF16),
            jax.ShapeDtypeStruct((QKV_W, N_KV_ROWS), BF16),
        ],
        compiler_params=_cparams(("parallel",)),
        name="mla_kv_expand",
    )(p["cache_ckv"], p["cache_krp"], ckv, krp, p["m_w_ukv_t"], p["m_gk"], p["rope_t"])


def _attn_kernel(qt_ref, k_ref, vt_ref, o_ref, s_a, s_b, *, t_k, hps):
    c = (1.0 / math.sqrt(QK_HEAD_DIM)) * math.log2(math.e)
    kc = min(TKC, t_k)
    n_chunks = t_k // kc
    per_it = min(ATTN_CHUNKS_PER_ITER, n_chunks)
    n_it = n_chunks // per_it
    bufs = (s_a, s_b)

    def rows(h):
        return slice(h * HEAD_PAD, (h + 1) * HEAD_PAD)

    def scores(h, off, m8):
        st = _dot(k_ref[pl.ds(off, kc), rows(h)], qt_ref[rows(h), :])
        bufs[h % 2][pl.ds(off, kc), :] = st
        return jnp.maximum(m8, jnp.max(st.reshape(kc // SUBLANE, SUBLANE, TQ), axis=0))

    def weigh(h, off, m, acc):
        pt = jnp.exp2((bufs[h % 2][pl.ds(off, kc), :] - m) * c).astype(BF16)
        return acc + _dot(vt_ref[rows(h), pl.ds(off, kc)], pt)

    def phase(h_scores, h_weigh, m):
        def body(it, carry):
            m8, acc = carry
            for u in range(per_it):
                off = (it * per_it + u) * kc
                off = off if isinstance(off, int) else pl.multiple_of(off, kc)
                if h_scores is not None:
                    m8 = scores(h_scores, off, m8)
                if h_weigh is not None:
                    acc = weigh(h_weigh, off, m, acc)
            return m8, acc

        init = (jnp.full((SUBLANE, TQ), NEG_BIG, F32), jnp.zeros((HEAD_PAD, TQ), F32))
        return body(0, init) if n_it == 1 else lax.fori_loop(0, n_it, body, init)

    outs = []
    m8, _ = phase(0, None, None)
    for h in range(1, hps + 1):
        m = jnp.max(m8, axis=0, keepdims=True)
        m8, acc = phase(h if h < hps else None, h - 1, m)
        outs.append(acc[QK_NOPE_DIM:, :] / acc[0:1, :])
    o_ref[...] = jnp.concatenate(outs, axis=0).T.astype(BF16)


def _attention(qt, k, vt, *, n_batch, t_q, t_k, q_row0, kv_row0, hps):
    nq = t_q // TQ
    q0 = q_row0 // TQ
    k0 = kv_row0 // t_k
    return pl.pallas_call(
        functools.partial(_attn_kernel, t_k=t_k, hps=hps),
        grid=(n_batch, N_HEADS // hps, nq),
        in_specs=[
            pl.BlockSpec((hps * HEAD_PAD, TQ), lambda b, h, i: (h, q0 + b * nq + i)),
            pl.BlockSpec((t_k, hps * HEAD_PAD), lambda b, h, i: (k0 + b, h)),
            pl.BlockSpec((hps * HEAD_PAD, t_k), lambda b, h, i: (h, k0 + b)),
        ],
        out_specs=pl.BlockSpec((TQ, hps * V_HEAD_DIM), lambda b, h, i: (b * nq + i, h)),
        out_shape=jax.ShapeDtypeStruct((n_batch * t_q, N_HEADS * V_HEAD_DIM), BF16),
        scratch_shapes=[pltpu.VMEM((t_k, TQ), F32), pltpu.VMEM((t_k, TQ), F32)],
        compiler_params=_cparams(("parallel", "parallel", "parallel")),
        name=f"mla_attention_tk{t_k}",
    )(qt, k, vt)


def _out_proj_kernel(ap_ref, as_ref, x_ref, mod_ref, wo_ref, o_ref):
    a = jnp.where(pl.program_id(0) < N_PROMPT // TB, ap_ref[...], as_ref[...])
    o_ref[...] = x_ref[...] + mod_ref[0][2:3] * _dot(a, wo_ref[...])


def _out_proj(attn_p, attn_s, x, p, layer):
    return pl.pallas_call(
        _out_proj_kernel,
        grid=(N_TOK // TB,),
        in_specs=_split_rows_specs(False) + [
            pl.BlockSpec((TB, D_MODEL), lambda i: (i, 0)),
            _mod_spec(TB, layer),
            _layer_spec((D_MODEL, D_MODEL), layer // 2),
        ],
        out_specs=pl.BlockSpec((TB, D_MODEL), lambda i: (i, 0)),
        out_shape=jax.ShapeDtypeStruct((N_TOK, D_MODEL), F32),
        compiler_params=_cparams(("parallel",)),
        name="mla_out_proj",
    )(attn_p, attn_s, x, p["mod"], p["m_w_o"])


def _mla_layer(x, p, layer):
    qt, ckv, krp = _mla_proj(x, p, layer)
    k, vt = _kv_expand(ckv, krp, p, layer)
    a_p = _attention(qt, k, vt, n_batch=BATCH, t_q=SEQ, t_k=SEQ, q_row0=0, kv_row0=DEC_BATCH * KV_LEN,
                     hps=N_HEADS)
    a_s = _attention(qt, k, vt, n_batch=DEC_BATCH, t_q=DEC_SEQ, t_k=KV_LEN, q_row0=N_PROMPT, kv_row0=0,
                     hps=4)
    x = _out_proj(a_p, a_s, x, p, layer)
    state_ckv = ckv[:N_PROMPT].reshape(BATCH, SEQ, KV_RANK)
    state_krope = krp[:N_PROMPT, QK_NOPE_DIM:QK_HEAD_DIM].reshape(BATCH, SEQ, QK_ROPE_DIM)
    return x, state_ckv, state_krope


def _tile_plan(cnt8, plan_ref, start_ref):
    tm = float(TM)
    cnt_row = cnt8[0:1, :]
    cnt_col = cnt8.T[:, 0:1]
    tiles_row = jnp.floor((cnt_row + (tm - 1.0)) * (1.0 / tm))
    tiles_col = jnp.floor((cnt_col + (tm - 1.0)) * (1.0 / tm))
    sub = lax.broadcasted_iota(jnp.int32, (LANE, LANE), 0).astype(F32)
    lan = lax.broadcasted_iota(jnp.int32, (LANE, LANE), 1).astype(F32)
    tile_end_row = jnp.sum(jnp.where(sub <= lan, tiles_col, 0.0), axis=0, keepdims=True)
    tile_end_col = jnp.sum(jnp.where(lan <= sub, tiles_row, 0.0), axis=1, keepdims=True)
    n_used = jnp.max(tile_end_row, axis=1, keepdims=True)
    start_col = (tile_end_col - tiles_col) * tm
    end_col = start_col + cnt_col
    cand = jnp.where(jnp.logical_and(lan > sub, tiles_row > 0.0), lan, float(LANE))
    next_col = jnp.min(cand, axis=1, keepdims=True)
    next_col = jnp.where(next_col < float(LANE), next_col, -1.0)
    n_lanes = plan_ref.shape[1]
    tidx = jnp.minimum(lax.broadcasted_iota(jnp.int32, (LANE, n_lanes), 1).astype(F32), n_used - 1.0)
    esub = lax.broadcasted_iota(jnp.int32, (LANE, n_lanes), 0).astype(F32)
    te_row = jnp.sum(jnp.where(tile_end_col <= tidx, 1.0, 0.0), axis=0, keepdims=True)
    mine = esub == te_row
    end_at = jnp.sum(jnp.where(mine, end_col, 0.0), axis=0, keepdims=True)
    tv_row = jnp.clip(end_at - tidx[0:1, :] * tm, 0.0, tm)
    nx_row = jnp.sum(jnp.where(mine, next_col, 0.0), axis=0, keepdims=True)
    nu_row = jnp.broadcast_to(n_used, (1, n_lanes))
    plan_ref[...] = jnp.concatenate([te_row, tv_row, nx_row, nu_row, jnp.zeros((SUBLANE - 4, n_lanes), F32)], axis=0)
    start_ref[...] = jnp.broadcast_to(start_col, (LANE, LANE))


def _route_kernel(x_ref, mod_ref, g_ref, wr_ref, br_ref, tri_ref, h_ref, meta_ref, metat_ref, plan_ref, start_ref,
                  carry):
    i = pl.program_id(0)

    @pl.when(i == 0)
    def _():
        carry[...] = jnp.zeros_like(carry)

    x = x_ref[...]
    m = mod_ref[0]
    h = _rms(x, g_ref[...]) * (1.0 + m[4:5]) + m[3:4]
    h_ref[...] = _pack_bf16_pairs(h)
    logits = _dot(h.astype(BF16), wr_ref[...]) + br_ref[...]
    lane = lax.broadcasted_iota(jnp.int32, logits.shape, 1).astype(F32)
    work = logits
    sel = jnp.zeros(logits.shape, F32)
    hits, tops = [], []
    for k in range(TOP_K):
        mk = jnp.max(work, axis=-1, keepdims=True)
        first = jnp.min(jnp.where(work == mk, lane, float(LANE)), axis=-1, keepdims=True)
        hit = lane == first
        sel = jnp.where(hit, 1.0, sel)
        work = jnp.where(hit, -jnp.inf, work)
        hits.append((hit, first))
        tops.append(mk)
    es = [jnp.exp(t - tops[0]) for t in tops]
    denom = es[0] + es[1] + es[2] + es[3]
    pos = _dot(tri_ref[...], sel.astype(BF16)) + carry[0:1, :]
    carry[...] = carry[...] + jnp.sum(sel, axis=0, keepdims=True)

    @pl.when(i == pl.num_programs(0) - 1)
    def _():
        _tile_plan(carry[...], plan_ref, start_ref)

    meta = jnp.zeros(logits.shape, F32)
    for k in range(TOP_K):
        hit, first = hits[k]
        pk = jnp.sum(jnp.where(hit, pos, 0.0), axis=-1, keepdims=True)
        meta = jnp.where(lane == float(k), first, meta)
        meta = jnp.where(lane == float(TOP_K + k), es[k] / denom, meta)
        meta = jnp.where(lane == float(2 * TOP_K + k), pk, meta)
    meta_ref[...] = meta
    metat_ref[...] = meta.T[:2 * SUBLANE, :]


def _route(x, p, layer):
    tb = ROUTE_TB
    return pl.pallas_call(
        _route_kernel,
        grid=(N_TOK // tb,),
        in_specs=[
            pl.BlockSpec((tb, D_MODEL), lambda i: (i, 0)),
            _mod_spec(tb, layer),
            _layer_spec((1, D_MODEL), layer),
            _layer_spec((D_MODEL, LANE), layer),
            _layer_spec((1, LANE), layer),
            _const_spec((tb, tb)),
        ],
        out_specs=[
            pl.BlockSpec((tb, D_MODEL // 2), lambda i: (i, 0)),
            pl.BlockSpec((tb, LANE), lambda i: (i, 0)),
            pl.BlockSpec((2 * SUBLANE, tb), lambda i: (0, i)),
            _const_spec((SUBLANE, PLAN_LANES)),
            _const_spec((LANE, LANE)),
        ],
        out_shape=[
            jax.ShapeDtypeStruct((N_TOK, D_MODEL // 2), jnp.uint32),
            jax.ShapeDtypeStruct((N_TOK, LANE), F32),
            jax.ShapeDtypeStruct((2 * SUBLANE, N_TOK), F32),
            jax.ShapeDtypeStruct((SUBLANE, PLAN_LANES), F32),
            jax.ShapeDtypeStruct((LANE, LANE), F32),
        ],
        scratch_shapes=[pltpu.VMEM((SUBLANE, LANE), F32)],
        compiler_params=_cparams(("arbitrary",)),
        name="moe_route",
    )(x, p["mod"], p["norm_ffn_g"], p["e_w_router"], p["e_b_router"], p["tri"])


def _slots_kernel(start_ref, metat_ref, dest_ref):
    start_col = start_ref[:, 0:1]
    esub = lax.broadcasted_iota(jnp.int32, (LANE, SLOT_TB), 0).astype(F32)
    rows = []
    for k in range(TOP_K):
        e = metat_ref[k:k + 1, :]
        first = jnp.sum(jnp.where(esub == e, start_col, 0.0), axis=0, keepdims=True)
        rows.append(first + metat_ref[2 * TOP_K + k:2 * TOP_K + k + 1, :])
    dest_ref[...] = jnp.concatenate(rows, axis=0).astype(jnp.int32)


def _slots(start, meta_t):
    return pl.pallas_call(
        _slots_kernel,
        grid=(N_TOK // SLOT_TB,),
        in_specs=[
            _const_spec((LANE, LANE)),
            pl.BlockSpec((2 * SUBLANE, SLOT_TB), lambda i: (0, i)),
        ],
        out_specs=pl.BlockSpec((TOP_K, SLOT_TB), lambda i: (0, i)),
        out_shape=jax.ShapeDtypeStruct((TOP_K, N_TOK), jnp.int32),
        compiler_params=_cparams(("parallel",)),
        name="moe_slots",
    )(start, meta_t)


def _sc_gather(table, idx, ch):
    b, w = idx.shape[0], table.shape[1]
    per_w = b // SC_WORKERS
    n_ch = per_w // ch
    assert per_w * SC_WORKERS == b and n_ch * ch == per_w and n_ch % 2 == 0
    mesh = plsc.VectorSubcoreMesh(core_axis_name="c", subcore_axis_name="s")

    @functools.partial(
        pl.kernel, mesh=mesh,
        out_type=jax.ShapeDtypeStruct((b, w), table.dtype),
        scratch_types=[
            pltpu.VMEM((n_ch, ch), jnp.int32),
            pltpu.VMEM((ch, w), table.dtype),
            pltpu.VMEM((ch, w), table.dtype),
            pltpu.SemaphoreType.DMA, pltpu.SemaphoreType.DMA,
            pltpu.SemaphoreType.DMA, pltpu.SemaphoreType.DMA,
        ],
        name="sc_row_gather",
    )
    def gather_rows(table_hbm, idx_hbm, out_hbm, idx_v, buf0, buf1, g0, g1, s0, s1):
        wid = lax.axis_index("s") * SC_CORES + lax.axis_index("c")
        base = wid * per_w
        pltpu.sync_copy(idx_hbm.at[wid], idx_v)

        def gather(j, buf, sem):
            return pltpu.make_async_copy(table_hbm.at[idx_v.at[j]], buf, sem)

        def store(j, buf, sem):
            return pltpu.make_async_copy(buf, out_hbm.at[pl.ds(base + j * ch, ch)], sem)

        gather(0, buf0, g0).start()

        @pl.loop(0, n_ch, step=2)
        def _(j):
            @pl.when(j > 0)
            def _():
                store(j - 1, buf1, s1).wait()

            gather(j + 1, buf1, g1).start()
            gather(j, buf0, g0).wait()
            store(j, buf0, s0).start()
            gather(j + 1, buf1, g1).wait()
            store(j + 1, buf1, s1).start()
            store(j, buf0, s0).wait()

            @pl.when(j + 2 < n_ch)
            def _():
                gather(j + 2, buf0, g0).start()

        store(n_ch - 1, buf1, s1).wait()

    return gather_rows(table, idx.reshape(SC_WORKERS, n_ch, ch))


def _sc_dispatch(rows, dest_t, ch):
    n, w = rows.shape
    per_w = n // SC_WORKERS
    n_ch = per_w // ch
    assert per_w * SC_WORKERS == n and n_ch * ch == per_w and n_ch % 2 == 0
    mesh = plsc.VectorSubcoreMesh(core_axis_name="c", subcore_axis_name="s")
    idx = dest_t.reshape(TOP_K, SC_WORKERS, n_ch, ch)

    @functools.partial(
        pl.kernel, mesh=mesh,
        out_type=jax.ShapeDtypeStruct((N_SLOTS, w), rows.dtype),
        scratch_types=[
            pltpu.VMEM((TOP_K * n_ch, ch), jnp.int32),
            pltpu.VMEM((ch, w), rows.dtype),
            pltpu.VMEM((ch, w), rows.dtype),
            pltpu.SemaphoreType.DMA, pltpu.SemaphoreType.DMA,
            pltpu.SemaphoreType.DMA, pltpu.SemaphoreType.DMA,
        ],
        name="sc_row_dispatch",
    )
    def dispatch_rows(rows_hbm, idx_hbm, out_hbm, idx_v, buf0, buf1, l0, l1, s0, s1):
        wid = lax.axis_index("s") * SC_CORES + lax.axis_index("c")
        base = wid * per_w
        for k in range(TOP_K):
            pltpu.sync_copy(idx_hbm.at[k, wid], idx_v.at[pl.ds(k * n_ch, n_ch)])

        def load(j, buf, sem):
            return pltpu.make_async_copy(rows_hbm.at[pl.ds(base + j * ch, ch)], buf, sem)

        def scatter(j, k, buf, sem):
            return pltpu.make_async_copy(buf, out_hbm.at[idx_v.at[k * n_ch + j]], sem)

        load(0, buf0, l0).start()

        @pl.loop(0, n_ch, step=2)
        def _(j):
            load(j + 1, buf1, l1).start()
            load(j, buf0, l0).wait()
            for k in range(TOP_K):
                scatter(j, k, buf0, s0).start()
            load(j + 1, buf1, l1).wait()
            for k in range(TOP_K):
                scatter(j + 1, k, buf1, s1).start()
            for k in range(TOP_K):
                scatter(j, k, buf0, s0).wait()

            @pl.when(j + 2 < n_ch)
            def _():
                load(j + 2, buf0, l0).start()

            for k in range(TOP_K):
                scatter(j + 1, k, buf1, s1).wait()

    return dispatch_rows(rows, idx)


def _deinterleave_matrix():
    src = jnp.arange(2 * LANE)[:, None]
    dst = jnp.arange(2 * LANE)[None, :]
    want = jnp.where(dst < LANE, 2 * dst, 2 * (dst - LANE) + 1)
    return (src == want).astype(BF16)


def _expert_kernel(te_ref, nu_ref, tv_ref, nx_ref, x_ref, wgu_hbm, bgu_ref, wd_hbm, bd_ref, perm_ref, o_ref,
                   wgu_st, wd_st, wgu_bf, wd_bf, sems, *, layer):
    i = pl.program_id(0)
    prev = te_ref[jnp.maximum(i - 1, 0)]
    fresh = jnp.logical_or(i == 0, te_ref[i] != prev)

    def fetch(e):
        return (pltpu.make_async_copy(wgu_hbm.at[layer, e], wgu_st, sems.at[0]),
                pltpu.make_async_copy(wd_hbm.at[layer, e], wd_st, sems.at[1]))

    @pl.when(i == 0)
    def _():
        for cp in fetch(te_ref[0]):
            cp.start()

    @pl.when(jnp.logical_and(fresh, i < nu_ref[0]))
    def _():
        for cp in fetch(te_ref[i]):
            cp.wait()
        for b in range(2 * D_FF // (2 * LANE)):
            sl = slice(b * 2 * LANE, (b + 1) * 2 * LANE)
            wgu_bf[:, sl] = _dot(wgu_st[:, sl].astype(BF16), perm_ref[...]).astype(BF16)
        wd_bf[...] = wd_st[...].astype(BF16)

        @pl.when(nx_ref[i] >= 0)
        def _():
            for cp in fetch(nx_ref[i]):
                cp.start()

    @pl.when(i < nu_ref[0])
    def _():
        row = lax.broadcasted_iota(jnp.int32, (TM, D_MODEL // 2), 0)
        w = jnp.where(row < tv_ref[i], x_ref[...], jnp.uint32(0))
        x = _unpack_bf16_pairs(w).astype(BF16)
        gu = _dot(x, wgu_bf[...]) + bgu_ref[...]
        acts = []
        for b in range(D_FF // LANE):
            glu = jnp.minimum(gu[:, b * 2 * LANE:b * 2 * LANE + LANE], SWIGLU_LIMIT)
            lin = jnp.clip(gu[:, b * 2 * LANE + LANE:(b + 1) * 2 * LANE], -SWIGLU_LIMIT, SWIGLU_LIMIT)
            acts.append((glu * jax.nn.sigmoid(SWIGLU_ALPHA * glu) * (lin + 1.0)).astype(BF16))
        act = jnp.concatenate(acts, axis=1)
        o_ref[...] = _pack_bf16_pairs(_dot(act, wd_bf[...]) + bd_ref[...])


def _experts(buf, tile_expert, n_used, tile_valid, tile_next, p, layer):
    def row_idx(i, te, nu, tv, nx):
        return (jnp.minimum(i, nu[0] - 1), 0)

    def b_idx(i, te, nu, tv, nx):
        return (layer, te[i], 0, 0)

    grid_spec = pltpu.PrefetchScalarGridSpec(
        num_scalar_prefetch=4,
        grid=(N_TILES,),
        in_specs=[
            pl.BlockSpec((TM, D_MODEL // 2), row_idx),
            pl.BlockSpec(memory_space=pl.ANY),
            pl.BlockSpec((None, None, 1, 2 * D_FF), b_idx),
            pl.BlockSpec(memory_space=pl.ANY),
            pl.BlockSpec((None, None, 1, D_MODEL), b_idx),
            _const_spec((2 * LANE, 2 * LANE)),
        ],
        out_specs=pl.BlockSpec((TM, D_MODEL // 2), row_idx),
        scratch_shapes=[
            pltpu.VMEM((D_MODEL, 2 * D_FF), F32),
            pltpu.VMEM((D_FF, D_MODEL), F32),
            pltpu.VMEM((D_MODEL, 2 * D_FF), BF16),
            pltpu.VMEM((D_FF, D_MODEL), BF16),
            pltpu.SemaphoreType.DMA((2,)),
        ],
    )
    return pl.pallas_call(
        functools.partial(_expert_kernel, layer=layer),
        grid_spec=grid_spec,
        out_shape=jax.ShapeDtypeStruct((N_SLOTS, D_MODEL // 2), jnp.uint32),
        compiler_params=_cparams(("arbitrary",)),
        name="moe_experts",
    )(tile_expert, n_used, tile_valid, tile_next, buf, p["e_w_gu"], p["e_b_gu"], p["e_w_down"], p["e_b_down"],
      p["deinterleave"])


def _pack_bf16_pairs(v):
    half = v.shape[1] // 2
    bits = pltpu.bitcast(v.astype(BF16).astype(F32), jnp.uint32)
    return (bits[:, half:] & jnp.uint32(0xFFFF0000)) | (bits[:, :half] >> 16)


def _unpack_bf16_pairs(w):
    return jnp.concatenate([pltpu.bitcast(w << 16, F32), pltpu.bitcast(w & jnp.uint32(0xFFFF0000), F32)],
                           axis=1)


def _combine_kernel(x_ref, mod_ref, y_ref, w_ref, o_ref):
    w = w_ref[:, TOP_K:2 * TOP_K]
    y = _unpack_bf16_pairs(y_ref[0]) * w[:, 0:1]
    for k in range(1, TOP_K):
        y = y + _unpack_bf16_pairs(y_ref[k]) * w[:, k:k + 1]
    o_ref[...] = x_ref[...] + mod_ref[0][5:6] * y


def _combine(x, y4, meta, p, layer):
    return pl.pallas_call(
        _combine_kernel,
        grid=(N_TOK // TB,),
        in_specs=[
            pl.BlockSpec((TB, D_MODEL), lambda i: (i, 0)),
            _mod_spec(TB, layer),
            pl.BlockSpec((TOP_K, TB, D_MODEL // 2), lambda i: (0, i, 0)),
            pl.BlockSpec((TB, LANE), lambda i: (i, 0)),
        ],
        out_specs=pl.BlockSpec((TB, D_MODEL), lambda i: (i, 0)),
        out_shape=jax.ShapeDtypeStruct((N_TOK, D_MODEL), F32),
        compiler_params=_cparams(("parallel",)),
        name="moe_combine",
    )(x, p["mod"], y4, meta)


def _moe_layer(x, p, layer):
    hp, meta, meta_t, plan, start = _route(x, p, layer)
    plan = plan[:4, :N_TILES].astype(jnp.int32)
    tile_expert, tile_valid, tile_next, n_used = plan[0], plan[1], plan[2], plan[3, :1]
    dest_t = _slots(start, meta_t)
    buf = _sc_dispatch(hp, dest_t, 64)
    yb = _experts(buf, tile_expert, n_used, tile_valid, tile_next, p, layer)
    y4 = _sc_gather(yb, dest_t.reshape(-1), 64).reshape(TOP_K, N_TOK, D_MODEL // 2)
    return _combine(x, y4, meta, p, layer)


def _prepare(c, cache_ckv, cache_krope, c_ctx, norm_mix_g, norm_ffn_g, w_mod, b_mod, g_w_in, g_b_in, g_norm_v,
             g_w_s, g_b_s, g_w_out, m_w_dq, m_norm_q, m_w_uq, m_w_dkv, m_norm_kv, m_w_ukv, m_qk_norm_q,
             m_qk_norm_k, m_w_o, e_w_router, e_b_router, e_w_gu, e_b_gu, e_w_down, e_b_down):
    n_mla = m_w_dq.shape[0]
    cond = jnp.concatenate([c_ctx[None, :], c, jnp.zeros((SUBLANE - N_COND, D_MODEL), F32)], axis=0)
    wdkv = jnp.concatenate([m_w_dkv[..., :KV_RANK], jnp.zeros((n_mla, D_MODEL, QK_NOPE_DIM), F32),
                            m_w_dkv[..., KV_RANK:], jnp.zeros((n_mla, D_MODEL, HEAD_PAD - QK_HEAD_DIM), F32)],
                           axis=-1)
    w_uq = jnp.pad(m_w_uq.reshape(n_mla, Q_RANK, N_HEADS, QK_HEAD_DIM),
                   ((0, 0), (0, 0), (0, 0), (0, HEAD_PAD - QK_HEAD_DIM))).reshape(n_mla, Q_RANK, QKV_W)

    def gain_col(g):
        return jnp.pad(g, ((0, 0), (0, HEAD_PAD - QK_HEAD_DIM)))[:, :, None]

    return {
        "mod": _modulation(cond, w_mod, b_mod),
        "rope_t": _rope_tables(),
        "norm_mix_g": norm_mix_g[:, None, :],
        "norm_ffn_g": norm_ffn_g[:, None, :],
        "g_w_in": g_w_in.astype(BF16),
        "g_b_in": g_b_in[:, None, :],
        "g_norm_v": g_norm_v[:, None, :],
        "g_w_s": g_w_s.astype(BF16),
        "g_b_st": jnp.swapaxes(g_b_s, 1, 2),
        "g_w_out": g_w_out.astype(BF16),
        "m_w_dq": m_w_dq.astype(BF16),
        "m_norm_q": m_norm_q[:, None, :],
        "m_w_uq_t": jnp.swapaxes(w_uq, 1, 2).astype(BF16),
        "m_gq": gain_col(m_qk_norm_q),
        "m_w_dkv": wdkv.astype(BF16),
        "m_norm_kv": m_norm_kv[:, None, :],
        "m_w_ukv_t": jnp.swapaxes(m_w_ukv, 1, 2).astype(BF16),
        "m_gk": gain_col(m_qk_norm_k),
        "m_w_o": m_w_o.astype(BF16),
        "cache_ckv": cache_ckv,
        "cache_krp": jnp.pad(cache_krope, ((0, 0), (0, 0), (0, 0), (QK_NOPE_DIM, HEAD_PAD - QK_HEAD_DIM))),
        "e_w_router": jnp.pad(e_w_router, ((0, 0), (0, 0), (0, LANE - N_EXPERTS))).astype(BF16),
        "e_b_router": jnp.pad(e_b_router, ((0, 0), (0, LANE - N_EXPERTS)), constant_values=NEG_BIG)[:, None, :],
        "tri": jnp.tri(ROUTE_TB, ROUTE_TB, -1, dtype=BF16),
        "e_w_gu": e_w_gu,
        "e_b_gu": e_b_gu.reshape(DEPTH, N_EXPERTS, D_FF // LANE, LANE, 2).swapaxes(3, 4).reshape(
            DEPTH, N_EXPERTS, 1, 2 * D_FF),
        "e_w_down": e_w_down,
        "e_b_down": e_b_down[:, :, None, :],
        "deinterleave": _deinterleave_matrix(),
    }


def kernel(x_prompt, x_sample, c, cache_ckv, cache_krope, c_ctx, norm_mix_g, norm_ffn_g, w_mod, b_mod,
           g_w_in, g_b_in, g_norm_v, g_w_s, g_b_s, g_w_out, m_w_dq, m_norm_q, m_w_uq, m_w_dkv,
           m_norm_kv, m_w_ukv, m_qk_norm_q, m_qk_norm_k, m_w_o, e_w_router, e_b_router, e_w_gu,
           e_b_gu, e_w_down, e_b_down):
    p = _prepare(c, cache_ckv, cache_krope, c_ctx, norm_mix_g, norm_ffn_g, w_mod, b_mod, g_w_in, g_b_in,
                 g_norm_v, g_w_s, g_b_s, g_w_out, m_w_dq, m_norm_q, m_w_uq, m_w_dkv, m_norm_kv, m_w_ukv,
                 m_qk_norm_q, m_qk_norm_k, m_w_o, e_w_router, e_b_router, e_w_gu, e_b_gu, e_w_down, e_b_down)
    x = None
    ckv_states, krope_states = [], []
    for layer in range(DEPTH):
        if layer % 2 == 0:
            if x is None:
                x = _gmlp_layer(x_prompt.reshape(N_PROMPT, D_MODEL), x_sample.reshape(N_SAMPLE, D_MODEL), p, layer)
            else:
                x = _gmlp_layer(x, x, p, layer)
        else:
            x, s_ckv, s_krope = _mla_layer(x, p, layer)
            ckv_states.append(s_ckv)
            krope_states.append(s_krope)
        x = _moe_layer(x, p, layer)
    y_prompt = x[:N_PROMPT].reshape(BATCH, SEQ, D_MODEL)
    y_sample = x[N_PROMPT:].reshape(DEC_BATCH, DEC_SEQ, D_MODEL)
    return (y_prompt, y_sample, jnp.stack(ckv_states, axis=1), jnp.stack(krope_states, axis=1))
```

```python
import functools
import math

import jax
import jax.numpy as jnp
from jax import lax
from jax.experimental import pallas as pl
from jax.experimental.pallas import tpu as pltpu
from jax.experimental.pallas import tpu_sc as plsc

F32 = jnp.float32
BF16 = jnp.bfloat16

D_MODEL = 1024
BATCH = 32
SEQ = 256
DEPTH = 4
DEC_BATCH = 2
DEC_SEQ = 4096
PAST_LEN = 512
GRID_W = 64
RMS_EPS = 1e-6
GMLP_WIDTH = 2 * D_MODEL
GMLP_GROUPS = 8
GROUP_W = GMLP_WIDTH // GMLP_GROUPS
CHUNK = 128
N_HEADS = 16
QK_NOPE_DIM = 64
QK_ROPE_DIM = 32
QK_HEAD_DIM = QK_NOPE_DIM + QK_ROPE_DIM
V_HEAD_DIM = 64
Q_RANK = 256
KV_RANK = 128
ROPE_THETA = 10000.0
N_EXPERTS = 32
TOP_K = 4
D_FF = D_MODEL
SWIGLU_LIMIT = 7.0
SWIGLU_ALPHA = 1.702

N_PROMPT = BATCH * SEQ
N_SAMPLE = DEC_BATCH * DEC_SEQ
N_TOK = N_PROMPT + N_SAMPLE
N_COND = 1 + DEC_BATCH
KV_LEN = PAST_LEN + DEC_SEQ
N_KV_ROWS = DEC_BATCH * KV_LEN + N_PROMPT

LANE = 128
SUBLANE = 8
HEAD_PAD = LANE
QKV_W = N_HEADS * HEAD_PAD
VMEM_LIMIT = 56 * 1024 * 1024

TB = 256
TQ = 256
TKC = 256
ATTN_CHUNKS_PER_ITER = 9
TM = 512
N_TILES = N_TOK * TOP_K // TM + N_EXPERTS
N_SLOTS = N_TILES * TM
PLAN_LANES = -(-N_TILES // LANE) * LANE
SLOT_TB = 2048
ROUTE_TB = 512
MOE_PARTS = 2
SC_CORES = 2
SC_WORKERS = SC_CORES * 16
ROPE_BLOCKS = DEC_SEQ // TB
NEG_BIG = -1e30


def _cparams(sem):
    return pltpu.CompilerParams(dimension_semantics=sem, vmem_limit_bytes=VMEM_LIMIT)


def _cond_of_block(i, tb):
    n_p = N_PROMPT // tb
    per = DEC_SEQ // tb
    return jnp.where(i < n_p, 0, 1 + (i - n_p) // per)


def _rms(x, g, n=None):
    n = x.shape[-1] if n is None else n
    ss = jnp.sum(x * x, axis=-1, keepdims=True) * (1.0 / n)
    return x * lax.rsqrt(ss + RMS_EPS) * g


def _dot(a, b):
    return jnp.dot(a, b, preferred_element_type=F32)


def _mod_kernel(c_ref, w_ref, b_ref, o_ref):
    c = c_ref[...]
    s = c * jax.nn.sigmoid(c)
    o_ref[0] = _dot(s.astype(BF16), w_ref[0].astype(BF16)) + b_ref[0]


def _modulation(cond, w_mod, b_mod):
    tn = 1536
    out = pl.pallas_call(
        _mod_kernel,
        grid=(DEPTH, 6 * D_MODEL // tn),
        in_specs=[
            pl.BlockSpec((SUBLANE, D_MODEL), lambda l, j: (0, 0)),
            pl.BlockSpec((1, D_MODEL, tn), lambda l, j: (l, 0, j)),
            pl.BlockSpec((1, 1, tn), lambda l, j: (l, 0, j)),
        ],
        out_specs=pl.BlockSpec((1, SUBLANE, tn), lambda l, j: (l, 0, j)),
        out_shape=jax.ShapeDtypeStruct((DEPTH, SUBLANE, 6 * D_MODEL), F32),
        compiler_params=_cparams(("parallel", "parallel")),
        name="adaln_mod",
    )(cond, w_mod, b_mod.reshape(DEPTH, 1, 6 * D_MODEL))
    m = out[:, :N_COND].reshape(DEPTH, N_COND, 6, D_MODEL)
    return jnp.pad(m, ((0, 0), (0, 0), (0, SUBLANE - 6), (0, 0)))


def _mod_spec(tb, layer, first_block=0):
    return pl.BlockSpec((None, 1, SUBLANE, D_MODEL),
                        lambda i: (layer, _cond_of_block(i + first_block, tb), 0, 0))


def _const_spec(shape):
    nd = len(shape)
    return pl.BlockSpec(shape, lambda *_: (0,) * nd)


def _layer_spec(shape, j):
    nd = len(shape)
    return pl.BlockSpec((None,) + tuple(shape), lambda *_: (j,) + (0,) * nd)


def _gmlp_kernel(xp_ref, xs_ref, mod_ref, g_ref, win_ref, bin_ref, gv_ref, ws_ref, bst_ref, wout_ref, o_ref):
    x = jnp.where(pl.program_id(0) < N_PROMPT // TB, xp_ref[...], xs_ref[...])
    m = mod_ref[0]
    h = _rms(x, g_ref[...]) * (1.0 + m[1:2]) + m[0:1]
    hb = h.astype(BF16)
    zv = jax.nn.gelu(_dot(hb, win_ref[:, GMLP_WIDTH:]) + bin_ref[:, GMLP_WIDTH:], approximate=True)
    vn = _rms(zv, gv_ref[...]).astype(BF16)
    rows = []
    for c in range(TB // CHUNK):
        cols = []
        for g in range(GMLP_GROUPS):
            blk = vn[c * CHUNK:(c + 1) * CHUNK, g * GROUP_W:(g + 1) * GROUP_W]
            cols.append(_dot(ws_ref[g], blk) + bst_ref[:, g:g + 1])
        rows.append(jnp.concatenate(cols, axis=1))
    vm = jnp.concatenate(rows, axis=0)
    u = jax.nn.gelu(_dot(hb, win_ref[:, :GMLP_WIDTH]) + bin_ref[:, :GMLP_WIDTH], approximate=True)
    d = _dot((u * vm).astype(BF16), wout_ref[...])
    o_ref[...] = x + m[2:3] * d


def _split_rows_specs():
    n_pb = N_PROMPT // TB
    return [pl.BlockSpec((TB, D_MODEL), lambda i: (jnp.minimum(i, n_pb - 1), 0)),
            pl.BlockSpec((TB, D_MODEL), lambda i: (jnp.maximum(i - n_pb, 0), 0))]


def _gmlp_layer(x_p, x_s, p, layer):
    j = layer // 2
    return pl.pallas_call(
        _gmlp_kernel,
        grid=(N_TOK // TB,),
        in_specs=_split_rows_specs() + [
            _mod_spec(TB, layer),
            _layer_spec((1, D_MODEL), layer),
            _layer_spec((D_MODEL, 2 * GMLP_WIDTH), j),
            _layer_spec((1, 2 * GMLP_WIDTH), j),
            _layer_spec((1, GMLP_WIDTH), j),
            _layer_spec((GMLP_GROUPS, CHUNK, CHUNK), j),
            _layer_spec((CHUNK, GMLP_GROUPS), j),
            _layer_spec((GMLP_WIDTH, D_MODEL), j),
        ],
        out_specs=pl.BlockSpec((TB, D_MODEL), lambda i: (i, 0)),
        out_shape=jax.ShapeDtypeStruct((N_TOK, D_MODEL), F32),
        compiler_params=_cparams(("parallel",)),
        name="gmlp_mixer",
    )(x_p, x_s, p["mod"], p["norm_mix_g"], p["g_w_in"], p["g_b_in"], p["g_norm_v"], p["g_w_s"], p["g_b_st"],
      p["g_w_out"])


def _rope_tables():
    t = jnp.arange(DEC_SEQ)
    row_id = (t // GRID_W).astype(F32)
    col_id = (t % GRID_W).astype(F32)
    axis_dim = QK_ROPE_DIM // 2
    inv_freq = ROPE_THETA ** (-jnp.arange(0, axis_dim, 2, dtype=F32) / axis_dim)
    ang = jnp.stack([row_id[:, None] * inv_freq, col_id[:, None] * inv_freq], axis=1)
    cos, sin = jnp.cos(ang), jnp.sin(ang)
    zeros = jnp.zeros_like(sin)
    cos_l = jnp.concatenate([cos, cos], axis=-1).reshape(DEC_SEQ, QK_ROPE_DIM)
    s1_l = jnp.concatenate([-sin, zeros], axis=-1).reshape(DEC_SEQ, QK_ROPE_DIM)
    s2_l = jnp.concatenate([zeros, sin], axis=-1).reshape(DEC_SEQ, QK_ROPE_DIM)

    def widen(rope_part, nope_fill):
        left = jnp.full((DEC_SEQ, QK_NOPE_DIM), nope_fill, F32)
        right = jnp.zeros((DEC_SEQ, HEAD_PAD - QK_HEAD_DIM), F32)
        return jnp.concatenate([left, rope_part, right], axis=-1)

    pos = jnp.stack([widen(cos_l, 1.0), widen(s1_l, 0.0), widen(s2_l, 0.0)])
    ident_c = jnp.concatenate([jnp.ones((TB, QK_HEAD_DIM), F32),
                               jnp.zeros((TB, HEAD_PAD - QK_HEAD_DIM), F32)], axis=-1)
    ident = jnp.stack([ident_c, jnp.zeros_like(ident_c), jnp.zeros_like(ident_c)])
    return jnp.swapaxes(jnp.concatenate([pos, ident], axis=1), 1, 2)


def _dot_nt(a, b):
    return lax.dot_general(a, b, (((1,), (1,)), ((), ())), preferred_element_type=F32)


def _shift_rows(x, n):
    n = n % x.shape[0]
    return jnp.concatenate([x[n:], x[:n]], axis=0)


def _rope_rows(xn, tabt_ref):
    half = QK_ROPE_DIM // 4
    return xn * tabt_ref[0] + _shift_rows(xn, half) * tabt_ref[1] + _shift_rows(xn, -half) * tabt_ref[2]


def _mla_proj_kernel(xp_ref, xs_ref, mod_ref, g_ref, wdq_ref, nq_ref, wuqt_ref, gq_ref, wdkv_ref, nkv_ref,
                     tabt_ref, qt_ref, ckv_ref, krp_ref):
    x = jnp.where(pl.program_id(0) < N_PROMPT // TB, xp_ref[...], xs_ref[...])
    m = mod_ref[0]
    h = _rms(x, g_ref[...]) * (1.0 + m[1:2]) + m[0:1]
    hb = h.astype(BF16)
    cq = _rms(_dot(hb, wdq_ref[...]), nq_ref[...])
    qt = _dot_nt(wuqt_ref[...], cq.astype(BF16))
    gq = gq_ref[...]
    for hd in range(N_HEADS):
        qh = qt[hd * HEAD_PAD:(hd + 1) * HEAD_PAD, :]
        ss = jnp.sum(qh * qh, axis=0, keepdims=True) * (1.0 / QK_HEAD_DIM)
        qn = qh * lax.rsqrt(ss + RMS_EPS) * gq
        qt_ref[hd * HEAD_PAD:(hd + 1) * HEAD_PAD, :] = _rope_rows(qn, tabt_ref).astype(BF16)
    kva = _dot(hb, wdkv_ref[...])
    ckv_ref[...] = _rms(kva[:, :KV_RANK], nkv_ref[...])
    krp_ref[...] = kva[:, KV_RANK:]


def _mla_proj(x_p, x_s, p, layer):
    j = layer // 2
    n_pb = N_PROMPT // TB

    def tab_idx(i):
        return (0, 0, jnp.where(i < n_pb, ROPE_BLOCKS, (i - n_pb) % ROPE_BLOCKS))

    return pl.pallas_call(
        _mla_proj_kernel,
        grid=(N_TOK // TB,),
        in_specs=_split_rows_specs() + [
            _mod_spec(TB, layer),
            _layer_spec((1, D_MODEL), layer),
            _layer_spec((D_MODEL, Q_RANK), j),
            _layer_spec((1, Q_RANK), j),
            _layer_spec((QKV_W, Q_RANK), j),
            _layer_spec((HEAD_PAD, 1), j),
            _layer_spec((D_MODEL, 2 * LANE), j),
            _layer_spec((1, KV_RANK), j),
            pl.BlockSpec((3, HEAD_PAD, TB), tab_idx),
        ],
        out_specs=[
            pl.BlockSpec((QKV_W, TB), lambda i: (0, i)),
            pl.BlockSpec((TB, KV_RANK), lambda i: (i, 0)),
            pl.BlockSpec((TB, HEAD_PAD), lambda i: (i, 0)),
        ],
        out_shape=[
            jax.ShapeDtypeStruct((QKV_W, N_TOK), BF16),
            jax.ShapeDtypeStruct((N_TOK, KV_RANK), F32),
            jax.ShapeDtypeStruct((N_TOK, HEAD_PAD), F32),
        ],
        compiler_params=_cparams(("parallel",)),
        name="mla_proj",
    )(x_p, x_s, p["mod"], p["norm_mix_g"], p["m_w_dq"], p["m_norm_q"], p["m_w_uq_t"], p["m_gq"], p["m_w_dkv"],
      p["m_norm_kv"], p["rope_t"])


def _kv_block_source(i):
    n_sb = DEC_BATCH * KV_LEN // TB
    per = KV_LEN // TB
    b = jnp.minimum(i // per, DEC_BATCH - 1)
    jj = i % per
    return jnp.logical_and(i < n_sb, jj < PAST_LEN // TB), b, jj


def _kv_expand_kernel(cckv_ref, ckrp_ref, ckv_ref, krp_ref, wukvt_ref, gk_ref, tabt_ref, k_ref, vt_ref):
    from_cache, _, _ = _kv_block_source(pl.program_id(0))
    ckv = jnp.where(from_cache, cckv_ref[...], ckv_ref[...])
    krp = jnp.where(from_cache, ckrp_ref[...], krp_ref[...])
    kvt = _dot_nt(wukvt_ref[...], ckv.astype(BF16))
    krpt = krp.T
    gk = gk_ref[...]
    row = lax.broadcasted_iota(jnp.int32, (HEAD_PAD, TB), 0)
    is_nope = row < QK_NOPE_DIM
    ones_row = jnp.where(row == 0, 1.0, 0.0)
    for hd in range(N_HEADS):
        blk = kvt[hd * HEAD_PAD:(hd + 1) * HEAD_PAD, :]
        kraw = jnp.where(is_nope, blk, krpt)
        ss = jnp.sum(kraw * kraw, axis=0, keepdims=True) * (1.0 / QK_HEAD_DIM)
        kn = kraw * lax.rsqrt(ss + RMS_EPS) * gk
        k_ref[:, hd * HEAD_PAD:(hd + 1) * HEAD_PAD] = _rope_rows(kn, tabt_ref).T.astype(BF16)
        vt_ref[hd * HEAD_PAD:(hd + 1) * HEAD_PAD, :] = jnp.where(is_nope, ones_row, blk).astype(BF16)


def _kv_expand(ckv, krp, p, layer):
    j = layer // 2
    n_sb = DEC_BATCH * KV_LEN // TB
    n_cache = PAST_LEN // TB
    n_pb = N_PROMPT // TB
    lat_blocks = DEC_SEQ // TB

    def tab_idx(i):
        from_cache, _, jj = _kv_block_source(i)
        return (0, 0, jnp.where((i >= n_sb) | from_cache, ROPE_BLOCKS, jj - n_cache))

    def cache_idx(i):
        _, b, jj = _kv_block_source(i)
        return (b, j, jnp.minimum(jj, n_cache - 1), 0)

    def tok_idx(i):
        _, b, jj = _kv_block_source(i)
        latent = n_pb + b * lat_blocks + jnp.maximum(jj - n_cache, 0)
        return (jnp.where(i < n_sb, latent, i - n_sb), 0)

    return pl.pallas_call(
        _kv_expand_kernel,
        grid=(N_KV_ROWS // TB,),
        in_specs=[
            pl.BlockSpec((None, None, TB, KV_RANK), cache_idx),
            pl.BlockSpec((None, None, TB, HEAD_PAD), cache_idx),
            pl.BlockSpec((TB, KV_RANK), tok_idx),
            pl.BlockSpec((TB, HEAD_PAD), tok_idx),
            _layer_spec((QKV_W, KV_RANK), j),
            _layer_spec((HEAD_PAD, 1), j),
            pl.BlockSpec((3, HEAD_PAD, TB), tab_idx),
        ],
        out_specs=[
            pl.BlockSpec((TB, QKV_W), lambda i: (i, 0)),
            pl.BlockSpec((QKV_W, TB), lambda i: (0, i)),
        ],
        out_shape=[
            jax.ShapeDtypeStruct((N_KV_ROWS, QKV_W), BF16),
            jax.ShapeDtypeStruct((QKV_W, N_KV_ROWS), BF16),
        ],
        compiler_params=_cparams(("parallel",)),
        name="mla_kv_expand",
    )(p["cache_ckv"], p["cache_krp"], ckv, krp, p["m_w_ukv_t"], p["m_gk"], p["rope_t"])


def _attn_kernel(qt_ref, k_ref, vt_ref, o_ref, s_a, s_b, *, t_k, hps):
    c = (1.0 / math.sqrt(QK_HEAD_DIM)) * math.log2(math.e)
    kc = min(TKC, t_k)
    n_chunks = t_k // kc
    per_it = min(ATTN_CHUNKS_PER_ITER, n_chunks)
    n_it = n_chunks // per_it
    bufs = (s_a, s_b)

    def rows(h):
        return slice(h * HEAD_PAD, (h + 1) * HEAD_PAD)

    def scores(h, off, m8):
        st = _dot(k_ref[pl.ds(off, kc), rows(h)], qt_ref[rows(h), :])
        bufs[h % 2][pl.ds(off, kc), :] = st
        return jnp.maximum(m8, jnp.max(st.reshape(kc // SUBLANE, SUBLANE, TQ), axis=0))

    def weigh(h, off, m, acc):
        pt = jnp.exp2((bufs[h % 2][pl.ds(off, kc), :] - m) * c).astype(BF16)
        return acc + _dot(vt_ref[rows(h), pl.ds(off, kc)], pt)

    def phase(h_scores, h_weigh, m):
        def body(it, carry):
            m8, acc = carry
            for u in range(per_it):
                off = (it * per_it + u) * kc
                off = off if isinstance(off, int) else pl.multiple_of(off, kc)
                if h_scores is not None:
                    m8 = scores(h_scores, off, m8)
                if h_weigh is not None:
                    acc = weigh(h_weigh, off, m, acc)
            return m8, acc

        init = (jnp.full((SUBLANE, TQ), NEG_BIG, F32), jnp.zeros((HEAD_PAD, TQ), F32))
        return body(0, init) if n_it == 1 else lax.fori_loop(0, n_it, body, init)

    outs = []
    m8, _ = phase(0, None, None)
    for h in range(1, hps + 1):
        m = jnp.max(m8, axis=0, keepdims=True)
        m8, acc = phase(h if h < hps else None, h - 1, m)
        outs.append(acc[QK_NOPE_DIM:, :] / acc[0:1, :])
    o_ref[...] = jnp.concatenate(outs, axis=0).T.astype(BF16)


def _attention(qt, k, vt, *, n_batch, t_q, t_k, q_row0, kv_row0, hps):
    nq = t_q // TQ
    q0 = q_row0 // TQ
    k0 = kv_row0 // t_k
    return pl.pallas_call(
        functools.partial(_attn_kernel, t_k=t_k, hps=hps),
        grid=(n_batch, N_HEADS // hps, nq),
        in_specs=[
            pl.BlockSpec((hps * HEAD_PAD, TQ), lambda b, h, i: (h, q0 + b * nq + i)),
            pl.BlockSpec((t_k, hps * HEAD_PAD), lambda b, h, i: (k0 + b, h)),
            pl.BlockSpec((hps * HEAD_PAD, t_k), lambda b, h, i: (h, k0 + b)),
        ],
        out_specs=pl.BlockSpec((TQ, hps * V_HEAD_DIM), lambda b, h, i: (b * nq + i, h)),
        out_shape=jax.ShapeDtypeStruct((n_batch * t_q, N_HEADS * V_HEAD_DIM), BF16),
        scratch_shapes=[pltpu.VMEM((t_k, TQ), F32), pltpu.VMEM((t_k, TQ), F32)],
        compiler_params=_cparams(("parallel", "parallel", "parallel")),
        name=f"mla_attention_tk{t_k}",
    )(qt, k, vt)


def _out_proj_kernel(ap_ref, as_ref, xp_ref, xs_ref, mod_ref, wo_ref, o_ref):
    is_prompt = pl.program_id(0) < N_PROMPT // TB
    a = jnp.where(is_prompt, ap_ref[...], as_ref[...])
    x = jnp.where(is_prompt, xp_ref[...], xs_ref[...])
    o_ref[...] = x + mod_ref[0][2:3] * _dot(a, wo_ref[...])


def _out_proj(attn_p, attn_s, x_p, x_s, p, layer):
    return pl.pallas_call(
        _out_proj_kernel,
        grid=(N_TOK // TB,),
        in_specs=_split_rows_specs() + _split_rows_specs() + [
            _mod_spec(TB, layer),
            _layer_spec((D_MODEL, D_MODEL), layer // 2),
        ],
        out_specs=pl.BlockSpec((TB, D_MODEL), lambda i: (i, 0)),
        out_shape=jax.ShapeDtypeStruct((N_TOK, D_MODEL), F32),
        compiler_params=_cparams(("parallel",)),
        name="mla_out_proj",
    )(attn_p, attn_s, x_p, x_s, p["mod"], p["m_w_o"])


def _mla_layer(x_p, x_s, p, layer):
    qt, ckv, krp = _mla_proj(x_p, x_s, p, layer)
    k, vt = _kv_expand(ckv, krp, p, layer)
    a_p = _attention(qt, k, vt, n_batch=BATCH, t_q=SEQ, t_k=SEQ, q_row0=0, kv_row0=DEC_BATCH * KV_LEN,
                     hps=N_HEADS)
    a_s = _attention(qt, k, vt, n_batch=DEC_BATCH, t_q=DEC_SEQ, t_k=KV_LEN, q_row0=N_PROMPT, kv_row0=0,
                     hps=4)
    x = _out_proj(a_p, a_s, x_p, x_s, p, layer)
    state_ckv = ckv[:N_PROMPT].reshape(BATCH, SEQ, KV_RANK)
    state_krope = krp[:N_PROMPT, QK_NOPE_DIM:QK_HEAD_DIM].reshape(BATCH, SEQ, QK_ROPE_DIM)
    return x, state_ckv, state_krope


def _tile_plan(cnt8, plan_ref, start_ref):
    tm = float(TM)
    cnt_row = cnt8[0:1, :]
    cnt_col = cnt8.T[:, 0:1]
    tiles_row = jnp.floor((cnt_row + (tm - 1.0)) * (1.0 / tm))
    tiles_col = jnp.floor((cnt_col + (tm - 1.0)) * (1.0 / tm))
    sub = lax.broadcasted_iota(jnp.int32, (LANE, LANE), 0).astype(F32)
    lan = lax.broadcasted_iota(jnp.int32, (LANE, LANE), 1).astype(F32)
    tile_end_row = jnp.sum(jnp.where(sub <= lan, tiles_col, 0.0), axis=0, keepdims=True)
    tile_end_col = jnp.sum(jnp.where(lan <= sub, tiles_row, 0.0), axis=1, keepdims=True)
    n_used = jnp.max(tile_end_row, axis=1, keepdims=True)
    start_col = (tile_end_col - tiles_col) * tm
    end_col = start_col + cnt_col
    cand = jnp.where(jnp.logical_and(lan > sub, tiles_row > 0.0), lan, float(LANE))
    next_col = jnp.min(cand, axis=1, keepdims=True)
    next_col = jnp.where(next_col < float(LANE), next_col, -1.0)
    n_lanes = plan_ref.shape[1]
    tidx = jnp.minimum(lax.broadcasted_iota(jnp.int32, (LANE, n_lanes), 1).astype(F32), n_used - 1.0)
    esub = lax.broadcasted_iota(jnp.int32, (LANE, n_lanes), 0).astype(F32)
    te_row = jnp.sum(jnp.where(tile_end_col <= tidx, 1.0, 0.0), axis=0, keepdims=True)
    mine = esub == te_row
    end_at = jnp.sum(jnp.where(mine, end_col, 0.0), axis=0, keepdims=True)
    tv_row = jnp.clip(end_at - tidx[0:1, :] * tm, 0.0, tm)
    nx_row = jnp.sum(jnp.where(mine, next_col, 0.0), axis=0, keepdims=True)
    nu_row = jnp.broadcast_to(n_used, (1, n_lanes))
    plan_ref[...] = jnp.concatenate([te_row, tv_row, nx_row, nu_row, jnp.zeros((SUBLANE - 4, n_lanes), F32)], axis=0)
    start_ref[...] = jnp.broadcast_to(start_col, (LANE, LANE))


def _route_kernel(x_ref, mod_ref, g_ref, wr_ref, br_ref, tri_ref, h_ref, meta_ref, metat_ref, plan_ref, start_ref,
                  carry):
    i = pl.program_id(0)

    @pl.when(i == 0)
    def _():
        carry[...] = jnp.zeros_like(carry)

    x = x_ref[...]
    m = mod_ref[0]
    h = _rms(x, g_ref[...]) * (1.0 + m[4:5]) + m[3:4]
    h_ref[...] = _pack_bf16_pairs(h)
    logits = _dot(h.astype(BF16), wr_ref[...]) + br_ref[...]
    lane = lax.broadcasted_iota(jnp.int32, logits.shape, 1).astype(F32)
    work = logits
    sel = jnp.zeros(logits.shape, F32)
    hits, tops = [], []
    for k in range(TOP_K):
        mk = jnp.max(work, axis=-1, keepdims=True)
        first = jnp.min(jnp.where(work == mk, lane, float(LANE)), axis=-1, keepdims=True)
        hit = lane == first
        sel = jnp.where(hit, 1.0, sel)
        work = jnp.where(hit, -jnp.inf, work)
        hits.append((hit, first))
        tops.append(mk)
    es = [jnp.exp(t - tops[0]) for t in tops]
    denom = es[0] + es[1] + es[2] + es[3]
    pos = _dot(tri_ref[...], sel.astype(BF16)) + carry[0:1, :]
    carry[...] = carry[...] + jnp.sum(sel, axis=0, keepdims=True)

    @pl.when(i == pl.num_programs(0) - 1)
    def _():
        _tile_plan(carry[...], plan_ref, start_ref)

    meta = jnp.zeros(logits.shape, F32)
    for k in range(TOP_K):
        hit, first = hits[k]
        pk = jnp.sum(jnp.where(hit, pos, 0.0), axis=-1, keepdims=True)
        meta = jnp.where(lane == float(k), first, meta)
        meta = jnp.where(lane == float(TOP_K + k), es[k] / denom, meta)
        meta = jnp.where(lane == float(2 * TOP_K + k), pk, meta)
    meta_ref[...] = meta
    metat_ref[...] = meta.T[:2 * SUBLANE, :]


def _route(x, p, layer):
    tb = ROUTE_TB
    return pl.pallas_call(
        _route_kernel,
        grid=(N_TOK // tb,),
        in_specs=[
            pl.BlockSpec((tb, D_MODEL), lambda i: (i, 0)),
            _mod_spec(tb, layer),
            _layer_spec((1, D_MODEL), layer),
            _layer_spec((D_MODEL, LANE), layer),
            _layer_spec((1, LANE), layer),
            _const_spec((tb, tb)),
        ],
        out_specs=[
            pl.BlockSpec((tb, D_MODEL // 2), lambda i: (i, 0)),
            pl.BlockSpec((tb, LANE), lambda i: (i, 0)),
            pl.BlockSpec((2 * SUBLANE, tb), lambda i: (0, i)),
            _const_spec((SUBLANE, PLAN_LANES)),
            _const_spec((LANE, LANE)),
        ],
        out_shape=[
            jax.ShapeDtypeStruct((N_TOK, D_MODEL // 2), jnp.uint32),
            jax.ShapeDtypeStruct((N_TOK, LANE), F32),
            jax.ShapeDtypeStruct((2 * SUBLANE, N_TOK), F32),
            jax.ShapeDtypeStruct((SUBLANE, PLAN_LANES), F32),
            jax.ShapeDtypeStruct((LANE, LANE), F32),
        ],
        scratch_shapes=[pltpu.VMEM((SUBLANE, LANE), F32)],
        compiler_params=_cparams(("arbitrary",)),
        name="moe_route",
    )(x, p["mod"], p["norm_ffn_g"], p["e_w_router"], p["e_b_router"], p["tri"])


def _slots_kernel(start_ref, metat_ref, dest_ref):
    start_col = start_ref[:, 0:1]
    esub = lax.broadcasted_iota(jnp.int32, (LANE, SLOT_TB), 0).astype(F32)
    rows = []
    for k in range(TOP_K):
        e = metat_ref[k:k + 1, :]
        first = jnp.sum(jnp.where(esub == e, start_col, 0.0), axis=0, keepdims=True)
        rows.append(first + metat_ref[2 * TOP_K + k:2 * TOP_K + k + 1, :])
    dest_ref[...] = jnp.concatenate(rows, axis=0).astype(jnp.int32)


def _slots(start, meta_t):
    return pl.pallas_call(
        _slots_kernel,
        grid=(N_TOK // SLOT_TB,),
        in_specs=[
            _const_spec((LANE, LANE)),
            pl.BlockSpec((2 * SUBLANE, SLOT_TB), lambda i: (0, i)),
        ],
        out_specs=pl.BlockSpec((TOP_K, SLOT_TB), lambda i: (0, i)),
        out_shape=jax.ShapeDtypeStruct((TOP_K, N_TOK), jnp.int32),
        compiler_params=_cparams(("parallel",)),
        name="moe_slots",
    )(start, meta_t)


def _sc_gather(table, idx, ch):
    b, w = idx.shape[0], table.shape[1]
    per_w = b // SC_WORKERS
    n_ch = per_w // ch
    assert per_w * SC_WORKERS == b and n_ch * ch == per_w and n_ch % 2 == 0
    mesh = plsc.VectorSubcoreMesh(core_axis_name="c", subcore_axis_name="s")

    @functools.partial(
        pl.kernel, mesh=mesh,
        out_type=jax.ShapeDtypeStruct((b, w), table.dtype),
        scratch_types=[
            pltpu.VMEM((n_ch, ch), jnp.int32),
            pltpu.VMEM((ch, w), table.dtype),
            pltpu.VMEM((ch, w), table.dtype),
            pltpu.SemaphoreType.DMA, pltpu.SemaphoreType.DMA,
            pltpu.SemaphoreType.DMA, pltpu.SemaphoreType.DMA,
        ],
        name="sc_row_gather",
    )
    def gather_rows(table_hbm, idx_hbm, out_hbm, idx_v, buf0, buf1, g0, g1, s0, s1):
        wid = lax.axis_index("s") * SC_CORES + lax.axis_index("c")
        base = wid * per_w
        pltpu.sync_copy(idx_hbm.at[wid], idx_v)

        def gather(j, buf, sem):
            return pltpu.make_async_copy(table_hbm.at[idx_v.at[j]], buf, sem)

        def store(j, buf, sem):
            return pltpu.make_async_copy(buf, out_hbm.at[pl.ds(base + j * ch, ch)], sem)

        gather(0, buf0, g0).start()

        @pl.loop(0, n_ch, step=2)
        def _(j):
            @pl.when(j > 0)
            def _():
                store(j - 1, buf1, s1).wait()

            gather(j + 1, buf1, g1).start()
            gather(j, buf0, g0).wait()
            store(j, buf0, s0).start()
            gather(j + 1, buf1, g1).wait()
            store(j + 1, buf1, s1).start()
            store(j, buf0, s0).wait()

            @pl.when(j + 2 < n_ch)
            def _():
                gather(j + 2, buf0, g0).start()

        store(n_ch - 1, buf1, s1).wait()

    return gather_rows(table, idx.reshape(SC_WORKERS, n_ch, ch))


def _sc_dispatch(rows, dest_t, ch):
    n, w = rows.shape
    per_w = n // SC_WORKERS
    n_ch = per_w // ch
    assert per_w * SC_WORKERS == n and n_ch * ch == per_w and n_ch % 2 == 0
    mesh = plsc.VectorSubcoreMesh(core_axis_name="c", subcore_axis_name="s")
    idx = dest_t.reshape(TOP_K, SC_WORKERS, n_ch, ch)

    @functools.partial(
        pl.kernel, mesh=mesh,
        out_type=jax.ShapeDtypeStruct((N_SLOTS, w), rows.dtype),
        scratch_types=[
            pltpu.VMEM((TOP_K * n_ch, ch), jnp.int32),
            pltpu.VMEM((ch, w), rows.dtype),
            pltpu.VMEM((ch, w), rows.dtype),
            pltpu.SemaphoreType.DMA, pltpu.SemaphoreType.DMA,
            pltpu.SemaphoreType.DMA, pltpu.SemaphoreType.DMA,
        ],
        name="sc_row_dispatch",
    )
    def dispatch_rows(rows_hbm, idx_hbm, out_hbm, idx_v, buf0, buf1, l0, l1, s0, s1):
        wid = lax.axis_index("s") * SC_CORES + lax.axis_index("c")
        base = wid * per_w
        for k in range(TOP_K):
            pltpu.sync_copy(idx_hbm.at[k, wid], idx_v.at[pl.ds(k * n_ch, n_ch)])

        def load(j, buf, sem):
            return pltpu.make_async_copy(rows_hbm.at[pl.ds(base + j * ch, ch)], buf, sem)

        def scatter(j, k, buf, sem):
            return pltpu.make_async_copy(buf, out_hbm.at[idx_v.at[k * n_ch + j]], sem)

        load(0, buf0, l0).start()

        @pl.loop(0, n_ch, step=2)
        def _(j):
            load(j + 1, buf1, l1).start()
            load(j, buf0, l0).wait()
            for k in range(TOP_K):
                scatter(j, k, buf0, s0).start()
            load(j + 1, buf1, l1).wait()
            for k in range(TOP_K):
                scatter(j + 1, k, buf1, s1).start()
            for k in range(TOP_K):
                scatter(j, k, buf0, s0).wait()

            @pl.when(j + 2 < n_ch)
            def _():
                load(j + 2, buf0, l0).start()

            for k in range(TOP_K):
                scatter(j + 1, k, buf1, s1).wait()

    return dispatch_rows(rows, idx)


def _deinterleave_matrix():
    src = jnp.arange(2 * LANE)[:, None]
    dst = jnp.arange(2 * LANE)[None, :]
    want = jnp.where(dst < LANE, 2 * dst, 2 * (dst - LANE) + 1)
    return (src == want).astype(BF16)


def _expert_kernel(te_ref, nu_ref, tv_ref, nx_ref, x_ref, wgu_hbm, bgu_ref, wd_hbm, bd_ref, perm_ref, o_ref,
                   wgu_st, wd_st, wgu_bf, wd_bf, sems, *, layer):
    i = pl.program_id(0)
    prev = te_ref[jnp.maximum(i - 1, 0)]
    fresh = jnp.logical_or(i == 0, te_ref[i] != prev)

    def fetch(e):
        return (pltpu.make_async_copy(wgu_hbm.at[layer, e], wgu_st, sems.at[0]),
                pltpu.make_async_copy(wd_hbm.at[layer, e], wd_st, sems.at[1]))

    @pl.when(i == 0)
    def _():
        for cp in fetch(te_ref[0]):
            cp.start()

    @pl.when(jnp.logical_and(fresh, i < nu_ref[0]))
    def _():
        for cp in fetch(te_ref[i]):
            cp.wait()
        for b in range(2 * D_FF // (2 * LANE)):
            sl = slice(b * 2 * LANE, (b + 1) * 2 * LANE)
            wgu_bf[:, sl] = _dot(wgu_st[:, sl].astype(BF16), perm_ref[...]).astype(BF16)
        wd_bf[...] = wd_st[...].astype(BF16)

        @pl.when(nx_ref[i] >= 0)
        def _():
            for cp in fetch(nx_ref[i]):
                cp.start()

    @pl.when(i < nu_ref[0])
    def _():
        row = lax.broadcasted_iota(jnp.int32, (TM, D_MODEL // 2), 0)
        w = jnp.where(row < tv_ref[i], x_ref[...], jnp.uint32(0))
        x = _unpack_bf16_pairs(w).astype(BF16)
        gu = _dot(x, wgu_bf[...]) + bgu_ref[...]
        acts = []
        for b in range(D_FF // LANE):
            glu = jnp.minimum(gu[:, b * 2 * LANE:b * 2 * LANE + LANE], SWIGLU_LIMIT)
            lin = jnp.clip(gu[:, b * 2 * LANE + LANE:(b + 1) * 2 * LANE], -SWIGLU_LIMIT, SWIGLU_LIMIT)
            acts.append((glu * jax.nn.sigmoid(SWIGLU_ALPHA * glu) * (lin + 1.0)).astype(BF16))
        act = jnp.concatenate(acts, axis=1)
        o_ref[...] = _pack_bf16_pairs(_dot(act, wd_bf[...]) + bd_ref[...])


def _experts(buf, tile_expert, n_used, tile_valid, tile_next, p, layer):
    def row_idx(i, te, nu, tv, nx):
        return (jnp.minimum(i, nu[0] - 1), 0)

    def b_idx(i, te, nu, tv, nx):
        return (layer, te[i], 0, 0)

    grid_spec = pltpu.PrefetchScalarGridSpec(
        num_scalar_prefetch=4,
        grid=(N_TILES,),
        in_specs=[
            pl.BlockSpec((TM, D_MODEL // 2), row_idx),
            pl.BlockSpec(memory_space=pl.ANY),
            pl.BlockSpec((None, None, 1, 2 * D_FF), b_idx),
            pl.BlockSpec(memory_space=pl.ANY),
            pl.BlockSpec((None, None, 1, D_MODEL), b_idx),
            _const_spec((2 * LANE, 2 * LANE)),
        ],
        out_specs=pl.BlockSpec((TM, D_MODEL // 2), row_idx),
        scratch_shapes=[
            pltpu.VMEM((D_MODEL, 2 * D_FF), F32),
            pltpu.VMEM((D_FF, D_MODEL), F32),
            pltpu.VMEM((D_MODEL, 2 * D_FF), BF16),
            pltpu.VMEM((D_FF, D_MODEL), BF16),
            pltpu.SemaphoreType.DMA((2,)),
        ],
    )
    return pl.pallas_call(
        functools.partial(_expert_kernel, layer=layer),
        grid_spec=grid_spec,
        out_shape=jax.ShapeDtypeStruct((N_SLOTS, D_MODEL // 2), jnp.uint32),
        compiler_params=_cparams(("arbitrary",)),
        name="moe_experts",
    )(tile_expert, n_used, tile_valid, tile_next, buf, p["e_w_gu"], p["e_b_gu"], p["e_w_down"], p["e_b_down"],
      p["deinterleave"])


def _pack_bf16_pairs(v):
    half = v.shape[1] // 2
    bits = pltpu.bitcast(v.astype(BF16).astype(F32), jnp.uint32)
    return (bits[:, half:] & jnp.uint32(0xFFFF0000)) | (bits[:, :half] >> 16)


def _unpack_bf16_pairs(w):
    return jnp.concatenate([pltpu.bitcast(w << 16, F32), pltpu.bitcast(w & jnp.uint32(0xFFFF0000), F32)],
                           axis=1)


def _combine_kernel(x_ref, mod_ref, y_ref, w_ref, o_ref):
    w = w_ref[:, TOP_K:2 * TOP_K]
    y = _unpack_bf16_pairs(y_ref[0]) * w[:, 0:1]
    for k in range(1, TOP_K):
        y = y + _unpack_bf16_pairs(y_ref[k]) * w[:, k:k + 1]
    o_ref[...] = x_ref[...] + mod_ref[0][5:6] * y


def _combine(x, y4, meta, p, layer, part):
    n_rows = N_TOK // MOE_PARTS
    first = part * n_rows // TB
    return pl.pallas_call(
        _combine_kernel,
        grid=(n_rows // TB,),
        in_specs=[
            pl.BlockSpec((TB, D_MODEL), lambda i: (i + first, 0)),
            _mod_spec(TB, layer, first),
            pl.BlockSpec((TOP_K, TB, D_MODEL // 2), lambda i: (0, i, 0)),
            pl.BlockSpec((TB, LANE), lambda i: (i + first, 0)),
        ],
        out_specs=pl.BlockSpec((TB, D_MODEL), lambda i: (i, 0)),
        out_shape=jax.ShapeDtypeStruct((n_rows, D_MODEL), F32),
        compiler_params=_cparams(("parallel",)),
        name="moe_combine",
    )(x, p["mod"], y4, meta)


def _moe_layer(x, p, layer):
    hp, meta, meta_t, plan, start = _route(x, p, layer)
    plan = plan[:4, :N_TILES].astype(jnp.int32)
    tile_expert, tile_valid, tile_next, n_used = plan[0], plan[1], plan[2], plan[3, :1]
    dest_t = _slots(start, meta_t)
    buf = _sc_dispatch(hp, dest_t, 64)
    yb = _experts(buf, tile_expert, n_used, tile_valid, tile_next, p, layer)
    n_rows = N_TOK // MOE_PARTS
    outs = []
    for part in range(MOE_PARTS):
        idx = dest_t[:, part * n_rows:(part + 1) * n_rows].reshape(-1)
        y4 = _sc_gather(yb, idx, 64).reshape(TOP_K, n_rows, D_MODEL // 2)
        outs.append(_combine(x, y4, meta, p, layer, part))
    return outs


def _prepare(c, cache_ckv, cache_krope, c_ctx, norm_mix_g, norm_ffn_g, w_mod, b_mod, g_w_in, g_b_in, g_norm_v,
             g_w_s, g_b_s, g_w_out, m_w_dq, m_norm_q, m_w_uq, m_w_dkv, m_norm_kv, m_w_ukv, m_qk_norm_q,
             m_qk_norm_k, m_w_o, e_w_router, e_b_router, e_w_gu, e_b_gu, e_w_down, e_b_down):
    n_mla = m_w_dq.shape[0]
    cond = jnp.concatenate([c_ctx[None, :], c, jnp.zeros((SUBLANE - N_COND, D_MODEL), F32)], axis=0)
    wdkv = jnp.concatenate([m_w_dkv[..., :KV_RANK], jnp.zeros((n_mla, D_MODEL, QK_NOPE_DIM), F32),
                            m_w_dkv[..., KV_RANK:], jnp.zeros((n_mla, D_MODEL, HEAD_PAD - QK_HEAD_DIM), F32)],
                           axis=-1)
    w_uq = jnp.pad(m_w_uq.reshape(n_mla, Q_RANK, N_HEADS, QK_HEAD_DIM),
                   ((0, 0), (0, 0), (0, 0), (0, HEAD_PAD - QK_HEAD_DIM))).reshape(n_mla, Q_RANK, QKV_W)

    def gain_col(g):
        return jnp.pad(g, ((0, 0), (0, HEAD_PAD - QK_HEAD_DIM)))[:, :, None]

    return {
        "mod": _modulation(cond, w_mod, b_mod),
        "rope_t": _rope_tables(),
        "norm_mix_g": norm_mix_g[:, None, :],
        "norm_ffn_g": norm_ffn_g[:, None, :],
        "g_w_in": g_w_in.astype(BF16),
        "g_b_in": g_b_in[:, None, :],
        "g_norm_v": g_norm_v[:, None, :],
        "g_w_s": g_w_s.astype(BF16),
        "g_b_st": jnp.swapaxes(g_b_s, 1, 2),
        "g_w_out": g_w_out.astype(BF16),
        "m_w_dq": m_w_dq.astype(BF16),
        "m_norm_q": m_norm_q[:, None, :],
        "m_w_uq_t": jnp.swapaxes(w_uq, 1, 2).astype(BF16),
        "m_gq": gain_col(m_qk_norm_q),
        "m_w_dkv": wdkv.astype(BF16),
        "m_norm_kv": m_norm_kv[:, None, :],
        "m_w_ukv_t": jnp.swapaxes(m_w_ukv, 1, 2).astype(BF16),
        "m_gk": gain_col(m_qk_norm_k),
        "m_w_o": m_w_o.astype(BF16),
        "cache_ckv": cache_ckv,
        "cache_krp": jnp.pad(cache_krope, ((0, 0), (0, 0), (0, 0), (QK_NOPE_DIM, HEAD_PAD - QK_HEAD_DIM))),
        "e_w_router": jnp.pad(e_w_router, ((0, 0), (0, 0), (0, LANE - N_EXPERTS))).astype(BF16),
        "e_b_router": jnp.pad(e_b_router, ((0, 0), (0, LANE - N_EXPERTS)), constant_values=NEG_BIG)[:, None, :],
        "tri": jnp.tri(ROUTE_TB, ROUTE_TB, -1, dtype=BF16),
        "e_w_gu": e_w_gu,
        "e_b_gu": e_b_gu.reshape(DEPTH, N_EXPERTS, D_FF // LANE, LANE, 2).swapaxes(3, 4).reshape(
            DEPTH, N_EXPERTS, 1, 2 * D_FF),
        "e_w_down": e_w_down,
        "e_b_down": e_b_down[:, :, None, :],
        "deinterleave": _deinterleave_matrix(),
    }


def kernel(x_prompt, x_sample, c, cache_ckv, cache_krope, c_ctx, norm_mix_g, norm_ffn_g, w_mod, b_mod,
           g_w_in, g_b_in, g_norm_v, g_w_s, g_b_s, g_w_out, m_w_dq, m_norm_q, m_w_uq, m_w_dkv,
           m_norm_kv, m_w_ukv, m_qk_norm_q, m_qk_norm_k, m_w_o, e_w_router, e_b_router, e_w_gu,
           e_b_gu, e_w_down, e_b_down):
    p = _prepare(c, cache_ckv, cache_krope, c_ctx, norm_mix_g, norm_ffn_g, w_mod, b_mod, g_w_in, g_b_in,
                 g_norm_v, g_w_s, g_b_s, g_w_out, m_w_dq, m_norm_q, m_w_uq, m_w_dkv, m_norm_kv, m_w_ukv,
                 m_qk_norm_q, m_qk_norm_k, m_w_o, e_w_router, e_b_router, e_w_gu, e_b_gu, e_w_down, e_b_down)
    assert MOE_PARTS == 2 and N_PROMPT == N_SAMPLE
    x_p, x_s = x_prompt.reshape(N_PROMPT, D_MODEL), x_sample.reshape(N_SAMPLE, D_MODEL)
    ckv_states, krope_states = [], []
    for layer in range(DEPTH):
        if layer % 2 == 0:
            x = _gmlp_layer(x_p, x_s, p, layer)
        else:
            x, s_ckv, s_krope = _mla_layer(x_p, x_s, p, layer)
            ckv_states.append(s_ckv)
            krope_states.append(s_krope)
        x_p, x_s = _moe_layer(x, p, layer)
    y_prompt = x_p.reshape(BATCH, SEQ, D_MODEL)
    y_sample = x_s.reshape(DEC_BATCH, DEC_SEQ, D_MODEL)
    return (y_prompt, y_sample, jnp.stack(ckv_states, axis=1), jnp.stack(krope_states, axis=1))
```

```python
import functools
import math

import jax
import jax.numpy as jnp
from jax import lax
from jax.experimental import pallas as pl
from jax.experimental.pallas import tpu as pltpu
from jax.experimental.pallas import tpu_sc as plsc

F32 = jnp.float32
BF16 = jnp.bfloat16

D_MODEL = 1024
BATCH = 32
SEQ = 256
DEPTH = 4
DEC_BATCH = 2
DEC_SEQ = 4096
PAST_LEN = 512
GRID_W = 64
RMS_EPS = 1e-6
GMLP_WIDTH = 2 * D_MODEL
GMLP_GROUPS = 8
GROUP_W = GMLP_WIDTH // GMLP_GROUPS
CHUNK = 128
N_HEADS = 16
QK_NOPE_DIM = 64
QK_ROPE_DIM = 32
QK_HEAD_DIM = QK_NOPE_DIM + QK_ROPE_DIM
V_HEAD_DIM = 64
Q_RANK = 256
KV_RANK = 128
ROPE_THETA = 10000.0
N_EXPERTS = 32
TOP_K = 4
D_FF = D_MODEL
SWIGLU_LIMIT = 7.0
SWIGLU_ALPHA = 1.702

N_PROMPT = BATCH * SEQ
N_SAMPLE = DEC_BATCH * DEC_SEQ
N_TOK = N_PROMPT + N_SAMPLE
N_COND = 1 + DEC_BATCH
KV_LEN = PAST_LEN + DEC_SEQ
N_KV_ROWS = DEC_BATCH * KV_LEN + N_PROMPT

LANE = 128
SUBLANE = 8
HEAD_PAD = LANE
QKV_W = N_HEADS * HEAD_PAD
VMEM_LIMIT = 56 * 1024 * 1024

TB = 256
WIDE_TB = 512
TQ = 256
TKC = 256
ATTN_CHUNKS_PER_ITER = 9
TM = 512
N_TILES = N_TOK * TOP_K // TM + N_EXPERTS
N_SLOTS = N_TILES * TM
PLAN_LANES = -(-N_TILES // LANE) * LANE
SLOT_TB = 2048
ROUTE_TB = 512
MOE_PARTS = 2
SC_CORES = 2
SC_WORKERS = SC_CORES * 16
ROPE_BLOCKS = DEC_SEQ // TB
NEG_BIG = -1e30


def _cparams(sem):
    return pltpu.CompilerParams(dimension_semantics=sem, vmem_limit_bytes=VMEM_LIMIT)


def _cond_of_block(i, tb):
    n_p = N_PROMPT // tb
    per = DEC_SEQ // tb
    return jnp.where(i < n_p, 0, 1 + (i - n_p) // per)


def _rms(x, g, n=None):
    n = x.shape[-1] if n is None else n
    ss = jnp.sum(x * x, axis=-1, keepdims=True) * (1.0 / n)
    return x * lax.rsqrt(ss + RMS_EPS) * g


def _dot(a, b):
    return jnp.dot(a, b, preferred_element_type=F32)


def _mod_kernel(c_ref, w_ref, b_ref, o_ref):
    c = c_ref[...]
    s = c * jax.nn.sigmoid(c)
    o_ref[0] = _dot(s.astype(BF16), w_ref[0].astype(BF16)) + b_ref[0]


def _modulation(cond, w_mod, b_mod):
    tn = 1536
    out = pl.pallas_call(
        _mod_kernel,
        grid=(DEPTH, 6 * D_MODEL // tn),
        in_specs=[
            pl.BlockSpec((SUBLANE, D_MODEL), lambda l, j: (0, 0)),
            pl.BlockSpec((1, D_MODEL, tn), lambda l, j: (l, 0, j)),
            pl.BlockSpec((1, 1, tn), lambda l, j: (l, 0, j)),
        ],
        out_specs=pl.BlockSpec((1, SUBLANE, tn), lambda l, j: (l, 0, j)),
        out_shape=jax.ShapeDtypeStruct((DEPTH, SUBLANE, 6 * D_MODEL), F32),
        compiler_params=_cparams(("parallel", "parallel")),
        name="adaln_mod",
    )(cond, w_mod, b_mod.reshape(DEPTH, 1, 6 * D_MODEL))
    m = out[:, :N_COND].reshape(DEPTH, N_COND, 6, D_MODEL)
    return jnp.pad(m, ((0, 0), (0, 0), (0, SUBLANE - 6), (0, 0)))


def _mod_spec(tb, layer, first_block=0):
    return pl.BlockSpec((None, 1, SUBLANE, D_MODEL),
                        lambda i: (layer, _cond_of_block(i + first_block, tb), 0, 0))


def _const_spec(shape):
    nd = len(shape)
    return pl.BlockSpec(shape, lambda *_: (0,) * nd)


def _layer_spec(shape, j):
    nd = len(shape)
    return pl.BlockSpec((None,) + tuple(shape), lambda *_: (j,) + (0,) * nd)


def _gmlp_kernel(xp_ref, xs_ref, mod_ref, g_ref, win_ref, bin_ref, gv_ref, ws_ref, bst_ref, wout_ref, o_ref):
    x = jnp.where(pl.program_id(0) < N_PROMPT // TB, xp_ref[...], xs_ref[...])
    m = mod_ref[0]
    h = _rms(x, g_ref[...]) * (1.0 + m[1:2]) + m[0:1]
    hb = h.astype(BF16)
    zv = jax.nn.gelu(_dot(hb, win_ref[:, GMLP_WIDTH:]) + bin_ref[:, GMLP_WIDTH:], approximate=True)
    vn = _rms(zv, gv_ref[...]).astype(BF16)
    rows = []
    for c in range(TB // CHUNK):
        cols = []
        for g in range(GMLP_GROUPS):
            blk = vn[c * CHUNK:(c + 1) * CHUNK, g * GROUP_W:(g + 1) * GROUP_W]
            cols.append(_dot(ws_ref[g], blk) + bst_ref[:, g:g + 1])
        rows.append(jnp.concatenate(cols, axis=1))
    vm = jnp.concatenate(rows, axis=0)
    u = jax.nn.gelu(_dot(hb, win_ref[:, :GMLP_WIDTH]) + bin_ref[:, :GMLP_WIDTH], approximate=True)
    d = _dot((u * vm).astype(BF16), wout_ref[...])
    o_ref[...] = x + m[2:3] * d


def _split_rows_specs(tb=TB):
    n_pb = N_PROMPT // tb
    return [pl.BlockSpec((tb, D_MODEL), lambda i: (jnp.minimum(i, n_pb - 1), 0)),
            pl.BlockSpec((tb, D_MODEL), lambda i: (jnp.maximum(i - n_pb, 0), 0))]


def _gmlp_layer(x_p, x_s, p, layer):
    j = layer // 2
    return pl.pallas_call(
        _gmlp_kernel,
        grid=(N_TOK // TB,),
        in_specs=_split_rows_specs() + [
            _mod_spec(TB, layer),
            _layer_spec((1, D_MODEL), layer),
            _layer_spec((D_MODEL, 2 * GMLP_WIDTH), j),
            _layer_spec((1, 2 * GMLP_WIDTH), j),
            _layer_spec((1, GMLP_WIDTH), j),
            _layer_spec((GMLP_GROUPS, CHUNK, CHUNK), j),
            _layer_spec((CHUNK, GMLP_GROUPS), j),
            _layer_spec((GMLP_WIDTH, D_MODEL), j),
        ],
        out_specs=pl.BlockSpec((TB, D_MODEL), lambda i: (i, 0)),
        out_shape=jax.ShapeDtypeStruct((N_TOK, D_MODEL), F32),
        compiler_params=_cparams(("parallel",)),
        name="gmlp_mixer",
    )(x_p, x_s, p["mod"], p["norm_mix_g"], p["g_w_in"], p["g_b_in"], p["g_norm_v"], p["g_w_s"], p["g_b_st"],
      p["g_w_out"])


def _rope_tables():
    t = jnp.arange(DEC_SEQ)
    row_id = (t // GRID_W).astype(F32)
    col_id = (t % GRID_W).astype(F32)
    axis_dim = QK_ROPE_DIM // 2
    inv_freq = ROPE_THETA ** (-jnp.arange(0, axis_dim, 2, dtype=F32) / axis_dim)
    ang = jnp.stack([row_id[:, None] * inv_freq, col_id[:, None] * inv_freq], axis=1)
    cos, sin = jnp.cos(ang), jnp.sin(ang)
    zeros = jnp.zeros_like(sin)
    cos_l = jnp.concatenate([cos, cos], axis=-1).reshape(DEC_SEQ, QK_ROPE_DIM)
    s1_l = jnp.concatenate([-sin, zeros], axis=-1).reshape(DEC_SEQ, QK_ROPE_DIM)
    s2_l = jnp.concatenate([zeros, sin], axis=-1).reshape(DEC_SEQ, QK_ROPE_DIM)

    def widen(rope_part, nope_fill):
        left = jnp.full((DEC_SEQ, QK_NOPE_DIM), nope_fill, F32)
        right = jnp.zeros((DEC_SEQ, HEAD_PAD - QK_HEAD_DIM), F32)
        return jnp.concatenate([left, rope_part, right], axis=-1)

    pos = jnp.stack([widen(cos_l, 1.0), widen(s1_l, 0.0), widen(s2_l, 0.0)])
    ident_c = jnp.concatenate([jnp.ones((TB, QK_HEAD_DIM), F32),
                               jnp.zeros((TB, HEAD_PAD - QK_HEAD_DIM), F32)], axis=-1)
    ident = jnp.stack([ident_c, jnp.zeros_like(ident_c), jnp.zeros_like(ident_c)])
    return jnp.swapaxes(jnp.concatenate([pos, ident], axis=1), 1, 2)


def _dot_nt(a, b):
    return lax.dot_general(a, b, (((1,), (1,)), ((), ())), preferred_element_type=F32)


def _shift_rows(x, n):
    n = n % x.shape[0]
    return jnp.concatenate([x[n:], x[:n]], axis=0)


def _rope_rows(xn, tabt_ref):
    half = QK_ROPE_DIM // 4
    return xn * tabt_ref[0] + _shift_rows(xn, half) * tabt_ref[1] + _shift_rows(xn, -half) * tabt_ref[2]


def _mla_proj_kernel(xp_ref, xs_ref, mod_ref, g_ref, wdq_ref, nq_ref, wuqt_ref, gq_ref, wdkv_ref, nkv_ref,
                     tabt_ref, qt_ref, ckv_ref, krp_ref):
    x = jnp.where(pl.program_id(0) < N_PROMPT // TB, xp_ref[...], xs_ref[...])
    m = mod_ref[0]
    h = _rms(x, g_ref[...]) * (1.0 + m[1:2]) + m[0:1]
    hb = h.astype(BF16)
    cq = _rms(_dot(hb, wdq_ref[...]), nq_ref[...])
    qt = _dot_nt(wuqt_ref[...], cq.astype(BF16))
    gq = gq_ref[...]
    for hd in range(N_HEADS):
        qh = qt[hd * HEAD_PAD:(hd + 1) * HEAD_PAD, :]
        ss = jnp.sum(qh * qh, axis=0, keepdims=True) * (1.0 / QK_HEAD_DIM)
        qn = qh * lax.rsqrt(ss + RMS_EPS) * gq
        qt_ref[hd * HEAD_PAD:(hd + 1) * HEAD_PAD, :] = _rope_rows(qn, tabt_ref).astype(BF16)
    kva = _dot(hb, wdkv_ref[...])
    ckv_ref[...] = _rms(kva[:, :KV_RANK], nkv_ref[...])
    krp_ref[...] = kva[:, KV_RANK:]


def _mla_proj(x_p, x_s, p, layer):
    j = layer // 2
    n_pb = N_PROMPT // TB

    def tab_idx(i):
        return (0, 0, jnp.where(i < n_pb, ROPE_BLOCKS, (i - n_pb) % ROPE_BLOCKS))

    return pl.pallas_call(
        _mla_proj_kernel,
        grid=(N_TOK // TB,),
        in_specs=_split_rows_specs() + [
            _mod_spec(TB, layer),
            _layer_spec((1, D_MODEL), layer),
            _layer_spec((D_MODEL, Q_RANK), j),
            _layer_spec((1, Q_RANK), j),
            _layer_spec((QKV_W, Q_RANK), j),
            _layer_spec((HEAD_PAD, 1), j),
            _layer_spec((D_MODEL, 2 * LANE), j),
            _layer_spec((1, KV_RANK), j),
            pl.BlockSpec((3, HEAD_PAD, TB), tab_idx),
        ],
        out_specs=[
            pl.BlockSpec((QKV_W, TB), lambda i: (0, i)),
            pl.BlockSpec((TB, KV_RANK), lambda i: (i, 0)),
            pl.BlockSpec((TB, HEAD_PAD), lambda i: (i, 0)),
        ],
        out_shape=[
            jax.ShapeDtypeStruct((QKV_W, N_TOK), BF16),
            jax.ShapeDtypeStruct((N_TOK, KV_RANK), F32),
            jax.ShapeDtypeStruct((N_TOK, HEAD_PAD), F32),
        ],
        compiler_params=_cparams(("parallel",)),
        name="mla_proj",
    )(x_p, x_s, p["mod"], p["norm_mix_g"], p["m_w_dq"], p["m_norm_q"], p["m_w_uq_t"], p["m_gq"], p["m_w_dkv"],
      p["m_norm_kv"], p["rope_t"])


def _kv_block_source(i):
    n_sb = DEC_BATCH * KV_LEN // TB
    per = KV_LEN // TB
    b = jnp.minimum(i // per, DEC_BATCH - 1)
    jj = i % per
    return jnp.logical_and(i < n_sb, jj < PAST_LEN // TB), b, jj


def _kv_expand_kernel(cckv_ref, ckrp_ref, ckv_ref, krp_ref, wukvt_ref, gk_ref, tabt_ref, k_ref, vt_ref):
    from_cache, _, _ = _kv_block_source(pl.program_id(0))
    ckv = jnp.where(from_cache, cckv_ref[...], ckv_ref[...])
    krp = jnp.where(from_cache, ckrp_ref[...], krp_ref[...])
    kvt = _dot_nt(wukvt_ref[...], ckv.astype(BF16))
    krpt = krp.T
    gk = gk_ref[...]
    row = lax.broadcasted_iota(jnp.int32, (HEAD_PAD, TB), 0)
    is_nope = row < QK_NOPE_DIM
    ones_row = jnp.where(row == 0, 1.0, 0.0)
    for hd in range(N_HEADS):
        blk = kvt[hd * HEAD_PAD:(hd + 1) * HEAD_PAD, :]
        kraw = jnp.where(is_nope, blk, krpt)
        ss = jnp.sum(kraw * kraw, axis=0, keepdims=True) * (1.0 / QK_HEAD_DIM)
        kn = kraw * lax.rsqrt(ss + RMS_EPS) * gk
        k_ref[:, hd * HEAD_PAD:(hd + 1) * HEAD_PAD] = _rope_rows(kn, tabt_ref).T.astype(BF16)
        vt_ref[hd * HEAD_PAD:(hd + 1) * HEAD_PAD, :] = jnp.where(is_nope, ones_row, blk).astype(BF16)


def _kv_expand(ckv, krp, p, layer):
    j = layer // 2
    n_sb = DEC_BATCH * KV_LEN // TB
    n_cache = PAST_LEN // TB
    n_pb = N_PROMPT // TB
    lat_blocks = DEC_SEQ // TB

    def tab_idx(i):
        from_cache, _, jj = _kv_block_source(i)
        return (0, 0, jnp.where((i >= n_sb) | from_cache, ROPE_BLOCKS, jj - n_cache))

    def cache_idx(i):
        _, b, jj = _kv_block_source(i)
        return (b, j, jnp.minimum(jj, n_cache - 1), 0)

    def tok_idx(i):
        _, b, jj = _kv_block_source(i)
        latent = n_pb + b * lat_blocks + jnp.maximum(jj - n_cache, 0)
        return (jnp.where(i < n_sb, latent, i - n_sb), 0)

    return pl.pallas_call(
        _kv_expand_kernel,
        grid=(N_KV_ROWS // TB,),
        in_specs=[
            pl.BlockSpec((None, None, TB, KV_RANK), cache_idx),
            pl.BlockSpec((None, None, TB, HEAD_PAD), cache_idx),
            pl.BlockSpec((TB, KV_RANK), tok_idx),
            pl.BlockSpec((TB, HEAD_PAD), tok_idx),
            _layer_spec((QKV_W, KV_RANK), j),
            _layer_spec((HEAD_PAD, 1), j),
            pl.BlockSpec((3, HEAD_PAD, TB), tab_idx),
        ],
        out_specs=[
            pl.BlockSpec((TB, QKV_W), lambda i: (i, 0)),
            pl.BlockSpec((QKV_W, TB), lambda i: (0, i)),
        ],
        out_shape=[
            jax.ShapeDtypeStruct((N_KV_ROWS, QKV_W), BF16),
            jax.ShapeDtypeStruct((QKV_W, N_KV_ROWS), BF16),
        ],
        compiler_params=_cparams(("parallel",)),
        name="mla_kv_expand",
    )(p["cache_ckv"], p["cache_krp"], ckv, krp, p["m_w_ukv_t"], p["m_gk"], p["rope_t"])


def _attn_kernel(qt_ref, k_ref, vt_ref, o_ref, s_a, s_b, *, t_k, hps, qps):
    c = (1.0 / math.sqrt(QK_HEAD_DIM)) * math.log2(math.e)
    kc = min(TKC, t_k)
    n_chunks = t_k // kc
    per_it = min(ATTN_CHUNKS_PER_ITER, n_chunks)
    n_it = n_chunks // per_it
    bufs = (s_a, s_b)

    units = [(qb, h) for qb in range(qps) for h in range(hps)]

    def rows(h):
        return slice(h * HEAD_PAD, (h + 1) * HEAD_PAD)

    def cols(qb):
        return slice(qb * TQ, (qb + 1) * TQ)

    def scores(u, off, m8):
        qb, h = units[u]
        st = _dot(k_ref[pl.ds(off, kc), rows(h)], qt_ref[rows(h), cols(qb)])
        bufs[u % 2][pl.ds(off, kc), :] = st
        return jnp.maximum(m8, jnp.max(st.reshape(kc // SUBLANE, SUBLANE, TQ), axis=0))

    def weigh(u, off, m, acc):
        pt = jnp.exp2((bufs[u % 2][pl.ds(off, kc), :] - m) * c).astype(BF16)
        return acc + _dot(vt_ref[rows(units[u][1]), pl.ds(off, kc)], pt)

    def phase(u_scores, u_weigh, m):
        def body(it, carry):
            m8, acc = carry
            for j in range(per_it):
                off = (it * per_it + j) * kc
                off = off if isinstance(off, int) else pl.multiple_of(off, kc)
                if u_scores is not None:
                    m8 = scores(u_scores, off, m8)
                if u_weigh is not None:
                    acc = weigh(u_weigh, off, m, acc)
            return m8, acc

        init = (jnp.full((SUBLANE, TQ), NEG_BIG, F32), jnp.zeros((HEAD_PAD, TQ), F32))
        return body(0, init) if n_it == 1 else lax.fori_loop(0, n_it, body, init)

    outs = []
    m8, _ = phase(0, None, None)
    for u in range(1, len(units) + 1):
        m = jnp.max(m8, axis=0, keepdims=True)
        m8, acc = phase(u if u < len(units) else None, u - 1, m)
        outs.append(acc[QK_NOPE_DIM:, :] / acc[0:1, :])
    for qb in range(qps):
        o_ref[cols(qb), :] = jnp.concatenate(outs[qb * hps:(qb + 1) * hps], axis=0).T.astype(BF16)


def _attention(qt, k, vt, *, n_batch, t_q, t_k, q_row0, kv_row0, hps, qps):
    tq = TQ * qps
    nq = t_q // tq
    q0 = q_row0 // tq
    k0 = kv_row0 // t_k
    return pl.pallas_call(
        functools.partial(_attn_kernel, t_k=t_k, hps=hps, qps=qps),
        grid=(n_batch, N_HEADS // hps, nq),
        in_specs=[
            pl.BlockSpec((hps * HEAD_PAD, tq), lambda b, h, i: (h, q0 + b * nq + i)),
            pl.BlockSpec((t_k, hps * HEAD_PAD), lambda b, h, i: (k0 + b, h)),
            pl.BlockSpec((hps * HEAD_PAD, t_k), lambda b, h, i: (h, k0 + b)),
        ],
        out_specs=pl.BlockSpec((tq, hps * V_HEAD_DIM), lambda b, h, i: (b * nq + i, h)),
        out_shape=jax.ShapeDtypeStruct((n_batch * t_q, N_HEADS * V_HEAD_DIM), BF16),
        scratch_shapes=[pltpu.VMEM((t_k, TQ), F32), pltpu.VMEM((t_k, TQ), F32)],
        compiler_params=_cparams(("parallel", "parallel", "parallel")),
        name=f"mla_attention_tk{t_k}",
    )(qt, k, vt)


def _out_proj_kernel(ap_ref, as_ref, xp_ref, xs_ref, mod_ref, wo_ref, o_ref):
    is_prompt = pl.program_id(0) < N_PROMPT // WIDE_TB
    a = jnp.where(is_prompt, ap_ref[...], as_ref[...])
    x = jnp.where(is_prompt, xp_ref[...], xs_ref[...])
    o_ref[...] = x + mod_ref[0][2:3] * _dot(a, wo_ref[...])


def _out_proj(attn_p, attn_s, x_p, x_s, p, layer):
    return pl.pallas_call(
        _out_proj_kernel,
        grid=(N_TOK // WIDE_TB,),
        in_specs=_split_rows_specs(WIDE_TB) + _split_rows_specs(WIDE_TB) + [
            _mod_spec(WIDE_TB, layer),
            _layer_spec((D_MODEL, D_MODEL), layer // 2),
        ],
        out_specs=pl.BlockSpec((WIDE_TB, D_MODEL), lambda i: (i, 0)),
        out_shape=jax.ShapeDtypeStruct((N_TOK, D_MODEL), F32),
        compiler_params=_cparams(("parallel",)),
        name="mla_out_proj",
    )(attn_p, attn_s, x_p, x_s, p["mod"], p["m_w_o"])


def _mla_layer(x_p, x_s, p, layer):
    qt, ckv, krp = _mla_proj(x_p, x_s, p, layer)
    k, vt = _kv_expand(ckv, krp, p, layer)
    a_p = _attention(qt, k, vt, n_batch=BATCH, t_q=SEQ, t_k=SEQ, q_row0=0, kv_row0=DEC_BATCH * KV_LEN,
                     hps=N_HEADS, qps=1)
    a_s = _attention(qt, k, vt, n_batch=DEC_BATCH, t_q=DEC_SEQ, t_k=KV_LEN, q_row0=N_PROMPT, kv_row0=0,
                     hps=4, qps=2)
    x = _out_proj(a_p, a_s, x_p, x_s, p, layer)
    state_ckv = ckv[:N_PROMPT].reshape(BATCH, SEQ, KV_RANK)
    state_krope = krp[:N_PROMPT, QK_NOPE_DIM:QK_HEAD_DIM].reshape(BATCH, SEQ, QK_ROPE_DIM)
    return x, state_ckv, state_krope


def _tile_plan(cnt8, plan_ref, start_ref):
    tm = float(TM)
    cnt_row = cnt8[0:1, :]
    cnt_col = cnt8.T[:, 0:1]
    tiles_row = jnp.floor((cnt_row + (tm - 1.0)) * (1.0 / tm))
    tiles_col = jnp.floor((cnt_col + (tm - 1.0)) * (1.0 / tm))
    sub = lax.broadcasted_iota(jnp.int32, (LANE, LANE), 0).astype(F32)
    lan = lax.broadcasted_iota(jnp.int32, (LANE, LANE), 1).astype(F32)
    tile_end_row = jnp.sum(jnp.where(sub <= lan, tiles_col, 0.0), axis=0, keepdims=True)
    tile_end_col = jnp.sum(jnp.where(lan <= sub, tiles_row, 0.0), axis=1, keepdims=True)
    n_used = jnp.max(tile_end_row, axis=1, keepdims=True)
    start_col = (tile_end_col - tiles_col) * tm
    end_col = start_col + cnt_col
    cand = jnp.where(jnp.logical_and(lan > sub, tiles_row > 0.0), lan, float(LANE))
    next_col = jnp.min(cand, axis=1, keepdims=True)
    next_col = jnp.where(next_col < float(LANE), next_col, -1.0)
    n_lanes = plan_ref.shape[1]
    tidx = jnp.minimum(lax.broadcasted_iota(jnp.int32, (LANE, n_lanes), 1).astype(F32), n_used - 1.0)
    esub = lax.broadcasted_iota(jnp.int32, (LANE, n_lanes), 0).astype(F32)
    te_row = jnp.sum(jnp.where(tile_end_col <= tidx, 1.0, 0.0), axis=0, keepdims=True)
    mine = esub == te_row
    end_at = jnp.sum(jnp.where(mine, end_col, 0.0), axis=0, keepdims=True)
    tv_row = jnp.clip(end_at - tidx[0:1, :] * tm, 0.0, tm)
    nx_row = jnp.sum(jnp.where(mine, next_col, 0.0), axis=0, keepdims=True)
    nu_row = jnp.broadcast_to(n_used, (1, n_lanes))
    plan_ref[...] = jnp.concatenate([te_row, tv_row, nx_row, nu_row, jnp.zeros((SUBLANE - 4, n_lanes), F32)], axis=0)
    start_ref[...] = jnp.broadcast_to(start_col, (LANE, LANE))


def _route_kernel(x_ref, mod_ref, g_ref, wr_ref, br_ref, tri_ref, h_ref, meta_ref, metat_ref, plan_ref, start_ref,
                  carry):
    i = pl.program_id(0)

    @pl.when(i == 0)
    def _():
        carry[...] = jnp.zeros_like(carry)

    x = x_ref[...]
    m = mod_ref[0]
    h = _rms(x, g_ref[...]) * (1.0 + m[4:5]) + m[3:4]
    h_ref[...] = _pack_bf16_pairs(h)
    logits = _dot(h.astype(BF16), wr_ref[...]) + br_ref[...]
    lane = lax.broadcasted_iota(jnp.int32, logits.shape, 1).astype(F32)
    work = logits
    sel = jnp.zeros(logits.shape, F32)
    hits, tops = [], []
    for k in range(TOP_K):
        mk = jnp.max(work, axis=-1, keepdims=True)
        first = jnp.min(jnp.where(work == mk, lane, float(LANE)), axis=-1, keepdims=True)
        hit = lane == first
        sel = jnp.where(hit, 1.0, sel)
        work = jnp.where(hit, -jnp.inf, work)
        hits.append((hit, first))
        tops.append(mk)
    es = [jnp.exp(t - tops[0]) for t in tops]
    denom = es[0] + es[1] + es[2] + es[3]
    pos = _dot(tri_ref[...], sel.astype(BF16)) + carry[0:1, :]
    carry[...] = carry[...] + jnp.sum(sel, axis=0, keepdims=True)

    @pl.when(i == pl.num_programs(0) - 1)
    def _():
        _tile_plan(carry[...], plan_ref, start_ref)

    meta = jnp.zeros(logits.shape, F32)
    for k in range(TOP_K):
        hit, first = hits[k]
        pk = jnp.sum(jnp.where(hit, pos, 0.0), axis=-1, keepdims=True)
        meta = jnp.where(lane == float(k), first, meta)
        meta = jnp.where(lane == float(TOP_K + k), es[k] / denom, meta)
        meta = jnp.where(lane == float(2 * TOP_K + k), pk, meta)
    meta_ref[...] = meta
    metat_ref[...] = meta.T[:2 * SUBLANE, :]


def _route(x, p, layer):
    tb = ROUTE_TB
    return pl.pallas_call(
        _route_kernel,
        grid=(N_TOK // tb,),
        in_specs=[
            pl.BlockSpec((tb, D_MODEL), lambda i: (i, 0)),
            _mod_spec(tb, layer),
            _layer_spec((1, D_MODEL), layer),
            _layer_spec((D_MODEL, LANE), layer),
            _layer_spec((1, LANE), layer),
            _const_spec((tb, tb)),
        ],
        out_specs=[
            pl.BlockSpec((tb, D_MODEL // 2), lambda i: (i, 0)),
            pl.BlockSpec((tb, LANE), lambda i: (i, 0)),
            pl.BlockSpec((2 * SUBLANE, tb), lambda i: (0, i)),
            _const_spec((SUBLANE, PLAN_LANES)),
            _const_spec((LANE, LANE)),
        ],
        out_shape=[
            jax.ShapeDtypeStruct((N_TOK, D_MODEL // 2), jnp.uint32),
            jax.ShapeDtypeStruct((N_TOK, LANE), F32),
            jax.ShapeDtypeStruct((2 * SUBLANE, N_TOK), F32),
            jax.ShapeDtypeStruct((SUBLANE, PLAN_LANES), F32),
            jax.ShapeDtypeStruct((LANE, LANE), F32),
        ],
        scratch_shapes=[pltpu.VMEM((SUBLANE, LANE), F32)],
        compiler_params=_cparams(("arbitrary",)),
        name="moe_route",
    )(x, p["mod"], p["norm_ffn_g"], p["e_w_router"], p["e_b_router"], p["tri"])


def _slots_kernel(start_ref, metat_ref, dest_ref):
    start_col = start_ref[:, 0:1]
    esub = lax.broadcasted_iota(jnp.int32, (LANE, SLOT_TB), 0).astype(F32)
    rows = []
    for k in range(TOP_K):
        e = metat_ref[k:k + 1, :]
        first = jnp.sum(jnp.where(esub == e, start_col, 0.0), axis=0, keepdims=True)
        rows.append(first + metat_ref[2 * TOP_K + k:2 * TOP_K + k + 1, :])
    dest_ref[...] = jnp.concatenate(rows, axis=0).astype(jnp.int32)


def _slots(start, meta_t):
    return pl.pallas_call(
        _slots_kernel,
        grid=(N_TOK // SLOT_TB,),
        in_specs=[
            _const_spec((LANE, LANE)),
            pl.BlockSpec((2 * SUBLANE, SLOT_TB), lambda i: (0, i)),
        ],
        out_specs=pl.BlockSpec((TOP_K, SLOT_TB), lambda i: (0, i)),
        out_shape=jax.ShapeDtypeStruct((TOP_K, N_TOK), jnp.int32),
        compiler_params=_cparams(("parallel",)),
        name="moe_slots",
    )(start, meta_t)


def _sc_gather(table, idx, ch):
    b, w = idx.shape[0], table.shape[1]
    per_w = b // SC_WORKERS
    n_ch = per_w // ch
    assert per_w * SC_WORKERS == b and n_ch * ch == per_w and n_ch % 2 == 0
    mesh = plsc.VectorSubcoreMesh(core_axis_name="c", subcore_axis_name="s")

    @functools.partial(
        pl.kernel, mesh=mesh,
        out_type=jax.ShapeDtypeStruct((b, w), table.dtype),
        scratch_types=[
            pltpu.VMEM((n_ch, ch), jnp.int32),
            pltpu.VMEM((ch, w), table.dtype),
            pltpu.VMEM((ch, w), table.dtype),
            pltpu.SemaphoreType.DMA, pltpu.SemaphoreType.DMA,
            pltpu.SemaphoreType.DMA, pltpu.SemaphoreType.DMA,
        ],
        name="sc_row_gather",
    )
    def gather_rows(table_hbm, idx_hbm, out_hbm, idx_v, buf0, buf1, g0, g1, s0, s1):
        wid = lax.axis_index("s") * SC_CORES + lax.axis_index("c")
        base = wid * per_w
        pltpu.sync_copy(idx_hbm.at[wid], idx_v)

        def gather(j, buf, sem):
            return pltpu.make_async_copy(table_hbm.at[idx_v.at[j]], buf, sem)

        def store(j, buf, sem):
            return pltpu.make_async_copy(buf, out_hbm.at[pl.ds(base + j * ch, ch)], sem)

        gather(0, buf0, g0).start()

        @pl.loop(0, n_ch, step=2)
        def _(j):
            @pl.when(j > 0)
            def _():
                store(j - 1, buf1, s1).wait()

            gather(j + 1, buf1, g1).start()
            gather(j, buf0, g0).wait()
            store(j, buf0, s0).start()
            gather(j + 1, buf1, g1).wait()
            store(j + 1, buf1, s1).start()
            store(j, buf0, s0).wait()

            @pl.when(j + 2 < n_ch)
            def _():
                gather(j + 2, buf0, g0).start()

        store(n_ch - 1, buf1, s1).wait()

    return gather_rows(table, idx.reshape(SC_WORKERS, n_ch, ch))


def _sc_dispatch(rows, dest_t, ch):
    n, w = rows.shape
    per_w = n // SC_WORKERS
    n_ch = per_w // ch
    assert per_w * SC_WORKERS == n and n_ch * ch == per_w and n_ch % 2 == 0
    mesh = plsc.VectorSubcoreMesh(core_axis_name="c", subcore_axis_name="s")
    idx = dest_t.reshape(TOP_K, SC_WORKERS, n_ch, ch)

    @functools.partial(
        pl.kernel, mesh=mesh,
        out_type=jax.ShapeDtypeStruct((N_SLOTS, w), rows.dtype),
        scratch_types=[
            pltpu.VMEM((TOP_K * n_ch, ch), jnp.int32),
            pltpu.VMEM((ch, w), rows.dtype),
            pltpu.VMEM((ch, w), rows.dtype),
            pltpu.SemaphoreType.DMA, pltpu.SemaphoreType.DMA,
            pltpu.SemaphoreType.DMA, pltpu.SemaphoreType.DMA,
        ],
        name="sc_row_dispatch",
    )
    def dispatch_rows(rows_hbm, idx_hbm, out_hbm, idx_v, buf0, buf1, l0, l1, s0, s1):
        wid = lax.axis_index("s") * SC_CORES + lax.axis_index("c")
        base = wid * per_w
        for k in range(TOP_K):
            pltpu.sync_copy(idx_hbm.at[k, wid], idx_v.at[pl.ds(k * n_ch, n_ch)])

        def load(j, buf, sem):
            return pltpu.make_async_copy(rows_hbm.at[pl.ds(base + j * ch, ch)], buf, sem)

        def scatter(j, k, buf, sem):
            return pltpu.make_async_copy(buf, out_hbm.at[idx_v.at[k * n_ch + j]], sem)

        load(0, buf0, l0).start()

        @pl.loop(0, n_ch, step=2)
        def _(j):
            load(j + 1, buf1, l1).start()
            load(j, buf0, l0).wait()
            for k in range(TOP_K):
                scatter(j, k, buf0, s0).start()
            load(j + 1, buf1, l1).wait()
            for k in range(TOP_K):
                scatter(j + 1, k, buf1, s1).start()
            for k in range(TOP_K):
                scatter(j, k, buf0, s0).wait()

            @pl.when(j + 2 < n_ch)
            def _():
                load(j + 2, buf0, l0).start()

            for k in range(TOP_K):
                scatter(j + 1, k, buf1, s1).wait()

    return dispatch_rows(rows, idx)


def _deinterleave_matrix():
    src = jnp.arange(2 * LANE)[:, None]
    dst = jnp.arange(2 * LANE)[None, :]
    want = jnp.where(dst < LANE, 2 * dst, 2 * (dst - LANE) + 1)
    return (src == want).astype(BF16)


def _expert_kernel(te_ref, nu_ref, tv_ref, nx_ref, x_ref, wgu_hbm, bgu_ref, wd_hbm, bd_ref, perm_ref, o_ref,
                   wgu_st, wd_st, wgu_bf, wd_bf, sems, *, layer):
    i = pl.program_id(0)
    prev = te_ref[jnp.maximum(i - 1, 0)]
    fresh = jnp.logical_or(i == 0, te_ref[i] != prev)

    def fetch(e):
        return (pltpu.make_async_copy(wgu_hbm.at[layer, e], wgu_st, sems.at[0]),
                pltpu.make_async_copy(wd_hbm.at[layer, e], wd_st, sems.at[1]))

    @pl.when(i == 0)
    def _():
        for cp in fetch(te_ref[0]):
            cp.start()

    @pl.when(jnp.logical_and(fresh, i < nu_ref[0]))
    def _():
        for cp in fetch(te_ref[i]):
            cp.wait()
        for b in range(2 * D_FF // (2 * LANE)):
            sl = slice(b * 2 * LANE, (b + 1) * 2 * LANE)
            wgu_bf[:, sl] = _dot(wgu_st[:, sl].astype(BF16), perm_ref[...]).astype(BF16)
        wd_bf[...] = wd_st[...].astype(BF16)

        @pl.when(nx_ref[i] >= 0)
        def _():
            for cp in fetch(nx_ref[i]):
                cp.start()

    @pl.when(i < nu_ref[0])
    def _():
        row = lax.broadcasted_iota(jnp.int32, (TM, D_MODEL // 2), 0)
        w = jnp.where(row < tv_ref[i], x_ref[...], jnp.uint32(0))
        x = _unpack_bf16_pairs(w).astype(BF16)
        gu = _dot(x, wgu_bf[...]) + bgu_ref[...]
        acts = []
        for b in range(D_FF // LANE):
            glu = jnp.minimum(gu[:, b * 2 * LANE:b * 2 * LANE + LANE], SWIGLU_LIMIT)
            lin = jnp.clip(gu[:, b * 2 * LANE + LANE:(b + 1) * 2 * LANE], -SWIGLU_LIMIT, SWIGLU_LIMIT)
            acts.append((glu * jax.nn.sigmoid(SWIGLU_ALPHA * glu) * (lin + 1.0)).astype(BF16))
        act = jnp.concatenate(acts, axis=1)
        o_ref[...] = _pack_bf16_pairs(_dot(act, wd_bf[...]) + bd_ref[...])


def _experts(buf, tile_expert, n_used, tile_valid, tile_next, p, layer):
    def row_idx(i, te, nu, tv, nx):
        return (jnp.minimum(i, nu[0] - 1), 0)

    def b_idx(i, te, nu, tv, nx):
        return (layer, te[i], 0, 0)

    grid_spec = pltpu.PrefetchScalarGridSpec(
        num_scalar_prefetch=4,
        grid=(N_TILES,),
        in_specs=[
            pl.BlockSpec((TM, D_MODEL // 2), row_idx),
            pl.BlockSpec(memory_space=pl.ANY),
            pl.BlockSpec((None, None, 1, 2 * D_FF), b_idx),
            pl.BlockSpec(memory_space=pl.ANY),
            pl.BlockSpec((None, None, 1, D_MODEL), b_idx),
            _const_spec((2 * LANE, 2 * LANE)),
        ],
        out_specs=pl.BlockSpec((TM, D_MODEL // 2), row_idx),
        scratch_shapes=[
            pltpu.VMEM((D_MODEL, 2 * D_FF), F32),
            pltpu.VMEM((D_FF, D_MODEL), F32),
            pltpu.VMEM((D_MODEL, 2 * D_FF), BF16),
            pltpu.VMEM((D_FF, D_MODEL), BF16),
            pltpu.SemaphoreType.DMA((2,)),
        ],
    )
    return pl.pallas_call(
        functools.partial(_expert_kernel, layer=layer),
        grid_spec=grid_spec,
        out_shape=jax.ShapeDtypeStruct((N_SLOTS, D_MODEL // 2), jnp.uint32),
        compiler_params=_cparams(("arbitrary",)),
        name="moe_experts",
    )(tile_expert, n_used, tile_valid, tile_next, buf, p["e_w_gu"], p["e_b_gu"], p["e_w_down"], p["e_b_down"],
      p["deinterleave"])


def _pack_bf16_pairs(v):
    half = v.shape[1] // 2
    bits = pltpu.bitcast(v.astype(BF16).astype(F32), jnp.uint32)
    return (bits[:, half:] & jnp.uint32(0xFFFF0000)) | (bits[:, :half] >> 16)


def _unpack_bf16_pairs(w):
    return jnp.concatenate([pltpu.bitcast(w << 16, F32), pltpu.bitcast(w & jnp.uint32(0xFFFF0000), F32)],
                           axis=1)


def _combine_kernel(x_ref, mod_ref, y_ref, w_ref, o_ref):
    w = w_ref[:, TOP_K:2 * TOP_K]
    y = _unpack_bf16_pairs(y_ref[0]) * w[:, 0:1]
    for k in range(1, TOP_K):
        y = y + _unpack_bf16_pairs(y_ref[k]) * w[:, k:k + 1]
    o_ref[...] = x_ref[...] + mod_ref[0][5:6] * y


def _combine(x, y4, meta, p, layer, part):
    n_rows = N_TOK // MOE_PARTS
    tb = WIDE_TB
    first = part * n_rows // tb
    return pl.pallas_call(
        _combine_kernel,
        grid=(n_rows // tb,),
        in_specs=[
            pl.BlockSpec((tb, D_MODEL), lambda i: (i + first, 0)),
            _mod_spec(tb, layer, first),
            pl.BlockSpec((TOP_K, tb, D_MODEL // 2), lambda i: (0, i, 0)),
            pl.BlockSpec((tb, LANE), lambda i: (i + first, 0)),
        ],
        out_specs=pl.BlockSpec((tb, D_MODEL), lambda i: (i, 0)),
        out_shape=jax.ShapeDtypeStruct((n_rows, D_MODEL), F32),
        compiler_params=_cparams(("parallel",)),
        name="moe_combine",
    )(x, p["mod"], y4, meta)


def _moe_layer(x, p, layer):
    hp, meta, meta_t, plan, start = _route(x, p, layer)
    plan = plan[:4, :N_TILES].astype(jnp.int32)
    tile_expert, tile_valid, tile_next, n_used = plan[0], plan[1], plan[2], plan[3, :1]
    dest_t = _slots(start, meta_t)
    buf = _sc_dispatch(hp, dest_t, 64)
    yb = _experts(buf, tile_expert, n_used, tile_valid, tile_next, p, layer)
    n_rows = N_TOK // MOE_PARTS
    outs = []
    for part in range(MOE_PARTS):
        idx = dest_t[:, part * n_rows:(part + 1) * n_rows].reshape(-1)
        y4 = _sc_gather(yb, idx, 64).reshape(TOP_K, n_rows, D_MODEL // 2)
        outs.append(_combine(x, y4, meta, p, layer, part))
    return outs


def _prepare(c, cache_ckv, cache_krope, c_ctx, norm_mix_g, norm_ffn_g, w_mod, b_mod, g_w_in, g_b_in, g_norm_v,
             g_w_s, g_b_s, g_w_out, m_w_dq, m_norm_q, m_w_uq, m_w_dkv, m_norm_kv, m_w_ukv, m_qk_norm_q,
             m_qk_norm_k, m_w_o, e_w_router, e_b_router, e_w_gu, e_b_gu, e_w_down, e_b_down):
    n_mla = m_w_dq.shape[0]
    cond = jnp.concatenate([c_ctx[None, :], c, jnp.zeros((SUBLANE - N_COND, D_MODEL), F32)], axis=0)
    wdkv = jnp.concatenate([m_w_dkv[..., :KV_RANK], jnp.zeros((n_mla, D_MODEL, QK_NOPE_DIM), F32),
                            m_w_dkv[..., KV_RANK:], jnp.zeros((n_mla, D_MODEL, HEAD_PAD - QK_HEAD_DIM), F32)],
                           axis=-1)
    w_uq = jnp.pad(m_w_uq.reshape(n_mla, Q_RANK, N_HEADS, QK_HEAD_DIM),
                   ((0, 0), (0, 0), (0, 0), (0, HEAD_PAD - QK_HEAD_DIM))).reshape(n_mla, Q_RANK, QKV_W)

    def gain_col(g):
        return jnp.pad(g, ((0, 0), (0, HEAD_PAD - QK_HEAD_DIM)))[:, :, None]

    return {
        "mod": _modulation(cond, w_mod, b_mod),
        "rope_t": _rope_tables(),
        "norm_mix_g": norm_mix_g[:, None, :],
        "norm_ffn_g": norm_ffn_g[:, None, :],
        "g_w_in": g_w_in.astype(BF16),
        "g_b_in": g_b_in[:, None, :],
        "g_norm_v": g_norm_v[:, None, :],
        "g_w_s": g_w_s.astype(BF16),
        "g_b_st": jnp.swapaxes(g_b_s, 1, 2),
        "g_w_out": g_w_out.astype(BF16),
        "m_w_dq": m_w_dq.astype(BF16),
        "m_norm_q": m_norm_q[:, None, :],
        "m_w_uq_t": jnp.swapaxes(w_uq, 1, 2).astype(BF16),
        "m_gq": gain_col(m_qk_norm_q),
        "m_w_dkv": wdkv.astype(BF16),
        "m_norm_kv": m_norm_kv[:, None, :],
        "m_w_ukv_t": jnp.swapaxes(m_w_ukv, 1, 2).astype(BF16),
        "m_gk": gain_col(m_qk_norm_k),
        "m_w_o": m_w_o.astype(BF16),
        "cache_ckv": cache_ckv,
        "cache_krp": jnp.pad(cache_krope, ((0, 0), (0, 0), (0, 0), (QK_NOPE_DIM, HEAD_PAD - QK_HEAD_DIM))),
        "e_w_router": jnp.pad(e_w_router, ((0, 0), (0, 0), (0, LANE - N_EXPERTS))).astype(BF16),
        "e_b_router": jnp.pad(e_b_router, ((0, 0), (0, LANE - N_EXPERTS)), constant_values=NEG_BIG)[:, None, :],
        "tri": jnp.tri(ROUTE_TB, ROUTE_TB, -1, dtype=BF16),
        "e_w_gu": e_w_gu,
        "e_b_gu": e_b_gu.reshape(DEPTH, N_EXPERTS, D_FF // LANE, LANE, 2).swapaxes(3, 4).reshape(
            DEPTH, N_EXPERTS, 1, 2 * D_FF),
        "e_w_down": e_w_down,
        "e_b_down": e_b_down[:, :, None, :],
        "deinterleave": _deinterleave_matrix(),
    }


def kernel(x_prompt, x_sample, c, cache_ckv, cache_krope, c_ctx, norm_mix_g, norm_ffn_g, w_mod, b_mod,
           g_w_in, g_b_in, g_norm_v, g_w_s, g_b_s, g_w_out, m_w_dq, m_norm_q, m_w_uq, m_w_dkv,
           m_norm_kv, m_w_ukv, m_qk_norm_q, m_qk_norm_k, m_w_o, e_w_router, e_b_router, e_w_gu,
           e_b_gu, e_w_down, e_b_down):
    p = _prepare(c, cache_ckv, cache_krope, c_ctx, norm_mix_g, norm_ffn_g, w_mod, b_mod, g_w_in, g_b_in,
                 g_norm_v, g_w_s, g_b_s, g_w_out, m_w_dq, m_norm_q, m_w_uq, m_w_dkv, m_norm_kv, m_w_ukv,
                 m_qk_norm_q, m_qk_norm_k, m_w_o, e_w_router, e_b_router, e_w_gu, e_b_gu, e_w_down, e_b_down)
    assert MOE_PARTS == 2 and N_PROMPT == N_SAMPLE
    x_p, x_s = x_prompt.reshape(N_PROMPT, D_MODEL), x_sample.reshape(N_SAMPLE, D_MODEL)
    ckv_states, krope_states = [], []
    for layer in range(DEPTH):
        if layer % 2 == 0:
            x = _gmlp_layer(x_p, x_s, p, layer)
        else:
            x, s_ckv, s_krope = _mla_layer(x_p, x_s, p, layer)
            ckv_states.append(s_ckv)
            krope_states.append(s_krope)
        x_p, x_s = _moe_layer(x, p, layer)
    y_prompt = x_p.reshape(BATCH, SEQ, D_MODEL)
    y_sample = x_s.reshape(DEC_BATCH, DEC_SEQ, D_MODEL)
    return (y_prompt, y_sample, jnp.stack(ckv_states, axis=1), jnp.stack(krope_states, axis=1))
```

```python
import functools
import math

import jax
import jax.numpy as jnp
from jax import lax
from jax.experimental import pallas as pl
from jax.experimental.pallas import tpu as pltpu
from jax.experimental.pallas import tpu_sc as plsc

F32 = jnp.float32
BF16 = jnp.bfloat16

D_MODEL = 1024
BATCH = 32
SEQ = 256
DEPTH = 4
DEC_BATCH = 2
DEC_SEQ = 4096
PAST_LEN = 512
GRID_W = 64
RMS_EPS = 1e-6
GMLP_WIDTH = 2 * D_MODEL
GMLP_GROUPS = 8
GROUP_W = GMLP_WIDTH // GMLP_GROUPS
CHUNK = 128
N_HEADS = 16
QK_NOPE_DIM = 64
QK_ROPE_DIM = 32
QK_HEAD_DIM = QK_NOPE_DIM + QK_ROPE_DIM
V_HEAD_DIM = 64
Q_RANK = 256
KV_RANK = 128
ROPE_THETA = 10000.0
N_EXPERTS = 32
TOP_K = 4
D_FF = D_MODEL
SWIGLU_LIMIT = 7.0
SWIGLU_ALPHA = 1.702

N_PROMPT = BATCH * SEQ
N_SAMPLE = DEC_BATCH * DEC_SEQ
N_TOK = N_PROMPT + N_SAMPLE
N_COND = 1 + DEC_BATCH
KV_LEN = PAST_LEN + DEC_SEQ
N_KV_ROWS = DEC_BATCH * KV_LEN + N_PROMPT

LANE = 128
SUBLANE = 8
HEAD_PAD = LANE
QKV_W = N_HEADS * HEAD_PAD
VMEM_LIMIT = 56 * 1024 * 1024

TB = 256
WIDE_TB = 512
TQ = 256
TKC = 256
ATTN_CHUNKS_PER_ITER = 9
TM = 512
N_TILES = N_TOK * TOP_K // TM + N_EXPERTS
N_SLOTS = N_TILES * TM
PLAN_LANES = -(-N_TILES // LANE) * LANE
SLOT_TB = 2048
ROUTE_TB = 512
MOE_PARTS = 2
SC_CORES = 2
SC_WORKERS = SC_CORES * 16
ROPE_BLOCKS = DEC_SEQ // TB
NEG_BIG = -1e30


def _cparams(sem):
    return pltpu.CompilerParams(dimension_semantics=sem, vmem_limit_bytes=VMEM_LIMIT)


def _cond_of_block(i, tb):
    n_p = N_PROMPT // tb
    per = DEC_SEQ // tb
    return jnp.where(i < n_p, 0, 1 + (i - n_p) // per)


def _rms(x, g, n=None):
    n = x.shape[-1] if n is None else n
    ss = jnp.sum(x * x, axis=-1, keepdims=True) * (1.0 / n)
    return x * lax.rsqrt(ss + RMS_EPS) * g


def _dot(a, b):
    return jnp.dot(a, b, preferred_element_type=F32)


def _mod_kernel(c_ref, w_ref, b_ref, o_ref):
    c = c_ref[...]
    s = c * jax.nn.sigmoid(c)
    o_ref[0] = _dot(s.astype(BF16), w_ref[0].astype(BF16)) + b_ref[0]


def _modulation(cond, w_mod, b_mod):
    tn = 1536
    out = pl.pallas_call(
        _mod_kernel,
        grid=(DEPTH, 6 * D_MODEL // tn),
        in_specs=[
            pl.BlockSpec((SUBLANE, D_MODEL), lambda l, j: (0, 0)),
            pl.BlockSpec((1, D_MODEL, tn), lambda l, j: (l, 0, j)),
            pl.BlockSpec((1, 1, tn), lambda l, j: (l, 0, j)),
        ],
        out_specs=pl.BlockSpec((1, SUBLANE, tn), lambda l, j: (l, 0, j)),
        out_shape=jax.ShapeDtypeStruct((DEPTH, SUBLANE, 6 * D_MODEL), F32),
        compiler_params=_cparams(("parallel", "parallel")),
        name="adaln_mod",
    )(cond, w_mod, b_mod.reshape(DEPTH, 1, 6 * D_MODEL))
    m = out[:, :N_COND].reshape(DEPTH, N_COND, 6, D_MODEL)
    return jnp.pad(m, ((0, 0), (0, 0), (0, SUBLANE - 6), (0, 0)))


def _mod_spec(tb, layer, first_block=0):
    return pl.BlockSpec((None, 1, SUBLANE, D_MODEL),
                        lambda i: (layer, _cond_of_block(i + first_block, tb), 0, 0))


def _const_spec(shape):
    nd = len(shape)
    return pl.BlockSpec(shape, lambda *_: (0,) * nd)


def _layer_spec(shape, j):
    nd = len(shape)
    return pl.BlockSpec((None,) + tuple(shape), lambda *_: (j,) + (0,) * nd)


def _gmlp_kernel(xp_ref, xs_ref, mod_ref, g_ref, win_ref, bin_ref, gv_ref, ws_ref, bst_ref, wout_ref, o_ref):
    x = jnp.where(pl.program_id(0) < N_PROMPT // WIDE_TB, xp_ref[...], xs_ref[...])
    m = mod_ref[0]
    h = _rms(x, g_ref[...]) * (1.0 + m[1:2]) + m[0:1]
    hb = h.astype(BF16)
    zv = jax.nn.gelu(_dot(hb, win_ref[:, GMLP_WIDTH:]) + bin_ref[:, GMLP_WIDTH:], approximate=True)
    vn = _rms(zv, gv_ref[...]).astype(BF16)
    rows = []
    for c in range(WIDE_TB // CHUNK):
        cols = []
        for g in range(GMLP_GROUPS):
            blk = vn[c * CHUNK:(c + 1) * CHUNK, g * GROUP_W:(g + 1) * GROUP_W]
            cols.append(_dot(ws_ref[g], blk) + bst_ref[:, g:g + 1])
        rows.append(jnp.concatenate(cols, axis=1))
    vm = jnp.concatenate(rows, axis=0)
    u = jax.nn.gelu(_dot(hb, win_ref[:, :GMLP_WIDTH]) + bin_ref[:, :GMLP_WIDTH], approximate=True)
    d = _dot((u * vm).astype(BF16), wout_ref[...])
    o_ref[...] = x + m[2:3] * d


def _split_rows_specs(tb=TB):
    n_pb = N_PROMPT // tb
    return [pl.BlockSpec((tb, D_MODEL), lambda i: (jnp.minimum(i, n_pb - 1), 0)),
            pl.BlockSpec((tb, D_MODEL), lambda i: (jnp.maximum(i - n_pb, 0), 0))]


def _gmlp_layer(x_p, x_s, p, layer):
    j = layer // 2
    return pl.pallas_call(
        _gmlp_kernel,
        grid=(N_TOK // WIDE_TB,),
        in_specs=_split_rows_specs(WIDE_TB) + [
            _mod_spec(WIDE_TB, layer),
            _layer_spec((1, D_MODEL), layer),
            _layer_spec((D_MODEL, 2 * GMLP_WIDTH), j),
            _layer_spec((1, 2 * GMLP_WIDTH), j),
            _layer_spec((1, GMLP_WIDTH), j),
            _layer_spec((GMLP_GROUPS, CHUNK, CHUNK), j),
            _layer_spec((CHUNK, GMLP_GROUPS), j),
            _layer_spec((GMLP_WIDTH, D_MODEL), j),
        ],
        out_specs=pl.BlockSpec((WIDE_TB, D_MODEL), lambda i: (i, 0)),
        out_shape=jax.ShapeDtypeStruct((N_TOK, D_MODEL), F32),
        compiler_params=_cparams(("parallel",)),
        name="gmlp_mixer",
    )(x_p, x_s, p["mod"], p["norm_mix_g"], p["g_w_in"], p["g_b_in"], p["g_norm_v"], p["g_w_s"], p["g_b_st"],
      p["g_w_out"])


def _rope_tables():
    t = jnp.arange(DEC_SEQ)
    row_id = (t // GRID_W).astype(F32)
    col_id = (t % GRID_W).astype(F32)
    axis_dim = QK_ROPE_DIM // 2
    inv_freq = ROPE_THETA ** (-jnp.arange(0, axis_dim, 2, dtype=F32) / axis_dim)
    ang = jnp.stack([row_id[:, None] * inv_freq, col_id[:, None] * inv_freq], axis=1)
    cos, sin = jnp.cos(ang), jnp.sin(ang)
    zeros = jnp.zeros_like(sin)
    cos_l = jnp.concatenate([cos, cos], axis=-1).reshape(DEC_SEQ, QK_ROPE_DIM)
    s1_l = jnp.concatenate([-sin, zeros], axis=-1).reshape(DEC_SEQ, QK_ROPE_DIM)
    s2_l = jnp.concatenate([zeros, sin], axis=-1).reshape(DEC_SEQ, QK_ROPE_DIM)

    def widen(rope_part, nope_fill):
        left = jnp.full((DEC_SEQ, QK_NOPE_DIM), nope_fill, F32)
        right = jnp.zeros((DEC_SEQ, HEAD_PAD - QK_HEAD_DIM), F32)
        return jnp.concatenate([left, rope_part, right], axis=-1)

    pos = jnp.stack([widen(cos_l, 1.0), widen(s1_l, 0.0), widen(s2_l, 0.0)])
    ident_c = jnp.concatenate([jnp.ones((TB, QK_HEAD_DIM), F32),
                               jnp.zeros((TB, HEAD_PAD - QK_HEAD_DIM), F32)], axis=-1)
    ident = jnp.stack([ident_c, jnp.zeros_like(ident_c), jnp.zeros_like(ident_c)])
    return jnp.swapaxes(jnp.concatenate([pos, ident], axis=1), 1, 2)


def _dot_nt(a, b):
    return lax.dot_general(a, b, (((1,), (1,)), ((), ())), preferred_element_type=F32)


def _shift_rows(x, n):
    n = n % x.shape[0]
    return jnp.concatenate([x[n:], x[:n]], axis=0)


def _rope_rows(xn, tabt_ref):
    half = QK_ROPE_DIM // 4
    return xn * tabt_ref[0] + _shift_rows(xn, half) * tabt_ref[1] + _shift_rows(xn, -half) * tabt_ref[2]


def _mla_proj_kernel(xp_ref, xs_ref, mod_ref, g_ref, wdq_ref, nq_ref, wuqt_ref, gq_ref, wdkv_ref, nkv_ref,
                     tabt_ref, qt_ref, ckv_ref, krp_ref):
    x = jnp.where(pl.program_id(0) < N_PROMPT // TB, xp_ref[...], xs_ref[...])
    m = mod_ref[0]
    h = _rms(x, g_ref[...]) * (1.0 + m[1:2]) + m[0:1]
    hb = h.astype(BF16)
    cq = _rms(_dot(hb, wdq_ref[...]), nq_ref[...])
    qt = _dot_nt(wuqt_ref[...], cq.astype(BF16))
    gq = gq_ref[...]
    for hd in range(N_HEADS):
        qh = qt[hd * HEAD_PAD:(hd + 1) * HEAD_PAD, :]
        ss = jnp.sum(qh * qh, axis=0, keepdims=True) * (1.0 / QK_HEAD_DIM)
        qn = qh * lax.rsqrt(ss + RMS_EPS) * gq
        qt_ref[hd * HEAD_PAD:(hd + 1) * HEAD_PAD, :] = _rope_rows(qn, tabt_ref).astype(BF16)
    kva = _dot(hb, wdkv_ref[...])
    ckv_ref[...] = _rms(kva[:, :KV_RANK], nkv_ref[...])
    krp_ref[...] = kva[:, KV_RANK:]


def _mla_proj(x_p, x_s, p, layer):
    j = layer // 2
    n_pb = N_PROMPT // TB

    def tab_idx(i):
        return (0, 0, jnp.where(i < n_pb, ROPE_BLOCKS, (i - n_pb) % ROPE_BLOCKS))

    return pl.pallas_call(
        _mla_proj_kernel,
        grid=(N_TOK // TB,),
        in_specs=_split_rows_specs() + [
            _mod_spec(TB, layer),
            _layer_spec((1, D_MODEL), layer),
            _layer_spec((D_MODEL, Q_RANK), j),
            _layer_spec((1, Q_RANK), j),
            _layer_spec((QKV_W, Q_RANK), j),
            _layer_spec((HEAD_PAD, 1), j),
            _layer_spec((D_MODEL, 2 * LANE), j),
            _layer_spec((1, KV_RANK), j),
            pl.BlockSpec((3, HEAD_PAD, TB), tab_idx),
        ],
        out_specs=[
            pl.BlockSpec((QKV_W, TB), lambda i: (0, i)),
            pl.BlockSpec((TB, KV_RANK), lambda i: (i, 0)),
            pl.BlockSpec((TB, HEAD_PAD), lambda i: (i, 0)),
        ],
        out_shape=[
            jax.ShapeDtypeStruct((QKV_W, N_TOK), BF16),
            jax.ShapeDtypeStruct((N_TOK, KV_RANK), F32),
            jax.ShapeDtypeStruct((N_TOK, HEAD_PAD), F32),
        ],
        compiler_params=_cparams(("parallel",)),
        name="mla_proj",
    )(x_p, x_s, p["mod"], p["norm_mix_g"], p["m_w_dq"], p["m_norm_q"], p["m_w_uq_t"], p["m_gq"], p["m_w_dkv"],
      p["m_norm_kv"], p["rope_t"])


def _kv_block_source(i):
    n_sb = DEC_BATCH * KV_LEN // TB
    per = KV_LEN // TB
    b = jnp.minimum(i // per, DEC_BATCH - 1)
    jj = i % per
    return jnp.logical_and(i < n_sb, jj < PAST_LEN // TB), b, jj


def _kv_expand_kernel(cckv_ref, ckrp_ref, ckv_ref, krp_ref, wukvt_ref, gk_ref, tabt_ref, k_ref, vt_ref):
    from_cache, _, _ = _kv_block_source(pl.program_id(0))
    ckv = jnp.where(from_cache, cckv_ref[...], ckv_ref[...])
    krp = jnp.where(from_cache, ckrp_ref[...], krp_ref[...])
    kvt = _dot_nt(wukvt_ref[...], ckv.astype(BF16))
    krpt = krp.T
    gk = gk_ref[...]
    row = lax.broadcasted_iota(jnp.int32, (HEAD_PAD, TB), 0)
    is_nope = row < QK_NOPE_DIM
    ones_row = jnp.where(row == 0, 1.0, 0.0)
    for hd in range(N_HEADS):
        blk = kvt[hd * HEAD_PAD:(hd + 1) * HEAD_PAD, :]
        kraw = jnp.where(is_nope, blk, krpt)
        ss = jnp.sum(kraw * kraw, axis=0, keepdims=True) * (1.0 / QK_HEAD_DIM)
        kn = kraw * lax.rsqrt(ss + RMS_EPS) * gk
        k_ref[:, hd * HEAD_PAD:(hd + 1) * HEAD_PAD] = _rope_rows(kn, tabt_ref).T.astype(BF16)
        vt_ref[hd * HEAD_PAD:(hd + 1) * HEAD_PAD, :] = jnp.where(is_nope, ones_row, blk).astype(BF16)


def _kv_expand(ckv, krp, p, layer):
    j = layer // 2
    n_sb = DEC_BATCH * KV_LEN // TB
    n_cache = PAST_LEN // TB
    n_pb = N_PROMPT // TB
    lat_blocks = DEC_SEQ // TB

    def tab_idx(i):
        from_cache, _, jj = _kv_block_source(i)
        return (0, 0, jnp.where((i >= n_sb) | from_cache, ROPE_BLOCKS, jj - n_cache))

    def cache_idx(i):
        _, b, jj = _kv_block_source(i)
        return (b, j, jnp.minimum(jj, n_cache - 1), 0)

    def tok_idx(i):
        _, b, jj = _kv_block_source(i)
        latent = n_pb + b * lat_blocks + jnp.maximum(jj - n_cache, 0)
        return (jnp.where(i < n_sb, latent, i - n_sb), 0)

    return pl.pallas_call(
        _kv_expand_kernel,
        grid=(N_KV_ROWS // TB,),
        in_specs=[
            pl.BlockSpec((None, None, TB, KV_RANK), cache_idx),
            pl.BlockSpec((None, None, TB, HEAD_PAD), cache_idx),
            pl.BlockSpec((TB, KV_RANK), tok_idx),
            pl.BlockSpec((TB, HEAD_PAD), tok_idx),
            _layer_spec((QKV_W, KV_RANK), j),
            _layer_spec((HEAD_PAD, 1), j),
            pl.BlockSpec((3, HEAD_PAD, TB), tab_idx),
        ],
        out_specs=[
            pl.BlockSpec((TB, QKV_W), lambda i: (i, 0)),
            pl.BlockSpec((QKV_W, TB), lambda i: (0, i)),
        ],
        out_shape=[
            jax.ShapeDtypeStruct((N_KV_ROWS, QKV_W), BF16),
            jax.ShapeDtypeStruct((QKV_W, N_KV_ROWS), BF16),
        ],
        compiler_params=_cparams(("parallel",)),
        name="mla_kv_expand",
    )(p["cache_ckv"], p["cache_krp"], ckv, krp, p["m_w_ukv_t"], p["m_gk"], p["rope_t"])


def _attn_kernel(qt_ref, k_ref, vt_ref, o_ref, s_a, s_b, *, t_k, hps, qps):
    c = (1.0 / math.sqrt(QK_HEAD_DIM)) * math.log2(math.e)
    kc = min(TKC, t_k)
    n_chunks = t_k // kc
    per_it = min(ATTN_CHUNKS_PER_ITER, n_chunks)
    n_it = n_chunks // per_it
    bufs = (s_a, s_b)

    units = [(qb, h) for qb in range(qps) for h in range(hps)]

    def rows(h):
        return slice(h * HEAD_PAD, (h + 1) * HEAD_PAD)

    def cols(qb):
        return slice(qb * TQ, (qb + 1) * TQ)

    def scores(u, off, m8):
        qb, h = units[u]
        st = _dot(k_ref[pl.ds(off, kc), rows(h)], qt_ref[rows(h), cols(qb)])
        bufs[u % 2][pl.ds(off, kc), :] = st
        return jnp.maximum(m8, jnp.max(st.reshape(kc // SUBLANE, SUBLANE, TQ), axis=0))

    def weigh(u, off, m, acc):
        pt = jnp.exp2((bufs[u % 2][pl.ds(off, kc), :] - m) * c).astype(BF16)
        return acc + _dot(vt_ref[rows(units[u][1]), pl.ds(off, kc)], pt)

    def phase(u_scores, u_weigh, m):
        def body(it, carry):
            m8, acc = carry
            for j in range(per_it):
                off = (it * per_it + j) * kc
                off = off if isinstance(off, int) else pl.multiple_of(off, kc)
                if u_scores is not None:
                    m8 = scores(u_scores, off, m8)
                if u_weigh is not None:
                    acc = weigh(u_weigh, off, m, acc)
            return m8, acc

        init = (jnp.full((SUBLANE, TQ), NEG_BIG, F32), jnp.zeros((HEAD_PAD, TQ), F32))
        return body(0, init) if n_it == 1 else lax.fori_loop(0, n_it, body, init)

    outs = []
    m8, _ = phase(0, None, None)
    for u in range(1, len(units) + 1):
        m = jnp.max(m8, axis=0, keepdims=True)
        m8, acc = phase(u if u < len(units) else None, u - 1, m)
        outs.append(acc[QK_NOPE_DIM:, :] / acc[0:1, :])
    for qb in range(qps):
        o_ref[cols(qb), :] = jnp.concatenate(outs[qb * hps:(qb + 1) * hps], axis=0).T.astype(BF16)


def _attention(qt, k, vt, *, n_batch, t_q, t_k, q_row0, kv_row0, hps, qps):
    tq = TQ * qps
    nq = t_q // tq
    q0 = q_row0 // tq
    k0 = kv_row0 // t_k
    return pl.pallas_call(
        functools.partial(_attn_kernel, t_k=t_k, hps=hps, qps=qps),
        grid=(n_batch, N_HEADS // hps, nq),
        in_specs=[
            pl.BlockSpec((hps * HEAD_PAD, tq), lambda b, h, i: (h, q0 + b * nq + i)),
            pl.BlockSpec((t_k, hps * HEAD_PAD), lambda b, h, i: (k0 + b, h)),
            pl.BlockSpec((hps * HEAD_PAD, t_k), lambda b, h, i: (h, k0 + b)),
        ],
        out_specs=pl.BlockSpec((tq, hps * V_HEAD_DIM), lambda b, h, i: (b * nq + i, h)),
        out_shape=jax.ShapeDtypeStruct((n_batch * t_q, N_HEADS * V_HEAD_DIM), BF16),
        scratch_shapes=[pltpu.VMEM((t_k, TQ), F32), pltpu.VMEM((t_k, TQ), F32)],
        compiler_params=_cparams(("parallel", "parallel", "parallel")),
        name=f"mla_attention_tk{t_k}",
    )(qt, k, vt)


def _out_proj_kernel(ap_ref, as_ref, xp_ref, xs_ref, mod_ref, wo_ref, o_ref):
    is_prompt = pl.program_id(0) < N_PROMPT // WIDE_TB
    a = jnp.where(is_prompt, ap_ref[...], as_ref[...])
    x = jnp.where(is_prompt, xp_ref[...], xs_ref[...])
    o_ref[...] = x + mod_ref[0][2:3] * _dot(a, wo_ref[...])


def _out_proj(attn_p, attn_s, x_p, x_s, p, layer):
    return pl.pallas_call(
        _out_proj_kernel,
        grid=(N_TOK // WIDE_TB,),
        in_specs=_split_rows_specs(WIDE_TB) + _split_rows_specs(WIDE_TB) + [
            _mod_spec(WIDE_TB, layer),
            _layer_spec((D_MODEL, D_MODEL), layer // 2),
        ],
        out_specs=pl.BlockSpec((WIDE_TB, D_MODEL), lambda i: (i, 0)),
        out_shape=jax.ShapeDtypeStruct((N_TOK, D_MODEL), F32),
        compiler_params=_cparams(("parallel",)),
        name="mla_out_proj",
    )(attn_p, attn_s, x_p, x_s, p["mod"], p["m_w_o"])


def _mla_layer(x_p, x_s, p, layer):
    qt, ckv, krp = _mla_proj(x_p, x_s, p, layer)
    k, vt = _kv_expand(ckv, krp, p, layer)
    a_p = _attention(qt, k, vt, n_batch=BATCH, t_q=SEQ, t_k=SEQ, q_row0=0, kv_row0=DEC_BATCH * KV_LEN,
                     hps=N_HEADS, qps=1)
    a_s = _attention(qt, k, vt, n_batch=DEC_BATCH, t_q=DEC_SEQ, t_k=KV_LEN, q_row0=N_PROMPT, kv_row0=0,
                     hps=4, qps=2)
    x = _out_proj(a_p, a_s, x_p, x_s, p, layer)
    state_ckv = ckv[:N_PROMPT].reshape(BATCH, SEQ, KV_RANK)
    state_krope = krp[:N_PROMPT, QK_NOPE_DIM:QK_HEAD_DIM].reshape(BATCH, SEQ, QK_ROPE_DIM)
    return x, state_ckv, state_krope


def _tile_plan(cnt8, plan_ref, start_ref):
    tm = float(TM)
    cnt_row = cnt8[0:1, :]
    cnt_col = cnt8.T[:, 0:1]
    tiles_row = jnp.floor((cnt_row + (tm - 1.0)) * (1.0 / tm))
    tiles_col = jnp.floor((cnt_col + (tm - 1.0)) * (1.0 / tm))
    sub = lax.broadcasted_iota(jnp.int32, (LANE, LANE), 0).astype(F32)
    lan = lax.broadcasted_iota(jnp.int32, (LANE, LANE), 1).astype(F32)
    tile_end_row = jnp.sum(jnp.where(sub <= lan, tiles_col, 0.0), axis=0, keepdims=True)
    tile_end_col = jnp.sum(jnp.where(lan <= sub, tiles_row, 0.0), axis=1, keepdims=True)
    n_used = jnp.max(tile_end_row, axis=1, keepdims=True)
    start_col = (tile_end_col - tiles_col) * tm
    end_col = start_col + cnt_col
    cand = jnp.where(jnp.logical_and(lan > sub, tiles_row > 0.0), lan, float(LANE))
    next_col = jnp.min(cand, axis=1, keepdims=True)
    next_col = jnp.where(next_col < float(LANE), next_col, -1.0)
    n_lanes = plan_ref.shape[1]
    tidx = jnp.minimum(lax.broadcasted_iota(jnp.int32, (LANE, n_lanes), 1).astype(F32), n_used - 1.0)
    esub = lax.broadcasted_iota(jnp.int32, (LANE, n_lanes), 0).astype(F32)
    te_row = jnp.sum(jnp.where(tile_end_col <= tidx, 1.0, 0.0), axis=0, keepdims=True)
    mine = esub == te_row
    end_at = jnp.sum(jnp.where(mine, end_col, 0.0), axis=0, keepdims=True)
    tv_row = jnp.clip(end_at - tidx[0:1, :] * tm, 0.0, tm)
    nx_row = jnp.sum(jnp.where(mine, next_col, 0.0), axis=0, keepdims=True)
    nu_row = jnp.broadcast_to(n_used, (1, n_lanes))
    plan_ref[...] = jnp.concatenate([te_row, tv_row, nx_row, nu_row, jnp.zeros((SUBLANE - 4, n_lanes), F32)], axis=0)
    start_ref[...] = jnp.broadcast_to(start_col, (LANE, LANE))


def _route_kernel(x_ref, mod_ref, g_ref, wr_ref, br_ref, tri_ref, h_ref, meta_ref, metat_ref, plan_ref, start_ref,
                  carry):
    i = pl.program_id(0)

    @pl.when(i == 0)
    def _():
        carry[...] = jnp.zeros_like(carry)

    x = x_ref[...]
    m = mod_ref[0]
    h = _rms(x, g_ref[...]) * (1.0 + m[4:5]) + m[3:4]
    h_ref[...] = _pack_bf16_pairs(h)
    logits = _dot(h.astype(BF16), wr_ref[...]) + br_ref[...]
    lane = lax.broadcasted_iota(jnp.int32, logits.shape, 1).astype(F32)
    work = logits
    sel = jnp.zeros(logits.shape, F32)
    hits, tops = [], []
    for k in range(TOP_K):
        mk = jnp.max(work, axis=-1, keepdims=True)
        first = jnp.min(jnp.where(work == mk, lane, float(LANE)), axis=-1, keepdims=True)
        hit = lane == first
        sel = jnp.where(hit, 1.0, sel)
        work = jnp.where(hit, -jnp.inf, work)
        hits.append((hit, first))
        tops.append(mk)
    es = [jnp.exp(t - tops[0]) for t in tops]
    denom = es[0] + es[1] + es[2] + es[3]
    pos = _dot(tri_ref[...], sel.astype(BF16)) + carry[0:1, :]
    carry[...] = carry[...] + jnp.sum(sel, axis=0, keepdims=True)

    @pl.when(i == pl.num_programs(0) - 1)
    def _():
        _tile_plan(carry[...], plan_ref, start_ref)

    meta = jnp.zeros(logits.shape, F32)
    for k in range(TOP_K):
        hit, first = hits[k]
        pk = jnp.sum(jnp.where(hit, pos, 0.0), axis=-1, keepdims=True)
        meta = jnp.where(lane == float(k), first, meta)
        meta = jnp.where(lane == float(TOP_K + k), es[k] / denom, meta)
        meta = jnp.where(lane == float(2 * TOP_K + k), pk, meta)
    meta_ref[...] = meta
    metat_ref[...] = meta.T[:2 * SUBLANE, :]


def _route(x, p, layer):
    tb = ROUTE_TB
    return pl.pallas_call(
        _route_kernel,
        grid=(N_TOK // tb,),
        in_specs=[
            pl.BlockSpec((tb, D_MODEL), lambda i: (i, 0)),
            _mod_spec(tb, layer),
            _layer_spec((1, D_MODEL), layer),
            _layer_spec((D_MODEL, LANE), layer),
            _layer_spec((1, LANE), layer),
            _const_spec((tb, tb)),
        ],
        out_specs=[
            pl.BlockSpec((tb, D_MODEL // 2), lambda i: (i, 0)),
            pl.BlockSpec((tb, LANE), lambda i: (i, 0)),
            pl.BlockSpec((2 * SUBLANE, tb), lambda i: (0, i)),
            _const_spec((SUBLANE, PLAN_LANES)),
            _const_spec((LANE, LANE)),
        ],
        out_shape=[
            jax.ShapeDtypeStruct((N_TOK, D_MODEL // 2), jnp.uint32),
            jax.ShapeDtypeStruct((N_TOK, LANE), F32),
            jax.ShapeDtypeStruct((2 * SUBLANE, N_TOK), F32),
            jax.ShapeDtypeStruct((SUBLANE, PLAN_LANES), F32),
            jax.ShapeDtypeStruct((LANE, LANE), F32),
        ],
        scratch_shapes=[pltpu.VMEM((SUBLANE, LANE), F32)],
        compiler_params=_cparams(("arbitrary",)),
        name="moe_route",
    )(x, p["mod"], p["norm_ffn_g"], p["e_w_router"], p["e_b_router"], p["tri"])


def _slots_kernel(start_ref, metat_ref, dest_ref):
    start_col = start_ref[:, 0:1]
    esub = lax.broadcasted_iota(jnp.int32, (LANE, SLOT_TB), 0).astype(F32)
    rows = []
    for k in range(TOP_K):
        e = metat_ref[k:k + 1, :]
        first = jnp.sum(jnp.where(esub == e, start_col, 0.0), axis=0, keepdims=True)
        rows.append(first + metat_ref[2 * TOP_K + k:2 * TOP_K + k + 1, :])
    dest_ref[...] = jnp.concatenate(rows, axis=0).astype(jnp.int32)


def _slots(start, meta_t):
    return pl.pallas_call(
        _slots_kernel,
        grid=(N_TOK // SLOT_TB,),
        in_specs=[
            _const_spec((LANE, LANE)),
            pl.BlockSpec((2 * SUBLANE, SLOT_TB), lambda i: (0, i)),
        ],
        out_specs=pl.BlockSpec((TOP_K, SLOT_TB), lambda i: (0, i)),
        out_shape=jax.ShapeDtypeStruct((TOP_K, N_TOK), jnp.int32),
        compiler_params=_cparams(("parallel",)),
        name="moe_slots",
    )(start, meta_t)


def _sc_gather(table, idx, ch):
    b, w = idx.shape[0], table.shape[1]
    per_w = b // SC_WORKERS
    n_ch = per_w // ch
    assert per_w * SC_WORKERS == b and n_ch * ch == per_w and n_ch % 2 == 0
    mesh = plsc.VectorSubcoreMesh(core_axis_name="c", subcore_axis_name="s")

    @functools.partial(
        pl.kernel, mesh=mesh,
        out_type=jax.ShapeDtypeStruct((b, w), table.dtype),
        scratch_types=[
            pltpu.VMEM((n_ch, ch), jnp.int32),
            pltpu.VMEM((ch, w), table.dtype),
            pltpu.VMEM((ch, w), table.dtype),
            pltpu.SemaphoreType.DMA, pltpu.SemaphoreType.DMA,
            pltpu.SemaphoreType.DMA, pltpu.SemaphoreType.DMA,
        ],
        name="sc_row_gather",
    )
    def gather_rows(table_hbm, idx_hbm, out_hbm, idx_v, buf0, buf1, g0, g1, s0, s1):
        wid = lax.axis_index("s") * SC_CORES + lax.axis_index("c")
        base = wid * per_w
        pltpu.sync_copy(idx_hbm.at[wid], idx_v)

        def gather(j, buf, sem):
            return pltpu.make_async_copy(table_hbm.at[idx_v.at[j]], buf, sem)

        def store(j, buf, sem):
            return pltpu.make_async_copy(buf, out_hbm.at[pl.ds(base + j * ch, ch)], sem)

        gather(0, buf0, g0).start()

        @pl.loop(0, n_ch, step=2)
        def _(j):
            @pl.when(j > 0)
            def _():
                store(j - 1, buf1, s1).wait()

            gather(j + 1, buf1, g1).start()
            gather(j, buf0, g0).wait()
            store(j, buf0, s0).start()
            gather(j + 1, buf1, g1).wait()
            store(j + 1, buf1, s1).start()
            store(j, buf0, s0).wait()

            @pl.when(j + 2 < n_ch)
            def _():
                gather(j + 2, buf0, g0).start()

        store(n_ch - 1, buf1, s1).wait()

    return gather_rows(table, idx.reshape(SC_WORKERS, n_ch, ch))


def _sc_dispatch(rows, dest_t, ch):
    n, w = rows.shape
    per_w = n // SC_WORKERS
    n_ch = per_w // ch
    assert per_w * SC_WORKERS == n and n_ch * ch == per_w and n_ch % 2 == 0
    mesh = plsc.VectorSubcoreMesh(core_axis_name="c", subcore_axis_name="s")
    idx = dest_t.reshape(TOP_K, SC_WORKERS, n_ch, ch)

    @functools.partial(
        pl.kernel, mesh=mesh,
        out_type=jax.ShapeDtypeStruct((N_SLOTS, w), rows.dtype),
        scratch_types=[
            pltpu.VMEM((TOP_K * n_ch, ch), jnp.int32),
            pltpu.VMEM((ch, w), rows.dtype),
            pltpu.VMEM((ch, w), rows.dtype),
            pltpu.SemaphoreType.DMA, pltpu.SemaphoreType.DMA,
            pltpu.SemaphoreType.DMA, pltpu.SemaphoreType.DMA,
        ],
        name="sc_row_dispatch",
    )
    def dispatch_rows(rows_hbm, idx_hbm, out_hbm, idx_v, buf0, buf1, l0, l1, s0, s1):
        wid = lax.axis_index("s") * SC_CORES + lax.axis_index("c")
        base = wid * per_w
        for k in range(TOP_K):
            pltpu.sync_copy(idx_hbm.at[k, wid], idx_v.at[pl.ds(k * n_ch, n_ch)])

        def load(j, buf, sem):
            return pltpu.make_async_copy(rows_hbm.at[pl.ds(base + j * ch, ch)], buf, sem)

        def scatter(j, k, buf, sem):
            return pltpu.make_async_copy(buf, out_hbm.at[idx_v.at[k * n_ch + j]], sem)

        load(0, buf0, l0).start()

        @pl.loop(0, n_ch, step=2)
        def _(j):
            load(j + 1, buf1, l1).start()
            load(j, buf0, l0).wait()
            for k in range(TOP_K):
                scatter(j, k, buf0, s0).start()
            load(j + 1, buf1, l1).wait()
            for k in range(TOP_K):
                scatter(j + 1, k, buf1, s1).start()
            for k in range(TOP_K):
                scatter(j, k, buf0, s0).wait()

            @pl.when(j + 2 < n_ch)
            def _():
                load(j + 2, buf0, l0).start()

            for k in range(TOP_K):
                scatter(j + 1, k, buf1, s1).wait()

    return dispatch_rows(rows, idx)


def _deinterleave_matrix():
    src = jnp.arange(2 * LANE)[:, None]
    dst = jnp.arange(2 * LANE)[None, :]
    want = jnp.where(dst < LANE, 2 * dst, 2 * (dst - LANE) + 1)
    return (src == want).astype(BF16)


def _expert_kernel(te_ref, nu_ref, tv_ref, nx_ref, x_ref, wgu_hbm, bgu_ref, wd_hbm, bd_ref, perm_ref, o_ref,
                   wgu_st, wd_st, wgu_bf, wd_bf, sems, *, layer):
    i = pl.program_id(0)
    prev = te_ref[jnp.maximum(i - 1, 0)]
    fresh = jnp.logical_or(i == 0, te_ref[i] != prev)

    def fetch(e):
        return (pltpu.make_async_copy(wgu_hbm.at[layer, e], wgu_st, sems.at[0]),
                pltpu.make_async_copy(wd_hbm.at[layer, e], wd_st, sems.at[1]))

    @pl.when(i == 0)
    def _():
        for cp in fetch(te_ref[0]):
            cp.start()

    @pl.when(jnp.logical_and(fresh, i < nu_ref[0]))
    def _():
        for cp in fetch(te_ref[i]):
            cp.wait()
        for b in range(2 * D_FF // (2 * LANE)):
            sl = slice(b * 2 * LANE, (b + 1) * 2 * LANE)
            wgu_bf[:, sl] = _dot(wgu_st[:, sl].astype(BF16), perm_ref[...]).astype(BF16)
        wd_bf[...] = wd_st[...].astype(BF16)

        @pl.when(nx_ref[i] >= 0)
        def _():
            for cp in fetch(nx_ref[i]):
                cp.start()

    @pl.when(i < nu_ref[0])
    def _():
        row = lax.broadcasted_iota(jnp.int32, (TM, D_MODEL // 2), 0)
        w = jnp.where(row < tv_ref[i], x_ref[...], jnp.uint32(0))
        x = _unpack_bf16_pairs(w).astype(BF16)
        gu = _dot(x, wgu_bf[...]) + bgu_ref[...]
        acts = []
        for b in range(D_FF // LANE):
            glu = jnp.minimum(gu[:, b * 2 * LANE:b * 2 * LANE + LANE], SWIGLU_LIMIT)
            lin = jnp.clip(gu[:, b * 2 * LANE + LANE:(b + 1) * 2 * LANE], -SWIGLU_LIMIT, SWIGLU_LIMIT)
            acts.append((glu * jax.nn.sigmoid(SWIGLU_ALPHA * glu) * (lin + 1.0)).astype(BF16))
        act = jnp.concatenate(acts, axis=1)
        o_ref[...] = _pack_bf16_pairs(_dot(act, wd_bf[...]) + bd_ref[...])


def _experts(buf, tile_expert, n_used, tile_valid, tile_next, p, layer):
    def row_idx(i, te, nu, tv, nx):
        return (jnp.minimum(i, nu[0] - 1), 0)

    def b_idx(i, te, nu, tv, nx):
        return (layer, te[i], 0, 0)

    grid_spec = pltpu.PrefetchScalarGridSpec(
        num_scalar_prefetch=4,
        grid=(N_TILES,),
        in_specs=[
            pl.BlockSpec((TM, D_MODEL // 2), row_idx),
            pl.BlockSpec(memory_space=pl.ANY),
            pl.BlockSpec((None, None, 1, 2 * D_FF), b_idx),
            pl.BlockSpec(memory_space=pl.ANY),
            pl.BlockSpec((None, None, 1, D_MODEL), b_idx),
            _const_spec((2 * LANE, 2 * LANE)),
        ],
        out_specs=pl.BlockSpec((TM, D_MODEL // 2), row_idx),
        scratch_shapes=[
            pltpu.VMEM((D_MODEL, 2 * D_FF), F32),
            pltpu.VMEM((D_FF, D_MODEL), F32),
            pltpu.VMEM((D_MODEL, 2 * D_FF), BF16),
            pltpu.VMEM((D_FF, D_MODEL), BF16),
            pltpu.SemaphoreType.DMA((2,)),
        ],
    )
    return pl.pallas_call(
        functools.partial(_expert_kernel, layer=layer),
        grid_spec=grid_spec,
        out_shape=jax.ShapeDtypeStruct((N_SLOTS, D_MODEL // 2), jnp.uint32),
        compiler_params=_cparams(("arbitrary",)),
        name="moe_experts",
    )(tile_expert, n_used, tile_valid, tile_next, buf, p["e_w_gu"], p["e_b_gu"], p["e_w_down"], p["e_b_down"],
      p["deinterleave"])


def _pack_bf16_pairs(v):
    half = v.shape[1] // 2
    bits = pltpu.bitcast(v.astype(BF16).astype(F32), jnp.uint32)
    return (bits[:, half:] & jnp.uint32(0xFFFF0000)) | (bits[:, :half] >> 16)


def _unpack_bf16_pairs(w):
    return jnp.concatenate([pltpu.bitcast(w << 16, F32), pltpu.bitcast(w & jnp.uint32(0xFFFF0000), F32)],
                           axis=1)


def _combine_kernel(x_ref, mod_ref, y_ref, w_ref, o_ref):
    w = w_ref[:, TOP_K:2 * TOP_K]
    y = _unpack_bf16_pairs(y_ref[0]) * w[:, 0:1]
    for k in range(1, TOP_K):
        y = y + _unpack_bf16_pairs(y_ref[k]) * w[:, k:k + 1]
    o_ref[...] = x_ref[...] + mod_ref[0][5:6] * y


def _combine(x, y4, meta, p, layer, part):
    n_rows = N_TOK // MOE_PARTS
    tb = WIDE_TB
    first = part * n_rows // tb
    return pl.pallas_call(
        _combine_kernel,
        grid=(n_rows // tb,),
        in_specs=[
            pl.BlockSpec((tb, D_MODEL), lambda i: (i + first, 0)),
            _mod_spec(tb, layer, first),
            pl.BlockSpec((TOP_K, tb, D_MODEL // 2), lambda i: (0, i, 0)),
            pl.BlockSpec((tb, LANE), lambda i: (i + first, 0)),
        ],
        out_specs=pl.BlockSpec((tb, D_MODEL), lambda i: (i, 0)),
        out_shape=jax.ShapeDtypeStruct((n_rows, D_MODEL), F32),
        compiler_params=_cparams(("parallel",)),
        name="moe_combine",
    )(x, p["mod"], y4, meta)


def _moe_layer(x, p, layer):
    hp, meta, meta_t, plan, start = _route(x, p, layer)
    plan = plan[:4, :N_TILES].astype(jnp.int32)
    tile_expert, tile_valid, tile_next, n_used = plan[0], plan[1], plan[2], plan[3, :1]
    dest_t = _slots(start, meta_t)
    buf = _sc_dispatch(hp, dest_t, 64)
    yb = _experts(buf, tile_expert, n_used, tile_valid, tile_next, p, layer)
    n_rows = N_TOK // MOE_PARTS
    outs = []
    for part in range(MOE_PARTS):
        idx = dest_t[:, part * n_rows:(part + 1) * n_rows].reshape(-1)
        y4 = _sc_gather(yb, idx, 64).reshape(TOP_K, n_rows, D_MODEL // 2)
        outs.append(_combine(x, y4, meta, p, layer, part))
    return outs


def _prepare(c, cache_ckv, cache_krope, c_ctx, norm_mix_g, norm_ffn_g, w_mod, b_mod, g_w_in, g_b_in, g_norm_v,
             g_w_s, g_b_s, g_w_out, m_w_dq, m_norm_q, m_w_uq, m_w_dkv, m_norm_kv, m_w_ukv, m_qk_norm_q,
             m_qk_norm_k, m_w_o, e_w_router, e_b_router, e_w_gu, e_b_gu, e_w_down, e_b_down):
    n_mla = m_w_dq.shape[0]
    cond = jnp.concatenate([c_ctx[None, :], c, jnp.zeros((SUBLANE - N_COND, D_MODEL), F32)], axis=0)
    wdkv = jnp.concatenate([m_w_dkv[..., :KV_RANK], jnp.zeros((n_mla, D_MODEL, QK_NOPE_DIM), F32),
                            m_w_dkv[..., KV_RANK:], jnp.zeros((n_mla, D_MODEL, HEAD_PAD - QK_HEAD_DIM), F32)],
                           axis=-1)
    w_uq = jnp.pad(m_w_uq.reshape(n_mla, Q_RANK, N_HEADS, QK_HEAD_DIM),
                   ((0, 0), (0, 0), (0, 0), (0, HEAD_PAD - QK_HEAD_DIM))).reshape(n_mla, Q_RANK, QKV_W)

    def gain_col(g):
        return jnp.pad(g, ((0, 0), (0, HEAD_PAD - QK_HEAD_DIM)))[:, :, None]

    return {
        "mod": _modulation(cond, w_mod, b_mod),
        "rope_t": _rope_tables(),
        "norm_mix_g": norm_mix_g[:, None, :],
        "norm_ffn_g": norm_ffn_g[:, None, :],
        "g_w_in": g_w_in.astype(BF16),
        "g_b_in": g_b_in[:, None, :],
        "g_norm_v": g_norm_v[:, None, :],
        "g_w_s": g_w_s.astype(BF16),
        "g_b_st": jnp.swapaxes(g_b_s, 1, 2),
        "g_w_out": g_w_out.astype(BF16),
        "m_w_dq": m_w_dq.astype(BF16),
        "m_norm_q": m_norm_q[:, None, :],
        "m_w_uq_t": jnp.swapaxes(w_uq, 1, 2).astype(BF16),
        "m_gq": gain_col(m_qk_norm_q),
        "m_w_dkv": wdkv.astype(BF16),
        "m_norm_kv": m_norm_kv[:, None, :],
        "m_w_ukv_t": jnp.swapaxes(m_w_ukv, 1, 2).astype(BF16),
        "m_gk": gain_col(m_qk_norm_k),
        "m_w_o": m_w_o.astype(BF16),
        "cache_ckv": cache_ckv,
        "cache_krp": jnp.pad(cache_krope, ((0, 0), (0, 0), (0, 0), (QK_NOPE_DIM, HEAD_PAD - QK_HEAD_DIM))),
        "e_w_router": jnp.pad(e_w_router, ((0, 0), (0, 0), (0, LANE - N_EXPERTS))).astype(BF16),
        "e_b_router": jnp.pad(e_b_router, ((0, 0), (0, LANE - N_EXPERTS)), constant_values=NEG_BIG)[:, None, :],
        "tri": jnp.tri(ROUTE_TB, ROUTE_TB, -1, dtype=BF16),
        "e_w_gu": e_w_gu,
        "e_b_gu": e_b_gu.reshape(DEPTH, N_EXPERTS, D_FF // LANE, LANE, 2).swapaxes(3, 4).reshape(
            DEPTH, N_EXPERTS, 1, 2 * D_FF),
        "e_w_down": e_w_down,
        "e_b_down": e_b_down[:, :, None, :],
        "deinterleave": _deinterleave_matrix(),
    }


def kernel(x_prompt, x_sample, c, cache_ckv, cache_krope, c_ctx, norm_mix_g, norm_ffn_g, w_mod, b_mod,
           g_w_in, g_b_in, g_norm_v, g_w_s, g_b_s, g_w_out, m_w_dq, m_norm_q, m_w_uq, m_w_dkv,
           m_norm_kv, m_w_ukv, m_qk_norm_q, m_qk_norm_k, m_w_o, e_w_router, e_b_router, e_w_gu,
           e_b_gu, e_w_down, e_b_down):
    p = _prepare(c, cache_ckv, cache_krope, c_ctx, norm_mix_g, norm_ffn_g, w_mod, b_mod, g_w_in, g_b_in,
                 g_norm_v, g_w_s, g_b_s, g_w_out, m_w_dq, m_norm_q, m_w_uq, m_w_dkv, m_norm_kv, m_w_ukv,
                 m_qk_norm_q, m_qk_norm_k, m_w_o, e_w_router, e_b_router, e_w_gu, e_b_gu, e_w_down, e_b_down)
    assert MOE_PARTS == 2 and N_PROMPT == N_SAMPLE
    x_p, x_s = x_prompt.reshape(N_PROMPT, D_MODEL), x_sample.reshape(N_SAMPLE, D_MODEL)
    ckv_states, krope_states = [], []
    for layer in range(DEPTH):
        if layer % 2 == 0:
            x = _gmlp_layer(x_p, x_s, p, layer)
        else:
            x, s_ckv, s_krope = _mla_layer(x_p, x_s, p, layer)
            ckv_states.append(s_ckv)
            krope_states.append(s_krope)
        x_p, x_s = _moe_layer(x, p, layer)
    y_prompt = x_p.reshape(BATCH, SEQ, D_MODEL)
    y_sample = x_s.reshape(DEC_BATCH, DEC_SEQ, D_MODEL)
    return (y_prompt, y_sample, jnp.stack(ckv_states, axis=1), jnp.stack(krope_states, axis=1))
```

```python
import functools
import math

import jax
import jax.numpy as jnp
from jax import lax
from jax.experimental import pallas as pl
from jax.experimental.pallas import tpu as pltpu
from jax.experimental.pallas import tpu_sc as plsc

F32 = jnp.float32
BF16 = jnp.bfloat16

D_MODEL = 1024
BATCH = 32
SEQ = 256
DEPTH = 4
DEC_BATCH = 2
DEC_SEQ = 4096
PAST_LEN = 512
GRID_W = 64
RMS_EPS = 1e-6
GMLP_WIDTH = 2 * D_MODEL
GMLP_GROUPS = 8
GROUP_W = GMLP_WIDTH // GMLP_GROUPS
CHUNK = 128
N_HEADS = 16
QK_NOPE_DIM = 64
QK_ROPE_DIM = 32
QK_HEAD_DIM = QK_NOPE_DIM + QK_ROPE_DIM
V_HEAD_DIM = 64
Q_RANK = 256
KV_RANK = 128
ROPE_THETA = 10000.0
N_EXPERTS = 32
TOP_K = 4
D_FF = D_MODEL
SWIGLU_LIMIT = 7.0
SWIGLU_ALPHA = 1.702

N_PROMPT = BATCH * SEQ
N_SAMPLE = DEC_BATCH * DEC_SEQ
N_TOK = N_PROMPT + N_SAMPLE
N_COND = 1 + DEC_BATCH
KV_LEN = PAST_LEN + DEC_SEQ
N_KV_ROWS = DEC_BATCH * KV_LEN + N_PROMPT

LANE = 128
SUBLANE = 8
HEAD_PAD = LANE
QKV_W = N_HEADS * HEAD_PAD
VMEM_LIMIT = 56 * 1024 * 1024

TB = 512
WIDE_TB = 512
TQ = 256
TKC = 256
ATTN_CHUNKS_PER_ITER = 9
TM = 512
N_TILES = N_TOK * TOP_K // TM + N_EXPERTS
N_SLOTS = N_TILES * TM
PLAN_LANES = -(-N_TILES // LANE) * LANE
SLOT_TB = 2048
ROUTE_TB = 1024
MOE_PARTS = 2
SC_CORES = 2
SC_WORKERS = SC_CORES * 16
ROPE_BLOCKS = DEC_SEQ // TB
NEG_BIG = -1e30


def _cparams(sem):
    return pltpu.CompilerParams(dimension_semantics=sem, vmem_limit_bytes=VMEM_LIMIT)


def _cond_of_block(i, tb):
    n_p = N_PROMPT // tb
    per = DEC_SEQ // tb
    return jnp.where(i < n_p, 0, 1 + (i - n_p) // per)


def _rms(x, g, n=None):
    n = x.shape[-1] if n is None else n
    ss = jnp.sum(x * x, axis=-1, keepdims=True) * (1.0 / n)
    return x * lax.rsqrt(ss + RMS_EPS) * g


def _dot(a, b):
    return jnp.dot(a, b, preferred_element_type=F32)


def _mod_kernel(c_ref, w_ref, b_ref, o_ref):
    c = c_ref[...]
    s = c * jax.nn.sigmoid(c)
    o_ref[0] = _dot(s.astype(BF16), w_ref[0].astype(BF16)) + b_ref[0]


def _modulation(cond, w_mod, b_mod):
    tn = 1536
    out = pl.pallas_call(
        _mod_kernel,
        grid=(DEPTH, 6 * D_MODEL // tn),
        in_specs=[
            pl.BlockSpec((SUBLANE, D_MODEL), lambda l, j: (0, 0)),
            pl.BlockSpec((1, D_MODEL, tn), lambda l, j: (l, 0, j)),
            pl.BlockSpec((1, 1, tn), lambda l, j: (l, 0, j)),
        ],
        out_specs=pl.BlockSpec((1, SUBLANE, tn), lambda l, j: (l, 0, j)),
        out_shape=jax.ShapeDtypeStruct((DEPTH, SUBLANE, 6 * D_MODEL), F32),
        compiler_params=_cparams(("parallel", "parallel")),
        name="adaln_mod",
    )(cond, w_mod, b_mod.reshape(DEPTH, 1, 6 * D_MODEL))
    m = out[:, :N_COND].reshape(DEPTH, N_COND, 6, D_MODEL)
    return jnp.pad(m, ((0, 0), (0, 0), (0, SUBLANE - 6), (0, 0)))


def _mod_spec(tb, layer, first_block=0):
    return pl.BlockSpec((None, 1, SUBLANE, D_MODEL),
                        lambda i: (layer, _cond_of_block(i + first_block, tb), 0, 0))


def _const_spec(shape):
    nd = len(shape)
    return pl.BlockSpec(shape, lambda *_: (0,) * nd)


def _layer_spec(shape, j):
    nd = len(shape)
    return pl.BlockSpec((None,) + tuple(shape), lambda *_: (j,) + (0,) * nd)


def _gmlp_kernel(xp_ref, xs_ref, mod_ref, g_ref, win_ref, bin_ref, gv_ref, ws_ref, bst_ref, wout_ref, o_ref):
    x = jnp.where(pl.program_id(0) < N_PROMPT // WIDE_TB, xp_ref[...], xs_ref[...])
    m = mod_ref[0]
    h = _rms(x, g_ref[...]) * (1.0 + m[1:2]) + m[0:1]
    hb = h.astype(BF16)
    zv = jax.nn.gelu(_dot(hb, win_ref[:, GMLP_WIDTH:]) + bin_ref[:, GMLP_WIDTH:], approximate=True)
    vn = _rms(zv, gv_ref[...]).astype(BF16)
    rows = []
    for c in range(WIDE_TB // CHUNK):
        cols = []
        for g in range(GMLP_GROUPS):
            blk = vn[c * CHUNK:(c + 1) * CHUNK, g * GROUP_W:(g + 1) * GROUP_W]
            cols.append(_dot(ws_ref[g], blk) + bst_ref[:, g:g + 1])
        rows.append(jnp.concatenate(cols, axis=1))
    vm = jnp.concatenate(rows, axis=0)
    u = jax.nn.gelu(_dot(hb, win_ref[:, :GMLP_WIDTH]) + bin_ref[:, :GMLP_WIDTH], approximate=True)
    d = _dot((u * vm).astype(BF16), wout_ref[...])
    o_ref[...] = x + m[2:3] * d


def _split_rows_specs(tb=TB):
    n_pb = N_PROMPT // tb
    return [pl.BlockSpec((tb, D_MODEL), lambda i: (jnp.minimum(i, n_pb - 1), 0)),
            pl.BlockSpec((tb, D_MODEL), lambda i: (jnp.maximum(i - n_pb, 0), 0))]


def _gmlp_layer(x_p, x_s, p, layer):
    j = layer // 2
    return pl.pallas_call(
        _gmlp_kernel,
        grid=(N_TOK // WIDE_TB,),
        in_specs=_split_rows_specs(WIDE_TB) + [
            _mod_spec(WIDE_TB, layer),
            _layer_spec((1, D_MODEL), layer),
            _layer_spec((D_MODEL, 2 * GMLP_WIDTH), j),
            _layer_spec((1, 2 * GMLP_WIDTH), j),
            _layer_spec((1, GMLP_WIDTH), j),
            _layer_spec((GMLP_GROUPS, CHUNK, CHUNK), j),
            _layer_spec((CHUNK, GMLP_GROUPS), j),
            _layer_spec((GMLP_WIDTH, D_MODEL), j),
        ],
        out_specs=pl.BlockSpec((WIDE_TB, D_MODEL), lambda i: (i, 0)),
        out_shape=jax.ShapeDtypeStruct((N_TOK, D_MODEL), F32),
        compiler_params=_cparams(("parallel",)),
        name="gmlp_mixer",
    )(x_p, x_s, p["mod"], p["norm_mix_g"], p["g_w_in"], p["g_b_in"], p["g_norm_v"], p["g_w_s"], p["g_b_st"],
      p["g_w_out"])


def _rope_tables():
    t = jnp.arange(DEC_SEQ)
    row_id = (t // GRID_W).astype(F32)
    col_id = (t % GRID_W).astype(F32)
    axis_dim = QK_ROPE_DIM // 2
    inv_freq = ROPE_THETA ** (-jnp.arange(0, axis_dim, 2, dtype=F32) / axis_dim)
    ang = jnp.stack([row_id[:, None] * inv_freq, col_id[:, None] * inv_freq], axis=1)
    cos, sin = jnp.cos(ang), jnp.sin(ang)
    zeros = jnp.zeros_like(sin)
    cos_l = jnp.concatenate([cos, cos], axis=-1).reshape(DEC_SEQ, QK_ROPE_DIM)
    s1_l = jnp.concatenate([-sin, zeros], axis=-1).reshape(DEC_SEQ, QK_ROPE_DIM)
    s2_l = jnp.concatenate([zeros, sin], axis=-1).reshape(DEC_SEQ, QK_ROPE_DIM)

    def widen(rope_part, nope_fill):
        left = jnp.full((DEC_SEQ, QK_NOPE_DIM), nope_fill, F32)
        right = jnp.zeros((DEC_SEQ, HEAD_PAD - QK_HEAD_DIM), F32)
        return jnp.concatenate([left, rope_part, right], axis=-1)

    pos = jnp.stack([widen(cos_l, 1.0), widen(s1_l, 0.0), widen(s2_l, 0.0)])
    ident_c = jnp.concatenate([jnp.ones((TB, QK_HEAD_DIM), F32),
                               jnp.zeros((TB, HEAD_PAD - QK_HEAD_DIM), F32)], axis=-1)
    ident = jnp.stack([ident_c, jnp.zeros_like(ident_c), jnp.zeros_like(ident_c)])
    return jnp.swapaxes(jnp.concatenate([pos, ident], axis=1), 1, 2)


def _dot_nt(a, b):
    return lax.dot_general(a, b, (((1,), (1,)), ((), ())), preferred_element_type=F32)


def _shift_rows(x, n):
    n = n % x.shape[0]
    return jnp.concatenate([x[n:], x[:n]], axis=0)


def _rope_rows(xn, tabt_ref):
    half = QK_ROPE_DIM // 4
    return xn * tabt_ref[0] + _shift_rows(xn, half) * tabt_ref[1] + _shift_rows(xn, -half) * tabt_ref[2]


def _mla_proj_kernel(xp_ref, xs_ref, mod_ref, g_ref, wdq_ref, nq_ref, wuqt_ref, gq_ref, wdkv_ref, nkv_ref,
                     tabt_ref, qt_ref, ckv_ref, krp_ref):
    x = jnp.where(pl.program_id(0) < N_PROMPT // TB, xp_ref[...], xs_ref[...])
    m = mod_ref[0]
    h = _rms(x, g_ref[...]) * (1.0 + m[1:2]) + m[0:1]
    hb = h.astype(BF16)
    cq = _rms(_dot(hb, wdq_ref[...]), nq_ref[...])
    qt = _dot_nt(wuqt_ref[...], cq.astype(BF16))
    gq = gq_ref[...]
    for hd in range(N_HEADS):
        qh = qt[hd * HEAD_PAD:(hd + 1) * HEAD_PAD, :]
        ss = jnp.sum(qh * qh, axis=0, keepdims=True) * (1.0 / QK_HEAD_DIM)
        qn = qh * lax.rsqrt(ss + RMS_EPS) * gq
        qt_ref[hd * HEAD_PAD:(hd + 1) * HEAD_PAD, :] = _rope_rows(qn, tabt_ref).astype(BF16)
    kva = _dot(hb, wdkv_ref[...])
    ckv_ref[...] = _rms(kva[:, :KV_RANK], nkv_ref[...])
    krp_ref[...] = kva[:, KV_RANK:]


def _mla_proj(x_p, x_s, p, layer):
    j = layer // 2
    n_pb = N_PROMPT // TB

    def tab_idx(i):
        return (0, 0, jnp.where(i < n_pb, ROPE_BLOCKS, (i - n_pb) % ROPE_BLOCKS))

    return pl.pallas_call(
        _mla_proj_kernel,
        grid=(N_TOK // TB,),
        in_specs=_split_rows_specs() + [
            _mod_spec(TB, layer),
            _layer_spec((1, D_MODEL), layer),
            _layer_spec((D_MODEL, Q_RANK), j),
            _layer_spec((1, Q_RANK), j),
            _layer_spec((QKV_W, Q_RANK), j),
            _layer_spec((HEAD_PAD, 1), j),
            _layer_spec((D_MODEL, 2 * LANE), j),
            _layer_spec((1, KV_RANK), j),
            pl.BlockSpec((3, HEAD_PAD, TB), tab_idx),
        ],
        out_specs=[
            pl.BlockSpec((QKV_W, TB), lambda i: (0, i)),
            pl.BlockSpec((TB, KV_RANK), lambda i: (i, 0)),
            pl.BlockSpec((TB, HEAD_PAD), lambda i: (i, 0)),
        ],
        out_shape=[
            jax.ShapeDtypeStruct((QKV_W, N_TOK), BF16),
            jax.ShapeDtypeStruct((N_TOK, KV_RANK), F32),
            jax.ShapeDtypeStruct((N_TOK, HEAD_PAD), F32),
        ],
        compiler_params=_cparams(("parallel",)),
        name="mla_proj",
    )(x_p, x_s, p["mod"], p["norm_mix_g"], p["m_w_dq"], p["m_norm_q"], p["m_w_uq_t"], p["m_gq"], p["m_w_dkv"],
      p["m_norm_kv"], p["rope_t"])


def _kv_block_source(i):
    n_sb = DEC_BATCH * KV_LEN // TB
    per = KV_LEN // TB
    b = jnp.minimum(i // per, DEC_BATCH - 1)
    jj = i % per
    return jnp.logical_and(i < n_sb, jj < PAST_LEN // TB), b, jj


def _kv_expand_kernel(cckv_ref, ckrp_ref, ckv_ref, krp_ref, wukvt_ref, gk_ref, tabt_ref, k_ref, vt_ref):
    from_cache, _, _ = _kv_block_source(pl.program_id(0))
    ckv = jnp.where(from_cache, cckv_ref[...], ckv_ref[...])
    krp = jnp.where(from_cache, ckrp_ref[...], krp_ref[...])
    kvt = _dot_nt(wukvt_ref[...], ckv.astype(BF16))
    krpt = krp.T
    gk = gk_ref[...]
    row = lax.broadcasted_iota(jnp.int32, (HEAD_PAD, TB), 0)
    is_nope = row < QK_NOPE_DIM
    ones_row = jnp.where(row == 0, 1.0, 0.0)
    for hd in range(N_HEADS):
        blk = kvt[hd * HEAD_PAD:(hd + 1) * HEAD_PAD, :]
        kraw = jnp.where(is_nope, blk, krpt)
        ss = jnp.sum(kraw * kraw, axis=0, keepdims=True) * (1.0 / QK_HEAD_DIM)
        kn = kraw * lax.rsqrt(ss + RMS_EPS) * gk
        k_ref[:, hd * HEAD_PAD:(hd + 1) * HEAD_PAD] = _rope_rows(kn, tabt_ref).T.astype(BF16)
        vt_ref[hd * HEAD_PAD:(hd + 1) * HEAD_PAD, :] = jnp.where(is_nope, ones_row, blk).astype(BF16)


def _kv_expand(ckv, krp, p, layer):
    j = layer // 2
    n_sb = DEC_BATCH * KV_LEN // TB
    n_cache = PAST_LEN // TB
    n_pb = N_PROMPT // TB
    lat_blocks = DEC_SEQ // TB

    def tab_idx(i):
        from_cache, _, jj = _kv_block_source(i)
        return (0, 0, jnp.where((i >= n_sb) | from_cache, ROPE_BLOCKS, jj - n_cache))

    def cache_idx(i):
        _, b, jj = _kv_block_source(i)
        return (b, j, jnp.minimum(jj, n_cache - 1), 0)

    def tok_idx(i):
        _, b, jj = _kv_block_source(i)
        latent = n_pb + b * lat_blocks + jnp.maximum(jj - n_cache, 0)
        return (jnp.where(i < n_sb, latent, i - n_sb), 0)

    return pl.pallas_call(
        _kv_expand_kernel,
        grid=(N_KV_ROWS // TB,),
        in_specs=[
            pl.BlockSpec((None, None, TB, KV_RANK), cache_idx),
            pl.BlockSpec((None, None, TB, HEAD_PAD), cache_idx),
            pl.BlockSpec((TB, KV_RANK), tok_idx),
            pl.BlockSpec((TB, HEAD_PAD), tok_idx),
            _layer_spec((QKV_W, KV_RANK), j),
            _layer_spec((HEAD_PAD, 1), j),
            pl.BlockSpec((3, HEAD_PAD, TB), tab_idx),
        ],
        out_specs=[
            pl.BlockSpec((TB, QKV_W), lambda i: (i, 0)),
            pl.BlockSpec((QKV_W, TB), lambda i: (0, i)),
        ],
        out_shape=[
            jax.ShapeDtypeStruct((N_KV_ROWS, QKV_W), BF16),
            jax.ShapeDtypeStruct((QKV_W, N_KV_ROWS), BF16),
        ],
        compiler_params=_cparams(("parallel",)),
        name="mla_kv_expand",
    )(p["cache_ckv"], p["cache_krp"], ckv, krp, p["m_w_ukv_t"], p["m_gk"], p["rope_t"])


def _attn_kernel(qt_ref, k_ref, vt_ref, o_ref, s_a, s_b, *, t_k, hps, qps):
    c = (1.0 / math.sqrt(QK_HEAD_DIM)) * math.log2(math.e)
    kc = min(TKC, t_k)
    n_chunks = t_k // kc
    per_it = min(ATTN_CHUNKS_PER_ITER, n_chunks)
    n_it = n_chunks // per_it
    bufs = (s_a, s_b)

    units = [(qb, h) for qb in range(qps) for h in range(hps)]

    def rows(h):
        return slice(h * HEAD_PAD, (h + 1) * HEAD_PAD)

    def cols(qb):
        return slice(qb * TQ, (qb + 1) * TQ)

    def scores(u, off, m8):
        qb, h = units[u]
        st = _dot(k_ref[pl.ds(off, kc), rows(h)], qt_ref[rows(h), cols(qb)])
        bufs[u % 2][pl.ds(off, kc), :] = st
        return jnp.maximum(m8, jnp.max(st.reshape(kc // SUBLANE, SUBLANE, TQ), axis=0))

    def weigh(u, off, m, acc):
        pt = jnp.exp2((bufs[u % 2][pl.ds(off, kc), :] - m) * c).astype(BF16)
        return acc + _dot(vt_ref[rows(units[u][1]), pl.ds(off, kc)], pt)

    def phase(u_scores, u_weigh, m):
        def body(it, carry):
            m8, acc = carry
            for j in range(per_it):
                off = (it * per_it + j) * kc
                off = off if isinstance(off, int) else pl.multiple_of(off, kc)
                if u_scores is not None:
                    m8 = scores(u_scores, off, m8)
                if u_weigh is not None:
                    acc = weigh(u_weigh, off, m, acc)
            return m8, acc

        init = (jnp.full((SUBLANE, TQ), NEG_BIG, F32), jnp.zeros((HEAD_PAD, TQ), F32))
        return body(0, init) if n_it == 1 else lax.fori_loop(0, n_it, body, init)

    outs = []
    m8, _ = phase(0, None, None)
    for u in range(1, len(units) + 1):
        m = jnp.max(m8, axis=0, keepdims=True)
        m8, acc = phase(u if u < len(units) else None, u - 1, m)
        outs.append(acc[QK_NOPE_DIM:, :] / acc[0:1, :])
    for qb in range(qps):
        o_ref[cols(qb), :] = jnp.concatenate(outs[qb * hps:(qb + 1) * hps], axis=0).T.astype(BF16)


def _attention(qt, k, vt, *, n_batch, t_q, t_k, q_row0, kv_row0, hps, qps):
    tq = TQ * qps
    nq = t_q // tq
    q0 = q_row0 // tq
    k0 = kv_row0 // t_k
    return pl.pallas_call(
        functools.partial(_attn_kernel, t_k=t_k, hps=hps, qps=qps),
        grid=(n_batch, N_HEADS // hps, nq),
        in_specs=[
            pl.BlockSpec((hps * HEAD_PAD, tq), lambda b, h, i: (h, q0 + b * nq + i)),
            pl.BlockSpec((t_k, hps * HEAD_PAD), lambda b, h, i: (k0 + b, h)),
            pl.BlockSpec((hps * HEAD_PAD, t_k), lambda b, h, i: (h, k0 + b)),
        ],
        out_specs=pl.BlockSpec((tq, hps * V_HEAD_DIM), lambda b, h, i: (b * nq + i, h)),
        out_shape=jax.ShapeDtypeStruct((n_batch * t_q, N_HEADS * V_HEAD_DIM), BF16),
        scratch_shapes=[pltpu.VMEM((t_k, TQ), F32), pltpu.VMEM((t_k, TQ), F32)],
        compiler_params=_cparams(("parallel", "parallel", "parallel")),
        name=f"mla_attention_tk{t_k}",
    )(qt, k, vt)


def _out_proj_kernel(ap_ref, as_ref, xp_ref, xs_ref, mod_ref, wo_ref, o_ref):
    is_prompt = pl.program_id(0) < N_PROMPT // WIDE_TB
    a = jnp.where(is_prompt, ap_ref[...], as_ref[...])
    x = jnp.where(is_prompt, xp_ref[...], xs_ref[...])
    o_ref[...] = x + mod_ref[0][2:3] * _dot(a, wo_ref[...])


def _out_proj(attn_p, attn_s, x_p, x_s, p, layer):
    return pl.pallas_call(
        _out_proj_kernel,
        grid=(N_TOK // WIDE_TB,),
        in_specs=_split_rows_specs(WIDE_TB) + _split_rows_specs(WIDE_TB) + [
            _mod_spec(WIDE_TB, layer),
            _layer_spec((D_MODEL, D_MODEL), layer // 2),
        ],
        out_specs=pl.BlockSpec((WIDE_TB, D_MODEL), lambda i: (i, 0)),
        out_shape=jax.ShapeDtypeStruct((N_TOK, D_MODEL), F32),
        compiler_params=_cparams(("parallel",)),
        name="mla_out_proj",
    )(attn_p, attn_s, x_p, x_s, p["mod"], p["m_w_o"])


def _mla_layer(x_p, x_s, p, layer):
    qt, ckv, krp = _mla_proj(x_p, x_s, p, layer)
    k, vt = _kv_expand(ckv, krp, p, layer)
    a_p = _attention(qt, k, vt, n_batch=BATCH, t_q=SEQ, t_k=SEQ, q_row0=0, kv_row0=DEC_BATCH * KV_LEN,
                     hps=N_HEADS, qps=1)
    a_s = _attention(qt, k, vt, n_batch=DEC_BATCH, t_q=DEC_SEQ, t_k=KV_LEN, q_row0=N_PROMPT, kv_row0=0,
                     hps=4, qps=2)
    x = _out_proj(a_p, a_s, x_p, x_s, p, layer)
    state_ckv = ckv[:N_PROMPT].reshape(BATCH, SEQ, KV_RANK)
    state_krope = krp[:N_PROMPT, QK_NOPE_DIM:QK_HEAD_DIM].reshape(BATCH, SEQ, QK_ROPE_DIM)
    return x, state_ckv, state_krope


def _tile_plan(cnt8, plan_ref, start_ref):
    tm = float(TM)
    cnt_row = cnt8[0:1, :]
    cnt_col = cnt8.T[:, 0:1]
    tiles_row = jnp.floor((cnt_row + (tm - 1.0)) * (1.0 / tm))
    tiles_col = jnp.floor((cnt_col + (tm - 1.0)) * (1.0 / tm))
    sub = lax.broadcasted_iota(jnp.int32, (LANE, LANE), 0).astype(F32)
    lan = lax.broadcasted_iota(jnp.int32, (LANE, LANE), 1).astype(F32)
    tile_end_row = jnp.sum(jnp.where(sub <= lan, tiles_col, 0.0), axis=0, keepdims=True)
    tile_end_col = jnp.sum(jnp.where(lan <= sub, tiles_row, 0.0), axis=1, keepdims=True)
    n_used = jnp.max(tile_end_row, axis=1, keepdims=True)
    start_col = (tile_end_col - tiles_col) * tm
    end_col = start_col + cnt_col
    cand = jnp.where(jnp.logical_and(lan > sub, tiles_row > 0.0), lan, float(LANE))
    next_col = jnp.min(cand, axis=1, keepdims=True)
    next_col = jnp.where(next_col < float(LANE), next_col, -1.0)
    n_lanes = plan_ref.shape[1]
    tidx = jnp.minimum(lax.broadcasted_iota(jnp.int32, (LANE, n_lanes), 1).astype(F32), n_used - 1.0)
    esub = lax.broadcasted_iota(jnp.int32, (LANE, n_lanes), 0).astype(F32)
    te_row = jnp.sum(jnp.where(tile_end_col <= tidx, 1.0, 0.0), axis=0, keepdims=True)
    mine = esub == te_row
    end_at = jnp.sum(jnp.where(mine, end_col, 0.0), axis=0, keepdims=True)
    tv_row = jnp.clip(end_at - tidx[0:1, :] * tm, 0.0, tm)
    nx_row = jnp.sum(jnp.where(mine, next_col, 0.0), axis=0, keepdims=True)
    nu_row = jnp.broadcast_to(n_used, (1, n_lanes))
    plan_ref[...] = jnp.concatenate([te_row, tv_row, nx_row, nu_row, jnp.zeros((SUBLANE - 4, n_lanes), F32)], axis=0)
    start_ref[...] = jnp.broadcast_to(start_col, (LANE, LANE))


def _route_kernel(x_ref, mod_ref, g_ref, wr_ref, br_ref, tri_ref, h_ref, meta_ref, metat_ref, plan_ref, start_ref,
                  carry):
    i = pl.program_id(0)

    @pl.when(i == 0)
    def _():
        carry[...] = jnp.zeros_like(carry)

    x = x_ref[...]
    m = mod_ref[0]
    h = _rms(x, g_ref[...]) * (1.0 + m[4:5]) + m[3:4]
    h_ref[...] = _pack_bf16_pairs(h)
    logits = _dot(h.astype(BF16), wr_ref[...]) + br_ref[...]
    lane = lax.broadcasted_iota(jnp.int32, logits.shape, 1).astype(F32)
    work = logits
    sel = jnp.zeros(logits.shape, F32)
    hits, tops = [], []
    for k in range(TOP_K):
        mk = jnp.max(work, axis=-1, keepdims=True)
        first = jnp.min(jnp.where(work == mk, lane, float(LANE)), axis=-1, keepdims=True)
        hit = lane == first
        sel = jnp.where(hit, 1.0, sel)
        work = jnp.where(hit, -jnp.inf, work)
        hits.append((hit, first))
        tops.append(mk)
    es = [jnp.exp(t - tops[0]) for t in tops]
    denom = es[0] + es[1] + es[2] + es[3]
    pos = _dot(tri_ref[...], sel.astype(BF16)) + carry[0:1, :]
    carry[...] = carry[...] + jnp.sum(sel, axis=0, keepdims=True)

    @pl.when(i == pl.num_programs(0) - 1)
    def _():
        _tile_plan(carry[...], plan_ref, start_ref)

    meta = jnp.zeros(logits.shape, F32)
    for k in range(TOP_K):
        hit, first = hits[k]
        pk = jnp.sum(jnp.where(hit, pos, 0.0), axis=-1, keepdims=True)
        meta = jnp.where(lane == float(k), first, meta)
        meta = jnp.where(lane == float(TOP_K + k), es[k] / denom, meta)
        meta = jnp.where(lane == float(2 * TOP_K + k), pk, meta)
    meta_ref[...] = meta
    metat_ref[...] = meta.T[:2 * SUBLANE, :]


def _route(x, p, layer):
    tb = ROUTE_TB
    return pl.pallas_call(
        _route_kernel,
        grid=(N_TOK // tb,),
        in_specs=[
            pl.BlockSpec((tb, D_MODEL), lambda i: (i, 0)),
            _mod_spec(tb, layer),
            _layer_spec((1, D_MODEL), layer),
            _layer_spec((D_MODEL, LANE), layer),
            _layer_spec((1, LANE), layer),
            _const_spec((tb, tb)),
        ],
        out_specs=[
            pl.BlockSpec((tb, D_MODEL // 2), lambda i: (i, 0)),
            pl.BlockSpec((tb, LANE), lambda i: (i, 0)),
            pl.BlockSpec((2 * SUBLANE, tb), lambda i: (0, i)),
            _const_spec((SUBLANE, PLAN_LANES)),
            _const_spec((LANE, LANE)),
        ],
        out_shape=[
            jax.ShapeDtypeStruct((N_TOK, D_MODEL // 2), jnp.uint32),
            jax.ShapeDtypeStruct((N_TOK, LANE), F32),
            jax.ShapeDtypeStruct((2 * SUBLANE, N_TOK), F32),
            jax.ShapeDtypeStruct((SUBLANE, PLAN_LANES), F32),
            jax.ShapeDtypeStruct((LANE, LANE), F32),
        ],
        scratch_shapes=[pltpu.VMEM((SUBLANE, LANE), F32)],
        compiler_params=_cparams(("arbitrary",)),
        name="moe_route",
    )(x, p["mod"], p["norm_ffn_g"], p["e_w_router"], p["e_b_router"], p["tri"])


def _slots_kernel(start_ref, metat_ref, dest_ref):
    start_col = start_ref[:, 0:1]
    esub = lax.broadcasted_iota(jnp.int32, (LANE, SLOT_TB), 0).astype(F32)
    rows = []
    for k in range(TOP_K):
        e = metat_ref[k:k + 1, :]
        first = jnp.sum(jnp.where(esub == e, start_col, 0.0), axis=0, keepdims=True)
        rows.append(first + metat_ref[2 * TOP_K + k:2 * TOP_K + k + 1, :])
    dest_ref[...] = jnp.concatenate(rows, axis=0).astype(jnp.int32)


def _slots(start, meta_t):
    return pl.pallas_call(
        _slots_kernel,
        grid=(N_TOK // SLOT_TB,),
        in_specs=[
            _const_spec((LANE, LANE)),
            pl.BlockSpec((2 * SUBLANE, SLOT_TB), lambda i: (0, i)),
        ],
        out_specs=pl.BlockSpec((TOP_K, SLOT_TB), lambda i: (0, i)),
        out_shape=jax.ShapeDtypeStruct((TOP_K, N_TOK), jnp.int32),
        compiler_params=_cparams(("parallel",)),
        name="moe_slots",
    )(start, meta_t)


def _sc_gather(table, idx, ch):
    b, w = idx.shape[0], table.shape[1]
    per_w = b // SC_WORKERS
    n_ch = per_w // ch
    assert per_w * SC_WORKERS == b and n_ch * ch == per_w and n_ch % 2 == 0
    mesh = plsc.VectorSubcoreMesh(core_axis_name="c", subcore_axis_name="s")

    @functools.partial(
        pl.kernel, mesh=mesh,
        out_type=jax.ShapeDtypeStruct((b, w), table.dtype),
        scratch_types=[
            pltpu.VMEM((n_ch, ch), jnp.int32),
            pltpu.VMEM((ch, w), table.dtype),
            pltpu.VMEM((ch, w), table.dtype),
            pltpu.SemaphoreType.DMA, pltpu.SemaphoreType.DMA,
            pltpu.SemaphoreType.DMA, pltpu.SemaphoreType.DMA,
        ],
        name="sc_row_gather",
    )
    def gather_rows(table_hbm, idx_hbm, out_hbm, idx_v, buf0, buf1, g0, g1, s0, s1):
        wid = lax.axis_index("s") * SC_CORES + lax.axis_index("c")
        base = wid * per_w
        pltpu.sync_copy(idx_hbm.at[wid], idx_v)

        def gather(j, buf, sem):
            return pltpu.make_async_copy(table_hbm.at[idx_v.at[j]], buf, sem)

        def store(j, buf, sem):
            return pltpu.make_async_copy(buf, out_hbm.at[pl.ds(base + j * ch, ch)], sem)

        gather(0, buf0, g0).start()

        @pl.loop(0, n_ch, step=2)
        def _(j):
            @pl.when(j > 0)
            def _():
                store(j - 1, buf1, s1).wait()

            gather(j + 1, buf1, g1).start()
            gather(j, buf0, g0).wait()
            store(j, buf0, s0).start()
            gather(j + 1, buf1, g1).wait()
            store(j + 1, buf1, s1).start()
            store(j, buf0, s0).wait()

            @pl.when(j + 2 < n_ch)
            def _():
                gather(j + 2, buf0, g0).start()

        store(n_ch - 1, buf1, s1).wait()

    return gather_rows(table, idx.reshape(SC_WORKERS, n_ch, ch))


def _sc_dispatch(rows, dest_t, ch):
    n, w = rows.shape
    per_w = n // SC_WORKERS
    n_ch = per_w // ch
    assert per_w * SC_WORKERS == n and n_ch * ch == per_w and n_ch % 2 == 0
    mesh = plsc.VectorSubcoreMesh(core_axis_name="c", subcore_axis_name="s")
    idx = dest_t.reshape(TOP_K, SC_WORKERS, n_ch, ch)

    @functools.partial(
        pl.kernel, mesh=mesh,
        out_type=jax.ShapeDtypeStruct((N_SLOTS, w), rows.dtype),
        scratch_types=[
            pltpu.VMEM((TOP_K * n_ch, ch), jnp.int32),
            pltpu.VMEM((ch, w), rows.dtype),
            pltpu.VMEM((ch, w), rows.dtype),
            pltpu.SemaphoreType.DMA, pltpu.SemaphoreType.DMA,
            pltpu.SemaphoreType.DMA, pltpu.SemaphoreType.DMA,
        ],
        name="sc_row_dispatch",
    )
    def dispatch_rows(rows_hbm, idx_hbm, out_hbm, idx_v, buf0, buf1, l0, l1, s0, s1):
        wid = lax.axis_index("s") * SC_CORES + lax.axis_index("c")
        base = wid * per_w
        for k in range(TOP_K):
            pltpu.sync_copy(idx_hbm.at[k, wid], idx_v.at[pl.ds(k * n_ch, n_ch)])

        def load(j, buf, sem):
            return pltpu.make_async_copy(rows_hbm.at[pl.ds(base + j * ch, ch)], buf, sem)

        def scatter(j, k, buf, sem):
            return pltpu.make_async_copy(buf, out_hbm.at[idx_v.at[k * n_ch + j]], sem)

        load(0, buf0, l0).start()

        @pl.loop(0, n_ch, step=2)
        def _(j):
            load(j + 1, buf1, l1).start()
            load(j, buf0, l0).wait()
            for k in range(TOP_K):
                scatter(j, k, buf0, s0).start()
            load(j + 1, buf1, l1).wait()
            for k in range(TOP_K):
                scatter(j + 1, k, buf1, s1).start()
            for k in range(TOP_K):
                scatter(j, k, buf0, s0).wait()

            @pl.when(j + 2 < n_ch)
            def _():
                load(j + 2, buf0, l0).start()

            for k in range(TOP_K):
                scatter(j + 1, k, buf1, s1).wait()

    return dispatch_rows(rows, idx)


def _deinterleave_matrix():
    src = jnp.arange(2 * LANE)[:, None]
    dst = jnp.arange(2 * LANE)[None, :]
    want = jnp.where(dst < LANE, 2 * dst, 2 * (dst - LANE) + 1)
    return (src == want).astype(BF16)


def _expert_kernel(te_ref, nu_ref, tv_ref, nx_ref, x_ref, wgu_hbm, bgu_ref, wd_hbm, bd_ref, perm_ref, o_ref,
                   wgu_st, wd_st, wgu_bf, wd_bf, sems, *, layer):
    i = pl.program_id(0)
    prev = te_ref[jnp.maximum(i - 1, 0)]
    fresh = jnp.logical_or(i == 0, te_ref[i] != prev)

    def fetch(e):
        return (pltpu.make_async_copy(wgu_hbm.at[layer, e], wgu_st, sems.at[0]),
                pltpu.make_async_copy(wd_hbm.at[layer, e], wd_st, sems.at[1]))

    @pl.when(i == 0)
    def _():
        for cp in fetch(te_ref[0]):
            cp.start()

    @pl.when(jnp.logical_and(fresh, i < nu_ref[0]))
    def _():
        for cp in fetch(te_ref[i]):
            cp.wait()
        for b in range(2 * D_FF // (2 * LANE)):
            sl = slice(b * 2 * LANE, (b + 1) * 2 * LANE)
            wgu_bf[:, sl] = _dot(wgu_st[:, sl].astype(BF16), perm_ref[...]).astype(BF16)
        wd_bf[...] = wd_st[...].astype(BF16)

        @pl.when(nx_ref[i] >= 0)
        def _():
            for cp in fetch(nx_ref[i]):
                cp.start()

    @pl.when(i < nu_ref[0])
    def _():
        row = lax.broadcasted_iota(jnp.int32, (TM, D_MODEL // 2), 0)
        w = jnp.where(row < tv_ref[i], x_ref[...], jnp.uint32(0))
        x = _unpack_bf16_pairs(w).astype(BF16)
        gu = _dot(x, wgu_bf[...]) + bgu_ref[...]
        acts = []
        for b in range(D_FF // LANE):
            glu = jnp.minimum(gu[:, b * 2 * LANE:b * 2 * LANE + LANE], SWIGLU_LIMIT)
            lin = jnp.clip(gu[:, b * 2 * LANE + LANE:(b + 1) * 2 * LANE], -SWIGLU_LIMIT, SWIGLU_LIMIT)
            acts.append((glu * jax.nn.sigmoid(SWIGLU_ALPHA * glu) * (lin + 1.0)).astype(BF16))
        act = jnp.concatenate(acts, axis=1)
        o_ref[...] = _pack_bf16_pairs(_dot(act, wd_bf[...]) + bd_ref[...])


def _experts(buf, tile_expert, n_used, tile_valid, tile_next, p, layer):
    def row_idx(i, te, nu, tv, nx):
        return (jnp.minimum(i, nu[0] - 1), 0)

    def b_idx(i, te, nu, tv, nx):
        return (layer, te[i], 0, 0)

    grid_spec = pltpu.PrefetchScalarGridSpec(
        num_scalar_prefetch=4,
        grid=(N_TILES,),
        in_specs=[
            pl.BlockSpec((TM, D_MODEL // 2), row_idx),
            pl.BlockSpec(memory_space=pl.ANY),
            pl.BlockSpec((None, None, 1, 2 * D_FF), b_idx),
            pl.BlockSpec(memory_space=pl.ANY),
            pl.BlockSpec((None, None, 1, D_MODEL), b_idx),
            _const_spec((2 * LANE, 2 * LANE)),
        ],
        out_specs=pl.BlockSpec((TM, D_MODEL // 2), row_idx),
        scratch_shapes=[
            pltpu.VMEM((D_MODEL, 2 * D_FF), F32),
            pltpu.VMEM((D_FF, D_MODEL), F32),
            pltpu.VMEM((D_MODEL, 2 * D_FF), BF16),
            pltpu.VMEM((D_FF, D_MODEL), BF16),
            pltpu.SemaphoreType.DMA((2,)),
        ],
    )
    return pl.pallas_call(
        functools.partial(_expert_kernel, layer=layer),
        grid_spec=grid_spec,
        out_shape=jax.ShapeDtypeStruct((N_SLOTS, D_MODEL // 2), jnp.uint32),
        compiler_params=_cparams(("arbitrary",)),
        name="moe_experts",
    )(tile_expert, n_used, tile_valid, tile_next, buf, p["e_w_gu"], p["e_b_gu"], p["e_w_down"], p["e_b_down"],
      p["deinterleave"])


def _pack_bf16_pairs(v):
    half = v.shape[1] // 2
    bits = pltpu.bitcast(v.astype(BF16).astype(F32), jnp.uint32)
    return (bits[:, half:] & jnp.uint32(0xFFFF0000)) | (bits[:, :half] >> 16)


def _unpack_bf16_pairs(w):
    return jnp.concatenate([pltpu.bitcast(w << 16, F32), pltpu.bitcast(w & jnp.uint32(0xFFFF0000), F32)],
                           axis=1)


def _combine_kernel(x_ref, mod_ref, y_ref, w_ref, o_ref):
    w = w_ref[:, TOP_K:2 * TOP_K]
    y = _unpack_bf16_pairs(y_ref[0]) * w[:, 0:1]
    for k in range(1, TOP_K):
        y = y + _unpack_bf16_pairs(y_ref[k]) * w[:, k:k + 1]
    o_ref[...] = x_ref[...] + mod_ref[0][5:6] * y


def _combine(x, y4, meta, p, layer, part):
    n_rows = N_TOK // MOE_PARTS
    tb = WIDE_TB
    first = part * n_rows // tb
    return pl.pallas_call(
        _combine_kernel,
        grid=(n_rows // tb,),
        in_specs=[
            pl.BlockSpec((tb, D_MODEL), lambda i: (i + first, 0)),
            _mod_spec(tb, layer, first),
            pl.BlockSpec((TOP_K, tb, D_MODEL // 2), lambda i: (0, i, 0)),
            pl.BlockSpec((tb, LANE), lambda i: (i + first, 0)),
        ],
        out_specs=pl.BlockSpec((tb, D_MODEL), lambda i: (i, 0)),
        out_shape=jax.ShapeDtypeStruct((n_rows, D_MODEL), F32),
        compiler_params=_cparams(("parallel",)),
        name="moe_combine",
    )(x, p["mod"], y4, meta)


def _moe_layer(x, p, layer):
    hp, meta, meta_t, plan, start = _route(x, p, layer)
    plan = plan[:4, :N_TILES].astype(jnp.int32)
    tile_expert, tile_valid, tile_next, n_used = plan[0], plan[1], plan[2], plan[3, :1]
    dest_t = _slots(start, meta_t)
    buf = _sc_dispatch(hp, dest_t, 64)
    yb = _experts(buf, tile_expert, n_used, tile_valid, tile_next, p, layer)
    n_rows = N_TOK // MOE_PARTS
    outs = []
    for part in range(MOE_PARTS):
        idx = dest_t[:, part * n_rows:(part + 1) * n_rows].reshape(-1)
        y4 = _sc_gather(yb, idx, 64).reshape(TOP_K, n_rows, D_MODEL // 2)
        outs.append(_combine(x, y4, meta, p, layer, part))
    return outs


def _prepare(c, cache_ckv, cache_krope, c_ctx, norm_mix_g, norm_ffn_g, w_mod, b_mod, g_w_in, g_b_in, g_norm_v,
             g_w_s, g_b_s, g_w_out, m_w_dq, m_norm_q, m_w_uq, m_w_dkv, m_norm_kv, m_w_ukv, m_qk_norm_q,
             m_qk_norm_k, m_w_o, e_w_router, e_b_router, e_w_gu, e_b_gu, e_w_down, e_b_down):
    n_mla = m_w_dq.shape[0]
    cond = jnp.concatenate([c_ctx[None, :], c, jnp.zeros((SUBLANE - N_COND, D_MODEL), F32)], axis=0)
    wdkv = jnp.concatenate([m_w_dkv[..., :KV_RANK], jnp.zeros((n_mla, D_MODEL, QK_NOPE_DIM), F32),
                            m_w_dkv[..., KV_RANK:], jnp.zeros((n_mla, D_MODEL, HEAD_PAD - QK_HEAD_DIM), F32)],
                           axis=-1)
    w_uq = jnp.pad(m_w_uq.reshape(n_mla, Q_RANK, N_HEADS, QK_HEAD_DIM),
                   ((0, 0), (0, 0), (0, 0), (0, HEAD_PAD - QK_HEAD_DIM))).reshape(n_mla, Q_RANK, QKV_W)

    def gain_col(g):
        return jnp.pad(g, ((0, 0), (0, HEAD_PAD - QK_HEAD_DIM)))[:, :, None]

    return {
        "mod": _modulation(cond, w_mod, b_mod),
        "rope_t": _rope_tables(),
        "norm_mix_g": norm_mix_g[:, None, :],
        "norm_ffn_g": norm_ffn_g[:, None, :],
        "g_w_in": g_w_in.astype(BF16),
        "g_b_in": g_b_in[:, None, :],
        "g_norm_v": g_norm_v[:, None, :],
        "g_w_s": g_w_s.astype(BF16),
        "g_b_st": jnp.swapaxes(g_b_s, 1, 2),
        "g_w_out": g_w_out.astype(BF16),
        "m_w_dq": m_w_dq.astype(BF16),
        "m_norm_q": m_norm_q[:, None, :],
        "m_w_uq_t": jnp.swapaxes(w_uq, 1, 2).astype(BF16),
        "m_gq": gain_col(m_qk_norm_q),
        "m_w_dkv": wdkv.astype(BF16),
        "m_norm_kv": m_norm_kv[:, None, :],
        "m_w_ukv_t": jnp.swapaxes(m_w_ukv, 1, 2).astype(BF16),
        "m_gk": gain_col(m_qk_norm_k),
        "m_w_o": m_w_o.astype(BF16),
        "cache_ckv": cache_ckv,
        "cache_krp": jnp.pad(cache_krope, ((0, 0), (0, 0), (0, 0), (QK_NOPE_DIM, HEAD_PAD - QK_HEAD_DIM))),
        "e_w_router": jnp.pad(e_w_router, ((0, 0), (0, 0), (0, LANE - N_EXPERTS))).astype(BF16),
        "e_b_router": jnp.pad(e_b_router, ((0, 0), (0, LANE - N_EXPERTS)), constant_values=NEG_BIG)[:, None, :],
        "tri": jnp.tri(ROUTE_TB, ROUTE_TB, -1, dtype=BF16),
        "e_w_gu": e_w_gu,
        "e_b_gu": e_b_gu.reshape(DEPTH, N_EXPERTS, D_FF // LANE, LANE, 2).swapaxes(3, 4).reshape(
            DEPTH, N_EXPERTS, 1, 2 * D_FF),
        "e_w_down": e_w_down,
        "e_b_down": e_b_down[:, :, None, :],
        "deinterleave": _deinterleave_matrix(),
    }


def kernel(x_prompt, x_sample, c, cache_ckv, cache_krope, c_ctx, norm_mix_g, norm_ffn_g, w_mod, b_mod,
           g_w_in, g_b_in, g_norm_v, g_w_s, g_b_s, g_w_out, m_w_dq, m_norm_q, m_w_uq, m_w_dkv,
           m_norm_kv, m_w_ukv, m_qk_norm_q, m_qk_norm_k, m_w_o, e_w_router, e_b_router, e_w_gu,
           e_b_gu, e_w_down, e_b_down):
    p = _prepare(c, cache_ckv, cache_krope, c_ctx, norm_mix_g, norm_ffn_g, w_mod, b_mod, g_w_in, g_b_in,
                 g_norm_v, g_w_s, g_b_s, g_w_out, m_w_dq, m_norm_q, m_w_uq, m_w_dkv, m_norm_kv, m_w_ukv,
                 m_qk_norm_q, m_qk_norm_k, m_w_o, e_w_router, e_b_router, e_w_gu, e_b_gu, e_w_down, e_b_down)
    assert MOE_PARTS == 2 and N_PROMPT == N_SAMPLE
    x_p, x_s = x_prompt.reshape(N_PROMPT, D_MODEL), x_sample.reshape(N_SAMPLE, D_MODEL)
    ckv_states, krope_states = [], []
    for layer in range(DEPTH):
        if layer % 2 == 0:
            x = _gmlp_layer(x_p, x_s, p, layer)
        else:
            x, s_ckv, s_krope = _mla_layer(x_p, x_s, p, layer)
            ckv_states.append(s_ckv)
            krope_states.append(s_krope)
        x_p, x_s = _moe_layer(x, p, layer)
    y_prompt = x_p.reshape(BATCH, SEQ, D_MODEL)
    y_sample = x_s.reshape(DEC_BATCH, DEC_SEQ, D_MODEL)
    return (y_prompt, y_sample, jnp.stack(ckv_states, axis=1), jnp.stack(krope_states, axis=1))
```

```python
import functools
import math

import jax
import jax.numpy as jnp
from jax import lax
from jax.experimental import pallas as pl
from jax.experimental.pallas import tpu as pltpu
from jax.experimental.pallas import tpu_sc as plsc

F32 = jnp.float32
BF16 = jnp.bfloat16

D_MODEL = 1024
BATCH = 32
SEQ = 256
DEPTH = 4
DEC_BATCH = 2
DEC_SEQ = 4096
PAST_LEN = 512
GRID_W = 64
RMS_EPS = 1e-6
GMLP_WIDTH = 2 * D_MODEL
GMLP_GROUPS = 8
GROUP_W = GMLP_WIDTH // GMLP_GROUPS
CHUNK = 128
N_HEADS = 16
QK_NOPE_DIM = 64
QK_ROPE_DIM = 32
QK_HEAD_DIM = QK_NOPE_DIM + QK_ROPE_DIM
V_HEAD_DIM = 64
Q_RANK = 256
KV_RANK = 128
ROPE_THETA = 10000.0
N_EXPERTS = 32
TOP_K = 4
D_FF = D_MODEL
SWIGLU_LIMIT = 7.0
SWIGLU_ALPHA = 1.702

N_PROMPT = BATCH * SEQ
N_SAMPLE = DEC_BATCH * DEC_SEQ
N_TOK = N_PROMPT + N_SAMPLE
N_COND = 1 + DEC_BATCH
KV_LEN = PAST_LEN + DEC_SEQ
N_KV_ROWS = DEC_BATCH * KV_LEN + N_PROMPT

LANE = 128
SUBLANE = 8
HEAD_PAD = LANE
QKV_W = N_HEADS * HEAD_PAD
VMEM_LIMIT = 56 * 1024 * 1024

TB = 512
WIDE_TB = 512
TQ = 256
TKC = 256
ATTN_CHUNKS_PER_ITER = 9
TM = 512
N_TILES = N_TOK * TOP_K // TM + N_EXPERTS
N_SLOTS = N_TILES * TM
PLAN_LANES = -(-N_TILES // LANE) * LANE
SLOT_TB = 2048
ROUTE_TB = 512
MOE_PARTS = 2
SC_CORES = 2
SC_WORKERS = SC_CORES * 16
ROPE_BLOCKS = DEC_SEQ // TB
NEG_BIG = -1e30


def _cparams(sem):
    return pltpu.CompilerParams(dimension_semantics=sem, vmem_limit_bytes=VMEM_LIMIT)


def _cond_of_block(i, tb):
    n_p = N_PROMPT // tb
    per = DEC_SEQ // tb
    return jnp.where(i < n_p, 0, 1 + (i - n_p) // per)


def _rms(x, g, n=None):
    n = x.shape[-1] if n is None else n
    ss = jnp.sum(x * x, axis=-1, keepdims=True) * (1.0 / n)
    return x * lax.rsqrt(ss + RMS_EPS) * g


def _dot(a, b):
    return jnp.dot(a, b, preferred_element_type=F32)


def _mod_kernel(c_ref, w_ref, b_ref, o_ref):
    c = c_ref[...]
    s = c * jax.nn.sigmoid(c)
    o_ref[0] = _dot(s.astype(BF16), w_ref[0].astype(BF16)) + b_ref[0]


def _modulation(cond, w_mod, b_mod):
    tn = 1536
    out = pl.pallas_call(
        _mod_kernel,
        grid=(DEPTH, 6 * D_MODEL // tn),
        in_specs=[
            pl.BlockSpec((SUBLANE, D_MODEL), lambda l, j: (0, 0)),
            pl.BlockSpec((1, D_MODEL, tn), lambda l, j: (l, 0, j)),
            pl.BlockSpec((1, 1, tn), lambda l, j: (l, 0, j)),
        ],
        out_specs=pl.BlockSpec((1, SUBLANE, tn), lambda l, j: (l, 0, j)),
        out_shape=jax.ShapeDtypeStruct((DEPTH, SUBLANE, 6 * D_MODEL), F32),
        compiler_params=_cparams(("parallel", "parallel")),
        name="adaln_mod",
    )(cond, w_mod, b_mod.reshape(DEPTH, 1, 6 * D_MODEL))
    m = out[:, :N_COND].reshape(DEPTH, N_COND, 6, D_MODEL)
    return jnp.pad(m, ((0, 0), (0, 0), (0, SUBLANE - 6), (0, 0)))


def _mod_spec(tb, layer, first_block=0):
    return pl.BlockSpec((None, 1, SUBLANE, D_MODEL),
                        lambda i: (layer, _cond_of_block(i + first_block, tb), 0, 0))


def _const_spec(shape):
    nd = len(shape)
    return pl.BlockSpec(shape, lambda *_: (0,) * nd)


def _layer_spec(shape, j):
    nd = len(shape)
    return pl.BlockSpec((None,) + tuple(shape), lambda *_: (j,) + (0,) * nd)


def _gmlp_kernel(xp_ref, xs_ref, mod_ref, g_ref, win_ref, bin_ref, gv_ref, ws_ref, bst_ref, wout_ref, o_ref):
    x = jnp.where(pl.program_id(0) < N_PROMPT // WIDE_TB, xp_ref[...], xs_ref[...])
    m = mod_ref[0]
    h = _rms(x, g_ref[...]) * (1.0 + m[1:2]) + m[0:1]
    hb = h.astype(BF16)
    zv = jax.nn.gelu(_dot(hb, win_ref[:, GMLP_WIDTH:]) + bin_ref[:, GMLP_WIDTH:], approximate=True)
    vn = _rms(zv, gv_ref[...]).astype(BF16)
    rows = []
    for c in range(WIDE_TB // CHUNK):
        cols = []
        for g in range(GMLP_GROUPS):
            blk = vn[c * CHUNK:(c + 1) * CHUNK, g * GROUP_W:(g + 1) * GROUP_W]
            cols.append(_dot(ws_ref[g], blk) + bst_ref[:, g:g + 1])
        rows.append(jnp.concatenate(cols, axis=1))
    vm = jnp.concatenate(rows, axis=0)
    u = jax.nn.gelu(_dot(hb, win_ref[:, :GMLP_WIDTH]) + bin_ref[:, :GMLP_WIDTH], approximate=True)
    d = _dot((u * vm).astype(BF16), wout_ref[...])
    o_ref[...] = x + m[2:3] * d


def _split_rows_specs(tb=TB):
    n_pb = N_PROMPT // tb
    return [pl.BlockSpec((tb, D_MODEL), lambda i: (jnp.minimum(i, n_pb - 1), 0)),
            pl.BlockSpec((tb, D_MODEL), lambda i: (jnp.maximum(i - n_pb, 0), 0))]


def _gmlp_layer(x_p, x_s, p, layer):
    j = layer // 2
    return pl.pallas_call(
        _gmlp_kernel,
        grid=(N_TOK // WIDE_TB,),
        in_specs=_split_rows_specs(WIDE_TB) + [
            _mod_spec(WIDE_TB, layer),
            _layer_spec((1, D_MODEL), layer),
            _layer_spec((D_MODEL, 2 * GMLP_WIDTH), j),
            _layer_spec((1, 2 * GMLP_WIDTH), j),
            _layer_spec((1, GMLP_WIDTH), j),
            _layer_spec((GMLP_GROUPS, CHUNK, CHUNK), j),
            _layer_spec((CHUNK, GMLP_GROUPS), j),
            _layer_spec((GMLP_WIDTH, D_MODEL), j),
        ],
        out_specs=pl.BlockSpec((WIDE_TB, D_MODEL), lambda i: (i, 0)),
        out_shape=jax.ShapeDtypeStruct((N_TOK, D_MODEL), F32),
        compiler_params=_cparams(("parallel",)),
        name="gmlp_mixer",
    )(x_p, x_s, p["mod"], p["norm_mix_g"], p["g_w_in"], p["g_b_in"], p["g_norm_v"], p["g_w_s"], p["g_b_st"],
      p["g_w_out"])


def _rope_tables():
    t = jnp.arange(DEC_SEQ)
    row_id = (t // GRID_W).astype(F32)
    col_id = (t % GRID_W).astype(F32)
    axis_dim = QK_ROPE_DIM // 2
    inv_freq = ROPE_THETA ** (-jnp.arange(0, axis_dim, 2, dtype=F32) / axis_dim)
    ang = jnp.stack([row_id[:, None] * inv_freq, col_id[:, None] * inv_freq], axis=1)
    cos, sin = jnp.cos(ang), jnp.sin(ang)
    zeros = jnp.zeros_like(sin)
    cos_l = jnp.concatenate([cos, cos], axis=-1).reshape(DEC_SEQ, QK_ROPE_DIM)
    s1_l = jnp.concatenate([-sin, zeros], axis=-1).reshape(DEC_SEQ, QK_ROPE_DIM)
    s2_l = jnp.concatenate([zeros, sin], axis=-1).reshape(DEC_SEQ, QK_ROPE_DIM)

    def widen(rope_part, nope_fill):
        left = jnp.full((DEC_SEQ, QK_NOPE_DIM), nope_fill, F32)
        right = jnp.zeros((DEC_SEQ, HEAD_PAD - QK_HEAD_DIM), F32)
        return jnp.concatenate([left, rope_part, right], axis=-1)

    pos = jnp.stack([widen(cos_l, 1.0), widen(s1_l, 0.0), widen(s2_l, 0.0)])
    ident_c = jnp.concatenate([jnp.ones((TB, QK_HEAD_DIM), F32),
                               jnp.zeros((TB, HEAD_PAD - QK_HEAD_DIM), F32)], axis=-1)
    ident = jnp.stack([ident_c, jnp.zeros_like(ident_c), jnp.zeros_like(ident_c)])
    return jnp.swapaxes(jnp.concatenate([pos, ident], axis=1), 1, 2)


def _dot_nt(a, b):
    return lax.dot_general(a, b, (((1,), (1,)), ((), ())), preferred_element_type=F32)


def _shift_rows(x, n):
    n = n % x.shape[0]
    return jnp.concatenate([x[n:], x[:n]], axis=0)


def _rope_rows(xn, tabt_ref):
    half = QK_ROPE_DIM // 4
    return xn * tabt_ref[0] + _shift_rows(xn, half) * tabt_ref[1] + _shift_rows(xn, -half) * tabt_ref[2]


def _mla_proj_kernel(xp_ref, xs_ref, mod_ref, g_ref, wdq_ref, nq_ref, wuqt_ref, gq_ref, wdkv_ref, nkv_ref,
                     tabt_ref, qt_ref, ckv_ref, krp_ref):
    x = jnp.where(pl.program_id(0) < N_PROMPT // TB, xp_ref[...], xs_ref[...])
    m = mod_ref[0]
    h = _rms(x, g_ref[...]) * (1.0 + m[1:2]) + m[0:1]
    hb = h.astype(BF16)
    cq = _rms(_dot(hb, wdq_ref[...]), nq_ref[...])
    qt = _dot_nt(wuqt_ref[...], cq.astype(BF16))
    gq = gq_ref[...]
    for hd in range(N_HEADS):
        qh = qt[hd * HEAD_PAD:(hd + 1) * HEAD_PAD, :]
        ss = jnp.sum(qh * qh, axis=0, keepdims=True) * (1.0 / QK_HEAD_DIM)
        qn = qh * lax.rsqrt(ss + RMS_EPS) * gq
        qt_ref[hd * HEAD_PAD:(hd + 1) * HEAD_PAD, :] = _rope_rows(qn, tabt_ref).astype(BF16)
    kva = _dot(hb, wdkv_ref[...])
    ckv_ref[...] = _rms(kva[:, :KV_RANK], nkv_ref[...])
    krp_ref[...] = kva[:, KV_RANK:]


def _mla_proj(x_p, x_s, p, layer):
    j = layer // 2
    n_pb = N_PROMPT // TB

    def tab_idx(i):
        return (0, 0, jnp.where(i < n_pb, ROPE_BLOCKS, (i - n_pb) % ROPE_BLOCKS))

    return pl.pallas_call(
        _mla_proj_kernel,
        grid=(N_TOK // TB,),
        in_specs=_split_rows_specs() + [
            _mod_spec(TB, layer),
            _layer_spec((1, D_MODEL), layer),
            _layer_spec((D_MODEL, Q_RANK), j),
            _layer_spec((1, Q_RANK), j),
            _layer_spec((QKV_W, Q_RANK), j),
            _layer_spec((HEAD_PAD, 1), j),
            _layer_spec((D_MODEL, 2 * LANE), j),
            _layer_spec((1, KV_RANK), j),
            pl.BlockSpec((3, HEAD_PAD, TB), tab_idx),
        ],
        out_specs=[
            pl.BlockSpec((QKV_W, TB), lambda i: (0, i)),
            pl.BlockSpec((TB, KV_RANK), lambda i: (i, 0)),
            pl.BlockSpec((TB, HEAD_PAD), lambda i: (i, 0)),
        ],
        out_shape=[
            jax.ShapeDtypeStruct((QKV_W, N_TOK), BF16),
            jax.ShapeDtypeStruct((N_TOK, KV_RANK), F32),
            jax.ShapeDtypeStruct((N_TOK, HEAD_PAD), F32),
        ],
        compiler_params=_cparams(("parallel",)),
        name="mla_proj",
    )(x_p, x_s, p["mod"], p["norm_mix_g"], p["m_w_dq"], p["m_norm_q"], p["m_w_uq_t"], p["m_gq"], p["m_w_dkv"],
      p["m_norm_kv"], p["rope_t"])


def _kv_block_source(i):
    n_sb = DEC_BATCH * KV_LEN // TB
    per = KV_LEN // TB
    b = jnp.minimum(i // per, DEC_BATCH - 1)
    jj = i % per
    return jnp.logical_and(i < n_sb, jj < PAST_LEN // TB), b, jj


def _kv_expand_kernel(cckv_ref, ckrp_ref, ckv_ref, krp_ref, wukvt_ref, gk_ref, tabt_ref, k_ref, vt_ref):
    from_cache, _, _ = _kv_block_source(pl.program_id(0))
    ckv = jnp.where(from_cache, cckv_ref[...], ckv_ref[...])
    krp = jnp.where(from_cache, ckrp_ref[...], krp_ref[...])
    kvt = _dot_nt(wukvt_ref[...], ckv.astype(BF16))
    krpt = krp.T
    gk = gk_ref[...]
    row = lax.broadcasted_iota(jnp.int32, (HEAD_PAD, TB), 0)
    is_nope = row < QK_NOPE_DIM
    ones_row = jnp.where(row == 0, 1.0, 0.0)
    for hd in range(N_HEADS):
        blk = kvt[hd * HEAD_PAD:(hd + 1) * HEAD_PAD, :]
        kraw = jnp.where(is_nope, blk, krpt)
        ss = jnp.sum(kraw * kraw, axis=0, keepdims=True) * (1.0 / QK_HEAD_DIM)
        kn = kraw * lax.rsqrt(ss + RMS_EPS) * gk
        k_ref[:, hd * HEAD_PAD:(hd + 1) * HEAD_PAD] = _rope_rows(kn, tabt_ref).T.astype(BF16)
        vt_ref[hd * HEAD_PAD:(hd + 1) * HEAD_PAD, :] = jnp.where(is_nope, ones_row, blk).astype(BF16)


def _kv_expand(ckv, krp, p, layer):
    j = layer // 2
    n_sb = DEC_BATCH * KV_LEN // TB
    n_cache = PAST_LEN // TB
    n_pb = N_PROMPT // TB
    lat_blocks = DEC_SEQ // TB

    def tab_idx(i):
        from_cache, _, jj = _kv_block_source(i)
        return (0, 0, jnp.where((i >= n_sb) | from_cache, ROPE_BLOCKS, jj - n_cache))

    def cache_idx(i):
        _, b, jj = _kv_block_source(i)
        return (b, j, jnp.minimum(jj, n_cache - 1), 0)

    def tok_idx(i):
        _, b, jj = _kv_block_source(i)
        latent = n_pb + b * lat_blocks + jnp.maximum(jj - n_cache, 0)
        return (jnp.where(i < n_sb, latent, i - n_sb), 0)

    return pl.pallas_call(
        _kv_expand_kernel,
        grid=(N_KV_ROWS // TB,),
        in_specs=[
            pl.BlockSpec((None, None, TB, KV_RANK), cache_idx),
            pl.BlockSpec((None, None, TB, HEAD_PAD), cache_idx),
            pl.BlockSpec((TB, KV_RANK), tok_idx),
            pl.BlockSpec((TB, HEAD_PAD), tok_idx),
            _layer_spec((QKV_W, KV_RANK), j),
            _layer_spec((HEAD_PAD, 1), j),
            pl.BlockSpec((3, HEAD_PAD, TB), tab_idx),
        ],
        out_specs=[
            pl.BlockSpec((TB, QKV_W), lambda i: (i, 0)),
            pl.BlockSpec((QKV_W, TB), lambda i: (0, i)),
        ],
        out_shape=[
            jax.ShapeDtypeStruct((N_KV_ROWS, QKV_W), BF16),
            jax.ShapeDtypeStruct((QKV_W, N_KV_ROWS), BF16),
        ],
        compiler_params=_cparams(("parallel",)),
        name="mla_kv_expand",
    )(p["cache_ckv"], p["cache_krp"], ckv, krp, p["m_w_ukv_t"], p["m_gk"], p["rope_t"])


def _attn_kernel(qt_ref, k_ref, vt_ref, o_ref, s_a, s_b, *, t_k, hps, qps):
    c = (1.0 / math.sqrt(QK_HEAD_DIM)) * math.log2(math.e)
    kc = min(TKC, t_k)
    n_chunks = t_k // kc
    per_it = min(ATTN_CHUNKS_PER_ITER, n_chunks)
    n_it = n_chunks // per_it
    bufs = (s_a, s_b)

    units = [(qb, h) for qb in range(qps) for h in range(hps)]

    def rows(h):
        return slice(h * HEAD_PAD, (h + 1) * HEAD_PAD)

    def cols(qb):
        return slice(qb * TQ, (qb + 1) * TQ)

    def scores(u, off, m8):
        qb, h = units[u]
        st = _dot(k_ref[pl.ds(off, kc), rows(h)], qt_ref[rows(h), cols(qb)])
        bufs[u % 2][pl.ds(off, kc), :] = st
        return jnp.maximum(m8, jnp.max(st.reshape(kc // SUBLANE, SUBLANE, TQ), axis=0))

    def weigh(u, off, m, acc):
        pt = jnp.exp2((bufs[u % 2][pl.ds(off, kc), :] - m) * c).astype(BF16)
        return acc + _dot(vt_ref[rows(units[u][1]), pl.ds(off, kc)], pt)

    def phase(u_scores, u_weigh, m):
        def body(it, carry):
            m8, acc = carry
            for j in range(per_it):
                off = (it * per_it + j) * kc
                off = off if isinstance(off, int) else pl.multiple_of(off, kc)
                if u_scores is not None:
                    m8 = scores(u_scores, off, m8)
                if u_weigh is not None:
                    acc = weigh(u_weigh, off, m, acc)
            return m8, acc

        init = (jnp.full((SUBLANE, TQ), NEG_BIG, F32), jnp.zeros((HEAD_PAD, TQ), F32))
        return body(0, init) if n_it == 1 else lax.fori_loop(0, n_it, body, init)

    outs = []
    if n_chunks == 1:
        sts = [_dot(k_ref[:, rows(h)], qt_ref[rows(h), cols(qb)]) for qb, h in units]
        for (qb, h), st in zip(units, sts):
            m = jnp.max(st, axis=0, keepdims=True)
            acc = _dot(vt_ref[rows(h), :], jnp.exp2((st - m) * c).astype(BF16))
            outs.append(acc[QK_NOPE_DIM:, :] / acc[0:1, :])
    else:
        m8, _ = phase(0, None, None)
        for u in range(1, len(units) + 1):
            m = jnp.max(m8, axis=0, keepdims=True)
            m8, acc = phase(u if u < len(units) else None, u - 1, m)
            outs.append(acc[QK_NOPE_DIM:, :] / acc[0:1, :])
    for qb in range(qps):
        o_ref[cols(qb), :] = jnp.concatenate(outs[qb * hps:(qb + 1) * hps], axis=0).T.astype(BF16)


def _attention(qt, k, vt, *, n_batch, t_q, t_k, q_row0, kv_row0, hps, qps):
    tq = TQ * qps
    nq = t_q // tq
    q0 = q_row0 // tq
    k0 = kv_row0 // t_k
    return pl.pallas_call(
        functools.partial(_attn_kernel, t_k=t_k, hps=hps, qps=qps),
        grid=(n_batch, N_HEADS // hps, nq),
        in_specs=[
            pl.BlockSpec((hps * HEAD_PAD, tq), lambda b, h, i: (h, q0 + b * nq + i)),
            pl.BlockSpec((t_k, hps * HEAD_PAD), lambda b, h, i: (k0 + b, h)),
            pl.BlockSpec((hps * HEAD_PAD, t_k), lambda b, h, i: (h, k0 + b)),
        ],
        out_specs=pl.BlockSpec((tq, hps * V_HEAD_DIM), lambda b, h, i: (b * nq + i, h)),
        out_shape=jax.ShapeDtypeStruct((n_batch * t_q, N_HEADS * V_HEAD_DIM), BF16),
        scratch_shapes=[pltpu.VMEM((t_k, TQ), F32), pltpu.VMEM((t_k, TQ), F32)],
        compiler_params=_cparams(("parallel", "parallel", "parallel")),
        name=f"mla_attention_tk{t_k}",
    )(qt, k, vt)


def _out_proj_kernel(ap_ref, as_ref, xp_ref, xs_ref, mod_ref, wo_ref, o_ref):
    is_prompt = pl.program_id(0) < N_PROMPT // WIDE_TB
    a = jnp.where(is_prompt, ap_ref[...], as_ref[...])
    x = jnp.where(is_prompt, xp_ref[...], xs_ref[...])
    o_ref[...] = x + mod_ref[0][2:3] * _dot(a, wo_ref[...])


def _out_proj(attn_p, attn_s, x_p, x_s, p, layer):
    return pl.pallas_call(
        _out_proj_kernel,
        grid=(N_TOK // WIDE_TB,),
        in_specs=_split_rows_specs(WIDE_TB) + _split_rows_specs(WIDE_TB) + [
            _mod_spec(WIDE_TB, layer),
            _layer_spec((D_MODEL, D_MODEL), layer // 2),
        ],
        out_specs=pl.BlockSpec((WIDE_TB, D_MODEL), lambda i: (i, 0)),
        out_shape=jax.ShapeDtypeStruct((N_TOK, D_MODEL), F32),
        compiler_params=_cparams(("parallel",)),
        name="mla_out_proj",
    )(attn_p, attn_s, x_p, x_s, p["mod"], p["m_w_o"])


def _mla_layer(x_p, x_s, p, layer):
    qt, ckv, krp = _mla_proj(x_p, x_s, p, layer)
    k, vt = _kv_expand(ckv, krp, p, layer)
    a_p = _attention(qt, k, vt, n_batch=BATCH, t_q=SEQ, t_k=SEQ, q_row0=0, kv_row0=DEC_BATCH * KV_LEN,
                     hps=N_HEADS, qps=1)
    a_s = _attention(qt, k, vt, n_batch=DEC_BATCH, t_q=DEC_SEQ, t_k=KV_LEN, q_row0=N_PROMPT, kv_row0=0,
                     hps=4, qps=2)
    x = _out_proj(a_p, a_s, x_p, x_s, p, layer)
    state_ckv = ckv[:N_PROMPT].reshape(BATCH, SEQ, KV_RANK)
    state_krope = krp[:N_PROMPT, QK_NOPE_DIM:QK_HEAD_DIM].reshape(BATCH, SEQ, QK_ROPE_DIM)
    return x, state_ckv, state_krope


def _tile_plan(cnt8, plan_ref, start_ref):
    tm = float(TM)
    cnt_row = cnt8[0:1, :]
    cnt_col = cnt8.T[:, 0:1]
    tiles_row = jnp.floor((cnt_row + (tm - 1.0)) * (1.0 / tm))
    tiles_col = jnp.floor((cnt_col + (tm - 1.0)) * (1.0 / tm))
    sub = lax.broadcasted_iota(jnp.int32, (LANE, LANE), 0).astype(F32)
    lan = lax.broadcasted_iota(jnp.int32, (LANE, LANE), 1).astype(F32)
    tile_end_row = jnp.sum(jnp.where(sub <= lan, tiles_col, 0.0), axis=0, keepdims=True)
    tile_end_col = jnp.sum(jnp.where(lan <= sub, tiles_row, 0.0), axis=1, keepdims=True)
    n_used = jnp.max(tile_end_row, axis=1, keepdims=True)
    start_col = (tile_end_col - tiles_col) * tm
    end_col = start_col + cnt_col
    cand = jnp.where(jnp.logical_and(lan > sub, tiles_row > 0.0), lan, float(LANE))
    next_col = jnp.min(cand, axis=1, keepdims=True)
    next_col = jnp.where(next_col < float(LANE), next_col, -1.0)
    n_lanes = plan_ref.shape[1]
    tidx = jnp.minimum(lax.broadcasted_iota(jnp.int32, (LANE, n_lanes), 1).astype(F32), n_used - 1.0)
    esub = lax.broadcasted_iota(jnp.int32, (LANE, n_lanes), 0).astype(F32)
    te_row = jnp.sum(jnp.where(tile_end_col <= tidx, 1.0, 0.0), axis=0, keepdims=True)
    mine = esub == te_row
    end_at = jnp.sum(jnp.where(mine, end_col, 0.0), axis=0, keepdims=True)
    tv_row = jnp.clip(end_at - tidx[0:1, :] * tm, 0.0, tm)
    nx_row = jnp.sum(jnp.where(mine, next_col, 0.0), axis=0, keepdims=True)
    nu_row = jnp.broadcast_to(n_used, (1, n_lanes))
    plan_ref[...] = jnp.concatenate([te_row, tv_row, nx_row, nu_row, jnp.zeros((SUBLANE - 4, n_lanes), F32)], axis=0)
    start_ref[...] = jnp.broadcast_to(start_col, (LANE, LANE))


def _route_kernel(x_ref, mod_ref, g_ref, wr_ref, br_ref, tri_ref, h_ref, meta_ref, metat_ref, plan_ref, start_ref,
                  carry):
    i = pl.program_id(0)

    @pl.when(i == 0)
    def _():
        carry[...] = jnp.zeros_like(carry)

    x = x_ref[...]
    m = mod_ref[0]
    h = _rms(x, g_ref[...]) * (1.0 + m[4:5]) + m[3:4]
    h_ref[...] = _pack_bf16_pairs(h)
    logits = _dot(h.astype(BF16), wr_ref[...]) + br_ref[...]
    lane = lax.broadcasted_iota(jnp.int32, logits.shape, 1).astype(F32)
    work = logits
    sel = jnp.zeros(logits.shape, F32)
    hits, tops = [], []
    for k in range(TOP_K):
        mk = jnp.max(work, axis=-1, keepdims=True)
        first = jnp.min(jnp.where(work == mk, lane, float(LANE)), axis=-1, keepdims=True)
        hit = lane == first
        sel = jnp.where(hit, 1.0, sel)
        work = jnp.where(hit, -jnp.inf, work)
        hits.append((hit, first))
        tops.append(mk)
    es = [jnp.exp(t - tops[0]) for t in tops]
    denom = es[0] + es[1] + es[2] + es[3]
    pos = _dot(tri_ref[...], sel.astype(BF16)) + carry[0:1, :]
    carry[...] = carry[...] + jnp.sum(sel, axis=0, keepdims=True)

    @pl.when(i == pl.num_programs(0) - 1)
    def _():
        _tile_plan(carry[...], plan_ref, start_ref)

    meta = jnp.zeros(logits.shape, F32)
    for k in range(TOP_K):
        hit, first = hits[k]
        pk = jnp.sum(jnp.where(hit, pos, 0.0), axis=-1, keepdims=True)
        meta = jnp.where(lane == float(k), first, meta)
        meta = jnp.where(lane == float(TOP_K + k), es[k] / denom, meta)
        meta = jnp.where(lane == float(2 * TOP_K + k), pk, meta)
    meta_ref[...] = meta
    metat_ref[...] = meta.T[:2 * SUBLANE, :]


def _route(x, p, layer):
    tb = ROUTE_TB
    return pl.pallas_call(
        _route_kernel,
        grid=(N_TOK // tb,),
        in_specs=[
            pl.BlockSpec((tb, D_MODEL), lambda i: (i, 0)),
            _mod_spec(tb, layer),
            _layer_spec((1, D_MODEL), layer),
            _layer_spec((D_MODEL, LANE), layer),
            _layer_spec((1, LANE), layer),
            _const_spec((tb, tb)),
        ],
        out_specs=[
            pl.BlockSpec((tb, D_MODEL // 2), lambda i: (i, 0)),
            pl.BlockSpec((tb, LANE), lambda i: (i, 0)),
            pl.BlockSpec((2 * SUBLANE, tb), lambda i: (0, i)),
            _const_spec((SUBLANE, PLAN_LANES)),
            _const_spec((LANE, LANE)),
        ],
        out_shape=[
            jax.ShapeDtypeStruct((N_TOK, D_MODEL // 2), jnp.uint32),
            jax.ShapeDtypeStruct((N_TOK, LANE), F32),
            jax.ShapeDtypeStruct((2 * SUBLANE, N_TOK), F32),
            jax.ShapeDtypeStruct((SUBLANE, PLAN_LANES), F32),
            jax.ShapeDtypeStruct((LANE, LANE), F32),
        ],
        scratch_shapes=[pltpu.VMEM((SUBLANE, LANE), F32)],
        compiler_params=_cparams(("arbitrary",)),
        name="moe_route",
    )(x, p["mod"], p["norm_ffn_g"], p["e_w_router"], p["e_b_router"], p["tri"])


def _slots_kernel(start_ref, metat_ref, dest_ref):
    start_col = start_ref[:, 0:1]
    esub = lax.broadcasted_iota(jnp.int32, (LANE, SLOT_TB), 0).astype(F32)
    rows = []
    for k in range(TOP_K):
        e = metat_ref[k:k + 1, :]
        first = jnp.sum(jnp.where(esub == e, start_col, 0.0), axis=0, keepdims=True)
        rows.append(first + metat_ref[2 * TOP_K + k:2 * TOP_K + k + 1, :])
    dest_ref[...] = jnp.concatenate(rows, axis=0).astype(jnp.int32)


def _slots(start, meta_t):
    return pl.pallas_call(
        _slots_kernel,
        grid=(N_TOK // SLOT_TB,),
        in_specs=[
            _const_spec((LANE, LANE)),
            pl.BlockSpec((2 * SUBLANE, SLOT_TB), lambda i: (0, i)),
        ],
        out_specs=pl.BlockSpec((TOP_K, SLOT_TB), lambda i: (0, i)),
        out_shape=jax.ShapeDtypeStruct((TOP_K, N_TOK), jnp.int32),
        compiler_params=_cparams(("parallel",)),
        name="moe_slots",
    )(start, meta_t)


def _sc_gather(table, idx, ch):
    b, w = idx.shape[0], table.shape[1]
    per_w = b // SC_WORKERS
    n_ch = per_w // ch
    assert per_w * SC_WORKERS == b and n_ch * ch == per_w and n_ch % 2 == 0
    mesh = plsc.VectorSubcoreMesh(core_axis_name="c", subcore_axis_name="s")

    @functools.partial(
        pl.kernel, mesh=mesh,
        out_type=jax.ShapeDtypeStruct((b, w), table.dtype),
        scratch_types=[
            pltpu.VMEM((n_ch, ch), jnp.int32),
            pltpu.VMEM((ch, w), table.dtype),
            pltpu.VMEM((ch, w), table.dtype),
            pltpu.SemaphoreType.DMA, pltpu.SemaphoreType.DMA,
            pltpu.SemaphoreType.DMA, pltpu.SemaphoreType.DMA,
        ],
        name="sc_row_gather",
    )
    def gather_rows(table_hbm, idx_hbm, out_hbm, idx_v, buf0, buf1, g0, g1, s0, s1):
        wid = lax.axis_index("s") * SC_CORES + lax.axis_index("c")
        base = wid * per_w
        pltpu.sync_copy(idx_hbm.at[wid], idx_v)

        def gather(j, buf, sem):
            return pltpu.make_async_copy(table_hbm.at[idx_v.at[j]], buf, sem)

        def store(j, buf, sem):
            return pltpu.make_async_copy(buf, out_hbm.at[pl.ds(base + j * ch, ch)], sem)

        gather(0, buf0, g0).start()

        @pl.loop(0, n_ch, step=2)
        def _(j):
            @pl.when(j > 0)
            def _():
                store(j - 1, buf1, s1).wait()

            gather(j + 1, buf1, g1).start()
            gather(j, buf0, g0).wait()
            store(j, buf0, s0).start()
            gather(j + 1, buf1, g1).wait()
            store(j + 1, buf1, s1).start()
            store(j, buf0, s0).wait()

            @pl.when(j + 2 < n_ch)
            def _():
                gather(j + 2, buf0, g0).start()

        store(n_ch - 1, buf1, s1).wait()

    return gather_rows(table, idx.reshape(SC_WORKERS, n_ch, ch))


def _sc_dispatch(rows, dest_t, ch):
    n, w = rows.shape
    per_w = n // SC_WORKERS
    n_ch = per_w // ch
    assert per_w * SC_WORKERS == n and n_ch * ch == per_w and n_ch % 2 == 0
    mesh = plsc.VectorSubcoreMesh(core_axis_name="c", subcore_axis_name="s")
    idx = dest_t.reshape(TOP_K, SC_WORKERS, n_ch, ch)

    @functools.partial(
        pl.kernel, mesh=mesh,
        out_type=jax.ShapeDtypeStruct((N_SLOTS, w), rows.dtype),
        scratch_types=[
            pltpu.VMEM((TOP_K * n_ch, ch), jnp.int32),
            pltpu.VMEM((ch, w), rows.dtype),
            pltpu.VMEM((ch, w), rows.dtype),
            pltpu.SemaphoreType.DMA, pltpu.SemaphoreType.DMA,
            pltpu.SemaphoreType.DMA, pltpu.SemaphoreType.DMA,
        ],
        name="sc_row_dispatch",
    )
    def dispatch_rows(rows_hbm, idx_hbm, out_hbm, idx_v, buf0, buf1, l0, l1, s0, s1):
        wid = lax.axis_index("s") * SC_CORES + lax.axis_index("c")
        base = wid * per_w
        for k in range(TOP_K):
            pltpu.sync_copy(idx_hbm.at[k, wid], idx_v.at[pl.ds(k * n_ch, n_ch)])

        def load(j, buf, sem):
            return pltpu.make_async_copy(rows_hbm.at[pl.ds(base + j * ch, ch)], buf, sem)

        def scatter(j, k, buf, sem):
            return pltpu.make_async_copy(buf, out_hbm.at[idx_v.at[k * n_ch + j]], sem)

        load(0, buf0, l0).start()

        @pl.loop(0, n_ch, step=2)
        def _(j):
            load(j + 1, buf1, l1).start()
            load(j, buf0, l0).wait()
            for k in range(TOP_K):
                scatter(j, k, buf0, s0).start()
            load(j + 1, buf1, l1).wait()
            for k in range(TOP_K):
                scatter(j + 1, k, buf1, s1).start()
            for k in range(TOP_K):
                scatter(j, k, buf0, s0).wait()

            @pl.when(j + 2 < n_ch)
            def _():
                load(j + 2, buf0, l0).start()

            for k in range(TOP_K):
                scatter(j + 1, k, buf1, s1).wait()

    return dispatch_rows(rows, idx)


def _deinterleave_matrix():
    src = jnp.arange(2 * LANE)[:, None]
    dst = jnp.arange(2 * LANE)[None, :]
    want = jnp.where(dst < LANE, 2 * dst, 2 * (dst - LANE) + 1)
    return (src == want).astype(BF16)


def _expert_kernel(te_ref, nu_ref, tv_ref, nx_ref, x_ref, wgu_hbm, bgu_ref, wd_hbm, bd_ref, perm_ref, o_ref,
                   wgu_st, wd_st, wgu_bf, wd_bf, sems, *, layer):
    i = pl.program_id(0)
    prev = te_ref[jnp.maximum(i - 1, 0)]
    fresh = jnp.logical_or(i == 0, te_ref[i] != prev)

    def fetch(e):
        return (pltpu.make_async_copy(wgu_hbm.at[layer, e], wgu_st, sems.at[0]),
                pltpu.make_async_copy(wd_hbm.at[layer, e], wd_st, sems.at[1]))

    @pl.when(i == 0)
    def _():
        for cp in fetch(te_ref[0]):
            cp.start()

    @pl.when(jnp.logical_and(fresh, i < nu_ref[0]))
    def _():
        for cp in fetch(te_ref[i]):
            cp.wait()
        for b in range(2 * D_FF // (2 * LANE)):
            sl = slice(b * 2 * LANE, (b + 1) * 2 * LANE)
            wgu_bf[:, sl] = _dot(wgu_st[:, sl].astype(BF16), perm_ref[...]).astype(BF16)
        wd_bf[...] = wd_st[...].astype(BF16)

        @pl.when(nx_ref[i] >= 0)
        def _():
            for cp in fetch(nx_ref[i]):
                cp.start()

    @pl.when(i < nu_ref[0])
    def _():
        row = lax.broadcasted_iota(jnp.int32, (TM, D_MODEL // 2), 0)
        w = jnp.where(row < tv_ref[i], x_ref[...], jnp.uint32(0))
        x = _unpack_bf16_pairs(w).astype(BF16)
        gu = _dot(x, wgu_bf[...]) + bgu_ref[...]
        acts = []
        for b in range(D_FF // LANE):
            glu = jnp.minimum(gu[:, b * 2 * LANE:b * 2 * LANE + LANE], SWIGLU_LIMIT)
            lin = jnp.clip(gu[:, b * 2 * LANE + LANE:(b + 1) * 2 * LANE], -SWIGLU_LIMIT, SWIGLU_LIMIT)
            acts.append((glu * jax.nn.sigmoid(SWIGLU_ALPHA * glu) * (lin + 1.0)).astype(BF16))
        act = jnp.concatenate(acts, axis=1)
        o_ref[...] = _pack_bf16_pairs(_dot(act, wd_bf[...]) + bd_ref[...])


def _experts(buf, tile_expert, n_used, tile_valid, tile_next, p, layer):
    def row_idx(i, te, nu, tv, nx):
        return (jnp.minimum(i, nu[0] - 1), 0)

    def b_idx(i, te, nu, tv, nx):
        return (layer, te[i], 0, 0)

    grid_spec = pltpu.PrefetchScalarGridSpec(
        num_scalar_prefetch=4,
        grid=(N_TILES,),
        in_specs=[
            pl.BlockSpec((TM, D_MODEL // 2), row_idx),
            pl.BlockSpec(memory_space=pl.ANY),
            pl.BlockSpec((None, None, 1, 2 * D_FF), b_idx),
            pl.BlockSpec(memory_space=pl.ANY),
            pl.BlockSpec((None, None, 1, D_MODEL), b_idx),
            _const_spec((2 * LANE, 2 * LANE)),
        ],
        out_specs=pl.BlockSpec((TM, D_MODEL // 2), row_idx),
        scratch_shapes=[
            pltpu.VMEM((D_MODEL, 2 * D_FF), F32),
            pltpu.VMEM((D_FF, D_MODEL), F32),
            pltpu.VMEM((D_MODEL, 2 * D_FF), BF16),
            pltpu.VMEM((D_FF, D_MODEL), BF16),
            pltpu.SemaphoreType.DMA((2,)),
        ],
    )
    return pl.pallas_call(
        functools.partial(_expert_kernel, layer=layer),
        grid_spec=grid_spec,
        out_shape=jax.ShapeDtypeStruct((N_SLOTS, D_MODEL // 2), jnp.uint32),
        compiler_params=_cparams(("arbitrary",)),
        name="moe_experts",
    )(tile_expert, n_used, tile_valid, tile_next, buf, p["e_w_gu"], p["e_b_gu"], p["e_w_down"], p["e_b_down"],
      p["deinterleave"])


def _pack_bf16_pairs(v):
    half = v.shape[1] // 2
    bits = pltpu.bitcast(v.astype(BF16).astype(F32), jnp.uint32)
    return (bits[:, half:] & jnp.uint32(0xFFFF0000)) | (bits[:, :half] >> 16)


def _unpack_bf16_pairs(w):
    return jnp.concatenate([pltpu.bitcast(w << 16, F32), pltpu.bitcast(w & jnp.uint32(0xFFFF0000), F32)],
                           axis=1)


def _combine_kernel(x_ref, mod_ref, y_ref, w_ref, o_ref):
    w = w_ref[:, TOP_K:2 * TOP_K]
    y = _unpack_bf16_pairs(y_ref[0]) * w[:, 0:1]
    for k in range(1, TOP_K):
        y = y + _unpack_bf16_pairs(y_ref[k]) * w[:, k:k + 1]
    o_ref[...] = x_ref[...] + mod_ref[0][5:6] * y


def _combine(x, y4, meta, p, layer, part):
    n_rows = N_TOK // MOE_PARTS
    tb = WIDE_TB
    first = part * n_rows // tb
    return pl.pallas_call(
        _combine_kernel,
        grid=(n_rows // tb,),
        in_specs=[
            pl.BlockSpec((tb, D_MODEL), lambda i: (i + first, 0)),
            _mod_spec(tb, layer, first),
            pl.BlockSpec((TOP_K, tb, D_MODEL // 2), lambda i: (0, i, 0)),
            pl.BlockSpec((tb, LANE), lambda i: (i + first, 0)),
        ],
        out_specs=pl.BlockSpec((tb, D_MODEL), lambda i: (i, 0)),
        out_shape=jax.ShapeDtypeStruct((n_rows, D_MODEL), F32),
        compiler_params=_cparams(("parallel",)),
        name="moe_combine",
    )(x, p["mod"], y4, meta)


def _moe_layer(x, p, layer):
    hp, meta, meta_t, plan, start = _route(x, p, layer)
    plan = plan[:4, :N_TILES].astype(jnp.int32)
    tile_expert, tile_valid, tile_next, n_used = plan[0], plan[1], plan[2], plan[3, :1]
    dest_t = _slots(start, meta_t)
    buf = _sc_dispatch(hp, dest_t, 64)
    yb = _experts(buf, tile_expert, n_used, tile_valid, tile_next, p, layer)
    n_rows = N_TOK // MOE_PARTS
    outs = []
    for part in range(MOE_PARTS):
        idx = dest_t[:, part * n_rows:(part + 1) * n_rows].reshape(-1)
        y4 = _sc_gather(yb, idx, 64).reshape(TOP_K, n_rows, D_MODEL // 2)
        outs.append(_combine(x, y4, meta, p, layer, part))
    return outs


def _prepare(c, cache_ckv, cache_krope, c_ctx, norm_mix_g, norm_ffn_g, w_mod, b_mod, g_w_in, g_b_in, g_norm_v,
             g_w_s, g_b_s, g_w_out, m_w_dq, m_norm_q, m_w_uq, m_w_dkv, m_norm_kv, m_w_ukv, m_qk_norm_q,
             m_qk_norm_k, m_w_o, e_w_router, e_b_router, e_w_gu, e_b_gu, e_w_down, e_b_down):
    n_mla = m_w_dq.shape[0]
    cond = jnp.concatenate([c_ctx[None, :], c, jnp.zeros((SUBLANE - N_COND, D_MODEL), F32)], axis=0)
    wdkv = jnp.concatenate([m_w_dkv[..., :KV_RANK], jnp.zeros((n_mla, D_MODEL, QK_NOPE_DIM), F32),
                            m_w_dkv[..., KV_RANK:], jnp.zeros((n_mla, D_MODEL, HEAD_PAD - QK_HEAD_DIM), F32)],
                           axis=-1)
    w_uq = jnp.pad(m_w_uq.reshape(n_mla, Q_RANK, N_HEADS, QK_HEAD_DIM),
                   ((0, 0), (0, 0), (0, 0), (0, HEAD_PAD - QK_HEAD_DIM))).reshape(n_mla, Q_RANK, QKV_W)

    def gain_col(g):
        return jnp.pad(g, ((0, 0), (0, HEAD_PAD - QK_HEAD_DIM)))[:, :, None]

    return {
        "mod": _modulation(cond, w_mod, b_mod),
        "rope_t": _rope_tables(),
        "norm_mix_g": norm_mix_g[:, None, :],
        "norm_ffn_g": norm_ffn_g[:, None, :],
        "g_w_in": g_w_in.astype(BF16),
        "g_b_in": g_b_in[:, None, :],
        "g_norm_v": g_norm_v[:, None, :],
        "g_w_s": g_w_s.astype(BF16),
        "g_b_st": jnp.swapaxes(g_b_s, 1, 2),
        "g_w_out": g_w_out.astype(BF16),
        "m_w_dq": m_w_dq.astype(BF16),
        "m_norm_q": m_norm_q[:, None, :],
        "m_w_uq_t": jnp.swapaxes(w_uq, 1, 2).astype(BF16),
        "m_gq": gain_col(m_qk_norm_q),
        "m_w_dkv": wdkv.astype(BF16),
        "m_norm_kv": m_norm_kv[:, None, :],
        "m_w_ukv_t": jnp.swapaxes(m_w_ukv, 1, 2).astype(BF16),
        "m_gk": gain_col(m_qk_norm_k),
        "m_w_o": m_w_o.astype(BF16),
        "cache_ckv": cache_ckv,
        "cache_krp": jnp.pad(cache_krope, ((0, 0), (0, 0), (0, 0), (QK_NOPE_DIM, HEAD_PAD - QK_HEAD_DIM))),
        "e_w_router": jnp.pad(e_w_router, ((0, 0), (0, 0), (0, LANE - N_EXPERTS))).astype(BF16),
        "e_b_router": jnp.pad(e_b_router, ((0, 0), (0, LANE - N_EXPERTS)), constant_values=NEG_BIG)[:, None, :],
        "tri": jnp.tri(ROUTE_TB, ROUTE_TB, -1, dtype=BF16),
        "e_w_gu": e_w_gu,
        "e_b_gu": e_b_gu.reshape(DEPTH, N_EXPERTS, D_FF // LANE, LANE, 2).swapaxes(3, 4).reshape(
            DEPTH, N_EXPERTS, 1, 2 * D_FF),
        "e_w_down": e_w_down,
        "e_b_down": e_b_down[:, :, None, :],
        "deinterleave": _deinterleave_matrix(),
    }


def kernel(x_prompt, x_sample, c, cache_ckv, cache_krope, c_ctx, norm_mix_g, norm_ffn_g, w_mod, b_mod,
           g_w_in, g_b_in, g_norm_v, g_w_s, g_b_s, g_w_out, m_w_dq, m_norm_q, m_w_uq, m_w_dkv,
           m_norm_kv, m_w_ukv, m_qk_norm_q, m_qk_norm_k, m_w_o, e_w_router, e_b_router, e_w_gu,
           e_b_gu, e_w_down, e_b_down):
    p = _prepare(c, cache_ckv, cache_krope, c_ctx, norm_mix_g, norm_ffn_g, w_mod, b_mod, g_w_in, g_b_in,
                 g_norm_v, g_w_s, g_b_s, g_w_out, m_w_dq, m_norm_q, m_w_uq, m_w_dkv, m_norm_kv, m_w_ukv,
                 m_qk_norm_q, m_qk_norm_k, m_w_o, e_w_router, e_b_router, e_w_gu, e_b_gu, e_w_down, e_b_down)
    assert MOE_PARTS == 2 and N_PROMPT == N_SAMPLE
    x_p, x_s = x_prompt.reshape(N_PROMPT, D_MODEL), x_sample.reshape(N_SAMPLE, D_MODEL)
    ckv_states, krope_states = [], []
    for layer in range(DEPTH):
        if layer % 2 == 0:
            x = _gmlp_layer(x_p, x_s, p, layer)
        else:
            x, s_ckv, s_krope = _mla_layer(x_p, x_s, p, layer)
            ckv_states.append(s_ckv)
            krope_states.append(s_krope)
        x_p, x_s = _moe_layer(x, p, layer)
    y_prompt = x_p.reshape(BATCH, SEQ, D_MODEL)
    y_sample = x_s.reshape(DEC_BATCH, DEC_SEQ, D_MODEL)
    return (y_prompt, y_sample, jnp.stack(ckv_states, axis=1), jnp.stack(krope_states, axis=1))
```

```python
import functools
import math

import jax
import jax.numpy as jnp
from jax import lax
from jax.experimental import pallas as pl
from jax.experimental.pallas import tpu as pltpu
from jax.experimental.pallas import tpu_sc as plsc

F32 = jnp.float32
BF16 = jnp.bfloat16

D_MODEL = 1024
BATCH = 32
SEQ = 256
DEPTH = 4
DEC_BATCH = 2
DEC_SEQ = 4096
PAST_LEN = 512
GRID_W = 64
RMS_EPS = 1e-6
GMLP_WIDTH = 2 * D_MODEL
GMLP_GROUPS = 8
GROUP_W = GMLP_WIDTH // GMLP_GROUPS
CHUNK = 128
N_HEADS = 16
QK_NOPE_DIM = 64
QK_ROPE_DIM = 32
QK_HEAD_DIM = QK_NOPE_DIM + QK_ROPE_DIM
V_HEAD_DIM = 64
Q_RANK = 256
KV_RANK = 128
ROPE_THETA = 10000.0
N_EXPERTS = 32
TOP_K = 4
D_FF = D_MODEL
SWIGLU_LIMIT = 7.0
SWIGLU_ALPHA = 1.702

N_PROMPT = BATCH * SEQ
N_SAMPLE = DEC_BATCH * DEC_SEQ
N_TOK = N_PROMPT + N_SAMPLE
N_COND = 1 + DEC_BATCH
KV_LEN = PAST_LEN + DEC_SEQ
N_KV_ROWS = DEC_BATCH * KV_LEN + N_PROMPT

LANE = 128
SUBLANE = 8
HEAD_PAD = LANE
QKV_W = N_HEADS * HEAD_PAD
VMEM_LIMIT = 56 * 1024 * 1024

TB = 512
WIDE_TB = 512
TQ = 256
TKC = 256
ATTN_CHUNKS_PER_ITER = 9
TM = 512
N_TILES = N_TOK * TOP_K // TM + N_EXPERTS
N_SLOTS = N_TILES * TM
PLAN_LANES = -(-N_TILES // LANE) * LANE
SLOT_TB = 2048
ROUTE_TB = 512
MOE_PARTS = 2
SC_CORES = 2
SC_WORKERS = SC_CORES * 16
ROPE_BLOCKS = DEC_SEQ // TB
NEG_BIG = -1e30


def _cparams(sem):
    return pltpu.CompilerParams(dimension_semantics=sem, vmem_limit_bytes=VMEM_LIMIT)


def _cond_of_block(i, tb):
    n_p = N_PROMPT // tb
    per = DEC_SEQ // tb
    return jnp.where(i < n_p, 0, 1 + (i - n_p) // per)


def _rms(x, g, n=None):
    n = x.shape[-1] if n is None else n
    ss = jnp.sum(x * x, axis=-1, keepdims=True) * (1.0 / n)
    return x * lax.rsqrt(ss + RMS_EPS) * g


def _dot(a, b):
    return jnp.dot(a, b, preferred_element_type=F32)


def _mod_kernel(c_ref, w_ref, b_ref, o_ref):
    c = c_ref[...]
    s = c * jax.nn.sigmoid(c)
    o_ref[0] = _dot(s.astype(BF16), w_ref[0].astype(BF16)) + b_ref[0]


def _modulation(cond, w_mod, b_mod):
    tn = 1536
    out = pl.pallas_call(
        _mod_kernel,
        grid=(DEPTH, 6 * D_MODEL // tn),
        in_specs=[
            pl.BlockSpec((SUBLANE, D_MODEL), lambda l, j: (0, 0)),
            pl.BlockSpec((1, D_MODEL, tn), lambda l, j: (l, 0, j)),
            pl.BlockSpec((1, 1, tn), lambda l, j: (l, 0, j)),
        ],
        out_specs=pl.BlockSpec((1, SUBLANE, tn), lambda l, j: (l, 0, j)),
        out_shape=jax.ShapeDtypeStruct((DEPTH, SUBLANE, 6 * D_MODEL), F32),
        compiler_params=_cparams(("parallel", "parallel")),
        name="adaln_mod",
    )(cond, w_mod, b_mod.reshape(DEPTH, 1, 6 * D_MODEL))
    m = out[:, :N_COND].reshape(DEPTH, N_COND, 6, D_MODEL)
    return jnp.pad(m, ((0, 0), (0, 0), (0, SUBLANE - 6), (0, 0)))


def _mod_spec(tb, layer, first_block=0):
    return pl.BlockSpec((None, 1, SUBLANE, D_MODEL),
                        lambda i: (layer, _cond_of_block(i + first_block, tb), 0, 0))


def _const_spec(shape):
    nd = len(shape)
    return pl.BlockSpec(shape, lambda *_: (0,) * nd)


def _layer_spec(shape, j):
    nd = len(shape)
    return pl.BlockSpec((None,) + tuple(shape), lambda *_: (j,) + (0,) * nd)


def _gelu_tanh(x):
    a = math.sqrt(2.0 / math.pi)
    hx = 0.5 * x
    return hx + hx * jnp.tanh(x * (a + (0.044715 * a) * (x * x)))


def _gmlp_kernel(xp_ref, xs_ref, mod_ref, g_ref, win_ref, bin_ref, gv_ref, ws_ref, bst_ref, wout_ref, o_ref):
    x = jnp.where(pl.program_id(0) < N_PROMPT // WIDE_TB, xp_ref[...], xs_ref[...])
    m = mod_ref[0]
    h = _rms(x, g_ref[...]) * (1.0 + m[1:2]) + m[0:1]
    hb = h.astype(BF16)
    zv = _gelu_tanh(_dot(hb, win_ref[:, GMLP_WIDTH:]) + bin_ref[:, GMLP_WIDTH:])
    vn = _rms(zv, gv_ref[...]).astype(BF16)
    rows = []
    for c in range(WIDE_TB // CHUNK):
        cols = []
        for g in range(GMLP_GROUPS):
            blk = vn[c * CHUNK:(c + 1) * CHUNK, g * GROUP_W:(g + 1) * GROUP_W]
            cols.append(_dot(ws_ref[g], blk) + bst_ref[:, g:g + 1])
        rows.append(jnp.concatenate(cols, axis=1))
    vm = jnp.concatenate(rows, axis=0)
    u = _gelu_tanh(_dot(hb, win_ref[:, :GMLP_WIDTH]) + bin_ref[:, :GMLP_WIDTH])
    d = _dot((u * vm).astype(BF16), wout_ref[...])
    o_ref[...] = x + m[2:3] * d


def _split_rows_specs(tb=TB):
    n_pb = N_PROMPT // tb
    return [pl.BlockSpec((tb, D_MODEL), lambda i: (jnp.minimum(i, n_pb - 1), 0)),
            pl.BlockSpec((tb, D_MODEL), lambda i: (jnp.maximum(i - n_pb, 0), 0))]


def _gmlp_layer(x_p, x_s, p, layer):
    j = layer // 2
    return pl.pallas_call(
        _gmlp_kernel,
        grid=(N_TOK // WIDE_TB,),
        in_specs=_split_rows_specs(WIDE_TB) + [
            _mod_spec(WIDE_TB, layer),
            _layer_spec((1, D_MODEL), layer),
            _layer_spec((D_MODEL, 2 * GMLP_WIDTH), j),
            _layer_spec((1, 2 * GMLP_WIDTH), j),
            _layer_spec((1, GMLP_WIDTH), j),
            _layer_spec((GMLP_GROUPS, CHUNK, CHUNK), j),
            _layer_spec((CHUNK, GMLP_GROUPS), j),
            _layer_spec((GMLP_WIDTH, D_MODEL), j),
        ],
        out_specs=pl.BlockSpec((WIDE_TB, D_MODEL), lambda i: (i, 0)),
        out_shape=jax.ShapeDtypeStruct((N_TOK, D_MODEL), F32),
        compiler_params=_cparams(("parallel",)),
        name="gmlp_mixer",
    )(x_p, x_s, p["mod"], p["norm_mix_g"], p["g_w_in"], p["g_b_in"], p["g_norm_v"], p["g_w_s"], p["g_b_st"],
      p["g_w_out"])


def _rope_tables():
    t = jnp.arange(DEC_SEQ)
    row_id = (t // GRID_W).astype(F32)
    col_id = (t % GRID_W).astype(F32)
    axis_dim = QK_ROPE_DIM // 2
    inv_freq = ROPE_THETA ** (-jnp.arange(0, axis_dim, 2, dtype=F32) / axis_dim)
    ang = jnp.stack([row_id[:, None] * inv_freq, col_id[:, None] * inv_freq], axis=1)
    cos, sin = jnp.cos(ang), jnp.sin(ang)
    zeros = jnp.zeros_like(sin)
    cos_l = jnp.concatenate([cos, cos], axis=-1).reshape(DEC_SEQ, QK_ROPE_DIM)
    s1_l = jnp.concatenate([-sin, zeros], axis=-1).reshape(DEC_SEQ, QK_ROPE_DIM)
    s2_l = jnp.concatenate([zeros, sin], axis=-1).reshape(DEC_SEQ, QK_ROPE_DIM)

    def widen(rope_part, nope_fill):
        left = jnp.full((DEC_SEQ, QK_NOPE_DIM), nope_fill, F32)
        right = jnp.zeros((DEC_SEQ, HEAD_PAD - QK_HEAD_DIM), F32)
        return jnp.concatenate([left, rope_part, right], axis=-1)

    pos = jnp.stack([widen(cos_l, 1.0), widen(s1_l, 0.0), widen(s2_l, 0.0)])
    ident_c = jnp.concatenate([jnp.ones((TB, QK_HEAD_DIM), F32),
                               jnp.zeros((TB, HEAD_PAD - QK_HEAD_DIM), F32)], axis=-1)
    ident = jnp.stack([ident_c, jnp.zeros_like(ident_c), jnp.zeros_like(ident_c)])
    return jnp.swapaxes(jnp.concatenate([pos, ident], axis=1), 1, 2)


def _dot_nt(a, b):
    return lax.dot_general(a, b, (((1,), (1,)), ((), ())), preferred_element_type=F32)


def _shift_rows(x, n):
    n = n % x.shape[0]
    return jnp.concatenate([x[n:], x[:n]], axis=0)


def _head_norm_rope(nope, rope, g_ref, tabt_ref):
    half = QK_ROPE_DIM // 4
    lo, hi = QK_NOPE_DIM, QK_HEAD_DIM
    ss = (jnp.sum(nope * nope, axis=0, keepdims=True) + jnp.sum(rope * rope, axis=0, keepdims=True))
    rs = lax.rsqrt(ss * (1.0 / QK_HEAD_DIM) + RMS_EPS)
    xn = nope * rs * g_ref[:lo, :]
    xr = rope * rs * g_ref[lo:hi, :]
    rot = (xr * tabt_ref[0, lo:hi, :] + _shift_rows(xr, half) * tabt_ref[1, lo:hi, :]
           + _shift_rows(xr, -half) * tabt_ref[2, lo:hi, :])
    return jnp.concatenate([xn, rot, jnp.zeros((HEAD_PAD - hi, nope.shape[1]), F32)], axis=0)


def _mla_proj_kernel(xp_ref, xs_ref, mod_ref, g_ref, wdq_ref, nq_ref, wuqt_ref, gq_ref, wdkv_ref, nkv_ref,
                     tabt_ref, qt_ref, ckv_ref, krp_ref):
    x = jnp.where(pl.program_id(0) < N_PROMPT // TB, xp_ref[...], xs_ref[...])
    m = mod_ref[0]
    h = _rms(x, g_ref[...]) * (1.0 + m[1:2]) + m[0:1]
    hb = h.astype(BF16)
    cq = _rms(_dot(hb, wdq_ref[...]), nq_ref[...])
    qt = _dot_nt(wuqt_ref[...], cq.astype(BF16))
    for hd in range(N_HEADS):
        r0 = hd * HEAD_PAD
        qh = _head_norm_rope(qt[r0:r0 + QK_NOPE_DIM, :], qt[r0 + QK_NOPE_DIM:r0 + QK_HEAD_DIM, :], gq_ref, tabt_ref)
        qt_ref[r0:r0 + HEAD_PAD, :] = qh.astype(BF16)
    kva = _dot(hb, wdkv_ref[...])
    ckv_ref[...] = _rms(kva[:, :KV_RANK], nkv_ref[...])
    krp_ref[...] = kva[:, KV_RANK:]


def _mla_proj(x_p, x_s, p, layer):
    j = layer // 2
    n_pb = N_PROMPT // TB

    def tab_idx(i):
        return (0, 0, jnp.where(i < n_pb, ROPE_BLOCKS, (i - n_pb) % ROPE_BLOCKS))

    return pl.pallas_call(
        _mla_proj_kernel,
        grid=(N_TOK // TB,),
        in_specs=_split_rows_specs() + [
            _mod_spec(TB, layer),
            _layer_spec((1, D_MODEL), layer),
            _layer_spec((D_MODEL, Q_RANK), j),
            _layer_spec((1, Q_RANK), j),
            _layer_spec((QKV_W, Q_RANK), j),
            _layer_spec((HEAD_PAD, 1), j),
            _layer_spec((D_MODEL, 2 * LANE), j),
            _layer_spec((1, KV_RANK), j),
            pl.BlockSpec((3, HEAD_PAD, TB), tab_idx),
        ],
        out_specs=[
            pl.BlockSpec((QKV_W, TB), lambda i: (0, i)),
            pl.BlockSpec((TB, KV_RANK), lambda i: (i, 0)),
            pl.BlockSpec((TB, HEAD_PAD), lambda i: (i, 0)),
        ],
        out_shape=[
            jax.ShapeDtypeStruct((QKV_W, N_TOK), BF16),
            jax.ShapeDtypeStruct((N_TOK, KV_RANK), F32),
            jax.ShapeDtypeStruct((N_TOK, HEAD_PAD), F32),
        ],
        compiler_params=_cparams(("parallel",)),
        name="mla_proj",
    )(x_p, x_s, p["mod"], p["norm_mix_g"], p["m_w_dq"], p["m_norm_q"], p["m_w_uq_t"], p["m_gq"], p["m_w_dkv"],
      p["m_norm_kv"], p["rope_t"])


def _kv_block_source(i):
    n_sb = DEC_BATCH * KV_LEN // TB
    per = KV_LEN // TB
    b = jnp.minimum(i // per, DEC_BATCH - 1)
    jj = i % per
    return jnp.logical_and(i < n_sb, jj < PAST_LEN // TB), b, jj


def _kv_expand_kernel(cckv_ref, ckrp_ref, ckv_ref, krp_ref, wukvt_ref, gk_ref, tabt_ref, k_ref, vt_ref):
    from_cache, _, _ = _kv_block_source(pl.program_id(0))
    ckv = jnp.where(from_cache, cckv_ref[...], ckv_ref[...])
    krp = jnp.where(from_cache, ckrp_ref[...], krp_ref[...])
    kvt = _dot_nt(wukvt_ref[...], ckv.astype(BF16))
    k_rope = krp.T[QK_NOPE_DIM:QK_HEAD_DIM, :]
    row = lax.broadcasted_iota(jnp.int32, (QK_NOPE_DIM, TB), 0)
    ones_row = jnp.where(row == 0, 1.0, 0.0)
    for hd in range(N_HEADS):
        r0 = hd * HEAD_PAD
        kh = _head_norm_rope(kvt[r0:r0 + QK_NOPE_DIM, :], k_rope, gk_ref, tabt_ref)
        k_ref[:, r0:r0 + HEAD_PAD] = kh.T.astype(BF16)
        vt_ref[r0:r0 + HEAD_PAD, :] = jnp.concatenate([ones_row, kvt[r0 + QK_NOPE_DIM:r0 + HEAD_PAD, :]],
                                                      axis=0).astype(BF16)


def _kv_expand(ckv, krp, p, layer):
    j = layer // 2
    n_sb = DEC_BATCH * KV_LEN // TB
    n_cache = PAST_LEN // TB
    n_pb = N_PROMPT // TB
    lat_blocks = DEC_SEQ // TB

    def tab_idx(i):
        from_cache, _, jj = _kv_block_source(i)
        return (0, 0, jnp.where((i >= n_sb) | from_cache, ROPE_BLOCKS, jj - n_cache))

    def cache_idx(i):
        _, b, jj = _kv_block_source(i)
        return (b, j, jnp.minimum(jj, n_cache - 1), 0)

    def tok_idx(i):
        _, b, jj = _kv_block_source(i)
        latent = n_pb + b * lat_blocks + jnp.maximum(jj - n_cache, 0)
        return (jnp.where(i < n_sb, latent, i - n_sb), 0)

    return pl.pallas_call(
        _kv_expand_kernel,
        grid=(N_KV_ROWS // TB,),
        in_specs=[
            pl.BlockSpec((None, None, TB, KV_RANK), cache_idx),
            pl.BlockSpec((None, None, TB, HEAD_PAD), cache_idx),
            pl.BlockSpec((TB, KV_RANK), tok_idx),
            pl.BlockSpec((TB, HEAD_PAD), tok_idx),
            _layer_spec((QKV_W, KV_RANK), j),
            _layer_spec((HEAD_PAD, 1), j),
            pl.BlockSpec((3, HEAD_PAD, TB), tab_idx),
        ],
        out_specs=[
            pl.BlockSpec((TB, QKV_W), lambda i: (i, 0)),
            pl.BlockSpec((QKV_W, TB), lambda i: (0, i)),
        ],
        out_shape=[
            jax.ShapeDtypeStruct((N_KV_ROWS, QKV_W), BF16),
            jax.ShapeDtypeStruct((QKV_W, N_KV_ROWS), BF16),
        ],
        compiler_params=_cparams(("parallel",)),
        name="mla_kv_expand",
    )(p["cache_ckv"], p["cache_krp"], ckv, krp, p["m_w_ukv_t"], p["m_gk"], p["rope_t"])


def _attn_kernel(qt_ref, k_ref, vt_ref, o_ref, s_a, s_b, *, t_k, hps, qps):
    c = (1.0 / math.sqrt(QK_HEAD_DIM)) * math.log2(math.e)
    kc = min(TKC, t_k)
    n_chunks = t_k // kc
    per_it = min(ATTN_CHUNKS_PER_ITER, n_chunks)
    n_it = n_chunks // per_it
    bufs = (s_a, s_b)

    units = [(qb, h) for qb in range(qps) for h in range(hps)]

    def rows(h):
        return slice(h * HEAD_PAD, (h + 1) * HEAD_PAD)

    def cols(qb):
        return slice(qb * TQ, (qb + 1) * TQ)

    def scores(u, off, m8):
        qb, h = units[u]
        st = _dot(k_ref[pl.ds(off, kc), rows(h)], qt_ref[rows(h), cols(qb)])
        bufs[u % 2][pl.ds(off, kc), :] = st
        return jnp.maximum(m8, jnp.max(st.reshape(kc // SUBLANE, SUBLANE, TQ), axis=0))

    def weigh(u, off, m, acc):
        pt = jnp.exp2((bufs[u % 2][pl.ds(off, kc), :] - m) * c).astype(BF16)
        return acc + _dot(vt_ref[rows(units[u][1]), pl.ds(off, kc)], pt)

    def phase(u_scores, u_weigh, m):
        def body(it, carry):
            m8, acc = carry
            for j in range(per_it):
                off = (it * per_it + j) * kc
                off = off if isinstance(off, int) else pl.multiple_of(off, kc)
                if u_scores is not None:
                    m8 = scores(u_scores, off, m8)
                if u_weigh is not None:
                    acc = weigh(u_weigh, off, m, acc)
            return m8, acc

        init = (jnp.full((SUBLANE, TQ), NEG_BIG, F32), jnp.zeros((HEAD_PAD, TQ), F32))
        return body(0, init) if n_it == 1 else lax.fori_loop(0, n_it, body, init)

    outs = []
    if n_chunks == 1:
        sts = [_dot(k_ref[:, rows(h)], qt_ref[rows(h), cols(qb)]) for qb, h in units]
        for (qb, h), st in zip(units, sts):
            m = jnp.max(st, axis=0, keepdims=True)
            acc = _dot(vt_ref[rows(h), :], jnp.exp2((st - m) * c).astype(BF16))
            outs.append(acc[QK_NOPE_DIM:, :] / acc[0:1, :])
    else:
        m8, _ = phase(0, None, None)
        for u in range(1, len(units) + 1):
            m = jnp.max(m8, axis=0, keepdims=True)
            m8, acc = phase(u if u < len(units) else None, u - 1, m)
            outs.append(acc[QK_NOPE_DIM:, :] / acc[0:1, :])
    for qb in range(qps):
        o_ref[cols(qb), :] = jnp.concatenate(outs[qb * hps:(qb + 1) * hps], axis=0).T.astype(BF16)


def _attention(qt, k, vt, *, n_batch, t_q, t_k, q_row0, kv_row0, hps, qps):
    tq = TQ * qps
    nq = t_q // tq
    q0 = q_row0 // tq
    k0 = kv_row0 // t_k
    return pl.pallas_call(
        functools.partial(_attn_kernel, t_k=t_k, hps=hps, qps=qps),
        grid=(n_batch, N_HEADS // hps, nq),
        in_specs=[
            pl.BlockSpec((hps * HEAD_PAD, tq), lambda b, h, i: (h, q0 + b * nq + i)),
            pl.BlockSpec((t_k, hps * HEAD_PAD), lambda b, h, i: (k0 + b, h)),
            pl.BlockSpec((hps * HEAD_PAD, t_k), lambda b, h, i: (h, k0 + b)),
        ],
        out_specs=pl.BlockSpec((tq, hps * V_HEAD_DIM), lambda b, h, i: (b * nq + i, h)),
        out_shape=jax.ShapeDtypeStruct((n_batch * t_q, N_HEADS * V_HEAD_DIM), BF16),
        scratch_shapes=[pltpu.VMEM((t_k, TQ), F32), pltpu.VMEM((t_k, TQ), F32)],
        compiler_params=_cparams(("parallel", "parallel", "parallel")),
        name=f"mla_attention_tk{t_k}",
    )(qt, k, vt)


def _out_proj_kernel(ap_ref, as_ref, xp_ref, xs_ref, mod_ref, wo_ref, o_ref):
    is_prompt = pl.program_id(0) < N_PROMPT // WIDE_TB
    a = jnp.where(is_prompt, ap_ref[...], as_ref[...])
    x = jnp.where(is_prompt, xp_ref[...], xs_ref[...])
    o_ref[...] = x + mod_ref[0][2:3] * _dot(a, wo_ref[...])


def _out_proj(attn_p, attn_s, x_p, x_s, p, layer):
    return pl.pallas_call(
        _out_proj_kernel,
        grid=(N_TOK // WIDE_TB,),
        in_specs=_split_rows_specs(WIDE_TB) + _split_rows_specs(WIDE_TB) + [
            _mod_spec(WIDE_TB, layer),
            _layer_spec((D_MODEL, D_MODEL), layer // 2),
        ],
        out_specs=pl.BlockSpec((WIDE_TB, D_MODEL), lambda i: (i, 0)),
        out_shape=jax.ShapeDtypeStruct((N_TOK, D_MODEL), F32),
        compiler_params=_cparams(("parallel",)),
        name="mla_out_proj",
    )(attn_p, attn_s, x_p, x_s, p["mod"], p["m_w_o"])


def _mla_layer(x_p, x_s, p, layer):
    qt, ckv, krp = _mla_proj(x_p, x_s, p, layer)
    k, vt = _kv_expand(ckv, krp, p, layer)
    a_p = _attention(qt, k, vt, n_batch=BATCH, t_q=SEQ, t_k=SEQ, q_row0=0, kv_row0=DEC_BATCH * KV_LEN,
                     hps=N_HEADS, qps=1)
    a_s = _attention(qt, k, vt, n_batch=DEC_BATCH, t_q=DEC_SEQ, t_k=KV_LEN, q_row0=N_PROMPT, kv_row0=0,
                     hps=4, qps=2)
    x = _out_proj(a_p, a_s, x_p, x_s, p, layer)
    state_ckv = ckv[:N_PROMPT].reshape(BATCH, SEQ, KV_RANK)
    state_krope = krp[:N_PROMPT, QK_NOPE_DIM:QK_HEAD_DIM].reshape(BATCH, SEQ, QK_ROPE_DIM)
    return x, state_ckv, state_krope


def _tile_plan(cnt8, plan_ref, start_ref):
    tm = float(TM)
    cnt_row = cnt8[0:1, :]
    cnt_col = cnt8.T[:, 0:1]
    tiles_row = jnp.floor((cnt_row + (tm - 1.0)) * (1.0 / tm))
    tiles_col = jnp.floor((cnt_col + (tm - 1.0)) * (1.0 / tm))
    sub = lax.broadcasted_iota(jnp.int32, (LANE, LANE), 0).astype(F32)
    lan = lax.broadcasted_iota(jnp.int32, (LANE, LANE), 1).astype(F32)
    tile_end_row = jnp.sum(jnp.where(sub <= lan, tiles_col, 0.0), axis=0, keepdims=True)
    tile_end_col = jnp.sum(jnp.where(lan <= sub, tiles_row, 0.0), axis=1, keepdims=True)
    n_used = jnp.max(tile_end_row, axis=1, keepdims=True)
    start_col = (tile_end_col - tiles_col) * tm
    end_col = start_col + cnt_col
    cand = jnp.where(jnp.logical_and(lan > sub, tiles_row > 0.0), lan, float(LANE))
    next_col = jnp.min(cand, axis=1, keepdims=True)
    next_col = jnp.where(next_col < float(LANE), next_col, -1.0)
    n_lanes = plan_ref.shape[1]
    tidx = jnp.minimum(lax.broadcasted_iota(jnp.int32, (LANE, n_lanes), 1).astype(F32), n_used - 1.0)
    esub = lax.broadcasted_iota(jnp.int32, (LANE, n_lanes), 0).astype(F32)
    te_row = jnp.sum(jnp.where(tile_end_col <= tidx, 1.0, 0.0), axis=0, keepdims=True)
    mine = esub == te_row
    end_at = jnp.sum(jnp.where(mine, end_col, 0.0), axis=0, keepdims=True)
    tv_row = jnp.clip(end_at - tidx[0:1, :] * tm, 0.0, tm)
    nx_row = jnp.sum(jnp.where(mine, next_col, 0.0), axis=0, keepdims=True)
    nu_row = jnp.broadcast_to(n_used, (1, n_lanes))
    plan_ref[...] = jnp.concatenate([te_row, tv_row, nx_row, nu_row, jnp.zeros((SUBLANE - 4, n_lanes), F32)], axis=0)
    start_ref[...] = jnp.broadcast_to(start_col, (LANE, LANE))


def _route_kernel(x_ref, mod_ref, g_ref, wr_ref, br_ref, tri_ref, h_ref, meta_ref, metat_ref, plan_ref, start_ref,
                  carry):
    i = pl.program_id(0)

    @pl.when(i == 0)
    def _():
        carry[...] = jnp.zeros_like(carry)

    x = x_ref[...]
    m = mod_ref[0]
    h = _rms(x, g_ref[...]) * (1.0 + m[4:5]) + m[3:4]
    h_ref[...] = _pack_bf16_pairs(h)
    logits = _dot(h.astype(BF16), wr_ref[...]) + br_ref[...]
    lane = lax.broadcasted_iota(jnp.int32, logits.shape, 1).astype(F32)
    work = logits
    sel = jnp.zeros(logits.shape, F32)
    hits, tops = [], []
    for k in range(TOP_K):
        mk = jnp.max(work, axis=-1, keepdims=True)
        first = jnp.min(jnp.where(work == mk, lane, float(LANE)), axis=-1, keepdims=True)
        hit = lane == first
        sel = jnp.where(hit, 1.0, sel)
        work = jnp.where(hit, -jnp.inf, work)
        hits.append((hit, first))
        tops.append(mk)
    es = [jnp.exp(t - tops[0]) for t in tops]
    denom = es[0] + es[1] + es[2] + es[3]
    pos = _dot(tri_ref[...], sel.astype(BF16)) + carry[0:1, :]
    carry[...] = carry[...] + jnp.sum(sel, axis=0, keepdims=True)

    @pl.when(i == pl.num_programs(0) - 1)
    def _():
        _tile_plan(carry[...], plan_ref, start_ref)

    meta = jnp.zeros(logits.shape, F32)
    for k in range(TOP_K):
        hit, first = hits[k]
        pk = jnp.sum(jnp.where(hit, pos, 0.0), axis=-1, keepdims=True)
        meta = jnp.where(lane == float(k), first, meta)
        meta = jnp.where(lane == float(TOP_K + k), es[k] / denom, meta)
        meta = jnp.where(lane == float(2 * TOP_K + k), pk, meta)
    meta_ref[...] = meta
    metat_ref[...] = meta.T[:2 * SUBLANE, :]


def _route(x, p, layer):
    tb = ROUTE_TB
    return pl.pallas_call(
        _route_kernel,
        grid=(N_TOK // tb,),
        in_specs=[
            pl.BlockSpec((tb, D_MODEL), lambda i: (i, 0)),
            _mod_spec(tb, layer),
            _layer_spec((1, D_MODEL), layer),
            _layer_spec((D_MODEL, LANE), layer),
            _layer_spec((1, LANE), layer),
            _const_spec((tb, tb)),
        ],
        out_specs=[
            pl.BlockSpec((tb, D_MODEL // 2), lambda i: (i, 0)),
            pl.BlockSpec((tb, LANE), lambda i: (i, 0)),
            pl.BlockSpec((2 * SUBLANE, tb), lambda i: (0, i)),
            _const_spec((SUBLANE, PLAN_LANES)),
            _const_spec((LANE, LANE)),
        ],
        out_shape=[
            jax.ShapeDtypeStruct((N_TOK, D_MODEL // 2), jnp.uint32),
            jax.ShapeDtypeStruct((N_TOK, LANE), F32),
            jax.ShapeDtypeStruct((2 * SUBLANE, N_TOK), F32),
            jax.ShapeDtypeStruct((SUBLANE, PLAN_LANES), F32),
            jax.ShapeDtypeStruct((LANE, LANE), F32),
        ],
        scratch_shapes=[pltpu.VMEM((SUBLANE, LANE), F32)],
        compiler_params=_cparams(("arbitrary",)),
        name="moe_route",
    )(x, p["mod"], p["norm_ffn_g"], p["e_w_router"], p["e_b_router"], p["tri"])


def _slots_kernel(start_ref, metat_ref, dest_ref):
    start_col = start_ref[:, 0:1]
    esub = lax.broadcasted_iota(jnp.int32, (LANE, SLOT_TB), 0).astype(F32)
    rows = []
    for k in range(TOP_K):
        e = metat_ref[k:k + 1, :]
        first = jnp.sum(jnp.where(esub == e, start_col, 0.0), axis=0, keepdims=True)
        rows.append(first + metat_ref[2 * TOP_K + k:2 * TOP_K + k + 1, :])
    dest_ref[...] = jnp.concatenate(rows, axis=0).astype(jnp.int32)


def _slots(start, meta_t):
    return pl.pallas_call(
        _slots_kernel,
        grid=(N_TOK // SLOT_TB,),
        in_specs=[
            _const_spec((LANE, LANE)),
            pl.BlockSpec((2 * SUBLANE, SLOT_TB), lambda i: (0, i)),
        ],
        out_specs=pl.BlockSpec((TOP_K, SLOT_TB), lambda i: (0, i)),
        out_shape=jax.ShapeDtypeStruct((TOP_K, N_TOK), jnp.int32),
        compiler_params=_cparams(("parallel",)),
        name="moe_slots",
    )(start, meta_t)


def _sc_gather(table, idx, ch):
    b, w = idx.shape[0], table.shape[1]
    per_w = b // SC_WORKERS
    n_ch = per_w // ch
    assert per_w * SC_WORKERS == b and n_ch * ch == per_w and n_ch % 2 == 0
    mesh = plsc.VectorSubcoreMesh(core_axis_name="c", subcore_axis_name="s")

    @functools.partial(
        pl.kernel, mesh=mesh,
        out_type=jax.ShapeDtypeStruct((b, w), table.dtype),
        scratch_types=[
            pltpu.VMEM((n_ch, ch), jnp.int32),
            pltpu.VMEM((ch, w), table.dtype),
            pltpu.VMEM((ch, w), table.dtype),
            pltpu.SemaphoreType.DMA, pltpu.SemaphoreType.DMA,
            pltpu.SemaphoreType.DMA, pltpu.SemaphoreType.DMA,
        ],
        name="sc_row_gather",
    )
    def gather_rows(table_hbm, idx_hbm, out_hbm, idx_v, buf0, buf1, g0, g1, s0, s1):
        wid = lax.axis_index("s") * SC_CORES + lax.axis_index("c")
        base = wid * per_w
        pltpu.sync_copy(idx_hbm.at[wid], idx_v)

        def gather(j, buf, sem):
            return pltpu.make_async_copy(table_hbm.at[idx_v.at[j]], buf, sem)

        def store(j, buf, sem):
            return pltpu.make_async_copy(buf, out_hbm.at[pl.ds(base + j * ch, ch)], sem)

        gather(0, buf0, g0).start()

        @pl.loop(0, n_ch, step=2)
        def _(j):
            @pl.when(j > 0)
            def _():
                store(j - 1, buf1, s1).wait()

            gather(j + 1, buf1, g1).start()
            gather(j, buf0, g0).wait()
            store(j, buf0, s0).start()
            gather(j + 1, buf1, g1).wait()
            store(j + 1, buf1, s1).start()
            store(j, buf0, s0).wait()

            @pl.when(j + 2 < n_ch)
            def _():
                gather(j + 2, buf0, g0).start()

        store(n_ch - 1, buf1, s1).wait()

    return gather_rows(table, idx.reshape(SC_WORKERS, n_ch, ch))


def _sc_dispatch(rows, dest_t, ch):
    n, w = rows.shape
    per_w = n // SC_WORKERS
    n_ch = per_w // ch
    assert per_w * SC_WORKERS == n and n_ch * ch == per_w and n_ch % 2 == 0
    mesh = plsc.VectorSubcoreMesh(core_axis_name="c", subcore_axis_name="s")
    idx = dest_t.reshape(TOP_K, SC_WORKERS, n_ch, ch)

    @functools.partial(
        pl.kernel, mesh=mesh,
        out_type=jax.ShapeDtypeStruct((N_SLOTS, w), rows.dtype),
        scratch_types=[
            pltpu.VMEM((TOP_K * n_ch, ch), jnp.int32),
            pltpu.VMEM((ch, w), rows.dtype),
            pltpu.VMEM((ch, w), rows.dtype),
            pltpu.SemaphoreType.DMA, pltpu.SemaphoreType.DMA,
            pltpu.SemaphoreType.DMA, pltpu.SemaphoreType.DMA,
        ],
        name="sc_row_dispatch",
    )
    def dispatch_rows(rows_hbm, idx_hbm, out_hbm, idx_v, buf0, buf1, l0, l1, s0, s1):
        wid = lax.axis_index("s") * SC_CORES + lax.axis_index("c")
        base = wid * per_w
        for k in range(TOP_K):
            pltpu.sync_copy(idx_hbm.at[k, wid], idx_v.at[pl.ds(k * n_ch, n_ch)])

        def load(j, buf, sem):
            return pltpu.make_async_copy(rows_hbm.at[pl.ds(base + j * ch, ch)], buf, sem)

        def scatter(j, k, buf, sem):
            return pltpu.make_async_copy(buf, out_hbm.at[idx_v.at[k * n_ch + j]], sem)

        load(0, buf0, l0).start()

        @pl.loop(0, n_ch, step=2)
        def _(j):
            load(j + 1, buf1, l1).start()
            load(j, buf0, l0).wait()
            for k in range(TOP_K):
                scatter(j, k, buf0, s0).start()
            load(j + 1, buf1, l1).wait()
            for k in range(TOP_K):
                scatter(j + 1, k, buf1, s1).start()
            for k in range(TOP_K):
                scatter(j, k, buf0, s0).wait()

            @pl.when(j + 2 < n_ch)
            def _():
                load(j + 2, buf0, l0).start()

            for k in range(TOP_K):
                scatter(j + 1, k, buf1, s1).wait()

    return dispatch_rows(rows, idx)


def _deinterleave_matrix():
    src = jnp.arange(2 * LANE)[:, None]
    dst = jnp.arange(2 * LANE)[None, :]
    want = jnp.where(dst < LANE, 2 * dst, 2 * (dst - LANE) + 1)
    return (src == want).astype(BF16)


def _expert_kernel(te_ref, nu_ref, tv_ref, nx_ref, x_ref, wgu_hbm, bgu_ref, wd_hbm, bd_ref, perm_ref, o_ref,
                   wgu_st, wd_st, wgu_bf, wd_bf, sems, *, layer):
    i = pl.program_id(0)
    prev = te_ref[jnp.maximum(i - 1, 0)]
    fresh = jnp.logical_or(i == 0, te_ref[i] != prev)

    def fetch(e):
        return (pltpu.make_async_copy(wgu_hbm.at[layer, e], wgu_st, sems.at[0]),
                pltpu.make_async_copy(wd_hbm.at[layer, e], wd_st, sems.at[1]))

    @pl.when(i == 0)
    def _():
        for cp in fetch(te_ref[0]):
            cp.start()

    @pl.when(jnp.logical_and(fresh, i < nu_ref[0]))
    def _():
        for cp in fetch(te_ref[i]):
            cp.wait()
        for b in range(2 * D_FF // (2 * LANE)):
            sl = slice(b * 2 * LANE, (b + 1) * 2 * LANE)
            wgu_bf[:, sl] = _dot(wgu_st[:, sl].astype(BF16), perm_ref[...]).astype(BF16)
        wd_bf[...] = wd_st[...].astype(BF16)

        @pl.when(nx_ref[i] >= 0)
        def _():
            for cp in fetch(nx_ref[i]):
                cp.start()

    @pl.when(i < nu_ref[0])
    def _():
        row = lax.broadcasted_iota(jnp.int32, (TM, D_MODEL // 2), 0)
        w = jnp.where(row < tv_ref[i], x_ref[...], jnp.uint32(0))
        x = _unpack_bf16_pairs(w).astype(BF16)
        gu = _dot(x, wgu_bf[...]) + bgu_ref[...]
        acts = []
        for b in range(D_FF // LANE):
            glu = jnp.minimum(gu[:, b * 2 * LANE:b * 2 * LANE + LANE], SWIGLU_LIMIT)
            lin = jnp.clip(gu[:, b * 2 * LANE + LANE:(b + 1) * 2 * LANE], -SWIGLU_LIMIT, SWIGLU_LIMIT)
            acts.append((glu * jax.nn.sigmoid(SWIGLU_ALPHA * glu) * (lin + 1.0)).astype(BF16))
        act = jnp.concatenate(acts, axis=1)
        o_ref[...] = _pack_bf16_pairs(_dot(act, wd_bf[...]) + bd_ref[...])


def _experts(buf, tile_expert, n_used, tile_valid, tile_next, p, layer):
    def row_idx(i, te, nu, tv, nx):
        return (jnp.minimum(i, nu[0] - 1), 0)

    def b_idx(i, te, nu, tv, nx):
        return (layer, te[i], 0, 0)

    grid_spec = pltpu.PrefetchScalarGridSpec(
        num_scalar_prefetch=4,
        grid=(N_TILES,),
        in_specs=[
            pl.BlockSpec((TM, D_MODEL // 2), row_idx),
            pl.BlockSpec(memory_space=pl.ANY),
            pl.BlockSpec((None, None, 1, 2 * D_FF), b_idx),
            pl.BlockSpec(memory_space=pl.ANY),
            pl.BlockSpec((None, None, 1, D_MODEL), b_idx),
            _const_spec((2 * LANE, 2 * LANE)),
        ],
        out_specs=pl.BlockSpec((TM, D_MODEL // 2), row_idx),
        scratch_shapes=[
            pltpu.VMEM((D_MODEL, 2 * D_FF), F32),
            pltpu.VMEM((D_FF, D_MODEL), F32),
            pltpu.VMEM((D_MODEL, 2 * D_FF), BF16),
            pltpu.VMEM((D_FF, D_MODEL), BF16),
            pltpu.SemaphoreType.DMA((2,)),
        ],
    )
    return pl.pallas_call(
        functools.partial(_expert_kernel, layer=layer),
        grid_spec=grid_spec,
        out_shape=jax.ShapeDtypeStruct((N_SLOTS, D_MODEL // 2), jnp.uint32),
        compiler_params=_cparams(("arbitrary",)),
        name="moe_experts",
    )(tile_expert, n_used, tile_valid, tile_next, buf, p["e_w_gu"], p["e_b_gu"], p["e_w_down"], p["e_b_down"],
      p["deinterleave"])


def _pack_bf16_pairs(v):
    half = v.shape[1] // 2
    bits = pltpu.bitcast(v.astype(BF16).astype(F32), jnp.uint32)
    return (bits[:, half:] & jnp.uint32(0xFFFF0000)) | (bits[:, :half] >> 16)


def _unpack_bf16_pairs(w):
    return jnp.concatenate([pltpu.bitcast(w << 16, F32), pltpu.bitcast(w & jnp.uint32(0xFFFF0000), F32)],
                           axis=1)


def _combine_kernel(x_ref, mod_ref, y_ref, w_ref, o_ref):
    w = w_ref[:, TOP_K:2 * TOP_K]
    y = _unpack_bf16_pairs(y_ref[0]) * w[:, 0:1]
    for k in range(1, TOP_K):
        y = y + _unpack_bf16_pairs(y_ref[k]) * w[:, k:k + 1]
    o_ref[...] = x_ref[...] + mod_ref[0][5:6] * y


def _combine(x, y4, meta, p, layer, part):
    n_rows = N_TOK // MOE_PARTS
    tb = WIDE_TB
    first = part * n_rows // tb
    return pl.pallas_call(
        _combine_kernel,
        grid=(n_rows // tb,),
        in_specs=[
            pl.BlockSpec((tb, D_MODEL), lambda i: (i + first, 0)),
            _mod_spec(tb, layer, first),
            pl.BlockSpec((TOP_K, tb, D_MODEL // 2), lambda i: (0, i, 0)),
            pl.BlockSpec((tb, LANE), lambda i: (i + first, 0)),
        ],
        out_specs=pl.BlockSpec((tb, D_MODEL), lambda i: (i, 0)),
        out_shape=jax.ShapeDtypeStruct((n_rows, D_MODEL), F32),
        compiler_params=_cparams(("parallel",)),
        name="moe_combine",
    )(x, p["mod"], y4, meta)


def _moe_layer(x, p, layer):
    hp, meta, meta_t, plan, start = _route(x, p, layer)
    plan = plan[:4, :N_TILES].astype(jnp.int32)
    tile_expert, tile_valid, tile_next, n_used = plan[0], plan[1], plan[2], plan[3, :1]
    dest_t = _slots(start, meta_t)
    buf = _sc_dispatch(hp, dest_t, 64)
    yb = _experts(buf, tile_expert, n_used, tile_valid, tile_next, p, layer)
    n_rows = N_TOK // MOE_PARTS
    outs = []
    for part in range(MOE_PARTS):
        idx = dest_t[:, part * n_rows:(part + 1) * n_rows].reshape(-1)
        y4 = _sc_gather(yb, idx, 64).reshape(TOP_K, n_rows, D_MODEL // 2)
        outs.append(_combine(x, y4, meta, p, layer, part))
    return outs


def _prepare(c, cache_ckv, cache_krope, c_ctx, norm_mix_g, norm_ffn_g, w_mod, b_mod, g_w_in, g_b_in, g_norm_v,
             g_w_s, g_b_s, g_w_out, m_w_dq, m_norm_q, m_w_uq, m_w_dkv, m_norm_kv, m_w_ukv, m_qk_norm_q,
             m_qk_norm_k, m_w_o, e_w_router, e_b_router, e_w_gu, e_b_gu, e_w_down, e_b_down):
    n_mla = m_w_dq.shape[0]
    cond = jnp.concatenate([c_ctx[None, :], c, jnp.zeros((SUBLANE - N_COND, D_MODEL), F32)], axis=0)
    wdkv = jnp.concatenate([m_w_dkv[..., :KV_RANK], jnp.zeros((n_mla, D_MODEL, QK_NOPE_DIM), F32),
                            m_w_dkv[..., KV_RANK:], jnp.zeros((n_mla, D_MODEL, HEAD_PAD - QK_HEAD_DIM), F32)],
                           axis=-1)
    w_uq = jnp.pad(m_w_uq.reshape(n_mla, Q_RANK, N_HEADS, QK_HEAD_DIM),
                   ((0, 0), (0, 0), (0, 0), (0, HEAD_PAD - QK_HEAD_DIM))).reshape(n_mla, Q_RANK, QKV_W)

    def gain_col(g):
        return jnp.pad(g, ((0, 0), (0, HEAD_PAD - QK_HEAD_DIM)))[:, :, None]

    return {
        "mod": _modulation(cond, w_mod, b_mod),
        "rope_t": _rope_tables(),
        "norm_mix_g": norm_mix_g[:, None, :],
        "norm_ffn_g": norm_ffn_g[:, None, :],
        "g_w_in": g_w_in.astype(BF16),
        "g_b_in": g_b_in[:, None, :],
        "g_norm_v": g_norm_v[:, None, :],
        "g_w_s": g_w_s.astype(BF16),
        "g_b_st": jnp.swapaxes(g_b_s, 1, 2),
        "g_w_out": g_w_out.astype(BF16),
        "m_w_dq": m_w_dq.astype(BF16),
        "m_norm_q": m_norm_q[:, None, :],
        "m_w_uq_t": jnp.swapaxes(w_uq, 1, 2).astype(BF16),
        "m_gq": gain_col(m_qk_norm_q),
        "m_w_dkv": wdkv.astype(BF16),
        "m_norm_kv": m_norm_kv[:, None, :],
        "m_w_ukv_t": jnp.swapaxes(m_w_ukv, 1, 2).astype(BF16),
        "m_gk": gain_col(m_qk_norm_k),
        "m_w_o": m_w_o.astype(BF16),
        "cache_ckv": cache_ckv,
        "cache_krp": jnp.pad(cache_krope, ((0, 0), (0, 0), (0, 0), (QK_NOPE_DIM, HEAD_PAD - QK_HEAD_DIM))),
        "e_w_router": jnp.pad(e_w_router, ((0, 0), (0, 0), (0, LANE - N_EXPERTS))).astype(BF16),
        "e_b_router": jnp.pad(e_b_router, ((0, 0), (0, LANE - N_EXPERTS)), constant_values=NEG_BIG)[:, None, :],
        "tri": jnp.tri(ROUTE_TB, ROUTE_TB, -1, dtype=BF16),
        "e_w_gu": e_w_gu,
        "e_b_gu": e_b_gu.reshape(DEPTH, N_EXPERTS, D_FF // LANE, LANE, 2).swapaxes(3, 4).reshape(
            DEPTH, N_EXPERTS, 1, 2 * D_FF),
        "e_w_down": e_w_down,
        "e_b_down": e_b_down[:, :, None, :],
        "deinterleave": _deinterleave_matrix(),
    }


def kernel(x_prompt, x_sample, c, cache_ckv, cache_krope, c_ctx, norm_mix_g, norm_ffn_g, w_mod, b_mod,
           g_w_in, g_b_in, g_norm_v, g_w_s, g_b_s, g_w_out, m_w_dq, m_norm_q, m_w_uq, m_w_dkv,
           m_norm_kv, m_w_ukv, m_qk_norm_q, m_qk_norm_k, m_w_o, e_w_router, e_b_router, e_w_gu,
           e_b_gu, e_w_down, e_b_down):
    p = _prepare(c, cache_ckv, cache_krope, c_ctx, norm_mix_g, norm_ffn_g, w_mod, b_mod, g_w_in, g_b_in,
                 g_norm_v, g_w_s, g_b_s, g_w_out, m_w_dq, m_norm_q, m_w_uq, m_w_dkv, m_norm_kv, m_w_ukv,
                 m_qk_norm_q, m_qk_norm_k, m_w_o, e_w_router, e_b_router, e_w_gu, e_b_gu, e_w_down, e_b_down)
    assert MOE_PARTS == 2 and N_PROMPT == N_SAMPLE
    x_p, x_s = x_prompt.reshape(N_PROMPT, D_MODEL), x_sample.reshape(N_SAMPLE, D_MODEL)
    ckv_states, krope_states = [], []
    for layer in range(DEPTH):
        if layer % 2 == 0:
            x = _gmlp_layer(x_p, x_s, p, layer)
        else:
            x, s_ckv, s_krope = _mla_layer(x_p, x_s, p, layer)
            ckv_states.append(s_ckv)
            krope_states.append(s_krope)
        x_p, x_s = _moe_layer(x, p, layer)
    y_prompt = x_p.reshape(BATCH, SEQ, D_MODEL)
    y_sample = x_s.reshape(DEC_BATCH, DEC_SEQ, D_MODEL)
    return (y_prompt, y_sample, jnp.stack(ckv_states, axis=1), jnp.stack(krope_states, axis=1))
```

```python
import functools
import math

import jax
import jax.numpy as jnp
from jax import lax
from jax.experimental import pallas as pl
from jax.experimental.pallas import tpu as pltpu
from jax.experimental.pallas import tpu_sc as plsc

F32 = jnp.float32
BF16 = jnp.bfloat16

D_MODEL = 1024
BATCH = 32
SEQ = 256
DEPTH = 4
DEC_BATCH = 2
DEC_SEQ = 4096
PAST_LEN = 512
GRID_W = 64
RMS_EPS = 1e-6
GMLP_WIDTH = 2 * D_MODEL
GMLP_GROUPS = 8
GROUP_W = GMLP_WIDTH // GMLP_GROUPS
CHUNK = 128
N_HEADS = 16
QK_NOPE_DIM = 64
QK_ROPE_DIM = 32
QK_HEAD_DIM = QK_NOPE_DIM + QK_ROPE_DIM
V_HEAD_DIM = 64
Q_RANK = 256
KV_RANK = 128
ROPE_THETA = 10000.0
N_EXPERTS = 32
TOP_K = 4
D_FF = D_MODEL
SWIGLU_LIMIT = 7.0
SWIGLU_ALPHA = 1.702

N_PROMPT = BATCH * SEQ
N_SAMPLE = DEC_BATCH * DEC_SEQ
N_TOK = N_PROMPT + N_SAMPLE
N_COND = 1 + DEC_BATCH
KV_LEN = PAST_LEN + DEC_SEQ
N_KV_ROWS = DEC_BATCH * KV_LEN + N_PROMPT

LANE = 128
SUBLANE = 8
HEAD_PAD = LANE
QKV_W = N_HEADS * HEAD_PAD
VMEM_LIMIT = 56 * 1024 * 1024

TB = 512
WIDE_TB = 512
TQ = 256
TKC = 256
ATTN_CHUNKS_PER_ITER = 9
TM = 512
N_TILES = N_TOK * TOP_K // TM + N_EXPERTS
N_SLOTS = N_TILES * TM
PLAN_LANES = -(-N_TILES // LANE) * LANE
SLOT_TB = 2048
ROUTE_TB = 512
MOE_PARTS = 2
SC_CORES = 2
SC_WORKERS = SC_CORES * 16
ROPE_BLOCKS = DEC_SEQ // TB
NEG_BIG = -1e30


def _cparams(sem):
    return pltpu.CompilerParams(dimension_semantics=sem, vmem_limit_bytes=VMEM_LIMIT)


def _cond_of_block(i, tb):
    n_p = N_PROMPT // tb
    per = DEC_SEQ // tb
    return jnp.where(i < n_p, 0, 1 + (i - n_p) // per)


def _rms(x, g, n=None):
    n = x.shape[-1] if n is None else n
    ss = jnp.sum(x * x, axis=-1, keepdims=True) * (1.0 / n)
    return x * lax.rsqrt(ss + RMS_EPS) * g


def _dot(a, b):
    return jnp.dot(a, b, preferred_element_type=F32)


def _mod_kernel(c_ref, w_ref, b_ref, o_ref):
    c = c_ref[...]
    s = c * jax.nn.sigmoid(c)
    o_ref[0] = _dot(s.astype(BF16), w_ref[0].astype(BF16)) + b_ref[0]


def _modulation(cond, w_mod, b_mod):
    tn = 1536
    out = pl.pallas_call(
        _mod_kernel,
        grid=(DEPTH, 6 * D_MODEL // tn),
        in_specs=[
            pl.BlockSpec((SUBLANE, D_MODEL), lambda l, j: (0, 0)),
            pl.BlockSpec((1, D_MODEL, tn), lambda l, j: (l, 0, j)),
            pl.BlockSpec((1, 1, tn), lambda l, j: (l, 0, j)),
        ],
        out_specs=pl.BlockSpec((1, SUBLANE, tn), lambda l, j: (l, 0, j)),
        out_shape=jax.ShapeDtypeStruct((DEPTH, SUBLANE, 6 * D_MODEL), F32),
        compiler_params=_cparams(("parallel", "parallel")),
        name="adaln_mod",
    )(cond, w_mod, b_mod.reshape(DEPTH, 1, 6 * D_MODEL))
    m = out[:, :N_COND].reshape(DEPTH, N_COND, 6, D_MODEL)
    return jnp.pad(m, ((0, 0), (0, 0), (0, SUBLANE - 6), (0, 0)))


def _mod_spec(tb, layer, first_block=0):
    return pl.BlockSpec((None, 1, SUBLANE, D_MODEL),
                        lambda i: (layer, _cond_of_block(i + first_block, tb), 0, 0))


def _const_spec(shape):
    nd = len(shape)
    return pl.BlockSpec(shape, lambda *_: (0,) * nd)


def _layer_spec(shape, j):
    nd = len(shape)
    return pl.BlockSpec((None,) + tuple(shape), lambda *_: (j,) + (0,) * nd)


def _gelu_tanh(x):
    a = math.sqrt(2.0 / math.pi)
    hx = 0.5 * x
    return hx + hx * jnp.tanh(x * (a + (0.044715 * a) * (x * x)))


def _gmlp_kernel(xp_ref, xs_ref, mod_ref, g_ref, win_ref, bin_ref, gv_ref, ws_ref, bst_ref, wout_ref, o_ref):
    x = jnp.where(pl.program_id(0) < N_PROMPT // WIDE_TB, xp_ref[...], xs_ref[...])
    m = mod_ref[0]
    h = _rms(x, g_ref[...]) * (1.0 + m[1:2]) + m[0:1]
    hb = h.astype(BF16)
    zv = _gelu_tanh(_dot(hb, win_ref[:, GMLP_WIDTH:]) + bin_ref[:, GMLP_WIDTH:])
    vn = _rms(zv, gv_ref[...]).astype(BF16)
    rows = []
    for c in range(WIDE_TB // CHUNK):
        cols = []
        for g in range(GMLP_GROUPS):
            blk = vn[c * CHUNK:(c + 1) * CHUNK, g * GROUP_W:(g + 1) * GROUP_W]
            cols.append(_dot(ws_ref[g], blk) + bst_ref[:, g:g + 1])
        rows.append(jnp.concatenate(cols, axis=1))
    vm = jnp.concatenate(rows, axis=0)
    u = _gelu_tanh(_dot(hb, win_ref[:, :GMLP_WIDTH]) + bin_ref[:, :GMLP_WIDTH])
    d = _dot((u * vm).astype(BF16), wout_ref[...])
    o_ref[...] = x + m[2:3] * d


def _split_rows_specs(tb=TB):
    n_pb = N_PROMPT // tb
    return [pl.BlockSpec((tb, D_MODEL), lambda i: (jnp.minimum(i, n_pb - 1), 0)),
            pl.BlockSpec((tb, D_MODEL), lambda i: (jnp.maximum(i - n_pb, 0), 0))]


def _gmlp_layer(x_p, x_s, p, layer):
    j = layer // 2
    return pl.pallas_call(
        _gmlp_kernel,
        grid=(N_TOK // WIDE_TB,),
        in_specs=_split_rows_specs(WIDE_TB) + [
            _mod_spec(WIDE_TB, layer),
            _layer_spec((1, D_MODEL), layer),
            _layer_spec((D_MODEL, 2 * GMLP_WIDTH), j),
            _layer_spec((1, 2 * GMLP_WIDTH), j),
            _layer_spec((1, GMLP_WIDTH), j),
            _layer_spec((GMLP_GROUPS, CHUNK, CHUNK), j),
            _layer_spec((CHUNK, GMLP_GROUPS), j),
            _layer_spec((GMLP_WIDTH, D_MODEL), j),
        ],
        out_specs=pl.BlockSpec((WIDE_TB, D_MODEL), lambda i: (i, 0)),
        out_shape=jax.ShapeDtypeStruct((N_TOK, D_MODEL), F32),
        compiler_params=_cparams(("parallel",)),
        name="gmlp_mixer",
    )(x_p, x_s, p["mod"], p["norm_mix_g"], p["g_w_in"], p["g_b_in"], p["g_norm_v"], p["g_w_s"], p["g_b_st"],
      p["g_w_out"])


def _rope_tables():
    t = jnp.arange(DEC_SEQ)
    row_id = (t // GRID_W).astype(F32)
    col_id = (t % GRID_W).astype(F32)
    axis_dim = QK_ROPE_DIM // 2
    inv_freq = ROPE_THETA ** (-jnp.arange(0, axis_dim, 2, dtype=F32) / axis_dim)
    ang = jnp.stack([row_id[:, None] * inv_freq, col_id[:, None] * inv_freq], axis=1)
    cos, sin = jnp.cos(ang), jnp.sin(ang)
    zeros = jnp.zeros_like(sin)
    cos_l = jnp.concatenate([cos, cos], axis=-1).reshape(DEC_SEQ, QK_ROPE_DIM)
    s1_l = jnp.concatenate([-sin, zeros], axis=-1).reshape(DEC_SEQ, QK_ROPE_DIM)
    s2_l = jnp.concatenate([zeros, sin], axis=-1).reshape(DEC_SEQ, QK_ROPE_DIM)

    def widen(rope_part, nope_fill):
        left = jnp.full((DEC_SEQ, QK_NOPE_DIM), nope_fill, F32)
        right = jnp.zeros((DEC_SEQ, HEAD_PAD - QK_HEAD_DIM), F32)
        return jnp.concatenate([left, rope_part, right], axis=-1)

    pos = jnp.stack([widen(cos_l, 1.0), widen(s1_l, 0.0), widen(s2_l, 0.0)])
    ident_c = jnp.concatenate([jnp.ones((TB, QK_HEAD_DIM), F32),
                               jnp.zeros((TB, HEAD_PAD - QK_HEAD_DIM), F32)], axis=-1)
    ident = jnp.stack([ident_c, jnp.zeros_like(ident_c), jnp.zeros_like(ident_c)])
    return jnp.swapaxes(jnp.concatenate([pos, ident], axis=1), 1, 2)


def _dot_nt(a, b):
    return lax.dot_general(a, b, (((1,), (1,)), ((), ())), preferred_element_type=F32)


def _shift_rows(x, n):
    n = n % x.shape[0]
    return jnp.concatenate([x[n:], x[:n]], axis=0)


def _head_norm_rope(nope, rope, g_ref, tabt_ref):
    half = QK_ROPE_DIM // 4
    lo, hi = QK_NOPE_DIM, QK_HEAD_DIM
    ss = (jnp.sum(nope * nope, axis=0, keepdims=True) + jnp.sum(rope * rope, axis=0, keepdims=True))
    rs = lax.rsqrt(ss * (1.0 / QK_HEAD_DIM) + RMS_EPS)
    xn = nope * rs * g_ref[:lo, :]
    xr = rope * rs * g_ref[lo:hi, :]
    rot = (xr * tabt_ref[0, lo:hi, :] + _shift_rows(xr, half) * tabt_ref[1, lo:hi, :]
           + _shift_rows(xr, -half) * tabt_ref[2, lo:hi, :])
    return jnp.concatenate([xn, rot, jnp.zeros((HEAD_PAD - hi, nope.shape[1]), F32)], axis=0)


def _mla_proj_kernel(xp_ref, xs_ref, mod_ref, g_ref, wdq_ref, nq_ref, wuqt_ref, gq_ref, wdkv_ref, nkv_ref,
                     tabt_ref, qt_ref, ckv_ref, krp_ref):
    x = jnp.where(pl.program_id(0) < N_PROMPT // TB, xp_ref[...], xs_ref[...])
    m = mod_ref[0]
    h = _rms(x, g_ref[...]) * (1.0 + m[1:2]) + m[0:1]
    hb = h.astype(BF16)
    cq = _rms(_dot(hb, wdq_ref[...]), nq_ref[...])
    qt = _dot_nt(wuqt_ref[...], cq.astype(BF16))
    for hd in range(N_HEADS):
        r0 = hd * HEAD_PAD
        qh = _head_norm_rope(qt[r0:r0 + QK_NOPE_DIM, :], qt[r0 + QK_NOPE_DIM:r0 + QK_HEAD_DIM, :], gq_ref, tabt_ref)
        qt_ref[r0:r0 + HEAD_PAD, :] = qh.astype(BF16)
    kva = _dot(hb, wdkv_ref[...])
    ckv_ref[...] = _rms(kva[:, :KV_RANK], nkv_ref[...])
    krp_ref[...] = kva[:, KV_RANK:]


def _mla_proj(x_p, x_s, p, layer):
    j = layer // 2
    n_pb = N_PROMPT // TB

    def tab_idx(i):
        return (0, 0, jnp.where(i < n_pb, ROPE_BLOCKS, (i - n_pb) % ROPE_BLOCKS))

    return pl.pallas_call(
        _mla_proj_kernel,
        grid=(N_TOK // TB,),
        in_specs=_split_rows_specs() + [
            _mod_spec(TB, layer),
            _layer_spec((1, D_MODEL), layer),
            _layer_spec((D_MODEL, Q_RANK), j),
            _layer_spec((1, Q_RANK), j),
            _layer_spec((QKV_W, Q_RANK), j),
            _layer_spec((HEAD_PAD, 1), j),
            _layer_spec((D_MODEL, 2 * LANE), j),
            _layer_spec((1, KV_RANK), j),
            pl.BlockSpec((3, HEAD_PAD, TB), tab_idx),
        ],
        out_specs=[
            pl.BlockSpec((QKV_W, TB), lambda i: (0, i)),
            pl.BlockSpec((TB, KV_RANK), lambda i: (i, 0)),
            pl.BlockSpec((TB, HEAD_PAD), lambda i: (i, 0)),
        ],
        out_shape=[
            jax.ShapeDtypeStruct((QKV_W, N_TOK), BF16),
            jax.ShapeDtypeStruct((N_TOK, KV_RANK), F32),
            jax.ShapeDtypeStruct((N_TOK, HEAD_PAD), F32),
        ],
        compiler_params=_cparams(("parallel",)),
        name="mla_proj",
    )(x_p, x_s, p["mod"], p["norm_mix_g"], p["m_w_dq"], p["m_norm_q"], p["m_w_uq_t"], p["m_gq"], p["m_w_dkv"],
      p["m_norm_kv"], p["rope_t"])


def _kv_block_source(i):
    n_sb = DEC_BATCH * KV_LEN // TB
    per = KV_LEN // TB
    b = jnp.minimum(i // per, DEC_BATCH - 1)
    jj = i % per
    return jnp.logical_and(i < n_sb, jj < PAST_LEN // TB), b, jj


def _kv_expand_kernel(cckv_ref, ckrp_ref, ckv_ref, krp_ref, wukvt_ref, gk_ref, tabt_ref, k_ref, vt_ref):
    from_cache, _, _ = _kv_block_source(pl.program_id(0))
    ckv = jnp.where(from_cache, cckv_ref[...], ckv_ref[...])
    krp = jnp.where(from_cache, ckrp_ref[...], krp_ref[...])
    kvt = _dot_nt(wukvt_ref[...], ckv.astype(BF16))
    k_rope = krp.T[QK_NOPE_DIM:QK_HEAD_DIM, :]
    row = lax.broadcasted_iota(jnp.int32, (QK_NOPE_DIM, TB), 0)
    ones_row = jnp.where(row == 0, 1.0, 0.0)
    for hd in range(N_HEADS):
        r0 = hd * HEAD_PAD
        kh = _head_norm_rope(kvt[r0:r0 + QK_NOPE_DIM, :], k_rope, gk_ref, tabt_ref)
        k_ref[:, r0:r0 + HEAD_PAD] = kh.T.astype(BF16)
        vt_ref[r0:r0 + HEAD_PAD, :] = jnp.concatenate([ones_row, kvt[r0 + QK_NOPE_DIM:r0 + HEAD_PAD, :]],
                                                      axis=0).astype(BF16)


def _kv_expand(ckv, krp, p, layer):
    j = layer // 2
    n_sb = DEC_BATCH * KV_LEN // TB
    n_cache = PAST_LEN // TB
    n_pb = N_PROMPT // TB
    lat_blocks = DEC_SEQ // TB

    def tab_idx(i):
        from_cache, _, jj = _kv_block_source(i)
        return (0, 0, jnp.where((i >= n_sb) | from_cache, ROPE_BLOCKS, jj - n_cache))

    def cache_idx(i):
        _, b, jj = _kv_block_source(i)
        return (b, j, jnp.minimum(jj, n_cache - 1), 0)

    def tok_idx(i):
        _, b, jj = _kv_block_source(i)
        latent = n_pb + b * lat_blocks + jnp.maximum(jj - n_cache, 0)
        return (jnp.where(i < n_sb, latent, i - n_sb), 0)

    return pl.pallas_call(
        _kv_expand_kernel,
        grid=(N_KV_ROWS // TB,),
        in_specs=[
            pl.BlockSpec((None, None, TB, KV_RANK), cache_idx),
            pl.BlockSpec((None, None, TB, HEAD_PAD), cache_idx),
            pl.BlockSpec((TB, KV_RANK), tok_idx),
            pl.BlockSpec((TB, HEAD_PAD), tok_idx),
            _layer_spec((QKV_W, KV_RANK), j),
            _layer_spec((HEAD_PAD, 1), j),
            pl.BlockSpec((3, HEAD_PAD, TB), tab_idx),
        ],
        out_specs=[
            pl.BlockSpec((TB, QKV_W), lambda i: (i, 0)),
            pl.BlockSpec((QKV_W, TB), lambda i: (0, i)),
        ],
        out_shape=[
            jax.ShapeDtypeStruct((N_KV_ROWS, QKV_W), BF16),
            jax.ShapeDtypeStruct((QKV_W, N_KV_ROWS), BF16),
        ],
        compiler_params=_cparams(("parallel",)),
        name="mla_kv_expand",
    )(p["cache_ckv"], p["cache_krp"], ckv, krp, p["m_w_ukv_t"], p["m_gk"], p["rope_t"])


def _attn_kernel(qt_ref, k_ref, vt_ref, o_ref, s_a, s_b, *, t_k, hps, qps):
    c = (1.0 / math.sqrt(QK_HEAD_DIM)) * math.log2(math.e)
    kc = min(TKC, t_k)
    n_chunks = t_k // kc
    per_it = min(ATTN_CHUNKS_PER_ITER, n_chunks)
    n_it = n_chunks // per_it
    bufs = (s_a, s_b)

    units = [(qb, h) for qb in range(qps) for h in range(hps)]

    def rows(h):
        return slice(h * HEAD_PAD, (h + 1) * HEAD_PAD)

    def cols(qb):
        return slice(qb * TQ, (qb + 1) * TQ)

    def scores(u, off, m8):
        qb, h = units[u]
        st = _dot(k_ref[pl.ds(off, kc), rows(h)], qt_ref[rows(h), cols(qb)])
        bufs[u % 2][pl.ds(off, kc), :] = st
        return jnp.maximum(m8, jnp.max(st.reshape(kc // SUBLANE, SUBLANE, TQ), axis=0))

    def weigh(u, off, m, acc):
        pt = jnp.exp2((bufs[u % 2][pl.ds(off, kc), :] - m) * c).astype(BF16)
        return acc + _dot(vt_ref[rows(units[u][1]), pl.ds(off, kc)], pt)

    def phase(u_scores, u_weigh, m):
        def body(it, carry):
            m8, acc = carry
            for j in range(per_it):
                off = (it * per_it + j) * kc
                off = off if isinstance(off, int) else pl.multiple_of(off, kc)
                if u_scores is not None:
                    m8 = scores(u_scores, off, m8)
                if u_weigh is not None:
                    acc = weigh(u_weigh, off, m, acc)
            return m8, acc

        init = (jnp.full((SUBLANE, TQ), NEG_BIG, F32), jnp.zeros((HEAD_PAD, TQ), F32))
        return body(0, init) if n_it == 1 else lax.fori_loop(0, n_it, body, init)

    outs = []
    if n_chunks == 1:
        sts = [_dot(k_ref[:, rows(h)], qt_ref[rows(h), cols(qb)]) for qb, h in units]
        for (qb, h), st in zip(units, sts):
            m = jnp.max(st, axis=0, keepdims=True)
            acc = _dot(vt_ref[rows(h), :], jnp.exp2((st - m) * c).astype(BF16))
            outs.append(acc[QK_NOPE_DIM:, :] / acc[0:1, :])
    else:
        m8, _ = phase(0, None, None)
        for u in range(1, len(units) + 1):
            m = jnp.max(m8, axis=0, keepdims=True)
            m8, acc = phase(u if u < len(units) else None, u - 1, m)
            outs.append(acc[QK_NOPE_DIM:, :] / acc[0:1, :])
    for qb in range(qps):
        o_ref[cols(qb), :] = jnp.concatenate(outs[qb * hps:(qb + 1) * hps], axis=0).T.astype(BF16)


def _attention(qt, k, vt, *, n_batch, t_q, t_k, q_row0, kv_row0, hps, qps):
    tq = TQ * qps
    nq = t_q // tq
    q0 = q_row0 // tq
    k0 = kv_row0 // t_k
    return pl.pallas_call(
        functools.partial(_attn_kernel, t_k=t_k, hps=hps, qps=qps),
        grid=(n_batch, N_HEADS // hps, nq),
        in_specs=[
            pl.BlockSpec((hps * HEAD_PAD, tq), lambda b, h, i: (h, q0 + b * nq + i)),
            pl.BlockSpec((t_k, hps * HEAD_PAD), lambda b, h, i: (k0 + b, h)),
            pl.BlockSpec((hps * HEAD_PAD, t_k), lambda b, h, i: (h, k0 + b)),
        ],
        out_specs=pl.BlockSpec((tq, hps * V_HEAD_DIM), lambda b, h, i: (b * nq + i, h)),
        out_shape=jax.ShapeDtypeStruct((n_batch * t_q, N_HEADS * V_HEAD_DIM), BF16),
        scratch_shapes=[pltpu.VMEM((t_k, TQ), F32), pltpu.VMEM((t_k, TQ), F32)],
        compiler_params=_cparams(("parallel", "parallel", "parallel")),
        name=f"mla_attention_tk{t_k}",
    )(qt, k, vt)


def _out_proj_kernel(ap_ref, as_ref, xp_ref, xs_ref, mod_ref, wo_ref, o_ref):
    is_prompt = pl.program_id(0) < N_PROMPT // WIDE_TB
    a = jnp.where(is_prompt, ap_ref[...], as_ref[...])
    x = jnp.where(is_prompt, xp_ref[...], xs_ref[...])
    o_ref[...] = x + mod_ref[0][2:3] * _dot(a, wo_ref[...])


def _out_proj(attn_p, attn_s, x_p, x_s, p, layer):
    return pl.pallas_call(
        _out_proj_kernel,
        grid=(N_TOK // WIDE_TB,),
        in_specs=_split_rows_specs(WIDE_TB) + _split_rows_specs(WIDE_TB) + [
            _mod_spec(WIDE_TB, layer),
            _layer_spec((D_MODEL, D_MODEL), layer // 2),
        ],
        out_specs=pl.BlockSpec((WIDE_TB, D_MODEL), lambda i: (i, 0)),
        out_shape=jax.ShapeDtypeStruct((N_TOK, D_MODEL), F32),
        compiler_params=_cparams(("parallel",)),
        name="mla_out_proj",
    )(attn_p, attn_s, x_p, x_s, p["mod"], p["m_w_o"])


def _mla_layer(x_p, x_s, p, layer):
    qt, ckv, krp = _mla_proj(x_p, x_s, p, layer)
    k, vt = _kv_expand(ckv, krp, p, layer)
    a_p = _attention(qt, k, vt, n_batch=BATCH, t_q=SEQ, t_k=SEQ, q_row0=0, kv_row0=DEC_BATCH * KV_LEN,
                     hps=N_HEADS, qps=1)
    a_s = _attention(qt, k, vt, n_batch=DEC_BATCH, t_q=DEC_SEQ, t_k=KV_LEN, q_row0=N_PROMPT, kv_row0=0,
                     hps=4, qps=4)
    x = _out_proj(a_p, a_s, x_p, x_s, p, layer)
    state_ckv = ckv[:N_PROMPT].reshape(BATCH, SEQ, KV_RANK)
    state_krope = krp[:N_PROMPT, QK_NOPE_DIM:QK_HEAD_DIM].reshape(BATCH, SEQ, QK_ROPE_DIM)
    return x, state_ckv, state_krope


def _tile_plan(cnt8, plan_ref, start_ref):
    tm = float(TM)
    cnt_row = cnt8[0:1, :]
    cnt_col = cnt8.T[:, 0:1]
    tiles_row = jnp.floor((cnt_row + (tm - 1.0)) * (1.0 / tm))
    tiles_col = jnp.floor((cnt_col + (tm - 1.0)) * (1.0 / tm))
    sub = lax.broadcasted_iota(jnp.int32, (LANE, LANE), 0).astype(F32)
    lan = lax.broadcasted_iota(jnp.int32, (LANE, LANE), 1).astype(F32)
    tile_end_row = jnp.sum(jnp.where(sub <= lan, tiles_col, 0.0), axis=0, keepdims=True)
    tile_end_col = jnp.sum(jnp.where(lan <= sub, tiles_row, 0.0), axis=1, keepdims=True)
    n_used = jnp.max(tile_end_row, axis=1, keepdims=True)
    start_col = (tile_end_col - tiles_col) * tm
    end_col = start_col + cnt_col
    cand = jnp.where(jnp.logical_and(lan > sub, tiles_row > 0.0), lan, float(LANE))
    next_col = jnp.min(cand, axis=1, keepdims=True)
    next_col = jnp.where(next_col < float(LANE), next_col, -1.0)
    n_lanes = plan_ref.shape[1]
    tidx = jnp.minimum(lax.broadcasted_iota(jnp.int32, (LANE, n_lanes), 1).astype(F32), n_used - 1.0)
    esub = lax.broadcasted_iota(jnp.int32, (LANE, n_lanes), 0).astype(F32)
    te_row = jnp.sum(jnp.where(tile_end_col <= tidx, 1.0, 0.0), axis=0, keepdims=True)
    mine = esub == te_row
    end_at = jnp.sum(jnp.where(mine, end_col, 0.0), axis=0, keepdims=True)
    tv_row = jnp.clip(end_at - tidx[0:1, :] * tm, 0.0, tm)
    nx_row = jnp.sum(jnp.where(mine, next_col, 0.0), axis=0, keepdims=True)
    nu_row = jnp.broadcast_to(n_used, (1, n_lanes))
    plan_ref[...] = jnp.concatenate([te_row, tv_row, nx_row, nu_row, jnp.zeros((SUBLANE - 4, n_lanes), F32)], axis=0)
    start_ref[...] = jnp.broadcast_to(start_col, (LANE, LANE))


def _route_kernel(x_ref, mod_ref, g_ref, wr_ref, br_ref, tri_ref, h_ref, meta_ref, metat_ref, plan_ref, start_ref,
                  carry):
    i = pl.program_id(0)

    @pl.when(i == 0)
    def _():
        carry[...] = jnp.zeros_like(carry)

    x = x_ref[...]
    m = mod_ref[0]
    h = _rms(x, g_ref[...]) * (1.0 + m[4:5]) + m[3:4]
    h_ref[...] = _pack_bf16_pairs(h)
    logits = _dot(h.astype(BF16), wr_ref[...]) + br_ref[...]
    lane = lax.broadcasted_iota(jnp.int32, logits.shape, 1).astype(F32)
    work = logits
    sel = jnp.zeros(logits.shape, F32)
    hits, tops = [], []
    for k in range(TOP_K):
        mk = jnp.max(work, axis=-1, keepdims=True)
        first = jnp.min(jnp.where(work == mk, lane, float(LANE)), axis=-1, keepdims=True)
        hit = lane == first
        sel = jnp.where(hit, 1.0, sel)
        work = jnp.where(hit, -jnp.inf, work)
        hits.append((hit, first))
        tops.append(mk)
    es = [jnp.exp(t - tops[0]) for t in tops]
    denom = es[0] + es[1] + es[2] + es[3]
    pos = _dot(tri_ref[...], sel.astype(BF16)) + carry[0:1, :]
    carry[...] = carry[...] + jnp.sum(sel, axis=0, keepdims=True)

    @pl.when(i == pl.num_programs(0) - 1)
    def _():
        _tile_plan(carry[...], plan_ref, start_ref)

    meta = jnp.zeros(logits.shape, F32)
    for k in range(TOP_K):
        hit, first = hits[k]
        pk = jnp.sum(jnp.where(hit, pos, 0.0), axis=-1, keepdims=True)
        meta = jnp.where(lane == float(k), first, meta)
        meta = jnp.where(lane == float(TOP_K + k), es[k] / denom, meta)
        meta = jnp.where(lane == float(2 * TOP_K + k), pk, meta)
    meta_ref[...] = meta
    metat_ref[...] = meta.T[:2 * SUBLANE, :]


def _route(x, p, layer):
    tb = ROUTE_TB
    return pl.pallas_call(
        _route_kernel,
        grid=(N_TOK // tb,),
        in_specs=[
            pl.BlockSpec((tb, D_MODEL), lambda i: (i, 0)),
            _mod_spec(tb, layer),
            _layer_spec((1, D_MODEL), layer),
            _layer_spec((D_MODEL, LANE), layer),
            _layer_spec((1, LANE), layer),
            _const_spec((tb, tb)),
        ],
        out_specs=[
            pl.BlockSpec((tb, D_MODEL // 2), lambda i: (i, 0)),
            pl.BlockSpec((tb, LANE), lambda i: (i, 0)),
            pl.BlockSpec((2 * SUBLANE, tb), lambda i: (0, i)),
            _const_spec((SUBLANE, PLAN_LANES)),
            _const_spec((LANE, LANE)),
        ],
        out_shape=[
            jax.ShapeDtypeStruct((N_TOK, D_MODEL // 2), jnp.uint32),
            jax.ShapeDtypeStruct((N_TOK, LANE), F32),
            jax.ShapeDtypeStruct((2 * SUBLANE, N_TOK), F32),
            jax.ShapeDtypeStruct((SUBLANE, PLAN_LANES), F32),
            jax.ShapeDtypeStruct((LANE, LANE), F32),
        ],
        scratch_shapes=[pltpu.VMEM((SUBLANE, LANE), F32)],
        compiler_params=_cparams(("arbitrary",)),
        name="moe_route",
    )(x, p["mod"], p["norm_ffn_g"], p["e_w_router"], p["e_b_router"], p["tri"])


def _slots_kernel(start_ref, metat_ref, dest_ref):
    start_col = start_ref[:, 0:1]
    esub = lax.broadcasted_iota(jnp.int32, (LANE, SLOT_TB), 0).astype(F32)
    rows = []
    for k in range(TOP_K):
        e = metat_ref[k:k + 1, :]
        first = jnp.sum(jnp.where(esub == e, start_col, 0.0), axis=0, keepdims=True)
        rows.append(first + metat_ref[2 * TOP_K + k:2 * TOP_K + k + 1, :])
    dest_ref[...] = jnp.concatenate(rows, axis=0).astype(jnp.int32)


def _slots(start, meta_t):
    return pl.pallas_call(
        _slots_kernel,
        grid=(N_TOK // SLOT_TB,),
        in_specs=[
            _const_spec((LANE, LANE)),
            pl.BlockSpec((2 * SUBLANE, SLOT_TB), lambda i: (0, i)),
        ],
        out_specs=pl.BlockSpec((TOP_K, SLOT_TB), lambda i: (0, i)),
        out_shape=jax.ShapeDtypeStruct((TOP_K, N_TOK), jnp.int32),
        compiler_params=_cparams(("parallel",)),
        name="moe_slots",
    )(start, meta_t)


def _sc_gather(table, idx, ch):
    b, w = idx.shape[0], table.shape[1]
    per_w = b // SC_WORKERS
    n_ch = per_w // ch
    assert per_w * SC_WORKERS == b and n_ch * ch == per_w and n_ch % 2 == 0
    mesh = plsc.VectorSubcoreMesh(core_axis_name="c", subcore_axis_name="s")

    @functools.partial(
        pl.kernel, mesh=mesh,
        out_type=jax.ShapeDtypeStruct((b, w), table.dtype),
        scratch_types=[
            pltpu.VMEM((n_ch, ch), jnp.int32),
            pltpu.VMEM((ch, w), table.dtype),
            pltpu.VMEM((ch, w), table.dtype),
            pltpu.SemaphoreType.DMA, pltpu.SemaphoreType.DMA,
            pltpu.SemaphoreType.DMA, pltpu.SemaphoreType.DMA,
        ],
        name="sc_row_gather",
    )
    def gather_rows(table_hbm, idx_hbm, out_hbm, idx_v, buf0, buf1, g0, g1, s0, s1):
        wid = lax.axis_index("s") * SC_CORES + lax.axis_index("c")
        base = wid * per_w
        pltpu.sync_copy(idx_hbm.at[wid], idx_v)

        def gather(j, buf, sem):
            return pltpu.make_async_copy(table_hbm.at[idx_v.at[j]], buf, sem)

        def store(j, buf, sem):
            return pltpu.make_async_copy(buf, out_hbm.at[pl.ds(base + j * ch, ch)], sem)

        gather(0, buf0, g0).start()

        @pl.loop(0, n_ch, step=2)
        def _(j):
            @pl.when(j > 0)
            def _():
                store(j - 1, buf1, s1).wait()

            gather(j + 1, buf1, g1).start()
            gather(j, buf0, g0).wait()
            store(j, buf0, s0).start()
            gather(j + 1, buf1, g1).wait()
            store(j + 1, buf1, s1).start()
            store(j, buf0, s0).wait()

            @pl.when(j + 2 < n_ch)
            def _():
                gather(j + 2, buf0, g0).start()

        store(n_ch - 1, buf1, s1).wait()

    return gather_rows(table, idx.reshape(SC_WORKERS, n_ch, ch))


def _sc_dispatch(rows, dest_t, ch):
    n, w = rows.shape
    per_w = n // SC_WORKERS
    n_ch = per_w // ch
    assert per_w * SC_WORKERS == n and n_ch * ch == per_w and n_ch % 2 == 0
    mesh = plsc.VectorSubcoreMesh(core_axis_name="c", subcore_axis_name="s")
    idx = dest_t.reshape(TOP_K, SC_WORKERS, n_ch, ch)

    @functools.partial(
        pl.kernel, mesh=mesh,
        out_type=jax.ShapeDtypeStruct((N_SLOTS, w), rows.dtype),
        scratch_types=[
            pltpu.VMEM((TOP_K * n_ch, ch), jnp.int32),
            pltpu.VMEM((ch, w), rows.dtype),
            pltpu.VMEM((ch, w), rows.dtype),
            pltpu.SemaphoreType.DMA, pltpu.SemaphoreType.DMA,
            pltpu.SemaphoreType.DMA, pltpu.SemaphoreType.DMA,
        ],
        name="sc_row_dispatch",
    )
    def dispatch_rows(rows_hbm, idx_hbm, out_hbm, idx_v, buf0, buf1, l0, l1, s0, s1):
        wid = lax.axis_index("s") * SC_CORES + lax.axis_index("c")
        base = wid * per_w
        for k in range(TOP_K):
            pltpu.sync_copy(idx_hbm.at[k, wid], idx_v.at[pl.ds(k * n_ch, n_ch)])

        def load(j, buf, sem):
            return pltpu.make_async_copy(rows_hbm.at[pl.ds(base + j * ch, ch)], buf, sem)

        def scatter(j, k, buf, sem):
            return pltpu.make_async_copy(buf, out_hbm.at[idx_v.at[k * n_ch + j]], sem)

        load(0, buf0, l0).start()

        @pl.loop(0, n_ch, step=2)
        def _(j):
            load(j + 1, buf1, l1).start()
            load(j, buf0, l0).wait()
            for k in range(TOP_K):
                scatter(j, k, buf0, s0).start()
            load(j + 1, buf1, l1).wait()
            for k in range(TOP_K):
                scatter(j + 1, k, buf1, s1).start()
            for k in range(TOP_K):
                scatter(j, k, buf0, s0).wait()

            @pl.when(j + 2 < n_ch)
            def _():
                load(j + 2, buf0, l0).start()

            for k in range(TOP_K):
                scatter(j + 1, k, buf1, s1).wait()

    return dispatch_rows(rows, idx)


def _deinterleave_matrix():
    src = jnp.arange(2 * LANE)[:, None]
    dst = jnp.arange(2 * LANE)[None, :]
    want = jnp.where(dst < LANE, 2 * dst, 2 * (dst - LANE) + 1)
    return (src == want).astype(BF16)


def _expert_kernel(te_ref, nu_ref, tv_ref, nx_ref, x_ref, wgu_hbm, bgu_ref, wd_hbm, bd_ref, perm_ref, o_ref,
                   wgu_st, wd_st, wgu_bf, wd_bf, sems, *, layer):
    i = pl.program_id(0)
    prev = te_ref[jnp.maximum(i - 1, 0)]
    fresh = jnp.logical_or(i == 0, te_ref[i] != prev)

    def fetch(e):
        return (pltpu.make_async_copy(wgu_hbm.at[layer, e], wgu_st, sems.at[0]),
                pltpu.make_async_copy(wd_hbm.at[layer, e], wd_st, sems.at[1]))

    @pl.when(i == 0)
    def _():
        for cp in fetch(te_ref[0]):
            cp.start()

    @pl.when(jnp.logical_and(fresh, i < nu_ref[0]))
    def _():
        for cp in fetch(te_ref[i]):
            cp.wait()
        for b in range(2 * D_FF // (2 * LANE)):
            sl = slice(b * 2 * LANE, (b + 1) * 2 * LANE)
            wgu_bf[:, sl] = _dot(wgu_st[:, sl].astype(BF16), perm_ref[...]).astype(BF16)
        wd_bf[...] = wd_st[...].astype(BF16)

        @pl.when(nx_ref[i] >= 0)
        def _():
            for cp in fetch(nx_ref[i]):
                cp.start()

    @pl.when(i < nu_ref[0])
    def _():
        row = lax.broadcasted_iota(jnp.int32, (TM, D_MODEL // 2), 0)
        w = jnp.where(row < tv_ref[i], x_ref[...], jnp.uint32(0))
        x = _unpack_bf16_pairs(w).astype(BF16)
        gu = _dot(x, wgu_bf[...]) + bgu_ref[...]
        acts = []
        for b in range(D_FF // LANE):
            glu = jnp.minimum(gu[:, b * 2 * LANE:b * 2 * LANE + LANE], SWIGLU_LIMIT)
            lin = jnp.clip(gu[:, b * 2 * LANE + LANE:(b + 1) * 2 * LANE], -SWIGLU_LIMIT, SWIGLU_LIMIT)
            acts.append((glu * jax.nn.sigmoid(SWIGLU_ALPHA * glu) * (lin + 1.0)).astype(BF16))
        act = jnp.concatenate(acts, axis=1)
        o_ref[...] = _pack_bf16_pairs(_dot(act, wd_bf[...]) + bd_ref[...])


def _experts(buf, tile_expert, n_used, tile_valid, tile_next, p, layer):
    def row_idx(i, te, nu, tv, nx):
        return (jnp.minimum(i, nu[0] - 1), 0)

    def b_idx(i, te, nu, tv, nx):
        return (layer, te[i], 0, 0)

    grid_spec = pltpu.PrefetchScalarGridSpec(
        num_scalar_prefetch=4,
        grid=(N_TILES,),
        in_specs=[
            pl.BlockSpec((TM, D_MODEL // 2), row_idx),
            pl.BlockSpec(memory_space=pl.ANY),
            pl.BlockSpec((None, None, 1, 2 * D_FF), b_idx),
            pl.BlockSpec(memory_space=pl.ANY),
            pl.BlockSpec((None, None, 1, D_MODEL), b_idx),
            _const_spec((2 * LANE, 2 * LANE)),
        ],
        out_specs=pl.BlockSpec((TM, D_MODEL // 2), row_idx),
        scratch_shapes=[
            pltpu.VMEM((D_MODEL, 2 * D_FF), F32),
            pltpu.VMEM((D_FF, D_MODEL), F32),
            pltpu.VMEM((D_MODEL, 2 * D_FF), BF16),
            pltpu.VMEM((D_FF, D_MODEL), BF16),
            pltpu.SemaphoreType.DMA((2,)),
        ],
    )
    return pl.pallas_call(
        functools.partial(_expert_kernel, layer=layer),
        grid_spec=grid_spec,
        out_shape=jax.ShapeDtypeStruct((N_SLOTS, D_MODEL // 2), jnp.uint32),
        compiler_params=_cparams(("arbitrary",)),
        name="moe_experts",
    )(tile_expert, n_used, tile_valid, tile_next, buf, p["e_w_gu"], p["e_b_gu"], p["e_w_down"], p["e_b_down"],
      p["deinterleave"])


def _pack_bf16_pairs(v):
    half = v.shape[1] // 2
    bits = pltpu.bitcast(v.astype(BF16).astype(F32), jnp.uint32)
    return (bits[:, half:] & jnp.uint32(0xFFFF0000)) | (bits[:, :half] >> 16)


def _unpack_bf16_pairs(w):
    return jnp.concatenate([pltpu.bitcast(w << 16, F32), pltpu.bitcast(w & jnp.uint32(0xFFFF0000), F32)],
                           axis=1)


def _combine_kernel(x_ref, mod_ref, y_ref, w_ref, o_ref):
    w = w_ref[:, TOP_K:2 * TOP_K]
    y = _unpack_bf16_pairs(y_ref[0]) * w[:, 0:1]
    for k in range(1, TOP_K):
        y = y + _unpack_bf16_pairs(y_ref[k]) * w[:, k:k + 1]
    o_ref[...] = x_ref[...] + mod_ref[0][5:6] * y


def _combine(x, y4, meta, p, layer, part):
    n_rows = N_TOK // MOE_PARTS
    tb = WIDE_TB
    first = part * n_rows // tb
    return pl.pallas_call(
        _combine_kernel,
        grid=(n_rows // tb,),
        in_specs=[
            pl.BlockSpec((tb, D_MODEL), lambda i: (i + first, 0)),
            _mod_spec(tb, layer, first),
            pl.BlockSpec((TOP_K, tb, D_MODEL // 2), lambda i: (0, i, 0)),
            pl.BlockSpec((tb, LANE), lambda i: (i + first, 0)),
        ],
        out_specs=pl.BlockSpec((tb, D_MODEL), lambda i: (i, 0)),
        out_shape=jax.ShapeDtypeStruct((n_rows, D_MODEL), F32),
        compiler_params=_cparams(("parallel",)),
        name="moe_combine",
    )(x, p["mod"], y4, meta)


def _moe_layer(x, p, layer):
    hp, meta, meta_t, plan, start = _route(x, p, layer)
    plan = plan[:4, :N_TILES].astype(jnp.int32)
    tile_expert, tile_valid, tile_next, n_used = plan[0], plan[1], plan[2], plan[3, :1]
    dest_t = _slots(start, meta_t)
    buf = _sc_dispatch(hp, dest_t, 64)
    yb = _experts(buf, tile_expert, n_used, tile_valid, tile_next, p, layer)
    n_rows = N_TOK // MOE_PARTS
    outs = []
    for part in range(MOE_PARTS):
        idx = dest_t[:, part * n_rows:(part + 1) * n_rows].reshape(-1)
        y4 = _sc_gather(yb, idx, 64).reshape(TOP_K, n_rows, D_MODEL // 2)
        outs.append(_combine(x, y4, meta, p, layer, part))
    return outs


def _prepare(c, cache_ckv, cache_krope, c_ctx, norm_mix_g, norm_ffn_g, w_mod, b_mod, g_w_in, g_b_in, g_norm_v,
             g_w_s, g_b_s, g_w_out, m_w_dq, m_norm_q, m_w_uq, m_w_dkv, m_norm_kv, m_w_ukv, m_qk_norm_q,
             m_qk_norm_k, m_w_o, e_w_router, e_b_router, e_w_gu, e_b_gu, e_w_down, e_b_down):
    n_mla = m_w_dq.shape[0]
    cond = jnp.concatenate([c_ctx[None, :], c, jnp.zeros((SUBLANE - N_COND, D_MODEL), F32)], axis=0)
    wdkv = jnp.concatenate([m_w_dkv[..., :KV_RANK], jnp.zeros((n_mla, D_MODEL, QK_NOPE_DIM), F32),
                            m_w_dkv[..., KV_RANK:], jnp.zeros((n_mla, D_MODEL, HEAD_PAD - QK_HEAD_DIM), F32)],
                           axis=-1)
    w_uq = jnp.pad(m_w_uq.reshape(n_mla, Q_RANK, N_HEADS, QK_HEAD_DIM),
                   ((0, 0), (0, 0), (0, 0), (0, HEAD_PAD - QK_HEAD_DIM))).reshape(n_mla, Q_RANK, QKV_W)

    def gain_col(g):
        return jnp.pad(g, ((0, 0), (0, HEAD_PAD - QK_HEAD_DIM)))[:, :, None]

    return {
        "mod": _modulation(cond, w_mod, b_mod),
        "rope_t": _rope_tables(),
        "norm_mix_g": norm_mix_g[:, None, :],
        "norm_ffn_g": norm_ffn_g[:, None, :],
        "g_w_in": g_w_in.astype(BF16),
        "g_b_in": g_b_in[:, None, :],
        "g_norm_v": g_norm_v[:, None, :],
        "g_w_s": g_w_s.astype(BF16),
        "g_b_st": jnp.swapaxes(g_b_s, 1, 2),
        "g_w_out": g_w_out.astype(BF16),
        "m_w_dq": m_w_dq.astype(BF16),
        "m_norm_q": m_norm_q[:, None, :],
        "m_w_uq_t": jnp.swapaxes(w_uq, 1, 2).astype(BF16),
        "m_gq": gain_col(m_qk_norm_q),
        "m_w_dkv": wdkv.astype(BF16),
        "m_norm_kv": m_norm_kv[:, None, :],
        "m_w_ukv_t": jnp.swapaxes(m_w_ukv, 1, 2).astype(BF16),
        "m_gk": gain_col(m_qk_norm_k),
        "m_w_o": m_w_o.astype(BF16),
        "cache_ckv": cache_ckv,
        "cache_krp": jnp.pad(cache_krope, ((0, 0), (0, 0), (0, 0), (QK_NOPE_DIM, HEAD_PAD - QK_HEAD_DIM))),
        "e_w_router": jnp.pad(e_w_router, ((0, 0), (0, 0), (0, LANE - N_EXPERTS))).astype(BF16),
        "e_b_router": jnp.pad(e_b_router, ((0, 0), (0, LANE - N_EXPERTS)), constant_values=NEG_BIG)[:, None, :],
        "tri": jnp.tri(ROUTE_TB, ROUTE_TB, -1, dtype=BF16),
        "e_w_gu": e_w_gu,
        "e_b_gu": e_b_gu.reshape(DEPTH, N_EXPERTS, D_FF // LANE, LANE, 2).swapaxes(3, 4).reshape(
            DEPTH, N_EXPERTS, 1, 2 * D_FF),
        "e_w_down": e_w_down,
        "e_b_down": e_b_down[:, :, None, :],
        "deinterleave": _deinterleave_matrix(),
    }


def kernel(x_prompt, x_sample, c, cache_ckv, cache_krope, c_ctx, norm_mix_g, norm_ffn_g, w_mod, b_mod,
           g_w_in, g_b_in, g_norm_v, g_w_s, g_b_s, g_w_out, m_w_dq, m_norm_q, m_w_uq, m_w_dkv,
           m_norm_kv, m_w_ukv, m_qk_norm_q, m_qk_norm_k, m_w_o, e_w_router, e_b_router, e_w_gu,
           e_b_gu, e_w_down, e_b_down):
    p = _prepare(c, cache_ckv, cache_krope, c_ctx, norm_mix_g, norm_ffn_g, w_mod, b_mod, g_w_in, g_b_in,
                 g_norm_v, g_w_s, g_b_s, g_w_out, m_w_dq, m_norm_q, m_w_uq, m_w_dkv, m_norm_kv, m_w_ukv,
                 m_qk_norm_q, m_qk_norm_k, m_w_o, e_w_router, e_b_router, e_w_gu, e_b_gu, e_w_down, e_b_down)
    assert MOE_PARTS == 2 and N_PROMPT == N_SAMPLE
    x_p, x_s = x_prompt.reshape(N_PROMPT, D_MODEL), x_sample.reshape(N_SAMPLE, D_MODEL)
    ckv_states, krope_states = [], []
    for layer in range(DEPTH):
        if layer % 2 == 0:
            x = _gmlp_layer(x_p, x_s, p, layer)
        else:
            x, s_ckv, s_krope = _mla_layer(x_p, x_s, p, layer)
            ckv_states.append(s_ckv)
            krope_states.append(s_krope)
        x_p, x_s = _moe_layer(x, p, layer)
    y_prompt = x_p.reshape(BATCH, SEQ, D_MODEL)
    y_sample = x_s.reshape(DEC_BATCH, DEC_SEQ, D_MODEL)
    return (y_prompt, y_sample, jnp.stack(ckv_states, axis=1), jnp.stack(krope_states, axis=1))
```

```python
import functools
import math

import jax
import jax.numpy as jnp
from jax import lax
from jax.experimental import pallas as pl
from jax.experimental.pallas import tpu as pltpu
from jax.experimental.pallas import tpu_sc as plsc

F32 = jnp.float32
BF16 = jnp.bfloat16

D_MODEL = 1024
BATCH = 32
SEQ = 256
DEPTH = 4
DEC_BATCH = 2
DEC_SEQ = 4096
PAST_LEN = 512
GRID_W = 64
RMS_EPS = 1e-6
GMLP_WIDTH = 2 * D_MODEL
GMLP_GROUPS = 8
GROUP_W = GMLP_WIDTH // GMLP_GROUPS
CHUNK = 128
N_HEADS = 16
QK_NOPE_DIM = 64
QK_ROPE_DIM = 32
QK_HEAD_DIM = QK_NOPE_DIM + QK_ROPE_DIM
V_HEAD_DIM = 64
Q_RANK = 256
KV_RANK = 128
ROPE_THETA = 10000.0
N_EXPERTS = 32
TOP_K = 4
D_FF = D_MODEL
SWIGLU_LIMIT = 7.0
SWIGLU_ALPHA = 1.702

N_PROMPT = BATCH * SEQ
N_SAMPLE = DEC_BATCH * DEC_SEQ
N_TOK = N_PROMPT + N_SAMPLE
N_COND = 1 + DEC_BATCH
KV_LEN = PAST_LEN + DEC_SEQ
N_KV_ROWS = DEC_BATCH * KV_LEN + N_PROMPT

LANE = 128
SUBLANE = 8
HEAD_PAD = LANE
QKV_W = N_HEADS * HEAD_PAD
VMEM_LIMIT = 56 * 1024 * 1024

TB = 512
WIDE_TB = 512
TQ = 256
TKC = 256
ATTN_CHUNKS_PER_ITER = 9
TM = 512
N_TILES = N_TOK * TOP_K // TM + N_EXPERTS
N_SLOTS = N_TILES * TM
PLAN_LANES = -(-N_TILES // LANE) * LANE
SLOT_TB = 2048
ROUTE_TB = 512
MOE_PARTS = 2
SC_CORES = 2
SC_WORKERS = SC_CORES * 16
ROPE_BLOCKS = DEC_SEQ // TB
NEG_BIG = -1e30


def _cparams(sem):
    return pltpu.CompilerParams(dimension_semantics=sem, vmem_limit_bytes=VMEM_LIMIT)


def _cond_of_block(i, tb):
    n_p = N_PROMPT // tb
    per = DEC_SEQ // tb
    return jnp.where(i < n_p, 0, 1 + (i - n_p) // per)


def _rms(x, g, n=None):
    n = x.shape[-1] if n is None else n
    ss = jnp.sum(x * x, axis=-1, keepdims=True) * (1.0 / n)
    return x * lax.rsqrt(ss + RMS_EPS) * g


def _dot(a, b):
    return jnp.dot(a, b, preferred_element_type=F32)


def _mod_kernel(c_ref, w_ref, b_ref, o_ref):
    c = c_ref[...]
    s = c * jax.nn.sigmoid(c)
    o_ref[0] = _dot(s.astype(BF16), w_ref[0].astype(BF16)) + b_ref[0]


def _modulation(cond, w_mod, b_mod):
    tn = 1536
    out = pl.pallas_call(
        _mod_kernel,
        grid=(DEPTH, 6 * D_MODEL // tn),
        in_specs=[
            pl.BlockSpec((SUBLANE, D_MODEL), lambda l, j: (0, 0)),
            pl.BlockSpec((1, D_MODEL, tn), lambda l, j: (l, 0, j)),
            pl.BlockSpec((1, 1, tn), lambda l, j: (l, 0, j)),
        ],
        out_specs=pl.BlockSpec((1, SUBLANE, tn), lambda l, j: (l, 0, j)),
        out_shape=jax.ShapeDtypeStruct((DEPTH, SUBLANE, 6 * D_MODEL), F32),
        compiler_params=_cparams(("parallel", "parallel")),
        name="adaln_mod",
    )(cond, w_mod, b_mod.reshape(DEPTH, 1, 6 * D_MODEL))
    m = out[:, :N_COND].reshape(DEPTH, N_COND, 6, D_MODEL)
    return jnp.pad(m, ((0, 0), (0, 0), (0, SUBLANE - 6), (0, 0)))


def _mod_spec(tb, layer, first_block=0):
    return pl.BlockSpec((None, 1, SUBLANE, D_MODEL),
                        lambda i: (layer, _cond_of_block(i + first_block, tb), 0, 0))


def _const_spec(shape):
    nd = len(shape)
    return pl.BlockSpec(shape, lambda *_: (0,) * nd)


def _layer_spec(shape, j):
    nd = len(shape)
    return pl.BlockSpec((None,) + tuple(shape), lambda *_: (j,) + (0,) * nd)


def _gelu_tanh(x):
    a = math.sqrt(2.0 / math.pi)
    hx = 0.5 * x
    return hx + hx * jnp.tanh(x * (a + (0.044715 * a) * (x * x)))


def _gmlp_kernel(xp_ref, xs_ref, mod_ref, g_ref, win_ref, bin_ref, gv_ref, ws_ref, bst_ref, wout_ref, o_ref):
    x = jnp.where(pl.program_id(0) < N_PROMPT // WIDE_TB, xp_ref[...], xs_ref[...])
    m = mod_ref[0]
    h = _rms(x, g_ref[...]) * (1.0 + m[1:2]) + m[0:1]
    hb = h.astype(BF16)
    zv = _gelu_tanh(_dot(hb, win_ref[:, GMLP_WIDTH:]) + bin_ref[:, GMLP_WIDTH:])
    vn = _rms(zv, gv_ref[...]).astype(BF16)
    rows = []
    for c in range(WIDE_TB // CHUNK):
        cols = []
        for g in range(GMLP_GROUPS):
            blk = vn[c * CHUNK:(c + 1) * CHUNK, g * GROUP_W:(g + 1) * GROUP_W]
            cols.append(_dot(ws_ref[g], blk) + bst_ref[:, g:g + 1])
        rows.append(jnp.concatenate(cols, axis=1))
    vm = jnp.concatenate(rows, axis=0)
    u = _gelu_tanh(_dot(hb, win_ref[:, :GMLP_WIDTH]) + bin_ref[:, :GMLP_WIDTH])
    d = _dot((u * vm).astype(BF16), wout_ref[...])
    o_ref[...] = x + m[2:3] * d


def _split_rows_specs(tb=TB):
    n_pb = N_PROMPT // tb
    return [pl.BlockSpec((tb, D_MODEL), lambda i: (jnp.minimum(i, n_pb - 1), 0)),
            pl.BlockSpec((tb, D_MODEL), lambda i: (jnp.maximum(i - n_pb, 0), 0))]


def _gmlp_layer(x_p, x_s, p, layer):
    j = layer // 2
    return pl.pallas_call(
        _gmlp_kernel,
        grid=(N_TOK // WIDE_TB,),
        in_specs=_split_rows_specs(WIDE_TB) + [
            _mod_spec(WIDE_TB, layer),
            _layer_spec((1, D_MODEL), layer),
            _layer_spec((D_MODEL, 2 * GMLP_WIDTH), j),
            _layer_spec((1, 2 * GMLP_WIDTH), j),
            _layer_spec((1, GMLP_WIDTH), j),
            _layer_spec((GMLP_GROUPS, CHUNK, CHUNK), j),
            _layer_spec((CHUNK, GMLP_GROUPS), j),
            _layer_spec((GMLP_WIDTH, D_MODEL), j),
        ],
        out_specs=pl.BlockSpec((WIDE_TB, D_MODEL), lambda i: (i, 0)),
        out_shape=jax.ShapeDtypeStruct((N_TOK, D_MODEL), F32),
        compiler_params=_cparams(("parallel",)),
        name="gmlp_mixer",
    )(x_p, x_s, p["mod"], p["norm_mix_g"], p["g_w_in"], p["g_b_in"], p["g_norm_v"], p["g_w_s"], p["g_b_st"],
      p["g_w_out"])


def _rope_tables():
    t = jnp.arange(DEC_SEQ)
    row_id = (t // GRID_W).astype(F32)
    col_id = (t % GRID_W).astype(F32)
    axis_dim = QK_ROPE_DIM // 2
    inv_freq = ROPE_THETA ** (-jnp.arange(0, axis_dim, 2, dtype=F32) / axis_dim)
    ang = jnp.stack([row_id[:, None] * inv_freq, col_id[:, None] * inv_freq], axis=1)
    cos, sin = jnp.cos(ang), jnp.sin(ang)
    zeros = jnp.zeros_like(sin)
    cos_l = jnp.concatenate([cos, cos], axis=-1).reshape(DEC_SEQ, QK_ROPE_DIM)
    s1_l = jnp.concatenate([-sin, zeros], axis=-1).reshape(DEC_SEQ, QK_ROPE_DIM)
    s2_l = jnp.concatenate([zeros, sin], axis=-1).reshape(DEC_SEQ, QK_ROPE_DIM)

    def widen(rope_part, nope_fill):
        left = jnp.full((DEC_SEQ, QK_NOPE_DIM), nope_fill, F32)
        right = jnp.zeros((DEC_SEQ, HEAD_PAD - QK_HEAD_DIM), F32)
        return jnp.concatenate([left, rope_part, right], axis=-1)

    pos = jnp.stack([widen(cos_l, 1.0), widen(s1_l, 0.0), widen(s2_l, 0.0)])
    ident_c = jnp.concatenate([jnp.ones((TB, QK_HEAD_DIM), F32),
                               jnp.zeros((TB, HEAD_PAD - QK_HEAD_DIM), F32)], axis=-1)
    ident = jnp.stack([ident_c, jnp.zeros_like(ident_c), jnp.zeros_like(ident_c)])
    return jnp.swapaxes(jnp.concatenate([pos, ident], axis=1), 1, 2)


def _dot_nt(a, b):
    return lax.dot_general(a, b, (((1,), (1,)), ((), ())), preferred_element_type=F32)


def _shift_rows(x, n):
    n = n % x.shape[0]
    return jnp.concatenate([x[n:], x[:n]], axis=0)


def _head_norm_rope(nope, rope, g_ref, tabt_ref):
    half = QK_ROPE_DIM // 4
    lo, hi = QK_NOPE_DIM, QK_HEAD_DIM
    ss = (jnp.sum(nope * nope, axis=0, keepdims=True) + jnp.sum(rope * rope, axis=0, keepdims=True))
    rs = lax.rsqrt(ss * (1.0 / QK_HEAD_DIM) + RMS_EPS)
    xn = nope * rs * g_ref[:lo, :]
    xr = rope * rs * g_ref[lo:hi, :]
    rot = (xr * tabt_ref[0, lo:hi, :] + _shift_rows(xr, half) * tabt_ref[1, lo:hi, :]
           + _shift_rows(xr, -half) * tabt_ref[2, lo:hi, :])
    return jnp.concatenate([xn, rot, jnp.zeros((HEAD_PAD - hi, nope.shape[1]), F32)], axis=0)


def _mla_proj_kernel(xp_ref, xs_ref, mod_ref, g_ref, wdq_ref, nq_ref, wuqt_ref, gq_ref, wdkv_ref, nkv_ref,
                     tabt_ref, qt_ref, ckv_ref, krp_ref):
    x = jnp.where(pl.program_id(0) < N_PROMPT // TB, xp_ref[...], xs_ref[...])
    m = mod_ref[0]
    h = _rms(x, g_ref[...]) * (1.0 + m[1:2]) + m[0:1]
    hb = h.astype(BF16)
    cq = _rms(_dot(hb, wdq_ref[...]), nq_ref[...])
    qt = _dot_nt(wuqt_ref[...], cq.astype(BF16))
    for hd in range(N_HEADS):
        r0 = hd * HEAD_PAD
        qh = _head_norm_rope(qt[r0:r0 + QK_NOPE_DIM, :], qt[r0 + QK_NOPE_DIM:r0 + QK_HEAD_DIM, :], gq_ref, tabt_ref)
        qt_ref[r0:r0 + HEAD_PAD, :] = qh.astype(BF16)
    kva = _dot(hb, wdkv_ref[...])
    ckv_ref[...] = _rms(kva[:, :KV_RANK], nkv_ref[...])
    krp_ref[...] = kva[:, KV_RANK:]


def _mla_proj(x_p, x_s, p, layer):
    j = layer // 2
    n_pb = N_PROMPT // TB

    def tab_idx(i):
        return (0, 0, jnp.where(i < n_pb, ROPE_BLOCKS, (i - n_pb) % ROPE_BLOCKS))

    return pl.pallas_call(
        _mla_proj_kernel,
        grid=(N_TOK // TB,),
        in_specs=_split_rows_specs() + [
            _mod_spec(TB, layer),
            _layer_spec((1, D_MODEL), layer),
            _layer_spec((D_MODEL, Q_RANK), j),
            _layer_spec((1, Q_RANK), j),
            _layer_spec((QKV_W, Q_RANK), j),
            _layer_spec((HEAD_PAD, 1), j),
            _layer_spec((D_MODEL, 2 * LANE), j),
            _layer_spec((1, KV_RANK), j),
            pl.BlockSpec((3, HEAD_PAD, TB), tab_idx),
        ],
        out_specs=[
            pl.BlockSpec((QKV_W, TB), lambda i: (0, i)),
            pl.BlockSpec((TB, KV_RANK), lambda i: (i, 0)),
            pl.BlockSpec((TB, HEAD_PAD), lambda i: (i, 0)),
        ],
        out_shape=[
            jax.ShapeDtypeStruct((QKV_W, N_TOK), BF16),
            jax.ShapeDtypeStruct((N_TOK, KV_RANK), F32),
            jax.ShapeDtypeStruct((N_TOK, HEAD_PAD), F32),
        ],
        compiler_params=_cparams(("parallel",)),
        name="mla_proj",
    )(x_p, x_s, p["mod"], p["norm_mix_g"], p["m_w_dq"], p["m_norm_q"], p["m_w_uq_t"], p["m_gq"], p["m_w_dkv"],
      p["m_norm_kv"], p["rope_t"])


def _kv_block_source(i):
    n_sb = DEC_BATCH * KV_LEN // TB
    per = KV_LEN // TB
    b = jnp.minimum(i // per, DEC_BATCH - 1)
    jj = i % per
    return jnp.logical_and(i < n_sb, jj < PAST_LEN // TB), b, jj


def _kv_expand_kernel(cckv_ref, ckrp_ref, ckv_ref, krp_ref, wukvt_ref, gk_ref, tabt_ref, k_ref, vt_ref):
    from_cache, _, _ = _kv_block_source(pl.program_id(0))
    ckv = jnp.where(from_cache, cckv_ref[...], ckv_ref[...])
    krp = jnp.where(from_cache, ckrp_ref[...], krp_ref[...])
    kvt = _dot_nt(wukvt_ref[...], ckv.astype(BF16))
    k_rope = krp.T[QK_NOPE_DIM:QK_HEAD_DIM, :]
    row = lax.broadcasted_iota(jnp.int32, (QK_NOPE_DIM, TB), 0)
    ones_row = jnp.where(row == 0, 1.0, 0.0)
    for hd in range(N_HEADS):
        r0 = hd * HEAD_PAD
        kh = _head_norm_rope(kvt[r0:r0 + QK_NOPE_DIM, :], k_rope, gk_ref, tabt_ref)
        k_ref[:, r0:r0 + HEAD_PAD] = kh.T.astype(BF16)
        vt_ref[r0:r0 + HEAD_PAD, :] = jnp.concatenate([ones_row, kvt[r0 + QK_NOPE_DIM:r0 + HEAD_PAD, :]],
                                                      axis=0).astype(BF16)


def _kv_expand(ckv, krp, p, layer):
    j = layer // 2
    n_sb = DEC_BATCH * KV_LEN // TB
    n_cache = PAST_LEN // TB
    n_pb = N_PROMPT // TB
    lat_blocks = DEC_SEQ // TB

    def tab_idx(i):
        from_cache, _, jj = _kv_block_source(i)
        return (0, 0, jnp.where((i >= n_sb) | from_cache, ROPE_BLOCKS, jj - n_cache))

    def cache_idx(i):
        _, b, jj = _kv_block_source(i)
        return (b, j, jnp.minimum(jj, n_cache - 1), 0)

    def tok_idx(i):
        _, b, jj = _kv_block_source(i)
        latent = n_pb + b * lat_blocks + jnp.maximum(jj - n_cache, 0)
        return (jnp.where(i < n_sb, latent, i - n_sb), 0)

    return pl.pallas_call(
        _kv_expand_kernel,
        grid=(N_KV_ROWS // TB,),
        in_specs=[
            pl.BlockSpec((None, None, TB, KV_RANK), cache_idx),
            pl.BlockSpec((None, None, TB, HEAD_PAD), cache_idx),
            pl.BlockSpec((TB, KV_RANK), tok_idx),
            pl.BlockSpec((TB, HEAD_PAD), tok_idx),
            _layer_spec((QKV_W, KV_RANK), j),
            _layer_spec((HEAD_PAD, 1), j),
            pl.BlockSpec((3, HEAD_PAD, TB), tab_idx),
        ],
        out_specs=[
            pl.BlockSpec((TB, QKV_W), lambda i: (i, 0)),
            pl.BlockSpec((QKV_W, TB), lambda i: (0, i)),
        ],
        out_shape=[
            jax.ShapeDtypeStruct((N_KV_ROWS, QKV_W), BF16),
            jax.ShapeDtypeStruct((QKV_W, N_KV_ROWS), BF16),
        ],
        compiler_params=_cparams(("parallel",)),
        name="mla_kv_expand",
    )(p["cache_ckv"], p["cache_krp"], ckv, krp, p["m_w_ukv_t"], p["m_gk"], p["rope_t"])


def _attn_kernel(qt_ref, k_ref, vt_ref, o_ref, s_a, s_b, *, t_k, hps, qps):
    c = (1.0 / math.sqrt(QK_HEAD_DIM)) * math.log2(math.e)
    kc = min(TKC, t_k)
    n_chunks = t_k // kc
    per_it = min(ATTN_CHUNKS_PER_ITER, n_chunks)
    n_it = n_chunks // per_it
    bufs = (s_a, s_b)

    units = [(qb, h) for qb in range(qps) for h in range(hps)]

    def rows(h):
        return slice(h * HEAD_PAD, (h + 1) * HEAD_PAD)

    def cols(qb):
        return slice(qb * TQ, (qb + 1) * TQ)

    def scores(u, off, m8):
        qb, h = units[u]
        st = _dot(k_ref[pl.ds(off, kc), rows(h)], qt_ref[rows(h), cols(qb)])
        bufs[u % 2][pl.ds(off, kc), :] = st
        return jnp.maximum(m8, jnp.max(st.reshape(kc // SUBLANE, SUBLANE, TQ), axis=0))

    def weigh(u, off, m, acc):
        pt = jnp.exp2((bufs[u % 2][pl.ds(off, kc), :] - m) * c).astype(BF16)
        return acc + _dot(vt_ref[rows(units[u][1]), pl.ds(off, kc)], pt)

    def phase(u_scores, u_weigh, m):
        def body(it, carry):
            m8, acc = carry
            for j in range(per_it):
                off = (it * per_it + j) * kc
                off = off if isinstance(off, int) else pl.multiple_of(off, kc)
                if u_scores is not None:
                    m8 = scores(u_scores, off, m8)
                if u_weigh is not None:
                    acc = weigh(u_weigh, off, m, acc)
            return m8, acc

        init = (jnp.full((SUBLANE, TQ), NEG_BIG, F32), jnp.zeros((HEAD_PAD, TQ), F32))
        return body(0, init) if n_it == 1 else lax.fori_loop(0, n_it, body, init)

    outs = []
    if n_chunks == 1:
        sts = [_dot(k_ref[:, rows(h)], qt_ref[rows(h), cols(qb)]) for qb, h in units]
        for (qb, h), st in zip(units, sts):
            m = jnp.max(st, axis=0, keepdims=True)
            acc = _dot(vt_ref[rows(h), :], jnp.exp2((st - m) * c).astype(BF16))
            outs.append(acc[QK_NOPE_DIM:, :] / acc[0:1, :])
    else:
        m8, _ = phase(0, None, None)
        for u in range(1, len(units) + 1):
            m = jnp.max(m8, axis=0, keepdims=True)
            m8, acc = phase(u if u < len(units) else None, u - 1, m)
            outs.append(acc[QK_NOPE_DIM:, :] / acc[0:1, :])
    for qb in range(qps):
        o_ref[cols(qb), :] = jnp.concatenate(outs[qb * hps:(qb + 1) * hps], axis=0).T.astype(BF16)


def _attention(qt, k, vt, *, n_batch, t_q, t_k, q_row0, kv_row0, hps, qps):
    tq = TQ * qps
    nq = t_q // tq
    q0 = q_row0 // tq
    k0 = kv_row0 // t_k
    return pl.pallas_call(
        functools.partial(_attn_kernel, t_k=t_k, hps=hps, qps=qps),
        grid=(n_batch, N_HEADS // hps, nq),
        in_specs=[
            pl.BlockSpec((hps * HEAD_PAD, tq), lambda b, h, i: (h, q0 + b * nq + i)),
            pl.BlockSpec((t_k, hps * HEAD_PAD), lambda b, h, i: (k0 + b, h)),
            pl.BlockSpec((hps * HEAD_PAD, t_k), lambda b, h, i: (h, k0 + b)),
        ],
        out_specs=pl.BlockSpec((tq, hps * V_HEAD_DIM), lambda b, h, i: (b * nq + i, h)),
        out_shape=jax.ShapeDtypeStruct((n_batch * t_q, N_HEADS * V_HEAD_DIM), BF16),
        scratch_shapes=[pltpu.VMEM((t_k, TQ), F32), pltpu.VMEM((t_k, TQ), F32)],
        compiler_params=_cparams(("parallel", "parallel", "parallel")),
        name=f"mla_attention_tk{t_k}",
    )(qt, k, vt)


def _out_proj_kernel(ap_ref, as_ref, xp_ref, xs_ref, mod_ref, wo_ref, o_ref):
    is_prompt = pl.program_id(0) < N_PROMPT // WIDE_TB
    a = jnp.where(is_prompt, ap_ref[...], as_ref[...])
    x = jnp.where(is_prompt, xp_ref[...], xs_ref[...])
    o_ref[...] = x + mod_ref[0][2:3] * _dot(a, wo_ref[...])


def _out_proj(attn_p, attn_s, x_p, x_s, p, layer):
    return pl.pallas_call(
        _out_proj_kernel,
        grid=(N_TOK // WIDE_TB,),
        in_specs=_split_rows_specs(WIDE_TB) + _split_rows_specs(WIDE_TB) + [
            _mod_spec(WIDE_TB, layer),
            _layer_spec((D_MODEL, D_MODEL), layer // 2),
        ],
        out_specs=pl.BlockSpec((WIDE_TB, D_MODEL), lambda i: (i, 0)),
        out_shape=jax.ShapeDtypeStruct((N_TOK, D_MODEL), F32),
        compiler_params=_cparams(("parallel",)),
        name="mla_out_proj",
    )(attn_p, attn_s, x_p, x_s, p["mod"], p["m_w_o"])


def _mla_layer(x_p, x_s, p, layer):
    qt, ckv, krp = _mla_proj(x_p, x_s, p, layer)
    k, vt = _kv_expand(ckv, krp, p, layer)
    a_p = _attention(qt, k, vt, n_batch=BATCH, t_q=SEQ, t_k=SEQ, q_row0=0, kv_row0=DEC_BATCH * KV_LEN,
                     hps=N_HEADS, qps=1)
    a_s = _attention(qt, k, vt, n_batch=DEC_BATCH, t_q=DEC_SEQ, t_k=KV_LEN, q_row0=N_PROMPT, kv_row0=0,
                     hps=4, qps=4)
    x = _out_proj(a_p, a_s, x_p, x_s, p, layer)
    state_ckv = ckv[:N_PROMPT].reshape(BATCH, SEQ, KV_RANK)
    state_krope = krp[:N_PROMPT, QK_NOPE_DIM:QK_HEAD_DIM].reshape(BATCH, SEQ, QK_ROPE_DIM)
    return x, state_ckv, state_krope


def _tile_plan(cnt8, plan_ref, start_ref):
    tm = float(TM)
    cnt_row = cnt8[0:1, :]
    cnt_col = cnt8.T[:, 0:1]
    tiles_row = jnp.floor((cnt_row + (tm - 1.0)) * (1.0 / tm))
    tiles_col = jnp.floor((cnt_col + (tm - 1.0)) * (1.0 / tm))
    sub = lax.broadcasted_iota(jnp.int32, (LANE, LANE), 0).astype(F32)
    lan = lax.broadcasted_iota(jnp.int32, (LANE, LANE), 1).astype(F32)
    tile_end_row = jnp.sum(jnp.where(sub <= lan, tiles_col, 0.0), axis=0, keepdims=True)
    tile_end_col = jnp.sum(jnp.where(lan <= sub, tiles_row, 0.0), axis=1, keepdims=True)
    n_used = jnp.max(tile_end_row, axis=1, keepdims=True)
    start_col = (tile_end_col - tiles_col) * tm
    end_col = start_col + cnt_col
    cand = jnp.where(jnp.logical_and(lan > sub, tiles_row > 0.0), lan, float(LANE))
    next_col = jnp.min(cand, axis=1, keepdims=True)
    next_col = jnp.where(next_col < float(LANE), next_col, -1.0)
    n_lanes = plan_ref.shape[1]
    tidx = jnp.minimum(lax.broadcasted_iota(jnp.int32, (LANE, n_lanes), 1).astype(F32), n_used - 1.0)
    esub = lax.broadcasted_iota(jnp.int32, (LANE, n_lanes), 0).astype(F32)
    te_row = jnp.sum(jnp.where(tile_end_col <= tidx, 1.0, 0.0), axis=0, keepdims=True)
    mine = esub == te_row
    end_at = jnp.sum(jnp.where(mine, end_col, 0.0), axis=0, keepdims=True)
    tv_row = jnp.clip(end_at - tidx[0:1, :] * tm, 0.0, tm)
    nx_row = jnp.sum(jnp.where(mine, next_col, 0.0), axis=0, keepdims=True)
    nu_row = jnp.broadcast_to(n_used, (1, n_lanes))
    plan_ref[...] = jnp.concatenate([te_row, tv_row, nx_row, nu_row, jnp.zeros((SUBLANE - 4, n_lanes), F32)], axis=0)
    start_ref[...] = jnp.broadcast_to(start_col, (LANE, LANE))


def _route_kernel(x_ref, mod_ref, g_ref, wr_ref, br_ref, tri_ref, h_ref, meta_ref, metat_ref, plan_ref, start_ref,
                  carry):
    i = pl.program_id(0)

    @pl.when(i == 0)
    def _():
        carry[...] = jnp.zeros_like(carry)

    x = x_ref[...]
    m = mod_ref[0]
    h = _rms(x, g_ref[...]) * (1.0 + m[4:5]) + m[3:4]
    h_ref[...] = _pack_bf16_pairs(h)
    logits = _dot(h.astype(BF16), wr_ref[...]) + br_ref[...]
    lane = lax.broadcasted_iota(jnp.int32, logits.shape, 1).astype(F32)
    work = logits
    sel = jnp.zeros(logits.shape, F32)
    hits, tops = [], []
    for k in range(TOP_K):
        mk = jnp.max(work, axis=-1, keepdims=True)
        first = jnp.min(jnp.where(work == mk, lane, float(LANE)), axis=-1, keepdims=True)
        hit = lane == first
        sel = jnp.where(hit, 1.0, sel)
        work = jnp.where(hit, -jnp.inf, work)
        hits.append((hit, first))
        tops.append(mk)
    es = [jnp.exp(t - tops[0]) for t in tops]
    denom = es[0] + es[1] + es[2] + es[3]
    pos = _dot(tri_ref[...], sel.astype(BF16)) + carry[0:1, :]
    carry[...] = carry[...] + jnp.sum(sel, axis=0, keepdims=True)

    @pl.when(i == pl.num_programs(0) - 1)
    def _():
        _tile_plan(carry[...], plan_ref, start_ref)

    meta = jnp.zeros(logits.shape, F32)
    for k in range(TOP_K):
        hit, first = hits[k]
        pk = jnp.sum(jnp.where(hit, pos, 0.0), axis=-1, keepdims=True)
        meta = jnp.where(lane == float(k), first, meta)
        meta = jnp.where(lane == float(TOP_K + k), es[k] / denom, meta)
        meta = jnp.where(lane == float(2 * TOP_K + k), pk, meta)
    meta_ref[...] = meta
    metat_ref[...] = meta.T[:2 * SUBLANE, :]


def _route(x, p, layer):
    tb = ROUTE_TB
    return pl.pallas_call(
        _route_kernel,
        grid=(N_TOK // tb,),
        in_specs=[
            pl.BlockSpec((tb, D_MODEL), lambda i: (i, 0)),
            _mod_spec(tb, layer),
            _layer_spec((1, D_MODEL), layer),
            _layer_spec((D_MODEL, LANE), layer),
            _layer_spec((1, LANE), layer),
            _const_spec((tb, tb)),
        ],
        out_specs=[
            pl.BlockSpec((tb, D_MODEL // 2), lambda i: (i, 0)),
            pl.BlockSpec((tb, LANE), lambda i: (i, 0)),
            pl.BlockSpec((2 * SUBLANE, tb), lambda i: (0, i)),
            _const_spec((SUBLANE, PLAN_LANES)),
            _const_spec((LANE, LANE)),
        ],
        out_shape=[
            jax.ShapeDtypeStruct((N_TOK, D_MODEL // 2), jnp.uint32),
            jax.ShapeDtypeStruct((N_TOK, LANE), F32),
            jax.ShapeDtypeStruct((2 * SUBLANE, N_TOK), F32),
            jax.ShapeDtypeStruct((SUBLANE, PLAN_LANES), F32),
            jax.ShapeDtypeStruct((LANE, LANE), F32),
        ],
        scratch_shapes=[pltpu.VMEM((SUBLANE, LANE), F32)],
        compiler_params=_cparams(("arbitrary",)),
        name="moe_route",
    )(x, p["mod"], p["norm_ffn_g"], p["e_w_router"], p["e_b_router"], p["tri"])


def _slots_kernel(start_ref, metat_ref, dest_ref):
    start_col = start_ref[:, 0:1]
    esub = lax.broadcasted_iota(jnp.int32, (LANE, SLOT_TB), 0).astype(F32)
    rows = []
    for k in range(TOP_K):
        e = metat_ref[k:k + 1, :]
        first = jnp.sum(jnp.where(esub == e, start_col, 0.0), axis=0, keepdims=True)
        rows.append(first + metat_ref[2 * TOP_K + k:2 * TOP_K + k + 1, :])
    dest_ref[...] = jnp.concatenate(rows, axis=0).astype(jnp.int32)


def _slots(start, meta_t):
    return pl.pallas_call(
        _slots_kernel,
        grid=(N_TOK // SLOT_TB,),
        in_specs=[
            _const_spec((LANE, LANE)),
            pl.BlockSpec((2 * SUBLANE, SLOT_TB), lambda i: (0, i)),
        ],
        out_specs=pl.BlockSpec((TOP_K, SLOT_TB), lambda i: (0, i)),
        out_shape=jax.ShapeDtypeStruct((TOP_K, N_TOK), jnp.int32),
        compiler_params=_cparams(("parallel",)),
        name="moe_slots",
    )(start, meta_t)


def _sc_gather(table, idx, ch):
    b, w = idx.shape[0], table.shape[1]
    per_w = b // SC_WORKERS
    n_ch = per_w // ch
    assert per_w * SC_WORKERS == b and n_ch * ch == per_w and n_ch % 2 == 0
    mesh = plsc.VectorSubcoreMesh(core_axis_name="c", subcore_axis_name="s")

    @functools.partial(
        pl.kernel, mesh=mesh,
        out_type=jax.ShapeDtypeStruct((b, w), table.dtype),
        scratch_types=[
            pltpu.VMEM((n_ch, ch), jnp.int32),
            pltpu.VMEM((ch, w), table.dtype),
            pltpu.VMEM((ch, w), table.dtype),
            pltpu.SemaphoreType.DMA, pltpu.SemaphoreType.DMA,
            pltpu.SemaphoreType.DMA, pltpu.SemaphoreType.DMA,
        ],
        name="sc_row_gather",
    )
    def gather_rows(table_hbm, idx_hbm, out_hbm, idx_v, buf0, buf1, g0, g1, s0, s1):
        wid = lax.axis_index("s") * SC_CORES + lax.axis_index("c")
        base = wid * per_w
        pltpu.sync_copy(idx_hbm.at[wid], idx_v)

        def gather(j, buf, sem):
            return pltpu.make_async_copy(table_hbm.at[idx_v.at[j]], buf, sem)

        def store(j, buf, sem):
            return pltpu.make_async_copy(buf, out_hbm.at[pl.ds(base + j * ch, ch)], sem)

        gather(0, buf0, g0).start()

        @pl.loop(0, n_ch, step=2)
        def _(j):
            @pl.when(j > 0)
            def _():
                store(j - 1, buf1, s1).wait()

            gather(j + 1, buf1, g1).start()
            gather(j, buf0, g0).wait()
            store(j, buf0, s0).start()
            gather(j + 1, buf1, g1).wait()
            store(j + 1, buf1, s1).start()
            store(j, buf0, s0).wait()

            @pl.when(j + 2 < n_ch)
            def _():
                gather(j + 2, buf0, g0).start()

        store(n_ch - 1, buf1, s1).wait()

    return gather_rows(table, idx.reshape(SC_WORKERS, n_ch, ch))


def _sc_dispatch(rows, dest_t, ch):
    n, w = rows.shape
    per_w = n // SC_WORKERS
    n_ch = per_w // ch
    assert per_w * SC_WORKERS == n and n_ch * ch == per_w and n_ch % 2 == 0
    mesh = plsc.VectorSubcoreMesh(core_axis_name="c", subcore_axis_name="s")
    idx = dest_t.reshape(TOP_K, SC_WORKERS, n_ch, ch)

    @functools.partial(
        pl.kernel, mesh=mesh,
        out_type=jax.ShapeDtypeStruct((N_SLOTS, w), rows.dtype),
        scratch_types=[
            pltpu.VMEM((TOP_K * n_ch, ch), jnp.int32),
            pltpu.VMEM((ch, w), rows.dtype),
            pltpu.VMEM((ch, w), rows.dtype),
            pltpu.SemaphoreType.DMA, pltpu.SemaphoreType.DMA,
            pltpu.SemaphoreType.DMA, pltpu.SemaphoreType.DMA,
        ],
        name="sc_row_dispatch",
    )
    def dispatch_rows(rows_hbm, idx_hbm, out_hbm, idx_v, buf0, buf1, l0, l1, s0, s1):
        wid = lax.axis_index("s") * SC_CORES + lax.axis_index("c")
        base = wid * per_w
        for k in range(TOP_K):
            pltpu.sync_copy(idx_hbm.at[k, wid], idx_v.at[pl.ds(k * n_ch, n_ch)])

        def load(j, buf, sem):
            return pltpu.make_async_copy(rows_hbm.at[pl.ds(base + j * ch, ch)], buf, sem)

        def scatter(j, k, buf, sem):
            return pltpu.make_async_copy(buf, out_hbm.at[idx_v.at[k * n_ch + j]], sem)

        load(0, buf0, l0).start()

        @pl.loop(0, n_ch, step=2)
        def _(j):
            load(j + 1, buf1, l1).start()
            load(j, buf0, l0).wait()
            for k in range(TOP_K):
                scatter(j, k, buf0, s0).start()
            load(j + 1, buf1, l1).wait()
            for k in range(TOP_K):
                scatter(j + 1, k, buf1, s1).start()
            for k in range(TOP_K):
                scatter(j, k, buf0, s0).wait()

            @pl.when(j + 2 < n_ch)
            def _():
                load(j + 2, buf0, l0).start()

            for k in range(TOP_K):
                scatter(j + 1, k, buf1, s1).wait()

    return dispatch_rows(rows, idx)


def _deinterleave_matrix():
    src = jnp.arange(2 * LANE)[:, None]
    dst = jnp.arange(2 * LANE)[None, :]
    want = jnp.where(dst < LANE, 2 * dst, 2 * (dst - LANE) + 1)
    return (src == want).astype(BF16)


def _expert_kernel(te_ref, nu_ref, tv_ref, nx_ref, x_ref, wgu_hbm, bgu_ref, wd_hbm, bd_ref, perm_ref, o_ref,
                   wgu_st, wd_st, wgu_bf, wd_bf, sems, *, layer):
    i = pl.program_id(0)
    prev = te_ref[jnp.maximum(i - 1, 0)]
    fresh = jnp.logical_or(i == 0, te_ref[i] != prev)

    def fetch(e):
        return (pltpu.make_async_copy(wgu_hbm.at[layer, e], wgu_st, sems.at[0]),
                pltpu.make_async_copy(wd_hbm.at[layer, e], wd_st, sems.at[1]))

    @pl.when(i == 0)
    def _():
        for cp in fetch(te_ref[0]):
            cp.start()

    @pl.when(jnp.logical_and(fresh, i < nu_ref[0]))
    def _():
        for cp in fetch(te_ref[i]):
            cp.wait()
        for b in range(2 * D_FF // (2 * LANE)):
            sl = slice(b * 2 * LANE, (b + 1) * 2 * LANE)
            wgu_bf[:, sl] = _dot(wgu_st[:, sl].astype(BF16), perm_ref[...]).astype(BF16)
        wd_bf[...] = wd_st[...].astype(BF16)

        @pl.when(nx_ref[i] >= 0)
        def _():
            for cp in fetch(nx_ref[i]):
                cp.start()

    def ffn(n_rows):
        row = lax.broadcasted_iota(jnp.int32, (n_rows, D_MODEL // 2), 0)
        w = jnp.where(row < tv_ref[i], x_ref[:n_rows, :], jnp.uint32(0))
        x = _unpack_bf16_pairs(w).astype(BF16)
        gu = _dot(x, wgu_bf[...]) + bgu_ref[...]
        acts = []
        for b in range(D_FF // LANE):
            glu = jnp.minimum(gu[:, b * 2 * LANE:b * 2 * LANE + LANE], SWIGLU_LIMIT)
            lin = jnp.clip(gu[:, b * 2 * LANE + LANE:(b + 1) * 2 * LANE], -SWIGLU_LIMIT, SWIGLU_LIMIT)
            acts.append((glu * jax.nn.sigmoid(SWIGLU_ALPHA * glu) * (lin + 1.0)).astype(BF16))
        act = jnp.concatenate(acts, axis=1)
        o_ref[:n_rows, :] = _pack_bf16_pairs(_dot(act, wd_bf[...]) + bd_ref[...])

    in_use = i < nu_ref[0]

    @pl.when(jnp.logical_and(in_use, tv_ref[i] > TM // 2))
    def _():
        ffn(TM)

    @pl.when(jnp.logical_and(in_use, tv_ref[i] <= TM // 2))
    def _():
        ffn(TM // 2)


def _experts(buf, tile_expert, n_used, tile_valid, tile_next, p, layer):
    def row_idx(i, te, nu, tv, nx):
        return (jnp.minimum(i, nu[0] - 1), 0)

    def b_idx(i, te, nu, tv, nx):
        return (layer, te[i], 0, 0)

    grid_spec = pltpu.PrefetchScalarGridSpec(
        num_scalar_prefetch=4,
        grid=(N_TILES,),
        in_specs=[
            pl.BlockSpec((TM, D_MODEL // 2), row_idx),
            pl.BlockSpec(memory_space=pl.ANY),
            pl.BlockSpec((None, None, 1, 2 * D_FF), b_idx),
            pl.BlockSpec(memory_space=pl.ANY),
            pl.BlockSpec((None, None, 1, D_MODEL), b_idx),
            _const_spec((2 * LANE, 2 * LANE)),
        ],
        out_specs=pl.BlockSpec((TM, D_MODEL // 2), row_idx),
        scratch_shapes=[
            pltpu.VMEM((D_MODEL, 2 * D_FF), F32),
            pltpu.VMEM((D_FF, D_MODEL), F32),
            pltpu.VMEM((D_MODEL, 2 * D_FF), BF16),
            pltpu.VMEM((D_FF, D_MODEL), BF16),
            pltpu.SemaphoreType.DMA((2,)),
        ],
    )
    return pl.pallas_call(
        functools.partial(_expert_kernel, layer=layer),
        grid_spec=grid_spec,
        out_shape=jax.ShapeDtypeStruct((N_SLOTS, D_MODEL // 2), jnp.uint32),
        compiler_params=_cparams(("arbitrary",)),
        name="moe_experts",
    )(tile_expert, n_used, tile_valid, tile_next, buf, p["e_w_gu"], p["e_b_gu"], p["e_w_down"], p["e_b_down"],
      p["deinterleave"])


def _pack_bf16_pairs(v):
    half = v.shape[1] // 2
    bits = pltpu.bitcast(v.astype(BF16).astype(F32), jnp.uint32)
    return (bits[:, half:] & jnp.uint32(0xFFFF0000)) | (bits[:, :half] >> 16)


def _unpack_bf16_pairs(w):
    return jnp.concatenate([pltpu.bitcast(w << 16, F32), pltpu.bitcast(w & jnp.uint32(0xFFFF0000), F32)],
                           axis=1)


def _combine_kernel(x_ref, mod_ref, y_ref, w_ref, o_ref):
    w = w_ref[:, TOP_K:2 * TOP_K]
    y = _unpack_bf16_pairs(y_ref[0]) * w[:, 0:1]
    for k in range(1, TOP_K):
        y = y + _unpack_bf16_pairs(y_ref[k]) * w[:, k:k + 1]
    o_ref[...] = x_ref[...] + mod_ref[0][5:6] * y


def _combine(x, y4, meta, p, layer, part):
    n_rows = N_TOK // MOE_PARTS
    tb = WIDE_TB
    first = part * n_rows // tb
    return pl.pallas_call(
        _combine_kernel,
        grid=(n_rows // tb,),
        in_specs=[
            pl.BlockSpec((tb, D_MODEL), lambda i: (i + first, 0)),
            _mod_spec(tb, layer, first),
            pl.BlockSpec((TOP_K, tb, D_MODEL // 2), lambda i: (0, i, 0)),
            pl.BlockSpec((tb, LANE), lambda i: (i + first, 0)),
        ],
        out_specs=pl.BlockSpec((tb, D_MODEL), lambda i: (i, 0)),
        out_shape=jax.ShapeDtypeStruct((n_rows, D_MODEL), F32),
        compiler_params=_cparams(("parallel",)),
        name="moe_combine",
    )(x, p["mod"], y4, meta)


def _moe_layer(x, p, layer):
    hp, meta, meta_t, plan, start = _route(x, p, layer)
    plan = plan[:4, :N_TILES].astype(jnp.int32)
    tile_expert, tile_valid, tile_next, n_used = plan[0], plan[1], plan[2], plan[3, :1]
    dest_t = _slots(start, meta_t)
    buf = _sc_dispatch(hp, dest_t, 64)
    yb = _experts(buf, tile_expert, n_used, tile_valid, tile_next, p, layer)
    n_rows = N_TOK // MOE_PARTS
    outs = []
    for part in range(MOE_PARTS):
        idx = dest_t[:, part * n_rows:(part + 1) * n_rows].reshape(-1)
        y4 = _sc_gather(yb, idx, 64).reshape(TOP_K, n_rows, D_MODEL // 2)
        outs.append(_combine(x, y4, meta, p, layer, part))
    return outs


def _prepare(c, cache_ckv, cache_krope, c_ctx, norm_mix_g, norm_ffn_g, w_mod, b_mod, g_w_in, g_b_in, g_norm_v,
             g_w_s, g_b_s, g_w_out, m_w_dq, m_norm_q, m_w_uq, m_w_dkv, m_norm_kv, m_w_ukv, m_qk_norm_q,
             m_qk_norm_k, m_w_o, e_w_router, e_b_router, e_w_gu, e_b_gu, e_w_down, e_b_down):
    n_mla = m_w_dq.shape[0]
    cond = jnp.concatenate([c_ctx[None, :], c, jnp.zeros((SUBLANE - N_COND, D_MODEL), F32)], axis=0)
    wdkv = jnp.concatenate([m_w_dkv[..., :KV_RANK], jnp.zeros((n_mla, D_MODEL, QK_NOPE_DIM), F32),
                            m_w_dkv[..., KV_RANK:], jnp.zeros((n_mla, D_MODEL, HEAD_PAD - QK_HEAD_DIM), F32)],
                           axis=-1)
    w_uq = jnp.pad(m_w_uq.reshape(n_mla, Q_RANK, N_HEADS, QK_HEAD_DIM),
                   ((0, 0), (0, 0), (0, 0), (0, HEAD_PAD - QK_HEAD_DIM))).reshape(n_mla, Q_RANK, QKV_W)

    def gain_col(g):
        return jnp.pad(g, ((0, 0), (0, HEAD_PAD - QK_HEAD_DIM)))[:, :, None]

    return {
        "mod": _modulation(cond, w_mod, b_mod),
        "rope_t": _rope_tables(),
        "norm_mix_g": norm_mix_g[:, None, :],
        "norm_ffn_g": norm_ffn_g[:, None, :],
        "g_w_in": g_w_in.astype(BF16),
        "g_b_in": g_b_in[:, None, :],
        "g_norm_v": g_norm_v[:, None, :],
        "g_w_s": g_w_s.astype(BF16),
        "g_b_st": jnp.swapaxes(g_b_s, 1, 2),
        "g_w_out": g_w_out.astype(BF16),
        "m_w_dq": m_w_dq.astype(BF16),
        "m_norm_q": m_norm_q[:, None, :],
        "m_w_uq_t": jnp.swapaxes(w_uq, 1, 2).astype(BF16),
        "m_gq": gain_col(m_qk_norm_q),
        "m_w_dkv": wdkv.astype(BF16),
        "m_norm_kv": m_norm_kv[:, None, :],
        "m_w_ukv_t": jnp.swapaxes(m_w_ukv, 1, 2).astype(BF16),
        "m_gk": gain_col(m_qk_norm_k),
        "m_w_o": m_w_o.astype(BF16),
        "cache_ckv": cache_ckv,
        "cache_krp": jnp.pad(cache_krope, ((0, 0), (0, 0), (0, 0), (QK_NOPE_DIM, HEAD_PAD - QK_HEAD_DIM))),
        "e_w_router": jnp.pad(e_w_router, ((0, 0), (0, 0), (0, LANE - N_EXPERTS))).astype(BF16),
        "e_b_router": jnp.pad(e_b_router, ((0, 0), (0, LANE - N_EXPERTS)), constant_values=NEG_BIG)[:, None, :],
        "tri": jnp.tri(ROUTE_TB, ROUTE_TB, -1, dtype=BF16),
        "e_w_gu": e_w_gu,
        "e_b_gu": e_b_gu.reshape(DEPTH, N_EXPERTS, D_FF // LANE, LANE, 2).swapaxes(3, 4).reshape(
            DEPTH, N_EXPERTS, 1, 2 * D_FF),
        "e_w_down": e_w_down,
        "e_b_down": e_b_down[:, :, None, :],
        "deinterleave": _deinterleave_matrix(),
    }


def kernel(x_prompt, x_sample, c, cache_ckv, cache_krope, c_ctx, norm_mix_g, norm_ffn_g, w_mod, b_mod,
           g_w_in, g_b_in, g_norm_v, g_w_s, g_b_s, g_w_out, m_w_dq, m_norm_q, m_w_uq, m_w_dkv,
           m_norm_kv, m_w_ukv, m_qk_norm_q, m_qk_norm_k, m_w_o, e_w_router, e_b_router, e_w_gu,
           e_b_gu, e_w_down, e_b_down):
    p = _prepare(c, cache_ckv, cache_krope, c_ctx, norm_mix_g, norm_ffn_g, w_mod, b_mod, g_w_in, g_b_in,
                 g_norm_v, g_w_s, g_b_s, g_w_out, m_w_dq, m_norm_q, m_w_uq, m_w_dkv, m_norm_kv, m_w_ukv,
                 m_qk_norm_q, m_qk_norm_k, m_w_o, e_w_router, e_b_router, e_w_gu, e_b_gu, e_w_down, e_b_down)
    assert MOE_PARTS == 2 and N_PROMPT == N_SAMPLE
    x_p, x_s = x_prompt.reshape(N_PROMPT, D_MODEL), x_sample.reshape(N_SAMPLE, D_MODEL)
    ckv_states, krope_states = [], []
    for layer in range(DEPTH):
        if layer % 2 == 0:
            x = _gmlp_layer(x_p, x_s, p, layer)
        else:
            x, s_ckv, s_krope = _mla_layer(x_p, x_s, p, layer)
            ckv_states.append(s_ckv)
            krope_states.append(s_krope)
        x_p, x_s = _moe_layer(x, p, layer)
    y_prompt = x_p.reshape(BATCH, SEQ, D_MODEL)
    y_sample = x_s.reshape(DEC_BATCH, DEC_SEQ, D_MODEL)
    return (y_prompt, y_sample, jnp.stack(ckv_states, axis=1), jnp.stack(krope_states, axis=1))
```

```python
import functools
import math

import jax
import jax.numpy as jnp
from jax import lax
from jax.experimental import pallas as pl
from jax.experimental.pallas import tpu as pltpu
from jax.experimental.pallas import tpu_sc as plsc

F32 = jnp.float32
BF16 = jnp.bfloat16

D_MODEL = 1024
BATCH = 32
SEQ = 256
DEPTH = 4
DEC_BATCH = 2
DEC_SEQ = 4096
PAST_LEN = 512
GRID_W = 64
RMS_EPS = 1e-6
GMLP_WIDTH = 2 * D_MODEL
GMLP_GROUPS = 8
GROUP_W = GMLP_WIDTH // GMLP_GROUPS
CHUNK = 128
N_HEADS = 16
QK_NOPE_DIM = 64
QK_ROPE_DIM = 32
QK_HEAD_DIM = QK_NOPE_DIM + QK_ROPE_DIM
V_HEAD_DIM = 64
Q_RANK = 256
KV_RANK = 128
ROPE_THETA = 10000.0
N_EXPERTS = 32
TOP_K = 4
D_FF = D_MODEL
SWIGLU_LIMIT = 7.0
SWIGLU_ALPHA = 1.702

N_PROMPT = BATCH * SEQ
N_SAMPLE = DEC_BATCH * DEC_SEQ
N_TOK = N_PROMPT + N_SAMPLE
N_COND = 1 + DEC_BATCH
KV_LEN = PAST_LEN + DEC_SEQ
N_KV_ROWS = DEC_BATCH * KV_LEN + N_PROMPT

LANE = 128
SUBLANE = 8
HEAD_PAD = LANE
QKV_W = N_HEADS * HEAD_PAD
VMEM_LIMIT = 56 * 1024 * 1024

TB = 512
WIDE_TB = 512
TQ = 256
TKC = 256
ATTN_CHUNKS_PER_ITER = 9
TM = 512
N_TILES = N_TOK * TOP_K // TM + N_EXPERTS
N_SLOTS = N_TILES * TM
PLAN_LANES = -(-N_TILES // LANE) * LANE
SLOT_TB = 2048
ROUTE_TB = 512
MOE_PARTS = 2
SC_CORES = 2
SC_WORKERS = SC_CORES * 16
ROPE_BLOCKS = DEC_SEQ // TB
NEG_BIG = -1e30


def _cparams(sem):
    return pltpu.CompilerParams(dimension_semantics=sem, vmem_limit_bytes=VMEM_LIMIT)


def _cond_of_block(i, tb):
    n_p = N_PROMPT // tb
    per = DEC_SEQ // tb
    return jnp.where(i < n_p, 0, 1 + (i - n_p) // per)


def _rms(x, g, n=None):
    n = x.shape[-1] if n is None else n
    ss = jnp.sum(x * x, axis=-1, keepdims=True) * (1.0 / n)
    return x * lax.rsqrt(ss + RMS_EPS) * g


def _dot(a, b):
    return jnp.dot(a, b, preferred_element_type=F32)


def _mod_kernel(c_ref, w_ref, b_ref, o_ref):
    c = c_ref[...]
    s = c * jax.nn.sigmoid(c)
    o_ref[0] = _dot(s.astype(BF16), w_ref[0].astype(BF16)) + b_ref[0]


def _modulation(cond, w_mod, b_mod):
    tn = 1536
    out = pl.pallas_call(
        _mod_kernel,
        grid=(DEPTH, 6 * D_MODEL // tn),
        in_specs=[
            pl.BlockSpec((SUBLANE, D_MODEL), lambda l, j: (0, 0)),
            pl.BlockSpec((1, D_MODEL, tn), lambda l, j: (l, 0, j)),
            pl.BlockSpec((1, 1, tn), lambda l, j: (l, 0, j)),
        ],
        out_specs=pl.BlockSpec((1, SUBLANE, tn), lambda l, j: (l, 0, j)),
        out_shape=jax.ShapeDtypeStruct((DEPTH, SUBLANE, 6 * D_MODEL), F32),
        compiler_params=_cparams(("parallel", "parallel")),
        name="adaln_mod",
    )(cond, w_mod, b_mod.reshape(DEPTH, 1, 6 * D_MODEL))
    m = out[:, :N_COND].reshape(DEPTH, N_COND, 6, D_MODEL)
    return jnp.pad(m, ((0, 0), (0, 0), (0, SUBLANE - 6), (0, 0)))


def _mod_spec(tb, layer, first_block=0):
    return pl.BlockSpec((None, 1, SUBLANE, D_MODEL),
                        lambda i: (layer, _cond_of_block(i + first_block, tb), 0, 0))


def _const_spec(shape):
    nd = len(shape)
    return pl.BlockSpec(shape, lambda *_: (0,) * nd)


def _layer_spec(shape, j):
    nd = len(shape)
    return pl.BlockSpec((None,) + tuple(shape), lambda *_: (j,) + (0,) * nd)


def _gelu_tanh(x):
    a = math.sqrt(2.0 / math.pi)
    hx = 0.5 * x
    return hx + hx * jnp.tanh(x * (a + (0.044715 * a) * (x * x)))


def _gmlp_kernel(xp_ref, xs_ref, mod_ref, g_ref, win_ref, bin_ref, gv_ref, ws_ref, bst_ref, wout_ref, o_ref):
    x = jnp.where(pl.program_id(0) < N_PROMPT // WIDE_TB, xp_ref[...], xs_ref[...])
    m = mod_ref[0]
    h = _rms(x, g_ref[...]) * (1.0 + m[1:2]) + m[0:1]
    hb = h.astype(BF16)
    zv = _gelu_tanh(_dot(hb, win_ref[:, GMLP_WIDTH:]) + bin_ref[:, GMLP_WIDTH:])
    vn = _rms(zv, gv_ref[...]).astype(BF16)
    rows = []
    for c in range(WIDE_TB // CHUNK):
        cols = []
        for g in range(GMLP_GROUPS):
            blk = vn[c * CHUNK:(c + 1) * CHUNK, g * GROUP_W:(g + 1) * GROUP_W]
            cols.append(_dot(ws_ref[g], blk) + bst_ref[:, g:g + 1])
        rows.append(jnp.concatenate(cols, axis=1))
    vm = jnp.concatenate(rows, axis=0)
    u = _gelu_tanh(_dot(hb, win_ref[:, :GMLP_WIDTH]) + bin_ref[:, :GMLP_WIDTH])
    d = _dot((u * vm).astype(BF16), wout_ref[...])
    o_ref[...] = x + m[2:3] * d


def _split_rows_specs(tb=TB):
    n_pb = N_PROMPT // tb
    return [pl.BlockSpec((tb, D_MODEL), lambda i: (jnp.minimum(i, n_pb - 1), 0)),
            pl.BlockSpec((tb, D_MODEL), lambda i: (jnp.maximum(i - n_pb, 0), 0))]


def _gmlp_layer(x_p, x_s, p, layer):
    j = layer // 2
    return pl.pallas_call(
        _gmlp_kernel,
        grid=(N_TOK // WIDE_TB,),
        in_specs=_split_rows_specs(WIDE_TB) + [
            _mod_spec(WIDE_TB, layer),
            _layer_spec((1, D_MODEL), layer),
            _layer_spec((D_MODEL, 2 * GMLP_WIDTH), j),
            _layer_spec((1, 2 * GMLP_WIDTH), j),
            _layer_spec((1, GMLP_WIDTH), j),
            _layer_spec((GMLP_GROUPS, CHUNK, CHUNK), j),
            _layer_spec((CHUNK, GMLP_GROUPS), j),
            _layer_spec((GMLP_WIDTH, D_MODEL), j),
        ],
        out_specs=pl.BlockSpec((WIDE_TB, D_MODEL), lambda i: (i, 0)),
        out_shape=jax.ShapeDtypeStruct((N_TOK, D_MODEL), F32),
        compiler_params=_cparams(("parallel",)),
        name="gmlp_mixer",
    )(x_p, x_s, p["mod"], p["norm_mix_g"], p["g_w_in"], p["g_b_in"], p["g_norm_v"], p["g_w_s"], p["g_b_st"],
      p["g_w_out"])


def _rope_tables():
    t = jnp.arange(DEC_SEQ)
    row_id = (t // GRID_W).astype(F32)
    col_id = (t % GRID_W).astype(F32)
    axis_dim = QK_ROPE_DIM // 2
    inv_freq = ROPE_THETA ** (-jnp.arange(0, axis_dim, 2, dtype=F32) / axis_dim)
    ang = jnp.stack([row_id[:, None] * inv_freq, col_id[:, None] * inv_freq], axis=1)
    cos, sin = jnp.cos(ang), jnp.sin(ang)
    zeros = jnp.zeros_like(sin)
    cos_l = jnp.concatenate([cos, cos], axis=-1).reshape(DEC_SEQ, QK_ROPE_DIM)
    s1_l = jnp.concatenate([-sin, zeros], axis=-1).reshape(DEC_SEQ, QK_ROPE_DIM)
    s2_l = jnp.concatenate([zeros, sin], axis=-1).reshape(DEC_SEQ, QK_ROPE_DIM)

    def widen(rope_part, nope_fill):
        left = jnp.full((DEC_SEQ, QK_NOPE_DIM), nope_fill, F32)
        right = jnp.zeros((DEC_SEQ, HEAD_PAD - QK_HEAD_DIM), F32)
        return jnp.concatenate([left, rope_part, right], axis=-1)

    pos = jnp.stack([widen(cos_l, 1.0), widen(s1_l, 0.0), widen(s2_l, 0.0)])
    ident_c = jnp.concatenate([jnp.ones((TB, QK_HEAD_DIM), F32),
                               jnp.zeros((TB, HEAD_PAD - QK_HEAD_DIM), F32)], axis=-1)
    ident = jnp.stack([ident_c, jnp.zeros_like(ident_c), jnp.zeros_like(ident_c)])
    return jnp.swapaxes(jnp.concatenate([pos, ident], axis=1), 1, 2)


def _dot_nt(a, b):
    return lax.dot_general(a, b, (((1,), (1,)), ((), ())), preferred_element_type=F32)


def _shift_rows(x, n):
    n = n % x.shape[0]
    return jnp.concatenate([x[n:], x[:n]], axis=0)


def _head_norm_rope(nope, rope, g_ref, tabt_ref):
    half = QK_ROPE_DIM // 4
    lo, hi = QK_NOPE_DIM, QK_HEAD_DIM
    ss = (jnp.sum(nope * nope, axis=0, keepdims=True) + jnp.sum(rope * rope, axis=0, keepdims=True))
    rs = lax.rsqrt(ss * (1.0 / QK_HEAD_DIM) + RMS_EPS)
    xn = nope * rs * g_ref[:lo, :]
    xr = rope * rs * g_ref[lo:hi, :]
    rot = (xr * tabt_ref[0, lo:hi, :] + _shift_rows(xr, half) * tabt_ref[1, lo:hi, :]
           + _shift_rows(xr, -half) * tabt_ref[2, lo:hi, :])
    return jnp.concatenate([xn, rot, jnp.zeros((HEAD_PAD - hi, nope.shape[1]), F32)], axis=0)


def _mla_proj_kernel(xp_ref, xs_ref, mod_ref, g_ref, wdq_ref, nq_ref, wuqt_ref, gq_ref, wdkv_ref, nkv_ref,
                     tabt_ref, qt_ref, ckv_ref, krp_ref):
    x = jnp.where(pl.program_id(0) < N_PROMPT // TB, xp_ref[...], xs_ref[...])
    m = mod_ref[0]
    h = _rms(x, g_ref[...]) * (1.0 + m[1:2]) + m[0:1]
    hb = h.astype(BF16)
    cq = _rms(_dot(hb, wdq_ref[...]), nq_ref[...])
    qt = _dot_nt(wuqt_ref[...], cq.astype(BF16))
    for hd in range(N_HEADS):
        r0 = hd * HEAD_PAD
        qh = _head_norm_rope(qt[r0:r0 + QK_NOPE_DIM, :], qt[r0 + QK_NOPE_DIM:r0 + QK_HEAD_DIM, :], gq_ref, tabt_ref)
        qt_ref[r0:r0 + HEAD_PAD, :] = qh.astype(BF16)
    kva = _dot(hb, wdkv_ref[...])
    ckv_ref[...] = _rms(kva[:, :KV_RANK], nkv_ref[...])
    krp_ref[...] = kva[:, KV_RANK:]


def _mla_proj(x_p, x_s, p, layer):
    j = layer // 2
    n_pb = N_PROMPT // TB

    def tab_idx(i):
        return (0, 0, jnp.where(i < n_pb, ROPE_BLOCKS, (i - n_pb) % ROPE_BLOCKS))

    return pl.pallas_call(
        _mla_proj_kernel,
        grid=(N_TOK // TB,),
        in_specs=_split_rows_specs() + [
            _mod_spec(TB, layer),
            _layer_spec((1, D_MODEL), layer),
            _layer_spec((D_MODEL, Q_RANK), j),
            _layer_spec((1, Q_RANK), j),
            _layer_spec((QKV_W, Q_RANK), j),
            _layer_spec((HEAD_PAD, 1), j),
            _layer_spec((D_MODEL, 2 * LANE), j),
            _layer_spec((1, KV_RANK), j),
            pl.BlockSpec((3, HEAD_PAD, TB), tab_idx),
        ],
        out_specs=[
            pl.BlockSpec((QKV_W, TB), lambda i: (0, i)),
            pl.BlockSpec((TB, KV_RANK), lambda i: (i, 0)),
            pl.BlockSpec((TB, HEAD_PAD), lambda i: (i, 0)),
        ],
        out_shape=[
            jax.ShapeDtypeStruct((QKV_W, N_TOK), BF16),
            jax.ShapeDtypeStruct((N_TOK, KV_RANK), F32),
            jax.ShapeDtypeStruct((N_TOK, HEAD_PAD), F32),
        ],
        compiler_params=_cparams(("parallel",)),
        name="mla_proj",
    )(x_p, x_s, p["mod"], p["norm_mix_g"], p["m_w_dq"], p["m_norm_q"], p["m_w_uq_t"], p["m_gq"], p["m_w_dkv"],
      p["m_norm_kv"], p["rope_t"])


def _kv_block_source(i):
    n_sb = DEC_BATCH * KV_LEN // TB
    per = KV_LEN // TB
    b = jnp.minimum(i // per, DEC_BATCH - 1)
    jj = i % per
    return jnp.logical_and(i < n_sb, jj < PAST_LEN // TB), b, jj


def _kv_expand_kernel(cckv_ref, ckrp_ref, ckv_ref, krp_ref, wukvt_ref, gk_ref, tabt_ref, k_ref, vt_ref):
    from_cache, _, _ = _kv_block_source(pl.program_id(0))
    ckv = jnp.where(from_cache, cckv_ref[...], ckv_ref[...])
    krp = jnp.where(from_cache, ckrp_ref[...], krp_ref[...])
    kvt = _dot_nt(wukvt_ref[...], ckv.astype(BF16))
    k_rope = krp.T[QK_NOPE_DIM:QK_HEAD_DIM, :]
    row = lax.broadcasted_iota(jnp.int32, (QK_NOPE_DIM, TB), 0)
    ones_row = jnp.where(row == 0, 1.0, 0.0)
    for hd in range(N_HEADS):
        r0 = hd * HEAD_PAD
        kh = _head_norm_rope(kvt[r0:r0 + QK_NOPE_DIM, :], k_rope, gk_ref, tabt_ref)
        k_ref[:, r0:r0 + HEAD_PAD] = kh.T.astype(BF16)
        vt_ref[r0:r0 + HEAD_PAD, :] = jnp.concatenate([ones_row, kvt[r0 + QK_NOPE_DIM:r0 + HEAD_PAD, :]],
                                                      axis=0).astype(BF16)


def _kv_expand(ckv, krp, p, layer):
    j = layer // 2
    n_sb = DEC_BATCH * KV_LEN // TB
    n_cache = PAST_LEN // TB
    n_pb = N_PROMPT // TB
    lat_blocks = DEC_SEQ // TB

    def tab_idx(i):
        from_cache, _, jj = _kv_block_source(i)
        return (0, 0, jnp.where((i >= n_sb) | from_cache, ROPE_BLOCKS, jj - n_cache))

    def cache_idx(i):
        _, b, jj = _kv_block_source(i)
        return (b, j, jnp.minimum(jj, n_cache - 1), 0)

    def tok_idx(i):
        _, b, jj = _kv_block_source(i)
        latent = n_pb + b * lat_blocks + jnp.maximum(jj - n_cache, 0)
        return (jnp.where(i < n_sb, latent, i - n_sb), 0)

    return pl.pallas_call(
        _kv_expand_kernel,
        grid=(N_KV_ROWS // TB,),
        in_specs=[
            pl.BlockSpec((None, None, TB, KV_RANK), cache_idx),
            pl.BlockSpec((None, None, TB, HEAD_PAD), cache_idx),
            pl.BlockSpec((TB, KV_RANK), tok_idx),
            pl.BlockSpec((TB, HEAD_PAD), tok_idx),
            _layer_spec((QKV_W, KV_RANK), j),
            _layer_spec((HEAD_PAD, 1), j),
            pl.BlockSpec((3, HEAD_PAD, TB), tab_idx),
        ],
        out_specs=[
            pl.BlockSpec((TB, QKV_W), lambda i: (i, 0)),
            pl.BlockSpec((QKV_W, TB), lambda i: (0, i)),
        ],
        out_shape=[
            jax.ShapeDtypeStruct((N_KV_ROWS, QKV_W), BF16),
            jax.ShapeDtypeStruct((QKV_W, N_KV_ROWS), BF16),
        ],
        compiler_params=_cparams(("parallel",)),
        name="mla_kv_expand",
    )(p["cache_ckv"], p["cache_krp"], ckv, krp, p["m_w_ukv_t"], p["m_gk"], p["rope_t"])


def _attn_kernel(qt_ref, k_ref, vt_ref, o_ref, s_a, s_b, *, t_k, hps, qps):
    c = (1.0 / math.sqrt(QK_HEAD_DIM)) * math.log2(math.e)
    kc = min(TKC, t_k)
    n_chunks = t_k // kc
    per_it = min(ATTN_CHUNKS_PER_ITER, n_chunks)
    n_it = n_chunks // per_it
    bufs = (s_a, s_b)

    units = [(qb, h) for qb in range(qps) for h in range(hps)]

    def rows(h):
        return slice(h * HEAD_PAD, (h + 1) * HEAD_PAD)

    def cols(qb):
        return slice(qb * TQ, (qb + 1) * TQ)

    def scores(u, off, m8):
        qb, h = units[u]
        st = _dot(k_ref[pl.ds(off, kc), rows(h)], qt_ref[rows(h), cols(qb)])
        bufs[u % 2][pl.ds(off, kc), :] = st
        return jnp.maximum(m8, jnp.max(st.reshape(kc // SUBLANE, SUBLANE, TQ), axis=0))

    def weigh(u, off, m, acc):
        pt = jnp.exp2((bufs[u % 2][pl.ds(off, kc), :] - m) * c).astype(BF16)
        return acc + _dot(vt_ref[rows(units[u][1]), pl.ds(off, kc)], pt)

    def phase(u_scores, u_weigh, m):
        def body(it, carry):
            m8, acc = carry
            for j in range(per_it):
                off = (it * per_it + j) * kc
                off = off if isinstance(off, int) else pl.multiple_of(off, kc)
                if u_scores is not None:
                    m8 = scores(u_scores, off, m8)
                if u_weigh is not None:
                    acc = weigh(u_weigh, off, m, acc)
            return m8, acc

        init = (jnp.full((SUBLANE, TQ), NEG_BIG, F32), jnp.zeros((HEAD_PAD, TQ), F32))
        return body(0, init) if n_it == 1 else lax.fori_loop(0, n_it, body, init)

    outs = []
    if n_chunks == 1:
        sts = [_dot(k_ref[:, rows(h)], qt_ref[rows(h), cols(qb)]) for qb, h in units]
        for (qb, h), st in zip(units, sts):
            m = jnp.max(st, axis=0, keepdims=True)
            acc = _dot(vt_ref[rows(h), :], jnp.exp2((st - m) * c).astype(BF16))
            outs.append(acc[QK_NOPE_DIM:, :] / acc[0:1, :])
    else:
        m8, _ = phase(0, None, None)
        for u in range(1, len(units) + 1):
            m = jnp.max(m8, axis=0, keepdims=True)
            m8, acc = phase(u if u < len(units) else None, u - 1, m)
            outs.append(acc[QK_NOPE_DIM:, :] / acc[0:1, :])
    for qb in range(qps):
        o_ref[cols(qb), :] = jnp.concatenate(outs[qb * hps:(qb + 1) * hps], axis=0).T.astype(BF16)


def _attention(qt, k, vt, *, n_batch, t_q, t_k, q_row0, kv_row0, hps, qps):
    tq = TQ * qps
    nq = t_q // tq
    q0 = q_row0 // tq
    k0 = kv_row0 // t_k
    return pl.pallas_call(
        functools.partial(_attn_kernel, t_k=t_k, hps=hps, qps=qps),
        grid=(n_batch, N_HEADS // hps, nq),
        in_specs=[
            pl.BlockSpec((hps * HEAD_PAD, tq), lambda b, h, i: (h, q0 + b * nq + i)),
            pl.BlockSpec((t_k, hps * HEAD_PAD), lambda b, h, i: (k0 + b, h)),
            pl.BlockSpec((hps * HEAD_PAD, t_k), lambda b, h, i: (h, k0 + b)),
        ],
        out_specs=pl.BlockSpec((tq, hps * V_HEAD_DIM), lambda b, h, i: (b * nq + i, h)),
        out_shape=jax.ShapeDtypeStruct((n_batch * t_q, N_HEADS * V_HEAD_DIM), BF16),
        scratch_shapes=[pltpu.VMEM((t_k, TQ), F32), pltpu.VMEM((t_k, TQ), F32)],
        compiler_params=_cparams(("parallel", "parallel", "parallel")),
        name=f"mla_attention_tk{t_k}",
    )(qt, k, vt)


def _out_proj_kernel(ap_ref, as_ref, xp_ref, xs_ref, mod_ref, wo_ref, o_ref):
    is_prompt = pl.program_id(0) < N_PROMPT // WIDE_TB
    a = jnp.where(is_prompt, ap_ref[...], as_ref[...])
    x = jnp.where(is_prompt, xp_ref[...], xs_ref[...])
    o_ref[...] = x + mod_ref[0][2:3] * _dot(a, wo_ref[...])


def _out_proj(attn_p, attn_s, x_p, x_s, p, layer):
    return pl.pallas_call(
        _out_proj_kernel,
        grid=(N_TOK // WIDE_TB,),
        in_specs=_split_rows_specs(WIDE_TB) + _split_rows_specs(WIDE_TB) + [
            _mod_spec(WIDE_TB, layer),
            _layer_spec((D_MODEL, D_MODEL), layer // 2),
        ],
        out_specs=pl.BlockSpec((WIDE_TB, D_MODEL), lambda i: (i, 0)),
        out_shape=jax.ShapeDtypeStruct((N_TOK, D_MODEL), F32),
        compiler_params=_cparams(("parallel",)),
        name="mla_out_proj",
    )(attn_p, attn_s, x_p, x_s, p["mod"], p["m_w_o"])


def _mla_layer(x_p, x_s, p, layer):
    qt, ckv, krp = _mla_proj(x_p, x_s, p, layer)
    k, vt = _kv_expand(ckv, krp, p, layer)
    a_p = _attention(qt, k, vt, n_batch=BATCH, t_q=SEQ, t_k=SEQ, q_row0=0, kv_row0=DEC_BATCH * KV_LEN,
                     hps=N_HEADS, qps=1)
    a_s = _attention(qt, k, vt, n_batch=DEC_BATCH, t_q=DEC_SEQ, t_k=KV_LEN, q_row0=N_PROMPT, kv_row0=0,
                     hps=4, qps=4)
    x = _out_proj(a_p, a_s, x_p, x_s, p, layer)
    state_ckv = ckv[:N_PROMPT].reshape(BATCH, SEQ, KV_RANK)
    state_krope = krp[:N_PROMPT, QK_NOPE_DIM:QK_HEAD_DIM].reshape(BATCH, SEQ, QK_ROPE_DIM)
    return x, state_ckv, state_krope


def _tile_plan(cnt8, plan_ref, start_ref):
    tm = float(TM)
    cnt_row = cnt8[0:1, :]
    cnt_col = cnt8.T[:, 0:1]
    tiles_row = jnp.floor((cnt_row + (tm - 1.0)) * (1.0 / tm))
    tiles_col = jnp.floor((cnt_col + (tm - 1.0)) * (1.0 / tm))
    sub = lax.broadcasted_iota(jnp.int32, (LANE, LANE), 0).astype(F32)
    lan = lax.broadcasted_iota(jnp.int32, (LANE, LANE), 1).astype(F32)
    tile_end_row = jnp.sum(jnp.where(sub <= lan, tiles_col, 0.0), axis=0, keepdims=True)
    tile_end_col = jnp.sum(jnp.where(lan <= sub, tiles_row, 0.0), axis=1, keepdims=True)
    n_used = jnp.max(tile_end_row, axis=1, keepdims=True)
    start_col = (tile_end_col - tiles_col) * tm
    end_col = start_col + cnt_col
    cand = jnp.where(jnp.logical_and(lan > sub, tiles_row > 0.0), lan, float(LANE))
    next_col = jnp.min(cand, axis=1, keepdims=True)
    next_col = jnp.where(next_col < float(LANE), next_col, -1.0)
    n_lanes = plan_ref.shape[1]
    tidx = jnp.minimum(lax.broadcasted_iota(jnp.int32, (LANE, n_lanes), 1).astype(F32), n_used - 1.0)
    esub = lax.broadcasted_iota(jnp.int32, (LANE, n_lanes), 0).astype(F32)
    te_row = jnp.sum(jnp.where(tile_end_col <= tidx, 1.0, 0.0), axis=0, keepdims=True)
    mine = esub == te_row
    end_at = jnp.sum(jnp.where(mine, end_col, 0.0), axis=0, keepdims=True)
    tv_row = jnp.clip(end_at - tidx[0:1, :] * tm, 0.0, tm)
    nx_row = jnp.sum(jnp.where(mine, next_col, 0.0), axis=0, keepdims=True)
    nu_row = jnp.broadcast_to(n_used, (1, n_lanes))
    plan_ref[...] = jnp.concatenate([te_row, tv_row, nx_row, nu_row, jnp.zeros((SUBLANE - 4, n_lanes), F32)], axis=0)
    start_ref[...] = jnp.broadcast_to(start_col, (LANE, LANE))


def _route_kernel(x_ref, mod_ref, g_ref, wrt_ref, brt_ref, triu_ref, h_ref, metat_ref, plan_ref, start_ref, carry):
    i = pl.program_id(0)

    @pl.when(i == 0)
    def _():
        carry[...] = jnp.zeros_like(carry)

    x = x_ref[...]
    m = mod_ref[0]
    h = _rms(x, g_ref[...]) * (1.0 + m[4:5]) + m[3:4]
    h_ref[...] = _pack_bf16_pairs(h)
    logits = _dot_nt(wrt_ref[...], h.astype(BF16)) + brt_ref[...]
    n_tok = logits.shape[1]
    esub = lax.broadcasted_iota(jnp.int32, logits.shape, 0).astype(F32)
    work = logits
    sel = jnp.zeros(logits.shape, F32)
    hits, firsts, tops = [], [], []
    for k in range(TOP_K):
        mk = jnp.max(work, axis=0, keepdims=True)
        first = jnp.min(jnp.where(work == mk, esub, float(N_EXPERTS)), axis=0, keepdims=True)
        hit = esub == first
        sel = jnp.where(hit, 1.0, sel)
        work = jnp.where(hit, -jnp.inf, work)
        hits.append(hit)
        firsts.append(first)
        tops.append(mk)
    es = [jnp.exp(t - tops[0]) for t in tops]
    denom = es[0] + es[1] + es[2] + es[3]
    pos = _dot(sel.astype(BF16), triu_ref[...]) + carry[:, 0:1]
    carry[...] = carry[...] + jnp.sum(sel, axis=1, keepdims=True)

    @pl.when(i == pl.num_programs(0) - 1)
    def _():
        counts = jnp.concatenate([carry[...], jnp.zeros((LANE - N_EXPERTS, LANE), F32)], axis=0).T
        _tile_plan(counts[:SUBLANE, :], plan_ref, start_ref)

    ranks = [jnp.sum(jnp.where(hit, pos, 0.0), axis=0, keepdims=True) for hit in hits]
    gates = [e / denom for e in es]
    metat_ref[...] = jnp.concatenate(firsts + gates + ranks + [jnp.zeros((SUBLANE // 2, n_tok), F32)], axis=0)


def _route(x, p, layer):
    tb = ROUTE_TB
    return pl.pallas_call(
        _route_kernel,
        grid=(N_TOK // tb,),
        in_specs=[
            pl.BlockSpec((tb, D_MODEL), lambda i: (i, 0)),
            _mod_spec(tb, layer),
            _layer_spec((1, D_MODEL), layer),
            _layer_spec((N_EXPERTS, D_MODEL), layer),
            _layer_spec((N_EXPERTS, 1), layer),
            _const_spec((tb, tb)),
        ],
        out_specs=[
            pl.BlockSpec((tb, D_MODEL // 2), lambda i: (i, 0)),
            pl.BlockSpec((2 * SUBLANE, tb), lambda i: (0, i)),
            _const_spec((SUBLANE, PLAN_LANES)),
            _const_spec((LANE, LANE)),
        ],
        out_shape=[
            jax.ShapeDtypeStruct((N_TOK, D_MODEL // 2), jnp.uint32),
            jax.ShapeDtypeStruct((2 * SUBLANE, N_TOK), F32),
            jax.ShapeDtypeStruct((SUBLANE, PLAN_LANES), F32),
            jax.ShapeDtypeStruct((LANE, LANE), F32),
        ],
        scratch_shapes=[pltpu.VMEM((N_EXPERTS, LANE), F32)],
        compiler_params=_cparams(("arbitrary",)),
        name="moe_route",
    )(x, p["mod"], p["norm_ffn_g"], p["e_w_router_t"], p["e_b_router"], p["triu"])


def _slots_kernel(start_ref, metat_ref, dest_ref):
    start_col = start_ref[:, 0:1]
    esub = lax.broadcasted_iota(jnp.int32, (LANE, SLOT_TB), 0).astype(F32)
    rows = []
    for k in range(TOP_K):
        e = metat_ref[k:k + 1, :]
        first = jnp.sum(jnp.where(esub == e, start_col, 0.0), axis=0, keepdims=True)
        rows.append(first + metat_ref[2 * TOP_K + k:2 * TOP_K + k + 1, :])
    dest_ref[...] = jnp.concatenate(rows, axis=0).astype(jnp.int32)


def _slots(start, meta_t):
    return pl.pallas_call(
        _slots_kernel,
        grid=(N_TOK // SLOT_TB,),
        in_specs=[
            _const_spec((LANE, LANE)),
            pl.BlockSpec((2 * SUBLANE, SLOT_TB), lambda i: (0, i)),
        ],
        out_specs=pl.BlockSpec((TOP_K, SLOT_TB), lambda i: (0, i)),
        out_shape=jax.ShapeDtypeStruct((TOP_K, N_TOK), jnp.int32),
        compiler_params=_cparams(("parallel",)),
        name="moe_slots",
    )(start, meta_t)


def _sc_gather(table, idx, ch):
    b, w = idx.shape[0], table.shape[1]
    per_w = b // SC_WORKERS
    n_ch = per_w // ch
    assert per_w * SC_WORKERS == b and n_ch * ch == per_w and n_ch % 2 == 0
    mesh = plsc.VectorSubcoreMesh(core_axis_name="c", subcore_axis_name="s")

    @functools.partial(
        pl.kernel, mesh=mesh,
        out_type=jax.ShapeDtypeStruct((b, w), table.dtype),
        scratch_types=[
            pltpu.VMEM((n_ch, ch), jnp.int32),
            pltpu.VMEM((ch, w), table.dtype),
            pltpu.VMEM((ch, w), table.dtype),
            pltpu.SemaphoreType.DMA, pltpu.SemaphoreType.DMA,
            pltpu.SemaphoreType.DMA, pltpu.SemaphoreType.DMA,
        ],
        name="sc_row_gather",
    )
    def gather_rows(table_hbm, idx_hbm, out_hbm, idx_v, buf0, buf1, g0, g1, s0, s1):
        wid = lax.axis_index("s") * SC_CORES + lax.axis_index("c")
        base = wid * per_w
        pltpu.sync_copy(idx_hbm.at[wid], idx_v)

        def gather(j, buf, sem):
            return pltpu.make_async_copy(table_hbm.at[idx_v.at[j]], buf, sem)

        def store(j, buf, sem):
            return pltpu.make_async_copy(buf, out_hbm.at[pl.ds(base + j * ch, ch)], sem)

        gather(0, buf0, g0).start()

        @pl.loop(0, n_ch, step=2)
        def _(j):
            @pl.when(j > 0)
            def _():
                store(j - 1, buf1, s1).wait()

            gather(j + 1, buf1, g1).start()
            gather(j, buf0, g0).wait()
            store(j, buf0, s0).start()
            gather(j + 1, buf1, g1).wait()
            store(j + 1, buf1, s1).start()
            store(j, buf0, s0).wait()

            @pl.when(j + 2 < n_ch)
            def _():
                gather(j + 2, buf0, g0).start()

        store(n_ch - 1, buf1, s1).wait()

    return gather_rows(table, idx.reshape(SC_WORKERS, n_ch, ch))


def _sc_dispatch(rows, dest_t, ch):
    n, w = rows.shape
    per_w = n // SC_WORKERS
    n_ch = per_w // ch
    assert per_w * SC_WORKERS == n and n_ch * ch == per_w and n_ch % 2 == 0
    mesh = plsc.VectorSubcoreMesh(core_axis_name="c", subcore_axis_name="s")
    idx = dest_t.reshape(TOP_K, SC_WORKERS, n_ch, ch)

    @functools.partial(
        pl.kernel, mesh=mesh,
        out_type=jax.ShapeDtypeStruct((N_SLOTS, w), rows.dtype),
        scratch_types=[
            pltpu.VMEM((TOP_K * n_ch, ch), jnp.int32),
            pltpu.VMEM((ch, w), rows.dtype),
            pltpu.VMEM((ch, w), rows.dtype),
            pltpu.SemaphoreType.DMA, pltpu.SemaphoreType.DMA,
            pltpu.SemaphoreType.DMA, pltpu.SemaphoreType.DMA,
        ],
        name="sc_row_dispatch",
    )
    def dispatch_rows(rows_hbm, idx_hbm, out_hbm, idx_v, buf0, buf1, l0, l1, s0, s1):
        wid = lax.axis_index("s") * SC_CORES + lax.axis_index("c")
        base = wid * per_w
        for k in range(TOP_K):
            pltpu.sync_copy(idx_hbm.at[k, wid], idx_v.at[pl.ds(k * n_ch, n_ch)])

        def load(j, buf, sem):
            return pltpu.make_async_copy(rows_hbm.at[pl.ds(base + j * ch, ch)], buf, sem)

        def scatter(j, k, buf, sem):
            return pltpu.make_async_copy(buf, out_hbm.at[idx_v.at[k * n_ch + j]], sem)

        load(0, buf0, l0).start()

        @pl.loop(0, n_ch, step=2)
        def _(j):
            load(j + 1, buf1, l1).start()
            load(j, buf0, l0).wait()
            for k in range(TOP_K):
                scatter(j, k, buf0, s0).start()
            load(j + 1, buf1, l1).wait()
            for k in range(TOP_K):
                scatter(j + 1, k, buf1, s1).start()
            for k in range(TOP_K):
                scatter(j, k, buf0, s0).wait()

            @pl.when(j + 2 < n_ch)
            def _():
                load(j + 2, buf0, l0).start()

            for k in range(TOP_K):
                scatter(j + 1, k, buf1, s1).wait()

    return dispatch_rows(rows, idx)


def _deinterleave_matrix():
    src = jnp.arange(2 * LANE)[:, None]
    dst = jnp.arange(2 * LANE)[None, :]
    want = jnp.where(dst < LANE, 2 * dst, 2 * (dst - LANE) + 1)
    return (src == want).astype(BF16)


def _expert_kernel(te_ref, nu_ref, tv_ref, nx_ref, x_ref, wgu_hbm, bgu_ref, wd_hbm, bd_ref, perm_ref, o_ref,
                   wgu_st, wd_st, wgu_bf, wd_bf, sems, *, layer):
    i = pl.program_id(0)
    prev = te_ref[jnp.maximum(i - 1, 0)]
    fresh = jnp.logical_or(i == 0, te_ref[i] != prev)

    def fetch(e):
        return (pltpu.make_async_copy(wgu_hbm.at[layer, e], wgu_st, sems.at[0]),
                pltpu.make_async_copy(wd_hbm.at[layer, e], wd_st, sems.at[1]))

    @pl.when(i == 0)
    def _():
        for cp in fetch(te_ref[0]):
            cp.start()

    @pl.when(jnp.logical_and(fresh, i < nu_ref[0]))
    def _():
        for cp in fetch(te_ref[i]):
            cp.wait()
        for b in range(2 * D_FF // (2 * LANE)):
            sl = slice(b * 2 * LANE, (b + 1) * 2 * LANE)
            wgu_bf[:, sl] = _dot(wgu_st[:, sl].astype(BF16), perm_ref[...]).astype(BF16)
        wd_bf[...] = wd_st[...].astype(BF16)

        @pl.when(nx_ref[i] >= 0)
        def _():
            for cp in fetch(nx_ref[i]):
                cp.start()

    def ffn(n_rows):
        row = lax.broadcasted_iota(jnp.int32, (n_rows, D_MODEL // 2), 0)
        w = jnp.where(row < tv_ref[i], x_ref[:n_rows, :], jnp.uint32(0))
        x = _unpack_bf16_pairs(w).astype(BF16)
        gu = _dot(x, wgu_bf[...]) + bgu_ref[...]
        acts = []
        for b in range(D_FF // LANE):
            glu = jnp.minimum(gu[:, b * 2 * LANE:b * 2 * LANE + LANE], SWIGLU_LIMIT)
            lin = jnp.clip(gu[:, b * 2 * LANE + LANE:(b + 1) * 2 * LANE], -SWIGLU_LIMIT, SWIGLU_LIMIT)
            acts.append((glu * jax.nn.sigmoid(SWIGLU_ALPHA * glu) * (lin + 1.0)).astype(BF16))
        act = jnp.concatenate(acts, axis=1)
        o_ref[:n_rows, :] = _pack_bf16_pairs(_dot(act, wd_bf[...]) + bd_ref[...])

    in_use = i < nu_ref[0]

    @pl.when(jnp.logical_and(in_use, tv_ref[i] > TM // 2))
    def _():
        ffn(TM)

    @pl.when(jnp.logical_and(in_use, tv_ref[i] <= TM // 2))
    def _():
        ffn(TM // 2)


def _experts(buf, tile_expert, n_used, tile_valid, tile_next, p, layer):
    def row_idx(i, te, nu, tv, nx):
        return (jnp.minimum(i, nu[0] - 1), 0)

    def b_idx(i, te, nu, tv, nx):
        return (layer, te[i], 0, 0)

    grid_spec = pltpu.PrefetchScalarGridSpec(
        num_scalar_prefetch=4,
        grid=(N_TILES,),
        in_specs=[
            pl.BlockSpec((TM, D_MODEL // 2), row_idx),
            pl.BlockSpec(memory_space=pl.ANY),
            pl.BlockSpec((None, None, 1, 2 * D_FF), b_idx),
            pl.BlockSpec(memory_space=pl.ANY),
            pl.BlockSpec((None, None, 1, D_MODEL), b_idx),
            _const_spec((2 * LANE, 2 * LANE)),
        ],
        out_specs=pl.BlockSpec((TM, D_MODEL // 2), row_idx),
        scratch_shapes=[
            pltpu.VMEM((D_MODEL, 2 * D_FF), F32),
            pltpu.VMEM((D_FF, D_MODEL), F32),
            pltpu.VMEM((D_MODEL, 2 * D_FF), BF16),
            pltpu.VMEM((D_FF, D_MODEL), BF16),
            pltpu.SemaphoreType.DMA((2,)),
        ],
    )
    return pl.pallas_call(
        functools.partial(_expert_kernel, layer=layer),
        grid_spec=grid_spec,
        out_shape=jax.ShapeDtypeStruct((N_SLOTS, D_MODEL // 2), jnp.uint32),
        compiler_params=_cparams(("arbitrary",)),
        name="moe_experts",
    )(tile_expert, n_used, tile_valid, tile_next, buf, p["e_w_gu"], p["e_b_gu"], p["e_w_down"], p["e_b_down"],
      p["deinterleave"])


def _pack_bf16_pairs(v):
    half = v.shape[1] // 2
    bits = pltpu.bitcast(v.astype(BF16).astype(F32), jnp.uint32)
    return (bits[:, half:] & jnp.uint32(0xFFFF0000)) | (bits[:, :half] >> 16)


def _unpack_bf16_pairs(w):
    return jnp.concatenate([pltpu.bitcast(w << 16, F32), pltpu.bitcast(w & jnp.uint32(0xFFFF0000), F32)],
                           axis=1)


def _combine_kernel(x_ref, mod_ref, y_ref, mt_ref, o_ref):
    w = mt_ref[...].T[:, TOP_K:2 * TOP_K]
    y = _unpack_bf16_pairs(y_ref[0]) * w[:, 0:1]
    for k in range(1, TOP_K):
        y = y + _unpack_bf16_pairs(y_ref[k]) * w[:, k:k + 1]
    o_ref[...] = x_ref[...] + mod_ref[0][5:6] * y


def _combine(x, y4, meta_t, p, layer, part):
    n_rows = N_TOK // MOE_PARTS
    tb = WIDE_TB
    first = part * n_rows // tb
    return pl.pallas_call(
        _combine_kernel,
        grid=(n_rows // tb,),
        in_specs=[
            pl.BlockSpec((tb, D_MODEL), lambda i: (i + first, 0)),
            _mod_spec(tb, layer, first),
            pl.BlockSpec((TOP_K, tb, D_MODEL // 2), lambda i: (0, i, 0)),
            pl.BlockSpec((2 * SUBLANE, tb), lambda i: (0, i + first)),
        ],
        out_specs=pl.BlockSpec((tb, D_MODEL), lambda i: (i, 0)),
        out_shape=jax.ShapeDtypeStruct((n_rows, D_MODEL), F32),
        compiler_params=_cparams(("parallel",)),
        name="moe_combine",
    )(x, p["mod"], y4, meta_t)


def _moe_layer(x, p, layer):
    hp, meta_t, plan, start = _route(x, p, layer)
    plan = plan[:4, :N_TILES].astype(jnp.int32)
    tile_expert, tile_valid, tile_next, n_used = plan[0], plan[1], plan[2], plan[3, :1]
    dest_t = _slots(start, meta_t)
    buf = _sc_dispatch(hp, dest_t, 64)
    yb = _experts(buf, tile_expert, n_used, tile_valid, tile_next, p, layer)
    n_rows = N_TOK // MOE_PARTS
    outs = []
    for part in range(MOE_PARTS):
        idx = dest_t[:, part * n_rows:(part + 1) * n_rows].reshape(-1)
        y4 = _sc_gather(yb, idx, 64).reshape(TOP_K, n_rows, D_MODEL // 2)
        outs.append(_combine(x, y4, meta_t, p, layer, part))
    return outs


def _prepare(c, cache_ckv, cache_krope, c_ctx, norm_mix_g, norm_ffn_g, w_mod, b_mod, g_w_in, g_b_in, g_norm_v,
             g_w_s, g_b_s, g_w_out, m_w_dq, m_norm_q, m_w_uq, m_w_dkv, m_norm_kv, m_w_ukv, m_qk_norm_q,
             m_qk_norm_k, m_w_o, e_w_router, e_b_router, e_w_gu, e_b_gu, e_w_down, e_b_down):
    n_mla = m_w_dq.shape[0]
    cond = jnp.concatenate([c_ctx[None, :], c, jnp.zeros((SUBLANE - N_COND, D_MODEL), F32)], axis=0)
    wdkv = jnp.concatenate([m_w_dkv[..., :KV_RANK], jnp.zeros((n_mla, D_MODEL, QK_NOPE_DIM), F32),
                            m_w_dkv[..., KV_RANK:], jnp.zeros((n_mla, D_MODEL, HEAD_PAD - QK_HEAD_DIM), F32)],
                           axis=-1)
    w_uq = jnp.pad(m_w_uq.reshape(n_mla, Q_RANK, N_HEADS, QK_HEAD_DIM),
                   ((0, 0), (0, 0), (0, 0), (0, HEAD_PAD - QK_HEAD_DIM))).reshape(n_mla, Q_RANK, QKV_W)

    def gain_col(g):
        return jnp.pad(g, ((0, 0), (0, HEAD_PAD - QK_HEAD_DIM)))[:, :, None]

    return {
        "mod": _modulation(cond, w_mod, b_mod),
        "rope_t": _rope_tables(),
        "norm_mix_g": norm_mix_g[:, None, :],
        "norm_ffn_g": norm_ffn_g[:, None, :],
        "g_w_in": g_w_in.astype(BF16),
        "g_b_in": g_b_in[:, None, :],
        "g_norm_v": g_norm_v[:, None, :],
        "g_w_s": g_w_s.astype(BF16),
        "g_b_st": jnp.swapaxes(g_b_s, 1, 2),
        "g_w_out": g_w_out.astype(BF16),
        "m_w_dq": m_w_dq.astype(BF16),
        "m_norm_q": m_norm_q[:, None, :],
        "m_w_uq_t": jnp.swapaxes(w_uq, 1, 2).astype(BF16),
        "m_gq": gain_col(m_qk_norm_q),
        "m_w_dkv": wdkv.astype(BF16),
        "m_norm_kv": m_norm_kv[:, None, :],
        "m_w_ukv_t": jnp.swapaxes(m_w_ukv, 1, 2).astype(BF16),
        "m_gk": gain_col(m_qk_norm_k),
        "m_w_o": m_w_o.astype(BF16),
        "cache_ckv": cache_ckv,
        "cache_krp": jnp.pad(cache_krope, ((0, 0), (0, 0), (0, 0), (QK_NOPE_DIM, HEAD_PAD - QK_HEAD_DIM))),
        "e_w_router_t": jnp.swapaxes(e_w_router, 1, 2).astype(BF16),
        "e_b_router": e_b_router[:, :, None],
        "triu": jnp.tri(ROUTE_TB, ROUTE_TB, -1, dtype=BF16).T,
        "e_w_gu": e_w_gu,
        "e_b_gu": e_b_gu.reshape(DEPTH, N_EXPERTS, D_FF // LANE, LANE, 2).swapaxes(3, 4).reshape(
            DEPTH, N_EXPERTS, 1, 2 * D_FF),
        "e_w_down": e_w_down,
        "e_b_down": e_b_down[:, :, None, :],
        "deinterleave": _deinterleave_matrix(),
    }


def kernel(x_prompt, x_sample, c, cache_ckv, cache_krope, c_ctx, norm_mix_g, norm_ffn_g, w_mod, b_mod,
           g_w_in, g_b_in, g_norm_v, g_w_s, g_b_s, g_w_out, m_w_dq, m_norm_q, m_w_uq, m_w_dkv,
           m_norm_kv, m_w_ukv, m_qk_norm_q, m_qk_norm_k, m_w_o, e_w_router, e_b_router, e_w_gu,
           e_b_gu, e_w_down, e_b_down):
    p = _prepare(c, cache_ckv, cache_krope, c_ctx, norm_mix_g, norm_ffn_g, w_mod, b_mod, g_w_in, g_b_in,
                 g_norm_v, g_w_s, g_b_s, g_w_out, m_w_dq, m_norm_q, m_w_uq, m_w_dkv, m_norm_kv, m_w_ukv,
                 m_qk_norm_q, m_qk_norm_k, m_w_o, e_w_router, e_b_router, e_w_gu, e_b_gu, e_w_down, e_b_down)
    assert MOE_PARTS == 2 and N_PROMPT == N_SAMPLE
    x_p, x_s = x_prompt.reshape(N_PROMPT, D_MODEL), x_sample.reshape(N_SAMPLE, D_MODEL)
    ckv_states, krope_states = [], []
    for layer in range(DEPTH):
        if layer % 2 == 0:
            x = _gmlp_layer(x_p, x_s, p, layer)
        else:
            x, s_ckv, s_krope = _mla_layer(x_p, x_s, p, layer)
            ckv_states.append(s_ckv)
            krope_states.append(s_krope)
        x_p, x_s = _moe_layer(x, p, layer)
    y_prompt = x_p.reshape(BATCH, SEQ, D_MODEL)
    y_sample = x_s.reshape(DEC_BATCH, DEC_SEQ, D_MODEL)
    return (y_prompt, y_sample, jnp.stack(ckv_states, axis=1), jnp.stack(krope_states, axis=1))
```

```python
import functools
import math

import jax
import jax.numpy as jnp
from jax import lax
from jax.experimental import pallas as pl
from jax.experimental.pallas import tpu as pltpu
from jax.experimental.pallas import tpu_sc as plsc

F32 = jnp.float32
BF16 = jnp.bfloat16

D_MODEL = 1024
BATCH = 32
SEQ = 256
DEPTH = 4
DEC_BATCH = 2
DEC_SEQ = 4096
PAST_LEN = 512
GRID_W = 64
RMS_EPS = 1e-6
GMLP_WIDTH = 2 * D_MODEL
GMLP_GROUPS = 8
GROUP_W = GMLP_WIDTH // GMLP_GROUPS
CHUNK = 128
N_HEADS = 16
QK_NOPE_DIM = 64
QK_ROPE_DIM = 32
QK_HEAD_DIM = QK_NOPE_DIM + QK_ROPE_DIM
V_HEAD_DIM = 64
Q_RANK = 256
KV_RANK = 128
ROPE_THETA = 10000.0
N_EXPERTS = 32
TOP_K = 4
D_FF = D_MODEL
SWIGLU_LIMIT = 7.0
SWIGLU_ALPHA = 1.702

N_PROMPT = BATCH * SEQ
N_SAMPLE = DEC_BATCH * DEC_SEQ
N_TOK = N_PROMPT + N_SAMPLE
N_COND = 1 + DEC_BATCH
KV_LEN = PAST_LEN + DEC_SEQ
N_KV_ROWS = DEC_BATCH * KV_LEN + N_PROMPT

LANE = 128
SUBLANE = 8
HEAD_PAD = LANE
QKV_W = N_HEADS * HEAD_PAD
VMEM_LIMIT = 56 * 1024 * 1024

TB = 512
MOD_TN = 1536
TQ = 256
TKC = 256
ATTN_CHUNKS_PER_ITER = 9
TM = 512
N_TILES = N_TOK * TOP_K // TM + N_EXPERTS
N_SLOTS = N_TILES * TM
PLAN_LANES = -(-N_TILES // LANE) * LANE
SLOT_TB = 2048
ROUTE_TB = 512
MOE_PARTS = 2
SC_CORES = 2
SC_SUBCORES = 16
SC_WORKERS = SC_CORES * SC_SUBCORES
SC_CHUNK_ROWS = 64
ROPE_BLOCKS = DEC_SEQ // TB
NEG_BIG = -1e30


def _cparams(sem):
    return pltpu.CompilerParams(dimension_semantics=sem, vmem_limit_bytes=VMEM_LIMIT)


def _cond_of_block(i, tb):
    n_p = N_PROMPT // tb
    per = DEC_SEQ // tb
    return jnp.where(i < n_p, 0, 1 + (i - n_p) // per)


def _rms(x, g, n=None):
    n = x.shape[-1] if n is None else n
    ss = jnp.sum(x * x, axis=-1, keepdims=True) * (1.0 / n)
    return x * lax.rsqrt(ss + RMS_EPS) * g


def _dot(a, b):
    return jnp.dot(a, b, preferred_element_type=F32)


def _mod_kernel(c_ref, w_ref, b_ref, o_ref):
    c = c_ref[...]
    s = c * jax.nn.sigmoid(c)
    o_ref[0] = _dot(s.astype(BF16), w_ref[0].astype(BF16)) + b_ref[0]


def _modulation(cond, w_mod, b_mod):
    tn = MOD_TN
    out = pl.pallas_call(
        _mod_kernel,
        grid=(DEPTH, 6 * D_MODEL // tn),
        in_specs=[
            pl.BlockSpec((SUBLANE, D_MODEL), lambda l, j: (0, 0)),
            pl.BlockSpec((1, D_MODEL, tn), lambda l, j: (l, 0, j)),
            pl.BlockSpec((1, 1, tn), lambda l, j: (l, 0, j)),
        ],
        out_specs=pl.BlockSpec((1, SUBLANE, tn), lambda l, j: (l, 0, j)),
        out_shape=jax.ShapeDtypeStruct((DEPTH, SUBLANE, 6 * D_MODEL), F32),
        compiler_params=_cparams(("parallel", "parallel")),
        name="adaln_mod",
    )(cond, w_mod, b_mod.reshape(DEPTH, 1, 6 * D_MODEL))
    m = out[:, :N_COND].reshape(DEPTH, N_COND, 6, D_MODEL)
    return jnp.pad(m, ((0, 0), (0, 0), (0, SUBLANE - 6), (0, 0)))


def _mod_spec(tb, layer, first_block=0):
    return pl.BlockSpec((None, 1, SUBLANE, D_MODEL),
                        lambda i: (layer, _cond_of_block(i + first_block, tb), 0, 0))


def _const_spec(shape):
    nd = len(shape)
    return pl.BlockSpec(shape, lambda *_: (0,) * nd)


def _layer_spec(shape, j):
    nd = len(shape)
    return pl.BlockSpec((None,) + tuple(shape), lambda *_: (j,) + (0,) * nd)


def _gelu_tanh(x):
    a = math.sqrt(2.0 / math.pi)
    hx = 0.5 * x
    return hx + hx * jnp.tanh(x * (a + (0.044715 * a) * (x * x)))


def _gmlp_kernel(xp_ref, xs_ref, mod_ref, g_ref, win_ref, bin_ref, gv_ref, ws_ref, bst_ref, wout_ref, o_ref):
    x = jnp.where(pl.program_id(0) < N_PROMPT // TB, xp_ref[...], xs_ref[...])
    m = mod_ref[0]
    h = _rms(x, g_ref[...]) * (1.0 + m[1:2]) + m[0:1]
    hb = h.astype(BF16)
    zv = _gelu_tanh(_dot(hb, win_ref[:, GMLP_WIDTH:]) + bin_ref[:, GMLP_WIDTH:])
    vn = _rms(zv, gv_ref[...]).astype(BF16)
    rows = []
    for c in range(TB // CHUNK):
        cols = []
        for g in range(GMLP_GROUPS):
            blk = vn[c * CHUNK:(c + 1) * CHUNK, g * GROUP_W:(g + 1) * GROUP_W]
            cols.append(_dot(ws_ref[g], blk) + bst_ref[:, g:g + 1])
        rows.append(jnp.concatenate(cols, axis=1))
    vm = jnp.concatenate(rows, axis=0)
    u = _gelu_tanh(_dot(hb, win_ref[:, :GMLP_WIDTH]) + bin_ref[:, :GMLP_WIDTH])
    d = _dot((u * vm).astype(BF16), wout_ref[...])
    o_ref[...] = x + m[2:3] * d


def _split_rows_specs():
    n_pb = N_PROMPT // TB
    return [pl.BlockSpec((TB, D_MODEL), lambda i: (jnp.minimum(i, n_pb - 1), 0)),
            pl.BlockSpec((TB, D_MODEL), lambda i: (jnp.maximum(i - n_pb, 0), 0))]


def _gmlp_layer(x_p, x_s, p, layer):
    j = layer // 2
    return pl.pallas_call(
        _gmlp_kernel,
        grid=(N_TOK // TB,),
        in_specs=_split_rows_specs() + [
            _mod_spec(TB, layer),
            _layer_spec((1, D_MODEL), layer),
            _layer_spec((D_MODEL, 2 * GMLP_WIDTH), j),
            _layer_spec((1, 2 * GMLP_WIDTH), j),
            _layer_spec((1, GMLP_WIDTH), j),
            _layer_spec((GMLP_GROUPS, CHUNK, CHUNK), j),
            _layer_spec((CHUNK, GMLP_GROUPS), j),
            _layer_spec((GMLP_WIDTH, D_MODEL), j),
        ],
        out_specs=pl.BlockSpec((TB, D_MODEL), lambda i: (i, 0)),
        out_shape=jax.ShapeDtypeStruct((N_TOK, D_MODEL), F32),
        compiler_params=_cparams(("parallel",)),
        name="gmlp_mixer",
    )(x_p, x_s, p["mod"], p["norm_mix_g"], p["g_w_in"], p["g_b_in"], p["g_norm_v"], p["g_w_s"], p["g_b_st"],
      p["g_w_out"])


def _rope_tables():
    t = jnp.arange(DEC_SEQ)
    row_id = (t // GRID_W).astype(F32)
    col_id = (t % GRID_W).astype(F32)
    axis_dim = QK_ROPE_DIM // 2
    inv_freq = ROPE_THETA ** (-jnp.arange(0, axis_dim, 2, dtype=F32) / axis_dim)
    ang = jnp.stack([row_id[:, None] * inv_freq, col_id[:, None] * inv_freq], axis=1)
    cos, sin = jnp.cos(ang), jnp.sin(ang)
    zeros = jnp.zeros_like(sin)
    cos_l = jnp.concatenate([cos, cos], axis=-1).reshape(DEC_SEQ, QK_ROPE_DIM)
    s1_l = jnp.concatenate([-sin, zeros], axis=-1).reshape(DEC_SEQ, QK_ROPE_DIM)
    s2_l = jnp.concatenate([zeros, sin], axis=-1).reshape(DEC_SEQ, QK_ROPE_DIM)

    def widen(rope_part, nope_fill):
        left = jnp.full((DEC_SEQ, QK_NOPE_DIM), nope_fill, F32)
        right = jnp.zeros((DEC_SEQ, HEAD_PAD - QK_HEAD_DIM), F32)
        return jnp.concatenate([left, rope_part, right], axis=-1)

    pos = jnp.stack([widen(cos_l, 1.0), widen(s1_l, 0.0), widen(s2_l, 0.0)])
    ident_c = jnp.concatenate([jnp.ones((TB, QK_HEAD_DIM), F32),
                               jnp.zeros((TB, HEAD_PAD - QK_HEAD_DIM), F32)], axis=-1)
    ident = jnp.stack([ident_c, jnp.zeros_like(ident_c), jnp.zeros_like(ident_c)])
    return jnp.swapaxes(jnp.concatenate([pos, ident], axis=1), 1, 2)


def _dot_nt(a, b):
    return lax.dot_general(a, b, (((1,), (1,)), ((), ())), preferred_element_type=F32)


def _shift_rows(x, n):
    n = n % x.shape[0]
    return jnp.concatenate([x[n:], x[:n]], axis=0)


def _head_norm_rope(nope, rope, g_ref, tabt_ref):
    half = QK_ROPE_DIM // 4
    lo, hi = QK_NOPE_DIM, QK_HEAD_DIM
    ss = (jnp.sum(nope * nope, axis=0, keepdims=True) + jnp.sum(rope * rope, axis=0, keepdims=True))
    rs = lax.rsqrt(ss * (1.0 / QK_HEAD_DIM) + RMS_EPS)
    xn = nope * rs * g_ref[:lo, :]
    xr = rope * rs * g_ref[lo:hi, :]
    rot = (xr * tabt_ref[0, lo:hi, :] + _shift_rows(xr, half) * tabt_ref[1, lo:hi, :]
           + _shift_rows(xr, -half) * tabt_ref[2, lo:hi, :])
    return jnp.concatenate([xn, rot, jnp.zeros((HEAD_PAD - hi, nope.shape[1]), F32)], axis=0)


def _mla_proj_kernel(xp_ref, xs_ref, mod_ref, g_ref, wdq_ref, nq_ref, wuqt_ref, gq_ref, wdkv_ref, nkv_ref,
                     tabt_ref, qt_ref, ckv_ref, krp_ref):
    x = jnp.where(pl.program_id(0) < N_PROMPT // TB, xp_ref[...], xs_ref[...])
    m = mod_ref[0]
    h = _rms(x, g_ref[...]) * (1.0 + m[1:2]) + m[0:1]
    hb = h.astype(BF16)
    cq = _rms(_dot(hb, wdq_ref[...]), nq_ref[...])
    qt = _dot_nt(wuqt_ref[...], cq.astype(BF16))
    for hd in range(N_HEADS):
        r0 = hd * HEAD_PAD
        qh = _head_norm_rope(qt[r0:r0 + QK_NOPE_DIM, :], qt[r0 + QK_NOPE_DIM:r0 + QK_HEAD_DIM, :], gq_ref, tabt_ref)
        qt_ref[r0:r0 + HEAD_PAD, :] = qh.astype(BF16)
    kva = _dot(hb, wdkv_ref[...])
    ckv_ref[...] = _rms(kva[:, :KV_RANK], nkv_ref[...])
    krp_ref[...] = kva[:, KV_RANK:]


def _mla_proj(x_p, x_s, p, layer):
    j = layer // 2
    n_pb = N_PROMPT // TB

    def tab_idx(i):
        return (0, 0, jnp.where(i < n_pb, ROPE_BLOCKS, (i - n_pb) % ROPE_BLOCKS))

    return pl.pallas_call(
        _mla_proj_kernel,
        grid=(N_TOK // TB,),
        in_specs=_split_rows_specs() + [
            _mod_spec(TB, layer),
            _layer_spec((1, D_MODEL), layer),
            _layer_spec((D_MODEL, Q_RANK), j),
            _layer_spec((1, Q_RANK), j),
            _layer_spec((QKV_W, Q_RANK), j),
            _layer_spec((HEAD_PAD, 1), j),
            _layer_spec((D_MODEL, 2 * LANE), j),
            _layer_spec((1, KV_RANK), j),
            pl.BlockSpec((3, HEAD_PAD, TB), tab_idx),
        ],
        out_specs=[
            pl.BlockSpec((QKV_W, TB), lambda i: (0, i)),
            pl.BlockSpec((TB, KV_RANK), lambda i: (i, 0)),
            pl.BlockSpec((TB, HEAD_PAD), lambda i: (i, 0)),
        ],
        out_shape=[
            jax.ShapeDtypeStruct((QKV_W, N_TOK), BF16),
            jax.ShapeDtypeStruct((N_TOK, KV_RANK), F32),
            jax.ShapeDtypeStruct((N_TOK, HEAD_PAD), F32),
        ],
        compiler_params=_cparams(("parallel",)),
        name="mla_proj",
    )(x_p, x_s, p["mod"], p["norm_mix_g"], p["m_w_dq"], p["m_norm_q"], p["m_w_uq_t"], p["m_gq"], p["m_w_dkv"],
      p["m_norm_kv"], p["rope_t"])


def _kv_block_source(i):
    n_sb = DEC_BATCH * KV_LEN // TB
    per = KV_LEN // TB
    b = jnp.minimum(i // per, DEC_BATCH - 1)
    jj = i % per
    return jnp.logical_and(i < n_sb, jj < PAST_LEN // TB), b, jj


def _kv_expand_kernel(cckv_ref, ckrp_ref, ckv_ref, krp_ref, wukvt_ref, gk_ref, tabt_ref, k_ref, vt_ref):
    from_cache, _, _ = _kv_block_source(pl.program_id(0))
    ckv = jnp.where(from_cache, cckv_ref[...], ckv_ref[...])
    krp = jnp.where(from_cache, ckrp_ref[...], krp_ref[...])
    kvt = _dot_nt(wukvt_ref[...], ckv.astype(BF16))
    k_rope = krp.T[QK_NOPE_DIM:QK_HEAD_DIM, :]
    row = lax.broadcasted_iota(jnp.int32, (QK_NOPE_DIM, TB), 0)
    ones_row = jnp.where(row == 0, 1.0, 0.0)
    for hd in range(N_HEADS):
        r0 = hd * HEAD_PAD
        kh = _head_norm_rope(kvt[r0:r0 + QK_NOPE_DIM, :], k_rope, gk_ref, tabt_ref)
        k_ref[:, r0:r0 + HEAD_PAD] = kh.T.astype(BF16)
        vt_ref[r0:r0 + HEAD_PAD, :] = jnp.concatenate([ones_row, kvt[r0 + QK_NOPE_DIM:r0 + HEAD_PAD, :]],
                                                      axis=0).astype(BF16)


def _kv_expand(ckv, krp, p, layer):
    j = layer // 2
    n_sb = DEC_BATCH * KV_LEN // TB
    n_cache = PAST_LEN // TB
    n_pb = N_PROMPT // TB
    lat_blocks = DEC_SEQ // TB

    def tab_idx(i):
        from_cache, _, jj = _kv_block_source(i)
        return (0, 0, jnp.where((i >= n_sb) | from_cache, ROPE_BLOCKS, jj - n_cache))

    def cache_idx(i):
        _, b, jj = _kv_block_source(i)
        return (b, j, jnp.minimum(jj, n_cache - 1), 0)

    def tok_idx(i):
        _, b, jj = _kv_block_source(i)
        latent = n_pb + b * lat_blocks + jnp.maximum(jj - n_cache, 0)
        return (jnp.where(i < n_sb, latent, i - n_sb), 0)

    return pl.pallas_call(
        _kv_expand_kernel,
        grid=(N_KV_ROWS // TB,),
        in_specs=[
            pl.BlockSpec((None, None, TB, KV_RANK), cache_idx),
            pl.BlockSpec((None, None, TB, HEAD_PAD), cache_idx),
            pl.BlockSpec((TB, KV_RANK), tok_idx),
            pl.BlockSpec((TB, HEAD_PAD), tok_idx),
            _layer_spec((QKV_W, KV_RANK), j),
            _layer_spec((HEAD_PAD, 1), j),
            pl.BlockSpec((3, HEAD_PAD, TB), tab_idx),
        ],
        out_specs=[
            pl.BlockSpec((TB, QKV_W), lambda i: (i, 0)),
            pl.BlockSpec((QKV_W, TB), lambda i: (0, i)),
        ],
        out_shape=[
            jax.ShapeDtypeStruct((N_KV_ROWS, QKV_W), BF16),
            jax.ShapeDtypeStruct((QKV_W, N_KV_ROWS), BF16),
        ],
        compiler_params=_cparams(("parallel",)),
        name="mla_kv_expand",
    )(p["cache_ckv"], p["cache_krp"], ckv, krp, p["m_w_ukv_t"], p["m_gk"], p["rope_t"])


def _attn_kernel(qt_ref, k_ref, vt_ref, o_ref, s_a, s_b, *, t_k, hps, qps):
    c = (1.0 / math.sqrt(QK_HEAD_DIM)) * math.log2(math.e)
    kc = min(TKC, t_k)
    n_chunks = t_k // kc
    per_it = min(ATTN_CHUNKS_PER_ITER, n_chunks)
    n_it = n_chunks // per_it
    bufs = (s_a, s_b)

    units = [(qb, h) for qb in range(qps) for h in range(hps)]

    def rows(h):
        return slice(h * HEAD_PAD, (h + 1) * HEAD_PAD)

    def cols(qb):
        return slice(qb * TQ, (qb + 1) * TQ)

    def scores(u, off, m8):
        qb, h = units[u]
        st = _dot(k_ref[pl.ds(off, kc), rows(h)], qt_ref[rows(h), cols(qb)])
        bufs[u % 2][pl.ds(off, kc), :] = st
        return jnp.maximum(m8, jnp.max(st.reshape(kc // SUBLANE, SUBLANE, TQ), axis=0))

    def weigh(u, off, m, acc):
        pt = jnp.exp2((bufs[u % 2][pl.ds(off, kc), :] - m) * c).astype(BF16)
        return acc + _dot(vt_ref[rows(units[u][1]), pl.ds(off, kc)], pt)

    def phase(u_scores, u_weigh, m):
        def body(it, carry):
            m8, acc = carry
            for j in range(per_it):
                off = (it * per_it + j) * kc
                off = off if isinstance(off, int) else pl.multiple_of(off, kc)
                if u_scores is not None:
                    m8 = scores(u_scores, off, m8)
                if u_weigh is not None:
                    acc = weigh(u_weigh, off, m, acc)
            return m8, acc

        init = (jnp.full((SUBLANE, TQ), NEG_BIG, F32), jnp.zeros((HEAD_PAD, TQ), F32))
        return body(0, init) if n_it == 1 else lax.fori_loop(0, n_it, body, init)

    outs = []
    if n_chunks == 1:
        sts = [_dot(k_ref[:, rows(h)], qt_ref[rows(h), cols(qb)]) for qb, h in units]
        for (qb, h), st in zip(units, sts):
            m = jnp.max(st, axis=0, keepdims=True)
            acc = _dot(vt_ref[rows(h), :], jnp.exp2((st - m) * c).astype(BF16))
            outs.append(acc[QK_NOPE_DIM:, :] / acc[0:1, :])
    else:
        m8, _ = phase(0, None, None)
        for u in range(1, len(units) + 1):
            m = jnp.max(m8, axis=0, keepdims=True)
            m8, acc = phase(u if u < len(units) else None, u - 1, m)
            outs.append(acc[QK_NOPE_DIM:, :] / acc[0:1, :])
    for qb in range(qps):
        o_ref[cols(qb), :] = jnp.concatenate(outs[qb * hps:(qb + 1) * hps], axis=0).T.astype(BF16)


def _attention(qt, k, vt, *, n_batch, t_q, t_k, q_row0, kv_row0, hps, qps):
    tq = TQ * qps
    nq = t_q // tq
    q0 = q_row0 // tq
    k0 = kv_row0 // t_k
    return pl.pallas_call(
        functools.partial(_attn_kernel, t_k=t_k, hps=hps, qps=qps),
        grid=(n_batch, N_HEADS // hps, nq),
        in_specs=[
            pl.BlockSpec((hps * HEAD_PAD, tq), lambda b, h, i: (h, q0 + b * nq + i)),
            pl.BlockSpec((t_k, hps * HEAD_PAD), lambda b, h, i: (k0 + b, h)),
            pl.BlockSpec((hps * HEAD_PAD, t_k), lambda b, h, i: (h, k0 + b)),
        ],
        out_specs=pl.BlockSpec((tq, hps * V_HEAD_DIM), lambda b, h, i: (b * nq + i, h)),
        out_shape=jax.ShapeDtypeStruct((n_batch * t_q, N_HEADS * V_HEAD_DIM), BF16),
        scratch_shapes=[pltpu.VMEM((t_k, TQ), F32), pltpu.VMEM((t_k, TQ), F32)],
        compiler_params=_cparams(("parallel", "parallel", "parallel")),
        name=f"mla_attention_tk{t_k}",
    )(qt, k, vt)


def _out_proj_kernel(ap_ref, as_ref, xp_ref, xs_ref, mod_ref, wo_ref, o_ref):
    is_prompt = pl.program_id(0) < N_PROMPT // TB
    a = jnp.where(is_prompt, ap_ref[...], as_ref[...])
    x = jnp.where(is_prompt, xp_ref[...], xs_ref[...])
    o_ref[...] = x + mod_ref[0][2:3] * _dot(a, wo_ref[...])


def _out_proj(attn_p, attn_s, x_p, x_s, p, layer):
    return pl.pallas_call(
        _out_proj_kernel,
        grid=(N_TOK // TB,),
        in_specs=_split_rows_specs() + _split_rows_specs() + [
            _mod_spec(TB, layer),
            _layer_spec((D_MODEL, D_MODEL), layer // 2),
        ],
        out_specs=pl.BlockSpec((TB, D_MODEL), lambda i: (i, 0)),
        out_shape=jax.ShapeDtypeStruct((N_TOK, D_MODEL), F32),
        compiler_params=_cparams(("parallel",)),
        name="mla_out_proj",
    )(attn_p, attn_s, x_p, x_s, p["mod"], p["m_w_o"])


def _mla_layer(x_p, x_s, p, layer):
    qt, ckv, krp = _mla_proj(x_p, x_s, p, layer)
    k, vt = _kv_expand(ckv, krp, p, layer)
    a_p = _attention(qt, k, vt, n_batch=BATCH, t_q=SEQ, t_k=SEQ, q_row0=0, kv_row0=DEC_BATCH * KV_LEN,
                     hps=N_HEADS, qps=1)
    a_s = _attention(qt, k, vt, n_batch=DEC_BATCH, t_q=DEC_SEQ, t_k=KV_LEN, q_row0=N_PROMPT, kv_row0=0,
                     hps=4, qps=4)
    x = _out_proj(a_p, a_s, x_p, x_s, p, layer)
    state_ckv = ckv[:N_PROMPT].reshape(BATCH, SEQ, KV_RANK)
    state_krope = krp[:N_PROMPT, QK_NOPE_DIM:QK_HEAD_DIM].reshape(BATCH, SEQ, QK_ROPE_DIM)
    return x, state_ckv, state_krope


def _tile_plan(cnt8, plan_ref, start_ref):
    tm = float(TM)
    cnt_row = cnt8[0:1, :]
    cnt_col = cnt8.T[:, 0:1]
    tiles_row = jnp.floor((cnt_row + (tm - 1.0)) * (1.0 / tm))
    tiles_col = jnp.floor((cnt_col + (tm - 1.0)) * (1.0 / tm))
    sub = lax.broadcasted_iota(jnp.int32, (LANE, LANE), 0).astype(F32)
    lan = lax.broadcasted_iota(jnp.int32, (LANE, LANE), 1).astype(F32)
    tile_end_row = jnp.sum(jnp.where(sub <= lan, tiles_col, 0.0), axis=0, keepdims=True)
    tile_end_col = jnp.sum(jnp.where(lan <= sub, tiles_row, 0.0), axis=1, keepdims=True)
    n_used = jnp.max(tile_end_row, axis=1, keepdims=True)
    start_col = (tile_end_col - tiles_col) * tm
    end_col = start_col + cnt_col
    cand = jnp.where(jnp.logical_and(lan > sub, tiles_row > 0.0), lan, float(LANE))
    next_col = jnp.min(cand, axis=1, keepdims=True)
    next_col = jnp.where(next_col < float(LANE), next_col, -1.0)
    n_lanes = plan_ref.shape[1]
    tidx = jnp.minimum(lax.broadcasted_iota(jnp.int32, (LANE, n_lanes), 1).astype(F32), n_used - 1.0)
    esub = lax.broadcasted_iota(jnp.int32, (LANE, n_lanes), 0).astype(F32)
    te_row = jnp.sum(jnp.where(tile_end_col <= tidx, 1.0, 0.0), axis=0, keepdims=True)
    mine = esub == te_row
    end_at = jnp.sum(jnp.where(mine, end_col, 0.0), axis=0, keepdims=True)
    tv_row = jnp.clip(end_at - tidx[0:1, :] * tm, 0.0, tm)
    nx_row = jnp.sum(jnp.where(mine, next_col, 0.0), axis=0, keepdims=True)
    nu_row = jnp.broadcast_to(n_used, (1, n_lanes))
    plan_ref[...] = jnp.concatenate([te_row, tv_row, nx_row, nu_row, jnp.zeros((SUBLANE - 4, n_lanes), F32)], axis=0)
    start_ref[...] = jnp.broadcast_to(start_col, (LANE, LANE))


def _route_kernel(x_ref, mod_ref, g_ref, wrt_ref, brt_ref, triu_ref, h_ref, metat_ref, plan_ref, start_ref, carry):
    i = pl.program_id(0)

    @pl.when(i == 0)
    def _():
        carry[...] = jnp.zeros_like(carry)

    x = x_ref[...]
    m = mod_ref[0]
    h = _rms(x, g_ref[...]) * (1.0 + m[4:5]) + m[3:4]
    h_ref[...] = _pack_bf16_pairs(h)
    logits = _dot_nt(wrt_ref[...], h.astype(BF16)) + brt_ref[...]
    n_tok = logits.shape[1]
    esub = lax.broadcasted_iota(jnp.int32, logits.shape, 0).astype(F32)
    work = logits
    sel = jnp.zeros(logits.shape, F32)
    hits, firsts, tops = [], [], []
    for k in range(TOP_K):
        mk = jnp.max(work, axis=0, keepdims=True)
        first = jnp.min(jnp.where(work == mk, esub, float(N_EXPERTS)), axis=0, keepdims=True)
        hit = esub == first
        sel = jnp.where(hit, 1.0, sel)
        work = jnp.where(hit, -jnp.inf, work)
        hits.append(hit)
        firsts.append(first)
        tops.append(mk)
    es = [jnp.exp(t - tops[0]) for t in tops]
    denom = es[0] + es[1] + es[2] + es[3]
    pos = _dot(sel.astype(BF16), triu_ref[...]) + carry[:, 0:1]
    carry[...] = carry[...] + jnp.sum(sel, axis=1, keepdims=True)

    @pl.when(i == pl.num_programs(0) - 1)
    def _():
        counts = jnp.concatenate([carry[...], jnp.zeros((LANE - N_EXPERTS, LANE), F32)], axis=0).T
        _tile_plan(counts[:SUBLANE, :], plan_ref, start_ref)

    ranks = [jnp.sum(jnp.where(hit, pos, 0.0), axis=0, keepdims=True) for hit in hits]
    gates = [e / denom for e in es]
    metat_ref[...] = jnp.concatenate(firsts + gates + ranks + [jnp.zeros((SUBLANE // 2, n_tok), F32)], axis=0)


def _route(x, p, layer):
    tb = ROUTE_TB
    return pl.pallas_call(
        _route_kernel,
        grid=(N_TOK // tb,),
        in_specs=[
            pl.BlockSpec((tb, D_MODEL), lambda i: (i, 0)),
            _mod_spec(tb, layer),
            _layer_spec((1, D_MODEL), layer),
            _layer_spec((N_EXPERTS, D_MODEL), layer),
            _layer_spec((N_EXPERTS, 1), layer),
            _const_spec((tb, tb)),
        ],
        out_specs=[
            pl.BlockSpec((tb, D_MODEL // 2), lambda i: (i, 0)),
            pl.BlockSpec((2 * SUBLANE, tb), lambda i: (0, i)),
            _const_spec((SUBLANE, PLAN_LANES)),
            _const_spec((LANE, LANE)),
        ],
        out_shape=[
            jax.ShapeDtypeStruct((N_TOK, D_MODEL // 2), jnp.uint32),
            jax.ShapeDtypeStruct((2 * SUBLANE, N_TOK), F32),
            jax.ShapeDtypeStruct((SUBLANE, PLAN_LANES), F32),
            jax.ShapeDtypeStruct((LANE, LANE), F32),
        ],
        scratch_shapes=[pltpu.VMEM((N_EXPERTS, LANE), F32)],
        compiler_params=_cparams(("arbitrary",)),
        name="moe_route",
    )(x, p["mod"], p["norm_ffn_g"], p["e_w_router_t"], p["e_b_router"], p["triu"])


def _slots_kernel(start_ref, metat_ref, dest_ref):
    start_col = start_ref[:, 0:1]
    esub = lax.broadcasted_iota(jnp.int32, (LANE, SLOT_TB), 0).astype(F32)
    rows = []
    for k in range(TOP_K):
        e = metat_ref[k:k + 1, :]
        first = jnp.sum(jnp.where(esub == e, start_col, 0.0), axis=0, keepdims=True)
        rows.append(first + metat_ref[2 * TOP_K + k:2 * TOP_K + k + 1, :])
    dest_ref[...] = jnp.concatenate(rows, axis=0).astype(jnp.int32)


def _slots(start, meta_t):
    return pl.pallas_call(
        _slots_kernel,
        grid=(N_TOK // SLOT_TB,),
        in_specs=[
            _const_spec((LANE, LANE)),
            pl.BlockSpec((2 * SUBLANE, SLOT_TB), lambda i: (0, i)),
        ],
        out_specs=pl.BlockSpec((TOP_K, SLOT_TB), lambda i: (0, i)),
        out_shape=jax.ShapeDtypeStruct((TOP_K, N_TOK), jnp.int32),
        compiler_params=_cparams(("parallel",)),
        name="moe_slots",
    )(start, meta_t)


def _sc_gather(table, idx, ch):
    b, w = idx.shape[0], table.shape[1]
    per_w = b // SC_WORKERS
    n_ch = per_w // ch
    assert per_w * SC_WORKERS == b and n_ch * ch == per_w and n_ch % 2 == 0
    mesh = plsc.VectorSubcoreMesh(core_axis_name="c", subcore_axis_name="s")

    @functools.partial(
        pl.kernel, mesh=mesh,
        out_type=jax.ShapeDtypeStruct((b, w), table.dtype),
        scratch_types=[
            pltpu.VMEM((n_ch, ch), jnp.int32),
            pltpu.VMEM((ch, w), table.dtype),
            pltpu.VMEM((ch, w), table.dtype),
            pltpu.SemaphoreType.DMA, pltpu.SemaphoreType.DMA,
            pltpu.SemaphoreType.DMA, pltpu.SemaphoreType.DMA,
        ],
        name="sc_row_gather",
    )
    def gather_rows(table_hbm, idx_hbm, out_hbm, idx_v, buf0, buf1, g0, g1, s0, s1):
        wid = lax.axis_index("s") * SC_CORES + lax.axis_index("c")
        base = wid * per_w
        pltpu.sync_copy(idx_hbm.at[wid], idx_v)

        def gather(j, buf, sem):
            return pltpu.make_async_copy(table_hbm.at[idx_v.at[j]], buf, sem)

        def store(j, buf, sem):
            return pltpu.make_async_copy(buf, out_hbm.at[pl.ds(base + j * ch, ch)], sem)

        gather(0, buf0, g0).start()

        @pl.loop(0, n_ch, step=2)
        def _(j):
            @pl.when(j > 0)
            def _():
                store(j - 1, buf1, s1).wait()

            gather(j + 1, buf1, g1).start()
            gather(j, buf0, g0).wait()
            store(j, buf0, s0).start()
            gather(j + 1, buf1, g1).wait()
            store(j + 1, buf1, s1).start()
            store(j, buf0, s0).wait()

            @pl.when(j + 2 < n_ch)
            def _():
                gather(j + 2, buf0, g0).start()

        store(n_ch - 1, buf1, s1).wait()

    return gather_rows(table, idx.reshape(SC_WORKERS, n_ch, ch))


def _sc_dispatch(rows, dest_t, ch):
    n, w = rows.shape
    per_w = n // SC_WORKERS
    n_ch = per_w // ch
    assert per_w * SC_WORKERS == n and n_ch * ch == per_w and n_ch % 2 == 0
    mesh = plsc.VectorSubcoreMesh(core_axis_name="c", subcore_axis_name="s")
    idx = dest_t.reshape(TOP_K, SC_WORKERS, n_ch, ch)

    @functools.partial(
        pl.kernel, mesh=mesh,
        out_type=jax.ShapeDtypeStruct((N_SLOTS, w), rows.dtype),
        scratch_types=[
            pltpu.VMEM((TOP_K * n_ch, ch), jnp.int32),
            pltpu.VMEM((ch, w), rows.dtype),
            pltpu.VMEM((ch, w), rows.dtype),
            pltpu.SemaphoreType.DMA, pltpu.SemaphoreType.DMA,
            pltpu.SemaphoreType.DMA, pltpu.SemaphoreType.DMA,
        ],
        name="sc_row_dispatch",
    )
    def dispatch_rows(rows_hbm, idx_hbm, out_hbm, idx_v, buf0, buf1, l0, l1, s0, s1):
        wid = lax.axis_index("s") * SC_CORES + lax.axis_index("c")
        base = wid * per_w
        for k in range(TOP_K):
            pltpu.sync_copy(idx_hbm.at[k, wid], idx_v.at[pl.ds(k * n_ch, n_ch)])

        def load(j, buf, sem):
            return pltpu.make_async_copy(rows_hbm.at[pl.ds(base + j * ch, ch)], buf, sem)

        def scatter(j, k, buf, sem):
            return pltpu.make_async_copy(buf, out_hbm.at[idx_v.at[k * n_ch + j]], sem)

        load(0, buf0, l0).start()

        @pl.loop(0, n_ch, step=2)
        def _(j):
            load(j + 1, buf1, l1).start()
            load(j, buf0, l0).wait()
            for k in range(TOP_K):
                scatter(j, k, buf0, s0).start()
            load(j + 1, buf1, l1).wait()
            for k in range(TOP_K):
                scatter(j + 1, k, buf1, s1).start()
            for k in range(TOP_K):
                scatter(j, k, buf0, s0).wait()

            @pl.when(j + 2 < n_ch)
            def _():
                load(j + 2, buf0, l0).start()

            for k in range(TOP_K):
                scatter(j + 1, k, buf1, s1).wait()

    return dispatch_rows(rows, idx)


def _deinterleave_matrix():
    src = jnp.arange(2 * LANE)[:, None]
    dst = jnp.arange(2 * LANE)[None, :]
    want = jnp.where(dst < LANE, 2 * dst, 2 * (dst - LANE) + 1)
    return (src == want).astype(BF16)


def _expert_kernel(te_ref, nu_ref, tv_ref, nx_ref, x_ref, wgu_hbm, bgu_ref, wd_hbm, bd_ref, perm_ref, o_ref,
                   wgu_st, wd_st, wgu_bf, wd_bf, sems, *, layer):
    i = pl.program_id(0)
    prev = te_ref[jnp.maximum(i - 1, 0)]
    fresh = jnp.logical_or(i == 0, te_ref[i] != prev)

    def fetch(e):
        return (pltpu.make_async_copy(wgu_hbm.at[layer, e], wgu_st, sems.at[0]),
                pltpu.make_async_copy(wd_hbm.at[layer, e], wd_st, sems.at[1]))

    @pl.when(i == 0)
    def _():
        for cp in fetch(te_ref[0]):
            cp.start()

    @pl.when(jnp.logical_and(fresh, i < nu_ref[0]))
    def _():
        for cp in fetch(te_ref[i]):
            cp.wait()
        for b in range(2 * D_FF // (2 * LANE)):
            sl = slice(b * 2 * LANE, (b + 1) * 2 * LANE)
            wgu_bf[:, sl] = _dot(wgu_st[:, sl].astype(BF16), perm_ref[...]).astype(BF16)
        wd_bf[...] = wd_st[...].astype(BF16)

        @pl.when(nx_ref[i] >= 0)
        def _():
            for cp in fetch(nx_ref[i]):
                cp.start()

    def ffn(n_rows):
        row = lax.broadcasted_iota(jnp.int32, (n_rows, D_MODEL // 2), 0)
        w = jnp.where(row < tv_ref[i], x_ref[:n_rows, :], jnp.uint32(0))
        x = _unpack_bf16_pairs(w).astype(BF16)
        gu = _dot(x, wgu_bf[...]) + bgu_ref[...]
        acts = []
        for b in range(D_FF // LANE):
            glu = jnp.minimum(gu[:, b * 2 * LANE:b * 2 * LANE + LANE], SWIGLU_LIMIT)
            lin = jnp.clip(gu[:, b * 2 * LANE + LANE:(b + 1) * 2 * LANE], -SWIGLU_LIMIT, SWIGLU_LIMIT)
            acts.append((glu * jax.nn.sigmoid(SWIGLU_ALPHA * glu) * (lin + 1.0)).astype(BF16))
        act = jnp.concatenate(acts, axis=1)
        o_ref[:n_rows, :] = _pack_bf16_pairs(_dot(act, wd_bf[...]) + bd_ref[...])

    in_use = i < nu_ref[0]

    @pl.when(jnp.logical_and(in_use, tv_ref[i] > TM // 2))
    def _():
        ffn(TM)

    @pl.when(jnp.logical_and(in_use, tv_ref[i] <= TM // 2))
    def _():
        ffn(TM // 2)


def _experts(buf, tile_expert, n_used, tile_valid, tile_next, p, layer):
    def row_idx(i, te, nu, tv, nx):
        return (jnp.minimum(i, nu[0] - 1), 0)

    def b_idx(i, te, nu, tv, nx):
        return (layer, te[i], 0, 0)

    grid_spec = pltpu.PrefetchScalarGridSpec(
        num_scalar_prefetch=4,
        grid=(N_TILES,),
        in_specs=[
            pl.BlockSpec((TM, D_MODEL // 2), row_idx),
            pl.BlockSpec(memory_space=pl.ANY),
            pl.BlockSpec((None, None, 1, 2 * D_FF), b_idx),
            pl.BlockSpec(memory_space=pl.ANY),
            pl.BlockSpec((None, None, 1, D_MODEL), b_idx),
            _const_spec((2 * LANE, 2 * LANE)),
        ],
        out_specs=pl.BlockSpec((TM, D_MODEL // 2), row_idx),
        scratch_shapes=[
            pltpu.VMEM((D_MODEL, 2 * D_FF), F32),
            pltpu.VMEM((D_FF, D_MODEL), F32),
            pltpu.VMEM((D_MODEL, 2 * D_FF), BF16),
            pltpu.VMEM((D_FF, D_MODEL), BF16),
            pltpu.SemaphoreType.DMA((2,)),
        ],
    )
    return pl.pallas_call(
        functools.partial(_expert_kernel, layer=layer),
        grid_spec=grid_spec,
        out_shape=jax.ShapeDtypeStruct((N_SLOTS, D_MODEL // 2), jnp.uint32),
        compiler_params=_cparams(("arbitrary",)),
        name="moe_experts",
    )(tile_expert, n_used, tile_valid, tile_next, buf, p["e_w_gu"], p["e_b_gu"], p["e_w_down"], p["e_b_down"],
      p["deinterleave"])


def _pack_bf16_pairs(v):
    half = v.shape[1] // 2
    bits = pltpu.bitcast(v.astype(BF16).astype(F32), jnp.uint32)
    return (bits[:, half:] & jnp.uint32(0xFFFF0000)) | (bits[:, :half] >> 16)


def _unpack_bf16_pairs(w):
    return jnp.concatenate([pltpu.bitcast(w << 16, F32), pltpu.bitcast(w & jnp.uint32(0xFFFF0000), F32)],
                           axis=1)


def _combine_kernel(x_ref, mod_ref, y_ref, mt_ref, o_ref):
    w = mt_ref[...].T[:, TOP_K:2 * TOP_K]
    y = _unpack_bf16_pairs(y_ref[0]) * w[:, 0:1]
    for k in range(1, TOP_K):
        y = y + _unpack_bf16_pairs(y_ref[k]) * w[:, k:k + 1]
    o_ref[...] = x_ref[...] + mod_ref[0][5:6] * y


def _combine(x, y4, meta_t, p, layer, part):
    n_rows = N_TOK // MOE_PARTS
    tb = TB
    first = part * n_rows // tb
    return pl.pallas_call(
        _combine_kernel,
        grid=(n_rows // tb,),
        in_specs=[
            pl.BlockSpec((tb, D_MODEL), lambda i: (i + first, 0)),
            _mod_spec(tb, layer, first),
            pl.BlockSpec((TOP_K, tb, D_MODEL // 2), lambda i: (0, i, 0)),
            pl.BlockSpec((2 * SUBLANE, tb), lambda i: (0, i + first)),
        ],
        out_specs=pl.BlockSpec((tb, D_MODEL), lambda i: (i, 0)),
        out_shape=jax.ShapeDtypeStruct((n_rows, D_MODEL), F32),
        compiler_params=_cparams(("parallel",)),
        name="moe_combine",
    )(x, p["mod"], y4, meta_t)


def _moe_layer(x, p, layer):
    hp, meta_t, plan, start = _route(x, p, layer)
    plan = plan[:4, :N_TILES].astype(jnp.int32)
    tile_expert, tile_valid, tile_next, n_used = plan[0], plan[1], plan[2], plan[3, :1]
    dest_t = _slots(start, meta_t)
    buf = _sc_dispatch(hp, dest_t, SC_CHUNK_ROWS)
    yb = _experts(buf, tile_expert, n_used, tile_valid, tile_next, p, layer)
    n_rows = N_TOK // MOE_PARTS
    outs = []
    for part in range(MOE_PARTS):
        idx = dest_t[:, part * n_rows:(part + 1) * n_rows].reshape(-1)
        y4 = _sc_gather(yb, idx, SC_CHUNK_ROWS).reshape(TOP_K, n_rows, D_MODEL // 2)
        outs.append(_combine(x, y4, meta_t, p, layer, part))
    return outs


def _prepare(c, cache_ckv, cache_krope, c_ctx, norm_mix_g, norm_ffn_g, w_mod, b_mod, g_w_in, g_b_in, g_norm_v,
             g_w_s, g_b_s, g_w_out, m_w_dq, m_norm_q, m_w_uq, m_w_dkv, m_norm_kv, m_w_ukv, m_qk_norm_q,
             m_qk_norm_k, m_w_o, e_w_router, e_b_router, e_w_gu, e_b_gu, e_w_down, e_b_down):
    n_mla = m_w_dq.shape[0]
    cond = jnp.concatenate([c_ctx[None, :], c, jnp.zeros((SUBLANE - N_COND, D_MODEL), F32)], axis=0)
    wdkv = jnp.concatenate([m_w_dkv[..., :KV_RANK], jnp.zeros((n_mla, D_MODEL, QK_NOPE_DIM), F32),
                            m_w_dkv[..., KV_RANK:], jnp.zeros((n_mla, D_MODEL, HEAD_PAD - QK_HEAD_DIM), F32)],
                           axis=-1)
    w_uq = jnp.pad(m_w_uq.reshape(n_mla, Q_RANK, N_HEADS, QK_HEAD_DIM),
                   ((0, 0), (0, 0), (0, 0), (0, HEAD_PAD - QK_HEAD_DIM))).reshape(n_mla, Q_RANK, QKV_W)

    def gain_col(g):
        return jnp.pad(g, ((0, 0), (0, HEAD_PAD - QK_HEAD_DIM)))[:, :, None]

    return {
        "mod": _modulation(cond, w_mod, b_mod),
        "rope_t": _rope_tables(),
        "norm_mix_g": norm_mix_g[:, None, :],
        "norm_ffn_g": norm_ffn_g[:, None, :],
        "g_w_in": g_w_in.astype(BF16),
        "g_b_in": g_b_in[:, None, :],
        "g_norm_v": g_norm_v[:, None, :],
        "g_w_s": g_w_s.astype(BF16),
        "g_b_st": jnp.swapaxes(g_b_s, 1, 2),
        "g_w_out": g_w_out.astype(BF16),
        "m_w_dq": m_w_dq.astype(BF16),
        "m_norm_q": m_norm_q[:, None, :],
        "m_w_uq_t": jnp.swapaxes(w_uq, 1, 2).astype(BF16),
        "m_gq": gain_col(m_qk_norm_q),
        "m_w_dkv": wdkv.astype(BF16),
        "m_norm_kv": m_norm_kv[:, None, :],
        "m_w_ukv_t": jnp.swapaxes(m_w_ukv, 1, 2).astype(BF16),
        "m_gk": gain_col(m_qk_norm_k),
        "m_w_o": m_w_o.astype(BF16),
        "cache_ckv": cache_ckv,
        "cache_krp": jnp.pad(cache_krope, ((0, 0), (0, 0), (0, 0), (QK_NOPE_DIM, HEAD_PAD - QK_HEAD_DIM))),
        "e_w_router_t": jnp.swapaxes(e_w_router, 1, 2).astype(BF16),
        "e_b_router": e_b_router[:, :, None],
        "triu": jnp.tri(ROUTE_TB, ROUTE_TB, -1, dtype=BF16).T,
        "e_w_gu": e_w_gu,
        "e_b_gu": e_b_gu.reshape(DEPTH, N_EXPERTS, D_FF // LANE, LANE, 2).swapaxes(3, 4).reshape(
            DEPTH, N_EXPERTS, 1, 2 * D_FF),
        "e_w_down": e_w_down,
        "e_b_down": e_b_down[:, :, None, :],
        "deinterleave": _deinterleave_matrix(),
    }


def kernel(x_prompt, x_sample, c, cache_ckv, cache_krope, c_ctx, norm_mix_g, norm_ffn_g, w_mod, b_mod,
           g_w_in, g_b_in, g_norm_v, g_w_s, g_b_s, g_w_out, m_w_dq, m_norm_q, m_w_uq, m_w_dkv,
           m_norm_kv, m_w_ukv, m_qk_norm_q, m_qk_norm_k, m_w_o, e_w_router, e_b_router, e_w_gu,
           e_b_gu, e_w_down, e_b_down):
    p = _prepare(c, cache_ckv, cache_krope, c_ctx, norm_mix_g, norm_ffn_g, w_mod, b_mod, g_w_in, g_b_in,
                 g_norm_v, g_w_s, g_b_s, g_w_out, m_w_dq, m_norm_q, m_w_uq, m_w_dkv, m_norm_kv, m_w_ukv,
                 m_qk_norm_q, m_qk_norm_k, m_w_o, e_w_router, e_b_router, e_w_gu, e_b_gu, e_w_down, e_b_down)
    assert MOE_PARTS == 2 and N_PROMPT == N_SAMPLE
    x_p, x_s = x_prompt.reshape(N_PROMPT, D_MODEL), x_sample.reshape(N_SAMPLE, D_MODEL)
    ckv_states, krope_states = [], []
    for layer in range(DEPTH):
        if layer % 2 == 0:
            x = _gmlp_layer(x_p, x_s, p, layer)
        else:
            x, s_ckv, s_krope = _mla_layer(x_p, x_s, p, layer)
            ckv_states.append(s_ckv)
            krope_states.append(s_krope)
        x_p, x_s = _moe_layer(x, p, layer)
    y_prompt = x_p.reshape(BATCH, SEQ, D_MODEL)
    y_sample = x_s.reshape(DEC_BATCH, DEC_SEQ, D_MODEL)
    return (y_prompt, y_sample, jnp.stack(ckv_states, axis=1), jnp.stack(krope_states, axis=1))
```

```python
import functools
import math

import jax
import jax.numpy as jnp
from jax import lax
from jax.experimental import pallas as pl
from jax.experimental.pallas import tpu as pltpu
from jax.experimental.pallas import tpu_sc as plsc

F32 = jnp.float32
BF16 = jnp.bfloat16

D_MODEL = 1024
BATCH = 32
SEQ = 256
DEPTH = 4
DEC_BATCH = 2
DEC_SEQ = 4096
PAST_LEN = 512
GRID_W = 64
RMS_EPS = 1e-6
GMLP_WIDTH = 2 * D_MODEL
GMLP_GROUPS = 8
GROUP_W = GMLP_WIDTH // GMLP_GROUPS
CHUNK = 128
N_HEADS = 16
QK_NOPE_DIM = 64
QK_ROPE_DIM = 32
QK_HEAD_DIM = QK_NOPE_DIM + QK_ROPE_DIM
V_HEAD_DIM = 64
Q_RANK = 256
KV_RANK = 128
ROPE_THETA = 10000.0
N_EXPERTS = 32
TOP_K = 4
D_FF = D_MODEL
SWIGLU_LIMIT = 7.0
SWIGLU_ALPHA = 1.702

N_PROMPT = BATCH * SEQ
N_SAMPLE = DEC_BATCH * DEC_SEQ
N_TOK = N_PROMPT + N_SAMPLE
N_COND = 1 + DEC_BATCH
KV_LEN = PAST_LEN + DEC_SEQ
N_KV_ROWS = DEC_BATCH * KV_LEN + N_PROMPT

LANE = 128
SUBLANE = 8
HEAD_PAD = LANE
QKV_W = N_HEADS * HEAD_PAD
VMEM_LIMIT = 56 * 1024 * 1024

TB = 512
MOD_TN = 1536
TQ = 512
TKC = 256
ATTN_CHUNKS_PER_ITER = 9
TM = 512
N_TILES = N_TOK * TOP_K // TM + N_EXPERTS
N_SLOTS = N_TILES * TM
PLAN_LANES = -(-N_TILES // LANE) * LANE
SLOT_TB = 2048
ROUTE_TB = 512
MOE_PARTS = 2
SC_CORES = 2
SC_SUBCORES = 16
SC_WORKERS = SC_CORES * SC_SUBCORES
SC_CHUNK_ROWS = 64
ROPE_BLOCKS = DEC_SEQ // TB
NEG_BIG = -1e30


def _cparams(sem):
    return pltpu.CompilerParams(dimension_semantics=sem, vmem_limit_bytes=VMEM_LIMIT)


def _cond_of_block(i, tb):
    n_p = N_PROMPT // tb
    per = DEC_SEQ // tb
    return jnp.where(i < n_p, 0, 1 + (i - n_p) // per)


def _rms(x, g, n=None):
    n = x.shape[-1] if n is None else n
    ss = jnp.sum(x * x, axis=-1, keepdims=True) * (1.0 / n)
    return x * lax.rsqrt(ss + RMS_EPS) * g


def _dot(a, b):
    return jnp.dot(a, b, preferred_element_type=F32)


def _mod_kernel(c_ref, w_ref, b_ref, o_ref):
    c = c_ref[...]
    s = c * jax.nn.sigmoid(c)
    o_ref[0] = _dot(s.astype(BF16), w_ref[0].astype(BF16)) + b_ref[0]


def _modulation(cond, w_mod, b_mod):
    tn = MOD_TN
    out = pl.pallas_call(
        _mod_kernel,
        grid=(DEPTH, 6 * D_MODEL // tn),
        in_specs=[
            pl.BlockSpec((SUBLANE, D_MODEL), lambda l, j: (0, 0)),
            pl.BlockSpec((1, D_MODEL, tn), lambda l, j: (l, 0, j)),
            pl.BlockSpec((1, 1, tn), lambda l, j: (l, 0, j)),
        ],
        out_specs=pl.BlockSpec((1, SUBLANE, tn), lambda l, j: (l, 0, j)),
        out_shape=jax.ShapeDtypeStruct((DEPTH, SUBLANE, 6 * D_MODEL), F32),
        compiler_params=_cparams(("parallel", "parallel")),
        name="adaln_mod",
    )(cond, w_mod, b_mod.reshape(DEPTH, 1, 6 * D_MODEL))
    m = out[:, :N_COND].reshape(DEPTH, N_COND, 6, D_MODEL)
    return jnp.pad(m, ((0, 0), (0, 0), (0, SUBLANE - 6), (0, 0)))


def _mod_spec(tb, layer, first_block=0):
    return pl.BlockSpec((None, 1, SUBLANE, D_MODEL),
                        lambda i: (layer, _cond_of_block(i + first_block, tb), 0, 0))


def _const_spec(shape):
    nd = len(shape)
    return pl.BlockSpec(shape, lambda *_: (0,) * nd)


def _layer_spec(shape, j):
    nd = len(shape)
    return pl.BlockSpec((None,) + tuple(shape), lambda *_: (j,) + (0,) * nd)


def _gelu_tanh(x):
    a = math.sqrt(2.0 / math.pi)
    hx = 0.5 * x
    return hx + hx * jnp.tanh(x * (a + (0.044715 * a) * (x * x)))


def _gmlp_kernel(xp_ref, xs_ref, mod_ref, g_ref, win_ref, bin_ref, gv_ref, ws_ref, bst_ref, wout_ref, o_ref):
    x = jnp.where(pl.program_id(0) < N_PROMPT // TB, xp_ref[...], xs_ref[...])
    m = mod_ref[0]
    h = _rms(x, g_ref[...]) * (1.0 + m[1:2]) + m[0:1]
    hb = h.astype(BF16)
    zv = _gelu_tanh(_dot(hb, win_ref[:, GMLP_WIDTH:]) + bin_ref[:, GMLP_WIDTH:])
    vn = _rms(zv, gv_ref[...]).astype(BF16)
    rows = []
    for c in range(TB // CHUNK):
        cols = []
        for g in range(GMLP_GROUPS):
            blk = vn[c * CHUNK:(c + 1) * CHUNK, g * GROUP_W:(g + 1) * GROUP_W]
            cols.append(_dot(ws_ref[g], blk) + bst_ref[:, g:g + 1])
        rows.append(jnp.concatenate(cols, axis=1))
    vm = jnp.concatenate(rows, axis=0)
    u = _gelu_tanh(_dot(hb, win_ref[:, :GMLP_WIDTH]) + bin_ref[:, :GMLP_WIDTH])
    d = _dot((u * vm).astype(BF16), wout_ref[...])
    o_ref[...] = x + m[2:3] * d


def _split_rows_specs():
    n_pb = N_PROMPT // TB
    return [pl.BlockSpec((TB, D_MODEL), lambda i: (jnp.minimum(i, n_pb - 1), 0)),
            pl.BlockSpec((TB, D_MODEL), lambda i: (jnp.maximum(i - n_pb, 0), 0))]


def _gmlp_layer(x_p, x_s, p, layer):
    j = layer // 2
    return pl.pallas_call(
        _gmlp_kernel,
        grid=(N_TOK // TB,),
        in_specs=_split_rows_specs() + [
            _mod_spec(TB, layer),
            _layer_spec((1, D_MODEL), layer),
            _layer_spec((D_MODEL, 2 * GMLP_WIDTH), j),
            _layer_spec((1, 2 * GMLP_WIDTH), j),
            _layer_spec((1, GMLP_WIDTH), j),
            _layer_spec((GMLP_GROUPS, CHUNK, CHUNK), j),
            _layer_spec((CHUNK, GMLP_GROUPS), j),
            _layer_spec((GMLP_WIDTH, D_MODEL), j),
        ],
        out_specs=pl.BlockSpec((TB, D_MODEL), lambda i: (i, 0)),
        out_shape=jax.ShapeDtypeStruct((N_TOK, D_MODEL), F32),
        compiler_params=_cparams(("parallel",)),
        name="gmlp_mixer",
    )(x_p, x_s, p["mod"], p["norm_mix_g"], p["g_w_in"], p["g_b_in"], p["g_norm_v"], p["g_w_s"], p["g_b_st"],
      p["g_w_out"])


def _rope_tables():
    t = jnp.arange(DEC_SEQ)
    row_id = (t // GRID_W).astype(F32)
    col_id = (t % GRID_W).astype(F32)
    axis_dim = QK_ROPE_DIM // 2
    inv_freq = ROPE_THETA ** (-jnp.arange(0, axis_dim, 2, dtype=F32) / axis_dim)
    ang = jnp.stack([row_id[:, None] * inv_freq, col_id[:, None] * inv_freq], axis=1)
    cos, sin = jnp.cos(ang), jnp.sin(ang)
    zeros = jnp.zeros_like(sin)
    cos_l = jnp.concatenate([cos, cos], axis=-1).reshape(DEC_SEQ, QK_ROPE_DIM)
    s1_l = jnp.concatenate([-sin, zeros], axis=-1).reshape(DEC_SEQ, QK_ROPE_DIM)
    s2_l = jnp.concatenate([zeros, sin], axis=-1).reshape(DEC_SEQ, QK_ROPE_DIM)

    def widen(rope_part, nope_fill):
        left = jnp.full((DEC_SEQ, QK_NOPE_DIM), nope_fill, F32)
        right = jnp.zeros((DEC_SEQ, HEAD_PAD - QK_HEAD_DIM), F32)
        return jnp.concatenate([left, rope_part, right], axis=-1)

    pos = jnp.stack([widen(cos_l, 1.0), widen(s1_l, 0.0), widen(s2_l, 0.0)])
    ident_c = jnp.concatenate([jnp.ones((TB, QK_HEAD_DIM), F32),
                               jnp.zeros((TB, HEAD_PAD - QK_HEAD_DIM), F32)], axis=-1)
    ident = jnp.stack([ident_c, jnp.zeros_like(ident_c), jnp.zeros_like(ident_c)])
    return jnp.swapaxes(jnp.concatenate([pos, ident], axis=1), 1, 2)


def _dot_nt(a, b):
    return lax.dot_general(a, b, (((1,), (1,)), ((), ())), preferred_element_type=F32)


def _shift_rows(x, n):
    n = n % x.shape[0]
    return jnp.concatenate([x[n:], x[:n]], axis=0)


def _head_norm_rope(nope, rope, g_ref, tabt_ref):
    half = QK_ROPE_DIM // 4
    lo, hi = QK_NOPE_DIM, QK_HEAD_DIM
    ss = (jnp.sum(nope * nope, axis=0, keepdims=True) + jnp.sum(rope * rope, axis=0, keepdims=True))
    rs = lax.rsqrt(ss * (1.0 / QK_HEAD_DIM) + RMS_EPS)
    xn = nope * rs * g_ref[:lo, :]
    xr = rope * rs * g_ref[lo:hi, :]
    rot = (xr * tabt_ref[0, lo:hi, :] + _shift_rows(xr, half) * tabt_ref[1, lo:hi, :]
           + _shift_rows(xr, -half) * tabt_ref[2, lo:hi, :])
    return jnp.concatenate([xn, rot, jnp.zeros((HEAD_PAD - hi, nope.shape[1]), F32)], axis=0)


def _mla_proj_kernel(xp_ref, xs_ref, mod_ref, g_ref, wdq_ref, nq_ref, wuqt_ref, gq_ref, wdkv_ref, nkv_ref,
                     tabt_ref, qt_ref, ckv_ref, krp_ref):
    x = jnp.where(pl.program_id(0) < N_PROMPT // TB, xp_ref[...], xs_ref[...])
    m = mod_ref[0]
    h = _rms(x, g_ref[...]) * (1.0 + m[1:2]) + m[0:1]
    hb = h.astype(BF16)
    cq = _rms(_dot(hb, wdq_ref[...]), nq_ref[...])
    qt = _dot_nt(wuqt_ref[...], cq.astype(BF16))
    for hd in range(N_HEADS):
        r0 = hd * HEAD_PAD
        qh = _head_norm_rope(qt[r0:r0 + QK_NOPE_DIM, :], qt[r0 + QK_NOPE_DIM:r0 + QK_HEAD_DIM, :], gq_ref, tabt_ref)
        qt_ref[r0:r0 + HEAD_PAD, :] = qh.astype(BF16)
    kva = _dot(hb, wdkv_ref[...])
    ckv_ref[...] = _rms(kva[:, :KV_RANK], nkv_ref[...])
    krp_ref[...] = kva[:, KV_RANK:]


def _mla_proj(x_p, x_s, p, layer):
    j = layer // 2
    n_pb = N_PROMPT // TB

    def tab_idx(i):
        return (0, 0, jnp.where(i < n_pb, ROPE_BLOCKS, (i - n_pb) % ROPE_BLOCKS))

    return pl.pallas_call(
        _mla_proj_kernel,
        grid=(N_TOK // TB,),
        in_specs=_split_rows_specs() + [
            _mod_spec(TB, layer),
            _layer_spec((1, D_MODEL), layer),
            _layer_spec((D_MODEL, Q_RANK), j),
            _layer_spec((1, Q_RANK), j),
            _layer_spec((QKV_W, Q_RANK), j),
            _layer_spec((HEAD_PAD, 1), j),
            _layer_spec((D_MODEL, 2 * LANE), j),
            _layer_spec((1, KV_RANK), j),
            pl.BlockSpec((3, HEAD_PAD, TB), tab_idx),
        ],
        out_specs=[
            pl.BlockSpec((QKV_W, TB), lambda i: (0, i)),
            pl.BlockSpec((TB, KV_RANK), lambda i: (i, 0)),
            pl.BlockSpec((TB, HEAD_PAD), lambda i: (i, 0)),
        ],
        out_shape=[
            jax.ShapeDtypeStruct((QKV_W, N_TOK), BF16),
            jax.ShapeDtypeStruct((N_TOK, KV_RANK), F32),
            jax.ShapeDtypeStruct((N_TOK, HEAD_PAD), F32),
        ],
        compiler_params=_cparams(("parallel",)),
        name="mla_proj",
    )(x_p, x_s, p["mod"], p["norm_mix_g"], p["m_w_dq"], p["m_norm_q"], p["m_w_uq_t"], p["m_gq"], p["m_w_dkv"],
      p["m_norm_kv"], p["rope_t"])


def _kv_block_source(i):
    n_sb = DEC_BATCH * KV_LEN // TB
    per = KV_LEN // TB
    b = jnp.minimum(i // per, DEC_BATCH - 1)
    jj = i % per
    return jnp.logical_and(i < n_sb, jj < PAST_LEN // TB), b, jj


def _kv_expand_kernel(cckv_ref, ckrp_ref, ckv_ref, krp_ref, wukvt_ref, gk_ref, tabt_ref, k_ref, vt_ref):
    from_cache, _, _ = _kv_block_source(pl.program_id(0))
    ckv = jnp.where(from_cache, cckv_ref[...], ckv_ref[...])
    krp = jnp.where(from_cache, ckrp_ref[...], krp_ref[...])
    kvt = _dot_nt(wukvt_ref[...], ckv.astype(BF16))
    k_rope = krp.T[QK_NOPE_DIM:QK_HEAD_DIM, :]
    row = lax.broadcasted_iota(jnp.int32, (QK_NOPE_DIM, TB), 0)
    ones_row = jnp.where(row == 0, 1.0, 0.0)
    for hd in range(N_HEADS):
        r0 = hd * HEAD_PAD
        kh = _head_norm_rope(kvt[r0:r0 + QK_NOPE_DIM, :], k_rope, gk_ref, tabt_ref)
        k_ref[:, r0:r0 + HEAD_PAD] = kh.T.astype(BF16)
        vt_ref[r0:r0 + HEAD_PAD, :] = jnp.concatenate([ones_row, kvt[r0 + QK_NOPE_DIM:r0 + HEAD_PAD, :]],
                                                      axis=0).astype(BF16)


def _kv_expand(ckv, krp, p, layer):
    j = layer // 2
    n_sb = DEC_BATCH * KV_LEN // TB
    n_cache = PAST_LEN // TB
    n_pb = N_PROMPT // TB
    lat_blocks = DEC_SEQ // TB

    def tab_idx(i):
        from_cache, _, jj = _kv_block_source(i)
        return (0, 0, jnp.where((i >= n_sb) | from_cache, ROPE_BLOCKS, jj - n_cache))

    def cache_idx(i):
        _, b, jj = _kv_block_source(i)
        return (b, j, jnp.minimum(jj, n_cache - 1), 0)

    def tok_idx(i):
        _, b, jj = _kv_block_source(i)
        latent = n_pb + b * lat_blocks + jnp.maximum(jj - n_cache, 0)
        return (jnp.where(i < n_sb, latent, i - n_sb), 0)

    return pl.pallas_call(
        _kv_expand_kernel,
        grid=(N_KV_ROWS // TB,),
        in_specs=[
            pl.BlockSpec((None, None, TB, KV_RANK), cache_idx),
            pl.BlockSpec((None, None, TB, HEAD_PAD), cache_idx),
            pl.BlockSpec((TB, KV_RANK), tok_idx),
            pl.BlockSpec((TB, HEAD_PAD), tok_idx),
            _layer_spec((QKV_W, KV_RANK), j),
            _layer_spec((HEAD_PAD, 1), j),
            pl.BlockSpec((3, HEAD_PAD, TB), tab_idx),
        ],
        out_specs=[
            pl.BlockSpec((TB, QKV_W), lambda i: (i, 0)),
            pl.BlockSpec((QKV_W, TB), lambda i: (0, i)),
        ],
        out_shape=[
            jax.ShapeDtypeStruct((N_KV_ROWS, QKV_W), BF16),
            jax.ShapeDtypeStruct((QKV_W, N_KV_ROWS), BF16),
        ],
        compiler_params=_cparams(("parallel",)),
        name="mla_kv_expand",
    )(p["cache_ckv"], p["cache_krp"], ckv, krp, p["m_w_ukv_t"], p["m_gk"], p["rope_t"])


def _attn_kernel(qt_ref, k_ref, vt_ref, o_ref, s_a, s_b, *, t_k, hps, qps, tq):
    c = (1.0 / math.sqrt(QK_HEAD_DIM)) * math.log2(math.e)
    kc = min(TKC, t_k)
    n_chunks = t_k // kc
    per_it = min(ATTN_CHUNKS_PER_ITER, n_chunks)
    n_it = n_chunks // per_it
    bufs = (s_a, s_b)

    units = [(qb, h) for qb in range(qps) for h in range(hps)]

    def rows(h):
        return slice(h * HEAD_PAD, (h + 1) * HEAD_PAD)

    def cols(qb):
        return slice(qb * tq, (qb + 1) * tq)

    def scores(u, off, m8):
        qb, h = units[u]
        st = _dot(k_ref[pl.ds(off, kc), rows(h)], qt_ref[rows(h), cols(qb)])
        bufs[u % 2][pl.ds(off, kc), :] = st
        return jnp.maximum(m8, jnp.max(st.reshape(kc // SUBLANE, SUBLANE, tq), axis=0))

    def weigh(u, off, m, acc):
        pt = jnp.exp2((bufs[u % 2][pl.ds(off, kc), :] - m) * c).astype(BF16)
        return acc + _dot(vt_ref[rows(units[u][1]), pl.ds(off, kc)], pt)

    def phase(u_scores, u_weigh, m):
        def body(it, carry):
            m8, acc = carry
            for j in range(per_it):
                off = (it * per_it + j) * kc
                off = off if isinstance(off, int) else pl.multiple_of(off, kc)
                if u_scores is not None:
                    m8 = scores(u_scores, off, m8)
                if u_weigh is not None:
                    acc = weigh(u_weigh, off, m, acc)
            return m8, acc

        init = (jnp.full((SUBLANE, tq), NEG_BIG, F32), jnp.zeros((HEAD_PAD, tq), F32))
        return body(0, init) if n_it == 1 else lax.fori_loop(0, n_it, body, init)

    outs = []
    if n_chunks == 1:
        sts = [_dot(k_ref[:, rows(h)], qt_ref[rows(h), cols(qb)]) for qb, h in units]
        for (qb, h), st in zip(units, sts):
            m = jnp.max(st, axis=0, keepdims=True)
            acc = _dot(vt_ref[rows(h), :], jnp.exp2((st - m) * c).astype(BF16))
            outs.append(acc[QK_NOPE_DIM:, :] / acc[0:1, :])
    else:
        m8, _ = phase(0, None, None)
        for u in range(1, len(units) + 1):
            m = jnp.max(m8, axis=0, keepdims=True)
            m8, acc = phase(u if u < len(units) else None, u - 1, m)
            outs.append(acc[QK_NOPE_DIM:, :] / acc[0:1, :])
    for qb in range(qps):
        o_ref[cols(qb), :] = jnp.concatenate(outs[qb * hps:(qb + 1) * hps], axis=0).T.astype(BF16)


def _attention(qt, k, vt, *, n_batch, t_q, t_k, q_row0, kv_row0, hps, qps, tq_unit):
    tq = tq_unit * qps
    nq = t_q // tq
    q0 = q_row0 // tq
    k0 = kv_row0 // t_k
    return pl.pallas_call(
        functools.partial(_attn_kernel, t_k=t_k, hps=hps, qps=qps, tq=tq_unit),
        grid=(n_batch, N_HEADS // hps, nq),
        in_specs=[
            pl.BlockSpec((hps * HEAD_PAD, tq), lambda b, h, i: (h, q0 + b * nq + i)),
            pl.BlockSpec((t_k, hps * HEAD_PAD), lambda b, h, i: (k0 + b, h)),
            pl.BlockSpec((hps * HEAD_PAD, t_k), lambda b, h, i: (h, k0 + b)),
        ],
        out_specs=pl.BlockSpec((tq, hps * V_HEAD_DIM), lambda b, h, i: (b * nq + i, h)),
        out_shape=jax.ShapeDtypeStruct((n_batch * t_q, N_HEADS * V_HEAD_DIM), BF16),
        scratch_shapes=[pltpu.VMEM((t_k, tq_unit), F32), pltpu.VMEM((t_k, tq_unit), F32)],
        compiler_params=_cparams(("parallel", "parallel", "parallel")),
        name=f"mla_attention_tk{t_k}",
    )(qt, k, vt)


def _out_proj_kernel(ap_ref, as_ref, xp_ref, xs_ref, mod_ref, wo_ref, o_ref):
    is_prompt = pl.program_id(0) < N_PROMPT // TB
    a = jnp.where(is_prompt, ap_ref[...], as_ref[...])
    x = jnp.where(is_prompt, xp_ref[...], xs_ref[...])
    o_ref[...] = x + mod_ref[0][2:3] * _dot(a, wo_ref[...])


def _out_proj(attn_p, attn_s, x_p, x_s, p, layer):
    return pl.pallas_call(
        _out_proj_kernel,
        grid=(N_TOK // TB,),
        in_specs=_split_rows_specs() + _split_rows_specs() + [
            _mod_spec(TB, layer),
            _layer_spec((D_MODEL, D_MODEL), layer // 2),
        ],
        out_specs=pl.BlockSpec((TB, D_MODEL), lambda i: (i, 0)),
        out_shape=jax.ShapeDtypeStruct((N_TOK, D_MODEL), F32),
        compiler_params=_cparams(("parallel",)),
        name="mla_out_proj",
    )(attn_p, attn_s, x_p, x_s, p["mod"], p["m_w_o"])


def _mla_layer(x_p, x_s, p, layer):
    qt, ckv, krp = _mla_proj(x_p, x_s, p, layer)
    k, vt = _kv_expand(ckv, krp, p, layer)
    a_p = _attention(qt, k, vt, n_batch=BATCH, t_q=SEQ, t_k=SEQ, q_row0=0, kv_row0=DEC_BATCH * KV_LEN,
                     hps=N_HEADS, qps=1, tq_unit=SEQ)
    a_s = _attention(qt, k, vt, n_batch=DEC_BATCH, t_q=DEC_SEQ, t_k=KV_LEN, q_row0=N_PROMPT, kv_row0=0,
                     hps=4, qps=2, tq_unit=TQ)
    x = _out_proj(a_p, a_s, x_p, x_s, p, layer)
    state_ckv = ckv[:N_PROMPT].reshape(BATCH, SEQ, KV_RANK)
    state_krope = krp[:N_PROMPT, QK_NOPE_DIM:QK_HEAD_DIM].reshape(BATCH, SEQ, QK_ROPE_DIM)
    return x, state_ckv, state_krope


def _tile_plan(cnt8, plan_ref, start_ref):
    tm = float(TM)
    cnt_row = cnt8[0:1, :]
    cnt_col = cnt8.T[:, 0:1]
    tiles_row = jnp.floor((cnt_row + (tm - 1.0)) * (1.0 / tm))
    tiles_col = jnp.floor((cnt_col + (tm - 1.0)) * (1.0 / tm))
    sub = lax.broadcasted_iota(jnp.int32, (LANE, LANE), 0).astype(F32)
    lan = lax.broadcasted_iota(jnp.int32, (LANE, LANE), 1).astype(F32)
    tile_end_row = jnp.sum(jnp.where(sub <= lan, tiles_col, 0.0), axis=0, keepdims=True)
    tile_end_col = jnp.sum(jnp.where(lan <= sub, tiles_row, 0.0), axis=1, keepdims=True)
    n_used = jnp.max(tile_end_row, axis=1, keepdims=True)
    start_col = (tile_end_col - tiles_col) * tm
    end_col = start_col + cnt_col
    cand = jnp.where(jnp.logical_and(lan > sub, tiles_row > 0.0), lan, float(LANE))
    next_col = jnp.min(cand, axis=1, keepdims=True)
    next_col = jnp.where(next_col < float(LANE), next_col, -1.0)
    n_lanes = plan_ref.shape[1]
    tidx = jnp.minimum(lax.broadcasted_iota(jnp.int32, (LANE, n_lanes), 1).astype(F32), n_used - 1.0)
    esub = lax.broadcasted_iota(jnp.int32, (LANE, n_lanes), 0).astype(F32)
    te_row = jnp.sum(jnp.where(tile_end_col <= tidx, 1.0, 0.0), axis=0, keepdims=True)
    mine = esub == te_row
    end_at = jnp.sum(jnp.where(mine, end_col, 0.0), axis=0, keepdims=True)
    tv_row = jnp.clip(end_at - tidx[0:1, :] * tm, 0.0, tm)
    nx_row = jnp.sum(jnp.where(mine, next_col, 0.0), axis=0, keepdims=True)
    nu_row = jnp.broadcast_to(n_used, (1, n_lanes))
    plan_ref[...] = jnp.concatenate([te_row, tv_row, nx_row, nu_row, jnp.zeros((SUBLANE - 4, n_lanes), F32)], axis=0)
    start_ref[...] = jnp.broadcast_to(start_col, (LANE, LANE))


def _route_kernel(x_ref, mod_ref, g_ref, wrt_ref, brt_ref, triu_ref, h_ref, metat_ref, plan_ref, start_ref, carry):
    i = pl.program_id(0)

    @pl.when(i == 0)
    def _():
        carry[...] = jnp.zeros_like(carry)

    x = x_ref[...]
    m = mod_ref[0]
    h = _rms(x, g_ref[...]) * (1.0 + m[4:5]) + m[3:4]
    h_ref[...] = _pack_bf16_pairs(h)
    logits = _dot_nt(wrt_ref[...], h.astype(BF16)) + brt_ref[...]
    n_tok = logits.shape[1]
    esub = lax.broadcasted_iota(jnp.int32, logits.shape, 0).astype(F32)
    work = logits
    sel = jnp.zeros(logits.shape, F32)
    hits, firsts, tops = [], [], []
    for k in range(TOP_K):
        mk = jnp.max(work, axis=0, keepdims=True)
        first = jnp.min(jnp.where(work == mk, esub, float(N_EXPERTS)), axis=0, keepdims=True)
        hit = esub == first
        sel = jnp.where(hit, 1.0, sel)
        work = jnp.where(hit, -jnp.inf, work)
        hits.append(hit)
        firsts.append(first)
        tops.append(mk)
    es = [jnp.exp(t - tops[0]) for t in tops]
    denom = es[0] + es[1] + es[2] + es[3]
    pos = _dot(sel.astype(BF16), triu_ref[...]) + carry[:, 0:1]
    carry[...] = carry[...] + jnp.sum(sel, axis=1, keepdims=True)

    @pl.when(i == pl.num_programs(0) - 1)
    def _():
        counts = jnp.concatenate([carry[...], jnp.zeros((LANE - N_EXPERTS, LANE), F32)], axis=0).T
        _tile_plan(counts[:SUBLANE, :], plan_ref, start_ref)

    ranks = [jnp.sum(jnp.where(hit, pos, 0.0), axis=0, keepdims=True) for hit in hits]
    gates = [e / denom for e in es]
    metat_ref[...] = jnp.concatenate(firsts + gates + ranks + [jnp.zeros((SUBLANE // 2, n_tok), F32)], axis=0)


def _route(x, p, layer):
    tb = ROUTE_TB
    return pl.pallas_call(
        _route_kernel,
        grid=(N_TOK // tb,),
        in_specs=[
            pl.BlockSpec((tb, D_MODEL), lambda i: (i, 0)),
            _mod_spec(tb, layer),
            _layer_spec((1, D_MODEL), layer),
            _layer_spec((N_EXPERTS, D_MODEL), layer),
            _layer_spec((N_EXPERTS, 1), layer),
            _const_spec((tb, tb)),
        ],
        out_specs=[
            pl.BlockSpec((tb, D_MODEL // 2), lambda i: (i, 0)),
            pl.BlockSpec((2 * SUBLANE, tb), lambda i: (0, i)),
            _const_spec((SUBLANE, PLAN_LANES)),
            _const_spec((LANE, LANE)),
        ],
        out_shape=[
            jax.ShapeDtypeStruct((N_TOK, D_MODEL // 2), jnp.uint32),
            jax.ShapeDtypeStruct((2 * SUBLANE, N_TOK), F32),
            jax.ShapeDtypeStruct((SUBLANE, PLAN_LANES), F32),
            jax.ShapeDtypeStruct((LANE, LANE), F32),
        ],
        scratch_shapes=[pltpu.VMEM((N_EXPERTS, LANE), F32)],
        compiler_params=_cparams(("arbitrary",)),
        name="moe_route",
    )(x, p["mod"], p["norm_ffn_g"], p["e_w_router_t"], p["e_b_router"], p["triu"])


def _slots_kernel(start_ref, metat_ref, dest_ref):
    start_col = start_ref[:, 0:1]
    esub = lax.broadcasted_iota(jnp.int32, (LANE, SLOT_TB), 0).astype(F32)
    rows = []
    for k in range(TOP_K):
        e = metat_ref[k:k + 1, :]
        first = jnp.sum(jnp.where(esub == e, start_col, 0.0), axis=0, keepdims=True)
        rows.append(first + metat_ref[2 * TOP_K + k:2 * TOP_K + k + 1, :])
    dest_ref[...] = jnp.concatenate(rows, axis=0).astype(jnp.int32)


def _slots(start, meta_t):
    return pl.pallas_call(
        _slots_kernel,
        grid=(N_TOK // SLOT_TB,),
        in_specs=[
            _const_spec((LANE, LANE)),
            pl.BlockSpec((2 * SUBLANE, SLOT_TB), lambda i: (0, i)),
        ],
        out_specs=pl.BlockSpec((TOP_K, SLOT_TB), lambda i: (0, i)),
        out_shape=jax.ShapeDtypeStruct((TOP_K, N_TOK), jnp.int32),
        compiler_params=_cparams(("parallel",)),
        name="moe_slots",
    )(start, meta_t)


def _sc_gather(table, idx, ch):
    b, w = idx.shape[0], table.shape[1]
    per_w = b // SC_WORKERS
    n_ch = per_w // ch
    assert per_w * SC_WORKERS == b and n_ch * ch == per_w and n_ch % 2 == 0
    mesh = plsc.VectorSubcoreMesh(core_axis_name="c", subcore_axis_name="s")

    @functools.partial(
        pl.kernel, mesh=mesh,
        out_type=jax.ShapeDtypeStruct((b, w), table.dtype),
        scratch_types=[
            pltpu.VMEM((n_ch, ch), jnp.int32),
            pltpu.VMEM((ch, w), table.dtype),
            pltpu.VMEM((ch, w), table.dtype),
            pltpu.SemaphoreType.DMA, pltpu.SemaphoreType.DMA,
            pltpu.SemaphoreType.DMA, pltpu.SemaphoreType.DMA,
        ],
        name="sc_row_gather",
    )
    def gather_rows(table_hbm, idx_hbm, out_hbm, idx_v, buf0, buf1, g0, g1, s0, s1):
        wid = lax.axis_index("s") * SC_CORES + lax.axis_index("c")
        base = wid * per_w
        pltpu.sync_copy(idx_hbm.at[wid], idx_v)

        def gather(j, buf, sem):
            return pltpu.make_async_copy(table_hbm.at[idx_v.at[j]], buf, sem)

        def store(j, buf, sem):
            return pltpu.make_async_copy(buf, out_hbm.at[pl.ds(base + j * ch, ch)], sem)

        gather(0, buf0, g0).start()

        @pl.loop(0, n_ch, step=2)
        def _(j):
            @pl.when(j > 0)
            def _():
                store(j - 1, buf1, s1).wait()

            gather(j + 1, buf1, g1).start()
            gather(j, buf0, g0).wait()
            store(j, buf0, s0).start()
            gather(j + 1, buf1, g1).wait()
            store(j + 1, buf1, s1).start()
            store(j, buf0, s0).wait()

            @pl.when(j + 2 < n_ch)
            def _():
                gather(j + 2, buf0, g0).start()

        store(n_ch - 1, buf1, s1).wait()

    return gather_rows(table, idx.reshape(SC_WORKERS, n_ch, ch))


def _sc_dispatch(rows, dest_t, ch):
    n, w = rows.shape
    per_w = n // SC_WORKERS
    n_ch = per_w // ch
    assert per_w * SC_WORKERS == n and n_ch * ch == per_w and n_ch % 2 == 0
    mesh = plsc.VectorSubcoreMesh(core_axis_name="c", subcore_axis_name="s")
    idx = dest_t.reshape(TOP_K, SC_WORKERS, n_ch, ch)

    @functools.partial(
        pl.kernel, mesh=mesh,
        out_type=jax.ShapeDtypeStruct((N_SLOTS, w), rows.dtype),
        scratch_types=[
            pltpu.VMEM((TOP_K * n_ch, ch), jnp.int32),
            pltpu.VMEM((ch, w), rows.dtype),
            pltpu.VMEM((ch, w), rows.dtype),
            pltpu.SemaphoreType.DMA, pltpu.SemaphoreType.DMA,
            pltpu.SemaphoreType.DMA, pltpu.SemaphoreType.DMA,
        ],
        name="sc_row_dispatch",
    )
    def dispatch_rows(rows_hbm, idx_hbm, out_hbm, idx_v, buf0, buf1, l0, l1, s0, s1):
        wid = lax.axis_index("s") * SC_CORES + lax.axis_index("c")
        base = wid * per_w
        for k in range(TOP_K):
            pltpu.sync_copy(idx_hbm.at[k, wid], idx_v.at[pl.ds(k * n_ch, n_ch)])

        def load(j, buf, sem):
            return pltpu.make_async_copy(rows_hbm.at[pl.ds(base + j * ch, ch)], buf, sem)

        def scatter(j, k, buf, sem):
            return pltpu.make_async_copy(buf, out_hbm.at[idx_v.at[k * n_ch + j]], sem)

        load(0, buf0, l0).start()

        @pl.loop(0, n_ch, step=2)
        def _(j):
            load(j + 1, buf1, l1).start()
            load(j, buf0, l0).wait()
            for k in range(TOP_K):
                scatter(j, k, buf0, s0).start()
            load(j + 1, buf1, l1).wait()
            for k in range(TOP_K):
                scatter(j + 1, k, buf1, s1).start()
            for k in range(TOP_K):
                scatter(j, k, buf0, s0).wait()

            @pl.when(j + 2 < n_ch)
            def _():
                load(j + 2, buf0, l0).start()

            for k in range(TOP_K):
                scatter(j + 1, k, buf1, s1).wait()

    return dispatch_rows(rows, idx)


def _deinterleave_matrix():
    src = jnp.arange(2 * LANE)[:, None]
    dst = jnp.arange(2 * LANE)[None, :]
    want = jnp.where(dst < LANE, 2 * dst, 2 * (dst - LANE) + 1)
    return (src == want).astype(BF16)


def _expert_kernel(te_ref, nu_ref, tv_ref, nx_ref, x_ref, wgu_hbm, bgu_ref, wd_hbm, bd_ref, perm_ref, o_ref,
                   wgu_st, wd_st, wgu_bf, wd_bf, sems, *, layer):
    i = pl.program_id(0)
    prev = te_ref[jnp.maximum(i - 1, 0)]
    fresh = jnp.logical_or(i == 0, te_ref[i] != prev)

    def fetch(e):
        return (pltpu.make_async_copy(wgu_hbm.at[layer, e], wgu_st, sems.at[0]),
                pltpu.make_async_copy(wd_hbm.at[layer, e], wd_st, sems.at[1]))

    @pl.when(i == 0)
    def _():
        for cp in fetch(te_ref[0]):
            cp.start()

    @pl.when(jnp.logical_and(fresh, i < nu_ref[0]))
    def _():
        for cp in fetch(te_ref[i]):
            cp.wait()
        for b in range(2 * D_FF // (2 * LANE)):
            sl = slice(b * 2 * LANE, (b + 1) * 2 * LANE)
            wgu_bf[:, sl] = _dot(wgu_st[:, sl].astype(BF16), perm_ref[...]).astype(BF16)
        wd_bf[...] = wd_st[...].astype(BF16)

        @pl.when(nx_ref[i] >= 0)
        def _():
            for cp in fetch(nx_ref[i]):
                cp.start()

    def ffn(n_rows):
        row = lax.broadcasted_iota(jnp.int32, (n_rows, D_MODEL // 2), 0)
        w = jnp.where(row < tv_ref[i], x_ref[:n_rows, :], jnp.uint32(0))
        x = _unpack_bf16_pairs(w).astype(BF16)
        gu = _dot(x, wgu_bf[...]) + bgu_ref[...]
        acts = []
        for b in range(D_FF // LANE):
            glu = jnp.minimum(gu[:, b * 2 * LANE:b * 2 * LANE + LANE], SWIGLU_LIMIT)
            lin = jnp.clip(gu[:, b * 2 * LANE + LANE:(b + 1) * 2 * LANE], -SWIGLU_LIMIT, SWIGLU_LIMIT)
            acts.append((glu * jax.nn.sigmoid(SWIGLU_ALPHA * glu) * (lin + 1.0)).astype(BF16))
        act = jnp.concatenate(acts, axis=1)
        o_ref[:n_rows, :] = _pack_bf16_pairs(_dot(act, wd_bf[...]) + bd_ref[...])

    in_use = i < nu_ref[0]

    @pl.when(jnp.logical_and(in_use, tv_ref[i] > TM // 2))
    def _():
        ffn(TM)

    @pl.when(jnp.logical_and(in_use, tv_ref[i] <= TM // 2))
    def _():
        ffn(TM // 2)


def _experts(buf, tile_expert, n_used, tile_valid, tile_next, p, layer):
    def row_idx(i, te, nu, tv, nx):
        return (jnp.minimum(i, nu[0] - 1), 0)

    def b_idx(i, te, nu, tv, nx):
        return (layer, te[i], 0, 0)

    grid_spec = pltpu.PrefetchScalarGridSpec(
        num_scalar_prefetch=4,
        grid=(N_TILES,),
        in_specs=[
            pl.BlockSpec((TM, D_MODEL // 2), row_idx),
            pl.BlockSpec(memory_space=pl.ANY),
            pl.BlockSpec((None, None, 1, 2 * D_FF), b_idx),
            pl.BlockSpec(memory_space=pl.ANY),
            pl.BlockSpec((None, None, 1, D_MODEL), b_idx),
            _const_spec((2 * LANE, 2 * LANE)),
        ],
        out_specs=pl.BlockSpec((TM, D_MODEL // 2), row_idx),
        scratch_shapes=[
            pltpu.VMEM((D_MODEL, 2 * D_FF), F32),
            pltpu.VMEM((D_FF, D_MODEL), F32),
            pltpu.VMEM((D_MODEL, 2 * D_FF), BF16),
            pltpu.VMEM((D_FF, D_MODEL), BF16),
            pltpu.SemaphoreType.DMA((2,)),
        ],
    )
    return pl.pallas_call(
        functools.partial(_expert_kernel, layer=layer),
        grid_spec=grid_spec,
        out_shape=jax.ShapeDtypeStruct((N_SLOTS, D_MODEL // 2), jnp.uint32),
        compiler_params=_cparams(("arbitrary",)),
        name="moe_experts",
    )(tile_expert, n_used, tile_valid, tile_next, buf, p["e_w_gu"], p["e_b_gu"], p["e_w_down"], p["e_b_down"],
      p["deinterleave"])


def _pack_bf16_pairs(v):
    half = v.shape[1] // 2
    bits = pltpu.bitcast(v.astype(BF16).astype(F32), jnp.uint32)
    return (bits[:, half:] & jnp.uint32(0xFFFF0000)) | (bits[:, :half] >> 16)


def _unpack_bf16_pairs(w):
    return jnp.concatenate([pltpu.bitcast(w << 16, F32), pltpu.bitcast(w & jnp.uint32(0xFFFF0000), F32)],
                           axis=1)


def _combine_kernel(x_ref, mod_ref, y_ref, mt_ref, o_ref):
    w = mt_ref[...].T[:, TOP_K:2 * TOP_K]
    y = _unpack_bf16_pairs(y_ref[0]) * w[:, 0:1]
    for k in range(1, TOP_K):
        y = y + _unpack_bf16_pairs(y_ref[k]) * w[:, k:k + 1]
    o_ref[...] = x_ref[...] + mod_ref[0][5:6] * y


def _combine(x, y4, meta_t, p, layer, part):
    n_rows = N_TOK // MOE_PARTS
    tb = TB
    first = part * n_rows // tb
    return pl.pallas_call(
        _combine_kernel,
        grid=(n_rows // tb,),
        in_specs=[
            pl.BlockSpec((tb, D_MODEL), lambda i: (i + first, 0)),
            _mod_spec(tb, layer, first),
            pl.BlockSpec((TOP_K, tb, D_MODEL // 2), lambda i: (0, i, 0)),
            pl.BlockSpec((2 * SUBLANE, tb), lambda i: (0, i + first)),
        ],
        out_specs=pl.BlockSpec((tb, D_MODEL), lambda i: (i, 0)),
        out_shape=jax.ShapeDtypeStruct((n_rows, D_MODEL), F32),
        compiler_params=_cparams(("parallel",)),
        name="moe_combine",
    )(x, p["mod"], y4, meta_t)


def _moe_layer(x, p, layer):
    hp, meta_t, plan, start = _route(x, p, layer)
    plan = plan[:4, :N_TILES].astype(jnp.int32)
    tile_expert, tile_valid, tile_next, n_used = plan[0], plan[1], plan[2], plan[3, :1]
    dest_t = _slots(start, meta_t)
    buf = _sc_dispatch(hp, dest_t, SC_CHUNK_ROWS)
    yb = _experts(buf, tile_expert, n_used, tile_valid, tile_next, p, layer)
    n_rows = N_TOK // MOE_PARTS
    outs = []
    for part in range(MOE_PARTS):
        idx = dest_t[:, part * n_rows:(part + 1) * n_rows].reshape(-1)
        y4 = _sc_gather(yb, idx, SC_CHUNK_ROWS).reshape(TOP_K, n_rows, D_MODEL // 2)
        outs.append(_combine(x, y4, meta_t, p, layer, part))
    return outs


def _prepare(c, cache_ckv, cache_krope, c_ctx, norm_mix_g, norm_ffn_g, w_mod, b_mod, g_w_in, g_b_in, g_norm_v,
             g_w_s, g_b_s, g_w_out, m_w_dq, m_norm_q, m_w_uq, m_w_dkv, m_norm_kv, m_w_ukv, m_qk_norm_q,
             m_qk_norm_k, m_w_o, e_w_router, e_b_router, e_w_gu, e_b_gu, e_w_down, e_b_down):
    n_mla = m_w_dq.shape[0]
    cond = jnp.concatenate([c_ctx[None, :], c, jnp.zeros((SUBLANE - N_COND, D_MODEL), F32)], axis=0)
    wdkv = jnp.concatenate([m_w_dkv[..., :KV_RANK], jnp.zeros((n_mla, D_MODEL, QK_NOPE_DIM), F32),
                            m_w_dkv[..., KV_RANK:], jnp.zeros((n_mla, D_MODEL, HEAD_PAD - QK_HEAD_DIM), F32)],
                           axis=-1)
    w_uq = jnp.pad(m_w_uq.reshape(n_mla, Q_RANK, N_HEADS, QK_HEAD_DIM),
                   ((0, 0), (0, 0), (0, 0), (0, HEAD_PAD - QK_HEAD_DIM))).reshape(n_mla, Q_RANK, QKV_W)

    def gain_col(g):
        return jnp.pad(g, ((0, 0), (0, HEAD_PAD - QK_HEAD_DIM)))[:, :, None]

    return {
        "mod": _modulation(cond, w_mod, b_mod),
        "rope_t": _rope_tables(),
        "norm_mix_g": norm_mix_g[:, None, :],
        "norm_ffn_g": norm_ffn_g[:, None, :],
        "g_w_in": g_w_in.astype(BF16),
        "g_b_in": g_b_in[:, None, :],
        "g_norm_v": g_norm_v[:, None, :],
        "g_w_s": g_w_s.astype(BF16),
        "g_b_st": jnp.swapaxes(g_b_s, 1, 2),
        "g_w_out": g_w_out.astype(BF16),
        "m_w_dq": m_w_dq.astype(BF16),
        "m_norm_q": m_norm_q[:, None, :],
        "m_w_uq_t": jnp.swapaxes(w_uq, 1, 2).astype(BF16),
        "m_gq": gain_col(m_qk_norm_q),
        "m_w_dkv": wdkv.astype(BF16),
        "m_norm_kv": m_norm_kv[:, None, :],
        "m_w_ukv_t": jnp.swapaxes(m_w_ukv, 1, 2).astype(BF16),
        "m_gk": gain_col(m_qk_norm_k),
        "m_w_o": m_w_o.astype(BF16),
        "cache_ckv": cache_ckv,
        "cache_krp": jnp.pad(cache_krope, ((0, 0), (0, 0), (0, 0), (QK_NOPE_DIM, HEAD_PAD - QK_HEAD_DIM))),
        "e_w_router_t": jnp.swapaxes(e_w_router, 1, 2).astype(BF16),
        "e_b_router": e_b_router[:, :, None],
        "triu": jnp.tri(ROUTE_TB, ROUTE_TB, -1, dtype=BF16).T,
        "e_w_gu": e_w_gu,
        "e_b_gu": e_b_gu.reshape(DEPTH, N_EXPERTS, D_FF // LANE, LANE, 2).swapaxes(3, 4).reshape(
            DEPTH, N_EXPERTS, 1, 2 * D_FF),
        "e_w_down": e_w_down,
        "e_b_down": e_b_down[:, :, None, :],
        "deinterleave": _deinterleave_matrix(),
    }


def kernel(x_prompt, x_sample, c, cache_ckv, cache_krope, c_ctx, norm_mix_g, norm_ffn_g, w_mod, b_mod,
           g_w_in, g_b_in, g_norm_v, g_w_s, g_b_s, g_w_out, m_w_dq, m_norm_q, m_w_uq, m_w_dkv,
           m_norm_kv, m_w_ukv, m_qk_norm_q, m_qk_norm_k, m_w_o, e_w_router, e_b_router, e_w_gu,
           e_b_gu, e_w_down, e_b_down):
    p = _prepare(c, cache_ckv, cache_krope, c_ctx, norm_mix_g, norm_ffn_g, w_mod, b_mod, g_w_in, g_b_in,
                 g_norm_v, g_w_s, g_b_s, g_w_out, m_w_dq, m_norm_q, m_w_uq, m_w_dkv, m_norm_kv, m_w_ukv,
                 m_qk_norm_q, m_qk_norm_k, m_w_o, e_w_router, e_b_router, e_w_gu, e_b_gu, e_w_down, e_b_down)
    assert MOE_PARTS == 2 and N_PROMPT == N_SAMPLE
    x_p, x_s = x_prompt.reshape(N_PROMPT, D_MODEL), x_sample.reshape(N_SAMPLE, D_MODEL)
    ckv_states, krope_states = [], []
    for layer in range(DEPTH):
        if layer % 2 == 0:
            x = _gmlp_layer(x_p, x_s, p, layer)
        else:
            x, s_ckv, s_krope = _mla_layer(x_p, x_s, p, layer)
            ckv_states.append(s_ckv)
            krope_states.append(s_krope)
        x_p, x_s = _moe_layer(x, p, layer)
    y_prompt = x_p.reshape(BATCH, SEQ, D_MODEL)
    y_sample = x_s.reshape(DEC_BATCH, DEC_SEQ, D_MODEL)
    return (y_prompt, y_sample, jnp.stack(ckv_states, axis=1), jnp.stack(krope_states, axis=1))
```

```python
import functools
import math

import jax
import jax.numpy as jnp
from jax import lax
from jax.experimental import pallas as pl
from jax.experimental.pallas import tpu as pltpu
from jax.experimental.pallas import tpu_sc as plsc

F32 = jnp.float32
BF16 = jnp.bfloat16

D_MODEL = 1024
BATCH = 32
SEQ = 256
DEPTH = 4
DEC_BATCH = 2
DEC_SEQ = 4096
PAST_LEN = 512
GRID_W = 64
RMS_EPS = 1e-6
GMLP_WIDTH = 2 * D_MODEL
GMLP_GROUPS = 8
GROUP_W = GMLP_WIDTH // GMLP_GROUPS
CHUNK = 128
N_HEADS = 16
QK_NOPE_DIM = 64
QK_ROPE_DIM = 32
QK_HEAD_DIM = QK_NOPE_DIM + QK_ROPE_DIM
V_HEAD_DIM = 64
Q_RANK = 256
KV_RANK = 128
ROPE_THETA = 10000.0
N_EXPERTS = 32
TOP_K = 4
D_FF = D_MODEL
SWIGLU_LIMIT = 7.0
SWIGLU_ALPHA = 1.702

N_PROMPT = BATCH * SEQ
N_SAMPLE = DEC_BATCH * DEC_SEQ
N_TOK = N_PROMPT + N_SAMPLE
N_COND = 1 + DEC_BATCH
KV_LEN = PAST_LEN + DEC_SEQ
N_KV_ROWS = DEC_BATCH * KV_LEN + N_PROMPT

LANE = 128
SUBLANE = 8
HEAD_PAD = LANE
QKV_W = N_HEADS * HEAD_PAD
VMEM_LIMIT = 56 * 1024 * 1024

TB = 512
MOD_TN = 1536
TQ = 512
TKC = 256
ATTN_CHUNKS_PER_ITER = 9
TM = 512
N_TILES = N_TOK * TOP_K // TM + N_EXPERTS
N_SLOTS = N_TILES * TM
PLAN_LANES = -(-N_TILES // LANE) * LANE
SLOT_TB = 2048
ROUTE_TB = 512
MOE_PARTS = 2
SC_CORES = 2
SC_SUBCORES = 16
SC_WORKERS = SC_CORES * SC_SUBCORES
SC_CHUNK_ROWS = 64
ROPE_BLOCKS = DEC_SEQ // TB
NEG_BIG = -1e30


def _cparams(sem):
    return pltpu.CompilerParams(dimension_semantics=sem, vmem_limit_bytes=VMEM_LIMIT)


def _cond_of_block(i, tb):
    n_p = N_PROMPT // tb
    per = DEC_SEQ // tb
    return jnp.where(i < n_p, 0, 1 + (i - n_p) // per)


def _rms(x, g, n=None):
    n = x.shape[-1] if n is None else n
    ss = jnp.sum(x * x, axis=-1, keepdims=True) * (1.0 / n)
    return x * lax.rsqrt(ss + RMS_EPS) * g


def _dot(a, b):
    return jnp.dot(a, b, preferred_element_type=F32)


def _mod_kernel(c_ref, w_ref, b_ref, o_ref):
    c = c_ref[...]
    s = c * jax.nn.sigmoid(c)
    o_ref[0] = _dot(s.astype(BF16), w_ref[0].astype(BF16)) + b_ref[0]


def _modulation(cond, w_mod, b_mod):
    tn = MOD_TN
    out = pl.pallas_call(
        _mod_kernel,
        grid=(DEPTH, 6 * D_MODEL // tn),
        in_specs=[
            pl.BlockSpec((SUBLANE, D_MODEL), lambda l, j: (0, 0)),
            pl.BlockSpec((1, D_MODEL, tn), lambda l, j: (l, 0, j)),
            pl.BlockSpec((1, 1, tn), lambda l, j: (l, 0, j)),
        ],
        out_specs=pl.BlockSpec((1, SUBLANE, tn), lambda l, j: (l, 0, j)),
        out_shape=jax.ShapeDtypeStruct((DEPTH, SUBLANE, 6 * D_MODEL), F32),
        compiler_params=_cparams(("parallel", "parallel")),
        name="adaln_mod",
    )(cond, w_mod, b_mod.reshape(DEPTH, 1, 6 * D_MODEL))
    m = out[:, :N_COND].reshape(DEPTH, N_COND, 6, D_MODEL)
    return jnp.pad(m, ((0, 0), (0, 0), (0, SUBLANE - 6), (0, 0)))


def _mod_spec(tb, layer, first_block=0):
    return pl.BlockSpec((None, 1, SUBLANE, D_MODEL),
                        lambda i: (layer, _cond_of_block(i + first_block, tb), 0, 0))


def _const_spec(shape):
    nd = len(shape)
    return pl.BlockSpec(shape, lambda *_: (0,) * nd)


def _layer_spec(shape, j):
    nd = len(shape)
    return pl.BlockSpec((None,) + tuple(shape), lambda *_: (j,) + (0,) * nd)


def _gelu_tanh(x):
    a = math.sqrt(2.0 / math.pi)
    hx = 0.5 * x
    return hx + hx * jnp.tanh(x * (a + (0.044715 * a) * (x * x)))


def _gmlp_kernel(xp_ref, xs_ref, mod_ref, g_ref, win_ref, bin_ref, gv_ref, ws_ref, bst_ref, wout_ref, o_ref):
    x = jnp.where(pl.program_id(0) < N_PROMPT // TB, xp_ref[...], xs_ref[...])
    m = mod_ref[0]
    h = _rms(x, g_ref[...]) * (1.0 + m[1:2]) + m[0:1]
    hb = h.astype(BF16)
    zv = _gelu_tanh(_dot(hb, win_ref[:, GMLP_WIDTH:]) + bin_ref[:, GMLP_WIDTH:])
    vn = _rms(zv, gv_ref[...]).astype(BF16)
    rows = []
    for c in range(TB // CHUNK):
        cols = []
        for g in range(GMLP_GROUPS):
            blk = vn[c * CHUNK:(c + 1) * CHUNK, g * GROUP_W:(g + 1) * GROUP_W]
            cols.append(_dot(ws_ref[g], blk) + bst_ref[:, g:g + 1])
        rows.append(jnp.concatenate(cols, axis=1))
    vm = jnp.concatenate(rows, axis=0)
    u = _gelu_tanh(_dot(hb, win_ref[:, :GMLP_WIDTH]) + bin_ref[:, :GMLP_WIDTH])
    d = _dot((u * vm).astype(BF16), wout_ref[...])
    o_ref[...] = x + m[2:3] * d


def _split_rows_specs():
    n_pb = N_PROMPT // TB
    return [pl.BlockSpec((TB, D_MODEL), lambda i: (jnp.minimum(i, n_pb - 1), 0)),
            pl.BlockSpec((TB, D_MODEL), lambda i: (jnp.maximum(i - n_pb, 0), 0))]


def _gmlp_layer(x_p, x_s, p, layer):
    j = layer // 2
    return pl.pallas_call(
        _gmlp_kernel,
        grid=(N_TOK // TB,),
        in_specs=_split_rows_specs() + [
            _mod_spec(TB, layer),
            _layer_spec((1, D_MODEL), layer),
            _layer_spec((D_MODEL, 2 * GMLP_WIDTH), j),
            _layer_spec((1, 2 * GMLP_WIDTH), j),
            _layer_spec((1, GMLP_WIDTH), j),
            _layer_spec((GMLP_GROUPS, CHUNK, CHUNK), j),
            _layer_spec((CHUNK, GMLP_GROUPS), j),
            _layer_spec((GMLP_WIDTH, D_MODEL), j),
        ],
        out_specs=pl.BlockSpec((TB, D_MODEL), lambda i: (i, 0)),
        out_shape=jax.ShapeDtypeStruct((N_TOK, D_MODEL), F32),
        compiler_params=_cparams(("parallel",)),
        name="gmlp_mixer",
    )(x_p, x_s, p["mod"], p["norm_mix_g"], p["g_w_in"], p["g_b_in"], p["g_norm_v"], p["g_w_s"], p["g_b_st"],
      p["g_w_out"])


def _rope_tables():
    t = jnp.arange(DEC_SEQ)
    row_id = (t // GRID_W).astype(F32)
    col_id = (t % GRID_W).astype(F32)
    axis_dim = QK_ROPE_DIM // 2
    inv_freq = ROPE_THETA ** (-jnp.arange(0, axis_dim, 2, dtype=F32) / axis_dim)
    ang = jnp.stack([row_id[:, None] * inv_freq, col_id[:, None] * inv_freq], axis=1)
    cos, sin = jnp.cos(ang), jnp.sin(ang)
    zeros = jnp.zeros_like(sin)
    cos_l = jnp.concatenate([cos, cos], axis=-1).reshape(DEC_SEQ, QK_ROPE_DIM)
    s1_l = jnp.concatenate([-sin, zeros], axis=-1).reshape(DEC_SEQ, QK_ROPE_DIM)
    s2_l = jnp.concatenate([zeros, sin], axis=-1).reshape(DEC_SEQ, QK_ROPE_DIM)

    def widen(rope_part, nope_fill):
        left = jnp.full((DEC_SEQ, QK_NOPE_DIM), nope_fill, F32)
        right = jnp.zeros((DEC_SEQ, HEAD_PAD - QK_HEAD_DIM), F32)
        return jnp.concatenate([left, rope_part, right], axis=-1)

    pos = jnp.stack([widen(cos_l, 1.0), widen(s1_l, 0.0), widen(s2_l, 0.0)])
    ident_c = jnp.concatenate([jnp.ones((TB, QK_HEAD_DIM), F32),
                               jnp.zeros((TB, HEAD_PAD - QK_HEAD_DIM), F32)], axis=-1)
    ident = jnp.stack([ident_c, jnp.zeros_like(ident_c), jnp.zeros_like(ident_c)])
    return jnp.swapaxes(jnp.concatenate([pos, ident], axis=1), 1, 2)


def _dot_nt(a, b):
    return lax.dot_general(a, b, (((1,), (1,)), ((), ())), preferred_element_type=F32)


def _shift_rows(x, n):
    n = n % x.shape[0]
    return jnp.concatenate([x[n:], x[:n]], axis=0)


def _head_norm_rope(nope, rope, g_ref, tabt_ref):
    half = QK_ROPE_DIM // 4
    lo, hi = QK_NOPE_DIM, QK_HEAD_DIM
    ss = (jnp.sum(nope * nope, axis=0, keepdims=True) + jnp.sum(rope * rope, axis=0, keepdims=True))
    rs = lax.rsqrt(ss * (1.0 / QK_HEAD_DIM) + RMS_EPS)
    xn = nope * rs * g_ref[:lo, :]
    xr = rope * rs * g_ref[lo:hi, :]
    rot = (xr * tabt_ref[0, lo:hi, :] + _shift_rows(xr, half) * tabt_ref[1, lo:hi, :]
           + _shift_rows(xr, -half) * tabt_ref[2, lo:hi, :])
    return jnp.concatenate([xn, rot, jnp.zeros((HEAD_PAD - hi, nope.shape[1]), F32)], axis=0)


def _mla_proj_kernel(xp_ref, xs_ref, mod_ref, g_ref, wdq_ref, nq_ref, wuqt_ref, gq_ref, wdkv_ref, nkv_ref,
                     tabt_ref, qt_ref, ckv_ref, krp_ref):
    x = jnp.where(pl.program_id(0) < N_PROMPT // TB, xp_ref[...], xs_ref[...])
    m = mod_ref[0]
    h = _rms(x, g_ref[...]) * (1.0 + m[1:2]) + m[0:1]
    hb = h.astype(BF16)
    cq = _rms(_dot(hb, wdq_ref[...]), nq_ref[...])
    qt = _dot_nt(wuqt_ref[...], cq.astype(BF16))
    for hd in range(N_HEADS):
        r0 = hd * HEAD_PAD
        qh = _head_norm_rope(qt[r0:r0 + QK_NOPE_DIM, :], qt[r0 + QK_NOPE_DIM:r0 + QK_HEAD_DIM, :], gq_ref, tabt_ref)
        qt_ref[r0:r0 + HEAD_PAD, :] = qh.astype(BF16)
    kva = _dot(hb, wdkv_ref[...])
    ckv_ref[...] = _rms(kva[:, :KV_RANK], nkv_ref[...])
    krp_ref[...] = kva[:, KV_RANK:]


def _mla_proj(x_p, x_s, p, layer):
    j = layer // 2
    n_pb = N_PROMPT // TB

    def tab_idx(i):
        return (0, 0, jnp.where(i < n_pb, ROPE_BLOCKS, (i - n_pb) % ROPE_BLOCKS))

    return pl.pallas_call(
        _mla_proj_kernel,
        grid=(N_TOK // TB,),
        in_specs=_split_rows_specs() + [
            _mod_spec(TB, layer),
            _layer_spec((1, D_MODEL), layer),
            _layer_spec((D_MODEL, Q_RANK), j),
            _layer_spec((1, Q_RANK), j),
            _layer_spec((QKV_W, Q_RANK), j),
            _layer_spec((HEAD_PAD, 1), j),
            _layer_spec((D_MODEL, 2 * LANE), j),
            _layer_spec((1, KV_RANK), j),
            pl.BlockSpec((3, HEAD_PAD, TB), tab_idx),
        ],
        out_specs=[
            pl.BlockSpec((QKV_W, TB), lambda i: (0, i)),
            pl.BlockSpec((TB, KV_RANK), lambda i: (i, 0)),
            pl.BlockSpec((TB, HEAD_PAD), lambda i: (i, 0)),
        ],
        out_shape=[
            jax.ShapeDtypeStruct((QKV_W, N_TOK), BF16),
            jax.ShapeDtypeStruct((N_TOK, KV_RANK), F32),
            jax.ShapeDtypeStruct((N_TOK, HEAD_PAD), F32),
        ],
        compiler_params=_cparams(("parallel",)),
        name="mla_proj",
    )(x_p, x_s, p["mod"], p["norm_mix_g"], p["m_w_dq"], p["m_norm_q"], p["m_w_uq_t"], p["m_gq"], p["m_w_dkv"],
      p["m_norm_kv"], p["rope_t"])


def _kv_block_source(i):
    n_sb = DEC_BATCH * KV_LEN // TB
    per = KV_LEN // TB
    b = jnp.minimum(i // per, DEC_BATCH - 1)
    jj = i % per
    return jnp.logical_and(i < n_sb, jj < PAST_LEN // TB), b, jj


def _kv_expand_kernel(cckv_ref, ckrp_ref, ckv_ref, krp_ref, wukvt_ref, gk_ref, tabt_ref, k_ref, vt_ref):
    from_cache, _, _ = _kv_block_source(pl.program_id(0))
    ckv = jnp.where(from_cache, cckv_ref[...], ckv_ref[...])
    krp = jnp.where(from_cache, ckrp_ref[...], krp_ref[...])
    kvt = _dot_nt(wukvt_ref[...], ckv.astype(BF16))
    k_rope = krp.T[QK_NOPE_DIM:QK_HEAD_DIM, :]
    row = lax.broadcasted_iota(jnp.int32, (QK_NOPE_DIM, TB), 0)
    ones_row = jnp.where(row == 0, 1.0, 0.0)
    for hd in range(N_HEADS):
        r0 = hd * HEAD_PAD
        kh = _head_norm_rope(kvt[r0:r0 + QK_NOPE_DIM, :], k_rope, gk_ref, tabt_ref)
        k_ref[:, r0:r0 + HEAD_PAD] = kh.T.astype(BF16)
        vt_ref[r0:r0 + HEAD_PAD, :] = jnp.concatenate([ones_row, kvt[r0 + QK_NOPE_DIM:r0 + HEAD_PAD, :]],
                                                      axis=0).astype(BF16)


def _kv_expand(ckv, krp, p, layer):
    j = layer // 2
    n_sb = DEC_BATCH * KV_LEN // TB
    n_cache = PAST_LEN // TB
    n_pb = N_PROMPT // TB
    lat_blocks = DEC_SEQ // TB

    def tab_idx(i):
        from_cache, _, jj = _kv_block_source(i)
        return (0, 0, jnp.where((i >= n_sb) | from_cache, ROPE_BLOCKS, jj - n_cache))

    def cache_idx(i):
        _, b, jj = _kv_block_source(i)
        return (b, j, jnp.minimum(jj, n_cache - 1), 0)

    def tok_idx(i):
        _, b, jj = _kv_block_source(i)
        latent = n_pb + b * lat_blocks + jnp.maximum(jj - n_cache, 0)
        return (jnp.where(i < n_sb, latent, i - n_sb), 0)

    return pl.pallas_call(
        _kv_expand_kernel,
        grid=(N_KV_ROWS // TB,),
        in_specs=[
            pl.BlockSpec((None, None, TB, KV_RANK), cache_idx),
            pl.BlockSpec((None, None, TB, HEAD_PAD), cache_idx),
            pl.BlockSpec((TB, KV_RANK), tok_idx),
            pl.BlockSpec((TB, HEAD_PAD), tok_idx),
            _layer_spec((QKV_W, KV_RANK), j),
            _layer_spec((HEAD_PAD, 1), j),
            pl.BlockSpec((3, HEAD_PAD, TB), tab_idx),
        ],
        out_specs=[
            pl.BlockSpec((TB, QKV_W), lambda i: (i, 0)),
            pl.BlockSpec((QKV_W, TB), lambda i: (0, i)),
        ],
        out_shape=[
            jax.ShapeDtypeStruct((N_KV_ROWS, QKV_W), BF16),
            jax.ShapeDtypeStruct((QKV_W, N_KV_ROWS), BF16),
        ],
        compiler_params=_cparams(("parallel",)),
        name="mla_kv_expand",
    )(p["cache_ckv"], p["cache_krp"], ckv, krp, p["m_w_ukv_t"], p["m_gk"], p["rope_t"])


def _attn_kernel(qt_ref, k_ref, vt_ref, o_ref, s_a, s_b, *, t_k, hps, qps, tq):
    c = (1.0 / math.sqrt(QK_HEAD_DIM)) * math.log2(math.e)
    kc = min(TKC, t_k)
    n_chunks = t_k // kc
    per_it = min(ATTN_CHUNKS_PER_ITER, n_chunks)
    n_it = n_chunks // per_it
    bufs = (s_a, s_b)

    units = [(qb, h) for qb in range(qps) for h in range(hps)]

    def rows(h):
        return slice(h * HEAD_PAD, (h + 1) * HEAD_PAD)

    def cols(qb):
        return slice(qb * tq, (qb + 1) * tq)

    def scores(u, off, m8):
        qb, h = units[u]
        st = _dot(k_ref[pl.ds(off, kc), rows(h)], qt_ref[rows(h), cols(qb)])
        bufs[u % 2][pl.ds(off, kc), :] = st
        return jnp.maximum(m8, jnp.max(st.reshape(kc // SUBLANE, SUBLANE, tq), axis=0))

    def weigh(u, off, m, acc):
        pt = jnp.exp2((bufs[u % 2][pl.ds(off, kc), :] - m) * c).astype(BF16)
        return acc + _dot(vt_ref[rows(units[u][1]), pl.ds(off, kc)], pt)

    def phase(u_scores, u_weigh, m):
        def body(it, carry):
            m8, acc = carry
            for j in range(per_it):
                off = (it * per_it + j) * kc
                off = off if isinstance(off, int) else pl.multiple_of(off, kc)
                if u_scores is not None:
                    m8 = scores(u_scores, off, m8)
                if u_weigh is not None:
                    acc = weigh(u_weigh, off, m, acc)
            return m8, acc

        init = (jnp.full((SUBLANE, tq), NEG_BIG, F32), jnp.zeros((HEAD_PAD, tq), F32))
        return body(0, init) if n_it == 1 else lax.fori_loop(0, n_it, body, init)

    outs = []
    if n_chunks == 1:
        sts = [_dot(k_ref[:, rows(h)], qt_ref[rows(h), cols(qb)]) for qb, h in units]
        for (qb, h), st in zip(units, sts):
            m = jnp.max(st, axis=0, keepdims=True)
            acc = _dot(vt_ref[rows(h), :], jnp.exp2((st - m) * c).astype(BF16))
            outs.append(acc[QK_NOPE_DIM:, :] / acc[0:1, :])
    else:
        m8, _ = phase(0, None, None)
        for u in range(1, len(units) + 1):
            m = jnp.max(m8, axis=0, keepdims=True)
            m8, acc = phase(u if u < len(units) else None, u - 1, m)
            outs.append(acc[QK_NOPE_DIM:, :] / acc[0:1, :])
    for qb in range(qps):
        o_ref[cols(qb), :] = jnp.concatenate(outs[qb * hps:(qb + 1) * hps], axis=0).T.astype(BF16)


def _attention(qt, k, vt, *, n_batch, t_q, t_k, q_row0, kv_row0, hps, qps, tq_unit):
    tq = tq_unit * qps
    nq = t_q // tq
    q0 = q_row0 // tq
    k0 = kv_row0 // t_k
    return pl.pallas_call(
        functools.partial(_attn_kernel, t_k=t_k, hps=hps, qps=qps, tq=tq_unit),
        grid=(n_batch, N_HEADS // hps, nq),
        in_specs=[
            pl.BlockSpec((hps * HEAD_PAD, tq), lambda b, h, i: (h, q0 + b * nq + i)),
            pl.BlockSpec((t_k, hps * HEAD_PAD), lambda b, h, i: (k0 + b, h)),
            pl.BlockSpec((hps * HEAD_PAD, t_k), lambda b, h, i: (h, k0 + b)),
        ],
        out_specs=pl.BlockSpec((tq, hps * V_HEAD_DIM), lambda b, h, i: (b * nq + i, h)),
        out_shape=jax.ShapeDtypeStruct((n_batch * t_q, N_HEADS * V_HEAD_DIM), BF16),
        scratch_shapes=[pltpu.VMEM((t_k, tq_unit), F32), pltpu.VMEM((t_k, tq_unit), F32)],
        compiler_params=_cparams(("parallel", "parallel", "parallel")),
        name=f"mla_attention_tk{t_k}",
    )(qt, k, vt)


def _out_proj_kernel(ap_ref, as_ref, xp_ref, xs_ref, mod_ref, wo_ref, o_ref):
    is_prompt = pl.program_id(0) < N_PROMPT // TB
    a = jnp.where(is_prompt, ap_ref[...], as_ref[...])
    x = jnp.where(is_prompt, xp_ref[...], xs_ref[...])
    o_ref[...] = x + mod_ref[0][2:3] * _dot(a, wo_ref[...])


def _out_proj(attn_p, attn_s, x_p, x_s, p, layer):
    return pl.pallas_call(
        _out_proj_kernel,
        grid=(N_TOK // TB,),
        in_specs=_split_rows_specs() + _split_rows_specs() + [
            _mod_spec(TB, layer),
            _layer_spec((D_MODEL, D_MODEL), layer // 2),
        ],
        out_specs=pl.BlockSpec((TB, D_MODEL), lambda i: (i, 0)),
        out_shape=jax.ShapeDtypeStruct((N_TOK, D_MODEL), F32),
        compiler_params=_cparams(("parallel",)),
        name="mla_out_proj",
    )(attn_p, attn_s, x_p, x_s, p["mod"], p["m_w_o"])


def _mla_layer(x_p, x_s, p, layer):
    qt, ckv, krp = _mla_proj(x_p, x_s, p, layer)
    k, vt = _kv_expand(ckv, krp, p, layer)
    a_p = _attention(qt, k, vt, n_batch=BATCH, t_q=SEQ, t_k=SEQ, q_row0=0, kv_row0=DEC_BATCH * KV_LEN,
                     hps=N_HEADS, qps=1, tq_unit=SEQ)
    a_s = _attention(qt, k, vt, n_batch=DEC_BATCH, t_q=DEC_SEQ, t_k=KV_LEN, q_row0=N_PROMPT, kv_row0=0,
                     hps=4, qps=4, tq_unit=TQ)
    x = _out_proj(a_p, a_s, x_p, x_s, p, layer)
    state_ckv = ckv[:N_PROMPT].reshape(BATCH, SEQ, KV_RANK)
    state_krope = krp[:N_PROMPT, QK_NOPE_DIM:QK_HEAD_DIM].reshape(BATCH, SEQ, QK_ROPE_DIM)
    return x, state_ckv, state_krope


def _tile_plan(cnt8, plan_ref, start_ref):
    tm = float(TM)
    cnt_row = cnt8[0:1, :]
    cnt_col = cnt8.T[:, 0:1]
    tiles_row = jnp.floor((cnt_row + (tm - 1.0)) * (1.0 / tm))
    tiles_col = jnp.floor((cnt_col + (tm - 1.0)) * (1.0 / tm))
    sub = lax.broadcasted_iota(jnp.int32, (LANE, LANE), 0).astype(F32)
    lan = lax.broadcasted_iota(jnp.int32, (LANE, LANE), 1).astype(F32)
    tile_end_row = jnp.sum(jnp.where(sub <= lan, tiles_col, 0.0), axis=0, keepdims=True)
    tile_end_col = jnp.sum(jnp.where(lan <= sub, tiles_row, 0.0), axis=1, keepdims=True)
    n_used = jnp.max(tile_end_row, axis=1, keepdims=True)
    start_col = (tile_end_col - tiles_col) * tm
    end_col = start_col + cnt_col
    cand = jnp.where(jnp.logical_and(lan > sub, tiles_row > 0.0), lan, float(LANE))
    next_col = jnp.min(cand, axis=1, keepdims=True)
    next_col = jnp.where(next_col < float(LANE), next_col, -1.0)
    n_lanes = plan_ref.shape[1]
    tidx = jnp.minimum(lax.broadcasted_iota(jnp.int32, (LANE, n_lanes), 1).astype(F32), n_used - 1.0)
    esub = lax.broadcasted_iota(jnp.int32, (LANE, n_lanes), 0).astype(F32)
    te_row = jnp.sum(jnp.where(tile_end_col <= tidx, 1.0, 0.0), axis=0, keepdims=True)
    mine = esub == te_row
    end_at = jnp.sum(jnp.where(mine, end_col, 0.0), axis=0, keepdims=True)
    tv_row = jnp.clip(end_at - tidx[0:1, :] * tm, 0.0, tm)
    nx_row = jnp.sum(jnp.where(mine, next_col, 0.0), axis=0, keepdims=True)
    nu_row = jnp.broadcast_to(n_used, (1, n_lanes))
    plan_ref[...] = jnp.concatenate([te_row, tv_row, nx_row, nu_row, jnp.zeros((SUBLANE - 4, n_lanes), F32)], axis=0)
    start_ref[...] = jnp.broadcast_to(start_col, (LANE, LANE))


def _route_kernel(x_ref, mod_ref, g_ref, wrt_ref, brt_ref, triu_ref, h_ref, metat_ref, plan_ref, start_ref, carry):
    i = pl.program_id(0)

    @pl.when(i == 0)
    def _():
        carry[...] = jnp.zeros_like(carry)

    x = x_ref[...]
    m = mod_ref[0]
    h = _rms(x, g_ref[...]) * (1.0 + m[4:5]) + m[3:4]
    h_ref[...] = _pack_bf16_pairs(h)
    logits = _dot_nt(wrt_ref[...], h.astype(BF16)) + brt_ref[...]
    n_tok = logits.shape[1]
    esub = lax.broadcasted_iota(jnp.int32, logits.shape, 0).astype(F32)
    work = logits
    sel = jnp.zeros(logits.shape, F32)
    hits, firsts, tops = [], [], []
    for k in range(TOP_K):
        mk = jnp.max(work, axis=0, keepdims=True)
        first = jnp.min(jnp.where(work == mk, esub, float(N_EXPERTS)), axis=0, keepdims=True)
        hit = esub == first
        sel = jnp.where(hit, 1.0, sel)
        work = jnp.where(hit, -jnp.inf, work)
        hits.append(hit)
        firsts.append(first)
        tops.append(mk)
    es = [jnp.exp(t - tops[0]) for t in tops]
    denom = es[0] + es[1] + es[2] + es[3]
    pos = _dot(sel.astype(BF16), triu_ref[...]) + carry[:, 0:1]
    carry[...] = carry[...] + jnp.sum(sel, axis=1, keepdims=True)

    @pl.when(i == pl.num_programs(0) - 1)
    def _():
        counts = jnp.concatenate([carry[...], jnp.zeros((LANE - N_EXPERTS, LANE), F32)], axis=0).T
        _tile_plan(counts[:SUBLANE, :], plan_ref, start_ref)

    ranks = [jnp.sum(jnp.where(hit, pos, 0.0), axis=0, keepdims=True) for hit in hits]
    gates = [e / denom for e in es]
    metat_ref[...] = jnp.concatenate(firsts + gates + ranks + [jnp.zeros((SUBLANE // 2, n_tok), F32)], axis=0)


def _route(x, p, layer):
    tb = ROUTE_TB
    return pl.pallas_call(
        _route_kernel,
        grid=(N_TOK // tb,),
        in_specs=[
            pl.BlockSpec((tb, D_MODEL), lambda i: (i, 0)),
            _mod_spec(tb, layer),
            _layer_spec((1, D_MODEL), layer),
            _layer_spec((N_EXPERTS, D_MODEL), layer),
            _layer_spec((N_EXPERTS, 1), layer),
            _const_spec((tb, tb)),
        ],
        out_specs=[
            pl.BlockSpec((tb, D_MODEL // 2), lambda i: (i, 0)),
            pl.BlockSpec((2 * SUBLANE, tb), lambda i: (0, i)),
            _const_spec((SUBLANE, PLAN_LANES)),
            _const_spec((LANE, LANE)),
        ],
        out_shape=[
            jax.ShapeDtypeStruct((N_TOK, D_MODEL // 2), jnp.uint32),
            jax.ShapeDtypeStruct((2 * SUBLANE, N_TOK), F32),
            jax.ShapeDtypeStruct((SUBLANE, PLAN_LANES), F32),
            jax.ShapeDtypeStruct((LANE, LANE), F32),
        ],
        scratch_shapes=[pltpu.VMEM((N_EXPERTS, LANE), F32)],
        compiler_params=_cparams(("arbitrary",)),
        name="moe_route",
    )(x, p["mod"], p["norm_ffn_g"], p["e_w_router_t"], p["e_b_router"], p["triu"])


def _slots_kernel(start_ref, metat_ref, dest_ref):
    start_col = start_ref[:, 0:1]
    esub = lax.broadcasted_iota(jnp.int32, (LANE, SLOT_TB), 0).astype(F32)
    rows = []
    for k in range(TOP_K):
        e = metat_ref[k:k + 1, :]
        first = jnp.sum(jnp.where(esub == e, start_col, 0.0), axis=0, keepdims=True)
        rows.append(first + metat_ref[2 * TOP_K + k:2 * TOP_K + k + 1, :])
    dest_ref[...] = jnp.concatenate(rows, axis=0).astype(jnp.int32)


def _slots(start, meta_t):
    return pl.pallas_call(
        _slots_kernel,
        grid=(N_TOK // SLOT_TB,),
        in_specs=[
            _const_spec((LANE, LANE)),
            pl.BlockSpec((2 * SUBLANE, SLOT_TB), lambda i: (0, i)),
        ],
        out_specs=pl.BlockSpec((TOP_K, SLOT_TB), lambda i: (0, i)),
        out_shape=jax.ShapeDtypeStruct((TOP_K, N_TOK), jnp.int32),
        compiler_params=_cparams(("parallel",)),
        name="moe_slots",
    )(start, meta_t)


def _sc_gather(table, idx, ch):
    b, w = idx.shape[0], table.shape[1]
    per_w = b // SC_WORKERS
    n_ch = per_w // ch
    assert per_w * SC_WORKERS == b and n_ch * ch == per_w and n_ch % 2 == 0
    mesh = plsc.VectorSubcoreMesh(core_axis_name="c", subcore_axis_name="s")

    @functools.partial(
        pl.kernel, mesh=mesh,
        out_type=jax.ShapeDtypeStruct((b, w), table.dtype),
        scratch_types=[
            pltpu.VMEM((n_ch, ch), jnp.int32),
            pltpu.VMEM((ch, w), table.dtype),
            pltpu.VMEM((ch, w), table.dtype),
            pltpu.SemaphoreType.DMA, pltpu.SemaphoreType.DMA,
            pltpu.SemaphoreType.DMA, pltpu.SemaphoreType.DMA,
        ],
        name="sc_row_gather",
    )
    def gather_rows(table_hbm, idx_hbm, out_hbm, idx_v, buf0, buf1, g0, g1, s0, s1):
        wid = lax.axis_index("s") * SC_CORES + lax.axis_index("c")
        base = wid * per_w
        pltpu.sync_copy(idx_hbm.at[wid], idx_v)

        def gather(j, buf, sem):
            return pltpu.make_async_copy(table_hbm.at[idx_v.at[j]], buf, sem)

        def store(j, buf, sem):
            return pltpu.make_async_copy(buf, out_hbm.at[pl.ds(base + j * ch, ch)], sem)

        gather(0, buf0, g0).start()

        @pl.loop(0, n_ch, step=2)
        def _(j):
            @pl.when(j > 0)
            def _():
                store(j - 1, buf1, s1).wait()

            gather(j + 1, buf1, g1).start()
            gather(j, buf0, g0).wait()
            store(j, buf0, s0).start()
            gather(j + 1, buf1, g1).wait()
            store(j + 1, buf1, s1).start()
            store(j, buf0, s0).wait()

            @pl.when(j + 2 < n_ch)
            def _():
                gather(j + 2, buf0, g0).start()

        store(n_ch - 1, buf1, s1).wait()

    return gather_rows(table, idx.reshape(SC_WORKERS, n_ch, ch))


def _sc_dispatch(rows, dest_t, ch):
    n, w = rows.shape
    per_w = n // SC_WORKERS
    n_ch = per_w // ch
    assert per_w * SC_WORKERS == n and n_ch * ch == per_w and n_ch % 2 == 0
    mesh = plsc.VectorSubcoreMesh(core_axis_name="c", subcore_axis_name="s")
    idx = dest_t.reshape(TOP_K, SC_WORKERS, n_ch, ch)

    @functools.partial(
        pl.kernel, mesh=mesh,
        out_type=jax.ShapeDtypeStruct((N_SLOTS, w), rows.dtype),
        scratch_types=[
            pltpu.VMEM((TOP_K * n_ch, ch), jnp.int32),
            pltpu.VMEM((ch, w), rows.dtype),
            pltpu.VMEM((ch, w), rows.dtype),
            pltpu.SemaphoreType.DMA, pltpu.SemaphoreType.DMA,
            pltpu.SemaphoreType.DMA, pltpu.SemaphoreType.DMA,
        ],
        name="sc_row_dispatch",
    )
    def dispatch_rows(rows_hbm, idx_hbm, out_hbm, idx_v, buf0, buf1, l0, l1, s0, s1):
        wid = lax.axis_index("s") * SC_CORES + lax.axis_index("c")
        base = wid * per_w
        for k in range(TOP_K):
            pltpu.sync_copy(idx_hbm.at[k, wid], idx_v.at[pl.ds(k * n_ch, n_ch)])

        def load(j, buf, sem):
            return pltpu.make_async_copy(rows_hbm.at[pl.ds(base + j * ch, ch)], buf, sem)

        def scatter(j, k, buf, sem):
            return pltpu.make_async_copy(buf, out_hbm.at[idx_v.at[k * n_ch + j]], sem)

        load(0, buf0, l0).start()

        @pl.loop(0, n_ch, step=2)
        def _(j):
            load(j + 1, buf1, l1).start()
            load(j, buf0, l0).wait()
            for k in range(TOP_K):
                scatter(j, k, buf0, s0).start()
            load(j + 1, buf1, l1).wait()
            for k in range(TOP_K):
                scatter(j + 1, k, buf1, s1).start()
            for k in range(TOP_K):
                scatter(j, k, buf0, s0).wait()

            @pl.when(j + 2 < n_ch)
            def _():
                load(j + 2, buf0, l0).start()

            for k in range(TOP_K):
                scatter(j + 1, k, buf1, s1).wait()

    return dispatch_rows(rows, idx)


def _deinterleave_matrix():
    src = jnp.arange(2 * LANE)[:, None]
    dst = jnp.arange(2 * LANE)[None, :]
    want = jnp.where(dst < LANE, 2 * dst, 2 * (dst - LANE) + 1)
    return (src == want).astype(BF16)


def _expert_kernel(te_ref, nu_ref, tv_ref, nx_ref, x_ref, wgu_hbm, bgu_ref, wd_hbm, bd_ref, perm_ref, o_ref,
                   wgu_st, wd_st, wgu_bf, wd_bf, sems, *, layer):
    i = pl.program_id(0)
    prev = te_ref[jnp.maximum(i - 1, 0)]
    fresh = jnp.logical_or(i == 0, te_ref[i] != prev)

    def fetch(e):
        return (pltpu.make_async_copy(wgu_hbm.at[layer, e], wgu_st, sems.at[0]),
                pltpu.make_async_copy(wd_hbm.at[layer, e], wd_st, sems.at[1]))

    @pl.when(i == 0)
    def _():
        for cp in fetch(te_ref[0]):
            cp.start()

    @pl.when(jnp.logical_and(fresh, i < nu_ref[0]))
    def _():
        for cp in fetch(te_ref[i]):
            cp.wait()
        for b in range(2 * D_FF // (2 * LANE)):
            sl = slice(b * 2 * LANE, (b + 1) * 2 * LANE)
            wgu_bf[:, sl] = _dot(wgu_st[:, sl].astype(BF16), perm_ref[...]).astype(BF16)
        wd_bf[...] = wd_st[...].astype(BF16)

        @pl.when(nx_ref[i] >= 0)
        def _():
            for cp in fetch(nx_ref[i]):
                cp.start()

    def ffn(n_rows):
        row = lax.broadcasted_iota(jnp.int32, (n_rows, D_MODEL // 2), 0)
        w = jnp.where(row < tv_ref[i], x_ref[:n_rows, :], jnp.uint32(0))
        x = _unpack_bf16_pairs(w).astype(BF16)
        gu = _dot(x, wgu_bf[...]) + bgu_ref[...]
        acts = []
        for b in range(D_FF // LANE):
            glu = jnp.minimum(gu[:, b * 2 * LANE:b * 2 * LANE + LANE], SWIGLU_LIMIT)
            lin = jnp.clip(gu[:, b * 2 * LANE + LANE:(b + 1) * 2 * LANE], -SWIGLU_LIMIT, SWIGLU_LIMIT)
            acts.append((glu * jax.nn.sigmoid(SWIGLU_ALPHA * glu) * (lin + 1.0)).astype(BF16))
        act = jnp.concatenate(acts, axis=1)
        o_ref[:n_rows, :] = _pack_bf16_pairs(_dot(act, wd_bf[...]) + bd_ref[...])

    in_use = i < nu_ref[0]

    @pl.when(jnp.logical_and(in_use, tv_ref[i] > TM // 2))
    def _():
        ffn(TM)

    @pl.when(jnp.logical_and(in_use, tv_ref[i] <= TM // 2))
    def _():
        ffn(TM // 2)


def _experts(buf, tile_expert, n_used, tile_valid, tile_next, p, layer):
    def row_idx(i, te, nu, tv, nx):
        return (jnp.minimum(i, nu[0] - 1), 0)

    def b_idx(i, te, nu, tv, nx):
        return (layer, te[i], 0, 0)

    grid_spec = pltpu.PrefetchScalarGridSpec(
        num_scalar_prefetch=4,
        grid=(N_TILES,),
        in_specs=[
            pl.BlockSpec((TM, D_MODEL // 2), row_idx),
            pl.BlockSpec(memory_space=pl.ANY),
            pl.BlockSpec((None, None, 1, 2 * D_FF), b_idx),
            pl.BlockSpec(memory_space=pl.ANY),
            pl.BlockSpec((None, None, 1, D_MODEL), b_idx),
            _const_spec((2 * LANE, 2 * LANE)),
        ],
        out_specs=pl.BlockSpec((TM, D_MODEL // 2), row_idx),
        scratch_shapes=[
            pltpu.VMEM((D_MODEL, 2 * D_FF), F32),
            pltpu.VMEM((D_FF, D_MODEL), F32),
            pltpu.VMEM((D_MODEL, 2 * D_FF), BF16),
            pltpu.VMEM((D_FF, D_MODEL), BF16),
            pltpu.SemaphoreType.DMA((2,)),
        ],
    )
    return pl.pallas_call(
        functools.partial(_expert_kernel, layer=layer),
        grid_spec=grid_spec,
        out_shape=jax.ShapeDtypeStruct((N_SLOTS, D_MODEL // 2), jnp.uint32),
        compiler_params=_cparams(("arbitrary",)),
        name="moe_experts",
    )(tile_expert, n_used, tile_valid, tile_next, buf, p["e_w_gu"], p["e_b_gu"], p["e_w_down"], p["e_b_down"],
      p["deinterleave"])


def _pack_bf16_pairs(v):
    half = v.shape[1] // 2
    bits = pltpu.bitcast(v.astype(BF16).astype(F32), jnp.uint32)
    return (bits[:, half:] & jnp.uint32(0xFFFF0000)) | (bits[:, :half] >> 16)


def _unpack_bf16_pairs(w):
    return jnp.concatenate([pltpu.bitcast(w << 16, F32), pltpu.bitcast(w & jnp.uint32(0xFFFF0000), F32)],
                           axis=1)


def _combine_kernel(x_ref, mod_ref, y_ref, mt_ref, o_ref):
    w = mt_ref[...].T[:, TOP_K:2 * TOP_K]
    y = _unpack_bf16_pairs(y_ref[0]) * w[:, 0:1]
    for k in range(1, TOP_K):
        y = y + _unpack_bf16_pairs(y_ref[k]) * w[:, k:k + 1]
    o_ref[...] = x_ref[...] + mod_ref[0][5:6] * y


def _combine(x, y4, meta_t, p, layer, part):
    n_rows = N_TOK // MOE_PARTS
    tb = TB
    first = part * n_rows // tb
    return pl.pallas_call(
        _combine_kernel,
        grid=(n_rows // tb,),
        in_specs=[
            pl.BlockSpec((tb, D_MODEL), lambda i: (i + first, 0)),
            _mod_spec(tb, layer, first),
            pl.BlockSpec((TOP_K, tb, D_MODEL // 2), lambda i: (0, i, 0)),
            pl.BlockSpec((2 * SUBLANE, tb), lambda i: (0, i + first)),
        ],
        out_specs=pl.BlockSpec((tb, D_MODEL), lambda i: (i, 0)),
        out_shape=jax.ShapeDtypeStruct((n_rows, D_MODEL), F32),
        compiler_params=_cparams(("parallel",)),
        name="moe_combine",
    )(x, p["mod"], y4, meta_t)


def _moe_layer(x, p, layer):
    hp, meta_t, plan, start = _route(x, p, layer)
    plan = plan[:4, :N_TILES].astype(jnp.int32)
    tile_expert, tile_valid, tile_next, n_used = plan[0], plan[1], plan[2], plan[3, :1]
    dest_t = _slots(start, meta_t)
    buf = _sc_dispatch(hp, dest_t, SC_CHUNK_ROWS)
    yb = _experts(buf, tile_expert, n_used, tile_valid, tile_next, p, layer)
    n_rows = N_TOK // MOE_PARTS
    outs = []
    for part in range(MOE_PARTS):
        idx = dest_t[:, part * n_rows:(part + 1) * n_rows].reshape(-1)
        y4 = _sc_gather(yb, idx, SC_CHUNK_ROWS).reshape(TOP_K, n_rows, D_MODEL // 2)
        outs.append(_combine(x, y4, meta_t, p, layer, part))
    return outs


def _prepare(c, cache_ckv, cache_krope, c_ctx, norm_mix_g, norm_ffn_g, w_mod, b_mod, g_w_in, g_b_in, g_norm_v,
             g_w_s, g_b_s, g_w_out, m_w_dq, m_norm_q, m_w_uq, m_w_dkv, m_norm_kv, m_w_ukv, m_qk_norm_q,
             m_qk_norm_k, m_w_o, e_w_router, e_b_router, e_w_gu, e_b_gu, e_w_down, e_b_down):
    n_mla = m_w_dq.shape[0]
    cond = jnp.concatenate([c_ctx[None, :], c, jnp.zeros((SUBLANE - N_COND, D_MODEL), F32)], axis=0)
    wdkv = jnp.concatenate([m_w_dkv[..., :KV_RANK], jnp.zeros((n_mla, D_MODEL, QK_NOPE_DIM), F32),
                            m_w_dkv[..., KV_RANK:], jnp.zeros((n_mla, D_MODEL, HEAD_PAD - QK_HEAD_DIM), F32)],
                           axis=-1)
    w_uq = jnp.pad(m_w_uq.reshape(n_mla, Q_RANK, N_HEADS, QK_HEAD_DIM),
                   ((0, 0), (0, 0), (0, 0), (0, HEAD_PAD - QK_HEAD_DIM))).reshape(n_mla, Q_RANK, QKV_W)

    def gain_col(g):
        return jnp.pad(g, ((0, 0), (0, HEAD_PAD - QK_HEAD_DIM)))[:, :, None]

    return {
        "mod": _modulation(cond, w_mod, b_mod),
        "rope_t": _rope_tables(),
        "norm_mix_g": norm_mix_g[:, None, :],
        "norm_ffn_g": norm_ffn_g[:, None, :],
        "g_w_in": g_w_in.astype(BF16),
        "g_b_in": g_b_in[:, None, :],
        "g_norm_v": g_norm_v[:, None, :],
        "g_w_s": g_w_s.astype(BF16),
        "g_b_st": jnp.swapaxes(g_b_s, 1, 2),
        "g_w_out": g_w_out.astype(BF16),
        "m_w_dq": m_w_dq.astype(BF16),
        "m_norm_q": m_norm_q[:, None, :],
        "m_w_uq_t": jnp.swapaxes(w_uq, 1, 2).astype(BF16),
        "m_gq": gain_col(m_qk_norm_q),
        "m_w_dkv": wdkv.astype(BF16),
        "m_norm_kv": m_norm_kv[:, None, :],
        "m_w_ukv_t": jnp.swapaxes(m_w_ukv, 1, 2).astype(BF16),
        "m_gk": gain_col(m_qk_norm_k),
        "m_w_o": m_w_o.astype(BF16),
        "cache_ckv": cache_ckv,
        "cache_krp": jnp.pad(cache_krope, ((0, 0), (0, 0), (0, 0), (QK_NOPE_DIM, HEAD_PAD - QK_HEAD_DIM))),
        "e_w_router_t": jnp.swapaxes(e_w_router, 1, 2).astype(BF16),
        "e_b_router": e_b_router[:, :, None],
        "triu": jnp.tri(ROUTE_TB, ROUTE_TB, -1, dtype=BF16).T,
        "e_w_gu": e_w_gu,
        "e_b_gu": e_b_gu.reshape(DEPTH, N_EXPERTS, D_FF // LANE, LANE, 2).swapaxes(3, 4).reshape(
            DEPTH, N_EXPERTS, 1, 2 * D_FF),
        "e_w_down": e_w_down,
        "e_b_down": e_b_down[:, :, None, :],
        "deinterleave": _deinterleave_matrix(),
    }


def kernel(x_prompt, x_sample, c, cache_ckv, cache_krope, c_ctx, norm_mix_g, norm_ffn_g, w_mod, b_mod,
           g_w_in, g_b_in, g_norm_v, g_w_s, g_b_s, g_w_out, m_w_dq, m_norm_q, m_w_uq, m_w_dkv,
           m_norm_kv, m_w_ukv, m_qk_norm_q, m_qk_norm_k, m_w_o, e_w_router, e_b_router, e_w_gu,
           e_b_gu, e_w_down, e_b_down):
    p = _prepare(c, cache_ckv, cache_krope, c_ctx, norm_mix_g, norm_ffn_g, w_mod, b_mod, g_w_in, g_b_in,
                 g_norm_v, g_w_s, g_b_s, g_w_out, m_w_dq, m_norm_q, m_w_uq, m_w_dkv, m_norm_kv, m_w_ukv,
                 m_qk_norm_q, m_qk_norm_k, m_w_o, e_w_router, e_b_router, e_w_gu, e_b_gu, e_w_down, e_b_down)
    assert MOE_PARTS == 2 and N_PROMPT == N_SAMPLE
    x_p, x_s = x_prompt.reshape(N_PROMPT, D_MODEL), x_sample.reshape(N_SAMPLE, D_MODEL)
    ckv_states, krope_states = [], []
    for layer in range(DEPTH):
        if layer % 2 == 0:
            x = _gmlp_layer(x_p, x_s, p, layer)
        else:
            x, s_ckv, s_krope = _mla_layer(x_p, x_s, p, layer)
            ckv_states.append(s_ckv)
            krope_states.append(s_krope)
        x_p, x_s = _moe_layer(x, p, layer)
    y_prompt = x_p.reshape(BATCH, SEQ, D_MODEL)
    y_sample = x_s.reshape(DEC_BATCH, DEC_SEQ, D_MODEL)
    return (y_prompt, y_sample, jnp.stack(ckv_states, axis=1), jnp.stack(krope_states, axis=1))
```

```python
import functools
import math

import jax
import jax.numpy as jnp
from jax import lax
from jax.experimental import pallas as pl
from jax.experimental.pallas import tpu as pltpu
from jax.experimental.pallas import tpu_sc as plsc

F32 = jnp.float32
BF16 = jnp.bfloat16

D_MODEL = 1024
BATCH = 32
SEQ = 256
DEPTH = 4
DEC_BATCH = 2
DEC_SEQ = 4096
PAST_LEN = 512
GRID_W = 64
RMS_EPS = 1e-6
GMLP_WIDTH = 2 * D_MODEL
GMLP_GROUPS = 8
GROUP_W = GMLP_WIDTH // GMLP_GROUPS
CHUNK = 128
N_HEADS = 16
QK_NOPE_DIM = 64
QK_ROPE_DIM = 32
QK_HEAD_DIM = QK_NOPE_DIM + QK_ROPE_DIM
V_HEAD_DIM = 64
Q_RANK = 256
KV_RANK = 128
ROPE_THETA = 10000.0
N_EXPERTS = 32
TOP_K = 4
D_FF = D_MODEL
SWIGLU_LIMIT = 7.0
SWIGLU_ALPHA = 1.702

N_PROMPT = BATCH * SEQ
N_SAMPLE = DEC_BATCH * DEC_SEQ
N_TOK = N_PROMPT + N_SAMPLE
N_COND = 1 + DEC_BATCH
KV_LEN = PAST_LEN + DEC_SEQ
N_KV_ROWS = DEC_BATCH * KV_LEN + N_PROMPT

LANE = 128
SUBLANE = 8
HEAD_PAD = LANE
QKV_W = N_HEADS * HEAD_PAD
VMEM_LIMIT = 56 * 1024 * 1024

TB = 512
MOD_TN = 1536
TQ = 512
TKC = 256
ATTN_CHUNKS_PER_ITER = 9
TM = 1024
N_TILES = N_TOK * TOP_K // TM + N_EXPERTS
N_SLOTS = N_TILES * TM
PLAN_LANES = -(-N_TILES // LANE) * LANE
SLOT_TB = 2048
ROUTE_TB = 512
MOE_PARTS = 2
SC_CORES = 2
SC_SUBCORES = 16
SC_WORKERS = SC_CORES * SC_SUBCORES
SC_CHUNK_ROWS = 64
ROPE_BLOCKS = DEC_SEQ // TB
NEG_BIG = -1e30


def _cparams(sem):
    return pltpu.CompilerParams(dimension_semantics=sem, vmem_limit_bytes=VMEM_LIMIT)


def _cond_of_block(i, tb):
    n_p = N_PROMPT // tb
    per = DEC_SEQ // tb
    return jnp.where(i < n_p, 0, 1 + (i - n_p) // per)


def _rms(x, g, n=None):
    n = x.shape[-1] if n is None else n
    ss = jnp.sum(x * x, axis=-1, keepdims=True) * (1.0 / n)
    return x * lax.rsqrt(ss + RMS_EPS) * g


def _dot(a, b):
    return jnp.dot(a, b, preferred_element_type=F32)


def _mod_kernel(c_ref, w_ref, b_ref, o_ref):
    c = c_ref[...]
    s = c * jax.nn.sigmoid(c)
    o_ref[0] = _dot(s.astype(BF16), w_ref[0].astype(BF16)) + b_ref[0]


def _modulation(cond, w_mod, b_mod):
    tn = MOD_TN
    out = pl.pallas_call(
        _mod_kernel,
        grid=(DEPTH, 6 * D_MODEL // tn),
        in_specs=[
            pl.BlockSpec((SUBLANE, D_MODEL), lambda l, j: (0, 0)),
            pl.BlockSpec((1, D_MODEL, tn), lambda l, j: (l, 0, j)),
            pl.BlockSpec((1, 1, tn), lambda l, j: (l, 0, j)),
        ],
        out_specs=pl.BlockSpec((1, SUBLANE, tn), lambda l, j: (l, 0, j)),
        out_shape=jax.ShapeDtypeStruct((DEPTH, SUBLANE, 6 * D_MODEL), F32),
        compiler_params=_cparams(("parallel", "parallel")),
        name="adaln_mod",
    )(cond, w_mod, b_mod.reshape(DEPTH, 1, 6 * D_MODEL))
    m = out[:, :N_COND].reshape(DEPTH, N_COND, 6, D_MODEL)
    return jnp.pad(m, ((0, 0), (0, 0), (0, SUBLANE - 6), (0, 0)))


def _mod_spec(tb, layer, first_block=0):
    return pl.BlockSpec((None, 1, SUBLANE, D_MODEL),
                        lambda i: (layer, _cond_of_block(i + first_block, tb), 0, 0))


def _const_spec(shape):
    nd = len(shape)
    return pl.BlockSpec(shape, lambda *_: (0,) * nd)


def _layer_spec(shape, j):
    nd = len(shape)
    return pl.BlockSpec((None,) + tuple(shape), lambda *_: (j,) + (0,) * nd)


def _gelu_tanh(x):
    a = math.sqrt(2.0 / math.pi)
    hx = 0.5 * x
    return hx + hx * jnp.tanh(x * (a + (0.044715 * a) * (x * x)))


def _gmlp_kernel(xp_ref, xs_ref, mod_ref, g_ref, win_ref, bin_ref, gv_ref, ws_ref, bst_ref, wout_ref, o_ref):
    x = jnp.where(pl.program_id(0) < N_PROMPT // TB, xp_ref[...], xs_ref[...])
    m = mod_ref[0]
    h = _rms(x, g_ref[...]) * (1.0 + m[1:2]) + m[0:1]
    hb = h.astype(BF16)
    zv = _gelu_tanh(_dot(hb, win_ref[:, GMLP_WIDTH:]) + bin_ref[:, GMLP_WIDTH:])
    vn = _rms(zv, gv_ref[...]).astype(BF16)
    rows = []
    for c in range(TB // CHUNK):
        cols = []
        for g in range(GMLP_GROUPS):
            blk = vn[c * CHUNK:(c + 1) * CHUNK, g * GROUP_W:(g + 1) * GROUP_W]
            cols.append(_dot(ws_ref[g], blk) + bst_ref[:, g:g + 1])
        rows.append(jnp.concatenate(cols, axis=1))
    vm = jnp.concatenate(rows, axis=0)
    u = _gelu_tanh(_dot(hb, win_ref[:, :GMLP_WIDTH]) + bin_ref[:, :GMLP_WIDTH])
    d = _dot((u * vm).astype(BF16), wout_ref[...])
    o_ref[...] = x + m[2:3] * d


def _split_rows_specs():
    n_pb = N_PROMPT // TB
    return [pl.BlockSpec((TB, D_MODEL), lambda i: (jnp.minimum(i, n_pb - 1), 0)),
            pl.BlockSpec((TB, D_MODEL), lambda i: (jnp.maximum(i - n_pb, 0), 0))]


def _gmlp_layer(x_p, x_s, p, layer):
    j = layer // 2
    return pl.pallas_call(
        _gmlp_kernel,
        grid=(N_TOK // TB,),
        in_specs=_split_rows_specs() + [
            _mod_spec(TB, layer),
            _layer_spec((1, D_MODEL), layer),
            _layer_spec((D_MODEL, 2 * GMLP_WIDTH), j),
            _layer_spec((1, 2 * GMLP_WIDTH), j),
            _layer_spec((1, GMLP_WIDTH), j),
            _layer_spec((GMLP_GROUPS, CHUNK, CHUNK), j),
            _layer_spec((CHUNK, GMLP_GROUPS), j),
            _layer_spec((GMLP_WIDTH, D_MODEL), j),
        ],
        out_specs=pl.BlockSpec((TB, D_MODEL), lambda i: (i, 0)),
        out_shape=jax.ShapeDtypeStruct((N_TOK, D_MODEL), F32),
        compiler_params=_cparams(("parallel",)),
        name="gmlp_mixer",
    )(x_p, x_s, p["mod"], p["norm_mix_g"], p["g_w_in"], p["g_b_in"], p["g_norm_v"], p["g_w_s"], p["g_b_st"],
      p["g_w_out"])


def _rope_tables():
    t = jnp.arange(DEC_SEQ)
    row_id = (t // GRID_W).astype(F32)
    col_id = (t % GRID_W).astype(F32)
    axis_dim = QK_ROPE_DIM // 2
    inv_freq = ROPE_THETA ** (-jnp.arange(0, axis_dim, 2, dtype=F32) / axis_dim)
    ang = jnp.stack([row_id[:, None] * inv_freq, col_id[:, None] * inv_freq], axis=1)
    cos, sin = jnp.cos(ang), jnp.sin(ang)
    zeros = jnp.zeros_like(sin)
    cos_l = jnp.concatenate([cos, cos], axis=-1).reshape(DEC_SEQ, QK_ROPE_DIM)
    s1_l = jnp.concatenate([-sin, zeros], axis=-1).reshape(DEC_SEQ, QK_ROPE_DIM)
    s2_l = jnp.concatenate([zeros, sin], axis=-1).reshape(DEC_SEQ, QK_ROPE_DIM)

    def widen(rope_part, nope_fill):
        left = jnp.full((DEC_SEQ, QK_NOPE_DIM), nope_fill, F32)
        right = jnp.zeros((DEC_SEQ, HEAD_PAD - QK_HEAD_DIM), F32)
        return jnp.concatenate([left, rope_part, right], axis=-1)

    pos = jnp.stack([widen(cos_l, 1.0), widen(s1_l, 0.0), widen(s2_l, 0.0)])
    ident_c = jnp.concatenate([jnp.ones((TB, QK_HEAD_DIM), F32),
                               jnp.zeros((TB, HEAD_PAD - QK_HEAD_DIM), F32)], axis=-1)
    ident = jnp.stack([ident_c, jnp.zeros_like(ident_c), jnp.zeros_like(ident_c)])
    return jnp.swapaxes(jnp.concatenate([pos, ident], axis=1), 1, 2)


def _dot_nt(a, b):
    return lax.dot_general(a, b, (((1,), (1,)), ((), ())), preferred_element_type=F32)


def _shift_rows(x, n):
    n = n % x.shape[0]
    return jnp.concatenate([x[n:], x[:n]], axis=0)


def _head_norm_rope(nope, rope, g_ref, tabt_ref):
    half = QK_ROPE_DIM // 4
    lo, hi = QK_NOPE_DIM, QK_HEAD_DIM
    ss = (jnp.sum(nope * nope, axis=0, keepdims=True) + jnp.sum(rope * rope, axis=0, keepdims=True))
    rs = lax.rsqrt(ss * (1.0 / QK_HEAD_DIM) + RMS_EPS)
    xn = nope * rs * g_ref[:lo, :]
    xr = rope * rs * g_ref[lo:hi, :]
    rot = (xr * tabt_ref[0, lo:hi, :] + _shift_rows(xr, half) * tabt_ref[1, lo:hi, :]
           + _shift_rows(xr, -half) * tabt_ref[2, lo:hi, :])
    return jnp.concatenate([xn, rot, jnp.zeros((HEAD_PAD - hi, nope.shape[1]), F32)], axis=0)


def _mla_proj_kernel(xp_ref, xs_ref, mod_ref, g_ref, wdq_ref, nq_ref, wuqt_ref, gq_ref, wdkv_ref, nkv_ref,
                     tabt_ref, qt_ref, ckv_ref, krp_ref):
    x = jnp.where(pl.program_id(0) < N_PROMPT // TB, xp_ref[...], xs_ref[...])
    m = mod_ref[0]
    h = _rms(x, g_ref[...]) * (1.0 + m[1:2]) + m[0:1]
    hb = h.astype(BF16)
    cq = _rms(_dot(hb, wdq_ref[...]), nq_ref[...])
    qt = _dot_nt(wuqt_ref[...], cq.astype(BF16))
    for hd in range(N_HEADS):
        r0 = hd * HEAD_PAD
        qh = _head_norm_rope(qt[r0:r0 + QK_NOPE_DIM, :], qt[r0 + QK_NOPE_DIM:r0 + QK_HEAD_DIM, :], gq_ref, tabt_ref)
        qt_ref[r0:r0 + HEAD_PAD, :] = qh.astype(BF16)
    kva = _dot(hb, wdkv_ref[...])
    ckv_ref[...] = _rms(kva[:, :KV_RANK], nkv_ref[...])
    krp_ref[...] = kva[:, KV_RANK:]


def _mla_proj(x_p, x_s, p, layer):
    j = layer // 2
    n_pb = N_PROMPT // TB

    def tab_idx(i):
        return (0, 0, jnp.where(i < n_pb, ROPE_BLOCKS, (i - n_pb) % ROPE_BLOCKS))

    return pl.pallas_call(
        _mla_proj_kernel,
        grid=(N_TOK // TB,),
        in_specs=_split_rows_specs() + [
            _mod_spec(TB, layer),
            _layer_spec((1, D_MODEL), layer),
            _layer_spec((D_MODEL, Q_RANK), j),
            _layer_spec((1, Q_RANK), j),
            _layer_spec((QKV_W, Q_RANK), j),
            _layer_spec((HEAD_PAD, 1), j),
            _layer_spec((D_MODEL, 2 * LANE), j),
            _layer_spec((1, KV_RANK), j),
            pl.BlockSpec((3, HEAD_PAD, TB), tab_idx),
        ],
        out_specs=[
            pl.BlockSpec((QKV_W, TB), lambda i: (0, i)),
            pl.BlockSpec((TB, KV_RANK), lambda i: (i, 0)),
            pl.BlockSpec((TB, HEAD_PAD), lambda i: (i, 0)),
        ],
        out_shape=[
            jax.ShapeDtypeStruct((QKV_W, N_TOK), BF16),
            jax.ShapeDtypeStruct((N_TOK, KV_RANK), F32),
            jax.ShapeDtypeStruct((N_TOK, HEAD_PAD), F32),
        ],
        compiler_params=_cparams(("parallel",)),
        name="mla_proj",
    )(x_p, x_s, p["mod"], p["norm_mix_g"], p["m_w_dq"], p["m_norm_q"], p["m_w_uq_t"], p["m_gq"], p["m_w_dkv"],
      p["m_norm_kv"], p["rope_t"])


def _kv_block_source(i):
    n_sb = DEC_BATCH * KV_LEN // TB
    per = KV_LEN // TB
    b = jnp.minimum(i // per, DEC_BATCH - 1)
    jj = i % per
    return jnp.logical_and(i < n_sb, jj < PAST_LEN // TB), b, jj


def _kv_expand_kernel(cckv_ref, ckrp_ref, ckv_ref, krp_ref, wukvt_ref, gk_ref, tabt_ref, k_ref, vt_ref):
    from_cache, _, _ = _kv_block_source(pl.program_id(0))
    ckv = jnp.where(from_cache, cckv_ref[...], ckv_ref[...])
    krp = jnp.where(from_cache, ckrp_ref[...], krp_ref[...])
    kvt = _dot_nt(wukvt_ref[...], ckv.astype(BF16))
    k_rope = krp.T[QK_NOPE_DIM:QK_HEAD_DIM, :]
    row = lax.broadcasted_iota(jnp.int32, (QK_NOPE_DIM, TB), 0)
    ones_row = jnp.where(row == 0, 1.0, 0.0)
    for hd in range(N_HEADS):
        r0 = hd * HEAD_PAD
        kh = _head_norm_rope(kvt[r0:r0 + QK_NOPE_DIM, :], k_rope, gk_ref, tabt_ref)
        k_ref[:, r0:r0 + HEAD_PAD] = kh.T.astype(BF16)
        vt_ref[r0:r0 + HEAD_PAD, :] = jnp.concatenate([ones_row, kvt[r0 + QK_NOPE_DIM:r0 + HEAD_PAD, :]],
                                                      axis=0).astype(BF16)


def _kv_expand(ckv, krp, p, layer):
    j = layer // 2
    n_sb = DEC_BATCH * KV_LEN // TB
    n_cache = PAST_LEN // TB
    n_pb = N_PROMPT // TB
    lat_blocks = DEC_SEQ // TB

    def tab_idx(i):
        from_cache, _, jj = _kv_block_source(i)
        return (0, 0, jnp.where((i >= n_sb) | from_cache, ROPE_BLOCKS, jj - n_cache))

    def cache_idx(i):
        _, b, jj = _kv_block_source(i)
        return (b, j, jnp.minimum(jj, n_cache - 1), 0)

    def tok_idx(i):
        _, b, jj = _kv_block_source(i)
        latent = n_pb + b * lat_blocks + jnp.maximum(jj - n_cache, 0)
        return (jnp.where(i < n_sb, latent, i - n_sb), 0)

    return pl.pallas_call(
        _kv_expand_kernel,
        grid=(N_KV_ROWS // TB,),
        in_specs=[
            pl.BlockSpec((None, None, TB, KV_RANK), cache_idx),
            pl.BlockSpec((None, None, TB, HEAD_PAD), cache_idx),
            pl.BlockSpec((TB, KV_RANK), tok_idx),
            pl.BlockSpec((TB, HEAD_PAD), tok_idx),
            _layer_spec((QKV_W, KV_RANK), j),
            _layer_spec((HEAD_PAD, 1), j),
            pl.BlockSpec((3, HEAD_PAD, TB), tab_idx),
        ],
        out_specs=[
            pl.BlockSpec((TB, QKV_W), lambda i: (i, 0)),
            pl.BlockSpec((QKV_W, TB), lambda i: (0, i)),
        ],
        out_shape=[
            jax.ShapeDtypeStruct((N_KV_ROWS, QKV_W), BF16),
            jax.ShapeDtypeStruct((QKV_W, N_KV_ROWS), BF16),
        ],
        compiler_params=_cparams(("parallel",)),
        name="mla_kv_expand",
    )(p["cache_ckv"], p["cache_krp"], ckv, krp, p["m_w_ukv_t"], p["m_gk"], p["rope_t"])


def _attn_kernel(qt_ref, k_ref, vt_ref, o_ref, s_a, s_b, *, t_k, hps, qps, tq):
    c = (1.0 / math.sqrt(QK_HEAD_DIM)) * math.log2(math.e)
    kc = min(TKC, t_k)
    n_chunks = t_k // kc
    per_it = min(ATTN_CHUNKS_PER_ITER, n_chunks)
    n_it = n_chunks // per_it
    bufs = (s_a, s_b)

    units = [(qb, h) for qb in range(qps) for h in range(hps)]

    def rows(h):
        return slice(h * HEAD_PAD, (h + 1) * HEAD_PAD)

    def cols(qb):
        return slice(qb * tq, (qb + 1) * tq)

    def scores(u, off, m8):
        qb, h = units[u]
        st = _dot(k_ref[pl.ds(off, kc), rows(h)], qt_ref[rows(h), cols(qb)])
        bufs[u % 2][pl.ds(off, kc), :] = st
        return jnp.maximum(m8, jnp.max(st.reshape(kc // SUBLANE, SUBLANE, tq), axis=0))

    def weigh(u, off, m, acc):
        pt = jnp.exp2((bufs[u % 2][pl.ds(off, kc), :] - m) * c).astype(BF16)
        return acc + _dot(vt_ref[rows(units[u][1]), pl.ds(off, kc)], pt)

    def phase(u_scores, u_weigh, m):
        def body(it, carry):
            m8, acc = carry
            for j in range(per_it):
                off = (it * per_it + j) * kc
                off = off if isinstance(off, int) else pl.multiple_of(off, kc)
                if u_scores is not None:
                    m8 = scores(u_scores, off, m8)
                if u_weigh is not None:
                    acc = weigh(u_weigh, off, m, acc)
            return m8, acc

        init = (jnp.full((SUBLANE, tq), NEG_BIG, F32), jnp.zeros((HEAD_PAD, tq), F32))
        return body(0, init) if n_it == 1 else lax.fori_loop(0, n_it, body, init)

    outs = []
    if n_chunks == 1:
        sts = [_dot(k_ref[:, rows(h)], qt_ref[rows(h), cols(qb)]) for qb, h in units]
        for (qb, h), st in zip(units, sts):
            m = jnp.max(st, axis=0, keepdims=True)
            acc = _dot(vt_ref[rows(h), :], jnp.exp2((st - m) * c).astype(BF16))
            outs.append(acc[QK_NOPE_DIM:, :] / acc[0:1, :])
    else:
        m8, _ = phase(0, None, None)
        for u in range(1, len(units) + 1):
            m = jnp.max(m8, axis=0, keepdims=True)
            m8, acc = phase(u if u < len(units) else None, u - 1, m)
            outs.append(acc[QK_NOPE_DIM:, :] / acc[0:1, :])
    for qb in range(qps):
        o_ref[cols(qb), :] = jnp.concatenate(outs[qb * hps:(qb + 1) * hps], axis=0).T.astype(BF16)


def _attention(qt, k, vt, *, n_batch, t_q, t_k, q_row0, kv_row0, hps, qps, tq_unit):
    tq = tq_unit * qps
    nq = t_q // tq
    q0 = q_row0 // tq
    k0 = kv_row0 // t_k
    return pl.pallas_call(
        functools.partial(_attn_kernel, t_k=t_k, hps=hps, qps=qps, tq=tq_unit),
        grid=(n_batch, N_HEADS // hps, nq),
        in_specs=[
            pl.BlockSpec((hps * HEAD_PAD, tq), lambda b, h, i: (h, q0 + b * nq + i)),
            pl.BlockSpec((t_k, hps * HEAD_PAD), lambda b, h, i: (k0 + b, h)),
            pl.BlockSpec((hps * HEAD_PAD, t_k), lambda b, h, i: (h, k0 + b)),
        ],
        out_specs=pl.BlockSpec((tq, hps * V_HEAD_DIM), lambda b, h, i: (b * nq + i, h)),
        out_shape=jax.ShapeDtypeStruct((n_batch * t_q, N_HEADS * V_HEAD_DIM), BF16),
        scratch_shapes=[pltpu.VMEM((t_k, tq_unit), F32), pltpu.VMEM((t_k, tq_unit), F32)],
        compiler_params=_cparams(("parallel", "parallel", "parallel")),
        name=f"mla_attention_tk{t_k}",
    )(qt, k, vt)


def _out_proj_kernel(ap_ref, as_ref, xp_ref, xs_ref, mod_ref, wo_ref, o_ref):
    is_prompt = pl.program_id(0) < N_PROMPT // TB
    a = jnp.where(is_prompt, ap_ref[...], as_ref[...])
    x = jnp.where(is_prompt, xp_ref[...], xs_ref[...])
    o_ref[...] = x + mod_ref[0][2:3] * _dot(a, wo_ref[...])


def _out_proj(attn_p, attn_s, x_p, x_s, p, layer):
    return pl.pallas_call(
        _out_proj_kernel,
        grid=(N_TOK // TB,),
        in_specs=_split_rows_specs() + _split_rows_specs() + [
            _mod_spec(TB, layer),
            _layer_spec((D_MODEL, D_MODEL), layer // 2),
        ],
        out_specs=pl.BlockSpec((TB, D_MODEL), lambda i: (i, 0)),
        out_shape=jax.ShapeDtypeStruct((N_TOK, D_MODEL), F32),
        compiler_params=_cparams(("parallel",)),
        name="mla_out_proj",
    )(attn_p, attn_s, x_p, x_s, p["mod"], p["m_w_o"])


def _mla_layer(x_p, x_s, p, layer):
    qt, ckv, krp = _mla_proj(x_p, x_s, p, layer)
    k, vt = _kv_expand(ckv, krp, p, layer)
    a_p = _attention(qt, k, vt, n_batch=BATCH, t_q=SEQ, t_k=SEQ, q_row0=0, kv_row0=DEC_BATCH * KV_LEN,
                     hps=N_HEADS, qps=1, tq_unit=SEQ)
    a_s = _attention(qt, k, vt, n_batch=DEC_BATCH, t_q=DEC_SEQ, t_k=KV_LEN, q_row0=N_PROMPT, kv_row0=0,
                     hps=4, qps=4, tq_unit=TQ)
    x = _out_proj(a_p, a_s, x_p, x_s, p, layer)
    state_ckv = ckv[:N_PROMPT].reshape(BATCH, SEQ, KV_RANK)
    state_krope = krp[:N_PROMPT, QK_NOPE_DIM:QK_HEAD_DIM].reshape(BATCH, SEQ, QK_ROPE_DIM)
    return x, state_ckv, state_krope


def _tile_plan(cnt8, plan_ref, start_ref):
    tm = float(TM)
    cnt_row = cnt8[0:1, :]
    cnt_col = cnt8.T[:, 0:1]
    tiles_row = jnp.floor((cnt_row + (tm - 1.0)) * (1.0 / tm))
    tiles_col = jnp.floor((cnt_col + (tm - 1.0)) * (1.0 / tm))
    sub = lax.broadcasted_iota(jnp.int32, (LANE, LANE), 0).astype(F32)
    lan = lax.broadcasted_iota(jnp.int32, (LANE, LANE), 1).astype(F32)
    tile_end_row = jnp.sum(jnp.where(sub <= lan, tiles_col, 0.0), axis=0, keepdims=True)
    tile_end_col = jnp.sum(jnp.where(lan <= sub, tiles_row, 0.0), axis=1, keepdims=True)
    n_used = jnp.max(tile_end_row, axis=1, keepdims=True)
    start_col = (tile_end_col - tiles_col) * tm
    end_col = start_col + cnt_col
    cand = jnp.where(jnp.logical_and(lan > sub, tiles_row > 0.0), lan, float(LANE))
    next_col = jnp.min(cand, axis=1, keepdims=True)
    next_col = jnp.where(next_col < float(LANE), next_col, -1.0)
    n_lanes = plan_ref.shape[1]
    tidx = jnp.minimum(lax.broadcasted_iota(jnp.int32, (LANE, n_lanes), 1).astype(F32), n_used - 1.0)
    esub = lax.broadcasted_iota(jnp.int32, (LANE, n_lanes), 0).astype(F32)
    te_row = jnp.sum(jnp.where(tile_end_col <= tidx, 1.0, 0.0), axis=0, keepdims=True)
    mine = esub == te_row
    end_at = jnp.sum(jnp.where(mine, end_col, 0.0), axis=0, keepdims=True)
    tv_row = jnp.clip(end_at - tidx[0:1, :] * tm, 0.0, tm)
    nx_row = jnp.sum(jnp.where(mine, next_col, 0.0), axis=0, keepdims=True)
    nu_row = jnp.broadcast_to(n_used, (1, n_lanes))
    plan_ref[...] = jnp.concatenate([te_row, tv_row, nx_row, nu_row, jnp.zeros((SUBLANE - 4, n_lanes), F32)], axis=0)
    start_ref[...] = jnp.broadcast_to(start_col, (LANE, LANE))


def _route_kernel(x_ref, mod_ref, g_ref, wrt_ref, brt_ref, triu_ref, h_ref, metat_ref, plan_ref, start_ref, carry):
    i = pl.program_id(0)

    @pl.when(i == 0)
    def _():
        carry[...] = jnp.zeros_like(carry)

    x = x_ref[...]
    m = mod_ref[0]
    h = _rms(x, g_ref[...]) * (1.0 + m[4:5]) + m[3:4]
    h_ref[...] = _pack_bf16_pairs(h)
    logits = _dot_nt(wrt_ref[...], h.astype(BF16)) + brt_ref[...]
    n_tok = logits.shape[1]
    esub = lax.broadcasted_iota(jnp.int32, logits.shape, 0).astype(F32)
    work = logits
    sel = jnp.zeros(logits.shape, F32)
    hits, firsts, tops = [], [], []
    for k in range(TOP_K):
        mk = jnp.max(work, axis=0, keepdims=True)
        first = jnp.min(jnp.where(work == mk, esub, float(N_EXPERTS)), axis=0, keepdims=True)
        hit = esub == first
        sel = jnp.where(hit, 1.0, sel)
        work = jnp.where(hit, -jnp.inf, work)
        hits.append(hit)
        firsts.append(first)
        tops.append(mk)
    es = [jnp.exp(t - tops[0]) for t in tops]
    denom = es[0] + es[1] + es[2] + es[3]
    pos = _dot(sel.astype(BF16), triu_ref[...]) + carry[:, 0:1]
    carry[...] = carry[...] + jnp.sum(sel, axis=1, keepdims=True)

    @pl.when(i == pl.num_programs(0) - 1)
    def _():
        counts = jnp.concatenate([carry[...], jnp.zeros((LANE - N_EXPERTS, LANE), F32)], axis=0).T
        _tile_plan(counts[:SUBLANE, :], plan_ref, start_ref)

    ranks = [jnp.sum(jnp.where(hit, pos, 0.0), axis=0, keepdims=True) for hit in hits]
    gates = [e / denom for e in es]
    metat_ref[...] = jnp.concatenate(firsts + gates + ranks + [jnp.zeros((SUBLANE // 2, n_tok), F32)], axis=0)


def _route(x, p, layer):
    tb = ROUTE_TB
    return pl.pallas_call(
        _route_kernel,
        grid=(N_TOK // tb,),
        in_specs=[
            pl.BlockSpec((tb, D_MODEL), lambda i: (i, 0)),
            _mod_spec(tb, layer),
            _layer_spec((1, D_MODEL), layer),
            _layer_spec((N_EXPERTS, D_MODEL), layer),
            _layer_spec((N_EXPERTS, 1), layer),
            _const_spec((tb, tb)),
        ],
        out_specs=[
            pl.BlockSpec((tb, D_MODEL // 2), lambda i: (i, 0)),
            pl.BlockSpec((2 * SUBLANE, tb), lambda i: (0, i)),
            _const_spec((SUBLANE, PLAN_LANES)),
            _const_spec((LANE, LANE)),
        ],
        out_shape=[
            jax.ShapeDtypeStruct((N_TOK, D_MODEL // 2), jnp.uint32),
            jax.ShapeDtypeStruct((2 * SUBLANE, N_TOK), F32),
            jax.ShapeDtypeStruct((SUBLANE, PLAN_LANES), F32),
            jax.ShapeDtypeStruct((LANE, LANE), F32),
        ],
        scratch_shapes=[pltpu.VMEM((N_EXPERTS, LANE), F32)],
        compiler_params=_cparams(("arbitrary",)),
        name="moe_route",
    )(x, p["mod"], p["norm_ffn_g"], p["e_w_router_t"], p["e_b_router"], p["triu"])


def _slots_kernel(start_ref, metat_ref, dest_ref):
    start_col = start_ref[:, 0:1]
    esub = lax.broadcasted_iota(jnp.int32, (LANE, SLOT_TB), 0).astype(F32)
    rows = []
    for k in range(TOP_K):
        e = metat_ref[k:k + 1, :]
        first = jnp.sum(jnp.where(esub == e, start_col, 0.0), axis=0, keepdims=True)
        rows.append(first + metat_ref[2 * TOP_K + k:2 * TOP_K + k + 1, :])
    dest_ref[...] = jnp.concatenate(rows, axis=0).astype(jnp.int32)


def _slots(start, meta_t):
    return pl.pallas_call(
        _slots_kernel,
        grid=(N_TOK // SLOT_TB,),
        in_specs=[
            _const_spec((LANE, LANE)),
            pl.BlockSpec((2 * SUBLANE, SLOT_TB), lambda i: (0, i)),
        ],
        out_specs=pl.BlockSpec((TOP_K, SLOT_TB), lambda i: (0, i)),
        out_shape=jax.ShapeDtypeStruct((TOP_K, N_TOK), jnp.int32),
        compiler_params=_cparams(("parallel",)),
        name="moe_slots",
    )(start, meta_t)


def _sc_gather(table, idx, ch):
    b, w = idx.shape[0], table.shape[1]
    per_w = b // SC_WORKERS
    n_ch = per_w // ch
    assert per_w * SC_WORKERS == b and n_ch * ch == per_w and n_ch % 2 == 0
    mesh = plsc.VectorSubcoreMesh(core_axis_name="c", subcore_axis_name="s")

    @functools.partial(
        pl.kernel, mesh=mesh,
        out_type=jax.ShapeDtypeStruct((b, w), table.dtype),
        scratch_types=[
            pltpu.VMEM((n_ch, ch), jnp.int32),
            pltpu.VMEM((ch, w), table.dtype),
            pltpu.VMEM((ch, w), table.dtype),
            pltpu.SemaphoreType.DMA, pltpu.SemaphoreType.DMA,
            pltpu.SemaphoreType.DMA, pltpu.SemaphoreType.DMA,
        ],
        name="sc_row_gather",
    )
    def gather_rows(table_hbm, idx_hbm, out_hbm, idx_v, buf0, buf1, g0, g1, s0, s1):
        wid = lax.axis_index("s") * SC_CORES + lax.axis_index("c")
        base = wid * per_w
        pltpu.sync_copy(idx_hbm.at[wid], idx_v)

        def gather(j, buf, sem):
            return pltpu.make_async_copy(table_hbm.at[idx_v.at[j]], buf, sem)

        def store(j, buf, sem):
            return pltpu.make_async_copy(buf, out_hbm.at[pl.ds(base + j * ch, ch)], sem)

        gather(0, buf0, g0).start()

        @pl.loop(0, n_ch, step=2)
        def _(j):
            @pl.when(j > 0)
            def _():
                store(j - 1, buf1, s1).wait()

            gather(j + 1, buf1, g1).start()
            gather(j, buf0, g0).wait()
            store(j, buf0, s0).start()
            gather(j + 1, buf1, g1).wait()
            store(j + 1, buf1, s1).start()
            store(j, buf0, s0).wait()

            @pl.when(j + 2 < n_ch)
            def _():
                gather(j + 2, buf0, g0).start()

        store(n_ch - 1, buf1, s1).wait()

    return gather_rows(table, idx.reshape(SC_WORKERS, n_ch, ch))


def _sc_dispatch(rows, dest_t, ch):
    n, w = rows.shape
    per_w = n // SC_WORKERS
    n_ch = per_w // ch
    assert per_w * SC_WORKERS == n and n_ch * ch == per_w and n_ch % 2 == 0
    mesh = plsc.VectorSubcoreMesh(core_axis_name="c", subcore_axis_name="s")
    idx = dest_t.reshape(TOP_K, SC_WORKERS, n_ch, ch)

    @functools.partial(
        pl.kernel, mesh=mesh,
        out_type=jax.ShapeDtypeStruct((N_SLOTS, w), rows.dtype),
        scratch_types=[
            pltpu.VMEM((TOP_K * n_ch, ch), jnp.int32),
            pltpu.VMEM((ch, w), rows.dtype),
            pltpu.VMEM((ch, w), rows.dtype),
            pltpu.SemaphoreType.DMA, pltpu.SemaphoreType.DMA,
            pltpu.SemaphoreType.DMA, pltpu.SemaphoreType.DMA,
        ],
        name="sc_row_dispatch",
    )
    def dispatch_rows(rows_hbm, idx_hbm, out_hbm, idx_v, buf0, buf1, l0, l1, s0, s1):
        wid = lax.axis_index("s") * SC_CORES + lax.axis_index("c")
        base = wid * per_w
        for k in range(TOP_K):
            pltpu.sync_copy(idx_hbm.at[k, wid], idx_v.at[pl.ds(k * n_ch, n_ch)])

        def load(j, buf, sem):
            return pltpu.make_async_copy(rows_hbm.at[pl.ds(base + j * ch, ch)], buf, sem)

        def scatter(j, k, buf, sem):
            return pltpu.make_async_copy(buf, out_hbm.at[idx_v.at[k * n_ch + j]], sem)

        load(0, buf0, l0).start()

        @pl.loop(0, n_ch, step=2)
        def _(j):
            load(j + 1, buf1, l1).start()
            load(j, buf0, l0).wait()
            for k in range(TOP_K):
                scatter(j, k, buf0, s0).start()
            load(j + 1, buf1, l1).wait()
            for k in range(TOP_K):
                scatter(j + 1, k, buf1, s1).start()
            for k in range(TOP_K):
                scatter(j, k, buf0, s0).wait()

            @pl.when(j + 2 < n_ch)
            def _():
                load(j + 2, buf0, l0).start()

            for k in range(TOP_K):
                scatter(j + 1, k, buf1, s1).wait()

    return dispatch_rows(rows, idx)


def _deinterleave_matrix():
    src = jnp.arange(2 * LANE)[:, None]
    dst = jnp.arange(2 * LANE)[None, :]
    want = jnp.where(dst < LANE, 2 * dst, 2 * (dst - LANE) + 1)
    return (src == want).astype(BF16)


def _expert_kernel(te_ref, nu_ref, tv_ref, nx_ref, x_ref, wgu_hbm, bgu_ref, wd_hbm, bd_ref, perm_ref, o_ref,
                   wgu_st, wd_st, wgu_bf, wd_bf, sems, *, layer):
    i = pl.program_id(0)
    prev = te_ref[jnp.maximum(i - 1, 0)]
    fresh = jnp.logical_or(i == 0, te_ref[i] != prev)

    def fetch(e):
        return (pltpu.make_async_copy(wgu_hbm.at[layer, e], wgu_st, sems.at[0]),
                pltpu.make_async_copy(wd_hbm.at[layer, e], wd_st, sems.at[1]))

    @pl.when(i == 0)
    def _():
        for cp in fetch(te_ref[0]):
            cp.start()

    @pl.when(jnp.logical_and(fresh, i < nu_ref[0]))
    def _():
        for cp in fetch(te_ref[i]):
            cp.wait()
        for b in range(2 * D_FF // (2 * LANE)):
            sl = slice(b * 2 * LANE, (b + 1) * 2 * LANE)
            wgu_bf[:, sl] = _dot(wgu_st[:, sl].astype(BF16), perm_ref[...]).astype(BF16)
        wd_bf[...] = wd_st[...].astype(BF16)

        @pl.when(nx_ref[i] >= 0)
        def _():
            for cp in fetch(nx_ref[i]):
                cp.start()

    def ffn(n_rows):
        row = lax.broadcasted_iota(jnp.int32, (n_rows, D_MODEL // 2), 0)
        w = jnp.where(row < tv_ref[i], x_ref[:n_rows, :], jnp.uint32(0))
        x = _unpack_bf16_pairs(w).astype(BF16)
        gu = _dot(x, wgu_bf[...]) + bgu_ref[...]
        acts = []
        for b in range(D_FF // LANE):
            glu = jnp.minimum(gu[:, b * 2 * LANE:b * 2 * LANE + LANE], SWIGLU_LIMIT)
            lin = jnp.clip(gu[:, b * 2 * LANE + LANE:(b + 1) * 2 * LANE], -SWIGLU_LIMIT, SWIGLU_LIMIT)
            acts.append((glu * jax.nn.sigmoid(SWIGLU_ALPHA * glu) * (lin + 1.0)).astype(BF16))
        act = jnp.concatenate(acts, axis=1)
        o_ref[:n_rows, :] = _pack_bf16_pairs(_dot(act, wd_bf[...]) + bd_ref[...])

    in_use = i < nu_ref[0]

    @pl.when(jnp.logical_and(in_use, tv_ref[i] > TM // 2))
    def _():
        ffn(TM)

    @pl.when(jnp.logical_and(in_use, tv_ref[i] <= TM // 2))
    def _():
        ffn(TM // 2)


def _experts(buf, tile_expert, n_used, tile_valid, tile_next, p, layer):
    def row_idx(i, te, nu, tv, nx):
        return (jnp.minimum(i, nu[0] - 1), 0)

    def b_idx(i, te, nu, tv, nx):
        return (layer, te[i], 0, 0)

    grid_spec = pltpu.PrefetchScalarGridSpec(
        num_scalar_prefetch=4,
        grid=(N_TILES,),
        in_specs=[
            pl.BlockSpec((TM, D_MODEL // 2), row_idx),
            pl.BlockSpec(memory_space=pl.ANY),
            pl.BlockSpec((None, None, 1, 2 * D_FF), b_idx),
            pl.BlockSpec(memory_space=pl.ANY),
            pl.BlockSpec((None, None, 1, D_MODEL), b_idx),
            _const_spec((2 * LANE, 2 * LANE)),
        ],
        out_specs=pl.BlockSpec((TM, D_MODEL // 2), row_idx),
        scratch_shapes=[
            pltpu.VMEM((D_MODEL, 2 * D_FF), F32),
            pltpu.VMEM((D_FF, D_MODEL), F32),
            pltpu.VMEM((D_MODEL, 2 * D_FF), BF16),
            pltpu.VMEM((D_FF, D_MODEL), BF16),
            pltpu.SemaphoreType.DMA((2,)),
        ],
    )
    return pl.pallas_call(
        functools.partial(_expert_kernel, layer=layer),
        grid_spec=grid_spec,
        out_shape=jax.ShapeDtypeStruct((N_SLOTS, D_MODEL // 2), jnp.uint32),
        compiler_params=_cparams(("arbitrary",)),
        name="moe_experts",
    )(tile_expert, n_used, tile_valid, tile_next, buf, p["e_w_gu"], p["e_b_gu"], p["e_w_down"], p["e_b_down"],
      p["deinterleave"])


def _pack_bf16_pairs(v):
    half = v.shape[1] // 2
    bits = pltpu.bitcast(v.astype(BF16).astype(F32), jnp.uint32)
    return (bits[:, half:] & jnp.uint32(0xFFFF0000)) | (bits[:, :half] >> 16)


def _unpack_bf16_pairs(w):
    return jnp.concatenate([pltpu.bitcast(w << 16, F32), pltpu.bitcast(w & jnp.uint32(0xFFFF0000), F32)],
                           axis=1)


def _combine_kernel(x_ref, mod_ref, y_ref, mt_ref, o_ref):
    w = mt_ref[...].T[:, TOP_K:2 * TOP_K]
    y = _unpack_bf16_pairs(y_ref[0]) * w[:, 0:1]
    for k in range(1, TOP_K):
        y = y + _unpack_bf16_pairs(y_ref[k]) * w[:, k:k + 1]
    o_ref[...] = x_ref[...] + mod_ref[0][5:6] * y


def _combine(x, y4, meta_t, p, layer, part):
    n_rows = N_TOK // MOE_PARTS
    tb = TB
    first = part * n_rows // tb
    return pl.pallas_call(
        _combine_kernel,
        grid=(n_rows // tb,),
        in_specs=[
            pl.BlockSpec((tb, D_MODEL), lambda i: (i + first, 0)),
            _mod_spec(tb, layer, first),
            pl.BlockSpec((TOP_K, tb, D_MODEL // 2), lambda i: (0, i, 0)),
            pl.BlockSpec((2 * SUBLANE, tb), lambda i: (0, i + first)),
        ],
        out_specs=pl.BlockSpec((tb, D_MODEL), lambda i: (i, 0)),
        out_shape=jax.ShapeDtypeStruct((n_rows, D_MODEL), F32),
        compiler_params=_cparams(("parallel",)),
        name="moe_combine",
    )(x, p["mod"], y4, meta_t)


def _moe_layer(x, p, layer):
    hp, meta_t, plan, start = _route(x, p, layer)
    plan = plan[:4, :N_TILES].astype(jnp.int32)
    tile_expert, tile_valid, tile_next, n_used = plan[0], plan[1], plan[2], plan[3, :1]
    dest_t = _slots(start, meta_t)
    buf = _sc_dispatch(hp, dest_t, SC_CHUNK_ROWS)
    yb = _experts(buf, tile_expert, n_used, tile_valid, tile_next, p, layer)
    n_rows = N_TOK // MOE_PARTS
    outs = []
    for part in range(MOE_PARTS):
        idx = dest_t[:, part * n_rows:(part + 1) * n_rows].reshape(-1)
        y4 = _sc_gather(yb, idx, SC_CHUNK_ROWS).reshape(TOP_K, n_rows, D_MODEL // 2)
        outs.append(_combine(x, y4, meta_t, p, layer, part))
    return outs


def _prepare(c, cache_ckv, cache_krope, c_ctx, norm_mix_g, norm_ffn_g, w_mod, b_mod, g_w_in, g_b_in, g_norm_v,
             g_w_s, g_b_s, g_w_out, m_w_dq, m_norm_q, m_w_uq, m_w_dkv, m_norm_kv, m_w_ukv, m_qk_norm_q,
             m_qk_norm_k, m_w_o, e_w_router, e_b_router, e_w_gu, e_b_gu, e_w_down, e_b_down):
    n_mla = m_w_dq.shape[0]
    cond = jnp.concatenate([c_ctx[None, :], c, jnp.zeros((SUBLANE - N_COND, D_MODEL), F32)], axis=0)
    wdkv = jnp.concatenate([m_w_dkv[..., :KV_RANK], jnp.zeros((n_mla, D_MODEL, QK_NOPE_DIM), F32),
                            m_w_dkv[..., KV_RANK:], jnp.zeros((n_mla, D_MODEL, HEAD_PAD - QK_HEAD_DIM), F32)],
                           axis=-1)
    w_uq = jnp.pad(m_w_uq.reshape(n_mla, Q_RANK, N_HEADS, QK_HEAD_DIM),
                   ((0, 0), (0, 0), (0, 0), (0, HEAD_PAD - QK_HEAD_DIM))).reshape(n_mla, Q_RANK, QKV_W)

    def gain_col(g):
        return jnp.pad(g, ((0, 0), (0, HEAD_PAD - QK_HEAD_DIM)))[:, :, None]

    return {
        "mod": _modulation(cond, w_mod, b_mod),
        "rope_t": _rope_tables(),
        "norm_mix_g": norm_mix_g[:, None, :],
        "norm_ffn_g": norm_ffn_g[:, None, :],
        "g_w_in": g_w_in.astype(BF16),
        "g_b_in": g_b_in[:, None, :],
        "g_norm_v": g_norm_v[:, None, :],
        "g_w_s": g_w_s.astype(BF16),
        "g_b_st": jnp.swapaxes(g_b_s, 1, 2),
        "g_w_out": g_w_out.astype(BF16),
        "m_w_dq": m_w_dq.astype(BF16),
        "m_norm_q": m_norm_q[:, None, :],
        "m_w_uq_t": jnp.swapaxes(w_uq, 1, 2).astype(BF16),
        "m_gq": gain_col(m_qk_norm_q),
        "m_w_dkv": wdkv.astype(BF16),
        "m_norm_kv": m_norm_kv[:, None, :],
        "m_w_ukv_t": jnp.swapaxes(m_w_ukv, 1, 2).astype(BF16),
        "m_gk": gain_col(m_qk_norm_k),
        "m_w_o": m_w_o.astype(BF16),
        "cache_ckv": cache_ckv,
        "cache_krp": jnp.pad(cache_krope, ((0, 0), (0, 0), (0, 0), (QK_NOPE_DIM, HEAD_PAD - QK_HEAD_DIM))),
        "e_w_router_t": jnp.swapaxes(e_w_router, 1, 2).astype(BF16),
        "e_b_router": e_b_router[:, :, None],
        "triu": jnp.tri(ROUTE_TB, ROUTE_TB, -1, dtype=BF16).T,
        "e_w_gu": e_w_gu,
        "e_b_gu": e_b_gu.reshape(DEPTH, N_EXPERTS, D_FF // LANE, LANE, 2).swapaxes(3, 4).reshape(
            DEPTH, N_EXPERTS, 1, 2 * D_FF),
        "e_w_down": e_w_down,
        "e_b_down": e_b_down[:, :, None, :],
        "deinterleave": _deinterleave_matrix(),
    }


def kernel(x_prompt, x_sample, c, cache_ckv, cache_krope, c_ctx, norm_mix_g, norm_ffn_g, w_mod, b_mod,
           g_w_in, g_b_in, g_norm_v, g_w_s, g_b_s, g_w_out, m_w_dq, m_norm_q, m_w_uq, m_w_dkv,
           m_norm_kv, m_w_ukv, m_qk_norm_q, m_qk_norm_k, m_w_o, e_w_router, e_b_router, e_w_gu,
           e_b_gu, e_w_down, e_b_down):
    p = _prepare(c, cache_ckv, cache_krope, c_ctx, norm_mix_g, norm_ffn_g, w_mod, b_mod, g_w_in, g_b_in,
                 g_norm_v, g_w_s, g_b_s, g_w_out, m_w_dq, m_norm_q, m_w_uq, m_w_dkv, m_norm_kv, m_w_ukv,
                 m_qk_norm_q, m_qk_norm_k, m_w_o, e_w_router, e_b_router, e_w_gu, e_b_gu, e_w_down, e_b_down)
    assert MOE_PARTS == 2 and N_PROMPT == N_SAMPLE
    x_p, x_s = x_prompt.reshape(N_PROMPT, D_MODEL), x_sample.reshape(N_SAMPLE, D_MODEL)
    ckv_states, krope_states = [], []
    for layer in range(DEPTH):
        if layer % 2 == 0:
            x = _gmlp_layer(x_p, x_s, p, layer)
        else:
            x, s_ckv, s_krope = _mla_layer(x_p, x_s, p, layer)
            ckv_states.append(s_ckv)
            krope_states.append(s_krope)
        x_p, x_s = _moe_layer(x, p, layer)
    y_prompt = x_p.reshape(BATCH, SEQ, D_MODEL)
    y_sample = x_s.reshape(DEC_BATCH, DEC_SEQ, D_MODEL)
    return (y_prompt, y_sample, jnp.stack(ckv_states, axis=1), jnp.stack(krope_states, axis=1))
```

```python
import functools
import math

import jax
import jax.numpy as jnp
from jax import lax
from jax.experimental import pallas as pl
from jax.experimental.pallas import tpu as pltpu
from jax.experimental.pallas import tpu_sc as plsc

F32 = jnp.float32
BF16 = jnp.bfloat16

D_MODEL = 1024
BATCH = 32
SEQ = 256
DEPTH = 4
DEC_BATCH = 2
DEC_SEQ = 4096
PAST_LEN = 512
GRID_W = 64
RMS_EPS = 1e-6
GMLP_WIDTH = 2 * D_MODEL
GMLP_GROUPS = 8
GROUP_W = GMLP_WIDTH // GMLP_GROUPS
CHUNK = 128
N_HEADS = 16
QK_NOPE_DIM = 64
QK_ROPE_DIM = 32
QK_HEAD_DIM = QK_NOPE_DIM + QK_ROPE_DIM
V_HEAD_DIM = 64
Q_RANK = 256
KV_RANK = 128
ROPE_THETA = 10000.0
N_EXPERTS = 32
TOP_K = 4
D_FF = D_MODEL
SWIGLU_LIMIT = 7.0
SWIGLU_ALPHA = 1.702

N_PROMPT = BATCH * SEQ
N_SAMPLE = DEC_BATCH * DEC_SEQ
N_TOK = N_PROMPT + N_SAMPLE
N_COND = 1 + DEC_BATCH
KV_LEN = PAST_LEN + DEC_SEQ
N_KV_ROWS = DEC_BATCH * KV_LEN + N_PROMPT

LANE = 128
SUBLANE = 8
HEAD_PAD = LANE
QKV_W = N_HEADS * HEAD_PAD
VMEM_LIMIT = 56 * 1024 * 1024

TB = 512
MOD_TN = 1536
TQ = 512
TKC = 256
ATTN_CHUNKS_PER_ITER = 9
TM = 1024
N_TILES = N_TOK * TOP_K // TM + N_EXPERTS
N_SLOTS = N_TILES * TM
PLAN_LANES = -(-N_TILES // LANE) * LANE
SLOT_TB = 2048
ROUTE_TB = 512
MOE_PARTS = 2
SC_CORES = 2
SC_SUBCORES = 16
SC_WORKERS = SC_CORES * SC_SUBCORES
SC_CHUNK_ROWS = 64
ROPE_BLOCKS = DEC_SEQ // TB
NEG_BIG = -1e30


def _cparams(sem):
    return pltpu.CompilerParams(dimension_semantics=sem, vmem_limit_bytes=VMEM_LIMIT)


def _cond_of_block(i, tb):
    n_p = N_PROMPT // tb
    per = DEC_SEQ // tb
    return jnp.where(i < n_p, 0, 1 + (i - n_p) // per)


def _rms(x, g, n=None):
    n = x.shape[-1] if n is None else n
    ss = jnp.sum(x * x, axis=-1, keepdims=True) * (1.0 / n)
    return x * lax.rsqrt(ss + RMS_EPS) * g


def _dot(a, b):
    return jnp.dot(a, b, preferred_element_type=F32)


def _mod_kernel(c_ref, w_ref, b_ref, o_ref):
    c = c_ref[...]
    s = c * jax.nn.sigmoid(c)
    o_ref[0] = _dot(s.astype(BF16), w_ref[0].astype(BF16)) + b_ref[0]


def _modulation(cond, w_mod, b_mod):
    tn = MOD_TN
    out = pl.pallas_call(
        _mod_kernel,
        grid=(DEPTH, 6 * D_MODEL // tn),
        in_specs=[
            pl.BlockSpec((SUBLANE, D_MODEL), lambda l, j: (0, 0)),
            pl.BlockSpec((1, D_MODEL, tn), lambda l, j: (l, 0, j)),
            pl.BlockSpec((1, 1, tn), lambda l, j: (l, 0, j)),
        ],
        out_specs=pl.BlockSpec((1, SUBLANE, tn), lambda l, j: (l, 0, j)),
        out_shape=jax.ShapeDtypeStruct((DEPTH, SUBLANE, 6 * D_MODEL), F32),
        compiler_params=_cparams(("parallel", "parallel")),
        name="adaln_mod",
    )(cond, w_mod, b_mod.reshape(DEPTH, 1, 6 * D_MODEL))
    m = out[:, :N_COND].reshape(DEPTH, N_COND, 6, D_MODEL)
    return jnp.pad(m, ((0, 0), (0, 0), (0, SUBLANE - 6), (0, 0)))


def _mod_spec(tb, layer, first_block=0):
    return pl.BlockSpec((None, 1, SUBLANE, D_MODEL),
                        lambda i: (layer, _cond_of_block(i + first_block, tb), 0, 0))


def _const_spec(shape):
    nd = len(shape)
    return pl.BlockSpec(shape, lambda *_: (0,) * nd)


def _layer_spec(shape, j):
    nd = len(shape)
    return pl.BlockSpec((None,) + tuple(shape), lambda *_: (j,) + (0,) * nd)


def _gelu_tanh(x):
    a = math.sqrt(2.0 / math.pi)
    hx = 0.5 * x
    return hx + hx * jnp.tanh(x * (a + (0.044715 * a) * (x * x)))


def _gmlp_kernel(xp_ref, xs_ref, mod_ref, g_ref, win_ref, bin_ref, gv_ref, ws_ref, bst_ref, wout_ref, o_ref):
    x = jnp.where(pl.program_id(0) < N_PROMPT // TB, xp_ref[...], xs_ref[...])
    m = mod_ref[0]
    h = _rms(x, g_ref[...]) * (1.0 + m[1:2]) + m[0:1]
    hb = h.astype(BF16)
    zv = _gelu_tanh(_dot(hb, win_ref[:, GMLP_WIDTH:]) + bin_ref[:, GMLP_WIDTH:])
    vn = _rms(zv, gv_ref[...]).astype(BF16)
    rows = []
    for c in range(TB // CHUNK):
        cols = []
        for g in range(GMLP_GROUPS):
            blk = vn[c * CHUNK:(c + 1) * CHUNK, g * GROUP_W:(g + 1) * GROUP_W]
            cols.append(_dot(ws_ref[g], blk) + bst_ref[:, g:g + 1])
        rows.append(jnp.concatenate(cols, axis=1))
    vm = jnp.concatenate(rows, axis=0)
    u = _gelu_tanh(_dot(hb, win_ref[:, :GMLP_WIDTH]) + bin_ref[:, :GMLP_WIDTH])
    d = _dot((u * vm).astype(BF16), wout_ref[...])
    o_ref[...] = x + m[2:3] * d


def _split_rows_specs():
    n_pb = N_PROMPT // TB
    return [pl.BlockSpec((TB, D_MODEL), lambda i: (jnp.minimum(i, n_pb - 1), 0)),
            pl.BlockSpec((TB, D_MODEL), lambda i: (jnp.maximum(i - n_pb, 0), 0))]


def _gmlp_layer(x_p, x_s, p, layer):
    j = layer // 2
    return pl.pallas_call(
        _gmlp_kernel,
        grid=(N_TOK // TB,),
        in_specs=_split_rows_specs() + [
            _mod_spec(TB, layer),
            _layer_spec((1, D_MODEL), layer),
            _layer_spec((D_MODEL, 2 * GMLP_WIDTH), j),
            _layer_spec((1, 2 * GMLP_WIDTH), j),
            _layer_spec((1, GMLP_WIDTH), j),
            _layer_spec((GMLP_GROUPS, CHUNK, CHUNK), j),
            _layer_spec((CHUNK, GMLP_GROUPS), j),
            _layer_spec((GMLP_WIDTH, D_MODEL), j),
        ],
        out_specs=pl.BlockSpec((TB, D_MODEL), lambda i: (i, 0)),
        out_shape=jax.ShapeDtypeStruct((N_TOK, D_MODEL), F32),
        compiler_params=_cparams(("parallel",)),
        name="gmlp_mixer",
    )(x_p, x_s, p["mod"], p["norm_mix_g"], p["g_w_in"], p["g_b_in"], p["g_norm_v"], p["g_w_s"], p["g_b_st"],
      p["g_w_out"])


def _rope_tables():
    t = jnp.arange(DEC_SEQ)
    row_id = (t // GRID_W).astype(F32)
    col_id = (t % GRID_W).astype(F32)
    axis_dim = QK_ROPE_DIM // 2
    inv_freq = ROPE_THETA ** (-jnp.arange(0, axis_dim, 2, dtype=F32) / axis_dim)
    ang = jnp.stack([row_id[:, None] * inv_freq, col_id[:, None] * inv_freq], axis=1)
    cos, sin = jnp.cos(ang), jnp.sin(ang)
    zeros = jnp.zeros_like(sin)
    cos_l = jnp.concatenate([cos, cos], axis=-1).reshape(DEC_SEQ, QK_ROPE_DIM)
    s1_l = jnp.concatenate([-sin, zeros], axis=-1).reshape(DEC_SEQ, QK_ROPE_DIM)
    s2_l = jnp.concatenate([zeros, sin], axis=-1).reshape(DEC_SEQ, QK_ROPE_DIM)

    def widen(rope_part, nope_fill):
        left = jnp.full((DEC_SEQ, QK_NOPE_DIM), nope_fill, F32)
        right = jnp.zeros((DEC_SEQ, HEAD_PAD - QK_HEAD_DIM), F32)
        return jnp.concatenate([left, rope_part, right], axis=-1)

    pos = jnp.stack([widen(cos_l, 1.0), widen(s1_l, 0.0), widen(s2_l, 0.0)])
    ident_c = jnp.concatenate([jnp.ones((TB, QK_HEAD_DIM), F32),
                               jnp.zeros((TB, HEAD_PAD - QK_HEAD_DIM), F32)], axis=-1)
    ident = jnp.stack([ident_c, jnp.zeros_like(ident_c), jnp.zeros_like(ident_c)])
    return jnp.swapaxes(jnp.concatenate([pos, ident], axis=1), 1, 2)


def _dot_nt(a, b):
    return lax.dot_general(a, b, (((1,), (1,)), ((), ())), preferred_element_type=F32)


def _shift_rows(x, n):
    n = n % x.shape[0]
    return jnp.concatenate([x[n:], x[:n]], axis=0)


def _head_norm_rope(nope, rope, g_ref, tabt_ref):
    half = QK_ROPE_DIM // 4
    lo, hi = QK_NOPE_DIM, QK_HEAD_DIM
    ss = (jnp.sum(nope * nope, axis=0, keepdims=True) + jnp.sum(rope * rope, axis=0, keepdims=True))
    rs = lax.rsqrt(ss * (1.0 / QK_HEAD_DIM) + RMS_EPS)
    xn = nope * rs * g_ref[:lo, :]
    xr = rope * rs * g_ref[lo:hi, :]
    rot = (xr * tabt_ref[0, lo:hi, :] + _shift_rows(xr, half) * tabt_ref[1, lo:hi, :]
           + _shift_rows(xr, -half) * tabt_ref[2, lo:hi, :])
    return jnp.concatenate([xn, rot, jnp.zeros((HEAD_PAD - hi, nope.shape[1]), F32)], axis=0)


def _mla_proj_kernel(xp_ref, xs_ref, mod_ref, g_ref, wdq_ref, nq_ref, wuqt_ref, gq_ref, wdkv_ref, nkv_ref,
                     tabt_ref, qt_ref, ckv_ref, krp_ref):
    x = jnp.where(pl.program_id(0) < N_PROMPT // TB, xp_ref[...], xs_ref[...])
    m = mod_ref[0]
    h = _rms(x, g_ref[...]) * (1.0 + m[1:2]) + m[0:1]
    hb = h.astype(BF16)
    cq = _rms(_dot(hb, wdq_ref[...]), nq_ref[...])
    qt = _dot_nt(wuqt_ref[...], cq.astype(BF16))
    for hd in range(N_HEADS):
        r0 = hd * HEAD_PAD
        qh = _head_norm_rope(qt[r0:r0 + QK_NOPE_DIM, :], qt[r0 + QK_NOPE_DIM:r0 + QK_HEAD_DIM, :], gq_ref, tabt_ref)
        qt_ref[r0:r0 + HEAD_PAD, :] = qh.astype(BF16)
    kva = _dot(hb, wdkv_ref[...])
    ckv_ref[...] = _rms(kva[:, :KV_RANK], nkv_ref[...])
    krp_ref[...] = kva[:, KV_RANK:]


def _mla_proj(x_p, x_s, p, layer):
    j = layer // 2
    n_pb = N_PROMPT // TB

    def tab_idx(i):
        return (0, 0, jnp.where(i < n_pb, ROPE_BLOCKS, (i - n_pb) % ROPE_BLOCKS))

    return pl.pallas_call(
        _mla_proj_kernel,
        grid=(N_TOK // TB,),
        in_specs=_split_rows_specs() + [
            _mod_spec(TB, layer),
            _layer_spec((1, D_MODEL), layer),
            _layer_spec((D_MODEL, Q_RANK), j),
            _layer_spec((1, Q_RANK), j),
            _layer_spec((QKV_W, Q_RANK), j),
            _layer_spec((HEAD_PAD, 1), j),
            _layer_spec((D_MODEL, 2 * LANE), j),
            _layer_spec((1, KV_RANK), j),
            pl.BlockSpec((3, HEAD_PAD, TB), tab_idx),
        ],
        out_specs=[
            pl.BlockSpec((QKV_W, TB), lambda i: (0, i)),
            pl.BlockSpec((TB, KV_RANK), lambda i: (i, 0)),
            pl.BlockSpec((TB, HEAD_PAD), lambda i: (i, 0)),
        ],
        out_shape=[
            jax.ShapeDtypeStruct((QKV_W, N_TOK), BF16),
            jax.ShapeDtypeStruct((N_TOK, KV_RANK), F32),
            jax.ShapeDtypeStruct((N_TOK, HEAD_PAD), F32),
        ],
        compiler_params=_cparams(("parallel",)),
        name="mla_proj",
    )(x_p, x_s, p["mod"], p["norm_mix_g"], p["m_w_dq"], p["m_norm_q"], p["m_w_uq_t"], p["m_gq"], p["m_w_dkv"],
      p["m_norm_kv"], p["rope_t"])


def _kv_block_source(i):
    n_sb = DEC_BATCH * KV_LEN // TB
    per = KV_LEN // TB
    b = jnp.minimum(i // per, DEC_BATCH - 1)
    jj = i % per
    return jnp.logical_and(i < n_sb, jj < PAST_LEN // TB), b, jj


def _kv_expand_kernel(cckv_ref, ckrp_ref, ckv_ref, krp_ref, wukvt_ref, gk_ref, tabt_ref, k_ref, vt_ref):
    from_cache, _, _ = _kv_block_source(pl.program_id(0))
    ckv = jnp.where(from_cache, cckv_ref[...], ckv_ref[...])
    krp = jnp.where(from_cache, ckrp_ref[...], krp_ref[...])
    kvt = _dot_nt(wukvt_ref[...], ckv.astype(BF16))
    k_rope = krp.T[QK_NOPE_DIM:QK_HEAD_DIM, :]
    row = lax.broadcasted_iota(jnp.int32, (QK_NOPE_DIM, TB), 0)
    ones_row = jnp.where(row == 0, 1.0, 0.0)
    for hd in range(N_HEADS):
        r0 = hd * HEAD_PAD
        kh = _head_norm_rope(kvt[r0:r0 + QK_NOPE_DIM, :], k_rope, gk_ref, tabt_ref)
        k_ref[:, r0:r0 + HEAD_PAD] = kh.T.astype(BF16)
        vt_ref[r0:r0 + HEAD_PAD, :] = jnp.concatenate([ones_row, kvt[r0 + QK_NOPE_DIM:r0 + HEAD_PAD, :]],
                                                      axis=0).astype(BF16)


def _kv_expand(ckv, krp, p, layer):
    j = layer // 2
    n_sb = DEC_BATCH * KV_LEN // TB
    n_cache = PAST_LEN // TB
    n_pb = N_PROMPT // TB
    lat_blocks = DEC_SEQ // TB

    def tab_idx(i):
        from_cache, _, jj = _kv_block_source(i)
        return (0, 0, jnp.where((i >= n_sb) | from_cache, ROPE_BLOCKS, jj - n_cache))

    def cache_idx(i):
        _, b, jj = _kv_block_source(i)
        return (b, j, jnp.minimum(jj, n_cache - 1), 0)

    def tok_idx(i):
        _, b, jj = _kv_block_source(i)
        latent = n_pb + b * lat_blocks + jnp.maximum(jj - n_cache, 0)
        return (jnp.where(i < n_sb, latent, i - n_sb), 0)

    return pl.pallas_call(
        _kv_expand_kernel,
        grid=(N_KV_ROWS // TB,),
        in_specs=[
            pl.BlockSpec((None, None, TB, KV_RANK), cache_idx),
            pl.BlockSpec((None, None, TB, HEAD_PAD), cache_idx),
            pl.BlockSpec((TB, KV_RANK), tok_idx),
            pl.BlockSpec((TB, HEAD_PAD), tok_idx),
            _layer_spec((QKV_W, KV_RANK), j),
            _layer_spec((HEAD_PAD, 1), j),
            pl.BlockSpec((3, HEAD_PAD, TB), tab_idx),
        ],
        out_specs=[
            pl.BlockSpec((TB, QKV_W), lambda i: (i, 0)),
            pl.BlockSpec((QKV_W, TB), lambda i: (0, i)),
        ],
        out_shape=[
            jax.ShapeDtypeStruct((N_KV_ROWS, QKV_W), BF16),
            jax.ShapeDtypeStruct((QKV_W, N_KV_ROWS), BF16),
        ],
        compiler_params=_cparams(("parallel",)),
        name="mla_kv_expand",
    )(p["cache_ckv"], p["cache_krp"], ckv, krp, p["m_w_ukv_t"], p["m_gk"], p["rope_t"])


def _attn_kernel(qt_ref, k_ref, vt_ref, o_ref, s_a, s_b, *, t_k, hps, qps, tq):
    c = (1.0 / math.sqrt(QK_HEAD_DIM)) * math.log2(math.e)
    kc = min(TKC, t_k)
    n_chunks = t_k // kc
    per_it = min(ATTN_CHUNKS_PER_ITER, n_chunks)
    n_it = n_chunks // per_it
    bufs = (s_a, s_b)

    units = [(qb, h) for qb in range(qps) for h in range(hps)]

    def rows(h):
        return slice(h * HEAD_PAD, (h + 1) * HEAD_PAD)

    def cols(qb):
        return slice(qb * tq, (qb + 1) * tq)

    def scores(u, off, m8):
        qb, h = units[u]
        st = _dot(k_ref[pl.ds(off, kc), rows(h)], qt_ref[rows(h), cols(qb)])
        bufs[u % 2][pl.ds(off, kc), :] = st
        return jnp.maximum(m8, jnp.max(st.reshape(kc // SUBLANE, SUBLANE, tq), axis=0))

    def weigh(u, off, m, acc):
        pt = jnp.exp2((bufs[u % 2][pl.ds(off, kc), :] - m) * c).astype(BF16)
        return acc + _dot(vt_ref[rows(units[u][1]), pl.ds(off, kc)], pt)

    def phase(u_scores, u_weigh, m):
        def body(it, carry):
            m8, acc = carry
            for j in range(per_it):
                off = (it * per_it + j) * kc
                off = off if isinstance(off, int) else pl.multiple_of(off, kc)
                if u_scores is not None:
                    m8 = scores(u_scores, off, m8)
                if u_weigh is not None:
                    acc = weigh(u_weigh, off, m, acc)
            return m8, acc

        init = (jnp.full((SUBLANE, tq), NEG_BIG, F32), jnp.zeros((HEAD_PAD, tq), F32))
        return body(0, init) if n_it == 1 else lax.fori_loop(0, n_it, body, init)

    outs = []
    if n_chunks == 1:
        sts = [_dot(k_ref[:, rows(h)], qt_ref[rows(h), cols(qb)]) for qb, h in units]
        for (qb, h), st in zip(units, sts):
            m = jnp.max(st, axis=0, keepdims=True)
            acc = _dot(vt_ref[rows(h), :], jnp.exp2((st - m) * c).astype(BF16))
            outs.append(acc[QK_NOPE_DIM:, :] / acc[0:1, :])
    else:
        m8, _ = phase(0, None, None)
        for u in range(1, len(units) + 1):
            m = jnp.max(m8, axis=0, keepdims=True)
            m8, acc = phase(u if u < len(units) else None, u - 1, m)
            outs.append(acc[QK_NOPE_DIM:, :] / acc[0:1, :])
    for qb in range(qps):
        o_ref[cols(qb), :] = jnp.concatenate(outs[qb * hps:(qb + 1) * hps], axis=0).T.astype(BF16)


def _attention(qt, k, vt, *, n_batch, t_q, t_k, q_row0, kv_row0, hps, qps, tq_unit):
    tq = tq_unit * qps
    nq = t_q // tq
    q0 = q_row0 // tq
    k0 = kv_row0 // t_k
    return pl.pallas_call(
        functools.partial(_attn_kernel, t_k=t_k, hps=hps, qps=qps, tq=tq_unit),
        grid=(n_batch, N_HEADS // hps, nq),
        in_specs=[
            pl.BlockSpec((hps * HEAD_PAD, tq), lambda b, h, i: (h, q0 + b * nq + i)),
            pl.BlockSpec((t_k, hps * HEAD_PAD), lambda b, h, i: (k0 + b, h)),
            pl.BlockSpec((hps * HEAD_PAD, t_k), lambda b, h, i: (h, k0 + b)),
        ],
        out_specs=pl.BlockSpec((tq, hps * V_HEAD_DIM), lambda b, h, i: (b * nq + i, h)),
        out_shape=jax.ShapeDtypeStruct((n_batch * t_q, N_HEADS * V_HEAD_DIM), BF16),
        scratch_shapes=[pltpu.VMEM((t_k, tq_unit), F32), pltpu.VMEM((t_k, tq_unit), F32)],
        compiler_params=_cparams(("parallel", "parallel", "parallel")),
        name=f"mla_attention_tk{t_k}",
    )(qt, k, vt)


def _out_proj_kernel(ap_ref, as_ref, xp_ref, xs_ref, mod_ref, wo_ref, o_ref):
    is_prompt = pl.program_id(0) < N_PROMPT // TB
    a = jnp.where(is_prompt, ap_ref[...], as_ref[...])
    x = jnp.where(is_prompt, xp_ref[...], xs_ref[...])
    o_ref[...] = x + mod_ref[0][2:3] * _dot(a, wo_ref[...])


def _out_proj(attn_p, attn_s, x_p, x_s, p, layer):
    return pl.pallas_call(
        _out_proj_kernel,
        grid=(N_TOK // TB,),
        in_specs=_split_rows_specs() + _split_rows_specs() + [
            _mod_spec(TB, layer),
            _layer_spec((D_MODEL, D_MODEL), layer // 2),
        ],
        out_specs=pl.BlockSpec((TB, D_MODEL), lambda i: (i, 0)),
        out_shape=jax.ShapeDtypeStruct((N_TOK, D_MODEL), F32),
        compiler_params=_cparams(("parallel",)),
        name="mla_out_proj",
    )(attn_p, attn_s, x_p, x_s, p["mod"], p["m_w_o"])


def _mla_layer(x_p, x_s, p, layer):
    qt, ckv, krp = _mla_proj(x_p, x_s, p, layer)
    k, vt = _kv_expand(ckv, krp, p, layer)
    a_p = _attention(qt, k, vt, n_batch=BATCH, t_q=SEQ, t_k=SEQ, q_row0=0, kv_row0=DEC_BATCH * KV_LEN,
                     hps=N_HEADS, qps=1, tq_unit=SEQ)
    a_s = _attention(qt, k, vt, n_batch=DEC_BATCH, t_q=DEC_SEQ, t_k=KV_LEN, q_row0=N_PROMPT, kv_row0=0,
                     hps=4, qps=4, tq_unit=TQ)
    x = _out_proj(a_p, a_s, x_p, x_s, p, layer)
    state_ckv = ckv[:N_PROMPT].reshape(BATCH, SEQ, KV_RANK)
    state_krope = krp[:N_PROMPT, QK_NOPE_DIM:QK_HEAD_DIM].reshape(BATCH, SEQ, QK_ROPE_DIM)
    return x, state_ckv, state_krope


def _tile_plan(cnt8, plan_ref, start_ref):
    tm = float(TM)
    cnt_row = cnt8[0:1, :]
    cnt_col = cnt8.T[:, 0:1]
    tiles_row = jnp.floor((cnt_row + (tm - 1.0)) * (1.0 / tm))
    tiles_col = jnp.floor((cnt_col + (tm - 1.0)) * (1.0 / tm))
    sub = lax.broadcasted_iota(jnp.int32, (LANE, LANE), 0).astype(F32)
    lan = lax.broadcasted_iota(jnp.int32, (LANE, LANE), 1).astype(F32)
    tile_end_row = jnp.sum(jnp.where(sub <= lan, tiles_col, 0.0), axis=0, keepdims=True)
    tile_end_col = jnp.sum(jnp.where(lan <= sub, tiles_row, 0.0), axis=1, keepdims=True)
    n_used = jnp.max(tile_end_row, axis=1, keepdims=True)
    start_col = (tile_end_col - tiles_col) * tm
    end_col = start_col + cnt_col
    cand = jnp.where(jnp.logical_and(lan > sub, tiles_row > 0.0), lan, float(LANE))
    next_col = jnp.min(cand, axis=1, keepdims=True)
    next_col = jnp.where(next_col < float(LANE), next_col, -1.0)
    n_lanes = plan_ref.shape[1]
    tidx = jnp.minimum(lax.broadcasted_iota(jnp.int32, (LANE, n_lanes), 1).astype(F32), n_used - 1.0)
    esub = lax.broadcasted_iota(jnp.int32, (LANE, n_lanes), 0).astype(F32)
    te_row = jnp.sum(jnp.where(tile_end_col <= tidx, 1.0, 0.0), axis=0, keepdims=True)
    mine = esub == te_row
    end_at = jnp.sum(jnp.where(mine, end_col, 0.0), axis=0, keepdims=True)
    tv_row = jnp.clip(end_at - tidx[0:1, :] * tm, 0.0, tm)
    nx_row = jnp.sum(jnp.where(mine, next_col, 0.0), axis=0, keepdims=True)
    nu_row = jnp.broadcast_to(n_used, (1, n_lanes))
    plan_ref[...] = jnp.concatenate([te_row, tv_row, nx_row, nu_row, jnp.zeros((SUBLANE - 4, n_lanes), F32)], axis=0)
    start_ref[...] = jnp.broadcast_to(start_col, (LANE, LANE))


def _route_kernel(x_ref, mod_ref, g_ref, wrt_ref, brt_ref, triu_ref, h_ref, metat_ref, plan_ref, start_ref, carry):
    i = pl.program_id(0)

    @pl.when(i == 0)
    def _():
        carry[...] = jnp.zeros_like(carry)

    x = x_ref[...]
    m = mod_ref[0]
    h = _rms(x, g_ref[...]) * (1.0 + m[4:5]) + m[3:4]
    h_ref[...] = _pack_bf16_pairs(h)
    logits = _dot_nt(wrt_ref[...], h.astype(BF16)) + brt_ref[...]
    n_tok = logits.shape[1]
    esub = lax.broadcasted_iota(jnp.int32, logits.shape, 0).astype(F32)
    work = logits
    sel = jnp.zeros(logits.shape, F32)
    hits, firsts, tops = [], [], []
    for k in range(TOP_K):
        mk = jnp.max(work, axis=0, keepdims=True)
        first = jnp.min(jnp.where(work == mk, esub, float(N_EXPERTS)), axis=0, keepdims=True)
        hit = esub == first
        sel = jnp.where(hit, 1.0, sel)
        work = jnp.where(hit, -jnp.inf, work)
        hits.append(hit)
        firsts.append(first)
        tops.append(mk)
    es = [jnp.exp(t - tops[0]) for t in tops]
    denom = es[0] + es[1] + es[2] + es[3]
    pos = _dot(sel.astype(BF16), triu_ref[...]) + carry[:, 0:1]
    carry[...] = carry[...] + jnp.sum(sel, axis=1, keepdims=True)

    @pl.when(i == pl.num_programs(0) - 1)
    def _():
        counts = jnp.concatenate([carry[...], jnp.zeros((LANE - N_EXPERTS, LANE), F32)], axis=0).T
        _tile_plan(counts[:SUBLANE, :], plan_ref, start_ref)

    ranks = [jnp.sum(jnp.where(hit, pos, 0.0), axis=0, keepdims=True) for hit in hits]
    gates = [e / denom for e in es]
    metat_ref[...] = jnp.concatenate(firsts + gates + ranks + [jnp.zeros((SUBLANE // 2, n_tok), F32)], axis=0)


def _route(x, p, layer):
    tb = ROUTE_TB
    return pl.pallas_call(
        _route_kernel,
        grid=(N_TOK // tb,),
        in_specs=[
            pl.BlockSpec((tb, D_MODEL), lambda i: (i, 0)),
            _mod_spec(tb, layer),
            _layer_spec((1, D_MODEL), layer),
            _layer_spec((N_EXPERTS, D_MODEL), layer),
            _layer_spec((N_EXPERTS, 1), layer),
            _const_spec((tb, tb)),
        ],
        out_specs=[
            pl.BlockSpec((tb, D_MODEL // 2), lambda i: (i, 0)),
            pl.BlockSpec((2 * SUBLANE, tb), lambda i: (0, i)),
            _const_spec((SUBLANE, PLAN_LANES)),
            _const_spec((LANE, LANE)),
        ],
        out_shape=[
            jax.ShapeDtypeStruct((N_TOK, D_MODEL // 2), jnp.uint32),
            jax.ShapeDtypeStruct((2 * SUBLANE, N_TOK), F32),
            jax.ShapeDtypeStruct((SUBLANE, PLAN_LANES), F32),
            jax.ShapeDtypeStruct((LANE, LANE), F32),
        ],
        scratch_shapes=[pltpu.VMEM((N_EXPERTS, LANE), F32)],
        compiler_params=_cparams(("arbitrary",)),
        name="moe_route",
    )(x, p["mod"], p["norm_ffn_g"], p["e_w_router_t"], p["e_b_router"], p["triu"])


def _slots_kernel(start_ref, metat_ref, dest_ref):
    start_col = start_ref[:, 0:1]
    esub = lax.broadcasted_iota(jnp.int32, (LANE, SLOT_TB), 0).astype(F32)
    rows = []
    for k in range(TOP_K):
        e = metat_ref[k:k + 1, :]
        first = jnp.sum(jnp.where(esub == e, start_col, 0.0), axis=0, keepdims=True)
        rows.append(first + metat_ref[2 * TOP_K + k:2 * TOP_K + k + 1, :])
    dest_ref[...] = jnp.concatenate(rows, axis=0).astype(jnp.int32)


def _slots(start, meta_t):
    return pl.pallas_call(
        _slots_kernel,
        grid=(N_TOK // SLOT_TB,),
        in_specs=[
            _const_spec((LANE, LANE)),
            pl.BlockSpec((2 * SUBLANE, SLOT_TB), lambda i: (0, i)),
        ],
        out_specs=pl.BlockSpec((TOP_K, SLOT_TB), lambda i: (0, i)),
        out_shape=jax.ShapeDtypeStruct((TOP_K, N_TOK), jnp.int32),
        compiler_params=_cparams(("parallel",)),
        name="moe_slots",
    )(start, meta_t)


def _sc_gather(table, idx, ch):
    b, w = idx.shape[0], table.shape[1]
    per_w = b // SC_WORKERS
    n_ch = per_w // ch
    assert per_w * SC_WORKERS == b and n_ch * ch == per_w and n_ch % 2 == 0
    mesh = plsc.VectorSubcoreMesh(core_axis_name="c", subcore_axis_name="s")

    @functools.partial(
        pl.kernel, mesh=mesh,
        out_type=jax.ShapeDtypeStruct((b, w), table.dtype),
        scratch_types=[
            pltpu.VMEM((n_ch, ch), jnp.int32),
            pltpu.VMEM((ch, w), table.dtype),
            pltpu.VMEM((ch, w), table.dtype),
            pltpu.SemaphoreType.DMA, pltpu.SemaphoreType.DMA,
            pltpu.SemaphoreType.DMA, pltpu.SemaphoreType.DMA,
        ],
        name="sc_row_gather",
    )
    def gather_rows(table_hbm, idx_hbm, out_hbm, idx_v, buf0, buf1, g0, g1, s0, s1):
        wid = lax.axis_index("s") * SC_CORES + lax.axis_index("c")
        base = wid * per_w
        pltpu.sync_copy(idx_hbm.at[wid], idx_v)

        def gather(j, buf, sem):
            return pltpu.make_async_copy(table_hbm.at[idx_v.at[j]], buf, sem)

        def store(j, buf, sem):
            return pltpu.make_async_copy(buf, out_hbm.at[pl.ds(base + j * ch, ch)], sem)

        gather(0, buf0, g0).start()

        @pl.loop(0, n_ch, step=2)
        def _(j):
            @pl.when(j > 0)
            def _():
                store(j - 1, buf1, s1).wait()

            gather(j + 1, buf1, g1).start()
            gather(j, buf0, g0).wait()
            store(j, buf0, s0).start()
            gather(j + 1, buf1, g1).wait()
            store(j + 1, buf1, s1).start()
            store(j, buf0, s0).wait()

            @pl.when(j + 2 < n_ch)
            def _():
                gather(j + 2, buf0, g0).start()

        store(n_ch - 1, buf1, s1).wait()

    return gather_rows(table, idx.reshape(SC_WORKERS, n_ch, ch))


def _sc_dispatch(rows, dest_t, ch):
    n, w = rows.shape
    per_w = n // SC_WORKERS
    n_ch = per_w // ch
    assert per_w * SC_WORKERS == n and n_ch * ch == per_w and n_ch % 2 == 0
    mesh = plsc.VectorSubcoreMesh(core_axis_name="c", subcore_axis_name="s")
    idx = dest_t.reshape(TOP_K, SC_WORKERS, n_ch, ch)

    @functools.partial(
        pl.kernel, mesh=mesh,
        out_type=jax.ShapeDtypeStruct((N_SLOTS, w), rows.dtype),
        scratch_types=[
            pltpu.VMEM((TOP_K * n_ch, ch), jnp.int32),
            pltpu.VMEM((ch, w), rows.dtype),
            pltpu.VMEM((ch, w), rows.dtype),
            pltpu.SemaphoreType.DMA, pltpu.SemaphoreType.DMA,
            pltpu.SemaphoreType.DMA, pltpu.SemaphoreType.DMA,
        ],
        name="sc_row_dispatch",
    )
    def dispatch_rows(rows_hbm, idx_hbm, out_hbm, idx_v, buf0, buf1, l0, l1, s0, s1):
        wid = lax.axis_index("s") * SC_CORES + lax.axis_index("c")
        base = wid * per_w
        for k in range(TOP_K):
            pltpu.sync_copy(idx_hbm.at[k, wid], idx_v.at[pl.ds(k * n_ch, n_ch)])

        def load(j, buf, sem):
            return pltpu.make_async_copy(rows_hbm.at[pl.ds(base + j * ch, ch)], buf, sem)

        def scatter(j, k, buf, sem):
            return pltpu.make_async_copy(buf, out_hbm.at[idx_v.at[k * n_ch + j]], sem)

        load(0, buf0, l0).start()

        @pl.loop(0, n_ch, step=2)
        def _(j):
            load(j + 1, buf1, l1).start()
            load(j, buf0, l0).wait()
            for k in range(TOP_K):
                scatter(j, k, buf0, s0).start()
            load(j + 1, buf1, l1).wait()
            for k in range(TOP_K):
                scatter(j + 1, k, buf1, s1).start()
            for k in range(TOP_K):
                scatter(j, k, buf0, s0).wait()

            @pl.when(j + 2 < n_ch)
            def _():
                load(j + 2, buf0, l0).start()

            for k in range(TOP_K):
                scatter(j + 1, k, buf1, s1).wait()

    return dispatch_rows(rows, idx)


def _deinterleave_matrix():
    src = jnp.arange(2 * LANE)[:, None]
    dst = jnp.arange(2 * LANE)[None, :]
    want = jnp.where(dst < LANE, 2 * dst, 2 * (dst - LANE) + 1)
    return (src == want).astype(BF16)


def _expert_kernel(te_ref, nu_ref, tv_ref, nx_ref, x_ref, wgu_hbm, bgu_ref, wd_hbm, bd_ref, perm_ref, o_ref,
                   wgu_st, wd_st, wgu_bf, wd_bf, sems, *, layer):
    i = pl.program_id(0)
    prev = te_ref[jnp.maximum(i - 1, 0)]
    fresh = jnp.logical_or(i == 0, te_ref[i] != prev)

    def fetch(e):
        return (pltpu.make_async_copy(wgu_hbm.at[layer, e], wgu_st, sems.at[0]),
                pltpu.make_async_copy(wd_hbm.at[layer, e], wd_st, sems.at[1]))

    @pl.when(i == 0)
    def _():
        for cp in fetch(te_ref[0]):
            cp.start()

    @pl.when(jnp.logical_and(fresh, i < nu_ref[0]))
    def _():
        for cp in fetch(te_ref[i]):
            cp.wait()
        for b in range(2 * D_FF // (2 * LANE)):
            sl = slice(b * 2 * LANE, (b + 1) * 2 * LANE)
            wgu_bf[:, sl] = _dot(wgu_st[:, sl].astype(BF16), perm_ref[...]).astype(BF16)
        wd_bf[...] = wd_st[...].astype(BF16)

        @pl.when(nx_ref[i] >= 0)
        def _():
            for cp in fetch(nx_ref[i]):
                cp.start()

    def ffn(n_rows):
        row = lax.broadcasted_iota(jnp.int32, (n_rows, D_MODEL // 2), 0)
        w = jnp.where(row < tv_ref[i], x_ref[:n_rows, :], jnp.uint32(0))
        x = _unpack_bf16_pairs(w).astype(BF16)
        gu = _dot(x, wgu_bf[...]) + bgu_ref[...]
        acts = []
        for b in range(D_FF // LANE):
            glu = jnp.minimum(gu[:, b * 2 * LANE:b * 2 * LANE + LANE], SWIGLU_LIMIT)
            lin = jnp.clip(gu[:, b * 2 * LANE + LANE:(b + 1) * 2 * LANE], -SWIGLU_LIMIT, SWIGLU_LIMIT)
            acts.append((glu * jax.nn.sigmoid(SWIGLU_ALPHA * glu) * (lin + 1.0)).astype(BF16))
        act = jnp.concatenate(acts, axis=1)
        o_ref[:n_rows, :] = _pack_bf16_pairs(_dot(act, wd_bf[...]) + bd_ref[...])

    in_use = i < nu_ref[0]
    quarter = TM // 4
    for n_rows in range(quarter, TM + 1, quarter):
        fits = jnp.logical_and(tv_ref[i] > n_rows - quarter, tv_ref[i] <= n_rows)

        @pl.when(jnp.logical_and(in_use, fits))
        def _(n_rows=n_rows):
            ffn(n_rows)


def _experts(buf, tile_expert, n_used, tile_valid, tile_next, p, layer):
    def row_idx(i, te, nu, tv, nx):
        return (jnp.minimum(i, nu[0] - 1), 0)

    def b_idx(i, te, nu, tv, nx):
        return (layer, te[i], 0, 0)

    grid_spec = pltpu.PrefetchScalarGridSpec(
        num_scalar_prefetch=4,
        grid=(N_TILES,),
        in_specs=[
            pl.BlockSpec((TM, D_MODEL // 2), row_idx),
            pl.BlockSpec(memory_space=pl.ANY),
            pl.BlockSpec((None, None, 1, 2 * D_FF), b_idx),
            pl.BlockSpec(memory_space=pl.ANY),
            pl.BlockSpec((None, None, 1, D_MODEL), b_idx),
            _const_spec((2 * LANE, 2 * LANE)),
        ],
        out_specs=pl.BlockSpec((TM, D_MODEL // 2), row_idx),
        scratch_shapes=[
            pltpu.VMEM((D_MODEL, 2 * D_FF), F32),
            pltpu.VMEM((D_FF, D_MODEL), F32),
            pltpu.VMEM((D_MODEL, 2 * D_FF), BF16),
            pltpu.VMEM((D_FF, D_MODEL), BF16),
            pltpu.SemaphoreType.DMA((2,)),
        ],
    )
    return pl.pallas_call(
        functools.partial(_expert_kernel, layer=layer),
        grid_spec=grid_spec,
        out_shape=jax.ShapeDtypeStruct((N_SLOTS, D_MODEL // 2), jnp.uint32),
        compiler_params=_cparams(("arbitrary",)),
        name="moe_experts",
    )(tile_expert, n_used, tile_valid, tile_next, buf, p["e_w_gu"], p["e_b_gu"], p["e_w_down"], p["e_b_down"],
      p["deinterleave"])


def _pack_bf16_pairs(v):
    half = v.shape[1] // 2
    bits = pltpu.bitcast(v.astype(BF16).astype(F32), jnp.uint32)
    return (bits[:, half:] & jnp.uint32(0xFFFF0000)) | (bits[:, :half] >> 16)


def _unpack_bf16_pairs(w):
    return jnp.concatenate([pltpu.bitcast(w << 16, F32), pltpu.bitcast(w & jnp.uint32(0xFFFF0000), F32)],
                           axis=1)


def _combine_kernel(x_ref, mod_ref, y_ref, mt_ref, o_ref):
    w = mt_ref[...].T[:, TOP_K:2 * TOP_K]
    y = _unpack_bf16_pairs(y_ref[0]) * w[:, 0:1]
    for k in range(1, TOP_K):
        y = y + _unpack_bf16_pairs(y_ref[k]) * w[:, k:k + 1]
    o_ref[...] = x_ref[...] + mod_ref[0][5:6] * y


def _combine(x, y4, meta_t, p, layer, part):
    n_rows = N_TOK // MOE_PARTS
    tb = TB
    first = part * n_rows // tb
    return pl.pallas_call(
        _combine_kernel,
        grid=(n_rows // tb,),
        in_specs=[
            pl.BlockSpec((tb, D_MODEL), lambda i: (i + first, 0)),
            _mod_spec(tb, layer, first),
            pl.BlockSpec((TOP_K, tb, D_MODEL // 2), lambda i: (0, i, 0)),
            pl.BlockSpec((2 * SUBLANE, tb), lambda i: (0, i + first)),
        ],
        out_specs=pl.BlockSpec((tb, D_MODEL), lambda i: (i, 0)),
        out_shape=jax.ShapeDtypeStruct((n_rows, D_MODEL), F32),
        compiler_params=_cparams(("parallel",)),
        name="moe_combine",
    )(x, p["mod"], y4, meta_t)


def _moe_layer(x, p, layer):
    hp, meta_t, plan, start = _route(x, p, layer)
    plan = plan[:4, :N_TILES].astype(jnp.int32)
    tile_expert, tile_valid, tile_next, n_used = plan[0], plan[1], plan[2], plan[3, :1]
    dest_t = _slots(start, meta_t)
    buf = _sc_dispatch(hp, dest_t, SC_CHUNK_ROWS)
    yb = _experts(buf, tile_expert, n_used, tile_valid, tile_next, p, layer)
    n_rows = N_TOK // MOE_PARTS
    outs = []
    for part in range(MOE_PARTS):
        idx = dest_t[:, part * n_rows:(part + 1) * n_rows].reshape(-1)
        y4 = _sc_gather(yb, idx, SC_CHUNK_ROWS).reshape(TOP_K, n_rows, D_MODEL // 2)
        outs.append(_combine(x, y4, meta_t, p, layer, part))
    return outs


def _prepare(c, cache_ckv, cache_krope, c_ctx, norm_mix_g, norm_ffn_g, w_mod, b_mod, g_w_in, g_b_in, g_norm_v,
             g_w_s, g_b_s, g_w_out, m_w_dq, m_norm_q, m_w_uq, m_w_dkv, m_norm_kv, m_w_ukv, m_qk_norm_q,
             m_qk_norm_k, m_w_o, e_w_router, e_b_router, e_w_gu, e_b_gu, e_w_down, e_b_down):
    n_mla = m_w_dq.shape[0]
    cond = jnp.concatenate([c_ctx[None, :], c, jnp.zeros((SUBLANE - N_COND, D_MODEL), F32)], axis=0)
    wdkv = jnp.concatenate([m_w_dkv[..., :KV_RANK], jnp.zeros((n_mla, D_MODEL, QK_NOPE_DIM), F32),
                            m_w_dkv[..., KV_RANK:], jnp.zeros((n_mla, D_MODEL, HEAD_PAD - QK_HEAD_DIM), F32)],
                           axis=-1)
    w_uq = jnp.pad(m_w_uq.reshape(n_mla, Q_RANK, N_HEADS, QK_HEAD_DIM),
                   ((0, 0), (0, 0), (0, 0), (0, HEAD_PAD - QK_HEAD_DIM))).reshape(n_mla, Q_RANK, QKV_W)

    def gain_col(g):
        return jnp.pad(g, ((0, 0), (0, HEAD_PAD - QK_HEAD_DIM)))[:, :, None]

    return {
        "mod": _modulation(cond, w_mod, b_mod),
        "rope_t": _rope_tables(),
        "norm_mix_g": norm_mix_g[:, None, :],
        "norm_ffn_g": norm_ffn_g[:, None, :],
        "g_w_in": g_w_in.astype(BF16),
        "g_b_in": g_b_in[:, None, :],
        "g_norm_v": g_norm_v[:, None, :],
        "g_w_s": g_w_s.astype(BF16),
        "g_b_st": jnp.swapaxes(g_b_s, 1, 2),
        "g_w_out": g_w_out.astype(BF16),
        "m_w_dq": m_w_dq.astype(BF16),
        "m_norm_q": m_norm_q[:, None, :],
        "m_w_uq_t": jnp.swapaxes(w_uq, 1, 2).astype(BF16),
        "m_gq": gain_col(m_qk_norm_q),
        "m_w_dkv": wdkv.astype(BF16),
        "m_norm_kv": m_norm_kv[:, None, :],
        "m_w_ukv_t": jnp.swapaxes(m_w_ukv, 1, 2).astype(BF16),
        "m_gk": gain_col(m_qk_norm_k),
        "m_w_o": m_w_o.astype(BF16),
        "cache_ckv": cache_ckv,
        "cache_krp": jnp.pad(cache_krope, ((0, 0), (0, 0), (0, 0), (QK_NOPE_DIM, HEAD_PAD - QK_HEAD_DIM))),
        "e_w_router_t": jnp.swapaxes(e_w_router, 1, 2).astype(BF16),
        "e_b_router": e_b_router[:, :, None],
        "triu": jnp.tri(ROUTE_TB, ROUTE_TB, -1, dtype=BF16).T,
        "e_w_gu": e_w_gu,
        "e_b_gu": e_b_gu.reshape(DEPTH, N_EXPERTS, D_FF // LANE, LANE, 2).swapaxes(3, 4).reshape(
            DEPTH, N_EXPERTS, 1, 2 * D_FF),
        "e_w_down": e_w_down,
        "e_b_down": e_b_down[:, :, None, :],
        "deinterleave": _deinterleave_matrix(),
    }


def kernel(x_prompt, x_sample, c, cache_ckv, cache_krope, c_ctx, norm_mix_g, norm_ffn_g, w_mod, b_mod,
           g_w_in, g_b_in, g_norm_v, g_w_s, g_b_s, g_w_out, m_w_dq, m_norm_q, m_w_uq, m_w_dkv,
           m_norm_kv, m_w_ukv, m_qk_norm_q, m_qk_norm_k, m_w_o, e_w_router, e_b_router, e_w_gu,
           e_b_gu, e_w_down, e_b_down):
    p = _prepare(c, cache_ckv, cache_krope, c_ctx, norm_mix_g, norm_ffn_g, w_mod, b_mod, g_w_in, g_b_in,
                 g_norm_v, g_w_s, g_b_s, g_w_out, m_w_dq, m_norm_q, m_w_uq, m_w_dkv, m_norm_kv, m_w_ukv,
                 m_qk_norm_q, m_qk_norm_k, m_w_o, e_w_router, e_b_router, e_w_gu, e_b_gu, e_w_down, e_b_down)
    assert MOE_PARTS == 2 and N_PROMPT == N_SAMPLE
    x_p, x_s = x_prompt.reshape(N_PROMPT, D_MODEL), x_sample.reshape(N_SAMPLE, D_MODEL)
    ckv_states, krope_states = [], []
    for layer in range(DEPTH):
        if layer % 2 == 0:
            x = _gmlp_layer(x_p, x_s, p, layer)
        else:
            x, s_ckv, s_krope = _mla_layer(x_p, x_s, p, layer)
            ckv_states.append(s_ckv)
            krope_states.append(s_krope)
        x_p, x_s = _moe_layer(x, p, layer)
    y_prompt = x_p.reshape(BATCH, SEQ, D_MODEL)
    y_sample = x_s.reshape(DEC_BATCH, DEC_SEQ, D_MODEL)
    return (y_prompt, y_sample, jnp.stack(ckv_states, axis=1), jnp.stack(krope_states, axis=1))
```

```python
import functools
import math

import jax
import jax.numpy as jnp
from jax import lax
from jax.experimental import pallas as pl
from jax.experimental.pallas import tpu as pltpu
from jax.experimental.pallas import tpu_sc as plsc

F32 = jnp.float32
BF16 = jnp.bfloat16

D_MODEL = 1024
BATCH = 32
SEQ = 256
DEPTH = 4
DEC_BATCH = 2
DEC_SEQ = 4096
PAST_LEN = 512
GRID_W = 64
RMS_EPS = 1e-6
GMLP_WIDTH = 2 * D_MODEL
GMLP_GROUPS = 8
GROUP_W = GMLP_WIDTH // GMLP_GROUPS
CHUNK = 128
N_HEADS = 16
QK_NOPE_DIM = 64
QK_ROPE_DIM = 32
QK_HEAD_DIM = QK_NOPE_DIM + QK_ROPE_DIM
V_HEAD_DIM = 64
Q_RANK = 256
KV_RANK = 128
ROPE_THETA = 10000.0
N_EXPERTS = 32
TOP_K = 4
D_FF = D_MODEL
SWIGLU_LIMIT = 7.0
SWIGLU_ALPHA = 1.702

N_PROMPT = BATCH * SEQ
N_SAMPLE = DEC_BATCH * DEC_SEQ
N_TOK = N_PROMPT + N_SAMPLE
N_COND = 1 + DEC_BATCH
KV_LEN = PAST_LEN + DEC_SEQ
N_KV_ROWS = DEC_BATCH * KV_LEN + N_PROMPT

LANE = 128
SUBLANE = 8
HEAD_PAD = LANE
QKV_W = N_HEADS * HEAD_PAD
VMEM_LIMIT = 56 * 1024 * 1024

TB = 512
MOD_TN = 1536
TQ = 512
TKC = 256
ATTN_CHUNKS_PER_ITER = 9
TM = 1024
EXPERT_ROW_GROUPS = 8
N_TILES = N_TOK * TOP_K // TM + N_EXPERTS
N_SLOTS = N_TILES * TM
PLAN_LANES = -(-N_TILES // LANE) * LANE
SLOT_TB = 2048
ROUTE_TB = 512
MOE_PARTS = 2
SC_CORES = 2
SC_SUBCORES = 16
SC_WORKERS = SC_CORES * SC_SUBCORES
SC_CHUNK_ROWS = 64
ROPE_BLOCKS = DEC_SEQ // TB
NEG_BIG = -1e30


def _cparams(sem):
    return pltpu.CompilerParams(dimension_semantics=sem, vmem_limit_bytes=VMEM_LIMIT)


def _cond_of_block(i, tb):
    n_p = N_PROMPT // tb
    per = DEC_SEQ // tb
    return jnp.where(i < n_p, 0, 1 + (i - n_p) // per)


def _rms(x, g, n=None):
    n = x.shape[-1] if n is None else n
    ss = jnp.sum(x * x, axis=-1, keepdims=True) * (1.0 / n)
    return x * lax.rsqrt(ss + RMS_EPS) * g


def _dot(a, b):
    return jnp.dot(a, b, preferred_element_type=F32)


def _mod_kernel(c_ref, w_ref, b_ref, o_ref):
    c = c_ref[...]
    s = c * jax.nn.sigmoid(c)
    o_ref[0] = _dot(s.astype(BF16), w_ref[0].astype(BF16)) + b_ref[0]


def _modulation(cond, w_mod, b_mod):
    tn = MOD_TN
    out = pl.pallas_call(
        _mod_kernel,
        grid=(DEPTH, 6 * D_MODEL // tn),
        in_specs=[
            pl.BlockSpec((SUBLANE, D_MODEL), lambda l, j: (0, 0)),
            pl.BlockSpec((1, D_MODEL, tn), lambda l, j: (l, 0, j)),
            pl.BlockSpec((1, 1, tn), lambda l, j: (l, 0, j)),
        ],
        out_specs=pl.BlockSpec((1, SUBLANE, tn), lambda l, j: (l, 0, j)),
        out_shape=jax.ShapeDtypeStruct((DEPTH, SUBLANE, 6 * D_MODEL), F32),
        compiler_params=_cparams(("parallel", "parallel")),
        name="adaln_mod",
    )(cond, w_mod, b_mod.reshape(DEPTH, 1, 6 * D_MODEL))
    m = out[:, :N_COND].reshape(DEPTH, N_COND, 6, D_MODEL)
    return jnp.pad(m, ((0, 0), (0, 0), (0, SUBLANE - 6), (0, 0)))


def _mod_spec(tb, layer, first_block=0):
    return pl.BlockSpec((None, 1, SUBLANE, D_MODEL),
                        lambda i: (layer, _cond_of_block(i + first_block, tb), 0, 0))


def _const_spec(shape):
    nd = len(shape)
    return pl.BlockSpec(shape, lambda *_: (0,) * nd)


def _layer_spec(shape, j):
    nd = len(shape)
    return pl.BlockSpec((None,) + tuple(shape), lambda *_: (j,) + (0,) * nd)


def _gelu_tanh(x):
    a = math.sqrt(2.0 / math.pi)
    hx = 0.5 * x
    return hx + hx * jnp.tanh(x * (a + (0.044715 * a) * (x * x)))


def _gmlp_kernel(xp_ref, xs_ref, mod_ref, g_ref, win_ref, bin_ref, gv_ref, ws_ref, bst_ref, wout_ref, o_ref):
    x = jnp.where(pl.program_id(0) < N_PROMPT // TB, xp_ref[...], xs_ref[...])
    m = mod_ref[0]
    h = _rms(x, g_ref[...]) * (1.0 + m[1:2]) + m[0:1]
    hb = h.astype(BF16)
    zv = _gelu_tanh(_dot(hb, win_ref[:, GMLP_WIDTH:]) + bin_ref[:, GMLP_WIDTH:])
    vn = _rms(zv, gv_ref[...]).astype(BF16)
    rows = []
    for c in range(TB // CHUNK):
        cols = []
        for g in range(GMLP_GROUPS):
            blk = vn[c * CHUNK:(c + 1) * CHUNK, g * GROUP_W:(g + 1) * GROUP_W]
            cols.append(_dot(ws_ref[g], blk) + bst_ref[:, g:g + 1])
        rows.append(jnp.concatenate(cols, axis=1))
    vm = jnp.concatenate(rows, axis=0)
    u = _gelu_tanh(_dot(hb, win_ref[:, :GMLP_WIDTH]) + bin_ref[:, :GMLP_WIDTH])
    d = _dot((u * vm).astype(BF16), wout_ref[...])
    o_ref[...] = x + m[2:3] * d


def _split_rows_specs():
    n_pb = N_PROMPT // TB
    return [pl.BlockSpec((TB, D_MODEL), lambda i: (jnp.minimum(i, n_pb - 1), 0)),
            pl.BlockSpec((TB, D_MODEL), lambda i: (jnp.maximum(i - n_pb, 0), 0))]


def _gmlp_layer(x_p, x_s, p, layer):
    j = layer // 2
    return pl.pallas_call(
        _gmlp_kernel,
        grid=(N_TOK // TB,),
        in_specs=_split_rows_specs() + [
            _mod_spec(TB, layer),
            _layer_spec((1, D_MODEL), layer),
            _layer_spec((D_MODEL, 2 * GMLP_WIDTH), j),
            _layer_spec((1, 2 * GMLP_WIDTH), j),
            _layer_spec((1, GMLP_WIDTH), j),
            _layer_spec((GMLP_GROUPS, CHUNK, CHUNK), j),
            _layer_spec((CHUNK, GMLP_GROUPS), j),
            _layer_spec((GMLP_WIDTH, D_MODEL), j),
        ],
        out_specs=pl.BlockSpec((TB, D_MODEL), lambda i: (i, 0)),
        out_shape=jax.ShapeDtypeStruct((N_TOK, D_MODEL), F32),
        compiler_params=_cparams(("parallel",)),
        name="gmlp_mixer",
    )(x_p, x_s, p["mod"], p["norm_mix_g"], p["g_w_in"], p["g_b_in"], p["g_norm_v"], p["g_w_s"], p["g_b_st"],
      p["g_w_out"])


def _rope_tables():
    t = jnp.arange(DEC_SEQ)
    row_id = (t // GRID_W).astype(F32)
    col_id = (t % GRID_W).astype(F32)
    axis_dim = QK_ROPE_DIM // 2
    inv_freq = ROPE_THETA ** (-jnp.arange(0, axis_dim, 2, dtype=F32) / axis_dim)
    ang = jnp.stack([row_id[:, None] * inv_freq, col_id[:, None] * inv_freq], axis=1)
    cos, sin = jnp.cos(ang), jnp.sin(ang)
    zeros = jnp.zeros_like(sin)
    cos_l = jnp.concatenate([cos, cos], axis=-1).reshape(DEC_SEQ, QK_ROPE_DIM)
    s1_l = jnp.concatenate([-sin, zeros], axis=-1).reshape(DEC_SEQ, QK_ROPE_DIM)
    s2_l = jnp.concatenate([zeros, sin], axis=-1).reshape(DEC_SEQ, QK_ROPE_DIM)

    def widen(rope_part, nope_fill):
        left = jnp.full((DEC_SEQ, QK_NOPE_DIM), nope_fill, F32)
        right = jnp.zeros((DEC_SEQ, HEAD_PAD - QK_HEAD_DIM), F32)
        return jnp.concatenate([left, rope_part, right], axis=-1)

    pos = jnp.stack([widen(cos_l, 1.0), widen(s1_l, 0.0), widen(s2_l, 0.0)])
    ident_c = jnp.concatenate([jnp.ones((TB, QK_HEAD_DIM), F32),
                               jnp.zeros((TB, HEAD_PAD - QK_HEAD_DIM), F32)], axis=-1)
    ident = jnp.stack([ident_c, jnp.zeros_like(ident_c), jnp.zeros_like(ident_c)])
    return jnp.swapaxes(jnp.concatenate([pos, ident], axis=1), 1, 2)


def _dot_nt(a, b):
    return lax.dot_general(a, b, (((1,), (1,)), ((), ())), preferred_element_type=F32)


def _shift_rows(x, n):
    n = n % x.shape[0]
    return jnp.concatenate([x[n:], x[:n]], axis=0)


def _head_norm_rope(nope, rope, g_ref, tabt_ref):
    half = QK_ROPE_DIM // 4
    lo, hi = QK_NOPE_DIM, QK_HEAD_DIM
    ss = (jnp.sum(nope * nope, axis=0, keepdims=True) + jnp.sum(rope * rope, axis=0, keepdims=True))
    rs = lax.rsqrt(ss * (1.0 / QK_HEAD_DIM) + RMS_EPS)
    xn = nope * rs * g_ref[:lo, :]
    xr = rope * rs * g_ref[lo:hi, :]
    rot = (xr * tabt_ref[0, lo:hi, :] + _shift_rows(xr, half) * tabt_ref[1, lo:hi, :]
           + _shift_rows(xr, -half) * tabt_ref[2, lo:hi, :])
    return jnp.concatenate([xn, rot, jnp.zeros((HEAD_PAD - hi, nope.shape[1]), F32)], axis=0)


def _mla_proj_kernel(xp_ref, xs_ref, mod_ref, g_ref, wdq_ref, nq_ref, wuqt_ref, gq_ref, wdkv_ref, nkv_ref,
                     tabt_ref, qt_ref, ckv_ref, krp_ref):
    x = jnp.where(pl.program_id(0) < N_PROMPT // TB, xp_ref[...], xs_ref[...])
    m = mod_ref[0]
    h = _rms(x, g_ref[...]) * (1.0 + m[1:2]) + m[0:1]
    hb = h.astype(BF16)
    cq = _rms(_dot(hb, wdq_ref[...]), nq_ref[...])
    qt = _dot_nt(wuqt_ref[...], cq.astype(BF16))
    for hd in range(N_HEADS):
        r0 = hd * HEAD_PAD
        qh = _head_norm_rope(qt[r0:r0 + QK_NOPE_DIM, :], qt[r0 + QK_NOPE_DIM:r0 + QK_HEAD_DIM, :], gq_ref, tabt_ref)
        qt_ref[r0:r0 + HEAD_PAD, :] = qh.astype(BF16)
    kva = _dot(hb, wdkv_ref[...])
    ckv_ref[...] = _rms(kva[:, :KV_RANK], nkv_ref[...])
    krp_ref[...] = kva[:, KV_RANK:]


def _mla_proj(x_p, x_s, p, layer):
    j = layer // 2
    n_pb = N_PROMPT // TB

    def tab_idx(i):
        return (0, 0, jnp.where(i < n_pb, ROPE_BLOCKS, (i - n_pb) % ROPE_BLOCKS))

    return pl.pallas_call(
        _mla_proj_kernel,
        grid=(N_TOK // TB,),
        in_specs=_split_rows_specs() + [
            _mod_spec(TB, layer),
            _layer_spec((1, D_MODEL), layer),
            _layer_spec((D_MODEL, Q_RANK), j),
            _layer_spec((1, Q_RANK), j),
            _layer_spec((QKV_W, Q_RANK), j),
            _layer_spec((HEAD_PAD, 1), j),
            _layer_spec((D_MODEL, 2 * LANE), j),
            _layer_spec((1, KV_RANK), j),
            pl.BlockSpec((3, HEAD_PAD, TB), tab_idx),
        ],
        out_specs=[
            pl.BlockSpec((QKV_W, TB), lambda i: (0, i)),
            pl.BlockSpec((TB, KV_RANK), lambda i: (i, 0)),
            pl.BlockSpec((TB, HEAD_PAD), lambda i: (i, 0)),
        ],
        out_shape=[
            jax.ShapeDtypeStruct((QKV_W, N_TOK), BF16),
            jax.ShapeDtypeStruct((N_TOK, KV_RANK), F32),
            jax.ShapeDtypeStruct((N_TOK, HEAD_PAD), F32),
        ],
        compiler_params=_cparams(("parallel",)),
        name="mla_proj",
    )(x_p, x_s, p["mod"], p["norm_mix_g"], p["m_w_dq"], p["m_norm_q"], p["m_w_uq_t"], p["m_gq"], p["m_w_dkv"],
      p["m_norm_kv"], p["rope_t"])


def _kv_block_source(i):
    n_sb = DEC_BATCH * KV_LEN // TB
    per = KV_LEN // TB
    b = jnp.minimum(i // per, DEC_BATCH - 1)
    jj = i % per
    return jnp.logical_and(i < n_sb, jj < PAST_LEN // TB), b, jj


def _kv_expand_kernel(cckv_ref, ckrp_ref, ckv_ref, krp_ref, wukvt_ref, gk_ref, tabt_ref, k_ref, vt_ref):
    from_cache, _, _ = _kv_block_source(pl.program_id(0))
    ckv = jnp.where(from_cache, cckv_ref[...], ckv_ref[...])
    krp = jnp.where(from_cache, ckrp_ref[...], krp_ref[...])
    kvt = _dot_nt(wukvt_ref[...], ckv.astype(BF16))
    k_rope = krp.T[QK_NOPE_DIM:QK_HEAD_DIM, :]
    row = lax.broadcasted_iota(jnp.int32, (QK_NOPE_DIM, TB), 0)
    ones_row = jnp.where(row == 0, 1.0, 0.0)
    for hd in range(N_HEADS):
        r0 = hd * HEAD_PAD
        kh = _head_norm_rope(kvt[r0:r0 + QK_NOPE_DIM, :], k_rope, gk_ref, tabt_ref)
        k_ref[:, r0:r0 + HEAD_PAD] = kh.T.astype(BF16)
        vt_ref[r0:r0 + HEAD_PAD, :] = jnp.concatenate([ones_row, kvt[r0 + QK_NOPE_DIM:r0 + HEAD_PAD, :]],
                                                      axis=0).astype(BF16)


def _kv_expand(ckv, krp, p, layer):
    j = layer // 2
    n_sb = DEC_BATCH * KV_LEN // TB
    n_cache = PAST_LEN // TB
    n_pb = N_PROMPT // TB
    lat_blocks = DEC_SEQ // TB

    def tab_idx(i):
        from_cache, _, jj = _kv_block_source(i)
        return (0, 0, jnp.where((i >= n_sb) | from_cache, ROPE_BLOCKS, jj - n_cache))

    def cache_idx(i):
        _, b, jj = _kv_block_source(i)
        return (b, j, jnp.minimum(jj, n_cache - 1), 0)

    def tok_idx(i):
        _, b, jj = _kv_block_source(i)
        latent = n_pb + b * lat_blocks + jnp.maximum(jj - n_cache, 0)
        return (jnp.where(i < n_sb, latent, i - n_sb), 0)

    return pl.pallas_call(
        _kv_expand_kernel,
        grid=(N_KV_ROWS // TB,),
        in_specs=[
            pl.BlockSpec((None, None, TB, KV_RANK), cache_idx),
            pl.BlockSpec((None, None, TB, HEAD_PAD), cache_idx),
            pl.BlockSpec((TB, KV_RANK), tok_idx),
            pl.BlockSpec((TB, HEAD_PAD), tok_idx),
            _layer_spec((QKV_W, KV_RANK), j),
            _layer_spec((HEAD_PAD, 1), j),
            pl.BlockSpec((3, HEAD_PAD, TB), tab_idx),
        ],
        out_specs=[
            pl.BlockSpec((TB, QKV_W), lambda i: (i, 0)),
            pl.BlockSpec((QKV_W, TB), lambda i: (0, i)),
        ],
        out_shape=[
            jax.ShapeDtypeStruct((N_KV_ROWS, QKV_W), BF16),
            jax.ShapeDtypeStruct((QKV_W, N_KV_ROWS), BF16),
        ],
        compiler_params=_cparams(("parallel",)),
        name="mla_kv_expand",
    )(p["cache_ckv"], p["cache_krp"], ckv, krp, p["m_w_ukv_t"], p["m_gk"], p["rope_t"])


def _attn_kernel(qt_ref, k_ref, vt_ref, o_ref, s_a, s_b, *, t_k, hps, qps, tq):
    c = (1.0 / math.sqrt(QK_HEAD_DIM)) * math.log2(math.e)
    kc = min(TKC, t_k)
    n_chunks = t_k // kc
    per_it = min(ATTN_CHUNKS_PER_ITER, n_chunks)
    n_it = n_chunks // per_it
    bufs = (s_a, s_b)

    units = [(qb, h) for qb in range(qps) for h in range(hps)]

    def rows(h):
        return slice(h * HEAD_PAD, (h + 1) * HEAD_PAD)

    def cols(qb):
        return slice(qb * tq, (qb + 1) * tq)

    def scores(u, off, m8):
        qb, h = units[u]
        st = _dot(k_ref[pl.ds(off, kc), rows(h)], qt_ref[rows(h), cols(qb)])
        bufs[u % 2][pl.ds(off, kc), :] = st
        return jnp.maximum(m8, jnp.max(st.reshape(kc // SUBLANE, SUBLANE, tq), axis=0))

    def weigh(u, off, m, acc):
        pt = jnp.exp2((bufs[u % 2][pl.ds(off, kc), :] - m) * c).astype(BF16)
        return acc + _dot(vt_ref[rows(units[u][1]), pl.ds(off, kc)], pt)

    def phase(u_scores, u_weigh, m):
        def body(it, carry):
            m8, acc = carry
            for j in range(per_it):
                off = (it * per_it + j) * kc
                off = off if isinstance(off, int) else pl.multiple_of(off, kc)
                if u_scores is not None:
                    m8 = scores(u_scores, off, m8)
                if u_weigh is not None:
                    acc = weigh(u_weigh, off, m, acc)
            return m8, acc

        init = (jnp.full((SUBLANE, tq), NEG_BIG, F32), jnp.zeros((HEAD_PAD, tq), F32))
        return body(0, init) if n_it == 1 else lax.fori_loop(0, n_it, body, init)

    outs = []
    if n_chunks == 1:
        sts = [_dot(k_ref[:, rows(h)], qt_ref[rows(h), cols(qb)]) for qb, h in units]
        for (qb, h), st in zip(units, sts):
            m = jnp.max(st, axis=0, keepdims=True)
            acc = _dot(vt_ref[rows(h), :], jnp.exp2((st - m) * c).astype(BF16))
            outs.append(acc[QK_NOPE_DIM:, :] / acc[0:1, :])
    else:
        m8, _ = phase(0, None, None)
        for u in range(1, len(units) + 1):
            m = jnp.max(m8, axis=0, keepdims=True)
            m8, acc = phase(u if u < len(units) else None, u - 1, m)
            outs.append(acc[QK_NOPE_DIM:, :] / acc[0:1, :])
    for qb in range(qps):
        o_ref[cols(qb), :] = jnp.concatenate(outs[qb * hps:(qb + 1) * hps], axis=0).T.astype(BF16)


def _attention(qt, k, vt, *, n_batch, t_q, t_k, q_row0, kv_row0, hps, qps, tq_unit):
    tq = tq_unit * qps
    nq = t_q // tq
    q0 = q_row0 // tq
    k0 = kv_row0 // t_k
    return pl.pallas_call(
        functools.partial(_attn_kernel, t_k=t_k, hps=hps, qps=qps, tq=tq_unit),
        grid=(n_batch, N_HEADS // hps, nq),
        in_specs=[
            pl.BlockSpec((hps * HEAD_PAD, tq), lambda b, h, i: (h, q0 + b * nq + i)),
            pl.BlockSpec((t_k, hps * HEAD_PAD), lambda b, h, i: (k0 + b, h)),
            pl.BlockSpec((hps * HEAD_PAD, t_k), lambda b, h, i: (h, k0 + b)),
        ],
        out_specs=pl.BlockSpec((tq, hps * V_HEAD_DIM), lambda b, h, i: (b * nq + i, h)),
        out_shape=jax.ShapeDtypeStruct((n_batch * t_q, N_HEADS * V_HEAD_DIM), BF16),
        scratch_shapes=[pltpu.VMEM((t_k, tq_unit), F32), pltpu.VMEM((t_k, tq_unit), F32)],
        compiler_params=_cparams(("parallel", "parallel", "parallel")),
        name=f"mla_attention_tk{t_k}",
    )(qt, k, vt)


def _out_proj_kernel(ap_ref, as_ref, xp_ref, xs_ref, mod_ref, wo_ref, o_ref):
    is_prompt = pl.program_id(0) < N_PROMPT // TB
    a = jnp.where(is_prompt, ap_ref[...], as_ref[...])
    x = jnp.where(is_prompt, xp_ref[...], xs_ref[...])
    o_ref[...] = x + mod_ref[0][2:3] * _dot(a, wo_ref[...])


def _out_proj(attn_p, attn_s, x_p, x_s, p, layer):
    return pl.pallas_call(
        _out_proj_kernel,
        grid=(N_TOK // TB,),
        in_specs=_split_rows_specs() + _split_rows_specs() + [
            _mod_spec(TB, layer),
            _layer_spec((D_MODEL, D_MODEL), layer // 2),
        ],
        out_specs=pl.BlockSpec((TB, D_MODEL), lambda i: (i, 0)),
        out_shape=jax.ShapeDtypeStruct((N_TOK, D_MODEL), F32),
        compiler_params=_cparams(("parallel",)),
        name="mla_out_proj",
    )(attn_p, attn_s, x_p, x_s, p["mod"], p["m_w_o"])


def _mla_layer(x_p, x_s, p, layer):
    qt, ckv, krp = _mla_proj(x_p, x_s, p, layer)
    k, vt = _kv_expand(ckv, krp, p, layer)
    a_p = _attention(qt, k, vt, n_batch=BATCH, t_q=SEQ, t_k=SEQ, q_row0=0, kv_row0=DEC_BATCH * KV_LEN,
                     hps=N_HEADS, qps=1, tq_unit=SEQ)
    a_s = _attention(qt, k, vt, n_batch=DEC_BATCH, t_q=DEC_SEQ, t_k=KV_LEN, q_row0=N_PROMPT, kv_row0=0,
                     hps=4, qps=4, tq_unit=TQ)
    x = _out_proj(a_p, a_s, x_p, x_s, p, layer)
    state_ckv = ckv[:N_PROMPT].reshape(BATCH, SEQ, KV_RANK)
    state_krope = krp[:N_PROMPT, QK_NOPE_DIM:QK_HEAD_DIM].reshape(BATCH, SEQ, QK_ROPE_DIM)
    return x, state_ckv, state_krope


def _tile_plan(cnt8, plan_ref, start_ref):
    tm = float(TM)
    cnt_row = cnt8[0:1, :]
    cnt_col = cnt8.T[:, 0:1]
    tiles_row = jnp.floor((cnt_row + (tm - 1.0)) * (1.0 / tm))
    tiles_col = jnp.floor((cnt_col + (tm - 1.0)) * (1.0 / tm))
    sub = lax.broadcasted_iota(jnp.int32, (LANE, LANE), 0).astype(F32)
    lan = lax.broadcasted_iota(jnp.int32, (LANE, LANE), 1).astype(F32)
    tile_end_row = jnp.sum(jnp.where(sub <= lan, tiles_col, 0.0), axis=0, keepdims=True)
    tile_end_col = jnp.sum(jnp.where(lan <= sub, tiles_row, 0.0), axis=1, keepdims=True)
    n_used = jnp.max(tile_end_row, axis=1, keepdims=True)
    start_col = (tile_end_col - tiles_col) * tm
    end_col = start_col + cnt_col
    cand = jnp.where(jnp.logical_and(lan > sub, tiles_row > 0.0), lan, float(LANE))
    next_col = jnp.min(cand, axis=1, keepdims=True)
    next_col = jnp.where(next_col < float(LANE), next_col, -1.0)
    n_lanes = plan_ref.shape[1]
    tidx = jnp.minimum(lax.broadcasted_iota(jnp.int32, (LANE, n_lanes), 1).astype(F32), n_used - 1.0)
    esub = lax.broadcasted_iota(jnp.int32, (LANE, n_lanes), 0).astype(F32)
    te_row = jnp.sum(jnp.where(tile_end_col <= tidx, 1.0, 0.0), axis=0, keepdims=True)
    mine = esub == te_row
    end_at = jnp.sum(jnp.where(mine, end_col, 0.0), axis=0, keepdims=True)
    tv_row = jnp.clip(end_at - tidx[0:1, :] * tm, 0.0, tm)
    nx_row = jnp.sum(jnp.where(mine, next_col, 0.0), axis=0, keepdims=True)
    nu_row = jnp.broadcast_to(n_used, (1, n_lanes))
    plan_ref[...] = jnp.concatenate([te_row, tv_row, nx_row, nu_row, jnp.zeros((SUBLANE - 4, n_lanes), F32)], axis=0)
    start_ref[...] = jnp.broadcast_to(start_col, (LANE, LANE))


def _route_kernel(x_ref, mod_ref, g_ref, wrt_ref, brt_ref, triu_ref, h_ref, metat_ref, plan_ref, start_ref, carry):
    i = pl.program_id(0)

    @pl.when(i == 0)
    def _():
        carry[...] = jnp.zeros_like(carry)

    x = x_ref[...]
    m = mod_ref[0]
    h = _rms(x, g_ref[...]) * (1.0 + m[4:5]) + m[3:4]
    h_ref[...] = _pack_bf16_pairs(h)
    logits = _dot_nt(wrt_ref[...], h.astype(BF16)) + brt_ref[...]
    n_tok = logits.shape[1]
    esub = lax.broadcasted_iota(jnp.int32, logits.shape, 0).astype(F32)
    work = logits
    sel = jnp.zeros(logits.shape, F32)
    hits, firsts, tops = [], [], []
    for k in range(TOP_K):
        mk = jnp.max(work, axis=0, keepdims=True)
        first = jnp.min(jnp.where(work == mk, esub, float(N_EXPERTS)), axis=0, keepdims=True)
        hit = esub == first
        sel = jnp.where(hit, 1.0, sel)
        work = jnp.where(hit, -jnp.inf, work)
        hits.append(hit)
        firsts.append(first)
        tops.append(mk)
    es = [jnp.exp(t - tops[0]) for t in tops]
    denom = es[0] + es[1] + es[2] + es[3]
    pos = _dot(sel.astype(BF16), triu_ref[...]) + carry[:, 0:1]
    carry[...] = carry[...] + jnp.sum(sel, axis=1, keepdims=True)

    @pl.when(i == pl.num_programs(0) - 1)
    def _():
        counts = jnp.concatenate([carry[...], jnp.zeros((LANE - N_EXPERTS, LANE), F32)], axis=0).T
        _tile_plan(counts[:SUBLANE, :], plan_ref, start_ref)

    ranks = [jnp.sum(jnp.where(hit, pos, 0.0), axis=0, keepdims=True) for hit in hits]
    gates = [e / denom for e in es]
    metat_ref[...] = jnp.concatenate(firsts + gates + ranks + [jnp.zeros((SUBLANE // 2, n_tok), F32)], axis=0)


def _route(x, p, layer):
    tb = ROUTE_TB
    return pl.pallas_call(
        _route_kernel,
        grid=(N_TOK // tb,),
        in_specs=[
            pl.BlockSpec((tb, D_MODEL), lambda i: (i, 0)),
            _mod_spec(tb, layer),
            _layer_spec((1, D_MODEL), layer),
            _layer_spec((N_EXPERTS, D_MODEL), layer),
            _layer_spec((N_EXPERTS, 1), layer),
            _const_spec((tb, tb)),
        ],
        out_specs=[
            pl.BlockSpec((tb, D_MODEL // 2), lambda i: (i, 0)),
            pl.BlockSpec((2 * SUBLANE, tb), lambda i: (0, i)),
            _const_spec((SUBLANE, PLAN_LANES)),
            _const_spec((LANE, LANE)),
        ],
        out_shape=[
            jax.ShapeDtypeStruct((N_TOK, D_MODEL // 2), jnp.uint32),
            jax.ShapeDtypeStruct((2 * SUBLANE, N_TOK), F32),
            jax.ShapeDtypeStruct((SUBLANE, PLAN_LANES), F32),
            jax.ShapeDtypeStruct((LANE, LANE), F32),
        ],
        scratch_shapes=[pltpu.VMEM((N_EXPERTS, LANE), F32)],
        compiler_params=_cparams(("arbitrary",)),
        name="moe_route",
    )(x, p["mod"], p["norm_ffn_g"], p["e_w_router_t"], p["e_b_router"], p["triu"])


def _slots_kernel(start_ref, metat_ref, dest_ref):
    start_col = start_ref[:, 0:1]
    esub = lax.broadcasted_iota(jnp.int32, (LANE, SLOT_TB), 0).astype(F32)
    rows = []
    for k in range(TOP_K):
        e = metat_ref[k:k + 1, :]
        first = jnp.sum(jnp.where(esub == e, start_col, 0.0), axis=0, keepdims=True)
        rows.append(first + metat_ref[2 * TOP_K + k:2 * TOP_K + k + 1, :])
    dest_ref[...] = jnp.concatenate(rows, axis=0).astype(jnp.int32)


def _slots(start, meta_t):
    return pl.pallas_call(
        _slots_kernel,
        grid=(N_TOK // SLOT_TB,),
        in_specs=[
            _const_spec((LANE, LANE)),
            pl.BlockSpec((2 * SUBLANE, SLOT_TB), lambda i: (0, i)),
        ],
        out_specs=pl.BlockSpec((TOP_K, SLOT_TB), lambda i: (0, i)),
        out_shape=jax.ShapeDtypeStruct((TOP_K, N_TOK), jnp.int32),
        compiler_params=_cparams(("parallel",)),
        name="moe_slots",
    )(start, meta_t)


def _sc_gather(table, idx, ch):
    b, w = idx.shape[0], table.shape[1]
    per_w = b // SC_WORKERS
    n_ch = per_w // ch
    assert per_w * SC_WORKERS == b and n_ch * ch == per_w and n_ch % 2 == 0
    mesh = plsc.VectorSubcoreMesh(core_axis_name="c", subcore_axis_name="s")

    @functools.partial(
        pl.kernel, mesh=mesh,
        out_type=jax.ShapeDtypeStruct((b, w), table.dtype),
        scratch_types=[
            pltpu.VMEM((n_ch, ch), jnp.int32),
            pltpu.VMEM((ch, w), table.dtype),
            pltpu.VMEM((ch, w), table.dtype),
            pltpu.SemaphoreType.DMA, pltpu.SemaphoreType.DMA,
            pltpu.SemaphoreType.DMA, pltpu.SemaphoreType.DMA,
        ],
        name="sc_row_gather",
    )
    def gather_rows(table_hbm, idx_hbm, out_hbm, idx_v, buf0, buf1, g0, g1, s0, s1):
        wid = lax.axis_index("s") * SC_CORES + lax.axis_index("c")
        base = wid * per_w
        pltpu.sync_copy(idx_hbm.at[wid], idx_v)

        def gather(j, buf, sem):
            return pltpu.make_async_copy(table_hbm.at[idx_v.at[j]], buf, sem)

        def store(j, buf, sem):
            return pltpu.make_async_copy(buf, out_hbm.at[pl.ds(base + j * ch, ch)], sem)

        gather(0, buf0, g0).start()

        @pl.loop(0, n_ch, step=2)
        def _(j):
            @pl.when(j > 0)
            def _():
                store(j - 1, buf1, s1).wait()

            gather(j + 1, buf1, g1).start()
            gather(j, buf0, g0).wait()
            store(j, buf0, s0).start()
            gather(j + 1, buf1, g1).wait()
            store(j + 1, buf1, s1).start()
            store(j, buf0, s0).wait()

            @pl.when(j + 2 < n_ch)
            def _():
                gather(j + 2, buf0, g0).start()

        store(n_ch - 1, buf1, s1).wait()

    return gather_rows(table, idx.reshape(SC_WORKERS, n_ch, ch))


def _sc_dispatch(rows, dest_t, ch):
    n, w = rows.shape
    per_w = n // SC_WORKERS
    n_ch = per_w // ch
    assert per_w * SC_WORKERS == n and n_ch * ch == per_w and n_ch % 2 == 0
    mesh = plsc.VectorSubcoreMesh(core_axis_name="c", subcore_axis_name="s")
    idx = dest_t.reshape(TOP_K, SC_WORKERS, n_ch, ch)

    @functools.partial(
        pl.kernel, mesh=mesh,
        out_type=jax.ShapeDtypeStruct((N_SLOTS, w), rows.dtype),
        scratch_types=[
            pltpu.VMEM((TOP_K * n_ch, ch), jnp.int32),
            pltpu.VMEM((ch, w), rows.dtype),
            pltpu.VMEM((ch, w), rows.dtype),
            pltpu.SemaphoreType.DMA, pltpu.SemaphoreType.DMA,
            pltpu.SemaphoreType.DMA, pltpu.SemaphoreType.DMA,
        ],
        name="sc_row_dispatch",
    )
    def dispatch_rows(rows_hbm, idx_hbm, out_hbm, idx_v, buf0, buf1, l0, l1, s0, s1):
        wid = lax.axis_index("s") * SC_CORES + lax.axis_index("c")
        base = wid * per_w
        for k in range(TOP_K):
            pltpu.sync_copy(idx_hbm.at[k, wid], idx_v.at[pl.ds(k * n_ch, n_ch)])

        def load(j, buf, sem):
            return pltpu.make_async_copy(rows_hbm.at[pl.ds(base + j * ch, ch)], buf, sem)

        def scatter(j, k, buf, sem):
            return pltpu.make_async_copy(buf, out_hbm.at[idx_v.at[k * n_ch + j]], sem)

        load(0, buf0, l0).start()

        @pl.loop(0, n_ch, step=2)
        def _(j):
            load(j + 1, buf1, l1).start()
            load(j, buf0, l0).wait()
            for k in range(TOP_K):
                scatter(j, k, buf0, s0).start()
            load(j + 1, buf1, l1).wait()
            for k in range(TOP_K):
                scatter(j + 1, k, buf1, s1).start()
            for k in range(TOP_K):
                scatter(j, k, buf0, s0).wait()

            @pl.when(j + 2 < n_ch)
            def _():
                load(j + 2, buf0, l0).start()

            for k in range(TOP_K):
                scatter(j + 1, k, buf1, s1).wait()

    return dispatch_rows(rows, idx)


def _deinterleave_matrix():
    src = jnp.arange(2 * LANE)[:, None]
    dst = jnp.arange(2 * LANE)[None, :]
    want = jnp.where(dst < LANE, 2 * dst, 2 * (dst - LANE) + 1)
    return (src == want).astype(BF16)


def _expert_kernel(te_ref, nu_ref, tv_ref, nx_ref, x_ref, wgu_hbm, bgu_ref, wd_hbm, bd_ref, perm_ref, o_ref,
                   wgu_st, wd_st, wgu_bf, wd_bf, sems, *, layer):
    i = pl.program_id(0)
    prev = te_ref[jnp.maximum(i - 1, 0)]
    fresh = jnp.logical_or(i == 0, te_ref[i] != prev)

    def fetch(e):
        return (pltpu.make_async_copy(wgu_hbm.at[layer, e], wgu_st, sems.at[0]),
                pltpu.make_async_copy(wd_hbm.at[layer, e], wd_st, sems.at[1]))

    @pl.when(i == 0)
    def _():
        for cp in fetch(te_ref[0]):
            cp.start()

    @pl.when(jnp.logical_and(fresh, i < nu_ref[0]))
    def _():
        for cp in fetch(te_ref[i]):
            cp.wait()
        for b in range(2 * D_FF // (2 * LANE)):
            sl = slice(b * 2 * LANE, (b + 1) * 2 * LANE)
            wgu_bf[:, sl] = _dot(wgu_st[:, sl].astype(BF16), perm_ref[...]).astype(BF16)
        wd_bf[...] = wd_st[...].astype(BF16)

        @pl.when(nx_ref[i] >= 0)
        def _():
            for cp in fetch(nx_ref[i]):
                cp.start()

    def ffn(n_rows):
        row = lax.broadcasted_iota(jnp.int32, (n_rows, D_MODEL // 2), 0)
        w = jnp.where(row < tv_ref[i], x_ref[:n_rows, :], jnp.uint32(0))
        x = _unpack_bf16_pairs(w).astype(BF16)
        gu = _dot(x, wgu_bf[...]) + bgu_ref[...]
        acts = []
        for b in range(D_FF // LANE):
            glu = jnp.minimum(gu[:, b * 2 * LANE:b * 2 * LANE + LANE], SWIGLU_LIMIT)
            lin = jnp.clip(gu[:, b * 2 * LANE + LANE:(b + 1) * 2 * LANE], -SWIGLU_LIMIT, SWIGLU_LIMIT)
            acts.append((glu * jax.nn.sigmoid(SWIGLU_ALPHA * glu) * (lin + 1.0)).astype(BF16))
        act = jnp.concatenate(acts, axis=1)
        o_ref[:n_rows, :] = _pack_bf16_pairs(_dot(act, wd_bf[...]) + bd_ref[...])

    in_use = i < nu_ref[0]
    quarter = TM // EXPERT_ROW_GROUPS
    for n_rows in range(quarter, TM + 1, quarter):
        fits = jnp.logical_and(tv_ref[i] > n_rows - quarter, tv_ref[i] <= n_rows)

        @pl.when(jnp.logical_and(in_use, fits))
        def _(n_rows=n_rows):
            ffn(n_rows)


def _experts(buf, tile_expert, n_used, tile_valid, tile_next, p, layer):
    def row_idx(i, te, nu, tv, nx):
        return (jnp.minimum(i, nu[0] - 1), 0)

    def b_idx(i, te, nu, tv, nx):
        return (layer, te[i], 0, 0)

    grid_spec = pltpu.PrefetchScalarGridSpec(
        num_scalar_prefetch=4,
        grid=(N_TILES,),
        in_specs=[
            pl.BlockSpec((TM, D_MODEL // 2), row_idx),
            pl.BlockSpec(memory_space=pl.ANY),
            pl.BlockSpec((None, None, 1, 2 * D_FF), b_idx),
            pl.BlockSpec(memory_space=pl.ANY),
            pl.BlockSpec((None, None, 1, D_MODEL), b_idx),
            _const_spec((2 * LANE, 2 * LANE)),
        ],
        out_specs=pl.BlockSpec((TM, D_MODEL // 2), row_idx),
        scratch_shapes=[
            pltpu.VMEM((D_MODEL, 2 * D_FF), F32),
            pltpu.VMEM((D_FF, D_MODEL), F32),
            pltpu.VMEM((D_MODEL, 2 * D_FF), BF16),
            pltpu.VMEM((D_FF, D_MODEL), BF16),
            pltpu.SemaphoreType.DMA((2,)),
        ],
    )
    return pl.pallas_call(
        functools.partial(_expert_kernel, layer=layer),
        grid_spec=grid_spec,
        out_shape=jax.ShapeDtypeStruct((N_SLOTS, D_MODEL // 2), jnp.uint32),
        compiler_params=_cparams(("arbitrary",)),
        name="moe_experts",
    )(tile_expert, n_used, tile_valid, tile_next, buf, p["e_w_gu"], p["e_b_gu"], p["e_w_down"], p["e_b_down"],
      p["deinterleave"])


def _pack_bf16_pairs(v):
    half = v.shape[1] // 2
    bits = pltpu.bitcast(v.astype(BF16).astype(F32), jnp.uint32)
    return (bits[:, half:] & jnp.uint32(0xFFFF0000)) | (bits[:, :half] >> 16)


def _unpack_bf16_pairs(w):
    return jnp.concatenate([pltpu.bitcast(w << 16, F32), pltpu.bitcast(w & jnp.uint32(0xFFFF0000), F32)],
                           axis=1)


def _combine_kernel(x_ref, mod_ref, y_ref, mt_ref, o_ref):
    w = mt_ref[...].T[:, TOP_K:2 * TOP_K]
    y = _unpack_bf16_pairs(y_ref[0]) * w[:, 0:1]
    for k in range(1, TOP_K):
        y = y + _unpack_bf16_pairs(y_ref[k]) * w[:, k:k + 1]
    o_ref[...] = x_ref[...] + mod_ref[0][5:6] * y


def _combine(x, y4, meta_t, p, layer, part):
    n_rows = N_TOK // MOE_PARTS
    tb = TB
    first = part * n_rows // tb
    return pl.pallas_call(
        _combine_kernel,
        grid=(n_rows // tb,),
        in_specs=[
            pl.BlockSpec((tb, D_MODEL), lambda i: (i + first, 0)),
            _mod_spec(tb, layer, first),
            pl.BlockSpec((TOP_K, tb, D_MODEL // 2), lambda i: (0, i, 0)),
            pl.BlockSpec((2 * SUBLANE, tb), lambda i: (0, i + first)),
        ],
        out_specs=pl.BlockSpec((tb, D_MODEL), lambda i: (i, 0)),
        out_shape=jax.ShapeDtypeStruct((n_rows, D_MODEL), F32),
        compiler_params=_cparams(("parallel",)),
        name="moe_combine",
    )(x, p["mod"], y4, meta_t)


def _moe_layer(x, p, layer):
    hp, meta_t, plan, start = _route(x, p, layer)
    plan = plan[:4, :N_TILES].astype(jnp.int32)
    tile_expert, tile_valid, tile_next, n_used = plan[0], plan[1], plan[2], plan[3, :1]
    dest_t = _slots(start, meta_t)
    buf = _sc_dispatch(hp, dest_t, SC_CHUNK_ROWS)
    yb = _experts(buf, tile_expert, n_used, tile_valid, tile_next, p, layer)
    n_rows = N_TOK // MOE_PARTS
    outs = []
    for part in range(MOE_PARTS):
        idx = dest_t[:, part * n_rows:(part + 1) * n_rows].reshape(-1)
        y4 = _sc_gather(yb, idx, SC_CHUNK_ROWS).reshape(TOP_K, n_rows, D_MODEL // 2)
        outs.append(_combine(x, y4, meta_t, p, layer, part))
    return outs


def _prepare(c, cache_ckv, cache_krope, c_ctx, norm_mix_g, norm_ffn_g, w_mod, b_mod, g_w_in, g_b_in, g_norm_v,
             g_w_s, g_b_s, g_w_out, m_w_dq, m_norm_q, m_w_uq, m_w_dkv, m_norm_kv, m_w_ukv, m_qk_norm_q,
             m_qk_norm_k, m_w_o, e_w_router, e_b_router, e_w_gu, e_b_gu, e_w_down, e_b_down):
    n_mla = m_w_dq.shape[0]
    cond = jnp.concatenate([c_ctx[None, :], c, jnp.zeros((SUBLANE - N_COND, D_MODEL), F32)], axis=0)
    wdkv = jnp.concatenate([m_w_dkv[..., :KV_RANK], jnp.zeros((n_mla, D_MODEL, QK_NOPE_DIM), F32),
                            m_w_dkv[..., KV_RANK:], jnp.zeros((n_mla, D_MODEL, HEAD_PAD - QK_HEAD_DIM), F32)],
                           axis=-1)
    w_uq = jnp.pad(m_w_uq.reshape(n_mla, Q_RANK, N_HEADS, QK_HEAD_DIM),
                   ((0, 0), (0, 0), (0, 0), (0, HEAD_PAD - QK_HEAD_DIM))).reshape(n_mla, Q_RANK, QKV_W)

    def gain_col(g):
        return jnp.pad(g, ((0, 0), (0, HEAD_PAD - QK_HEAD_DIM)))[:, :, None]

    return {
        "mod": _modulation(cond, w_mod, b_mod),
        "rope_t": _rope_tables(),
        "norm_mix_g": norm_mix_g[:, None, :],
        "norm_ffn_g": norm_ffn_g[:, None, :],
        "g_w_in": g_w_in.astype(BF16),
        "g_b_in": g_b_in[:, None, :],
        "g_norm_v": g_norm_v[:, None, :],
        "g_w_s": g_w_s.astype(BF16),
        "g_b_st": jnp.swapaxes(g_b_s, 1, 2),
        "g_w_out": g_w_out.astype(BF16),
        "m_w_dq": m_w_dq.astype(BF16),
        "m_norm_q": m_norm_q[:, None, :],
        "m_w_uq_t": jnp.swapaxes(w_uq, 1, 2).astype(BF16),
        "m_gq": gain_col(m_qk_norm_q),
        "m_w_dkv": wdkv.astype(BF16),
        "m_norm_kv": m_norm_kv[:, None, :],
        "m_w_ukv_t": jnp.swapaxes(m_w_ukv, 1, 2).astype(BF16),
        "m_gk": gain_col(m_qk_norm_k),
        "m_w_o": m_w_o.astype(BF16),
        "cache_ckv": cache_ckv,
        "cache_krp": jnp.pad(cache_krope, ((0, 0), (0, 0), (0, 0), (QK_NOPE_DIM, HEAD_PAD - QK_HEAD_DIM))),
        "e_w_router_t": jnp.swapaxes(e_w_router, 1, 2).astype(BF16),
        "e_b_router": e_b_router[:, :, None],
        "triu": jnp.tri(ROUTE_TB, ROUTE_TB, -1, dtype=BF16).T,
        "e_w_gu": e_w_gu,
        "e_b_gu": e_b_gu.reshape(DEPTH, N_EXPERTS, D_FF // LANE, LANE, 2).swapaxes(3, 4).reshape(
            DEPTH, N_EXPERTS, 1, 2 * D_FF),
        "e_w_down": e_w_down,
        "e_b_down": e_b_down[:, :, None, :],
        "deinterleave": _deinterleave_matrix(),
    }


def kernel(x_prompt, x_sample, c, cache_ckv, cache_krope, c_ctx, norm_mix_g, norm_ffn_g, w_mod, b_mod,
           g_w_in, g_b_in, g_norm_v, g_w_s, g_b_s, g_w_out, m_w_dq, m_norm_q, m_w_uq, m_w_dkv,
           m_norm_kv, m_w_ukv, m_qk_norm_q, m_qk_norm_k, m_w_o, e_w_router, e_b_router, e_w_gu,
           e_b_gu, e_w_down, e_b_down):
    p = _prepare(c, cache_ckv, cache_krope, c_ctx, norm_mix_g, norm_ffn_g, w_mod, b_mod, g_w_in, g_b_in,
                 g_norm_v, g_w_s, g_b_s, g_w_out, m_w_dq, m_norm_q, m_w_uq, m_w_dkv, m_norm_kv, m_w_ukv,
                 m_qk_norm_q, m_qk_norm_k, m_w_o, e_w_router, e_b_router, e_w_gu, e_b_gu, e_w_down, e_b_down)
    assert MOE_PARTS == 2 and N_PROMPT == N_SAMPLE
    x_p, x_s = x_prompt.reshape(N_PROMPT, D_MODEL), x_sample.reshape(N_SAMPLE, D_MODEL)
    ckv_states, krope_states = [], []
    for layer in range(DEPTH):
        if layer % 2 == 0:
            x = _gmlp_layer(x_p, x_s, p, layer)
        else:
            x, s_ckv, s_krope = _mla_layer(x_p, x_s, p, layer)
            ckv_states.append(s_ckv)
            krope_states.append(s_krope)
        x_p, x_s = _moe_layer(x, p, layer)
    y_prompt = x_p.reshape(BATCH, SEQ, D_MODEL)
    y_sample = x_s.reshape(DEC_BATCH, DEC_SEQ, D_MODEL)
    return (y_prompt, y_sample, jnp.stack(ckv_states, axis=1), jnp.stack(krope_states, axis=1))
```
